```python
import math
import jax, jax.numpy as jnp
from jax import lax
import numpy as np


D_MODEL = 1024
BATCH = 4
SEQ = 8192
DEPTH = 2

SSM_WIDTH = 256
SSM_GROUP = 16
SSM_GROUPS = SSM_WIDTH // SSM_GROUP
SSM_STATE = 64
HGRN_HEADS = 4
HGRN_DK = 64
HGRN_DV = 64
HGRN_WIDTH = HGRN_HEADS * HGRN_DV
HGRN_CHUNK = 64
ATT_HEADS = 8
ATT_HEAD_DIM = 64
ATT_WIDTH = ATT_HEADS * ATT_HEAD_DIM
ATT_Q_RANK = 256
ATT_KV_RANK = 128
IDX_HEADS = 4
IDX_DIM = 64
TOPK_MAX = 256
Q_BLOCK = 128
MASK_VALUE = -1e30
N_BRANCHES = 3
FFN_DENSE = 2816
N_EXPERTS = 8
TOP_K = 2
FFN_EXPERT = 3584
N_DENSE_LAYERS = (DEPTH + 1) // 2
N_MOE_LAYERS = DEPTH // 2
DEEPNORM_ALPHA = (2 * DEPTH) ** 0.25
DEEPNORM_BETA = (8 * DEPTH) ** -0.25
LN_EPS = 1e-5
RMS_EPS = 1e-6
F_MIN = 1e-12

IN_SPLITS = (SSM_WIDTH,
             HGRN_HEADS * HGRN_DK,
             HGRN_HEADS * HGRN_DK,
             HGRN_WIDTH,
             HGRN_WIDTH,
             ATT_Q_RANK,
             ATT_KV_RANK,
             IDX_DIM,
             IDX_HEADS,
             N_BRANCHES * D_MODEL)
N_IN = sum(IN_SPLITS)

kernel_name = 'hybrid_s5_hgrn2_dsa_deepnorm_moe'


def layer_norm(x, g, b):
    xf = x.astype(jnp.float32)
    mu = jnp.mean(xf, axis=-1, keepdims=True)
    var = jnp.mean(jnp.square(xf - mu), axis=-1, keepdims=True)
    return ((xf - mu) * lax.rsqrt(var + LN_EPS) * g + b).astype(x.dtype)


def rms_norm(x, g):
    xf = x.astype(jnp.float32)
    return (xf * lax.rsqrt(jnp.mean(xf * xf, axis=-1, keepdims=True) + RMS_EPS) * g).astype(x.dtype)


def swiglu(x, wg, wu, wd):
    return (jax.nn.silu(x @ wg) * (x @ wu)) @ wd


def _ssm_combine(e1, e2):
    a1r, a1i, b1r, b1i = e1
    a2r, a2i, b2r, b2i = e2
    return (a2r * a1r - a2i * a1i,
            a2r * a1i + a2i * a1r,
            a2r * b1r - a2i * b1i + b2r,
            a2r * b1i + a2i * b1r + b2i)


def s5_branch(u, log_dt, lam_re, lam_im, b_re, b_im, c_re, c_im, d_skip, w_glu, b_glu, w_out):
    f32 = jnp.float32
    bsz, seq, _ = u.shape
    uf = u.astype(f32)
    ug = uf.reshape(bsz, seq, SSM_GROUPS, SSM_GROUP)
    lre, lim = lam_re.astype(f32), lam_im.astype(f32)
    dt = jnp.exp(log_dt.astype(f32))[:, None]
    mag = jnp.exp(lre * dt)
    ang = lim * dt
    a_re, a_im = mag * jnp.cos(ang), mag * jnp.sin(ang)
    den = lre * lre + lim * lim
    coef_re = ((a_re - 1.0) * lre + a_im * lim) / den
    coef_im = (a_im * lre - (a_re - 1.0) * lim) / den
    br, bi = b_re.astype(f32), b_im.astype(f32)
    bb_re = coef_re[..., None] * br - coef_im[..., None] * bi
    bb_im = coef_re[..., None] * bi + coef_im[..., None] * br
    bu_re = jnp.einsum('bsgc,gpc->bsgp', ug, bb_re)
    bu_im = jnp.einsum('bsgc,gpc->bsgp', ug, bb_im)
    ar = jnp.broadcast_to(a_re, bu_re.shape)
    ai = jnp.broadcast_to(a_im, bu_re.shape)
    _, _, h_re, h_im = lax.associative_scan(_ssm_combine, (ar, ai, bu_re, bu_im), axis=1)
    y = (jnp.einsum('bsgp,gcp->bsgc', h_re, c_re.astype(f32))
         - jnp.einsum('bsgp,gcp->bsgc', h_im, c_im.astype(f32)))
    y = y.reshape(bsz, seq, SSM_WIDTH) + d_skip.astype(f32) * uf
    y = jax.nn.gelu(y)
    y = y * jax.nn.sigmoid(y @ w_glu.astype(f32) + b_glu.astype(f32))
    return y.astype(u.dtype) @ w_out


def hgrn2_branch(q, f_logit, i_in, g_out, lower_bound, norm_g, w_out):
    f32 = jnp.float32
    bsz, seq, _ = q.shape
    nc = seq // HGRN_CHUNK

    def heads(t, dim):
        return jnp.moveaxis(t.astype(f32).reshape(bsz, nc, HGRN_CHUNK, HGRN_HEADS, dim), 1, 0)

    z = f_logit.astype(f32)
    lb = lower_bound.astype(f32)
    f = lb + (1.0 - lb) * jax.nn.sigmoid(z)
    log_f = jnp.log(jnp.maximum(f, F_MIN))
    k_in = (1.0 - lb) * jax.nn.sigmoid(-z)
    qh, kh, vh = heads(q, HGRN_DK), heads(k_in, HGRN_DK), heads(i_in, HGRN_DV)
    bcum = jnp.cumsum(heads(log_f, HGRN_DK), axis=2)
    causal = jnp.tril(jnp.ones((HGRN_CHUNK, HGRN_CHUNK), dtype=bool))[None, :, :, None, None]

    def step(state, inp):
        qc, kc, vc, bc = inp
        o_inter = jnp.einsum('bthk,bhkv->bthv', qc * jnp.exp(bc), state)
        diff = bc[:, :, None] - bc[:, None, :]
        decay = jnp.where(causal, jnp.exp(jnp.where(causal, diff, 0.0)), 0.0)
        att = jnp.einsum('bthk,bshk,btshk->bhts', qc, kc, decay)
        o_intra = jnp.einsum('bhts,bshv->bthv', att, vc)
        b_last = bc[:, -1]
        state = (jnp.exp(b_last)[..., None] * state
                 + jnp.einsum('bshk,bshv->bhkv', kc * jnp.exp(b_last[:, None] - bc), vc))
        return state, o_inter + o_intra

    s0 = jnp.zeros((bsz, HGRN_HEADS, HGRN_DK, HGRN_DV), f32)
    _, o = lax.scan(step, s0, (qh, kh, vh, bcum))
    o = jnp.moveaxis(o, 0, 1).reshape(bsz, seq, HGRN_HEADS, HGRN_DV)
    o = rms_norm(o, norm_g).reshape(bsz, seq, HGRN_WIDTH) * jax.nn.silu(g_out.astype(f32))
    return o.astype(q.dtype) @ w_out


def dsa_branch(c_q, c_kv, k_idx, w_idx, q_norm_g, kv_norm_g, w_uq, w_qidx, w_ukv, w_out):
    f32 = jnp.float32
    bsz, seq, _ = c_q.shape
    n_sel = min(TOPK_MAX, seq // 4)
    cq = rms_norm(c_q, q_norm_g)
    ckv = rms_norm(c_kv, kv_norm_g)
    q = (cq @ w_uq).reshape(bsz, seq, ATT_HEADS, ATT_HEAD_DIM)
    q_idx = (cq @ w_qidx).reshape(bsz, seq, IDX_HEADS, IDX_DIM)
    kv = (ckv @ w_ukv).reshape(bsz, seq, 2, ATT_HEADS, ATT_HEAD_DIM)
    k, v = kv[:, :, 0], kv[:, :, 1]
    kid = k_idx.astype(f32)
    w = w_idx.astype(f32) * IDX_HEADS ** -0.5
    slopes = 2.0 ** (-8.0 * jnp.arange(1, ATT_HEADS + 1, dtype=f32) / ATT_HEADS)
    key_pos = jnp.arange(seq)
    nb = seq // Q_BLOCK

    def blocks(t):
        return jnp.moveaxis(t.reshape(bsz, nb, Q_BLOCK, *t.shape[2:]), 1, 0)

    def attend(inp):
        qb, qib, wb, start = inp
        q_pos = start + jnp.arange(Q_BLOCK)
        s_idx = jnp.einsum('bthd,bsd->bths', qib.astype(f32), kid) * IDX_DIM ** -0.5
        score = jnp.einsum('bths,bth->bts', jax.nn.relu(s_idx), wb)
        score = jnp.where(key_pos[None, None, :] <= q_pos[None, :, None], score, MASK_VALUE)
        _, sel = lax.top_k(score, n_sel)
        valid = sel <= q_pos[None, :, None]
        k_sel = jax.vmap(lambda kk, ii: kk[ii])(k, sel)
        v_sel = jax.vmap(lambda vv, ii: vv[ii])(v, sel)
        logits = jnp.einsum('bthd,btkhd->bthk', qb.astype(f32), k_sel.astype(f32)) * ATT_HEAD_DIM ** -0.5
        dist = (q_pos[None, :, None] - sel).astype(f32)
        logits = logits - slopes[None, None, :, None] * dist[:, :, None, :]
        logits = jnp.where(valid[:, :, None, :], logits, MASK_VALUE)
        p = jax.nn.softmax(logits, axis=-1)
        return jnp.einsum('bthk,btkhd->bthd', p, v_sel.astype(f32))

    starts = jnp.arange(nb) * Q_BLOCK
    o = lax.map(attend, (blocks(q), blocks(q_idx), blocks(w), starts))
    o = jnp.moveaxis(o, 0, 1).reshape(bsz, seq, ATT_WIDTH)
    return o.astype(c_q.dtype) @ w_out


def moe_ffn(x, router, wg, wu, wd):
    bsz, seq, d = x.shape
    xt = x.reshape(bsz * seq, d)
    logits = (xt @ router).astype(jnp.float32)
    top_val, top_idx = lax.top_k(logits, TOP_K)
    top_w = jax.nn.softmax(top_val, axis=-1)
    out = jnp.zeros((bsz * seq, d), jnp.float32)
    for e in range(N_EXPERTS):
        gate_e = jnp.sum(jnp.where(top_idx == e, top_w, 0.0), axis=-1)
        out = out + gate_e[:, None] * swiglu(xt, wg[e], wu[e], wd[e])
    return out.astype(x.dtype).reshape(bsz, seq, d)


def setup_inputs(seed: int = 0) -> dict:
    key = jax.random.key(seed)
    ks = jax.random.split(key, 40)
    f32 = jnp.float32

    def nrm(i, shape, scale):
        return jax.random.normal(ks[i], shape, f32) * scale

    G, P = SSM_GROUPS, SSM_STATE
    lam_im0 = jnp.broadcast_to(jnp.pi * jnp.arange(P, dtype=f32), (DEPTH, G, P))
    return {
        'x': nrm(0, (BATCH, SEQ, D_MODEL), 1.0),
        'w_in': nrm(1, (DEPTH, D_MODEL, N_IN), D_MODEL ** -0.5),
        'ssm_log_dt': jax.random.uniform(ks[2], (DEPTH, G), f32, math.log(1e-3), math.log(1e-1)),
        'ssm_lambda_re': -0.5 + nrm(3, (DEPTH, G, P), 0.01),
        'ssm_lambda_im': lam_im0 + nrm(4, (DEPTH, G, P), 0.01),
        'ssm_b_re': nrm(5, (DEPTH, G, P, SSM_GROUP), (2 * SSM_GROUP) ** -0.5),
        'ssm_b_im': nrm(6, (DEPTH, G, P, SSM_GROUP), (2 * SSM_GROUP) ** -0.5),
        'ssm_c_re': nrm(7, (DEPTH, G, SSM_GROUP, P), (2 * P) ** -0.5),
        'ssm_c_im': nrm(8, (DEPTH, G, SSM_GROUP, P), (2 * P) ** -0.5),
        'ssm_d': nrm(9, (DEPTH, SSM_WIDTH), 1.0),
        'ssm_w_glu': nrm(10, (DEPTH, SSM_WIDTH, SSM_WIDTH), SSM_WIDTH ** -0.5),
        'ssm_b_glu': nrm(11, (DEPTH, SSM_WIDTH), 0.01),
        'ssm_w_out': nrm(12, (DEPTH, SSM_WIDTH, D_MODEL), SSM_WIDTH ** -0.5 * DEEPNORM_BETA),
        'hgrn_lb_logits': nrm(13, (DEPTH, HGRN_HEADS * HGRN_DK), 1.0),
        'hgrn_norm_g': 1.0 + nrm(14, (DEPTH, HGRN_DV), 0.01),
        'hgrn_w_out': nrm(15, (DEPTH, HGRN_WIDTH, D_MODEL), HGRN_WIDTH ** -0.5 * DEEPNORM_BETA),
        'attn_q_norm_g': 1.0 + nrm(16, (DEPTH, ATT_Q_RANK), 0.01),
        'attn_kv_norm_g': 1.0 + nrm(17, (DEPTH, ATT_KV_RANK), 0.01),
        'attn_w_uq': nrm(18, (DEPTH, ATT_Q_RANK, ATT_WIDTH), ATT_Q_RANK ** -0.5),
        'attn_w_qidx': nrm(19, (DEPTH, ATT_Q_RANK, IDX_HEADS * IDX_DIM), ATT_Q_RANK ** -0.5),
        'attn_w_ukv': nrm(20, (DEPTH, ATT_KV_RANK, 2 * ATT_WIDTH), ATT_KV_RANK ** -0.5),
        'attn_w_out': nrm(21, (DEPTH, ATT_WIDTH, D_MODEL), ATT_WIDTH ** -0.5 * DEEPNORM_BETA),
        'w_o': nrm(22, (DEPTH, D_MODEL, D_MODEL), D_MODEL ** -0.5 * DEEPNORM_BETA),
        'ln_g': 1.0 + nrm(23, (DEPTH, 2, D_MODEL), 0.01),
        'ln_b': nrm(24, (DEPTH, 2, D_MODEL), 0.01),
        'ffn_w_gate': nrm(25, (N_DENSE_LAYERS, D_MODEL, FFN_DENSE), D_MODEL ** -0.5),
        'ffn_w_up': nrm(26, (N_DENSE_LAYERS, D_MODEL, FFN_DENSE), D_MODEL ** -0.5),
        'ffn_w_down': nrm(27, (N_DENSE_LAYERS, FFN_DENSE, D_MODEL), FFN_DENSE ** -0.5 * DEEPNORM_BETA),
        'moe_router': nrm(28, (N_MOE_LAYERS, D_MODEL, N_EXPERTS), D_MODEL ** -0.5),
        'moe_w_gate': nrm(29, (N_MOE_LAYERS, N_EXPERTS, D_MODEL, FFN_EXPERT), D_MODEL ** -0.5),
        'moe_w_up': nrm(30, (N_MOE_LAYERS, N_EXPERTS, D_MODEL, FFN_EXPERT), D_MODEL ** -0.5),
        'moe_w_down': nrm(31, (N_MOE_LAYERS, N_EXPERTS, FFN_EXPERT, D_MODEL), FFN_EXPERT ** -0.5 * DEEPNORM_BETA),
    }


def reference(x, w_in, ssm_log_dt, ssm_lambda_re, ssm_lambda_im, ssm_b_re, ssm_b_im, ssm_c_re, ssm_c_im,
              ssm_d, ssm_w_glu, ssm_b_glu, ssm_w_out, hgrn_lb_logits, hgrn_norm_g, hgrn_w_out,
              attn_q_norm_g, attn_kv_norm_g, attn_w_uq, attn_w_qidx, attn_w_ukv, attn_w_out, w_o,
              ln_g, ln_b, ffn_w_gate, ffn_w_up, ffn_w_down, moe_router, moe_w_gate, moe_w_up, moe_w_down):
    bsz, seq, _ = x.shape
    split_at = np.cumsum(IN_SPLITS)[:-1].tolist()
    lb_soft = jax.nn.softmax(hgrn_lb_logits.astype(jnp.float32), axis=0)
    lower_bounds = jnp.concatenate([jnp.zeros_like(lb_soft[:1]), jnp.cumsum(lb_soft[1:], axis=0)], axis=0)
    h = x
    for l in range(DEPTH):
        proj = h @ w_in[l]
        u, hq, hf, hi, hg, cq, ckv, kidx, widx, gates = jnp.split(proj, split_at, axis=-1)
        y_ssm = s5_branch(u, ssm_log_dt[l], ssm_lambda_re[l], ssm_lambda_im[l], ssm_b_re[l], ssm_b_im[l],
                          ssm_c_re[l], ssm_c_im[l], ssm_d[l], ssm_w_glu[l], ssm_b_glu[l], ssm_w_out[l])
        y_hg = hgrn2_branch(hq, hf, hi, hg, lower_bounds[l], hgrn_norm_g[l], hgrn_w_out[l])
        y_att = dsa_branch(cq, ckv, kidx, widx, attn_q_norm_g[l], attn_kv_norm_g[l],
                           attn_w_uq[l], attn_w_qidx[l], attn_w_ukv[l], attn_w_out[l])
        g = jax.nn.sigmoid(gates.astype(jnp.float32)).reshape(bsz, seq, N_BRANCHES, D_MODEL)
        mixed = g[:, :, 0] * y_ssm + g[:, :, 1] * y_hg + g[:, :, 2] * y_att
        mix_out = mixed.astype(h.dtype) @ w_o[l]
        h = layer_norm(DEEPNORM_ALPHA * h + mix_out, ln_g[l, 0], ln_b[l, 0])
        if l % 2 == 0:
            ffn = swiglu(h, ffn_w_gate[l // 2], ffn_w_up[l // 2], ffn_w_down[l // 2])
        else:
            ffn = moe_ffn(h, moe_router[l // 2], moe_w_gate[l // 2], moe_w_up[l // 2], moe_w_down[l // 2])
        h = layer_norm(DEEPNORM_ALPHA * h + ffn, ln_g[l, 1], ln_b[l, 1])
    return h
```

```python
import functools
import math

import jax
import jax.numpy as jnp
import numpy as np
from jax import lax
from jax.experimental import pallas as pl
from jax.experimental.pallas import tpu as pltpu

D_MODEL = 1024
DEPTH = 2
SSM_WIDTH = 256
SSM_GROUP = 16
SSM_GROUPS = SSM_WIDTH // SSM_GROUP
SSM_STATE = 64
HGRN_HEADS = 4
HGRN_DK = 64
HGRN_DV = 64
HGRN_WIDTH = HGRN_HEADS * HGRN_DV
HGRN_CHUNK = 64
ATT_HEADS = 8
ATT_HEAD_DIM = 64
ATT_WIDTH = ATT_HEADS * ATT_HEAD_DIM
ATT_Q_RANK = 256
ATT_KV_RANK = 128
IDX_HEADS = 4
IDX_DIM = 64
TOPK_MAX = 256
Q_BLOCK = 128
MASK_VALUE = -1e30
N_BRANCHES = 3
N_EXPERTS = 8
TOP_K = 2
DEEPNORM_ALPHA = (2 * DEPTH) ** 0.25
LN_EPS = 1e-5
RMS_EPS = 1e-6
F_MIN = 1e-12

IN_SPLITS = (SSM_WIDTH, HGRN_HEADS * HGRN_DK, HGRN_HEADS * HGRN_DK, HGRN_WIDTH, HGRN_WIDTH,
             ATT_Q_RANK, ATT_KV_RANK, IDX_DIM, IDX_HEADS, N_BRANCHES * D_MODEL)
N_IN = sum(IN_SPLITS)

VMEM_LIMIT_BYTES = 48 * 1024 * 1024
LANE = 128


def _round_up(n, m):
    return (n + m - 1) // m * m


def _matmul_body(x_ref, w_ref, o_ref):
    o_ref[...] = jnp.dot(x_ref[...].astype(jnp.bfloat16), w_ref[...],
                         preferred_element_type=jnp.float32).astype(o_ref.dtype)


def matmul(x, w, *, tm=512, tn=512, out_dtype=jnp.float32):
    m, k = x.shape
    _, n = w.shape
    tn = min(tn, n)
    assert m % tm == 0 and n % tn == 0
    return pl.pallas_call(
        _matmul_body,
        grid=(m // tm, n // tn),
        in_specs=[pl.BlockSpec((tm, k), lambda i, j: (i, 0)),
                  pl.BlockSpec((k, tn), lambda i, j: (0, j))],
        out_specs=pl.BlockSpec((tm, tn), lambda i, j: (i, j)),
        out_shape=jax.ShapeDtypeStruct((m, n), out_dtype),
        compiler_params=pltpu.CompilerParams(
            dimension_semantics=("parallel", "parallel"), vmem_limit_bytes=VMEM_LIMIT_BYTES),
        name="matmul",
    )(x, w)


def _layer_norm_rows(y, g, b):
    mu = jnp.mean(y, axis=-1, keepdims=True)
    yc = y - mu
    var = jnp.mean(yc * yc, axis=-1, keepdims=True)
    return yc * lax.rsqrt(var + LN_EPS) * g + b


def _merge_body(h_ref, ys_ref, yh_ref, ya_ref, gt_ref, wo_ref, g_ref, b_ref, o_ref):
    d = D_MODEL
    gates = jax.nn.sigmoid(gt_ref[...])
    mixed = (gates[:, 0:d] * ys_ref[...] + gates[:, d:2 * d] * yh_ref[...]
             + gates[:, 2 * d:3 * d] * ya_ref[...])
    mix_out = jnp.dot(mixed.astype(jnp.bfloat16), wo_ref[...], preferred_element_type=jnp.float32)
    o_ref[...] = _layer_norm_rows(DEEPNORM_ALPHA * h_ref[...] + mix_out, g_ref[...], b_ref[...])


def merge_project_norm(h, y_ssm, y_hg, y_att, gate_logits, w_o, ln_g, ln_b, *, tm=512):
    m, d = h.shape
    row = lambda i: (i, 0)
    full = lambda i: (0, 0)
    return pl.pallas_call(
        _merge_body,
        grid=(m // tm,),
        in_specs=[pl.BlockSpec((tm, d), row), pl.BlockSpec((tm, d), row), pl.BlockSpec((tm, d), row),
                  pl.BlockSpec((tm, d), row), pl.BlockSpec((tm, N_BRANCHES * d), row),
                  pl.BlockSpec((d, d), full), pl.BlockSpec((1, d), full), pl.BlockSpec((1, d), full)],
        out_specs=pl.BlockSpec((tm, d), row),
        out_shape=jax.ShapeDtypeStruct((m, d), jnp.float32),
        compiler_params=pltpu.CompilerParams(
            dimension_semantics=("parallel",), vmem_limit_bytes=VMEM_LIMIT_BYTES),
        name="merge_project_norm",
    )(h, y_ssm, y_hg, y_att, gate_logits, w_o, ln_g.reshape(1, d), ln_b.reshape(1, d))


def _ffn_body(h_ref, gate_ref, wg_ref, wu_ref, wd_ref, g_ref, b_ref, o_ref, acc_ref, *, n_experts):
    e = pl.program_id(1)
    f = pl.program_id(2)

    @pl.when((e == 0) & (f == 0))
    def _():
        acc_ref[...] = jnp.zeros_like(acc_ref)

    x = h_ref[...].astype(jnp.bfloat16)
    a = jnp.dot(x, wg_ref[0], preferred_element_type=jnp.float32)
    u = jnp.dot(x, wu_ref[0], preferred_element_type=jnp.float32)
    act = a * jax.nn.sigmoid(a) * u
    part = jnp.dot(act.astype(jnp.bfloat16), wd_ref[0], preferred_element_type=jnp.float32)
    if n_experts > 1:
        lane = lax.broadcasted_iota(jnp.int32, gate_ref.shape, 1)
        gate_e = jnp.sum(jnp.where(lane == e, gate_ref[...], 0.0), axis=-1, keepdims=True)
        part = gate_e * part
    acc_ref[...] += part

    @pl.when((e == pl.num_programs(1) - 1) & (f == pl.num_programs(2) - 1))
    def _():
        o_ref[...] = _layer_norm_rows(DEEPNORM_ALPHA * h_ref[...] + acc_ref[...], g_ref[...], b_ref[...])


def swiglu_experts_norm(h, gates, wg, wu, wd, ln_g, ln_b, *, tm=1024, tf=512):
    m, d = h.shape
    n_experts, _, f_dim = wg.shape
    assert f_dim % tf == 0 and m % tm == 0
    row = lambda i, e, f: (i, 0)
    full = lambda i, e, f: (0, 0)
    return pl.pallas_call(
        functools.partial(_ffn_body, n_experts=n_experts),
        grid=(m // tm, n_experts, f_dim // tf),
        in_specs=[pl.BlockSpec((tm, d), row), pl.BlockSpec((tm, LANE), row),
                  pl.BlockSpec((1, d, tf), lambda i, e, f: (e, 0, f)),
                  pl.BlockSpec((1, d, tf), lambda i, e, f: (e, 0, f)),
                  pl.BlockSpec((1, tf, d), lambda i, e, f: (e, f, 0)),
                  pl.BlockSpec((1, d), full), pl.BlockSpec((1, d), full)],
        out_specs=pl.BlockSpec((tm, d), row),
        out_shape=jax.ShapeDtypeStruct((m, d), jnp.float32),
        scratch_shapes=[pltpu.VMEM((tm, d), jnp.float32)],
        compiler_params=pltpu.CompilerParams(
            dimension_semantics=("parallel", "arbitrary", "arbitrary"),
            vmem_limit_bytes=VMEM_LIMIT_BYTES),
        name="swiglu_experts_norm",
    )(h, gates, wg, wu, wd, ln_g.reshape(1, d), ln_b.reshape(1, d))


def rms_norm(x, g):
    xf = x.astype(jnp.float32)
    return (xf * lax.rsqrt(jnp.mean(xf * xf, axis=-1, keepdims=True) + RMS_EPS) * g).astype(x.dtype)


def _ssm_combine(e1, e2):
    a1r, a1i, b1r, b1i = e1
    a2r, a2i, b2r, b2i = e2
    return (a2r * a1r - a2i * a1i, a2r * a1i + a2i * a1r,
            a2r * b1r - a2i * b1i + b2r, a2r * b1i + a2i * b1r + b2i)


def s5_branch(u, log_dt, lam_re, lam_im, b_re, b_im, c_re, c_im, d_skip, w_glu, b_glu, w_out):
    f32 = jnp.float32
    bsz, seq, _ = u.shape
    uf = u.astype(f32)
    ug = uf.reshape(bsz, seq, SSM_GROUPS, SSM_GROUP)
    lre, lim = lam_re.astype(f32), lam_im.astype(f32)
    dt = jnp.exp(log_dt.astype(f32))[:, None]
    mag = jnp.exp(lre * dt)
    ang = lim * dt
    a_re, a_im = mag * jnp.cos(ang), mag * jnp.sin(ang)
    den = lre * lre + lim * lim
    coef_re = ((a_re - 1.0) * lre + a_im * lim) / den
    coef_im = (a_im * lre - (a_re - 1.0) * lim) / den
    br, bi = b_re.astype(f32), b_im.astype(f32)
    bb_re = coef_re[..., None] * br - coef_im[..., None] * bi
    bb_im = coef_re[..., None] * bi + coef_im[..., None] * br
    bu_re = jnp.einsum('bsgc,gpc->bsgp', ug, bb_re)
    bu_im = jnp.einsum('bsgc,gpc->bsgp', ug, bb_im)
    ar = jnp.broadcast_to(a_re, bu_re.shape)
    ai = jnp.broadcast_to(a_im, bu_re.shape)
    _, _, h_re, h_im = lax.associative_scan(_ssm_combine, (ar, ai, bu_re, bu_im), axis=1)
    y = (jnp.einsum('bsgp,gcp->bsgc', h_re, c_re.astype(f32))
         - jnp.einsum('bsgp,gcp->bsgc', h_im, c_im.astype(f32)))
    y = y.reshape(bsz, seq, SSM_WIDTH) + d_skip.astype(f32) * uf
    y = jax.nn.gelu(y)
    y = y * jax.nn.sigmoid(y @ w_glu.astype(f32) + b_glu.astype(f32))
    return y.astype(u.dtype) @ w_out


def hgrn2_branch(q, f_logit, i_in, g_out, lower_bound, norm_g, w_out):
    f32 = jnp.float32
    bsz, seq, _ = q.shape
    nc = seq // HGRN_CHUNK

    def heads(t, dim):
        return jnp.moveaxis(t.astype(f32).reshape(bsz, nc, HGRN_CHUNK, HGRN_HEADS, dim), 1, 0)

    z = f_logit.astype(f32)
    lb = lower_bound.astype(f32)
    f = lb + (1.0 - lb) * jax.nn.sigmoid(z)
    log_f = jnp.log(jnp.maximum(f, F_MIN))
    k_in = (1.0 - lb) * jax.nn.sigmoid(-z)
    qh, kh, vh = heads(q, HGRN_DK), heads(k_in, HGRN_DK), heads(i_in, HGRN_DV)
    bcum = jnp.cumsum(heads(log_f, HGRN_DK), axis=2)
    causal = jnp.tril(jnp.ones((HGRN_CHUNK, HGRN_CHUNK), dtype=bool))[None, :, :, None, None]

    def step(state, inp):
        qc, kc, vc, bc = inp
        o_inter = jnp.einsum('bthk,bhkv->bthv', qc * jnp.exp(bc), state)
        diff = bc[:, :, None] - bc[:, None, :]
        decay = jnp.where(causal, jnp.exp(jnp.where(causal, diff, 0.0)), 0.0)
        att = jnp.einsum('bthk,bshk,btshk->bhts', qc, kc, decay)
        o_intra = jnp.einsum('bhts,bshv->bthv', att, vc)
        b_last = bc[:, -1]
        state = (jnp.exp(b_last)[..., None] * state
                 + jnp.einsum('bshk,bshv->bhkv', kc * jnp.exp(b_last[:, None] - bc), vc))
        return state, o_inter + o_intra

    s0 = jnp.zeros((bsz, HGRN_HEADS, HGRN_DK, HGRN_DV), f32)
    _, o = lax.scan(step, s0, (qh, kh, vh, bcum))
    o = jnp.moveaxis(o, 0, 1).reshape(bsz, seq, HGRN_HEADS, HGRN_DV)
    o = rms_norm(o, norm_g).reshape(bsz, seq, HGRN_WIDTH) * jax.nn.silu(g_out.astype(f32))
    return o.astype(q.dtype) @ w_out


def dsa_branch(c_q, c_kv, k_idx, w_idx, q_norm_g, kv_norm_g, w_uq, w_qidx, w_ukv, w_out):
    f32 = jnp.float32
    bsz, seq, _ = c_q.shape
    n_sel = min(TOPK_MAX, seq // 4)
    cq = rms_norm(c_q, q_norm_g)
    ckv = rms_norm(c_kv, kv_norm_g)
    q = (cq @ w_uq).reshape(bsz, seq, ATT_HEADS, ATT_HEAD_DIM)
    q_idx = (cq @ w_qidx).reshape(bsz, seq, IDX_HEADS, IDX_DIM)
    kv = (ckv @ w_ukv).reshape(bsz, seq, 2, ATT_HEADS, ATT_HEAD_DIM)
    k, v = kv[:, :, 0], kv[:, :, 1]
    kid = k_idx.astype(f32)
    w = w_idx.astype(f32) * IDX_HEADS ** -0.5
    slopes = 2.0 ** (-8.0 * jnp.arange(1, ATT_HEADS + 1, dtype=f32) / ATT_HEADS)
    key_pos = jnp.arange(seq)
    nb = seq // Q_BLOCK

    def blocks(t):
        return jnp.moveaxis(t.reshape(bsz, nb, Q_BLOCK, *t.shape[2:]), 1, 0)

    def attend(inp):
        qb, qib, wb, start = inp
        q_pos = start + jnp.arange(Q_BLOCK)
        s_idx = jnp.einsum('bthd,bsd->bths', qib.astype(f32), kid) * IDX_DIM ** -0.5
        score = jnp.einsum('bths,bth->bts', jax.nn.relu(s_idx), wb)
        score = jnp.where(key_pos[None, None, :] <= q_pos[None, :, None], score, MASK_VALUE)
        _, sel = lax.top_k(score, n_sel)
        valid = sel <= q_pos[None, :, None]
        k_sel = jax.vmap(lambda kk, ii: kk[ii])(k, sel)
        v_sel = jax.vmap(lambda vv, ii: vv[ii])(v, sel)
        logits = jnp.einsum('bthd,btkhd->bthk', qb.astype(f32), k_sel.astype(f32)) * ATT_HEAD_DIM ** -0.5
        dist = (q_pos[None, :, None] - sel).astype(f32)
        logits = logits - slopes[None, None, :, None] * dist[:, :, None, :]
        logits = jnp.where(valid[:, :, None, :], logits, MASK_VALUE)
        p = jax.nn.softmax(logits, axis=-1)
        return jnp.einsum('bthk,btkhd->bthd', p, v_sel.astype(f32))

    starts = jnp.arange(nb) * Q_BLOCK
    o = lax.map(attend, (blocks(q), blocks(q_idx), blocks(w), starts))
    o = jnp.moveaxis(o, 0, 1).reshape(bsz, seq, ATT_WIDTH)
    return o.astype(c_q.dtype) @ w_out


def router_gates(h, router):
    logits = (h @ router).astype(jnp.float32)
    top_val, top_idx = lax.top_k(logits, TOP_K)
    top_w = jax.nn.softmax(top_val, axis=-1)
    onehot = (top_idx[..., None] == jnp.arange(N_EXPERTS)[None, None, :])
    gates = jnp.sum(jnp.where(onehot, top_w[..., None], 0.0), axis=1)
    return jnp.pad(gates, ((0, 0), (0, LANE - N_EXPERTS)))


def kernel(x, w_in, ssm_log_dt, ssm_lambda_re, ssm_lambda_im, ssm_b_re, ssm_b_im, ssm_c_re, ssm_c_im,
           ssm_d, ssm_w_glu, ssm_b_glu, ssm_w_out, hgrn_lb_logits, hgrn_norm_g, hgrn_w_out,
           attn_q_norm_g, attn_kv_norm_g, attn_w_uq, attn_w_qidx, attn_w_ukv, attn_w_out, w_o,
           ln_g, ln_b, ffn_w_gate, ffn_w_up, ffn_w_down, moe_router, moe_w_gate, moe_w_up, moe_w_down):
    bsz, seq, d = x.shape
    m = bsz * seq
    bf16 = jnp.bfloat16
    split_at = np.cumsum(IN_SPLITS)[:-1].tolist()
    lb_soft = jax.nn.softmax(hgrn_lb_logits.astype(jnp.float32), axis=0)
    lower_bounds = jnp.concatenate([jnp.zeros_like(lb_soft[:1]), jnp.cumsum(lb_soft[1:], axis=0)], axis=0)
    n_in_pad = _round_up(N_IN, 512)
    h = x.reshape(m, d)
    for l in range(DEPTH):
        w_in_l = jnp.pad(w_in[l], ((0, 0), (0, n_in_pad - N_IN))).astype(bf16)
        proj = matmul(h, w_in_l)[:, :N_IN].reshape(bsz, seq, N_IN)
        u, hq, hf, hi, hg, cq, ckv, kidx, widx, gates = jnp.split(proj, split_at, axis=-1)
        y_ssm = s5_branch(u, ssm_log_dt[l], ssm_lambda_re[l], ssm_lambda_im[l], ssm_b_re[l], ssm_b_im[l],
                          ssm_c_re[l], ssm_c_im[l], ssm_d[l], ssm_w_glu[l], ssm_b_glu[l], ssm_w_out[l])
        y_hg = hgrn2_branch(hq, hf, hi, hg, lower_bounds[l], hgrn_norm_g[l], hgrn_w_out[l])
        y_att = dsa_branch(cq, ckv, kidx, widx, attn_q_norm_g[l], attn_kv_norm_g[l],
                           attn_w_uq[l], attn_w_qidx[l], attn_w_ukv[l], attn_w_out[l])
        h = merge_project_norm(h, y_ssm.reshape(m, d), y_hg.reshape(m, d), y_att.reshape(m, d),
                               gates.reshape(m, N_BRANCHES * d), w_o[l].astype(bf16),
                               ln_g[l, 0], ln_b[l, 0])
        if l % 2 == 0:
            ones = jnp.ones((m, LANE), jnp.float32)
            h = swiglu_experts_norm(h, ones, ffn_w_gate[l // 2][None].astype(bf16),
                                    ffn_w_up[l // 2][None].astype(bf16),
                                    ffn_w_down[l // 2][None].astype(bf16),
                                    ln_g[l, 1], ln_b[l, 1], tf=256)
        else:
            gate_w = router_gates(h, moe_router[l // 2])
            h = swiglu_experts_norm(h, gate_w, moe_w_gate[l // 2].astype(bf16),
                                    moe_w_up[l // 2].astype(bf16), moe_w_down[l // 2].astype(bf16),
                                    ln_g[l, 1], ln_b[l, 1], tf=512)
    return h.reshape(bsz, seq, d)
```

```python
import functools
import math

import jax
import jax.numpy as jnp
import numpy as np
from jax import lax
from jax.experimental import pallas as pl
from jax.experimental.pallas import tpu as pltpu

D_MODEL = 1024
DEPTH = 2
SSM_WIDTH = 256
SSM_GROUP = 16
SSM_GROUPS = SSM_WIDTH // SSM_GROUP
SSM_STATE = 64
HGRN_HEADS = 4
HGRN_DK = 64
HGRN_DV = 64
HGRN_WIDTH = HGRN_HEADS * HGRN_DV
HGRN_CHUNK = 64
ATT_HEADS = 8
ATT_HEAD_DIM = 64
ATT_WIDTH = ATT_HEADS * ATT_HEAD_DIM
ATT_Q_RANK = 256
ATT_KV_RANK = 128
IDX_HEADS = 4
IDX_DIM = 64
TOPK_MAX = 256
Q_BLOCK = 128
MASK_VALUE = -1e30
N_BRANCHES = 3
N_EXPERTS = 8
TOP_K = 2
DEEPNORM_ALPHA = (2 * DEPTH) ** 0.25
LN_EPS = 1e-5
RMS_EPS = 1e-6
F_MIN = 1e-12

IN_SPLITS = (SSM_WIDTH, HGRN_HEADS * HGRN_DK, HGRN_HEADS * HGRN_DK, HGRN_WIDTH, HGRN_WIDTH,
             ATT_Q_RANK, ATT_KV_RANK, IDX_DIM, IDX_HEADS, N_BRANCHES * D_MODEL)
N_IN = sum(IN_SPLITS)

VMEM_LIMIT_BYTES = 48 * 1024 * 1024
LANE = 128
MXU_DTYPE = jnp.bfloat16


def _round_up(n, m):
    return (n + m - 1) // m * m


def _matmul_body(x_ref, w_ref, o_ref):
    o_ref[...] = jnp.dot(x_ref[...].astype(MXU_DTYPE), w_ref[...],
                         preferred_element_type=jnp.float32).astype(o_ref.dtype)


def matmul(x, w, *, tm=512, tn=512, out_dtype=jnp.float32):
    m, k = x.shape
    _, n = w.shape
    tn = min(tn, n)
    assert m % tm == 0 and n % tn == 0
    return pl.pallas_call(
        _matmul_body,
        grid=(m // tm, n // tn),
        in_specs=[pl.BlockSpec((tm, k), lambda i, j: (i, 0)),
                  pl.BlockSpec((k, tn), lambda i, j: (0, j))],
        out_specs=pl.BlockSpec((tm, tn), lambda i, j: (i, j)),
        out_shape=jax.ShapeDtypeStruct((m, n), out_dtype),
        compiler_params=pltpu.CompilerParams(
            dimension_semantics=("parallel", "parallel"), vmem_limit_bytes=VMEM_LIMIT_BYTES),
        name="matmul",
    )(x, w)


def _layer_norm_rows(y, g, b):
    mu = jnp.mean(y, axis=-1, keepdims=True)
    yc = y - mu
    var = jnp.mean(yc * yc, axis=-1, keepdims=True)
    return yc * lax.rsqrt(var + LN_EPS) * g + b


def _merge_body(h_ref, ys_ref, yh_ref, ya_ref, gt_ref, wo_ref, g_ref, b_ref, o_ref):
    d = D_MODEL
    gates = jax.nn.sigmoid(gt_ref[...])
    mixed = (gates[:, 0:d] * ys_ref[...] + gates[:, d:2 * d] * yh_ref[...]
             + gates[:, 2 * d:3 * d] * ya_ref[...])
    mix_out = jnp.dot(mixed.astype(MXU_DTYPE), wo_ref[...], preferred_element_type=jnp.float32)
    o_ref[...] = _layer_norm_rows(DEEPNORM_ALPHA * h_ref[...] + mix_out, g_ref[...], b_ref[...])


def merge_project_norm(h, y_ssm, y_hg, y_att, gate_logits, w_o, ln_g, ln_b, *, tm=512):
    m, d = h.shape
    row = lambda i: (i, 0)
    full = lambda i: (0, 0)
    return pl.pallas_call(
        _merge_body,
        grid=(m // tm,),
        in_specs=[pl.BlockSpec((tm, d), row), pl.BlockSpec((tm, d), row), pl.BlockSpec((tm, d), row),
                  pl.BlockSpec((tm, d), row), pl.BlockSpec((tm, N_BRANCHES * d), row),
                  pl.BlockSpec((d, d), full), pl.BlockSpec((1, d), full), pl.BlockSpec((1, d), full)],
        out_specs=pl.BlockSpec((tm, d), row),
        out_shape=jax.ShapeDtypeStruct((m, d), jnp.float32),
        compiler_params=pltpu.CompilerParams(
            dimension_semantics=("parallel",), vmem_limit_bytes=VMEM_LIMIT_BYTES),
        name="merge_project_norm",
    )(h, y_ssm, y_hg, y_att, gate_logits, w_o, ln_g.reshape(1, d), ln_b.reshape(1, d))


def _ffn_body(h_ref, gate_ref, wg_ref, wu_ref, wd_ref, g_ref, b_ref, o_ref, acc_ref, *, n_experts):
    e = pl.program_id(1)
    f = pl.program_id(2)

    @pl.when((e == 0) & (f == 0))
    def _():
        acc_ref[...] = jnp.zeros_like(acc_ref)

    x = h_ref[...].astype(MXU_DTYPE)
    a = jnp.dot(x, wg_ref[0], preferred_element_type=jnp.float32)
    u = jnp.dot(x, wu_ref[0], preferred_element_type=jnp.float32)
    act = a * jax.nn.sigmoid(a) * u
    part = jnp.dot(act.astype(MXU_DTYPE), wd_ref[0], preferred_element_type=jnp.float32)
    if n_experts > 1:
        lane = lax.broadcasted_iota(jnp.int32, gate_ref.shape, 1)
        gate_e = jnp.sum(jnp.where(lane == e, gate_ref[...], 0.0), axis=-1, keepdims=True)
        part = gate_e * part
    acc_ref[...] += part

    @pl.when((e == pl.num_programs(1) - 1) & (f == pl.num_programs(2) - 1))
    def _():
        o_ref[...] = _layer_norm_rows(DEEPNORM_ALPHA * h_ref[...] + acc_ref[...], g_ref[...], b_ref[...])


def swiglu_experts_norm(h, gates, wg, wu, wd, ln_g, ln_b, *, tm=1024, tf=512):
    m, d = h.shape
    n_experts, _, f_dim = wg.shape
    assert f_dim % tf == 0 and m % tm == 0
    row = lambda i, e, f: (i, 0)
    full = lambda i, e, f: (0, 0)
    return pl.pallas_call(
        functools.partial(_ffn_body, n_experts=n_experts),
        grid=(m // tm, n_experts, f_dim // tf),
        in_specs=[pl.BlockSpec((tm, d), row), pl.BlockSpec((tm, LANE), row),
                  pl.BlockSpec((1, d, tf), lambda i, e, f: (e, 0, f)),
                  pl.BlockSpec((1, d, tf), lambda i, e, f: (e, 0, f)),
                  pl.BlockSpec((1, tf, d), lambda i, e, f: (e, f, 0)),
                  pl.BlockSpec((1, d), full), pl.BlockSpec((1, d), full)],
        out_specs=pl.BlockSpec((tm, d), row),
        out_shape=jax.ShapeDtypeStruct((m, d), jnp.float32),
        scratch_shapes=[pltpu.VMEM((tm, d), jnp.float32)],
        compiler_params=pltpu.CompilerParams(
            dimension_semantics=("parallel", "arbitrary", "arbitrary"),
            vmem_limit_bytes=VMEM_LIMIT_BYTES),
        name="swiglu_experts_norm",
    )(h, gates, wg, wu, wd, ln_g.reshape(1, d), ln_b.reshape(1, d))


def rms_norm(x, g):
    xf = x.astype(jnp.float32)
    return (xf * lax.rsqrt(jnp.mean(xf * xf, axis=-1, keepdims=True) + RMS_EPS) * g).astype(x.dtype)


def _ssm_combine(e1, e2):
    a1r, a1i, b1r, b1i = e1
    a2r, a2i, b2r, b2i = e2
    return (a2r * a1r - a2i * a1i, a2r * a1i + a2i * a1r,
            a2r * b1r - a2i * b1i + b2r, a2r * b1i + a2i * b1r + b2i)


def s5_branch(u, log_dt, lam_re, lam_im, b_re, b_im, c_re, c_im, d_skip, w_glu, b_glu, w_out):
    f32 = jnp.float32
    bsz, seq, _ = u.shape
    uf = u.astype(f32)
    ug = uf.reshape(bsz, seq, SSM_GROUPS, SSM_GROUP)
    lre, lim = lam_re.astype(f32), lam_im.astype(f32)
    dt = jnp.exp(log_dt.astype(f32))[:, None]
    mag = jnp.exp(lre * dt)
    ang = lim * dt
    a_re, a_im = mag * jnp.cos(ang), mag * jnp.sin(ang)
    den = lre * lre + lim * lim
    coef_re = ((a_re - 1.0) * lre + a_im * lim) / den
    coef_im = (a_im * lre - (a_re - 1.0) * lim) / den
    br, bi = b_re.astype(f32), b_im.astype(f32)
    bb_re = coef_re[..., None] * br - coef_im[..., None] * bi
    bb_im = coef_re[..., None] * bi + coef_im[..., None] * br
    bu_re = jnp.einsum('bsgc,gpc->bsgp', ug, bb_re)
    bu_im = jnp.einsum('bsgc,gpc->bsgp', ug, bb_im)
    ar = jnp.broadcast_to(a_re, bu_re.shape)
    ai = jnp.broadcast_to(a_im, bu_re.shape)
    _, _, h_re, h_im = lax.associative_scan(_ssm_combine, (ar, ai, bu_re, bu_im), axis=1)
    y = (jnp.einsum('bsgp,gcp->bsgc', h_re, c_re.astype(f32))
         - jnp.einsum('bsgp,gcp->bsgc', h_im, c_im.astype(f32)))
    y = y.reshape(bsz, seq, SSM_WIDTH) + d_skip.astype(f32) * uf
    y = jax.nn.gelu(y)
    y = y * jax.nn.sigmoid(y @ w_glu.astype(f32) + b_glu.astype(f32))
    return y.astype(u.dtype) @ w_out


def hgrn2_branch(q, f_logit, i_in, g_out, lower_bound, norm_g, w_out):
    f32 = jnp.float32
    bsz, seq, _ = q.shape
    nc = seq // HGRN_CHUNK

    def heads(t, dim):
        return jnp.moveaxis(t.astype(f32).reshape(bsz, nc, HGRN_CHUNK, HGRN_HEADS, dim), 1, 0)

    z = f_logit.astype(f32)
    lb = lower_bound.astype(f32)
    f = lb + (1.0 - lb) * jax.nn.sigmoid(z)
    log_f = jnp.log(jnp.maximum(f, F_MIN))
    k_in = (1.0 - lb) * jax.nn.sigmoid(-z)
    qh, kh, vh = heads(q, HGRN_DK), heads(k_in, HGRN_DK), heads(i_in, HGRN_DV)
    bcum = jnp.cumsum(heads(log_f, HGRN_DK), axis=2)
    causal = jnp.tril(jnp.ones((HGRN_CHUNK, HGRN_CHUNK), dtype=bool))[None, :, :, None, None]

    def step(state, inp):
        qc, kc, vc, bc = inp
        o_inter = jnp.einsum('bthk,bhkv->bthv', qc * jnp.exp(bc), state)
        diff = bc[:, :, None] - bc[:, None, :]
        decay = jnp.where(causal, jnp.exp(jnp.where(causal, diff, 0.0)), 0.0)
        att = jnp.einsum('bthk,bshk,btshk->bhts', qc, kc, decay)
        o_intra = jnp.einsum('bhts,bshv->bthv', att, vc)
        b_last = bc[:, -1]
        state = (jnp.exp(b_last)[..., None] * state
                 + jnp.einsum('bshk,bshv->bhkv', kc * jnp.exp(b_last[:, None] - bc), vc))
        return state, o_inter + o_intra

    s0 = jnp.zeros((bsz, HGRN_HEADS, HGRN_DK, HGRN_DV), f32)
    _, o = lax.scan(step, s0, (qh, kh, vh, bcum))
    o = jnp.moveaxis(o, 0, 1).reshape(bsz, seq, HGRN_HEADS, HGRN_DV)
    o = rms_norm(o, norm_g).reshape(bsz, seq, HGRN_WIDTH) * jax.nn.silu(g_out.astype(f32))
    return o.astype(q.dtype) @ w_out


def _norm_matmul_body(x_ref, g_ref, w_ref, o_ref):
    x = x_ref[...]
    xn = x * lax.rsqrt(jnp.mean(x * x, axis=-1, keepdims=True) + RMS_EPS) * g_ref[...]
    o_ref[...] = jnp.dot(xn.astype(MXU_DTYPE), w_ref[...],
                         preferred_element_type=jnp.float32).astype(o_ref.dtype)


def norm_matmul(x, g, w, *, tm=1024, out_dtype=None):
    m, k = x.shape
    _, n = w.shape
    tm = min(tm, m)
    return pl.pallas_call(
        _norm_matmul_body,
        grid=(m // tm,),
        in_specs=[pl.BlockSpec((tm, k), lambda i: (i, 0)), pl.BlockSpec((1, k), lambda i: (0, 0)),
                  pl.BlockSpec((k, n), lambda i: (0, 0))],
        out_specs=pl.BlockSpec((tm, n), lambda i: (i, 0)),
        out_shape=jax.ShapeDtypeStruct((m, n), out_dtype or MXU_DTYPE),
        compiler_params=pltpu.CompilerParams(
            dimension_semantics=("parallel",), vmem_limit_bytes=VMEM_LIMIT_BYTES),
        name="norm_matmul",
    )(x, g.reshape(1, k), w)


_INT_MIN = -2 ** 31
_MASK_KEY = int(np.float32(MASK_VALUE).view(np.int32)) ^ 0x7FFFFFFF
_SEARCH_ROWS = 64
_SEARCH_LANES = 512


def _dsa_body(qi_ref, kj_ref, qidx_ref, w_ref, kidt_ref, q_ref, kt_ref, v_ref, tri_ref, wout_ref,
              o_ref, keys_ref, thr_ref, need_ref, carry_ref, m_ref, acc_ref, *, tq, tk, seq, n_sel):
    f32 = jnp.float32
    p_id = pl.program_id(1)
    i = qi_ref[p_id]
    j = kj_ref[p_id]
    row_local = lax.broadcasted_iota(jnp.int32, (tq, tk), 0)
    col_local = lax.broadcasted_iota(jnp.int32, (tq, tk), 1)

    @pl.when(j == 0)
    def _select():
        def score_block(jj, carry):
            off = pl.multiple_of(jj * tk, tk)
            kb = kidt_ref[0, :, pl.ds(off, tk)]
            sc = jnp.zeros((tq, tk), f32)
            for h in range(IDX_HEADS):
                s = jnp.dot(qidx_ref[0, h], kb, preferred_element_type=f32)
                sc = sc + jnp.maximum(s, 0.0) * w_ref[0, :, h:h + 1]
            causal = (col_local + jj * tk) <= (row_local + i * tq)
            sc = jnp.where(causal, sc, MASK_VALUE)
            sc = jnp.where(sc == 0.0, 0.0, sc)
            bits = lax.bitcast_convert_type(sc, jnp.int32)
            keys_ref[:, pl.ds(off, tk)] = jnp.where(bits < 0, bits ^ 0x7FFFFFFF, bits)
            return carry

        lax.fori_loop(0, i + 1, score_block, 0)

        n_proc = (i + 1) * tk
        n_iter = n_proc // _SEARCH_LANES
        n_masked_tail = (seq - n_proc).astype(f32)

        def row_group(r, carry):
            r0 = pl.multiple_of(r * _SEARCH_ROWS, _SEARCH_ROWS)

            def count_ge(cand):
                cand_b = jnp.broadcast_to(cand, (_SEARCH_ROWS, LANE))

                def chunk(c, acc):
                    base = pl.multiple_of(c * _SEARCH_LANES, _SEARCH_LANES)
                    for u in range(_SEARCH_LANES // LANE):
                        kk = keys_ref[pl.ds(r0, _SEARCH_ROWS), pl.ds(base + u * LANE, LANE)]
                        acc = acc + jnp.where(kk >= cand_b, 1.0, 0.0)
                    return acc

                acc = lax.fori_loop(0, n_iter, chunk, jnp.zeros((_SEARCH_ROWS, LANE), f32))
                cnt = jnp.sum(acc, axis=1, keepdims=True)
                return cnt + jnp.where(cand <= _MASK_KEY, n_masked_tail, 0.0)

            zero = jnp.zeros((_SEARCH_ROWS, 1), jnp.int32)
            v0 = jnp.where(count_ge(zero) >= n_sel, zero, zero + _INT_MIN)

            def bit_step(b, v):
                cand = v | jnp.left_shift(jnp.int32(1), 30 - b)
                return jnp.where(count_ge(cand) >= n_sel, cand, v)

            v = lax.fori_loop(0, 31, bit_step, v0)
            thr_ref[pl.ds(r0, _SEARCH_ROWS), :] = v
            need_ref[pl.ds(r0, _SEARCH_ROWS), :] = n_sel - count_ge(v + 1)
            return carry

        lax.fori_loop(0, tq // _SEARCH_ROWS, row_group, 0)
        carry_ref[...] = jnp.zeros_like(carry_ref)
        m_ref[...] = jnp.full_like(m_ref, MASK_VALUE)
        acc_ref[...] = jnp.zeros_like(acc_ref)

    keys_blk = keys_ref[:, pl.ds(pl.multiple_of(j * tk, tk), tk)]
    thr = thr_ref[...]
    eq = keys_blk == thr
    tie_rank = carry_ref[...] + jnp.dot(jnp.where(eq, 1.0, 0.0).astype(MXU_DTYPE), tri_ref[...],
                                         preferred_element_type=f32)
    carry_ref[...] = tie_rank[:, tk - 1:tk]
    sel = (keys_blk > thr) | (eq & (tie_rank <= need_ref[...]))
    rel = (row_local + i * tq) - (col_local + j * tk)
    mask = sel & (rel >= 0)
    dist = rel.astype(f32)
    for h in range(ATT_HEADS):
        slope = 2.0 ** (-8.0 * (h + 1) / ATT_HEADS)
        s = jnp.dot(q_ref[0, h], kt_ref[0, h], preferred_element_type=f32) - slope * dist
        s = jnp.where(mask, s, MASK_VALUE)
        m_old = m_ref[h]
        m_new = jnp.maximum(m_old, jnp.max(s, axis=1, keepdims=True))
        p = jnp.exp(s - m_new)
        acc_ref[h] = jnp.exp(m_old - m_new) * acc_ref[h] + jnp.dot(
            p.astype(MXU_DTYPE), v_ref[0, h], preferred_element_type=f32)
        m_ref[h] = m_new

    @pl.when(j == i)
    def _finish():
        y = jnp.zeros((tq, D_MODEL), f32)
        for h in range(ATT_HEADS):
            a = acc_ref[h]
            o_h = a[:, :ATT_HEAD_DIM] / a[:, ATT_HEAD_DIM:ATT_HEAD_DIM + 1]
            y = y + jnp.dot(o_h.astype(MXU_DTYPE), wout_ref[h], preferred_element_type=f32)
        o_ref[0] = y


def dsa_attention(q_idx, w, kid_t, q, k_t, v_aug, w_out, *, tq=512):
    bsz, _, seq, _ = q.shape
    tq = min(tq, seq)
    tk = tq
    n_sel = min(TOPK_MAX, seq // 4)
    nq = seq // tq
    assert seq % tq == 0 and tk % _SEARCH_LANES == 0 and tq % _SEARCH_ROWS == 0
    pairs = [(a, b) for a in range(nq) for b in range(a + 1)]
    qi = jnp.asarray([a for a, _ in pairs], jnp.int32)
    kj = jnp.asarray([b for _, b in pairs], jnp.int32)
    tri = jnp.triu(jnp.ones((tk, tk), MXU_DTYPE))
    grid_spec = pltpu.PrefetchScalarGridSpec(
        num_scalar_prefetch=2,
        grid=(bsz, len(pairs)),
        in_specs=[
            pl.BlockSpec((1, IDX_HEADS, tq, IDX_DIM), lambda b, p, qi, kj: (b, 0, qi[p], 0)),
            pl.BlockSpec((1, tq, IDX_HEADS), lambda b, p, qi, kj: (b, qi[p], 0)),
            pl.BlockSpec((1, IDX_DIM, seq), lambda b, p, qi, kj: (b, 0, 0)),
            pl.BlockSpec((1, ATT_HEADS, tq, ATT_HEAD_DIM), lambda b, p, qi, kj: (b, 0, qi[p], 0)),
            pl.BlockSpec((1, ATT_HEADS, ATT_HEAD_DIM, tk), lambda b, p, qi, kj: (b, 0, 0, kj[p])),
            pl.BlockSpec((1, ATT_HEADS, tk, LANE), lambda b, p, qi, kj: (b, 0, kj[p], 0)),
            pl.BlockSpec((tk, tk), lambda b, p, qi, kj: (0, 0)),
            pl.BlockSpec((ATT_HEADS, ATT_HEAD_DIM, D_MODEL), lambda b, p, qi, kj: (0, 0, 0)),
        ],
        out_specs=pl.BlockSpec((1, tq, D_MODEL), lambda b, p, qi, kj: (b, qi[p], 0)),
        scratch_shapes=[
            pltpu.VMEM((tq, seq), jnp.int32),
            pltpu.VMEM((tq, 1), jnp.int32),
            pltpu.VMEM((tq, 1), jnp.float32),
            pltpu.VMEM((tq, 1), jnp.float32),
            pltpu.VMEM((ATT_HEADS, tq, 1), jnp.float32),
            pltpu.VMEM((ATT_HEADS, tq, LANE), jnp.float32),
        ])
    return pl.pallas_call(
        functools.partial(_dsa_body, tq=tq, tk=tk, seq=seq, n_sel=n_sel),
        grid_spec=grid_spec,
        out_shape=jax.ShapeDtypeStruct((bsz, seq, D_MODEL), jnp.float32),
        compiler_params=pltpu.CompilerParams(
            dimension_semantics=("parallel", "arbitrary"), vmem_limit_bytes=VMEM_LIMIT_BYTES),
        name="dsa_attention",
    )(qi, kj, q_idx, w, kid_t, q, k_t, v_aug, tri, w_out)


def dsa_branch(c_q, c_kv, k_idx, w_idx, q_norm_g, kv_norm_g, w_uq, w_qidx, w_ukv, w_out):
    bsz, seq, _ = c_q.shape
    m = bsz * seq
    w_q = jnp.concatenate([w_uq * ATT_HEAD_DIM ** -0.5, w_qidx * IDX_DIM ** -0.5], axis=1).astype(MXU_DTYPE)
    qq = norm_matmul(c_q.reshape(m, ATT_Q_RANK), q_norm_g, w_q)
    kv = norm_matmul(c_kv.reshape(m, ATT_KV_RANK), kv_norm_g, w_ukv.astype(MXU_DTYPE))
    q = qq[:, :ATT_WIDTH].reshape(bsz, seq, ATT_HEADS, ATT_HEAD_DIM).transpose(0, 2, 1, 3)
    q_idx = qq[:, ATT_WIDTH:].reshape(bsz, seq, IDX_HEADS, IDX_DIM).transpose(0, 2, 1, 3)
    k_t = kv[:, :ATT_WIDTH].reshape(bsz, seq, ATT_HEADS, ATT_HEAD_DIM).transpose(0, 2, 3, 1)
    v = kv[:, ATT_WIDTH:].reshape(bsz, seq, ATT_HEADS, ATT_HEAD_DIM).transpose(0, 2, 1, 3)
    ones = jnp.ones((bsz, ATT_HEADS, seq, 1), MXU_DTYPE)
    v_aug = jnp.concatenate([v, ones, jnp.zeros((bsz, ATT_HEADS, seq, LANE - ATT_HEAD_DIM - 1), MXU_DTYPE)],
                            axis=-1)
    kid_t = k_idx.astype(MXU_DTYPE).transpose(0, 2, 1)
    w = w_idx.astype(jnp.float32) * IDX_HEADS ** -0.5
    return dsa_attention(q_idx, w, kid_t, q, k_t, v_aug,
                         w_out.astype(MXU_DTYPE).reshape(ATT_HEADS, ATT_HEAD_DIM, D_MODEL))


def router_gates(h, router):
    logits = (h @ router).astype(jnp.float32)
    top_val, top_idx = lax.top_k(logits, TOP_K)
    top_w = jax.nn.softmax(top_val, axis=-1)
    onehot = (top_idx[..., None] == jnp.arange(N_EXPERTS)[None, None, :])
    gates = jnp.sum(jnp.where(onehot, top_w[..., None], 0.0), axis=1)
    return jnp.pad(gates, ((0, 0), (0, LANE - N_EXPERTS)))


def kernel(x, w_in, ssm_log_dt, ssm_lambda_re, ssm_lambda_im, ssm_b_re, ssm_b_im, ssm_c_re, ssm_c_im,
           ssm_d, ssm_w_glu, ssm_b_glu, ssm_w_out, hgrn_lb_logits, hgrn_norm_g, hgrn_w_out,
           attn_q_norm_g, attn_kv_norm_g, attn_w_uq, attn_w_qidx, attn_w_ukv, attn_w_out, w_o,
           ln_g, ln_b, ffn_w_gate, ffn_w_up, ffn_w_down, moe_router, moe_w_gate, moe_w_up, moe_w_down):
    bsz, seq, d = x.shape
    m = bsz * seq
    bf16 = jnp.bfloat16
    split_at = np.cumsum(IN_SPLITS)[:-1].tolist()
    lb_soft = jax.nn.softmax(hgrn_lb_logits.astype(jnp.float32), axis=0)
    lower_bounds = jnp.concatenate([jnp.zeros_like(lb_soft[:1]), jnp.cumsum(lb_soft[1:], axis=0)], axis=0)
    n_in_pad = _round_up(N_IN, 512)
    h = x.reshape(m, d)
    for l in range(DEPTH):
        w_in_l = jnp.pad(w_in[l], ((0, 0), (0, n_in_pad - N_IN))).astype(bf16)
        proj = matmul(h, w_in_l)[:, :N_IN].reshape(bsz, seq, N_IN)
        u, hq, hf, hi, hg, cq, ckv, kidx, widx, gates = jnp.split(proj, split_at, axis=-1)
        y_ssm = s5_branch(u, ssm_log_dt[l], ssm_lambda_re[l], ssm_lambda_im[l], ssm_b_re[l], ssm_b_im[l],
                          ssm_c_re[l], ssm_c_im[l], ssm_d[l], ssm_w_glu[l], ssm_b_glu[l], ssm_w_out[l])
        y_hg = hgrn2_branch(hq, hf, hi, hg, lower_bounds[l], hgrn_norm_g[l], hgrn_w_out[l])
        y_att = dsa_branch(cq, ckv, kidx, widx, attn_q_norm_g[l], attn_kv_norm_g[l],
                           attn_w_uq[l], attn_w_qidx[l], attn_w_ukv[l], attn_w_out[l])
        h = merge_project_norm(h, y_ssm.reshape(m, d), y_hg.reshape(m, d), y_att.reshape(m, d),
                               gates.reshape(m, N_BRANCHES * d), w_o[l].astype(bf16),
                               ln_g[l, 0], ln_b[l, 0])
        if l % 2 == 0:
            ones = jnp.ones((m, LANE), jnp.float32)
            h = swiglu_experts_norm(h, ones, ffn_w_gate[l // 2][None].astype(bf16),
                                    ffn_w_up[l // 2][None].astype(bf16),
                                    ffn_w_down[l // 2][None].astype(bf16),
                                    ln_g[l, 1], ln_b[l, 1], tf=256)
        else:
            gate_w = router_gates(h, moe_router[l // 2])
            h = swiglu_experts_norm(h, gate_w, moe_w_gate[l // 2].astype(bf16),
                                    moe_w_up[l // 2].astype(bf16), moe_w_down[l // 2].astype(bf16),
                                    ln_g[l, 1], ln_b[l, 1], tf=512)
    return h.reshape(bsz, seq, d)
```

```python
import functools
import math

import jax
import jax.numpy as jnp
import numpy as np
from jax import lax
from jax.experimental import pallas as pl
from jax.experimental.pallas import tpu as pltpu

D_MODEL = 1024
DEPTH = 2
SSM_WIDTH = 256
SSM_GROUP = 16
SSM_GROUPS = SSM_WIDTH // SSM_GROUP
SSM_STATE = 64
HGRN_HEADS = 4
HGRN_DK = 64
HGRN_DV = 64
HGRN_WIDTH = HGRN_HEADS * HGRN_DV
HGRN_CHUNK = 64
ATT_HEADS = 8
ATT_HEAD_DIM = 64
ATT_WIDTH = ATT_HEADS * ATT_HEAD_DIM
ATT_Q_RANK = 256
ATT_KV_RANK = 128
IDX_HEADS = 4
IDX_DIM = 64
TOPK_MAX = 256
Q_BLOCK = 128
MASK_VALUE = -1e30
N_BRANCHES = 3
N_EXPERTS = 8
TOP_K = 2
DEEPNORM_ALPHA = (2 * DEPTH) ** 0.25
LN_EPS = 1e-5
RMS_EPS = 1e-6
F_MIN = 1e-12

IN_SPLITS = (SSM_WIDTH, HGRN_HEADS * HGRN_DK, HGRN_HEADS * HGRN_DK, HGRN_WIDTH, HGRN_WIDTH,
             ATT_Q_RANK, ATT_KV_RANK, IDX_DIM, IDX_HEADS, N_BRANCHES * D_MODEL)
N_IN = sum(IN_SPLITS)

VMEM_LIMIT_BYTES = 48 * 1024 * 1024
LANE = 128
MXU_DTYPE = jnp.bfloat16


def _round_up(n, m):
    return (n + m - 1) // m * m


def _matmul_body(x_ref, w_ref, o_ref):
    o_ref[...] = jnp.dot(x_ref[...].astype(MXU_DTYPE), w_ref[...],
                         preferred_element_type=jnp.float32).astype(o_ref.dtype)


def matmul(x, w, *, tm=512, tn=512, out_dtype=jnp.float32):
    m, k = x.shape
    _, n = w.shape
    tn = min(tn, n)
    assert m % tm == 0 and n % tn == 0
    return pl.pallas_call(
        _matmul_body,
        grid=(m // tm, n // tn),
        in_specs=[pl.BlockSpec((tm, k), lambda i, j: (i, 0)),
                  pl.BlockSpec((k, tn), lambda i, j: (0, j))],
        out_specs=pl.BlockSpec((tm, tn), lambda i, j: (i, j)),
        out_shape=jax.ShapeDtypeStruct((m, n), out_dtype),
        compiler_params=pltpu.CompilerParams(
            dimension_semantics=("parallel", "parallel"), vmem_limit_bytes=VMEM_LIMIT_BYTES),
        name="matmul",
    )(x, w)


def _layer_norm_rows(y, g, b):
    mu = jnp.mean(y, axis=-1, keepdims=True)
    yc = y - mu
    var = jnp.mean(yc * yc, axis=-1, keepdims=True)
    return yc * lax.rsqrt(var + LN_EPS) * g + b


def _merge_body(h_ref, ys_ref, yh_ref, ya_ref, gt_ref, wo_ref, g_ref, b_ref, o_ref):
    d = D_MODEL
    gates = jax.nn.sigmoid(gt_ref[...])
    mixed = (gates[:, 0:d] * ys_ref[...] + gates[:, d:2 * d] * yh_ref[...]
             + gates[:, 2 * d:3 * d] * ya_ref[...])
    mix_out = jnp.dot(mixed.astype(MXU_DTYPE), wo_ref[...], preferred_element_type=jnp.float32)
    o_ref[...] = _layer_norm_rows(DEEPNORM_ALPHA * h_ref[...] + mix_out, g_ref[...], b_ref[...])


def merge_project_norm(h, y_ssm, y_hg, y_att, gate_logits, w_o, ln_g, ln_b, *, tm=512):
    m, d = h.shape
    row = lambda i: (i, 0)
    full = lambda i: (0, 0)
    return pl.pallas_call(
        _merge_body,
        grid=(m // tm,),
        in_specs=[pl.BlockSpec((tm, d), row), pl.BlockSpec((tm, d), row), pl.BlockSpec((tm, d), row),
                  pl.BlockSpec((tm, d), row), pl.BlockSpec((tm, N_BRANCHES * d), row),
                  pl.BlockSpec((d, d), full), pl.BlockSpec((1, d), full), pl.BlockSpec((1, d), full)],
        out_specs=pl.BlockSpec((tm, d), row),
        out_shape=jax.ShapeDtypeStruct((m, d), jnp.float32),
        compiler_params=pltpu.CompilerParams(
            dimension_semantics=("parallel",), vmem_limit_bytes=VMEM_LIMIT_BYTES),
        name="merge_project_norm",
    )(h, y_ssm, y_hg, y_att, gate_logits, w_o, ln_g.reshape(1, d), ln_b.reshape(1, d))


def _ffn_body(h_ref, gate_ref, wg_ref, wu_ref, wd_ref, g_ref, b_ref, o_ref, acc_ref, *, n_experts):
    e = pl.program_id(1)
    f = pl.program_id(2)

    @pl.when((e == 0) & (f == 0))
    def _():
        acc_ref[...] = jnp.zeros_like(acc_ref)

    x = h_ref[...].astype(MXU_DTYPE)
    a = jnp.dot(x, wg_ref[0], preferred_element_type=jnp.float32)
    u = jnp.dot(x, wu_ref[0], preferred_element_type=jnp.float32)
    act = a * jax.nn.sigmoid(a) * u
    part = jnp.dot(act.astype(MXU_DTYPE), wd_ref[0], preferred_element_type=jnp.float32)
    if n_experts > 1:
        lane = lax.broadcasted_iota(jnp.int32, gate_ref.shape, 1)
        gate_e = jnp.sum(jnp.where(lane == e, gate_ref[...], 0.0), axis=-1, keepdims=True)
        part = gate_e * part
    acc_ref[...] += part

    @pl.when((e == pl.num_programs(1) - 1) & (f == pl.num_programs(2) - 1))
    def _():
        o_ref[...] = _layer_norm_rows(DEEPNORM_ALPHA * h_ref[...] + acc_ref[...], g_ref[...], b_ref[...])


def swiglu_experts_norm(h, gates, wg, wu, wd, ln_g, ln_b, *, tm=1024, tf=512):
    m, d = h.shape
    n_experts, _, f_dim = wg.shape
    assert f_dim % tf == 0 and m % tm == 0
    row = lambda i, e, f: (i, 0)
    full = lambda i, e, f: (0, 0)
    return pl.pallas_call(
        functools.partial(_ffn_body, n_experts=n_experts),
        grid=(m // tm, n_experts, f_dim // tf),
        in_specs=[pl.BlockSpec((tm, d), row), pl.BlockSpec((tm, LANE), row),
                  pl.BlockSpec((1, d, tf), lambda i, e, f: (e, 0, f)),
                  pl.BlockSpec((1, d, tf), lambda i, e, f: (e, 0, f)),
                  pl.BlockSpec((1, tf, d), lambda i, e, f: (e, f, 0)),
                  pl.BlockSpec((1, d), full), pl.BlockSpec((1, d), full)],
        out_specs=pl.BlockSpec((tm, d), row),
        out_shape=jax.ShapeDtypeStruct((m, d), jnp.float32),
        scratch_shapes=[pltpu.VMEM((tm, d), jnp.float32)],
        compiler_params=pltpu.CompilerParams(
            dimension_semantics=("parallel", "arbitrary", "arbitrary"),
            vmem_limit_bytes=VMEM_LIMIT_BYTES),
        name="swiglu_experts_norm",
    )(h, gates, wg, wu, wd, ln_g.reshape(1, d), ln_b.reshape(1, d))


def rms_norm(x, g):
    xf = x.astype(jnp.float32)
    return (xf * lax.rsqrt(jnp.mean(xf * xf, axis=-1, keepdims=True) + RMS_EPS) * g).astype(x.dtype)


def _ssm_combine(e1, e2):
    a1r, a1i, b1r, b1i = e1
    a2r, a2i, b2r, b2i = e2
    return (a2r * a1r - a2i * a1i, a2r * a1i + a2i * a1r,
            a2r * b1r - a2i * b1i + b2r, a2r * b1i + a2i * b1r + b2i)


S5_CHUNK = 8
S5_CHUNK_WIDTH = S5_CHUNK * SSM_WIDTH
S5_STATE_WIDTH = SSM_GROUPS * SSM_STATE


def s5_operators(log_dt, lam_re, lam_im, b_re, b_im, c_re, c_im):
    f32 = jnp.float32
    n = S5_CHUNK
    lre, lim = lam_re.astype(f32), lam_im.astype(f32)
    dt = jnp.exp(log_dt.astype(f32))[:, None]
    mag = jnp.exp(lre * dt)
    ang = lim * dt
    a_re, a_im = mag * jnp.cos(ang), mag * jnp.sin(ang)
    den = lre * lre + lim * lim
    coef_re = ((a_re - 1.0) * lre + a_im * lim) / den
    coef_im = (a_im * lre - (a_re - 1.0) * lim) / den
    br, bi = b_re.astype(f32), b_im.astype(f32)
    bb_re = coef_re[..., None] * br - coef_im[..., None] * bi
    bb_im = coef_re[..., None] * bi + coef_im[..., None] * br
    j = jnp.arange(n + 1, dtype=f32)[:, None, None]
    pw_re = jnp.exp(j * lre * dt) * jnp.cos(j * ang)
    pw_im = jnp.exp(j * lre * dt) * jnp.sin(j * ang)
    eye = jnp.eye(SSM_GROUPS, dtype=f32)
    cr, ci = c_re.astype(f32), c_im.astype(f32)
    ab_re = pw_re[..., None] * bb_re - pw_im[..., None] * bb_im
    ab_im = pw_re[..., None] * bb_im + pw_im[..., None] * bb_re
    kern = (jnp.einsum('gcp,jgpd->jgcd', cr, ab_re[:n]) - jnp.einsum('gcp,jgpd->jgcd', ci, ab_im[:n]))
    lag = jnp.arange(n)[None, :] - jnp.arange(n)[:, None]
    toep = jnp.where((lag >= 0)[:, :, None, None, None], kern[jnp.maximum(lag, 0)], 0.0)
    t_mat = jnp.einsum('abgcd,gh->agdbhc', toep, eye).reshape(S5_CHUNK_WIDTH, S5_CHUNK_WIDTH)
    v = jnp.stack([ab_re[:n][::-1], ab_im[:n][::-1]])
    w_in = jnp.einsum('rlgpc,gh->lhcrgp', v, eye).reshape(S5_CHUNK_WIDTH, 2 * S5_STATE_WIDTH)
    ar, ai = pw_re[1:], pw_im[1:]
    wo_re = jnp.einsum('gcp,lgp->gplc', cr, ar) - jnp.einsum('gcp,lgp->gplc', ci, ai)
    wo_im = -jnp.einsum('gcp,lgp->gplc', cr, ai) - jnp.einsum('gcp,lgp->gplc', ci, ar)
    w_out = jnp.einsum('rgplc,gh->rhplgc', jnp.stack([wo_re, wo_im]), eye)
    w_out = w_out.reshape(2 * S5_STATE_WIDTH, S5_CHUNK_WIDTH)
    decay = jnp.stack([pw_re[n].reshape(1, S5_STATE_WIDTH), pw_im[n].reshape(1, S5_STATE_WIDTH)])
    return (jnp.concatenate([t_mat, w_in], axis=1).astype(MXU_DTYPE), w_out.astype(MXU_DTYPE), decay)


def _s5_scan_body(u_ref, tw_ref, wout_ref, decay_ref, y_ref, h_ref, s_ref, hs_ref, *, tm):
    f32 = jnp.float32
    sw = S5_STATE_WIDTH

    @pl.when(pl.program_id(1) == 0)
    def _():
        h_ref[...] = jnp.zeros_like(h_ref)

    r = jnp.dot(u_ref[0].astype(MXU_DTYPE), tw_ref[...], preferred_element_type=f32)
    s_ref[...] = r[:, S5_CHUNK_WIDTH:]
    d_re = decay_ref[0]
    d_im = decay_ref[1]

    def eight_chunks(k, carry):
        h_re, h_im = carry
        r0 = pl.multiple_of(k * 8, 8)
        inc = s_ref[pl.ds(r0, 8), :]
        rows_re, rows_im = [], []
        for t in range(8):
            rows_re.append(h_re)
            rows_im.append(h_im)
            h_re, h_im = (d_re * h_re - d_im * h_im + inc[t:t + 1, :sw],
                          d_re * h_im + d_im * h_re + inc[t:t + 1, sw:])
        hs_ref[pl.ds(r0, 8), :] = jnp.concatenate(
            [jnp.concatenate(rows_re, axis=0), jnp.concatenate(rows_im, axis=0)], axis=1)
        return h_re, h_im

    h_re, h_im = lax.fori_loop(0, tm // 8, eight_chunks, (h_ref[0:1, :], h_ref[1:2, :]))
    h_ref[0:1, :] = h_re
    h_ref[1:2, :] = h_im
    y_ref[0] = r[:, :S5_CHUNK_WIDTH] + jnp.dot(hs_ref[...].astype(MXU_DTYPE), wout_ref[...],
                                                preferred_element_type=f32)


def s5_scan(u, tw, w_out, decay, *, tm=256):
    bsz, seq, _ = u.shape
    n_chunks = seq // S5_CHUNK
    tm = min(tm, n_chunks)
    assert seq % S5_CHUNK == 0 and n_chunks % tm == 0 and tm % 8 == 0
    once = pl.Buffered(1)
    y = pl.pallas_call(
        functools.partial(_s5_scan_body, tm=tm),
        grid=(bsz, n_chunks // tm),
        in_specs=[pl.BlockSpec((1, tm, S5_CHUNK_WIDTH), lambda b, i: (b, i, 0)),
                  pl.BlockSpec(tw.shape, lambda b, i: (0, 0), pipeline_mode=once),
                  pl.BlockSpec(w_out.shape, lambda b, i: (0, 0), pipeline_mode=once),
                  pl.BlockSpec(decay.shape, lambda b, i: (0, 0, 0), pipeline_mode=once)],
        out_specs=pl.BlockSpec((1, tm, S5_CHUNK_WIDTH), lambda b, i: (b, i, 0)),
        out_shape=jax.ShapeDtypeStruct((bsz, n_chunks, S5_CHUNK_WIDTH), jnp.float32),
        scratch_shapes=[pltpu.VMEM((2, S5_STATE_WIDTH), jnp.float32),
                        pltpu.VMEM((tm, 2 * S5_STATE_WIDTH), jnp.float32),
                        pltpu.VMEM((tm, 2 * S5_STATE_WIDTH), jnp.float32)],
        compiler_params=pltpu.CompilerParams(
            dimension_semantics=("parallel", "arbitrary"), vmem_limit_bytes=VMEM_LIMIT_BYTES),
        name="s5_scan",
    )(u.reshape(bsz, n_chunks, S5_CHUNK_WIDTH), tw, w_out, decay)
    return y.reshape(bsz, seq, SSM_WIDTH)


def _s5_out_body(y_ref, u_ref, d_ref, wglu_ref, bglu_ref, wout_ref, o_ref):
    f32 = jnp.float32
    y = jax.nn.gelu(y_ref[...] + d_ref[...] * u_ref[...])
    gate = jnp.dot(y.astype(MXU_DTYPE), wglu_ref[...], preferred_element_type=f32) + bglu_ref[...]
    y = y * jax.nn.sigmoid(gate)
    o_ref[...] = jnp.dot(y.astype(MXU_DTYPE), wout_ref[...], preferred_element_type=f32)


def s5_output(y, u, d_skip, w_glu, b_glu, w_out, *, tm=1024):
    m, c = y.shape
    tm = min(tm, m)
    row = lambda i: (i, 0)
    full = lambda i: (0, 0)
    return pl.pallas_call(
        _s5_out_body,
        grid=(m // tm,),
        in_specs=[pl.BlockSpec((tm, c), row), pl.BlockSpec((tm, c), row), pl.BlockSpec((1, c), full),
                  pl.BlockSpec((c, c), full), pl.BlockSpec((1, c), full), pl.BlockSpec((c, D_MODEL), full)],
        out_specs=pl.BlockSpec((tm, D_MODEL), row),
        out_shape=jax.ShapeDtypeStruct((m, D_MODEL), jnp.float32),
        compiler_params=pltpu.CompilerParams(
            dimension_semantics=("parallel",), vmem_limit_bytes=VMEM_LIMIT_BYTES),
        name="s5_output",
    )(y, u, d_skip.reshape(1, c), w_glu.astype(MXU_DTYPE), b_glu.reshape(1, c), w_out.astype(MXU_DTYPE))


def s5_branch(u, log_dt, lam_re, lam_im, b_re, b_im, c_re, c_im, d_skip, w_glu, b_glu, w_out):
    bsz, seq, _ = u.shape
    tw, w_state_out, decay = s5_operators(log_dt, lam_re, lam_im, b_re, b_im, c_re, c_im)
    y = s5_scan(u, tw, w_state_out, decay)
    out = s5_output(y.reshape(bsz * seq, SSM_WIDTH), u.reshape(bsz * seq, SSM_WIDTH),
                    d_skip, w_glu, b_glu, w_out)
    return out.reshape(bsz, seq, D_MODEL)


def hgrn2_branch(q, f_logit, i_in, g_out, lower_bound, norm_g, w_out):
    f32 = jnp.float32
    bsz, seq, _ = q.shape
    nc = seq // HGRN_CHUNK

    def heads(t, dim):
        return jnp.moveaxis(t.astype(f32).reshape(bsz, nc, HGRN_CHUNK, HGRN_HEADS, dim), 1, 0)

    z = f_logit.astype(f32)
    lb = lower_bound.astype(f32)
    f = lb + (1.0 - lb) * jax.nn.sigmoid(z)
    log_f = jnp.log(jnp.maximum(f, F_MIN))
    k_in = (1.0 - lb) * jax.nn.sigmoid(-z)
    qh, kh, vh = heads(q, HGRN_DK), heads(k_in, HGRN_DK), heads(i_in, HGRN_DV)
    bcum = jnp.cumsum(heads(log_f, HGRN_DK), axis=2)
    causal = jnp.tril(jnp.ones((HGRN_CHUNK, HGRN_CHUNK), dtype=bool))[None, :, :, None, None]

    def step(state, inp):
        qc, kc, vc, bc = inp
        o_inter = jnp.einsum('bthk,bhkv->bthv', qc * jnp.exp(bc), state)
        diff = bc[:, :, None] - bc[:, None, :]
        decay = jnp.where(causal, jnp.exp(jnp.where(causal, diff, 0.0)), 0.0)
        att = jnp.einsum('bthk,bshk,btshk->bhts', qc, kc, decay)
        o_intra = jnp.einsum('bhts,bshv->bthv', att, vc)
        b_last = bc[:, -1]
        state = (jnp.exp(b_last)[..., None] * state
                 + jnp.einsum('bshk,bshv->bhkv', kc * jnp.exp(b_last[:, None] - bc), vc))
        return state, o_inter + o_intra

    s0 = jnp.zeros((bsz, HGRN_HEADS, HGRN_DK, HGRN_DV), f32)
    _, o = lax.scan(step, s0, (qh, kh, vh, bcum))
    o = jnp.moveaxis(o, 0, 1).reshape(bsz, seq, HGRN_HEADS, HGRN_DV)
    o = rms_norm(o, norm_g).reshape(bsz, seq, HGRN_WIDTH) * jax.nn.silu(g_out.astype(f32))
    return o.astype(q.dtype) @ w_out


def _norm_matmul_body(x_ref, g_ref, w_ref, o_ref):
    x = x_ref[...]
    xn = x * lax.rsqrt(jnp.mean(x * x, axis=-1, keepdims=True) + RMS_EPS) * g_ref[...]
    o_ref[...] = jnp.dot(xn.astype(MXU_DTYPE), w_ref[...],
                         preferred_element_type=jnp.float32).astype(o_ref.dtype)


def norm_matmul(x, g, w, *, tm=1024, out_dtype=None):
    m, k = x.shape
    _, n = w.shape
    tm = min(tm, m)
    return pl.pallas_call(
        _norm_matmul_body,
        grid=(m // tm,),
        in_specs=[pl.BlockSpec((tm, k), lambda i: (i, 0)), pl.BlockSpec((1, k), lambda i: (0, 0)),
                  pl.BlockSpec((k, n), lambda i: (0, 0))],
        out_specs=pl.BlockSpec((tm, n), lambda i: (i, 0)),
        out_shape=jax.ShapeDtypeStruct((m, n), out_dtype or MXU_DTYPE),
        compiler_params=pltpu.CompilerParams(
            dimension_semantics=("parallel",), vmem_limit_bytes=VMEM_LIMIT_BYTES),
        name="norm_matmul",
    )(x, g.reshape(1, k), w)


_INT_MIN = -2 ** 31
_MASK_KEY = int(np.float32(MASK_VALUE).view(np.int32)) ^ 0x7FFFFFFF
_SEARCH_ROWS = 64
_SEARCH_LANES = 512


def _dsa_body(qi_ref, kj_ref, qidx_ref, w_ref, kidt_ref, q_ref, kt_ref, v_ref, tri_ref, wout_ref,
              o_ref, keys_ref, thr_ref, need_ref, carry_ref, m_ref, acc_ref, *, tq, tk, seq, n_sel):
    f32 = jnp.float32
    p_id = pl.program_id(1)
    i = qi_ref[p_id]
    j = kj_ref[p_id]
    row_local = lax.broadcasted_iota(jnp.int32, (tq, tk), 0)
    col_local = lax.broadcasted_iota(jnp.int32, (tq, tk), 1)

    @pl.when(j == 0)
    def _select():
        def score_block(jj, carry):
            off = pl.multiple_of(jj * tk, tk)
            kb = kidt_ref[0, :, pl.ds(off, tk)]
            sc = jnp.zeros((tq, tk), f32)
            for h in range(IDX_HEADS):
                s = jnp.dot(qidx_ref[0, h], kb, preferred_element_type=f32)
                sc = sc + jnp.maximum(s, 0.0) * w_ref[0, :, h:h + 1]
            causal = (col_local + jj * tk) <= (row_local + i * tq)
            sc = jnp.where(causal, sc, MASK_VALUE)
            sc = jnp.where(sc == 0.0, 0.0, sc)
            bits = lax.bitcast_convert_type(sc, jnp.int32)
            keys_ref[:, pl.ds(off, tk)] = jnp.where(bits < 0, bits ^ 0x7FFFFFFF, bits)
            return carry

        lax.fori_loop(0, i + 1, score_block, 0)

        n_proc = (i + 1) * tk
        n_iter = n_proc // _SEARCH_LANES
        n_masked_tail = (seq - n_proc).astype(f32)

        def row_group(r, carry):
            r0 = pl.multiple_of(r * _SEARCH_ROWS, _SEARCH_ROWS)

            def count_ge(cand):
                cand_b = jnp.broadcast_to(cand, (_SEARCH_ROWS, LANE))

                def chunk(c, acc):
                    base = pl.multiple_of(c * _SEARCH_LANES, _SEARCH_LANES)
                    for u in range(_SEARCH_LANES // LANE):
                        kk = keys_ref[pl.ds(r0, _SEARCH_ROWS), pl.ds(base + u * LANE, LANE)]
                        acc = acc + jnp.where(kk >= cand_b, 1.0, 0.0)
                    return acc

                acc = lax.fori_loop(0, n_iter, chunk, jnp.zeros((_SEARCH_ROWS, LANE), f32))
                cnt = jnp.sum(acc, axis=1, keepdims=True)
                return cnt + jnp.where(cand <= _MASK_KEY, n_masked_tail, 0.0)

            zero = jnp.zeros((_SEARCH_ROWS, 1), jnp.int32)
            v0 = jnp.where(count_ge(zero) >= n_sel, zero, zero + _INT_MIN)

            def bit_step(b, v):
                cand = v | jnp.left_shift(jnp.int32(1), 30 - b)
                return jnp.where(count_ge(cand) >= n_sel, cand, v)

            v = lax.fori_loop(0, 31, bit_step, v0)
            thr_ref[pl.ds(r0, _SEARCH_ROWS), :] = v
            need_ref[pl.ds(r0, _SEARCH_ROWS), :] = n_sel - count_ge(v + 1)
            return carry

        lax.fori_loop(0, tq // _SEARCH_ROWS, row_group, 0)
        carry_ref[...] = jnp.zeros_like(carry_ref)
        m_ref[...] = jnp.full_like(m_ref, MASK_VALUE)
        acc_ref[...] = jnp.zeros_like(acc_ref)

    keys_blk = keys_ref[:, pl.ds(pl.multiple_of(j * tk, tk), tk)]
    thr = thr_ref[...]
    eq = keys_blk == thr
    tie_rank = carry_ref[...] + jnp.dot(jnp.where(eq, 1.0, 0.0).astype(MXU_DTYPE), tri_ref[...],
                                         preferred_element_type=f32)
    carry_ref[...] = tie_rank[:, tk - 1:tk]
    sel = (keys_blk > thr) | (eq & (tie_rank <= need_ref[...]))
    rel = (row_local + i * tq) - (col_local + j * tk)
    mask = sel & (rel >= 0)
    dist = rel.astype(f32)
    for h in range(ATT_HEADS):
        slope = 2.0 ** (-8.0 * (h + 1) / ATT_HEADS)
        s = jnp.dot(q_ref[0, h], kt_ref[0, h], preferred_element_type=f32) - slope * dist
        s = jnp.where(mask, s, MASK_VALUE)
        m_old = m_ref[h]
        m_new = jnp.maximum(m_old, jnp.max(s, axis=1, keepdims=True))
        p = jnp.exp(s - m_new)
        acc_ref[h] = jnp.exp(m_old - m_new) * acc_ref[h] + jnp.dot(
            p.astype(MXU_DTYPE), v_ref[0, h], preferred_element_type=f32)
        m_ref[h] = m_new

    @pl.when(j == i)
    def _finish():
        y = jnp.zeros((tq, D_MODEL), f32)
        for h in range(ATT_HEADS):
            a = acc_ref[h]
            o_h = a[:, :ATT_HEAD_DIM] / a[:, ATT_HEAD_DIM:ATT_HEAD_DIM + 1]
            y = y + jnp.dot(o_h.astype(MXU_DTYPE), wout_ref[h], preferred_element_type=f32)
        o_ref[0] = y


def dsa_attention(q_idx, w, kid_t, q, k_t, v_aug, w_out, *, tq=512):
    bsz, _, seq, _ = q.shape
    tq = min(tq, seq)
    tk = tq
    n_sel = min(TOPK_MAX, seq // 4)
    nq = seq // tq
    assert seq % tq == 0 and tk % _SEARCH_LANES == 0 and tq % _SEARCH_ROWS == 0
    pairs = [(a, b) for a in range(nq) for b in range(a + 1)]
    qi = jnp.asarray([a for a, _ in pairs], jnp.int32)
    kj = jnp.asarray([b for _, b in pairs], jnp.int32)
    tri = jnp.triu(jnp.ones((tk, tk), MXU_DTYPE))
    grid_spec = pltpu.PrefetchScalarGridSpec(
        num_scalar_prefetch=2,
        grid=(bsz, len(pairs)),
        in_specs=[
            pl.BlockSpec((1, IDX_HEADS, tq, IDX_DIM), lambda b, p, qi, kj: (b, 0, qi[p], 0)),
            pl.BlockSpec((1, tq, IDX_HEADS), lambda b, p, qi, kj: (b, qi[p], 0)),
            pl.BlockSpec((1, IDX_DIM, seq), lambda b, p, qi, kj: (b, 0, 0)),
            pl.BlockSpec((1, ATT_HEADS, tq, ATT_HEAD_DIM), lambda b, p, qi, kj: (b, 0, qi[p], 0)),
            pl.BlockSpec((1, ATT_HEADS, ATT_HEAD_DIM, tk), lambda b, p, qi, kj: (b, 0, 0, kj[p])),
            pl.BlockSpec((1, ATT_HEADS, tk, LANE), lambda b, p, qi, kj: (b, 0, kj[p], 0)),
            pl.BlockSpec((tk, tk), lambda b, p, qi, kj: (0, 0)),
            pl.BlockSpec((ATT_HEADS, ATT_HEAD_DIM, D_MODEL), lambda b, p, qi, kj: (0, 0, 0)),
        ],
        out_specs=pl.BlockSpec((1, tq, D_MODEL), lambda b, p, qi, kj: (b, qi[p], 0)),
        scratch_shapes=[
            pltpu.VMEM((tq, seq), jnp.int32),
            pltpu.VMEM((tq, 1), jnp.int32),
            pltpu.VMEM((tq, 1), jnp.float32),
            pltpu.VMEM((tq, 1), jnp.float32),
            pltpu.VMEM((ATT_HEADS, tq, 1), jnp.float32),
            pltpu.VMEM((ATT_HEADS, tq, LANE), jnp.float32),
        ])
    return pl.pallas_call(
        functools.partial(_dsa_body, tq=tq, tk=tk, seq=seq, n_sel=n_sel),
        grid_spec=grid_spec,
        out_shape=jax.ShapeDtypeStruct((bsz, seq, D_MODEL), jnp.float32),
        compiler_params=pltpu.CompilerParams(
            dimension_semantics=("parallel", "arbitrary"), vmem_limit_bytes=VMEM_LIMIT_BYTES),
        name="dsa_attention",
    )(qi, kj, q_idx, w, kid_t, q, k_t, v_aug, tri, w_out)


def dsa_branch(c_q, c_kv, k_idx, w_idx, q_norm_g, kv_norm_g, w_uq, w_qidx, w_ukv, w_out):
    bsz, seq, _ = c_q.shape
    m = bsz * seq
    w_q = jnp.concatenate([w_uq * ATT_HEAD_DIM ** -0.5, w_qidx * IDX_DIM ** -0.5], axis=1).astype(MXU_DTYPE)
    qq = norm_matmul(c_q.reshape(m, ATT_Q_RANK), q_norm_g, w_q)
    kv = norm_matmul(c_kv.reshape(m, ATT_KV_RANK), kv_norm_g, w_ukv.astype(MXU_DTYPE))
    q = qq[:, :ATT_WIDTH].reshape(bsz, seq, ATT_HEADS, ATT_HEAD_DIM).transpose(0, 2, 1, 3)
    q_idx = qq[:, ATT_WIDTH:].reshape(bsz, seq, IDX_HEADS, IDX_DIM).transpose(0, 2, 1, 3)
    k_t = kv[:, :ATT_WIDTH].reshape(bsz, seq, ATT_HEADS, ATT_HEAD_DIM).transpose(0, 2, 3, 1)
    v = kv[:, ATT_WIDTH:].reshape(bsz, seq, ATT_HEADS, ATT_HEAD_DIM).transpose(0, 2, 1, 3)
    ones = jnp.ones((bsz, ATT_HEADS, seq, 1), MXU_DTYPE)
    v_aug = jnp.concatenate([v, ones, jnp.zeros((bsz, ATT_HEADS, seq, LANE - ATT_HEAD_DIM - 1), MXU_DTYPE)],
                            axis=-1)
    kid_t = k_idx.astype(MXU_DTYPE).transpose(0, 2, 1)
    w = w_idx.astype(jnp.float32) * IDX_HEADS ** -0.5
    return dsa_attention(q_idx, w, kid_t, q, k_t, v_aug,
                         w_out.astype(MXU_DTYPE).reshape(ATT_HEADS, ATT_HEAD_DIM, D_MODEL))


def router_gates(h, router):
    logits = (h @ router).astype(jnp.float32)
    top_val, top_idx = lax.top_k(logits, TOP_K)
    top_w = jax.nn.softmax(top_val, axis=-1)
    onehot = (top_idx[..., None] == jnp.arange(N_EXPERTS)[None, None, :])
    gates = jnp.sum(jnp.where(onehot, top_w[..., None], 0.0), axis=1)
    return jnp.pad(gates, ((0, 0), (0, LANE - N_EXPERTS)))


def kernel(x, w_in, ssm_log_dt, ssm_lambda_re, ssm_lambda_im, ssm_b_re, ssm_b_im, ssm_c_re, ssm_c_im,
           ssm_d, ssm_w_glu, ssm_b_glu, ssm_w_out, hgrn_lb_logits, hgrn_norm_g, hgrn_w_out,
           attn_q_norm_g, attn_kv_norm_g, attn_w_uq, attn_w_qidx, attn_w_ukv, attn_w_out, w_o,
           ln_g, ln_b, ffn_w_gate, ffn_w_up, ffn_w_down, moe_router, moe_w_gate, moe_w_up, moe_w_down):
    bsz, seq, d = x.shape
    m = bsz * seq
    bf16 = jnp.bfloat16
    split_at = np.cumsum(IN_SPLITS)[:-1].tolist()
    lb_soft = jax.nn.softmax(hgrn_lb_logits.astype(jnp.float32), axis=0)
    lower_bounds = jnp.concatenate([jnp.zeros_like(lb_soft[:1]), jnp.cumsum(lb_soft[1:], axis=0)], axis=0)
    n_in_pad = _round_up(N_IN, 512)
    h = x.reshape(m, d)
    for l in range(DEPTH):
        w_in_l = jnp.pad(w_in[l], ((0, 0), (0, n_in_pad - N_IN))).astype(bf16)
        proj = matmul(h, w_in_l)[:, :N_IN].reshape(bsz, seq, N_IN)
        u, hq, hf, hi, hg, cq, ckv, kidx, widx, gates = jnp.split(proj, split_at, axis=-1)
        y_ssm = s5_branch(u, ssm_log_dt[l], ssm_lambda_re[l], ssm_lambda_im[l], ssm_b_re[l], ssm_b_im[l],
                          ssm_c_re[l], ssm_c_im[l], ssm_d[l], ssm_w_glu[l], ssm_b_glu[l], ssm_w_out[l])
        y_hg = hgrn2_branch(hq, hf, hi, hg, lower_bounds[l], hgrn_norm_g[l], hgrn_w_out[l])
        y_att = dsa_branch(cq, ckv, kidx, widx, attn_q_norm_g[l], attn_kv_norm_g[l],
                           attn_w_uq[l], attn_w_qidx[l], attn_w_ukv[l], attn_w_out[l])
        h = merge_project_norm(h, y_ssm.reshape(m, d), y_hg.reshape(m, d), y_att.reshape(m, d),
                               gates.reshape(m, N_BRANCHES * d), w_o[l].astype(bf16),
                               ln_g[l, 0], ln_b[l, 0])
        if l % 2 == 0:
            ones = jnp.ones((m, LANE), jnp.float32)
            h = swiglu_experts_norm(h, ones, ffn_w_gate[l // 2][None].astype(bf16),
                                    ffn_w_up[l // 2][None].astype(bf16),
                                    ffn_w_down[l // 2][None].astype(bf16),
                                    ln_g[l, 1], ln_b[l, 1], tf=256)
        else:
            gate_w = router_gates(h, moe_router[l // 2])
            h = swiglu_experts_norm(h, gate_w, moe_w_gate[l // 2].astype(bf16),
                                    moe_w_up[l // 2].astype(bf16), moe_w_down[l // 2].astype(bf16),
                                    ln_g[l, 1], ln_b[l, 1], tf=512)
    return h.reshape(bsz, seq, d)
```

```python
import functools
import math

import jax
import jax.numpy as jnp
import numpy as np
from jax import lax
from jax.experimental import pallas as pl
from jax.experimental.pallas import tpu as pltpu

D_MODEL = 1024
DEPTH = 2
SSM_WIDTH = 256
SSM_GROUP = 16
SSM_GROUPS = SSM_WIDTH // SSM_GROUP
SSM_STATE = 64
HGRN_HEADS = 4
HGRN_DK = 64
HGRN_DV = 64
HGRN_WIDTH = HGRN_HEADS * HGRN_DV
HGRN_CHUNK = 64
ATT_HEADS = 8
ATT_HEAD_DIM = 64
ATT_WIDTH = ATT_HEADS * ATT_HEAD_DIM
ATT_Q_RANK = 256
ATT_KV_RANK = 128
IDX_HEADS = 4
IDX_DIM = 64
TOPK_MAX = 256
Q_BLOCK = 128
MASK_VALUE = -1e30
N_BRANCHES = 3
N_EXPERTS = 8
TOP_K = 2
DEEPNORM_ALPHA = (2 * DEPTH) ** 0.25
LN_EPS = 1e-5
RMS_EPS = 1e-6
F_MIN = 1e-12

IN_SPLITS = (SSM_WIDTH, HGRN_HEADS * HGRN_DK, HGRN_HEADS * HGRN_DK, HGRN_WIDTH, HGRN_WIDTH,
             ATT_Q_RANK, ATT_KV_RANK, IDX_DIM, IDX_HEADS, N_BRANCHES * D_MODEL)
N_IN = sum(IN_SPLITS)

VMEM_LIMIT_BYTES = 48 * 1024 * 1024
LANE = 128
MXU_DTYPE = jnp.bfloat16


def _round_up(n, m):
    return (n + m - 1) // m * m


def _matmul_body(x_ref, w_ref, o_ref):
    o_ref[...] = jnp.dot(x_ref[...].astype(MXU_DTYPE), w_ref[...],
                         preferred_element_type=jnp.float32).astype(o_ref.dtype)


def matmul(x, w, *, tm=512, tn=512, out_dtype=jnp.float32):
    m, k = x.shape
    _, n = w.shape
    tn = min(tn, n)
    assert m % tm == 0 and n % tn == 0
    return pl.pallas_call(
        _matmul_body,
        grid=(m // tm, n // tn),
        in_specs=[pl.BlockSpec((tm, k), lambda i, j: (i, 0)),
                  pl.BlockSpec((k, tn), lambda i, j: (0, j))],
        out_specs=pl.BlockSpec((tm, tn), lambda i, j: (i, j)),
        out_shape=jax.ShapeDtypeStruct((m, n), out_dtype),
        compiler_params=pltpu.CompilerParams(
            dimension_semantics=("parallel", "parallel"), vmem_limit_bytes=VMEM_LIMIT_BYTES),
        name="matmul",
    )(x, w)


def _layer_norm_rows(y, g, b):
    mu = jnp.mean(y, axis=-1, keepdims=True)
    yc = y - mu
    var = jnp.mean(yc * yc, axis=-1, keepdims=True)
    return yc * lax.rsqrt(var + LN_EPS) * g + b


def _merge_body(h_ref, ys_ref, yh_ref, ya_ref, gt_ref, wo_ref, g_ref, b_ref, o_ref):
    d = D_MODEL
    gates = jax.nn.sigmoid(gt_ref[...])
    mixed = (gates[:, 0:d] * ys_ref[...] + gates[:, d:2 * d] * yh_ref[...]
             + gates[:, 2 * d:3 * d] * ya_ref[...])
    mix_out = jnp.dot(mixed.astype(MXU_DTYPE), wo_ref[...], preferred_element_type=jnp.float32)
    o_ref[...] = _layer_norm_rows(DEEPNORM_ALPHA * h_ref[...] + mix_out, g_ref[...], b_ref[...])


def merge_project_norm(h, y_ssm, y_hg, y_att, gate_logits, w_o, ln_g, ln_b, *, tm=512):
    m, d = h.shape
    row = lambda i: (i, 0)
    full = lambda i: (0, 0)
    return pl.pallas_call(
        _merge_body,
        grid=(m // tm,),
        in_specs=[pl.BlockSpec((tm, d), row), pl.BlockSpec((tm, d), row), pl.BlockSpec((tm, d), row),
                  pl.BlockSpec((tm, d), row), pl.BlockSpec((tm, N_BRANCHES * d), row),
                  pl.BlockSpec((d, d), full), pl.BlockSpec((1, d), full), pl.BlockSpec((1, d), full)],
        out_specs=pl.BlockSpec((tm, d), row),
        out_shape=jax.ShapeDtypeStruct((m, d), jnp.float32),
        compiler_params=pltpu.CompilerParams(
            dimension_semantics=("parallel",), vmem_limit_bytes=VMEM_LIMIT_BYTES),
        name="merge_project_norm",
    )(h, y_ssm, y_hg, y_att, gate_logits, w_o, ln_g.reshape(1, d), ln_b.reshape(1, d))


def _ffn_body(h_ref, gate_ref, wg_ref, wu_ref, wd_ref, g_ref, b_ref, o_ref, acc_ref, *, n_experts):
    e = pl.program_id(1)
    f = pl.program_id(2)

    @pl.when((e == 0) & (f == 0))
    def _():
        acc_ref[...] = jnp.zeros_like(acc_ref)

    x = h_ref[...].astype(MXU_DTYPE)
    a = jnp.dot(x, wg_ref[0], preferred_element_type=jnp.float32)
    u = jnp.dot(x, wu_ref[0], preferred_element_type=jnp.float32)
    act = a * jax.nn.sigmoid(a) * u
    part = jnp.dot(act.astype(MXU_DTYPE), wd_ref[0], preferred_element_type=jnp.float32)
    if n_experts > 1:
        lane = lax.broadcasted_iota(jnp.int32, gate_ref.shape, 1)
        gate_e = jnp.sum(jnp.where(lane == e, gate_ref[...], 0.0), axis=-1, keepdims=True)
        part = gate_e * part
    acc_ref[...] += part

    @pl.when((e == pl.num_programs(1) - 1) & (f == pl.num_programs(2) - 1))
    def _():
        o_ref[...] = _layer_norm_rows(DEEPNORM_ALPHA * h_ref[...] + acc_ref[...], g_ref[...], b_ref[...])


def swiglu_experts_norm(h, gates, wg, wu, wd, ln_g, ln_b, *, tm=1024, tf=512):
    m, d = h.shape
    n_experts, _, f_dim = wg.shape
    assert f_dim % tf == 0 and m % tm == 0
    row = lambda i, e, f: (i, 0)
    full = lambda i, e, f: (0, 0)
    return pl.pallas_call(
        functools.partial(_ffn_body, n_experts=n_experts),
        grid=(m // tm, n_experts, f_dim // tf),
        in_specs=[pl.BlockSpec((tm, d), row), pl.BlockSpec((tm, LANE), row),
                  pl.BlockSpec((1, d, tf), lambda i, e, f: (e, 0, f)),
                  pl.BlockSpec((1, d, tf), lambda i, e, f: (e, 0, f)),
                  pl.BlockSpec((1, tf, d), lambda i, e, f: (e, f, 0)),
                  pl.BlockSpec((1, d), full), pl.BlockSpec((1, d), full)],
        out_specs=pl.BlockSpec((tm, d), row),
        out_shape=jax.ShapeDtypeStruct((m, d), jnp.float32),
        scratch_shapes=[pltpu.VMEM((tm, d), jnp.float32)],
        compiler_params=pltpu.CompilerParams(
            dimension_semantics=("parallel", "arbitrary", "arbitrary"),
            vmem_limit_bytes=VMEM_LIMIT_BYTES),
        name="swiglu_experts_norm",
    )(h, gates, wg, wu, wd, ln_g.reshape(1, d), ln_b.reshape(1, d))


def rms_norm(x, g):
    xf = x.astype(jnp.float32)
    return (xf * lax.rsqrt(jnp.mean(xf * xf, axis=-1, keepdims=True) + RMS_EPS) * g).astype(x.dtype)


def _ssm_combine(e1, e2):
    a1r, a1i, b1r, b1i = e1
    a2r, a2i, b2r, b2i = e2
    return (a2r * a1r - a2i * a1i, a2r * a1i + a2i * a1r,
            a2r * b1r - a2i * b1i + b2r, a2r * b1i + a2i * b1r + b2i)


S5_CHUNK = 8
S5_CHUNK_WIDTH = S5_CHUNK * SSM_WIDTH
S5_STATE_WIDTH = SSM_GROUPS * SSM_STATE


def s5_operators(log_dt, lam_re, lam_im, b_re, b_im, c_re, c_im):
    f32 = jnp.float32
    n = S5_CHUNK
    lre, lim = lam_re.astype(f32), lam_im.astype(f32)
    dt = jnp.exp(log_dt.astype(f32))[:, None]
    mag = jnp.exp(lre * dt)
    ang = lim * dt
    a_re, a_im = mag * jnp.cos(ang), mag * jnp.sin(ang)
    den = lre * lre + lim * lim
    coef_re = ((a_re - 1.0) * lre + a_im * lim) / den
    coef_im = (a_im * lre - (a_re - 1.0) * lim) / den
    br, bi = b_re.astype(f32), b_im.astype(f32)
    bb_re = coef_re[..., None] * br - coef_im[..., None] * bi
    bb_im = coef_re[..., None] * bi + coef_im[..., None] * br
    j = jnp.arange(n + 1, dtype=f32)[:, None, None]
    pw_re = jnp.exp(j * lre * dt) * jnp.cos(j * ang)
    pw_im = jnp.exp(j * lre * dt) * jnp.sin(j * ang)
    eye = jnp.eye(SSM_GROUPS, dtype=f32)
    cr, ci = c_re.astype(f32), c_im.astype(f32)
    ab_re = pw_re[..., None] * bb_re - pw_im[..., None] * bb_im
    ab_im = pw_re[..., None] * bb_im + pw_im[..., None] * bb_re
    kern = (jnp.einsum('gcp,jgpd->jgcd', cr, ab_re[:n]) - jnp.einsum('gcp,jgpd->jgcd', ci, ab_im[:n]))
    lag = jnp.arange(n)[None, :] - jnp.arange(n)[:, None]
    toep = jnp.where((lag >= 0)[:, :, None, None, None], kern[jnp.maximum(lag, 0)], 0.0)
    t_mat = jnp.einsum('abgcd,gh->agdbhc', toep, eye).reshape(S5_CHUNK_WIDTH, S5_CHUNK_WIDTH)
    v = jnp.stack([ab_re[:n][::-1], ab_im[:n][::-1]])
    w_in = jnp.einsum('rlgpc,gh->lhcrgp', v, eye).reshape(S5_CHUNK_WIDTH, 2 * S5_STATE_WIDTH)
    ar, ai = pw_re[1:], pw_im[1:]
    wo_re = jnp.einsum('gcp,lgp->gplc', cr, ar) - jnp.einsum('gcp,lgp->gplc', ci, ai)
    wo_im = -jnp.einsum('gcp,lgp->gplc', cr, ai) - jnp.einsum('gcp,lgp->gplc', ci, ar)
    w_out = jnp.einsum('rgplc,gh->rhplgc', jnp.stack([wo_re, wo_im]), eye)
    w_out = w_out.reshape(2 * S5_STATE_WIDTH, S5_CHUNK_WIDTH)
    decay = jnp.stack([pw_re[n].reshape(1, S5_STATE_WIDTH), pw_im[n].reshape(1, S5_STATE_WIDTH)])
    return (jnp.concatenate([t_mat, w_in], axis=1).astype(MXU_DTYPE), w_out.astype(MXU_DTYPE), decay)


def _s5_scan_body(u_ref, tw_ref, wout_ref, decay_ref, y_ref, h_ref, s_ref, hs_ref, *, tm):
    f32 = jnp.float32
    sw = S5_STATE_WIDTH

    @pl.when(pl.program_id(1) == 0)
    def _():
        h_ref[...] = jnp.zeros_like(h_ref)

    r = jnp.dot(u_ref[0].astype(MXU_DTYPE), tw_ref[...], preferred_element_type=f32)
    s_ref[...] = r[:, S5_CHUNK_WIDTH:]
    d_re = decay_ref[0]
    d_im = decay_ref[1]

    def eight_chunks(k, carry):
        h_re, h_im = carry
        r0 = pl.multiple_of(k * 8, 8)
        inc = s_ref[pl.ds(r0, 8), :]
        rows_re, rows_im = [], []
        for t in range(8):
            rows_re.append(h_re)
            rows_im.append(h_im)
            h_re, h_im = (d_re * h_re - d_im * h_im + inc[t:t + 1, :sw],
                          d_re * h_im + d_im * h_re + inc[t:t + 1, sw:])
        hs_ref[pl.ds(r0, 8), :] = jnp.concatenate(
            [jnp.concatenate(rows_re, axis=0), jnp.concatenate(rows_im, axis=0)], axis=1)
        return h_re, h_im

    h_re, h_im = lax.fori_loop(0, tm // 8, eight_chunks, (h_ref[0:1, :], h_ref[1:2, :]))
    h_ref[0:1, :] = h_re
    h_ref[1:2, :] = h_im
    y_ref[0] = r[:, :S5_CHUNK_WIDTH] + jnp.dot(hs_ref[...].astype(MXU_DTYPE), wout_ref[...],
                                                preferred_element_type=f32)


def s5_scan(u, tw, w_out, decay, *, tm=256):
    bsz, seq, _ = u.shape
    n_chunks = seq // S5_CHUNK
    tm = min(tm, n_chunks)
    assert seq % S5_CHUNK == 0 and n_chunks % tm == 0 and tm % 8 == 0
    once = pl.Buffered(1)
    y = pl.pallas_call(
        functools.partial(_s5_scan_body, tm=tm),
        grid=(bsz, n_chunks // tm),
        in_specs=[pl.BlockSpec((1, tm, S5_CHUNK_WIDTH), lambda b, i: (b, i, 0)),
                  pl.BlockSpec(tw.shape, lambda b, i: (0, 0), pipeline_mode=once),
                  pl.BlockSpec(w_out.shape, lambda b, i: (0, 0), pipeline_mode=once),
                  pl.BlockSpec(decay.shape, lambda b, i: (0, 0, 0), pipeline_mode=once)],
        out_specs=pl.BlockSpec((1, tm, S5_CHUNK_WIDTH), lambda b, i: (b, i, 0)),
        out_shape=jax.ShapeDtypeStruct((bsz, n_chunks, S5_CHUNK_WIDTH), jnp.float32),
        scratch_shapes=[pltpu.VMEM((2, S5_STATE_WIDTH), jnp.float32),
                        pltpu.VMEM((tm, 2 * S5_STATE_WIDTH), jnp.float32),
                        pltpu.VMEM((tm, 2 * S5_STATE_WIDTH), jnp.float32)],
        compiler_params=pltpu.CompilerParams(
            dimension_semantics=("parallel", "arbitrary"), vmem_limit_bytes=VMEM_LIMIT_BYTES),
        name="s5_scan",
    )(u.reshape(bsz, n_chunks, S5_CHUNK_WIDTH), tw, w_out, decay)
    return y.reshape(bsz, seq, SSM_WIDTH)


def _s5_out_body(y_ref, u_ref, d_ref, wglu_ref, bglu_ref, wout_ref, o_ref):
    f32 = jnp.float32
    y = jax.nn.gelu(y_ref[...] + d_ref[...] * u_ref[...])
    gate = jnp.dot(y.astype(MXU_DTYPE), wglu_ref[...], preferred_element_type=f32) + bglu_ref[...]
    y = y * jax.nn.sigmoid(gate)
    o_ref[...] = jnp.dot(y.astype(MXU_DTYPE), wout_ref[...], preferred_element_type=f32)


def s5_output(y, u, d_skip, w_glu, b_glu, w_out, *, tm=1024):
    m, c = y.shape
    tm = min(tm, m)
    row = lambda i: (i, 0)
    full = lambda i: (0, 0)
    return pl.pallas_call(
        _s5_out_body,
        grid=(m // tm,),
        in_specs=[pl.BlockSpec((tm, c), row), pl.BlockSpec((tm, c), row), pl.BlockSpec((1, c), full),
                  pl.BlockSpec((c, c), full), pl.BlockSpec((1, c), full), pl.BlockSpec((c, D_MODEL), full)],
        out_specs=pl.BlockSpec((tm, D_MODEL), row),
        out_shape=jax.ShapeDtypeStruct((m, D_MODEL), jnp.float32),
        compiler_params=pltpu.CompilerParams(
            dimension_semantics=("parallel",), vmem_limit_bytes=VMEM_LIMIT_BYTES),
        name="s5_output",
    )(y, u, d_skip.reshape(1, c), w_glu.astype(MXU_DTYPE), b_glu.reshape(1, c), w_out.astype(MXU_DTYPE))


def s5_branch(u, log_dt, lam_re, lam_im, b_re, b_im, c_re, c_im, d_skip, w_glu, b_glu, w_out):
    bsz, seq, _ = u.shape
    tw, w_state_out, decay = s5_operators(log_dt, lam_re, lam_im, b_re, b_im, c_re, c_im)
    y = s5_scan(u, tw, w_state_out, decay)
    out = s5_output(y.reshape(bsz * seq, SSM_WIDTH), u.reshape(bsz * seq, SSM_WIDTH),
                    d_skip, w_glu, b_glu, w_out)
    return out.reshape(bsz, seq, D_MODEL)


HG_CHUNK = 128
HG_LEVELS = 7
HG_KDIM = HGRN_HEADS * HGRN_DK


def _hgrn_segment_sums():
    c = HG_CHUNK
    t = np.arange(c)[:, None]
    u = np.arange(c)[None, :]
    blocks = []
    for lvl in range(1, HG_LEVELS + 1):
        m = (t >> lvl << lvl) + (1 << (lvl - 1)) - 1
        right = ((t >> (lvl - 1)) & 1) == 1
        blocks.append(np.where(right, (u > m) & (u <= t), (u > t) & (u <= m)))
    blocks.append(u <= t)
    blocks.append(u > t)
    return np.concatenate(blocks, axis=0).astype(np.float32)


def _hgrn_body(q_ref, z_ref, v_ref, g_ref, seg_ref, lb_ref, ng_ref, hmean_ref, wout_ref, o_ref, st_ref):
    f32 = jnp.float32
    c = HG_CHUNK

    @pl.when(pl.program_id(1) == 0)
    def _():
        st_ref[...] = jnp.zeros_like(st_ref)

    q = q_ref[0]
    z = z_ref[0]
    v = v_ref[0]
    lb = lb_ref[...]
    f = lb + (1.0 - lb) * jax.nn.sigmoid(z)
    logf = jnp.log(jnp.maximum(f, F_MIN))
    kin = (1.0 - lb) * jax.nn.sigmoid(-z)

    p1 = logf.astype(MXU_DTYPE)
    r1 = logf - p1.astype(f32)
    p2 = r1.astype(MXU_DTYPE)
    p3 = (r1 - p2.astype(f32)).astype(MXU_DTYPE)
    seg = seg_ref[...]
    sums = (jnp.dot(seg, p1, preferred_element_type=f32) + jnp.dot(seg, p2, preferred_element_type=f32)
            + jnp.dot(seg, p3, preferred_element_type=f32))

    lane_head = lax.broadcasted_iota(jnp.int32, (c, HG_KDIM), 1) // HGRN_DK
    tok = lax.broadcasted_iota(jnp.int32, (c, HG_KDIM), 0)
    row_t = lax.broadcasted_iota(jnp.int32, (HGRN_HEADS * c, c), 0) % c
    col_s = lax.broadcasted_iota(jnp.int32, (HGRN_HEADS * c, c), 1)

    def per_head_rows(x):
        return jnp.concatenate([jnp.where(lane_head == h, x, 0.0) for h in range(HGRN_HEADS)],
                               axis=0).astype(MXU_DTYPE)

    def scores(ql, kl):
        return lax.dot_general(per_head_rows(ql), kl.astype(MXU_DTYPE), (((1,), (1,)), ((), ())),
                               preferred_element_type=f32)

    att = jnp.where(row_t == col_s, scores(q, kin), 0.0)
    for lvl in range(1, HG_LEVELS + 1):
        decay = jnp.exp(sums[(lvl - 1) * c:lvl * c])
        right = ((tok >> (lvl - 1)) & 1) == 1
        a = scores(jnp.where(right, q * decay, 0.0), jnp.where(right, 0.0, kin * decay))
        att = att + jnp.where((row_t >> lvl) == (col_s >> lvl), a, 0.0)

    b = sums[HG_LEVELS * c:(HG_LEVELS + 1) * c]
    tail = sums[(HG_LEVELS + 1) * c:(HG_LEVELS + 2) * c]
    v_m = v.astype(MXU_DTYPE)
    st = st_ref[...]
    o = lax.dot_general((q * jnp.exp(b)).astype(MXU_DTYPE), st.astype(MXU_DTYPE),
                        (((1,), (1,)), ((), ())), preferred_element_type=f32)
    for h in range(HGRN_HEADS):
        o_h = jnp.dot(att[h * c:(h + 1) * c].astype(MXU_DTYPE), v_m, preferred_element_type=f32)
        o = o + jnp.where(lane_head == h, o_h, 0.0)

    kv = jnp.dot(v.T.astype(MXU_DTYPE), (kin * jnp.exp(tail)).astype(MXU_DTYPE), preferred_element_type=f32)
    sr = lax.broadcasted_iota(jnp.int32, st.shape, 0) // HGRN_DV
    sc = lax.broadcasted_iota(jnp.int32, st.shape, 1) // HGRN_DK
    st_ref[...] = st * jnp.exp(b[c - 1:c, :]) + jnp.where(sr == sc, kv, 0.0)

    o2 = o * o
    o2_hi = o2.astype(MXU_DTYPE)
    o2_lo = (o2 - o2_hi.astype(f32)).astype(MXU_DTYPE)
    ms = (jnp.dot(o2_hi, hmean_ref[...], preferred_element_type=f32)
          + jnp.dot(o2_lo, hmean_ref[...], preferred_element_type=f32))
    g = g_ref[0]
    out = o * lax.rsqrt(ms + RMS_EPS) * ng_ref[...] * (g * jax.nn.sigmoid(g))
    o_ref[0] = jnp.dot(out.astype(MXU_DTYPE), wout_ref[...], preferred_element_type=f32)


def hgrn2_branch(q, f_logit, i_in, g_out, lower_bound, norm_g, w_out):
    bsz, seq, _ = q.shape
    assert seq % HG_CHUNK == 0 and HGRN_DK == HGRN_DV
    seg = jnp.asarray(_hgrn_segment_sums(), MXU_DTYPE)
    head_mean = jnp.asarray(np.kron(np.eye(HGRN_HEADS), np.full((HGRN_DV, HGRN_DV), 1.0 / HGRN_DV)), MXU_DTYPE)
    tok = lambda b, i: (b, i, 0)
    full = lambda b, i: (0, 0)
    blk = pl.BlockSpec((1, HG_CHUNK, HG_KDIM), tok)
    return pl.pallas_call(
        _hgrn_body,
        grid=(bsz, seq // HG_CHUNK),
        in_specs=[blk, blk, blk, blk,
                  pl.BlockSpec(seg.shape, full), pl.BlockSpec((1, HG_KDIM), full),
                  pl.BlockSpec((1, HGRN_WIDTH), full), pl.BlockSpec(head_mean.shape, full),
                  pl.BlockSpec((HGRN_WIDTH, D_MODEL), full)],
        out_specs=pl.BlockSpec((1, HG_CHUNK, D_MODEL), tok),
        out_shape=jax.ShapeDtypeStruct((bsz, seq, D_MODEL), jnp.float32),
        scratch_shapes=[pltpu.VMEM((HGRN_WIDTH, HG_KDIM), jnp.float32)],
        compiler_params=pltpu.CompilerParams(
            dimension_semantics=("parallel", "arbitrary"), vmem_limit_bytes=VMEM_LIMIT_BYTES),
        name="hgrn2",
    )(q, f_logit, i_in, g_out, seg, lower_bound.reshape(1, HG_KDIM).astype(jnp.float32),
      jnp.tile(norm_g.astype(jnp.float32), HGRN_HEADS).reshape(1, HGRN_WIDTH), head_mean,
      w_out.astype(MXU_DTYPE))


def _norm_matmul_body(x_ref, g_ref, w_ref, o_ref):
    x = x_ref[...]
    xn = x * lax.rsqrt(jnp.mean(x * x, axis=-1, keepdims=True) + RMS_EPS) * g_ref[...]
    o_ref[...] = jnp.dot(xn.astype(MXU_DTYPE), w_ref[...],
                         preferred_element_type=jnp.float32).astype(o_ref.dtype)


def norm_matmul(x, g, w, *, tm=1024, out_dtype=None):
    m, k = x.shape
    _, n = w.shape
    tm = min(tm, m)
    return pl.pallas_call(
        _norm_matmul_body,
        grid=(m // tm,),
        in_specs=[pl.BlockSpec((tm, k), lambda i: (i, 0)), pl.BlockSpec((1, k), lambda i: (0, 0)),
                  pl.BlockSpec((k, n), lambda i: (0, 0))],
        out_specs=pl.BlockSpec((tm, n), lambda i: (i, 0)),
        out_shape=jax.ShapeDtypeStruct((m, n), out_dtype or MXU_DTYPE),
        compiler_params=pltpu.CompilerParams(
            dimension_semantics=("parallel",), vmem_limit_bytes=VMEM_LIMIT_BYTES),
        name="norm_matmul",
    )(x, g.reshape(1, k), w)


_INT_MIN = -2 ** 31
_MASK_KEY = int(np.float32(MASK_VALUE).view(np.int32)) ^ 0x7FFFFFFF
_SEARCH_ROWS = 64
_SEARCH_LANES = 512


def _dsa_body(qi_ref, kj_ref, qidx_ref, w_ref, kidt_ref, q_ref, kt_ref, v_ref, tri_ref, wout_ref,
              o_ref, keys_ref, thr_ref, need_ref, carry_ref, m_ref, acc_ref, *, tq, tk, seq, n_sel):
    f32 = jnp.float32
    p_id = pl.program_id(1)
    i = qi_ref[p_id]
    j = kj_ref[p_id]
    row_local = lax.broadcasted_iota(jnp.int32, (tq, tk), 0)
    col_local = lax.broadcasted_iota(jnp.int32, (tq, tk), 1)

    @pl.when(j == 0)
    def _select():
        def score_block(jj, carry):
            off = pl.multiple_of(jj * tk, tk)
            kb = kidt_ref[0, :, pl.ds(off, tk)]
            sc = jnp.zeros((tq, tk), f32)
            for h in range(IDX_HEADS):
                s = jnp.dot(qidx_ref[0, h], kb, preferred_element_type=f32)
                sc = sc + jnp.maximum(s, 0.0) * w_ref[0, :, h:h + 1]
            causal = (col_local + jj * tk) <= (row_local + i * tq)
            sc = jnp.where(causal, sc, MASK_VALUE)
            sc = jnp.where(sc == 0.0, 0.0, sc)
            bits = lax.bitcast_convert_type(sc, jnp.int32)
            keys_ref[:, pl.ds(off, tk)] = jnp.where(bits < 0, bits ^ 0x7FFFFFFF, bits)
            return carry

        lax.fori_loop(0, i + 1, score_block, 0)

        n_proc = (i + 1) * tk
        n_iter = n_proc // _SEARCH_LANES
        n_masked_tail = (seq - n_proc).astype(f32)

        def row_group(r, carry):
            r0 = pl.multiple_of(r * _SEARCH_ROWS, _SEARCH_ROWS)

            def count_ge(cand):
                cand_b = jnp.broadcast_to(cand, (_SEARCH_ROWS, LANE))

                def chunk(c, acc):
                    base = pl.multiple_of(c * _SEARCH_LANES, _SEARCH_LANES)
                    for u in range(_SEARCH_LANES // LANE):
                        kk = keys_ref[pl.ds(r0, _SEARCH_ROWS), pl.ds(base + u * LANE, LANE)]
                        acc = acc + jnp.where(kk >= cand_b, 1.0, 0.0)
                    return acc

                acc = lax.fori_loop(0, n_iter, chunk, jnp.zeros((_SEARCH_ROWS, LANE), f32))
                cnt = jnp.sum(acc, axis=1, keepdims=True)
                return cnt + jnp.where(cand <= _MASK_KEY, n_masked_tail, 0.0)

            zero = jnp.zeros((_SEARCH_ROWS, 1), jnp.int32)
            v0 = jnp.where(count_ge(zero) >= n_sel, zero, zero + _INT_MIN)

            def bit_step(b, v):
                cand = v | jnp.left_shift(jnp.int32(1), 30 - b)
                return jnp.where(count_ge(cand) >= n_sel, cand, v)

            v = lax.fori_loop(0, 31, bit_step, v0)
            thr_ref[pl.ds(r0, _SEARCH_ROWS), :] = v
            need_ref[pl.ds(r0, _SEARCH_ROWS), :] = n_sel - count_ge(v + 1)
            return carry

        lax.fori_loop(0, tq // _SEARCH_ROWS, row_group, 0)
        carry_ref[...] = jnp.zeros_like(carry_ref)
        m_ref[...] = jnp.full_like(m_ref, MASK_VALUE)
        acc_ref[...] = jnp.zeros_like(acc_ref)

    keys_blk = keys_ref[:, pl.ds(pl.multiple_of(j * tk, tk), tk)]
    thr = thr_ref[...]
    eq = keys_blk == thr
    tie_rank = carry_ref[...] + jnp.dot(jnp.where(eq, 1.0, 0.0).astype(MXU_DTYPE), tri_ref[...],
                                         preferred_element_type=f32)
    carry_ref[...] = tie_rank[:, tk - 1:tk]
    sel = (keys_blk > thr) | (eq & (tie_rank <= need_ref[...]))
    rel = (row_local + i * tq) - (col_local + j * tk)
    mask = sel & (rel >= 0)
    dist = rel.astype(f32)
    for h in range(ATT_HEADS):
        slope = 2.0 ** (-8.0 * (h + 1) / ATT_HEADS)
        s = jnp.dot(q_ref[0, h], kt_ref[0, h], preferred_element_type=f32) - slope * dist
        s = jnp.where(mask, s, MASK_VALUE)
        m_old = m_ref[h]
        m_new = jnp.maximum(m_old, jnp.max(s, axis=1, keepdims=True))
        p = jnp.exp(s - m_new)
        acc_ref[h] = jnp.exp(m_old - m_new) * acc_ref[h] + jnp.dot(
            p.astype(MXU_DTYPE), v_ref[0, h], preferred_element_type=f32)
        m_ref[h] = m_new

    @pl.when(j == i)
    def _finish():
        y = jnp.zeros((tq, D_MODEL), f32)
        for h in range(ATT_HEADS):
            a = acc_ref[h]
            o_h = a[:, :ATT_HEAD_DIM] / a[:, ATT_HEAD_DIM:ATT_HEAD_DIM + 1]
            y = y + jnp.dot(o_h.astype(MXU_DTYPE), wout_ref[h], preferred_element_type=f32)
        o_ref[0] = y


def dsa_attention(q_idx, w, kid_t, q, k_t, v_aug, w_out, *, tq=512):
    bsz, _, seq, _ = q.shape
    tq = min(tq, seq)
    tk = tq
    n_sel = min(TOPK_MAX, seq // 4)
    nq = seq // tq
    assert seq % tq == 0 and tk % _SEARCH_LANES == 0 and tq % _SEARCH_ROWS == 0
    pairs = [(a, b) for a in range(nq) for b in range(a + 1)]
    qi = jnp.asarray([a for a, _ in pairs], jnp.int32)
    kj = jnp.asarray([b for _, b in pairs], jnp.int32)
    tri = jnp.triu(jnp.ones((tk, tk), MXU_DTYPE))
    grid_spec = pltpu.PrefetchScalarGridSpec(
        num_scalar_prefetch=2,
        grid=(bsz, len(pairs)),
        in_specs=[
            pl.BlockSpec((1, IDX_HEADS, tq, IDX_DIM), lambda b, p, qi, kj: (b, 0, qi[p], 0)),
            pl.BlockSpec((1, tq, IDX_HEADS), lambda b, p, qi, kj: (b, qi[p], 0)),
            pl.BlockSpec((1, IDX_DIM, seq), lambda b, p, qi, kj: (b, 0, 0)),
            pl.BlockSpec((1, ATT_HEADS, tq, ATT_HEAD_DIM), lambda b, p, qi, kj: (b, 0, qi[p], 0)),
            pl.BlockSpec((1, ATT_HEADS, ATT_HEAD_DIM, tk), lambda b, p, qi, kj: (b, 0, 0, kj[p])),
            pl.BlockSpec((1, ATT_HEADS, tk, LANE), lambda b, p, qi, kj: (b, 0, kj[p], 0)),
            pl.BlockSpec((tk, tk), lambda b, p, qi, kj: (0, 0)),
            pl.BlockSpec((ATT_HEADS, ATT_HEAD_DIM, D_MODEL), lambda b, p, qi, kj: (0, 0, 0)),
        ],
        out_specs=pl.BlockSpec((1, tq, D_MODEL), lambda b, p, qi, kj: (b, qi[p], 0)),
        scratch_shapes=[
            pltpu.VMEM((tq, seq), jnp.int32),
            pltpu.VMEM((tq, 1), jnp.int32),
            pltpu.VMEM((tq, 1), jnp.float32),
            pltpu.VMEM((tq, 1), jnp.float32),
            pltpu.VMEM((ATT_HEADS, tq, 1), jnp.float32),
            pltpu.VMEM((ATT_HEADS, tq, LANE), jnp.float32),
        ])
    return pl.pallas_call(
        functools.partial(_dsa_body, tq=tq, tk=tk, seq=seq, n_sel=n_sel),
        grid_spec=grid_spec,
        out_shape=jax.ShapeDtypeStruct((bsz, seq, D_MODEL), jnp.float32),
        compiler_params=pltpu.CompilerParams(
            dimension_semantics=("parallel", "arbitrary"), vmem_limit_bytes=VMEM_LIMIT_BYTES),
        name="dsa_attention",
    )(qi, kj, q_idx, w, kid_t, q, k_t, v_aug, tri, w_out)


def dsa_branch(c_q, c_kv, k_idx, w_idx, q_norm_g, kv_norm_g, w_uq, w_qidx, w_ukv, w_out):
    bsz, seq, _ = c_q.shape
    m = bsz * seq
    w_q = jnp.concatenate([w_uq * ATT_HEAD_DIM ** -0.5, w_qidx * IDX_DIM ** -0.5], axis=1).astype(MXU_DTYPE)
    qq = norm_matmul(c_q.reshape(m, ATT_Q_RANK), q_norm_g, w_q)
    kv = norm_matmul(c_kv.reshape(m, ATT_KV_RANK), kv_norm_g, w_ukv.astype(MXU_DTYPE))
    q = qq[:, :ATT_WIDTH].reshape(bsz, seq, ATT_HEADS, ATT_HEAD_DIM).transpose(0, 2, 1, 3)
    q_idx = qq[:, ATT_WIDTH:].reshape(bsz, seq, IDX_HEADS, IDX_DIM).transpose(0, 2, 1, 3)
    k_t = kv[:, :ATT_WIDTH].reshape(bsz, seq, ATT_HEADS, ATT_HEAD_DIM).transpose(0, 2, 3, 1)
    v = kv[:, ATT_WIDTH:].reshape(bsz, seq, ATT_HEADS, ATT_HEAD_DIM).transpose(0, 2, 1, 3)
    ones = jnp.ones((bsz, ATT_HEADS, seq, 1), MXU_DTYPE)
    v_aug = jnp.concatenate([v, ones, jnp.zeros((bsz, ATT_HEADS, seq, LANE - ATT_HEAD_DIM - 1), MXU_DTYPE)],
                            axis=-1)
    kid_t = k_idx.astype(MXU_DTYPE).transpose(0, 2, 1)
    w = w_idx.astype(jnp.float32) * IDX_HEADS ** -0.5
    return dsa_attention(q_idx, w, kid_t, q, k_t, v_aug,
                         w_out.astype(MXU_DTYPE).reshape(ATT_HEADS, ATT_HEAD_DIM, D_MODEL))


def router_gates(h, router):
    logits = (h @ router).astype(jnp.float32)
    top_val, top_idx = lax.top_k(logits, TOP_K)
    top_w = jax.nn.softmax(top_val, axis=-1)
    onehot = (top_idx[..., None] == jnp.arange(N_EXPERTS)[None, None, :])
    gates = jnp.sum(jnp.where(onehot, top_w[..., None], 0.0), axis=1)
    return jnp.pad(gates, ((0, 0), (0, LANE - N_EXPERTS)))


def kernel(x, w_in, ssm_log_dt, ssm_lambda_re, ssm_lambda_im, ssm_b_re, ssm_b_im, ssm_c_re, ssm_c_im,
           ssm_d, ssm_w_glu, ssm_b_glu, ssm_w_out, hgrn_lb_logits, hgrn_norm_g, hgrn_w_out,
           attn_q_norm_g, attn_kv_norm_g, attn_w_uq, attn_w_qidx, attn_w_ukv, attn_w_out, w_o,
           ln_g, ln_b, ffn_w_gate, ffn_w_up, ffn_w_down, moe_router, moe_w_gate, moe_w_up, moe_w_down):
    bsz, seq, d = x.shape
    m = bsz * seq
    bf16 = jnp.bfloat16
    split_at = np.cumsum(IN_SPLITS)[:-1].tolist()
    lb_soft = jax.nn.softmax(hgrn_lb_logits.astype(jnp.float32), axis=0)
    lower_bounds = jnp.concatenate([jnp.zeros_like(lb_soft[:1]), jnp.cumsum(lb_soft[1:], axis=0)], axis=0)
    n_in_pad = _round_up(N_IN, 512)
    h = x.reshape(m, d)
    for l in range(DEPTH):
        w_in_l = jnp.pad(w_in[l], ((0, 0), (0, n_in_pad - N_IN))).astype(bf16)
        proj = matmul(h, w_in_l)[:, :N_IN].reshape(bsz, seq, N_IN)
        u, hq, hf, hi, hg, cq, ckv, kidx, widx, gates = jnp.split(proj, split_at, axis=-1)
        y_ssm = s5_branch(u, ssm_log_dt[l], ssm_lambda_re[l], ssm_lambda_im[l], ssm_b_re[l], ssm_b_im[l],
                          ssm_c_re[l], ssm_c_im[l], ssm_d[l], ssm_w_glu[l], ssm_b_glu[l], ssm_w_out[l])
        y_hg = hgrn2_branch(hq, hf, hi, hg, lower_bounds[l], hgrn_norm_g[l], hgrn_w_out[l])
        y_att = dsa_branch(cq, ckv, kidx, widx, attn_q_norm_g[l], attn_kv_norm_g[l],
                           attn_w_uq[l], attn_w_qidx[l], attn_w_ukv[l], attn_w_out[l])
        h = merge_project_norm(h, y_ssm.reshape(m, d), y_hg.reshape(m, d), y_att.reshape(m, d),
                               gates.reshape(m, N_BRANCHES * d), w_o[l].astype(bf16),
                               ln_g[l, 0], ln_b[l, 0])
        if l % 2 == 0:
            ones = jnp.ones((m, LANE), jnp.float32)
            h = swiglu_experts_norm(h, ones, ffn_w_gate[l // 2][None].astype(bf16),
                                    ffn_w_up[l // 2][None].astype(bf16),
                                    ffn_w_down[l // 2][None].astype(bf16),
                                    ln_g[l, 1], ln_b[l, 1], tf=256)
        else:
            gate_w = router_gates(h, moe_router[l // 2])
            h = swiglu_experts_norm(h, gate_w, moe_w_gate[l // 2].astype(bf16),
                                    moe_w_up[l // 2].astype(bf16), moe_w_down[l // 2].astype(bf16),
                                    ln_g[l, 1], ln_b[l, 1], tf=512)
    return h.reshape(bsz, seq, d)
```

```python
import functools
import math

import jax
import jax.numpy as jnp
import numpy as np
from jax import lax
from jax.experimental import pallas as pl
from jax.experimental.pallas import tpu as pltpu

D_MODEL = 1024
DEPTH = 2
SSM_WIDTH = 256
SSM_GROUP = 16
SSM_GROUPS = SSM_WIDTH // SSM_GROUP
SSM_STATE = 64
HGRN_HEADS = 4
HGRN_DK = 64
HGRN_DV = 64
HGRN_WIDTH = HGRN_HEADS * HGRN_DV
HGRN_CHUNK = 64
ATT_HEADS = 8
ATT_HEAD_DIM = 64
ATT_WIDTH = ATT_HEADS * ATT_HEAD_DIM
ATT_Q_RANK = 256
ATT_KV_RANK = 128
IDX_HEADS = 4
IDX_DIM = 64
TOPK_MAX = 256
Q_BLOCK = 128
MASK_VALUE = -1e30
N_BRANCHES = 3
N_EXPERTS = 8
TOP_K = 2
DEEPNORM_ALPHA = (2 * DEPTH) ** 0.25
LN_EPS = 1e-5
RMS_EPS = 1e-6
F_MIN = 1e-12

IN_SPLITS = (SSM_WIDTH, HGRN_HEADS * HGRN_DK, HGRN_HEADS * HGRN_DK, HGRN_WIDTH, HGRN_WIDTH,
             ATT_Q_RANK, ATT_KV_RANK, IDX_DIM, IDX_HEADS, N_BRANCHES * D_MODEL)
N_IN = sum(IN_SPLITS)

VMEM_LIMIT_BYTES = 48 * 1024 * 1024
LANE = 128
MXU_DTYPE = jnp.bfloat16


def _round_up(n, m):
    return (n + m - 1) // m * m


def _matmul_body(x_ref, w_ref, o_ref):
    o_ref[...] = jnp.dot(x_ref[...].astype(MXU_DTYPE), w_ref[...],
                         preferred_element_type=jnp.float32).astype(o_ref.dtype)


def matmul(x, w, *, tm=512, tn=512, out_dtype=jnp.float32):
    m, k = x.shape
    _, n = w.shape
    tn = min(tn, n)
    assert m % tm == 0 and n % tn == 0
    return pl.pallas_call(
        _matmul_body,
        grid=(m // tm, n // tn),
        in_specs=[pl.BlockSpec((tm, k), lambda i, j: (i, 0)),
                  pl.BlockSpec((k, tn), lambda i, j: (0, j))],
        out_specs=pl.BlockSpec((tm, tn), lambda i, j: (i, j)),
        out_shape=jax.ShapeDtypeStruct((m, n), out_dtype),
        compiler_params=pltpu.CompilerParams(
            dimension_semantics=("parallel", "parallel"), vmem_limit_bytes=VMEM_LIMIT_BYTES),
        name="matmul",
    )(x, w)


def _layer_norm_rows(y, g, b):
    mu = jnp.mean(y, axis=-1, keepdims=True)
    yc = y - mu
    var = jnp.mean(yc * yc, axis=-1, keepdims=True)
    return yc * lax.rsqrt(var + LN_EPS) * g + b


def _merge_body(h_ref, ys_ref, yh_ref, ya_ref, gt_ref, wo_ref, g_ref, b_ref, o_ref):
    d = D_MODEL
    gates = jax.nn.sigmoid(gt_ref[...])
    mixed = (gates[:, 0:d] * ys_ref[...] + gates[:, d:2 * d] * yh_ref[...]
             + gates[:, 2 * d:3 * d] * ya_ref[...])
    mix_out = jnp.dot(mixed.astype(MXU_DTYPE), wo_ref[...], preferred_element_type=jnp.float32)
    o_ref[...] = _layer_norm_rows(DEEPNORM_ALPHA * h_ref[...] + mix_out, g_ref[...], b_ref[...])


def merge_project_norm(h, y_ssm, y_hg, y_att, gate_logits, w_o, ln_g, ln_b, *, tm=512):
    m, d = h.shape
    row = lambda i: (i, 0)
    full = lambda i: (0, 0)
    return pl.pallas_call(
        _merge_body,
        grid=(m // tm,),
        in_specs=[pl.BlockSpec((tm, d), row), pl.BlockSpec((tm, d), row), pl.BlockSpec((tm, d), row),
                  pl.BlockSpec((tm, d), row), pl.BlockSpec((tm, N_BRANCHES * d), row),
                  pl.BlockSpec((d, d), full), pl.BlockSpec((1, d), full), pl.BlockSpec((1, d), full)],
        out_specs=pl.BlockSpec((tm, d), row),
        out_shape=jax.ShapeDtypeStruct((m, d), jnp.float32),
        compiler_params=pltpu.CompilerParams(
            dimension_semantics=("parallel",), vmem_limit_bytes=VMEM_LIMIT_BYTES),
        name="merge_project_norm",
    )(h, y_ssm, y_hg, y_att, gate_logits, w_o, ln_g.reshape(1, d), ln_b.reshape(1, d))


def _ffn_body(h_ref, gate_ref, wg_ref, wu_ref, wd_ref, g_ref, b_ref, o_ref, acc_ref, *, n_experts):
    e = pl.program_id(1)
    f = pl.program_id(2)

    @pl.when((e == 0) & (f == 0))
    def _():
        acc_ref[...] = jnp.zeros_like(acc_ref)

    x = h_ref[...].astype(MXU_DTYPE)
    a = jnp.dot(x, wg_ref[0], preferred_element_type=jnp.float32)
    u = jnp.dot(x, wu_ref[0], preferred_element_type=jnp.float32)
    act = a * jax.nn.sigmoid(a) * u
    part = jnp.dot(act.astype(MXU_DTYPE), wd_ref[0], preferred_element_type=jnp.float32)
    if n_experts > 1:
        lane = lax.broadcasted_iota(jnp.int32, gate_ref.shape, 1)
        gate_e = jnp.sum(jnp.where(lane == e, gate_ref[...], 0.0), axis=-1, keepdims=True)
        part = gate_e * part
    acc_ref[...] += part

    @pl.when((e == pl.num_programs(1) - 1) & (f == pl.num_programs(2) - 1))
    def _():
        o_ref[...] = _layer_norm_rows(DEEPNORM_ALPHA * h_ref[...] + acc_ref[...], g_ref[...], b_ref[...])


def swiglu_experts_norm(h, gates, wg, wu, wd, ln_g, ln_b, *, tm=1024, tf=512):
    m, d = h.shape
    n_experts, _, f_dim = wg.shape
    assert f_dim % tf == 0 and m % tm == 0
    row = lambda i, e, f: (i, 0)
    full = lambda i, e, f: (0, 0)
    return pl.pallas_call(
        functools.partial(_ffn_body, n_experts=n_experts),
        grid=(m // tm, n_experts, f_dim // tf),
        in_specs=[pl.BlockSpec((tm, d), row), pl.BlockSpec((tm, LANE), row),
                  pl.BlockSpec((1, d, tf), lambda i, e, f: (e, 0, f)),
                  pl.BlockSpec((1, d, tf), lambda i, e, f: (e, 0, f)),
                  pl.BlockSpec((1, tf, d), lambda i, e, f: (e, f, 0)),
                  pl.BlockSpec((1, d), full), pl.BlockSpec((1, d), full)],
        out_specs=pl.BlockSpec((tm, d), row),
        out_shape=jax.ShapeDtypeStruct((m, d), jnp.float32),
        scratch_shapes=[pltpu.VMEM((tm, d), jnp.float32)],
        compiler_params=pltpu.CompilerParams(
            dimension_semantics=("parallel", "arbitrary", "arbitrary"),
            vmem_limit_bytes=VMEM_LIMIT_BYTES),
        name="swiglu_experts_norm",
    )(h, gates, wg, wu, wd, ln_g.reshape(1, d), ln_b.reshape(1, d))


def rms_norm(x, g):
    xf = x.astype(jnp.float32)
    return (xf * lax.rsqrt(jnp.mean(xf * xf, axis=-1, keepdims=True) + RMS_EPS) * g).astype(x.dtype)


def _ssm_combine(e1, e2):
    a1r, a1i, b1r, b1i = e1
    a2r, a2i, b2r, b2i = e2
    return (a2r * a1r - a2i * a1i, a2r * a1i + a2i * a1r,
            a2r * b1r - a2i * b1i + b2r, a2r * b1i + a2i * b1r + b2i)


S5_CHUNK = 8
S5_CHUNK_WIDTH = S5_CHUNK * SSM_WIDTH
S5_STATE_WIDTH = SSM_GROUPS * SSM_STATE


def s5_operators(log_dt, lam_re, lam_im, b_re, b_im, c_re, c_im):
    f32 = jnp.float32
    n = S5_CHUNK
    lre, lim = lam_re.astype(f32), lam_im.astype(f32)
    dt = jnp.exp(log_dt.astype(f32))[:, None]
    mag = jnp.exp(lre * dt)
    ang = lim * dt
    a_re, a_im = mag * jnp.cos(ang), mag * jnp.sin(ang)
    den = lre * lre + lim * lim
    coef_re = ((a_re - 1.0) * lre + a_im * lim) / den
    coef_im = (a_im * lre - (a_re - 1.0) * lim) / den
    br, bi = b_re.astype(f32), b_im.astype(f32)
    bb_re = coef_re[..., None] * br - coef_im[..., None] * bi
    bb_im = coef_re[..., None] * bi + coef_im[..., None] * br
    j = jnp.arange(n + 1, dtype=f32)[:, None, None]
    pw_re = jnp.exp(j * lre * dt) * jnp.cos(j * ang)
    pw_im = jnp.exp(j * lre * dt) * jnp.sin(j * ang)
    eye = jnp.eye(SSM_GROUPS, dtype=f32)
    cr, ci = c_re.astype(f32), c_im.astype(f32)
    ab_re = pw_re[..., None] * bb_re - pw_im[..., None] * bb_im
    ab_im = pw_re[..., None] * bb_im + pw_im[..., None] * bb_re
    kern = (jnp.einsum('gcp,jgpd->jgcd', cr, ab_re[:n]) - jnp.einsum('gcp,jgpd->jgcd', ci, ab_im[:n]))
    lag = jnp.arange(n)[None, :] - jnp.arange(n)[:, None]
    toep = jnp.where((lag >= 0)[:, :, None, None, None], kern[jnp.maximum(lag, 0)], 0.0)
    t_mat = jnp.einsum('abgcd,gh->agdbhc', toep, eye).reshape(S5_CHUNK_WIDTH, S5_CHUNK_WIDTH)
    v = jnp.stack([ab_re[:n][::-1], ab_im[:n][::-1]])
    w_in = jnp.einsum('rlgpc,gh->lhcrgp', v, eye).reshape(S5_CHUNK_WIDTH, 2 * S5_STATE_WIDTH)
    ar, ai = pw_re[1:], pw_im[1:]
    wo_re = jnp.einsum('gcp,lgp->gplc', cr, ar) - jnp.einsum('gcp,lgp->gplc', ci, ai)
    wo_im = -jnp.einsum('gcp,lgp->gplc', cr, ai) - jnp.einsum('gcp,lgp->gplc', ci, ar)
    w_out = jnp.einsum('rgplc,gh->rhplgc', jnp.stack([wo_re, wo_im]), eye)
    w_out = w_out.reshape(2 * S5_STATE_WIDTH, S5_CHUNK_WIDTH)
    decay = jnp.stack([pw_re[n].reshape(1, S5_STATE_WIDTH), pw_im[n].reshape(1, S5_STATE_WIDTH)])
    return (jnp.concatenate([t_mat, w_in], axis=1).astype(MXU_DTYPE), w_out.astype(MXU_DTYPE), decay)


def _s5_scan_body(u_ref, tw_ref, wout_ref, decay_ref, y_ref, h_ref, s_ref, hs_ref, *, tm):
    f32 = jnp.float32
    sw = S5_STATE_WIDTH

    @pl.when(pl.program_id(1) == 0)
    def _():
        h_ref[...] = jnp.zeros_like(h_ref)

    r = jnp.dot(u_ref[0].astype(MXU_DTYPE), tw_ref[...], preferred_element_type=f32)
    s_ref[...] = r[:, S5_CHUNK_WIDTH:]
    d_re = decay_ref[0]
    d_im = decay_ref[1]

    def eight_chunks(k, carry):
        h_re, h_im = carry
        r0 = pl.multiple_of(k * 8, 8)
        inc = s_ref[pl.ds(r0, 8), :]
        rows_re, rows_im = [], []
        for t in range(8):
            rows_re.append(h_re)
            rows_im.append(h_im)
            h_re, h_im = (d_re * h_re - d_im * h_im + inc[t:t + 1, :sw],
                          d_re * h_im + d_im * h_re + inc[t:t + 1, sw:])
        hs_ref[pl.ds(r0, 8), :] = jnp.concatenate(
            [jnp.concatenate(rows_re, axis=0), jnp.concatenate(rows_im, axis=0)], axis=1)
        return h_re, h_im

    h_re, h_im = lax.fori_loop(0, tm // 8, eight_chunks, (h_ref[0:1, :], h_ref[1:2, :]))
    h_ref[0:1, :] = h_re
    h_ref[1:2, :] = h_im
    y_ref[0] = r[:, :S5_CHUNK_WIDTH] + jnp.dot(hs_ref[...].astype(MXU_DTYPE), wout_ref[...],
                                                preferred_element_type=f32)


def s5_scan(u, tw, w_out, decay, *, tm=256):
    bsz, seq, _ = u.shape
    n_chunks = seq // S5_CHUNK
    tm = min(tm, n_chunks)
    assert seq % S5_CHUNK == 0 and n_chunks % tm == 0 and tm % 8 == 0
    once = pl.Buffered(1)
    y = pl.pallas_call(
        functools.partial(_s5_scan_body, tm=tm),
        grid=(bsz, n_chunks // tm),
        in_specs=[pl.BlockSpec((1, tm, S5_CHUNK_WIDTH), lambda b, i: (b, i, 0)),
                  pl.BlockSpec(tw.shape, lambda b, i: (0, 0), pipeline_mode=once),
                  pl.BlockSpec(w_out.shape, lambda b, i: (0, 0), pipeline_mode=once),
                  pl.BlockSpec(decay.shape, lambda b, i: (0, 0, 0), pipeline_mode=once)],
        out_specs=pl.BlockSpec((1, tm, S5_CHUNK_WIDTH), lambda b, i: (b, i, 0)),
        out_shape=jax.ShapeDtypeStruct((bsz, n_chunks, S5_CHUNK_WIDTH), jnp.float32),
        scratch_shapes=[pltpu.VMEM((2, S5_STATE_WIDTH), jnp.float32),
                        pltpu.VMEM((tm, 2 * S5_STATE_WIDTH), jnp.float32),
                        pltpu.VMEM((tm, 2 * S5_STATE_WIDTH), jnp.float32)],
        compiler_params=pltpu.CompilerParams(
            dimension_semantics=("parallel", "arbitrary"), vmem_limit_bytes=VMEM_LIMIT_BYTES),
        name="s5_scan",
    )(u.reshape(bsz, n_chunks, S5_CHUNK_WIDTH), tw, w_out, decay)
    return y.reshape(bsz, seq, SSM_WIDTH)


def _s5_out_body(y_ref, u_ref, d_ref, wglu_ref, bglu_ref, wout_ref, o_ref):
    f32 = jnp.float32
    y = jax.nn.gelu(y_ref[...] + d_ref[...] * u_ref[...])
    gate = jnp.dot(y.astype(MXU_DTYPE), wglu_ref[...], preferred_element_type=f32) + bglu_ref[...]
    y = y * jax.nn.sigmoid(gate)
    o_ref[...] = jnp.dot(y.astype(MXU_DTYPE), wout_ref[...], preferred_element_type=f32)


def s5_output(y, u, d_skip, w_glu, b_glu, w_out, *, tm=1024):
    m, c = y.shape
    tm = min(tm, m)
    row = lambda i: (i, 0)
    full = lambda i: (0, 0)
    return pl.pallas_call(
        _s5_out_body,
        grid=(m // tm,),
        in_specs=[pl.BlockSpec((tm, c), row), pl.BlockSpec((tm, c), row), pl.BlockSpec((1, c), full),
                  pl.BlockSpec((c, c), full), pl.BlockSpec((1, c), full), pl.BlockSpec((c, D_MODEL), full)],
        out_specs=pl.BlockSpec((tm, D_MODEL), row),
        out_shape=jax.ShapeDtypeStruct((m, D_MODEL), jnp.float32),
        compiler_params=pltpu.CompilerParams(
            dimension_semantics=("parallel",), vmem_limit_bytes=VMEM_LIMIT_BYTES),
        name="s5_output",
    )(y, u, d_skip.reshape(1, c), w_glu.astype(MXU_DTYPE), b_glu.reshape(1, c), w_out.astype(MXU_DTYPE))


def s5_branch(u, log_dt, lam_re, lam_im, b_re, b_im, c_re, c_im, d_skip, w_glu, b_glu, w_out):
    bsz, seq, _ = u.shape
    tw, w_state_out, decay = s5_operators(log_dt, lam_re, lam_im, b_re, b_im, c_re, c_im)
    y = s5_scan(u, tw, w_state_out, decay)
    out = s5_output(y.reshape(bsz * seq, SSM_WIDTH), u.reshape(bsz * seq, SSM_WIDTH),
                    d_skip, w_glu, b_glu, w_out)
    return out.reshape(bsz, seq, D_MODEL)


HG_CHUNK = 128
HG_LEVELS = 7
HG_KDIM = HGRN_HEADS * HGRN_DK


def _hgrn_segment_sums():
    c = HG_CHUNK
    t = np.arange(c)[:, None]
    u = np.arange(c)[None, :]
    blocks = []
    for lvl in range(1, HG_LEVELS + 1):
        m = (t >> lvl << lvl) + (1 << (lvl - 1)) - 1
        right = ((t >> (lvl - 1)) & 1) == 1
        blocks.append(np.where(right, (u > m) & (u <= t), (u > t) & (u <= m)))
    blocks.append(u <= t)
    blocks.append(u > t)
    return np.concatenate(blocks, axis=0).astype(np.float32)


def _hgrn_body(q_ref, z_ref, v_ref, g_ref, seg_ref, lb_ref, ng_ref, hmean_ref, wout_ref, o_ref, st_ref):
    f32 = jnp.float32
    c = HG_CHUNK

    @pl.when(pl.program_id(1) == 0)
    def _():
        st_ref[...] = jnp.zeros_like(st_ref)

    q = q_ref[0]
    z = z_ref[0]
    v = v_ref[0]
    lb = lb_ref[...]
    f = lb + (1.0 - lb) * jax.nn.sigmoid(z)
    logf = jnp.log(jnp.maximum(f, F_MIN))
    kin = (1.0 - lb) * jax.nn.sigmoid(-z)

    p1 = logf.astype(MXU_DTYPE)
    r1 = logf - p1.astype(f32)
    p2 = r1.astype(MXU_DTYPE)
    p3 = (r1 - p2.astype(f32)).astype(MXU_DTYPE)
    seg = seg_ref[...]
    sums = (jnp.dot(seg, p1, preferred_element_type=f32) + jnp.dot(seg, p2, preferred_element_type=f32)
            + jnp.dot(seg, p3, preferred_element_type=f32))

    lane_head = lax.broadcasted_iota(jnp.int32, (c, HG_KDIM), 1) // HGRN_DK
    tok = lax.broadcasted_iota(jnp.int32, (c, HG_KDIM), 0)
    row_t = lax.broadcasted_iota(jnp.int32, (HGRN_HEADS * c, c), 0) % c
    col_s = lax.broadcasted_iota(jnp.int32, (HGRN_HEADS * c, c), 1)

    def per_head_rows(x):
        return jnp.concatenate([jnp.where(lane_head == h, x, 0.0) for h in range(HGRN_HEADS)],
                               axis=0).astype(MXU_DTYPE)

    def scores(ql, kl):
        return lax.dot_general(per_head_rows(ql), kl.astype(MXU_DTYPE), (((1,), (1,)), ((), ())),
                               preferred_element_type=f32)

    att = jnp.where(row_t == col_s, scores(q, kin), 0.0)
    for lvl in range(1, HG_LEVELS + 1):
        decay = jnp.exp(sums[(lvl - 1) * c:lvl * c])
        right = ((tok >> (lvl - 1)) & 1) == 1
        a = scores(jnp.where(right, q * decay, 0.0), jnp.where(right, 0.0, kin * decay))
        att = att + jnp.where((row_t >> lvl) == (col_s >> lvl), a, 0.0)

    b = sums[HG_LEVELS * c:(HG_LEVELS + 1) * c]
    tail = sums[(HG_LEVELS + 1) * c:(HG_LEVELS + 2) * c]
    v_m = v.astype(MXU_DTYPE)
    st = st_ref[...]
    o = lax.dot_general((q * jnp.exp(b)).astype(MXU_DTYPE), st.astype(MXU_DTYPE),
                        (((1,), (1,)), ((), ())), preferred_element_type=f32)
    for h in range(HGRN_HEADS):
        o_h = jnp.dot(att[h * c:(h + 1) * c].astype(MXU_DTYPE), v_m, preferred_element_type=f32)
        o = o + jnp.where(lane_head == h, o_h, 0.0)

    kv = jnp.dot(v.T.astype(MXU_DTYPE), (kin * jnp.exp(tail)).astype(MXU_DTYPE), preferred_element_type=f32)
    sr = lax.broadcasted_iota(jnp.int32, st.shape, 0) // HGRN_DV
    sc = lax.broadcasted_iota(jnp.int32, st.shape, 1) // HGRN_DK
    st_ref[...] = st * jnp.exp(b[c - 1:c, :]) + jnp.where(sr == sc, kv, 0.0)

    o2 = o * o
    o2_hi = o2.astype(MXU_DTYPE)
    o2_lo = (o2 - o2_hi.astype(f32)).astype(MXU_DTYPE)
    ms = (jnp.dot(o2_hi, hmean_ref[...], preferred_element_type=f32)
          + jnp.dot(o2_lo, hmean_ref[...], preferred_element_type=f32))
    g = g_ref[0]
    out = o * lax.rsqrt(ms + RMS_EPS) * ng_ref[...] * (g * jax.nn.sigmoid(g))
    o_ref[0] = jnp.dot(out.astype(MXU_DTYPE), wout_ref[...], preferred_element_type=f32)


def hgrn2_branch(q, f_logit, i_in, g_out, lower_bound, norm_g, w_out):
    bsz, seq, _ = q.shape
    assert seq % HG_CHUNK == 0 and HGRN_DK == HGRN_DV
    seg = jnp.asarray(_hgrn_segment_sums(), MXU_DTYPE)
    head_mean = jnp.asarray(np.kron(np.eye(HGRN_HEADS), np.full((HGRN_DV, HGRN_DV), 1.0 / HGRN_DV)), MXU_DTYPE)
    tok = lambda b, i: (b, i, 0)
    full = lambda b, i: (0, 0)
    blk = pl.BlockSpec((1, HG_CHUNK, HG_KDIM), tok)
    return pl.pallas_call(
        _hgrn_body,
        grid=(bsz, seq // HG_CHUNK),
        in_specs=[blk, blk, blk, blk,
                  pl.BlockSpec(seg.shape, full), pl.BlockSpec((1, HG_KDIM), full),
                  pl.BlockSpec((1, HGRN_WIDTH), full), pl.BlockSpec(head_mean.shape, full),
                  pl.BlockSpec((HGRN_WIDTH, D_MODEL), full)],
        out_specs=pl.BlockSpec((1, HG_CHUNK, D_MODEL), tok),
        out_shape=jax.ShapeDtypeStruct((bsz, seq, D_MODEL), jnp.float32),
        scratch_shapes=[pltpu.VMEM((HGRN_WIDTH, HG_KDIM), jnp.float32)],
        compiler_params=pltpu.CompilerParams(
            dimension_semantics=("parallel", "arbitrary"), vmem_limit_bytes=VMEM_LIMIT_BYTES),
        name="hgrn2",
    )(q, f_logit, i_in, g_out, seg, lower_bound.reshape(1, HG_KDIM).astype(jnp.float32),
      jnp.tile(norm_g.astype(jnp.float32), HGRN_HEADS).reshape(1, HGRN_WIDTH), head_mean,
      w_out.astype(MXU_DTYPE))


def _norm_matmul_body(x_ref, g_ref, w_ref, o_ref):
    x = x_ref[...]
    xn = x * lax.rsqrt(jnp.mean(x * x, axis=-1, keepdims=True) + RMS_EPS) * g_ref[...]
    o_ref[...] = jnp.dot(xn.astype(MXU_DTYPE), w_ref[...],
                         preferred_element_type=jnp.float32).astype(o_ref.dtype)


def norm_matmul(x, g, w, *, tm=1024, out_dtype=None):
    m, k = x.shape
    _, n = w.shape
    tm = min(tm, m)
    return pl.pallas_call(
        _norm_matmul_body,
        grid=(m // tm,),
        in_specs=[pl.BlockSpec((tm, k), lambda i: (i, 0)), pl.BlockSpec((1, k), lambda i: (0, 0)),
                  pl.BlockSpec((k, n), lambda i: (0, 0))],
        out_specs=pl.BlockSpec((tm, n), lambda i: (i, 0)),
        out_shape=jax.ShapeDtypeStruct((m, n), out_dtype or MXU_DTYPE),
        compiler_params=pltpu.CompilerParams(
            dimension_semantics=("parallel",), vmem_limit_bytes=VMEM_LIMIT_BYTES),
        name="norm_matmul",
    )(x, g.reshape(1, k), w)


_INT_MIN = -2 ** 31
_MASK_KEY = int(np.float32(MASK_VALUE).view(np.int32)) ^ 0x7FFFFFFF
_SEARCH_ROWS = 128
_ATT_ROWS = 64
_SEARCH_LANES = 512


def _dsa_body(qi_ref, kj_ref, qidx_ref, w_ref, kidt_ref, q_ref, kt_ref, v_ref, tri_ref, wout_ref,
              o_ref, keys_ref, thr_ref, need_ref, carry_ref, m_ref, acc_ref, bias_ref, s_ref, p_ref,
              alpha_ref, *, tq, tk, seq, n_sel):
    f32 = jnp.float32
    p_id = pl.program_id(1)
    i = qi_ref[p_id]
    j = kj_ref[p_id]
    row_local = lax.broadcasted_iota(jnp.int32, (tq, tk), 0)
    col_local = lax.broadcasted_iota(jnp.int32, (tq, tk), 1)

    @pl.when(j == 0)
    def _select():
        def score_block(jj, carry):
            off = pl.multiple_of(jj * tk, tk)
            kb = kidt_ref[0, :, pl.ds(off, tk)]
            sc = jnp.zeros((tq, tk), f32)
            for h in range(IDX_HEADS):
                s = jnp.dot(qidx_ref[0, h], kb, preferred_element_type=f32)
                sc = sc + jnp.maximum(s, 0.0) * w_ref[0, :, h:h + 1]
            causal = (col_local + jj * tk) <= (row_local + i * tq)
            sc = jnp.where(causal, sc, MASK_VALUE)
            sc = jnp.where(sc == 0.0, 0.0, sc)
            bits = lax.bitcast_convert_type(sc, jnp.int32)
            keys_ref[:, pl.ds(off, tk)] = jnp.where(bits < 0, bits ^ 0x7FFFFFFF, bits)
            return carry

        lax.fori_loop(0, i + 1, score_block, 0)

        n_proc = (i + 1) * tk
        n_iter = n_proc // _SEARCH_LANES
        n_masked_tail = (seq - n_proc).astype(f32)

        def row_group(r, carry):
            r0 = pl.multiple_of(r * _SEARCH_ROWS, _SEARCH_ROWS)

            def count_ge(cand):
                cand_b = jnp.broadcast_to(cand, (_SEARCH_ROWS, LANE))

                def chunk(c, acc):
                    base = pl.multiple_of(c * _SEARCH_LANES, _SEARCH_LANES)
                    for u in range(_SEARCH_LANES // LANE):
                        kk = keys_ref[pl.ds(r0, _SEARCH_ROWS), pl.ds(base + u * LANE, LANE)]
                        acc = acc + jnp.where(kk >= cand_b, 1.0, 0.0)
                    return acc

                acc = lax.fori_loop(0, n_iter, chunk, jnp.zeros((_SEARCH_ROWS, LANE), f32))
                cnt = jnp.sum(acc, axis=1, keepdims=True)
                return cnt + jnp.where(cand <= _MASK_KEY, n_masked_tail, 0.0)

            zero = jnp.zeros((_SEARCH_ROWS, 1), jnp.int32)
            v0 = jnp.where(count_ge(zero) >= n_sel, zero, zero + _INT_MIN)

            def bit_step(b, v):
                cand = v | jnp.left_shift(jnp.int32(1), 30 - b)
                return jnp.where(count_ge(cand) >= n_sel, cand, v)

            v = lax.fori_loop(0, 31, bit_step, v0)
            thr_ref[pl.ds(r0, _SEARCH_ROWS), :] = v
            need_ref[pl.ds(r0, _SEARCH_ROWS), :] = n_sel - count_ge(v + 1)
            return carry

        lax.fori_loop(0, tq // _SEARCH_ROWS, row_group, 0)
        carry_ref[...] = jnp.zeros_like(carry_ref)
        m_ref[...] = jnp.full_like(m_ref, MASK_VALUE)
        acc_ref[...] = jnp.zeros_like(acc_ref)

    keys_blk = keys_ref[:, pl.ds(pl.multiple_of(j * tk, tk), tk)]
    thr = thr_ref[...]
    eq = keys_blk == thr
    tie_rank = carry_ref[...] + jnp.dot(jnp.where(eq, 1.0, 0.0).astype(MXU_DTYPE), tri_ref[...],
                                         preferred_element_type=f32)
    carry_ref[...] = tie_rank[:, tk - 1:tk]
    sel = (keys_blk > thr) | (eq & (tie_rank <= need_ref[...]))
    causal = (row_local + i * tq) >= (col_local + j * tk)
    bias_ref[...] = jnp.where(sel & causal, 0.0, MASK_VALUE)

    def logits(h, slot):
        s_ref[slot] = jnp.dot(q_ref[0, h], kt_ref[0, h], preferred_element_type=f32)

    def softmax(h, slot):
        slope_tk = jnp.right_shift(tk, h + 1).astype(f32)
        for r in range(0, tq, _ATT_ROWS):
            rows = pl.ds(r, _ATT_ROWS)
            s = s_ref[slot, rows, :] + bias_ref[rows, :]
            m_old = m_ref[h, rows, :] - slope_tk
            m_new = jnp.maximum(m_old, jnp.max(s, axis=1, keepdims=True))
            p_ref[slot, rows, :] = jnp.exp(s - m_new).astype(MXU_DTYPE)
            alpha_ref[slot, rows, :] = jnp.exp(m_old - m_new)
            m_ref[h, rows, :] = m_new

    def weighted_values(h, slot):
        acc_ref[h] = alpha_ref[slot] * acc_ref[h] + jnp.dot(p_ref[slot], v_ref[0, h],
                                                             preferred_element_type=f32)

    for t in range(ATT_HEADS + 2):
        if t < ATT_HEADS:
            logits(t, t % 2)
        if 1 <= t <= ATT_HEADS:
            softmax(t - 1, (t - 1) % 2)
        if t >= 2:
            weighted_values(t - 2, t % 2)

    @pl.when(j == i)
    def _finish():
        heads = []
        for h in range(ATT_HEADS):
            a = acc_ref[h]
            heads.append((a[:, :ATT_HEAD_DIM] / a[:, ATT_HEAD_DIM:ATT_HEAD_DIM + 1]).astype(MXU_DTYPE))
        o_ref[0] = jnp.dot(jnp.concatenate(heads, axis=1), wout_ref[...], preferred_element_type=f32)


ATT_QK_DIM = 80


def _with_distance_columns(q, k_t, tq, tk):
    bsz, nh, seq, dh = q.shape
    assert 8 % nh == 0 and max(tq, tk) <= 512
    f32 = jnp.float32
    slopes = (2.0 ** (-8.0 * jnp.arange(1, nh + 1, dtype=f32) / nh))[:, None]
    t_loc = jnp.arange(seq) % tq
    s_loc = jnp.arange(seq) % tk
    t_even, t_odd = (t_loc // 2 * 2).astype(f32)[None], (t_loc % 2).astype(f32)[None]
    s_even, s_odd = (s_loc // 2 * 2).astype(f32), (s_loc % 2).astype(f32)
    one = jnp.ones((seq,), f32)
    q_cols = jnp.stack([-slopes * t_even, -slopes * t_odd, slopes * one, slopes * one], axis=-1)
    k_rows = jnp.stack([one, one, s_even, s_odd], axis=0)
    pad = ATT_QK_DIM - dh - 4
    q_aug = jnp.concatenate([q, jnp.broadcast_to(q_cols.astype(q.dtype), (bsz, nh, seq, 4)),
                             jnp.zeros((bsz, nh, seq, pad), q.dtype)], axis=-1)
    k_aug = jnp.concatenate([k_t, jnp.broadcast_to(k_rows.astype(q.dtype), (bsz, nh, 4, seq)),
                             jnp.zeros((bsz, nh, pad, seq), q.dtype)], axis=2)
    return q_aug, k_aug


def dsa_attention(q_idx, w, kid_t, q, k_t, v_aug, w_out, *, tq=512):
    bsz, _, seq, _ = q.shape
    tq = min(tq, seq)
    tk = tq
    n_sel = min(TOPK_MAX, seq // 4)
    nq = seq // tq
    assert seq % tq == 0 and tk % _SEARCH_LANES == 0 and tq % _SEARCH_ROWS == 0
    q, k_t = _with_distance_columns(q, k_t, tq, tk)
    pairs = [(a, b) for a in range(nq) for b in range(a + 1)]
    qi = jnp.asarray([a for a, _ in pairs], jnp.int32)
    kj = jnp.asarray([b for _, b in pairs], jnp.int32)
    tri = jnp.triu(jnp.ones((tk, tk), MXU_DTYPE))
    grid_spec = pltpu.PrefetchScalarGridSpec(
        num_scalar_prefetch=2,
        grid=(bsz, len(pairs)),
        in_specs=[
            pl.BlockSpec((1, IDX_HEADS, tq, IDX_DIM), lambda b, p, qi, kj: (b, 0, qi[p], 0)),
            pl.BlockSpec((1, tq, IDX_HEADS), lambda b, p, qi, kj: (b, qi[p], 0)),
            pl.BlockSpec((1, IDX_DIM, seq), lambda b, p, qi, kj: (b, 0, 0)),
            pl.BlockSpec((1, ATT_HEADS, tq, ATT_QK_DIM), lambda b, p, qi, kj: (b, 0, qi[p], 0)),
            pl.BlockSpec((1, ATT_HEADS, ATT_QK_DIM, tk), lambda b, p, qi, kj: (b, 0, 0, kj[p])),
            pl.BlockSpec((1, ATT_HEADS, tk, LANE), lambda b, p, qi, kj: (b, 0, kj[p], 0)),
            pl.BlockSpec((tk, tk), lambda b, p, qi, kj: (0, 0)),
            pl.BlockSpec((ATT_WIDTH, D_MODEL), lambda b, p, qi, kj: (0, 0)),
        ],
        out_specs=pl.BlockSpec((1, tq, D_MODEL), lambda b, p, qi, kj: (b, qi[p], 0)),
        scratch_shapes=[
            pltpu.VMEM((tq, seq), jnp.int32),
            pltpu.VMEM((tq, 1), jnp.int32),
            pltpu.VMEM((tq, 1), jnp.float32),
            pltpu.VMEM((tq, 1), jnp.float32),
            pltpu.VMEM((ATT_HEADS, tq, 1), jnp.float32),
            pltpu.VMEM((ATT_HEADS, tq, LANE), jnp.float32),
            pltpu.VMEM((tq, tk), jnp.float32),
            pltpu.VMEM((2, tq, tk), jnp.float32),
            pltpu.VMEM((2, tq, tk), MXU_DTYPE),
            pltpu.VMEM((2, tq, 1), jnp.float32),
        ])
    return pl.pallas_call(
        functools.partial(_dsa_body, tq=tq, tk=tk, seq=seq, n_sel=n_sel),
        grid_spec=grid_spec,
        out_shape=jax.ShapeDtypeStruct((bsz, seq, D_MODEL), jnp.float32),
        compiler_params=pltpu.CompilerParams(
            dimension_semantics=("parallel", "arbitrary"), vmem_limit_bytes=VMEM_LIMIT_BYTES),
        name="dsa_attention",
    )(qi, kj, q_idx, w, kid_t, q, k_t, v_aug, tri, w_out)


def dsa_branch(c_q, c_kv, k_idx, w_idx, q_norm_g, kv_norm_g, w_uq, w_qidx, w_ukv, w_out):
    bsz, seq, _ = c_q.shape
    m = bsz * seq
    w_q = jnp.concatenate([w_uq * ATT_HEAD_DIM ** -0.5, w_qidx * IDX_DIM ** -0.5], axis=1).astype(MXU_DTYPE)
    qq = norm_matmul(c_q.reshape(m, ATT_Q_RANK), q_norm_g, w_q)
    kv = norm_matmul(c_kv.reshape(m, ATT_KV_RANK), kv_norm_g, w_ukv.astype(MXU_DTYPE))
    q = qq[:, :ATT_WIDTH].reshape(bsz, seq, ATT_HEADS, ATT_HEAD_DIM).transpose(0, 2, 1, 3)
    q_idx = qq[:, ATT_WIDTH:].reshape(bsz, seq, IDX_HEADS, IDX_DIM).transpose(0, 2, 1, 3)
    k_t = kv[:, :ATT_WIDTH].reshape(bsz, seq, ATT_HEADS, ATT_HEAD_DIM).transpose(0, 2, 3, 1)
    v = kv[:, ATT_WIDTH:].reshape(bsz, seq, ATT_HEADS, ATT_HEAD_DIM).transpose(0, 2, 1, 3)
    ones = jnp.ones((bsz, ATT_HEADS, seq, 1), MXU_DTYPE)
    v_aug = jnp.concatenate([v, ones, jnp.zeros((bsz, ATT_HEADS, seq, LANE - ATT_HEAD_DIM - 1), MXU_DTYPE)],
                            axis=-1)
    kid_t = k_idx.astype(MXU_DTYPE).transpose(0, 2, 1)
    w = w_idx.astype(jnp.float32) * IDX_HEADS ** -0.5
    return dsa_attention(q_idx, w, kid_t, q, k_t, v_aug,
                         w_out.astype(MXU_DTYPE))


def router_gates(h, router):
    logits = (h @ router).astype(jnp.float32)
    top_val, top_idx = lax.top_k(logits, TOP_K)
    top_w = jax.nn.softmax(top_val, axis=-1)
    onehot = (top_idx[..., None] == jnp.arange(N_EXPERTS)[None, None, :])
    gates = jnp.sum(jnp.where(onehot, top_w[..., None], 0.0), axis=1)
    return jnp.pad(gates, ((0, 0), (0, LANE - N_EXPERTS)))


def kernel(x, w_in, ssm_log_dt, ssm_lambda_re, ssm_lambda_im, ssm_b_re, ssm_b_im, ssm_c_re, ssm_c_im,
           ssm_d, ssm_w_glu, ssm_b_glu, ssm_w_out, hgrn_lb_logits, hgrn_norm_g, hgrn_w_out,
           attn_q_norm_g, attn_kv_norm_g, attn_w_uq, attn_w_qidx, attn_w_ukv, attn_w_out, w_o,
           ln_g, ln_b, ffn_w_gate, ffn_w_up, ffn_w_down, moe_router, moe_w_gate, moe_w_up, moe_w_down):
    bsz, seq, d = x.shape
    m = bsz * seq
    bf16 = jnp.bfloat16
    split_at = np.cumsum(IN_SPLITS)[:-1].tolist()
    lb_soft = jax.nn.softmax(hgrn_lb_logits.astype(jnp.float32), axis=0)
    lower_bounds = jnp.concatenate([jnp.zeros_like(lb_soft[:1]), jnp.cumsum(lb_soft[1:], axis=0)], axis=0)
    n_in_pad = _round_up(N_IN, 512)
    h = x.reshape(m, d)
    for l in range(DEPTH):
        w_in_l = jnp.pad(w_in[l], ((0, 0), (0, n_in_pad - N_IN))).astype(bf16)
        proj = matmul(h, w_in_l)[:, :N_IN].reshape(bsz, seq, N_IN)
        u, hq, hf, hi, hg, cq, ckv, kidx, widx, gates = jnp.split(proj, split_at, axis=-1)
        y_ssm = s5_branch(u, ssm_log_dt[l], ssm_lambda_re[l], ssm_lambda_im[l], ssm_b_re[l], ssm_b_im[l],
                          ssm_c_re[l], ssm_c_im[l], ssm_d[l], ssm_w_glu[l], ssm_b_glu[l], ssm_w_out[l])
        y_hg = hgrn2_branch(hq, hf, hi, hg, lower_bounds[l], hgrn_norm_g[l], hgrn_w_out[l])
        y_att = dsa_branch(cq, ckv, kidx, widx, attn_q_norm_g[l], attn_kv_norm_g[l],
                           attn_w_uq[l], attn_w_qidx[l], attn_w_ukv[l], attn_w_out[l])
        h = merge_project_norm(h, y_ssm.reshape(m, d), y_hg.reshape(m, d), y_att.reshape(m, d),
                               gates.reshape(m, N_BRANCHES * d), w_o[l].astype(bf16),
                               ln_g[l, 0], ln_b[l, 0])
        if l % 2 == 0:
            ones = jnp.ones((m, LANE), jnp.float32)
            h = swiglu_experts_norm(h, ones, ffn_w_gate[l // 2][None].astype(bf16),
                                    ffn_w_up[l // 2][None].astype(bf16),
                                    ffn_w_down[l // 2][None].astype(bf16),
                                    ln_g[l, 1], ln_b[l, 1], tf=256)
        else:
            gate_w = router_gates(h, moe_router[l // 2])
            h = swiglu_experts_norm(h, gate_w, moe_w_gate[l // 2].astype(bf16),
                                    moe_w_up[l // 2].astype(bf16), moe_w_down[l // 2].astype(bf16),
                                    ln_g[l, 1], ln_b[l, 1], tf=512)
    return h.reshape(bsz, seq, d)
```

```python
import functools
import math

import jax
import jax.numpy as jnp
import numpy as np
from jax import lax
from jax.experimental import pallas as pl
from jax.experimental.pallas import tpu as pltpu

D_MODEL = 1024
DEPTH = 2
SSM_WIDTH = 256
SSM_GROUP = 16
SSM_GROUPS = SSM_WIDTH // SSM_GROUP
SSM_STATE = 64
HGRN_HEADS = 4
HGRN_DK = 64
HGRN_DV = 64
HGRN_WIDTH = HGRN_HEADS * HGRN_DV
HGRN_CHUNK = 64
ATT_HEADS = 8
ATT_HEAD_DIM = 64
ATT_WIDTH = ATT_HEADS * ATT_HEAD_DIM
ATT_Q_RANK = 256
ATT_KV_RANK = 128
IDX_HEADS = 4
IDX_DIM = 64
TOPK_MAX = 256
Q_BLOCK = 128
MASK_VALUE = -1e30
N_BRANCHES = 3
N_EXPERTS = 8
TOP_K = 2
DEEPNORM_ALPHA = (2 * DEPTH) ** 0.25
LN_EPS = 1e-5
RMS_EPS = 1e-6
F_MIN = 1e-12

IN_SPLITS = (SSM_WIDTH, HGRN_HEADS * HGRN_DK, HGRN_HEADS * HGRN_DK, HGRN_WIDTH, HGRN_WIDTH,
             ATT_Q_RANK, ATT_KV_RANK, IDX_DIM, IDX_HEADS, N_BRANCHES * D_MODEL)
N_IN = sum(IN_SPLITS)

VMEM_LIMIT_BYTES = 48 * 1024 * 1024
LANE = 128
MXU_DTYPE = jnp.bfloat16


def _round_up(n, m):
    return (n + m - 1) // m * m


def _matmul_body(x_ref, w_ref, o_ref):
    o_ref[...] = jnp.dot(x_ref[...].astype(MXU_DTYPE), w_ref[...],
                         preferred_element_type=jnp.float32).astype(o_ref.dtype)


def matmul(x, w, *, tm=512, tn=512, out_dtype=jnp.float32):
    m, k = x.shape
    _, n = w.shape
    tn = min(tn, n)
    assert m % tm == 0 and n % tn == 0
    return pl.pallas_call(
        _matmul_body,
        grid=(m // tm, n // tn),
        in_specs=[pl.BlockSpec((tm, k), lambda i, j: (i, 0)),
                  pl.BlockSpec((k, tn), lambda i, j: (0, j))],
        out_specs=pl.BlockSpec((tm, tn), lambda i, j: (i, j)),
        out_shape=jax.ShapeDtypeStruct((m, n), out_dtype),
        compiler_params=pltpu.CompilerParams(
            dimension_semantics=("parallel", "parallel"), vmem_limit_bytes=VMEM_LIMIT_BYTES),
        name="matmul",
    )(x, w)


def _layer_norm_rows(y, g, b):
    mu = jnp.mean(y, axis=-1, keepdims=True)
    yc = y - mu
    var = jnp.mean(yc * yc, axis=-1, keepdims=True)
    return yc * lax.rsqrt(var + LN_EPS) * g + b


def _merge_body(h_ref, ys_ref, yh_ref, ya_ref, gt_ref, wo_ref, g_ref, b_ref, o_ref):
    d = D_MODEL
    gates = jax.nn.sigmoid(gt_ref[...])
    mixed = (gates[:, 0:d] * ys_ref[...] + gates[:, d:2 * d] * yh_ref[...]
             + gates[:, 2 * d:3 * d] * ya_ref[...])
    mix_out = jnp.dot(mixed.astype(MXU_DTYPE), wo_ref[...], preferred_element_type=jnp.float32)
    o_ref[...] = _layer_norm_rows(DEEPNORM_ALPHA * h_ref[...] + mix_out, g_ref[...], b_ref[...])


def merge_project_norm(h, y_ssm, y_hg, y_att, gate_logits, w_o, ln_g, ln_b, *, tm=512):
    m, d = h.shape
    row = lambda i: (i, 0)
    full = lambda i: (0, 0)
    return pl.pallas_call(
        _merge_body,
        grid=(m // tm,),
        in_specs=[pl.BlockSpec((tm, d), row), pl.BlockSpec((tm, d), row), pl.BlockSpec((tm, d), row),
                  pl.BlockSpec((tm, d), row), pl.BlockSpec((tm, N_BRANCHES * d), row),
                  pl.BlockSpec((d, d), full), pl.BlockSpec((1, d), full), pl.BlockSpec((1, d), full)],
        out_specs=pl.BlockSpec((tm, d), row),
        out_shape=jax.ShapeDtypeStruct((m, d), jnp.float32),
        compiler_params=pltpu.CompilerParams(
            dimension_semantics=("parallel",), vmem_limit_bytes=VMEM_LIMIT_BYTES),
        name="merge_project_norm",
    )(h, y_ssm, y_hg, y_att, gate_logits, w_o, ln_g.reshape(1, d), ln_b.reshape(1, d))


def _ffn_body(h_ref, gate_ref, wg_ref, wu_ref, wd_ref, g_ref, b_ref, o_ref, acc_ref, *, n_experts):
    e = pl.program_id(1)
    f = pl.program_id(2)

    @pl.when((e == 0) & (f == 0))
    def _():
        acc_ref[...] = jnp.zeros_like(acc_ref)

    x = h_ref[...].astype(MXU_DTYPE)
    a = jnp.dot(x, wg_ref[0], preferred_element_type=jnp.float32)
    u = jnp.dot(x, wu_ref[0], preferred_element_type=jnp.float32)
    act = a * jax.nn.sigmoid(a) * u
    part = jnp.dot(act.astype(MXU_DTYPE), wd_ref[0], preferred_element_type=jnp.float32)
    if n_experts > 1:
        lane = lax.broadcasted_iota(jnp.int32, gate_ref.shape, 1)
        gate_e = jnp.sum(jnp.where(lane == e, gate_ref[...], 0.0), axis=-1, keepdims=True)
        part = gate_e * part
    acc_ref[...] += part

    @pl.when((e == pl.num_programs(1) - 1) & (f == pl.num_programs(2) - 1))
    def _():
        o_ref[...] = _layer_norm_rows(DEEPNORM_ALPHA * h_ref[...] + acc_ref[...], g_ref[...], b_ref[...])


def swiglu_experts_norm(h, gates, wg, wu, wd, ln_g, ln_b, *, tm=1024, tf=512):
    m, d = h.shape
    n_experts, _, f_dim = wg.shape
    assert f_dim % tf == 0 and m % tm == 0
    row = lambda i, e, f: (i, 0)
    full = lambda i, e, f: (0, 0)
    return pl.pallas_call(
        functools.partial(_ffn_body, n_experts=n_experts),
        grid=(m // tm, n_experts, f_dim // tf),
        in_specs=[pl.BlockSpec((tm, d), row), pl.BlockSpec((tm, LANE), row),
                  pl.BlockSpec((1, d, tf), lambda i, e, f: (e, 0, f)),
                  pl.BlockSpec((1, d, tf), lambda i, e, f: (e, 0, f)),
                  pl.BlockSpec((1, tf, d), lambda i, e, f: (e, f, 0)),
                  pl.BlockSpec((1, d), full), pl.BlockSpec((1, d), full)],
        out_specs=pl.BlockSpec((tm, d), row),
        out_shape=jax.ShapeDtypeStruct((m, d), jnp.float32),
        scratch_shapes=[pltpu.VMEM((tm, d), jnp.float32)],
        compiler_params=pltpu.CompilerParams(
            dimension_semantics=("parallel", "arbitrary", "arbitrary"),
            vmem_limit_bytes=VMEM_LIMIT_BYTES),
        name="swiglu_experts_norm",
    )(h, gates, wg, wu, wd, ln_g.reshape(1, d), ln_b.reshape(1, d))


def rms_norm(x, g):
    xf = x.astype(jnp.float32)
    return (xf * lax.rsqrt(jnp.mean(xf * xf, axis=-1, keepdims=True) + RMS_EPS) * g).astype(x.dtype)


def _ssm_combine(e1, e2):
    a1r, a1i, b1r, b1i = e1
    a2r, a2i, b2r, b2i = e2
    return (a2r * a1r - a2i * a1i, a2r * a1i + a2i * a1r,
            a2r * b1r - a2i * b1i + b2r, a2r * b1i + a2i * b1r + b2i)


S5_CHUNK = 8
S5_CHUNK_WIDTH = S5_CHUNK * SSM_WIDTH
S5_STATE_WIDTH = SSM_GROUPS * SSM_STATE


def s5_operators(log_dt, lam_re, lam_im, b_re, b_im, c_re, c_im):
    f32 = jnp.float32
    n = S5_CHUNK
    lre, lim = lam_re.astype(f32), lam_im.astype(f32)
    dt = jnp.exp(log_dt.astype(f32))[:, None]
    mag = jnp.exp(lre * dt)
    ang = lim * dt
    a_re, a_im = mag * jnp.cos(ang), mag * jnp.sin(ang)
    den = lre * lre + lim * lim
    coef_re = ((a_re - 1.0) * lre + a_im * lim) / den
    coef_im = (a_im * lre - (a_re - 1.0) * lim) / den
    br, bi = b_re.astype(f32), b_im.astype(f32)
    bb_re = coef_re[..., None] * br - coef_im[..., None] * bi
    bb_im = coef_re[..., None] * bi + coef_im[..., None] * br
    j = jnp.arange(n + 1, dtype=f32)[:, None, None]
    pw_re = jnp.exp(j * lre * dt) * jnp.cos(j * ang)
    pw_im = jnp.exp(j * lre * dt) * jnp.sin(j * ang)
    eye = jnp.eye(SSM_GROUPS, dtype=f32)
    cr, ci = c_re.astype(f32), c_im.astype(f32)
    ab_re = pw_re[..., None] * bb_re - pw_im[..., None] * bb_im
    ab_im = pw_re[..., None] * bb_im + pw_im[..., None] * bb_re
    kern = (jnp.einsum('gcp,jgpd->jgcd', cr, ab_re[:n]) - jnp.einsum('gcp,jgpd->jgcd', ci, ab_im[:n]))
    lag = jnp.arange(n)[None, :] - jnp.arange(n)[:, None]
    toep = jnp.where((lag >= 0)[:, :, None, None, None], kern[jnp.maximum(lag, 0)], 0.0)
    t_mat = jnp.einsum('abgcd,gh->agdbhc', toep, eye).reshape(S5_CHUNK_WIDTH, S5_CHUNK_WIDTH)
    v = jnp.stack([ab_re[:n][::-1], ab_im[:n][::-1]])
    w_in = jnp.einsum('rlgpc,gh->lhcrgp', v, eye).reshape(S5_CHUNK_WIDTH, 2 * S5_STATE_WIDTH)
    ar, ai = pw_re[1:], pw_im[1:]
    wo_re = jnp.einsum('gcp,lgp->gplc', cr, ar) - jnp.einsum('gcp,lgp->gplc', ci, ai)
    wo_im = -jnp.einsum('gcp,lgp->gplc', cr, ai) - jnp.einsum('gcp,lgp->gplc', ci, ar)
    w_out = jnp.einsum('rgplc,gh->rhplgc', jnp.stack([wo_re, wo_im]), eye)
    w_out = w_out.reshape(2 * S5_STATE_WIDTH, S5_CHUNK_WIDTH)
    decay = jnp.stack([pw_re[n].reshape(1, S5_STATE_WIDTH), pw_im[n].reshape(1, S5_STATE_WIDTH)])
    return (jnp.concatenate([t_mat, w_in], axis=1).astype(MXU_DTYPE), w_out.astype(MXU_DTYPE), decay)


def _s5_scan_body(u_ref, tw_ref, wout_ref, decay_ref, y_ref, h_ref, s_ref, hs_ref, *, tm):
    f32 = jnp.float32
    sw = S5_STATE_WIDTH

    @pl.when(pl.program_id(1) == 0)
    def _():
        h_ref[...] = jnp.zeros_like(h_ref)

    r = jnp.dot(u_ref[0].astype(MXU_DTYPE), tw_ref[...], preferred_element_type=f32)
    s_ref[...] = r[:, S5_CHUNK_WIDTH:]
    d_re = decay_ref[0]
    d_im = decay_ref[1]

    def eight_chunks(k, carry):
        h_re, h_im = carry
        r0 = pl.multiple_of(k * 8, 8)
        inc = s_ref[pl.ds(r0, 8), :]
        rows_re, rows_im = [], []
        for t in range(8):
            rows_re.append(h_re)
            rows_im.append(h_im)
            h_re, h_im = (d_re * h_re - d_im * h_im + inc[t:t + 1, :sw],
                          d_re * h_im + d_im * h_re + inc[t:t + 1, sw:])
        hs_ref[pl.ds(r0, 8), :] = jnp.concatenate(
            [jnp.concatenate(rows_re, axis=0), jnp.concatenate(rows_im, axis=0)], axis=1)
        return h_re, h_im

    h_re, h_im = lax.fori_loop(0, tm // 8, eight_chunks, (h_ref[0:1, :], h_ref[1:2, :]))
    h_ref[0:1, :] = h_re
    h_ref[1:2, :] = h_im
    y_ref[0] = r[:, :S5_CHUNK_WIDTH] + jnp.dot(hs_ref[...].astype(MXU_DTYPE), wout_ref[...],
                                                preferred_element_type=f32)


def s5_scan(u, tw, w_out, decay, *, tm=256):
    bsz, seq, _ = u.shape
    n_chunks = seq // S5_CHUNK
    tm = min(tm, n_chunks)
    assert seq % S5_CHUNK == 0 and n_chunks % tm == 0 and tm % 8 == 0
    once = pl.Buffered(1)
    y = pl.pallas_call(
        functools.partial(_s5_scan_body, tm=tm),
        grid=(bsz, n_chunks // tm),
        in_specs=[pl.BlockSpec((1, tm, S5_CHUNK_WIDTH), lambda b, i: (b, i, 0)),
                  pl.BlockSpec(tw.shape, lambda b, i: (0, 0), pipeline_mode=once),
                  pl.BlockSpec(w_out.shape, lambda b, i: (0, 0), pipeline_mode=once),
                  pl.BlockSpec(decay.shape, lambda b, i: (0, 0, 0), pipeline_mode=once)],
        out_specs=pl.BlockSpec((1, tm, S5_CHUNK_WIDTH), lambda b, i: (b, i, 0)),
        out_shape=jax.ShapeDtypeStruct((bsz, n_chunks, S5_CHUNK_WIDTH), jnp.float32),
        scratch_shapes=[pltpu.VMEM((2, S5_STATE_WIDTH), jnp.float32),
                        pltpu.VMEM((tm, 2 * S5_STATE_WIDTH), jnp.float32),
                        pltpu.VMEM((tm, 2 * S5_STATE_WIDTH), jnp.float32)],
        compiler_params=pltpu.CompilerParams(
            dimension_semantics=("parallel", "arbitrary"), vmem_limit_bytes=VMEM_LIMIT_BYTES),
        name="s5_scan",
    )(u.reshape(bsz, n_chunks, S5_CHUNK_WIDTH), tw, w_out, decay)
    return y.reshape(bsz, seq, SSM_WIDTH)


def _s5_out_body(y_ref, u_ref, d_ref, wglu_ref, bglu_ref, wout_ref, o_ref):
    f32 = jnp.float32
    y = jax.nn.gelu(y_ref[...] + d_ref[...] * u_ref[...])
    gate = jnp.dot(y.astype(MXU_DTYPE), wglu_ref[...], preferred_element_type=f32) + bglu_ref[...]
    y = y * jax.nn.sigmoid(gate)
    o_ref[...] = jnp.dot(y.astype(MXU_DTYPE), wout_ref[...], preferred_element_type=f32)


def s5_output(y, u, d_skip, w_glu, b_glu, w_out, *, tm=1024):
    m, c = y.shape
    tm = min(tm, m)
    row = lambda i: (i, 0)
    full = lambda i: (0, 0)
    return pl.pallas_call(
        _s5_out_body,
        grid=(m // tm,),
        in_specs=[pl.BlockSpec((tm, c), row), pl.BlockSpec((tm, c), row), pl.BlockSpec((1, c), full),
                  pl.BlockSpec((c, c), full), pl.BlockSpec((1, c), full), pl.BlockSpec((c, D_MODEL), full)],
        out_specs=pl.BlockSpec((tm, D_MODEL), row),
        out_shape=jax.ShapeDtypeStruct((m, D_MODEL), jnp.float32),
        compiler_params=pltpu.CompilerParams(
            dimension_semantics=("parallel",), vmem_limit_bytes=VMEM_LIMIT_BYTES),
        name="s5_output",
    )(y, u, d_skip.reshape(1, c), w_glu.astype(MXU_DTYPE), b_glu.reshape(1, c), w_out.astype(MXU_DTYPE))


def s5_branch(u, log_dt, lam_re, lam_im, b_re, b_im, c_re, c_im, d_skip, w_glu, b_glu, w_out):
    bsz, seq, _ = u.shape
    tw, w_state_out, decay = s5_operators(log_dt, lam_re, lam_im, b_re, b_im, c_re, c_im)
    y = s5_scan(u, tw, w_state_out, decay)
    out = s5_output(y.reshape(bsz * seq, SSM_WIDTH), u.reshape(bsz * seq, SSM_WIDTH),
                    d_skip, w_glu, b_glu, w_out)
    return out.reshape(bsz, seq, D_MODEL)


HG_CHUNK = 128
HG_LEVELS = 7
HG_KDIM = HGRN_HEADS * HGRN_DK


def _hgrn_segment_sums():
    c = HG_CHUNK
    t = np.arange(c)[:, None]
    u = np.arange(c)[None, :]
    blocks = []
    for lvl in range(1, HG_LEVELS + 1):
        m = (t >> lvl << lvl) + (1 << (lvl - 1)) - 1
        right = ((t >> (lvl - 1)) & 1) == 1
        blocks.append(np.where(right, (u > m) & (u <= t), (u > t) & (u <= m)))
    blocks.append(u <= t)
    blocks.append(u > t)
    return np.concatenate(blocks, axis=0).astype(np.float32)


def _hgrn_body(q_ref, z_ref, v_ref, g_ref, seg_ref, lb_ref, ng_ref, hmean_ref, wout_ref, o_ref, st_ref):
    f32 = jnp.float32
    c = HG_CHUNK

    @pl.when(pl.program_id(1) == 0)
    def _():
        st_ref[...] = jnp.zeros_like(st_ref)

    q = q_ref[0]
    z = z_ref[0]
    v = v_ref[0]
    lb = lb_ref[...]
    f = lb + (1.0 - lb) * jax.nn.sigmoid(z)
    logf = jnp.log(jnp.maximum(f, F_MIN))
    kin = (1.0 - lb) * jax.nn.sigmoid(-z)

    p1 = logf.astype(MXU_DTYPE)
    r1 = logf - p1.astype(f32)
    p2 = r1.astype(MXU_DTYPE)
    p3 = (r1 - p2.astype(f32)).astype(MXU_DTYPE)
    seg = seg_ref[...]
    sums = (jnp.dot(seg, p1, preferred_element_type=f32) + jnp.dot(seg, p2, preferred_element_type=f32)
            + jnp.dot(seg, p3, preferred_element_type=f32))

    lane_head = lax.broadcasted_iota(jnp.int32, (c, HG_KDIM), 1) // HGRN_DK
    tok = lax.broadcasted_iota(jnp.int32, (c, HG_KDIM), 0)
    row_t = lax.broadcasted_iota(jnp.int32, (HGRN_HEADS * c, c), 0) % c
    col_s = lax.broadcasted_iota(jnp.int32, (HGRN_HEADS * c, c), 1)

    def per_head_rows(x):
        return jnp.concatenate([jnp.where(lane_head == h, x, 0.0) for h in range(HGRN_HEADS)],
                               axis=0).astype(MXU_DTYPE)

    def scores(ql, kl):
        return lax.dot_general(per_head_rows(ql), kl.astype(MXU_DTYPE), (((1,), (1,)), ((), ())),
                               preferred_element_type=f32)

    att = jnp.where(row_t == col_s, scores(q, kin), 0.0)
    for lvl in range(1, HG_LEVELS + 1):
        decay = jnp.exp(sums[(lvl - 1) * c:lvl * c])
        right = ((tok >> (lvl - 1)) & 1) == 1
        a = scores(jnp.where(right, q * decay, 0.0), jnp.where(right, 0.0, kin * decay))
        att = att + jnp.where((row_t >> lvl) == (col_s >> lvl), a, 0.0)

    b = sums[HG_LEVELS * c:(HG_LEVELS + 1) * c]
    tail = sums[(HG_LEVELS + 1) * c:(HG_LEVELS + 2) * c]
    v_m = v.astype(MXU_DTYPE)
    st = st_ref[...]
    o = lax.dot_general((q * jnp.exp(b)).astype(MXU_DTYPE), st.astype(MXU_DTYPE),
                        (((1,), (1,)), ((), ())), preferred_element_type=f32)
    for h in range(HGRN_HEADS):
        o_h = jnp.dot(att[h * c:(h + 1) * c].astype(MXU_DTYPE), v_m, preferred_element_type=f32)
        o = o + jnp.where(lane_head == h, o_h, 0.0)

    kv = jnp.dot(v.T.astype(MXU_DTYPE), (kin * jnp.exp(tail)).astype(MXU_DTYPE), preferred_element_type=f32)
    sr = lax.broadcasted_iota(jnp.int32, st.shape, 0) // HGRN_DV
    sc = lax.broadcasted_iota(jnp.int32, st.shape, 1) // HGRN_DK
    st_ref[...] = st * jnp.exp(b[c - 1:c, :]) + jnp.where(sr == sc, kv, 0.0)

    o2 = o * o
    o2_hi = o2.astype(MXU_DTYPE)
    o2_lo = (o2 - o2_hi.astype(f32)).astype(MXU_DTYPE)
    ms = (jnp.dot(o2_hi, hmean_ref[...], preferred_element_type=f32)
          + jnp.dot(o2_lo, hmean_ref[...], preferred_element_type=f32))
    g = g_ref[0]
    out = o * lax.rsqrt(ms + RMS_EPS) * ng_ref[...] * (g * jax.nn.sigmoid(g))
    o_ref[0] = jnp.dot(out.astype(MXU_DTYPE), wout_ref[...], preferred_element_type=f32)


def hgrn2_branch(q, f_logit, i_in, g_out, lower_bound, norm_g, w_out):
    bsz, seq, _ = q.shape
    assert seq % HG_CHUNK == 0 and HGRN_DK == HGRN_DV
    seg = jnp.asarray(_hgrn_segment_sums(), MXU_DTYPE)
    head_mean = jnp.asarray(np.kron(np.eye(HGRN_HEADS), np.full((HGRN_DV, HGRN_DV), 1.0 / HGRN_DV)), MXU_DTYPE)
    tok = lambda b, i: (b, i, 0)
    full = lambda b, i: (0, 0)
    blk = pl.BlockSpec((1, HG_CHUNK, HG_KDIM), tok)
    return pl.pallas_call(
        _hgrn_body,
        grid=(bsz, seq // HG_CHUNK),
        in_specs=[blk, blk, blk, blk,
                  pl.BlockSpec(seg.shape, full), pl.BlockSpec((1, HG_KDIM), full),
                  pl.BlockSpec((1, HGRN_WIDTH), full), pl.BlockSpec(head_mean.shape, full),
                  pl.BlockSpec((HGRN_WIDTH, D_MODEL), full)],
        out_specs=pl.BlockSpec((1, HG_CHUNK, D_MODEL), tok),
        out_shape=jax.ShapeDtypeStruct((bsz, seq, D_MODEL), jnp.float32),
        scratch_shapes=[pltpu.VMEM((HGRN_WIDTH, HG_KDIM), jnp.float32)],
        compiler_params=pltpu.CompilerParams(
            dimension_semantics=("parallel", "arbitrary"), vmem_limit_bytes=VMEM_LIMIT_BYTES),
        name="hgrn2",
    )(q, f_logit, i_in, g_out, seg, lower_bound.reshape(1, HG_KDIM).astype(jnp.float32),
      jnp.tile(norm_g.astype(jnp.float32), HGRN_HEADS).reshape(1, HGRN_WIDTH), head_mean,
      w_out.astype(MXU_DTYPE))


def _norm_matmul_body(x_ref, g_ref, w_ref, o_ref):
    x = x_ref[...]
    xn = x * lax.rsqrt(jnp.mean(x * x, axis=-1, keepdims=True) + RMS_EPS) * g_ref[...]
    o_ref[...] = jnp.dot(xn.astype(MXU_DTYPE), w_ref[...],
                         preferred_element_type=jnp.float32).astype(o_ref.dtype)


def norm_matmul(x, g, w, *, tm=1024, out_dtype=None):
    m, k = x.shape
    _, n = w.shape
    tm = min(tm, m)
    return pl.pallas_call(
        _norm_matmul_body,
        grid=(m // tm,),
        in_specs=[pl.BlockSpec((tm, k), lambda i: (i, 0)), pl.BlockSpec((1, k), lambda i: (0, 0)),
                  pl.BlockSpec((k, n), lambda i: (0, 0))],
        out_specs=pl.BlockSpec((tm, n), lambda i: (i, 0)),
        out_shape=jax.ShapeDtypeStruct((m, n), out_dtype or MXU_DTYPE),
        compiler_params=pltpu.CompilerParams(
            dimension_semantics=("parallel",), vmem_limit_bytes=VMEM_LIMIT_BYTES),
        name="norm_matmul",
    )(x, g.reshape(1, k), w)


_INT_MIN = -2 ** 31
_MASK_KEY = int(np.float32(MASK_VALUE).view(np.int32)) ^ 0x7FFFFFFF
_SEARCH_ROWS = 128
_ATT_ROWS = 64
_SEARCH_LANES = 512


def _dsa_body(qi_ref, kj_ref, qidx_ref, w_ref, kidt_ref, q_ref, kt_ref, v_ref, tri_ref, wout_ref,
              o_ref, keys_ref, thr_ref, need_ref, carry_ref, m_ref, acc_ref, bias_ref, s_ref, p_ref,
              alpha_ref, *, tq, tk, seq, n_sel):
    f32 = jnp.float32
    p_id = pl.program_id(1)
    i = qi_ref[p_id]
    j = kj_ref[p_id]
    row_local = lax.broadcasted_iota(jnp.int32, (tq, tk), 0)
    col_local = lax.broadcasted_iota(jnp.int32, (tq, tk), 1)

    @pl.when(j == 0)
    def _select():
        def score_block(jj, carry):
            off = pl.multiple_of(jj * tk, tk)
            kb = kidt_ref[0, :, pl.ds(off, tk)]
            sc = jnp.zeros((tq, tk), f32)
            for h in range(IDX_HEADS):
                s = jnp.dot(qidx_ref[0, h], kb, preferred_element_type=f32)
                sc = sc + jnp.maximum(s, 0.0) * w_ref[0, :, h:h + 1]
            causal = (col_local + jj * tk) <= (row_local + i * tq)
            sc = jnp.where(causal, sc, MASK_VALUE)
            sc = jnp.where(sc == 0.0, 0.0, sc)
            bits = lax.bitcast_convert_type(sc, jnp.int32)
            keys_ref[:, pl.ds(off, tk)] = jnp.where(bits < 0, bits ^ 0x7FFFFFFF, bits)
            return carry

        lax.fori_loop(0, i + 1, score_block, 0)

        n_proc = (i + 1) * tk
        n_iter = n_proc // _SEARCH_LANES
        n_masked_tail = (seq - n_proc).astype(f32)

        def row_group(r, carry):
            r0 = pl.multiple_of(r * _SEARCH_ROWS, _SEARCH_ROWS)

            def count_ge(cand):
                cand_b = jnp.broadcast_to(cand, (_SEARCH_ROWS, LANE))

                def chunk(c, acc):
                    base = pl.multiple_of(c * _SEARCH_LANES, _SEARCH_LANES)
                    for u in range(_SEARCH_LANES // LANE):
                        kk = keys_ref[pl.ds(r0, _SEARCH_ROWS), pl.ds(base + u * LANE, LANE)]
                        acc = acc + jnp.where(kk >= cand_b, 1.0, 0.0)
                    return acc

                acc = lax.fori_loop(0, n_iter, chunk, jnp.zeros((_SEARCH_ROWS, LANE), f32))
                cnt = jnp.sum(acc, axis=1, keepdims=True)
                return cnt + jnp.where(cand <= _MASK_KEY, n_masked_tail, 0.0)

            zero = jnp.zeros((_SEARCH_ROWS, 1), jnp.int32)
            v0 = jnp.where(count_ge(zero) >= n_sel, zero, zero + _INT_MIN)

            def bit_step(b, v):
                cand = v | jnp.left_shift(jnp.int32(1), 30 - b)
                return jnp.where(count_ge(cand) >= n_sel, cand, v)

            v = lax.fori_loop(0, 31, bit_step, v0)
            thr_ref[pl.ds(r0, _SEARCH_ROWS), :] = v
            need_ref[pl.ds(r0, _SEARCH_ROWS), :] = n_sel - count_ge(v + 1)
            return carry

        lax.fori_loop(0, tq // _SEARCH_ROWS, row_group, 0)
        carry_ref[...] = jnp.zeros_like(carry_ref)
        m_ref[...] = jnp.full_like(m_ref, MASK_VALUE)
        acc_ref[...] = jnp.zeros_like(acc_ref)

    keys_blk = keys_ref[:, pl.ds(pl.multiple_of(j * tk, tk), tk)]
    thr = thr_ref[...]
    eq = keys_blk == thr
    tie_rank = carry_ref[...] + jnp.dot(jnp.where(eq, 1.0, 0.0).astype(MXU_DTYPE), tri_ref[...],
                                         preferred_element_type=f32)
    carry_ref[...] = tie_rank[:, tk - 1:tk]
    sel = (keys_blk > thr) | (eq & (tie_rank <= need_ref[...]))
    causal = (row_local + i * tq) >= (col_local + j * tk)
    bias_ref[...] = jnp.where(sel & causal, 0.0, MASK_VALUE)

    def logits(h, slot):
        s_ref[slot] = jnp.dot(q_ref[0, h], kt_ref[0, h], preferred_element_type=f32)

    def softmax(h, slot):
        slope_tk = jnp.right_shift(tk, h + 1).astype(f32)
        for r in range(0, tq, _ATT_ROWS):
            rows = pl.ds(r, _ATT_ROWS)
            s = s_ref[slot, rows, :] + bias_ref[rows, :]
            m_old = m_ref[h, rows, :] - slope_tk
            m_new = jnp.maximum(m_old, jnp.max(s, axis=1, keepdims=True))
            p_ref[slot, rows, :] = jnp.exp(s - m_new).astype(MXU_DTYPE)
            alpha_ref[slot, rows, :] = jnp.exp(m_old - m_new)
            m_ref[h, rows, :] = m_new

    def weighted_values(h, slot):
        acc_ref[h] = alpha_ref[slot] * acc_ref[h] + jnp.dot(p_ref[slot], v_ref[0, h],
                                                             preferred_element_type=f32)

    for t in range(ATT_HEADS + 2):
        if t < ATT_HEADS:
            logits(t, t % 2)
        if 1 <= t <= ATT_HEADS:
            softmax(t - 1, (t - 1) % 2)
        if t >= 2:
            weighted_values(t - 2, t % 2)

    @pl.when(j == i)
    def _finish():
        heads = []
        for h in range(ATT_HEADS):
            a = acc_ref[h]
            heads.append((a[:, :ATT_HEAD_DIM] / a[:, ATT_HEAD_DIM:ATT_HEAD_DIM + 1]).astype(MXU_DTYPE))
        o_ref[0] = jnp.dot(jnp.concatenate(heads, axis=1), wout_ref[...], preferred_element_type=f32)


ATT_QK_DIM = 80


def _with_distance_columns(q, k_t, tq, tk):
    bsz, nh, seq, dh = q.shape
    assert 8 % nh == 0 and max(tq, tk) <= 512
    f32 = jnp.float32
    slopes = (2.0 ** (-8.0 * jnp.arange(1, nh + 1, dtype=f32) / nh))[:, None]
    t_loc = jnp.arange(seq) % tq
    s_loc = jnp.arange(seq) % tk
    t_even, t_odd = (t_loc // 2 * 2).astype(f32)[None], (t_loc % 2).astype(f32)[None]
    s_even, s_odd = (s_loc // 2 * 2).astype(f32), (s_loc % 2).astype(f32)
    one = jnp.ones((seq,), f32)
    q_cols = jnp.stack([-slopes * t_even, -slopes * t_odd, slopes * one, slopes * one], axis=-1)
    k_rows = jnp.stack([one, one, s_even, s_odd], axis=0)
    pad = ATT_QK_DIM - dh - 4
    q_aug = jnp.concatenate([q, jnp.broadcast_to(q_cols.astype(q.dtype), (bsz, nh, seq, 4)),
                             jnp.zeros((bsz, nh, seq, pad), q.dtype)], axis=-1)
    k_aug = jnp.concatenate([k_t, jnp.broadcast_to(k_rows.astype(q.dtype), (bsz, nh, 4, seq)),
                             jnp.zeros((bsz, nh, pad, seq), q.dtype)], axis=2)
    return q_aug, k_aug


def dsa_attention(q_idx, w, kid_t, q, k_t, v_aug, w_out, *, tq=512):
    bsz, _, seq, _ = q.shape
    tq = min(tq, seq)
    tk = tq
    n_sel = min(TOPK_MAX, seq // 4)
    nq = seq // tq
    assert seq % tq == 0 and tk % _SEARCH_LANES == 0 and tq % _SEARCH_ROWS == 0
    q, k_t = _with_distance_columns(q, k_t, tq, tk)
    pairs = [(a, b) for a in range(nq) for b in range(a + 1)]
    qi = jnp.asarray([a for a, _ in pairs], jnp.int32)
    kj = jnp.asarray([b for _, b in pairs], jnp.int32)
    tri = jnp.triu(jnp.ones((tk, tk), MXU_DTYPE))
    grid_spec = pltpu.PrefetchScalarGridSpec(
        num_scalar_prefetch=2,
        grid=(bsz, len(pairs)),
        in_specs=[
            pl.BlockSpec((1, IDX_HEADS, tq, IDX_DIM), lambda b, p, qi, kj: (b, 0, qi[p], 0)),
            pl.BlockSpec((1, tq, IDX_HEADS), lambda b, p, qi, kj: (b, qi[p], 0)),
            pl.BlockSpec((1, IDX_DIM, seq), lambda b, p, qi, kj: (b, 0, 0)),
            pl.BlockSpec((1, ATT_HEADS, tq, ATT_QK_DIM), lambda b, p, qi, kj: (b, 0, qi[p], 0)),
            pl.BlockSpec((1, ATT_HEADS, ATT_QK_DIM, tk), lambda b, p, qi, kj: (b, 0, 0, kj[p])),
            pl.BlockSpec((1, ATT_HEADS, tk, LANE), lambda b, p, qi, kj: (b, 0, kj[p], 0)),
            pl.BlockSpec((tk, tk), lambda b, p, qi, kj: (0, 0)),
            pl.BlockSpec((ATT_WIDTH, D_MODEL), lambda b, p, qi, kj: (0, 0)),
        ],
        out_specs=pl.BlockSpec((1, tq, D_MODEL), lambda b, p, qi, kj: (b, qi[p], 0)),
        scratch_shapes=[
            pltpu.VMEM((tq, seq), jnp.int32),
            pltpu.VMEM((tq, 1), jnp.int32),
            pltpu.VMEM((tq, 1), jnp.float32),
            pltpu.VMEM((tq, 1), jnp.float32),
            pltpu.VMEM((ATT_HEADS, tq, 1), jnp.float32),
            pltpu.VMEM((ATT_HEADS, tq, LANE), jnp.float32),
            pltpu.VMEM((tq, tk), jnp.float32),
            pltpu.VMEM((2, tq, tk), jnp.float32),
            pltpu.VMEM((2, tq, tk), MXU_DTYPE),
            pltpu.VMEM((2, tq, 1), jnp.float32),
        ])
    return pl.pallas_call(
        functools.partial(_dsa_body, tq=tq, tk=tk, seq=seq, n_sel=n_sel),
        grid_spec=grid_spec,
        out_shape=jax.ShapeDtypeStruct((bsz, seq, D_MODEL), jnp.float32),
        compiler_params=pltpu.CompilerParams(
            dimension_semantics=("parallel", "arbitrary"), vmem_limit_bytes=VMEM_LIMIT_BYTES),
        name="dsa_attention",
    )(qi, kj, q_idx, w, kid_t, q, k_t, v_aug, tri, w_out)


def dsa_branch(c_q, c_kv, k_idx, w_idx, q_norm_g, kv_norm_g, w_uq, w_qidx, w_ukv, w_out):
    bsz, seq, _ = c_q.shape
    m = bsz * seq
    w_q = jnp.concatenate([w_uq * ATT_HEAD_DIM ** -0.5, w_qidx * IDX_DIM ** -0.5], axis=1).astype(MXU_DTYPE)
    qq = norm_matmul(c_q.reshape(m, ATT_Q_RANK), q_norm_g, w_q)
    kv = norm_matmul(c_kv.reshape(m, ATT_KV_RANK), kv_norm_g, w_ukv.astype(MXU_DTYPE))
    q = qq[:, :ATT_WIDTH].reshape(bsz, seq, ATT_HEADS, ATT_HEAD_DIM).transpose(0, 2, 1, 3)
    q_idx = qq[:, ATT_WIDTH:].reshape(bsz, seq, IDX_HEADS, IDX_DIM).transpose(0, 2, 1, 3)
    k_t = kv[:, :ATT_WIDTH].reshape(bsz, seq, ATT_HEADS, ATT_HEAD_DIM).transpose(0, 2, 3, 1)
    v = kv[:, ATT_WIDTH:].reshape(bsz, seq, ATT_HEADS, ATT_HEAD_DIM).transpose(0, 2, 1, 3)
    ones = jnp.ones((bsz, ATT_HEADS, seq, 1), MXU_DTYPE)
    v_aug = jnp.concatenate([v, ones, jnp.zeros((bsz, ATT_HEADS, seq, LANE - ATT_HEAD_DIM - 1), MXU_DTYPE)],
                            axis=-1)
    kid_t = k_idx.astype(MXU_DTYPE).transpose(0, 2, 1)
    w = w_idx.astype(jnp.float32) * IDX_HEADS ** -0.5
    return dsa_attention(q_idx, w, kid_t, q, k_t, v_aug,
                         w_out.astype(MXU_DTYPE))


MOE_TOKENS = 1024
MOE_ROW_CLASSES = (256, 320, 512, 1024)
MOE_VMEM_LIMIT_BYTES = 56 * 1024 * 1024


def _moe_body(cnt_ref, h_ref, gate_ref, gatet_ref, tri_ref, wg_ref, wu_ref, wd_ref, g_ref, b_ref, o_ref,
              xb_ref, gather_ref, scatter_ref, xc_ref, yc_ref, acc_ref):
    f32 = jnp.float32
    tm = MOE_TOKENS
    i = pl.program_id(0)
    e = pl.program_id(1)
    f = pl.program_id(2)
    last_f = pl.num_programs(2) - 1

    @pl.when((e == 0) & (f == 0))
    def _():
        xb_ref[...] = h_ref[...].astype(MXU_DTYPE)
        acc_ref[...] = jnp.zeros_like(acc_ref)

    lane = lax.broadcasted_iota(jnp.int32, gate_ref.shape, 1)
    gate_col = jnp.sum(jnp.where(lane == e, gate_ref[...], 0.0), axis=-1, keepdims=True)

    def routed(size):
        def run():
            @pl.when(f == 0)
            def _():
                active_row = jnp.where(gatet_ref[pl.ds(e, 1), :] != 0.0, 1.0, 0.0)
                rank_row = jnp.dot(jnp.broadcast_to(active_row, (8, tm)).astype(MXU_DTYPE), tri_ref[...],
                                   preferred_element_type=f32)[0:1]
                slot = lax.broadcasted_iota(jnp.int32, (size, tm), 0).astype(f32)
                gather = jnp.where((slot == rank_row) & (active_row != 0.0), 1.0, 0.0)
                gather_ref[0:size, :] = gather.astype(MXU_DTYPE)
                rank_col = jnp.broadcast_to(rank_row, (LANE, tm)).T[:, 0:1]
                slot_l = lax.broadcasted_iota(jnp.int32, (tm, size), 1).astype(f32)
                scatter = jnp.where((slot_l == rank_col) & (gate_col != 0.0), 1.0, 0.0)
                scatter_ref[:, 0:size] = scatter.astype(MXU_DTYPE)
                xc_ref[0:size, :] = jnp.dot(gather.astype(MXU_DTYPE), xb_ref[...],
                                            preferred_element_type=f32).astype(MXU_DTYPE)
                yc_ref[0:size, :] = jnp.zeros((size, D_MODEL), f32)

            xc = xc_ref[0:size, :]
            a = jnp.dot(xc, wg_ref[0], preferred_element_type=f32)
            u = jnp.dot(xc, wu_ref[0], preferred_element_type=f32)
            act = a * jax.nn.sigmoid(a) * u
            yc_ref[0:size, :] += jnp.dot(act.astype(MXU_DTYPE), wd_ref[0], preferred_element_type=f32)

            @pl.when(f == last_f)
            def _():
                back = jnp.dot(scatter_ref[:, 0:size], yc_ref[0:size, :].astype(MXU_DTYPE),
                               preferred_element_type=f32)
                acc_ref[...] += gate_col * back
        return run

    count = cnt_ref[i * N_EXPERTS + e]
    size_class = sum((count > s).astype(jnp.int32) for s in MOE_ROW_CLASSES[:-1])
    for k, size in enumerate(MOE_ROW_CLASSES):
        pl.when((size_class == k) & (count > 0))(routed(size))

    @pl.when((e == pl.num_programs(1) - 1) & (f == last_f))
    def _():
        o_ref[...] = _layer_norm_rows(DEEPNORM_ALPHA * h_ref[...] + acc_ref[...], g_ref[...], b_ref[...])


def moe_experts_norm(h, gates, wg, wu, wd, ln_g, ln_b, *, tf=512):
    m, d = h.shape
    n_experts, _, f_dim = wg.shape
    tm = MOE_TOKENS
    assert m % tm == 0 and f_dim % tf == 0 and MOE_ROW_CLASSES[-1] == tm
    n_tiles = m // tm
    routed = gates[:, :n_experts] != 0.0
    counts = jnp.sum(routed.reshape(n_tiles, tm, n_experts), axis=1, dtype=jnp.int32).reshape(-1)
    gates_t = gates[:, :n_experts].T
    tri = jnp.triu(jnp.ones((tm, tm), MXU_DTYPE), k=1)
    row = lambda i, e, f, c: (i, 0)
    full = lambda i, e, f, c: (0, 0)
    grid_spec = pltpu.PrefetchScalarGridSpec(
        num_scalar_prefetch=1,
        grid=(n_tiles, n_experts, f_dim // tf),
        in_specs=[pl.BlockSpec((tm, d), row), pl.BlockSpec((tm, LANE), row),
                  pl.BlockSpec((n_experts, tm), lambda i, e, f, c: (0, i)),
                  pl.BlockSpec((tm, tm), full),
                  pl.BlockSpec((1, d, tf), lambda i, e, f, c: (e, 0, f)),
                  pl.BlockSpec((1, d, tf), lambda i, e, f, c: (e, 0, f)),
                  pl.BlockSpec((1, tf, d), lambda i, e, f, c: (e, f, 0)),
                  pl.BlockSpec((1, d), full), pl.BlockSpec((1, d), full)],
        out_specs=pl.BlockSpec((tm, d), row),
        scratch_shapes=[pltpu.VMEM((tm, d), MXU_DTYPE),
                        pltpu.VMEM((tm, tm), MXU_DTYPE),
                        pltpu.VMEM((tm, tm), MXU_DTYPE),
                        pltpu.VMEM((tm, d), MXU_DTYPE),
                        pltpu.VMEM((tm, d), jnp.float32),
                        pltpu.VMEM((tm, d), jnp.float32)])
    return pl.pallas_call(
        _moe_body,
        grid_spec=grid_spec,
        out_shape=jax.ShapeDtypeStruct((m, d), jnp.float32),
        compiler_params=pltpu.CompilerParams(
            dimension_semantics=("parallel", "arbitrary", "arbitrary"),
            vmem_limit_bytes=MOE_VMEM_LIMIT_BYTES),
        name="moe_experts_norm",
    )(counts, h, gates, gates_t, tri, wg, wu, wd, ln_g.reshape(1, d), ln_b.reshape(1, d))


def router_gates(h, router):
    logits = (h @ router).astype(jnp.float32)
    top_val, top_idx = lax.top_k(logits, TOP_K)
    top_w = jax.nn.softmax(top_val, axis=-1)
    onehot = (top_idx[..., None] == jnp.arange(N_EXPERTS)[None, None, :])
    gates = jnp.sum(jnp.where(onehot, top_w[..., None], 0.0), axis=1)
    return jnp.pad(gates, ((0, 0), (0, LANE - N_EXPERTS)))


def kernel(x, w_in, ssm_log_dt, ssm_lambda_re, ssm_lambda_im, ssm_b_re, ssm_b_im, ssm_c_re, ssm_c_im,
           ssm_d, ssm_w_glu, ssm_b_glu, ssm_w_out, hgrn_lb_logits, hgrn_norm_g, hgrn_w_out,
           attn_q_norm_g, attn_kv_norm_g, attn_w_uq, attn_w_qidx, attn_w_ukv, attn_w_out, w_o,
           ln_g, ln_b, ffn_w_gate, ffn_w_up, ffn_w_down, moe_router, moe_w_gate, moe_w_up, moe_w_down):
    bsz, seq, d = x.shape
    m = bsz * seq
    bf16 = jnp.bfloat16
    split_at = np.cumsum(IN_SPLITS)[:-1].tolist()
    lb_soft = jax.nn.softmax(hgrn_lb_logits.astype(jnp.float32), axis=0)
    lower_bounds = jnp.concatenate([jnp.zeros_like(lb_soft[:1]), jnp.cumsum(lb_soft[1:], axis=0)], axis=0)
    n_in_pad = _round_up(N_IN, 512)
    h = x.reshape(m, d)
    for l in range(DEPTH):
        w_in_l = jnp.pad(w_in[l], ((0, 0), (0, n_in_pad - N_IN))).astype(bf16)
        proj = matmul(h, w_in_l)[:, :N_IN].reshape(bsz, seq, N_IN)
        u, hq, hf, hi, hg, cq, ckv, kidx, widx, gates = jnp.split(proj, split_at, axis=-1)
        y_ssm = s5_branch(u, ssm_log_dt[l], ssm_lambda_re[l], ssm_lambda_im[l], ssm_b_re[l], ssm_b_im[l],
                          ssm_c_re[l], ssm_c_im[l], ssm_d[l], ssm_w_glu[l], ssm_b_glu[l], ssm_w_out[l])
        y_hg = hgrn2_branch(hq, hf, hi, hg, lower_bounds[l], hgrn_norm_g[l], hgrn_w_out[l])
        y_att = dsa_branch(cq, ckv, kidx, widx, attn_q_norm_g[l], attn_kv_norm_g[l],
                           attn_w_uq[l], attn_w_qidx[l], attn_w_ukv[l], attn_w_out[l])
        h = merge_project_norm(h, y_ssm.reshape(m, d), y_hg.reshape(m, d), y_att.reshape(m, d),
                               gates.reshape(m, N_BRANCHES * d), w_o[l].astype(bf16),
                               ln_g[l, 0], ln_b[l, 0])
        if l % 2 == 0:
            ones = jnp.ones((m, LANE), jnp.float32)
            h = swiglu_experts_norm(h, ones, ffn_w_gate[l // 2][None].astype(bf16),
                                    ffn_w_up[l // 2][None].astype(bf16),
                                    ffn_w_down[l // 2][None].astype(bf16),
                                    ln_g[l, 1], ln_b[l, 1], tf=256)
        else:
            gate_w = router_gates(h, moe_router[l // 2])
            h = moe_experts_norm(h, gate_w, moe_w_gate[l // 2].astype(bf16),
                                 moe_w_up[l // 2].astype(bf16), moe_w_down[l // 2].astype(bf16),
                                 ln_g[l, 1], ln_b[l, 1])
    return h.reshape(bsz, seq, d)
```

```python
import functools
import math

import jax
import jax.numpy as jnp
import numpy as np
from jax import lax
from jax.experimental import pallas as pl
from jax.experimental.pallas import tpu as pltpu

D_MODEL = 1024
DEPTH = 2
SSM_WIDTH = 256
SSM_GROUP = 16
SSM_GROUPS = SSM_WIDTH // SSM_GROUP
SSM_STATE = 64
HGRN_HEADS = 4
HGRN_DK = 64
HGRN_DV = 64
HGRN_WIDTH = HGRN_HEADS * HGRN_DV
HGRN_CHUNK = 64
ATT_HEADS = 8
ATT_HEAD_DIM = 64
ATT_WIDTH = ATT_HEADS * ATT_HEAD_DIM
ATT_Q_RANK = 256
ATT_KV_RANK = 128
IDX_HEADS = 4
IDX_DIM = 64
TOPK_MAX = 256
Q_BLOCK = 128
MASK_VALUE = -1e30
N_BRANCHES = 3
N_EXPERTS = 8
TOP_K = 2
DEEPNORM_ALPHA = (2 * DEPTH) ** 0.25
LN_EPS = 1e-5
RMS_EPS = 1e-6
F_MIN = 1e-12

IN_SPLITS = (SSM_WIDTH, HGRN_HEADS * HGRN_DK, HGRN_HEADS * HGRN_DK, HGRN_WIDTH, HGRN_WIDTH,
             ATT_Q_RANK, ATT_KV_RANK, IDX_DIM, IDX_HEADS, N_BRANCHES * D_MODEL)
N_IN = sum(IN_SPLITS)

VMEM_LIMIT_BYTES = 48 * 1024 * 1024
LANE = 128
MXU_DTYPE = jnp.bfloat16


def _round_up(n, m):
    return (n + m - 1) // m * m


HG_IN_WIDTH = 2 * HGRN_HEADS * HGRN_DK + 2 * HGRN_WIDTH
ATT_IN_USED = ATT_Q_RANK + ATT_KV_RANK + IDX_DIM + IDX_HEADS
ATT_IN_WIDTH = _round_up(ATT_IN_USED, LANE)
ATT_IDX_BLOCK = (ATT_Q_RANK + ATT_KV_RANK) // LANE
MIX_IN_USED = SSM_WIDTH + HG_IN_WIDTH + ATT_IN_USED
MIX_IN_WIDTH = SSM_WIDTH + HG_IN_WIDTH + ATT_IN_WIDTH


def _in_proj_body(h_ref, w_ref, u_ref, hg_ref, att_ref):
    r = jnp.dot(h_ref[...].astype(MXU_DTYPE), w_ref[...], preferred_element_type=jnp.float32)
    u_ref[...] = r[:, :SSM_WIDTH]
    hg_ref[...] = r[:, SSM_WIDTH:SSM_WIDTH + HG_IN_WIDTH]
    att_ref[...] = r[:, SSM_WIDTH + HG_IN_WIDTH:]


def in_proj(h, w_mix, *, tm=512):
    m, d = h.shape
    row = lambda i: (i, 0)
    widths = (SSM_WIDTH, HG_IN_WIDTH, ATT_IN_WIDTH)
    return pl.pallas_call(
        _in_proj_body,
        grid=(m // tm,),
        in_specs=[pl.BlockSpec((tm, d), row), pl.BlockSpec((d, MIX_IN_WIDTH), lambda i: (0, 0))],
        out_specs=[pl.BlockSpec((tm, w), row) for w in widths],
        out_shape=[jax.ShapeDtypeStruct((m, w), jnp.float32) for w in widths],
        compiler_params=pltpu.CompilerParams(
            dimension_semantics=("parallel",), vmem_limit_bytes=VMEM_LIMIT_BYTES),
        name="in_proj",
    )(h, w_mix)


def _layer_norm_rows(y, g, b):
    mu = jnp.mean(y, axis=-1, keepdims=True)
    yc = y - mu
    var = jnp.mean(yc * yc, axis=-1, keepdims=True)
    return yc * lax.rsqrt(var + LN_EPS) * g + b


def _merge_body(h_ref, ys_ref, yh_ref, ya_ref, wgate_ref, wo_ref, g_ref, b_ref, o_ref):
    d = D_MODEL
    f32 = jnp.float32
    h = h_ref[...]
    gates = jax.nn.sigmoid(jnp.dot(h.astype(MXU_DTYPE), wgate_ref[...], preferred_element_type=f32))
    mixed = (gates[:, 0:d] * ys_ref[...] + gates[:, d:2 * d] * yh_ref[...]
             + gates[:, 2 * d:3 * d] * ya_ref[...])
    mix_out = jnp.dot(mixed.astype(MXU_DTYPE), wo_ref[...], preferred_element_type=f32)
    o_ref[...] = _layer_norm_rows(DEEPNORM_ALPHA * h + mix_out, g_ref[...], b_ref[...])


def merge_project_norm(h, y_ssm, y_hg, y_att, w_gates, w_o, ln_g, ln_b, *, tm=512):
    m, d = h.shape
    row = lambda i: (i, 0)
    full = lambda i: (0, 0)
    return pl.pallas_call(
        _merge_body,
        grid=(m // tm,),
        in_specs=[pl.BlockSpec((tm, d), row), pl.BlockSpec((tm, d), row), pl.BlockSpec((tm, d), row),
                  pl.BlockSpec((tm, d), row), pl.BlockSpec((d, N_BRANCHES * d), full),
                  pl.BlockSpec((d, d), full), pl.BlockSpec((1, d), full), pl.BlockSpec((1, d), full)],
        out_specs=pl.BlockSpec((tm, d), row),
        out_shape=jax.ShapeDtypeStruct((m, d), jnp.float32),
        compiler_params=pltpu.CompilerParams(
            dimension_semantics=("parallel",), vmem_limit_bytes=VMEM_LIMIT_BYTES),
        name="merge_project_norm",
    )(h, y_ssm, y_hg, y_att, w_gates, w_o, ln_g.reshape(1, d), ln_b.reshape(1, d))


def _ffn_body(h_ref, gate_ref, wg_ref, wu_ref, wd_ref, g_ref, b_ref, o_ref, acc_ref, *, n_experts):
    e = pl.program_id(1)
    f = pl.program_id(2)

    @pl.when((e == 0) & (f == 0))
    def _():
        acc_ref[...] = jnp.zeros_like(acc_ref)

    x = h_ref[...].astype(MXU_DTYPE)
    a = jnp.dot(x, wg_ref[0], preferred_element_type=jnp.float32)
    u = jnp.dot(x, wu_ref[0], preferred_element_type=jnp.float32)
    act = a * jax.nn.sigmoid(a) * u
    part = jnp.dot(act.astype(MXU_DTYPE), wd_ref[0], preferred_element_type=jnp.float32)
    if n_experts > 1:
        lane = lax.broadcasted_iota(jnp.int32, gate_ref.shape, 1)
        gate_e = jnp.sum(jnp.where(lane == e, gate_ref[...], 0.0), axis=-1, keepdims=True)
        part = gate_e * part
    acc_ref[...] += part

    @pl.when((e == pl.num_programs(1) - 1) & (f == pl.num_programs(2) - 1))
    def _():
        o_ref[...] = _layer_norm_rows(DEEPNORM_ALPHA * h_ref[...] + acc_ref[...], g_ref[...], b_ref[...])


def swiglu_experts_norm(h, gates, wg, wu, wd, ln_g, ln_b, *, tm=1024, tf=512):
    m, d = h.shape
    n_experts, _, f_dim = wg.shape
    assert f_dim % tf == 0 and m % tm == 0
    row = lambda i, e, f: (i, 0)
    full = lambda i, e, f: (0, 0)
    return pl.pallas_call(
        functools.partial(_ffn_body, n_experts=n_experts),
        grid=(m // tm, n_experts, f_dim // tf),
        in_specs=[pl.BlockSpec((tm, d), row), pl.BlockSpec((tm, LANE), row),
                  pl.BlockSpec((1, d, tf), lambda i, e, f: (e, 0, f)),
                  pl.BlockSpec((1, d, tf), lambda i, e, f: (e, 0, f)),
                  pl.BlockSpec((1, tf, d), lambda i, e, f: (e, f, 0)),
                  pl.BlockSpec((1, d), full), pl.BlockSpec((1, d), full)],
        out_specs=pl.BlockSpec((tm, d), row),
        out_shape=jax.ShapeDtypeStruct((m, d), jnp.float32),
        scratch_shapes=[pltpu.VMEM((tm, d), jnp.float32)],
        compiler_params=pltpu.CompilerParams(
            dimension_semantics=("parallel", "arbitrary", "arbitrary"),
            vmem_limit_bytes=VMEM_LIMIT_BYTES),
        name="swiglu_experts_norm",
    )(h, gates, wg, wu, wd, ln_g.reshape(1, d), ln_b.reshape(1, d))


def rms_norm(x, g):
    xf = x.astype(jnp.float32)
    return (xf * lax.rsqrt(jnp.mean(xf * xf, axis=-1, keepdims=True) + RMS_EPS) * g).astype(x.dtype)


def _ssm_combine(e1, e2):
    a1r, a1i, b1r, b1i = e1
    a2r, a2i, b2r, b2i = e2
    return (a2r * a1r - a2i * a1i, a2r * a1i + a2i * a1r,
            a2r * b1r - a2i * b1i + b2r, a2r * b1i + a2i * b1r + b2i)


S5_CHUNK = 8
S5_CHUNK_WIDTH = S5_CHUNK * SSM_WIDTH
S5_STATE_WIDTH = SSM_GROUPS * SSM_STATE


def s5_operators(log_dt, lam_re, lam_im, b_re, b_im, c_re, c_im):
    f32 = jnp.float32
    n = S5_CHUNK
    lre, lim = lam_re.astype(f32), lam_im.astype(f32)
    dt = jnp.exp(log_dt.astype(f32))[:, None]
    mag = jnp.exp(lre * dt)
    ang = lim * dt
    a_re, a_im = mag * jnp.cos(ang), mag * jnp.sin(ang)
    den = lre * lre + lim * lim
    coef_re = ((a_re - 1.0) * lre + a_im * lim) / den
    coef_im = (a_im * lre - (a_re - 1.0) * lim) / den
    br, bi = b_re.astype(f32), b_im.astype(f32)
    bb_re = coef_re[..., None] * br - coef_im[..., None] * bi
    bb_im = coef_re[..., None] * bi + coef_im[..., None] * br
    j = jnp.arange(n + 1, dtype=f32)[:, None, None]
    pw_re = jnp.exp(j * lre * dt) * jnp.cos(j * ang)
    pw_im = jnp.exp(j * lre * dt) * jnp.sin(j * ang)
    eye = jnp.eye(SSM_GROUPS, dtype=f32)
    cr, ci = c_re.astype(f32), c_im.astype(f32)
    ab_re = pw_re[..., None] * bb_re - pw_im[..., None] * bb_im
    ab_im = pw_re[..., None] * bb_im + pw_im[..., None] * bb_re
    kern = (jnp.einsum('gcp,jgpd->jgcd', cr, ab_re[:n]) - jnp.einsum('gcp,jgpd->jgcd', ci, ab_im[:n]))
    lag = jnp.arange(n)[None, :] - jnp.arange(n)[:, None]
    toep = jnp.where((lag >= 0)[:, :, None, None, None], kern[jnp.maximum(lag, 0)], 0.0)
    t_mat = jnp.einsum('abgcd,gh->agdbhc', toep, eye).reshape(S5_CHUNK_WIDTH, S5_CHUNK_WIDTH)
    v = jnp.stack([ab_re[:n][::-1], ab_im[:n][::-1]])
    w_in = jnp.einsum('rlgpc,gh->lhcrgp', v, eye).reshape(S5_CHUNK_WIDTH, 2 * S5_STATE_WIDTH)
    ar, ai = pw_re[1:], pw_im[1:]
    wo_re = jnp.einsum('gcp,lgp->gplc', cr, ar) - jnp.einsum('gcp,lgp->gplc', ci, ai)
    wo_im = -jnp.einsum('gcp,lgp->gplc', cr, ai) - jnp.einsum('gcp,lgp->gplc', ci, ar)
    w_out = jnp.einsum('rgplc,gh->rhplgc', jnp.stack([wo_re, wo_im]), eye)
    w_out = w_out.reshape(2 * S5_STATE_WIDTH, S5_CHUNK_WIDTH)
    decay = jnp.stack([pw_re[n].reshape(1, S5_STATE_WIDTH), pw_im[n].reshape(1, S5_STATE_WIDTH)])
    return (jnp.concatenate([t_mat, w_in], axis=1).astype(MXU_DTYPE), w_out.astype(MXU_DTYPE), decay)


def _s5_scan_body(u_ref, tw_ref, wout_ref, decay_ref, y_ref, h_ref, s_ref, hs_ref, *, tm):
    f32 = jnp.float32
    sw = S5_STATE_WIDTH

    @pl.when(pl.program_id(1) == 0)
    def _():
        h_ref[...] = jnp.zeros_like(h_ref)

    r = jnp.dot(u_ref[0].astype(MXU_DTYPE), tw_ref[...], preferred_element_type=f32)
    s_ref[...] = r[:, S5_CHUNK_WIDTH:]
    d_re = decay_ref[0]
    d_im = decay_ref[1]

    def eight_chunks(k, carry):
        h_re, h_im = carry
        r0 = pl.multiple_of(k * 8, 8)
        inc = s_ref[pl.ds(r0, 8), :]
        rows_re, rows_im = [], []
        for t in range(8):
            rows_re.append(h_re)
            rows_im.append(h_im)
            h_re, h_im = (d_re * h_re - d_im * h_im + inc[t:t + 1, :sw],
                          d_re * h_im + d_im * h_re + inc[t:t + 1, sw:])
        hs_ref[pl.ds(r0, 8), :] = jnp.concatenate(
            [jnp.concatenate(rows_re, axis=0), jnp.concatenate(rows_im, axis=0)], axis=1)
        return h_re, h_im

    h_re, h_im = lax.fori_loop(0, tm // 8, eight_chunks, (h_ref[0:1, :], h_ref[1:2, :]))
    h_ref[0:1, :] = h_re
    h_ref[1:2, :] = h_im
    y_ref[0] = r[:, :S5_CHUNK_WIDTH] + jnp.dot(hs_ref[...].astype(MXU_DTYPE), wout_ref[...],
                                                preferred_element_type=f32)


def s5_scan(u, tw, w_out, decay, *, tm=256):
    bsz, seq, _ = u.shape
    n_chunks = seq // S5_CHUNK
    tm = min(tm, n_chunks)
    assert seq % S5_CHUNK == 0 and n_chunks % tm == 0 and tm % 8 == 0
    once = pl.Buffered(1)
    y = pl.pallas_call(
        functools.partial(_s5_scan_body, tm=tm),
        grid=(bsz, n_chunks // tm),
        in_specs=[pl.BlockSpec((1, tm, S5_CHUNK_WIDTH), lambda b, i: (b, i, 0)),
                  pl.BlockSpec(tw.shape, lambda b, i: (0, 0), pipeline_mode=once),
                  pl.BlockSpec(w_out.shape, lambda b, i: (0, 0), pipeline_mode=once),
                  pl.BlockSpec(decay.shape, lambda b, i: (0, 0, 0), pipeline_mode=once)],
        out_specs=pl.BlockSpec((1, tm, S5_CHUNK_WIDTH), lambda b, i: (b, i, 0)),
        out_shape=jax.ShapeDtypeStruct((bsz, n_chunks, S5_CHUNK_WIDTH), jnp.float32),
        scratch_shapes=[pltpu.VMEM((2, S5_STATE_WIDTH), jnp.float32),
                        pltpu.VMEM((tm, 2 * S5_STATE_WIDTH), jnp.float32),
                        pltpu.VMEM((tm, 2 * S5_STATE_WIDTH), jnp.float32)],
        compiler_params=pltpu.CompilerParams(
            dimension_semantics=("parallel", "arbitrary"), vmem_limit_bytes=VMEM_LIMIT_BYTES),
        name="s5_scan",
    )(u.reshape(bsz, n_chunks, S5_CHUNK_WIDTH), tw, w_out, decay)
    return y.reshape(bsz, seq, SSM_WIDTH)


def _s5_out_body(y_ref, u_ref, d_ref, wglu_ref, bglu_ref, wout_ref, o_ref):
    f32 = jnp.float32
    y = jax.nn.gelu(y_ref[...] + d_ref[...] * u_ref[...])
    gate = jnp.dot(y.astype(MXU_DTYPE), wglu_ref[...], preferred_element_type=f32) + bglu_ref[...]
    y = y * jax.nn.sigmoid(gate)
    o_ref[...] = jnp.dot(y.astype(MXU_DTYPE), wout_ref[...], preferred_element_type=f32)


def s5_output(y, u, d_skip, w_glu, b_glu, w_out, *, tm=1024):
    m, c = y.shape
    tm = min(tm, m)
    row = lambda i: (i, 0)
    full = lambda i: (0, 0)
    return pl.pallas_call(
        _s5_out_body,
        grid=(m // tm,),
        in_specs=[pl.BlockSpec((tm, c), row), pl.BlockSpec((tm, c), row), pl.BlockSpec((1, c), full),
                  pl.BlockSpec((c, c), full), pl.BlockSpec((1, c), full), pl.BlockSpec((c, D_MODEL), full)],
        out_specs=pl.BlockSpec((tm, D_MODEL), row),
        out_shape=jax.ShapeDtypeStruct((m, D_MODEL), jnp.float32),
        compiler_params=pltpu.CompilerParams(
            dimension_semantics=("parallel",), vmem_limit_bytes=VMEM_LIMIT_BYTES),
        name="s5_output",
    )(y, u, d_skip.reshape(1, c), w_glu.astype(MXU_DTYPE), b_glu.reshape(1, c), w_out.astype(MXU_DTYPE))


def s5_branch(u, log_dt, lam_re, lam_im, b_re, b_im, c_re, c_im, d_skip, w_glu, b_glu, w_out):
    bsz, seq, _ = u.shape
    tw, w_state_out, decay = s5_operators(log_dt, lam_re, lam_im, b_re, b_im, c_re, c_im)
    y = s5_scan(u, tw, w_state_out, decay)
    out = s5_output(y.reshape(bsz * seq, SSM_WIDTH), u.reshape(bsz * seq, SSM_WIDTH),
                    d_skip, w_glu, b_glu, w_out)
    return out.reshape(bsz, seq, D_MODEL)


HG_CHUNK = 128
HG_LEVELS = 7
HG_KDIM = HGRN_HEADS * HGRN_DK


def _hgrn_segment_sums():
    c = HG_CHUNK
    t = np.arange(c)[:, None]
    u = np.arange(c)[None, :]
    blocks = []
    for lvl in range(1, HG_LEVELS + 1):
        m = (t >> lvl << lvl) + (1 << (lvl - 1)) - 1
        right = ((t >> (lvl - 1)) & 1) == 1
        blocks.append(np.where(right, (u > m) & (u <= t), (u > t) & (u <= m)))
    blocks.append(u <= t)
    blocks.append(u > t)
    return np.concatenate(blocks, axis=0).astype(np.float32)


def _hgrn_body(q_ref, z_ref, v_ref, g_ref, seg_ref, lb_ref, ng_ref, hmean_ref, wout_ref, o_ref, st_ref):
    f32 = jnp.float32
    c = HG_CHUNK

    @pl.when(pl.program_id(1) == 0)
    def _():
        st_ref[...] = jnp.zeros_like(st_ref)

    q = q_ref[0]
    z = z_ref[0]
    v = v_ref[0]
    lb = lb_ref[...]
    f = lb + (1.0 - lb) * jax.nn.sigmoid(z)
    logf = jnp.log(jnp.maximum(f, F_MIN))
    kin = (1.0 - lb) * jax.nn.sigmoid(-z)

    p1 = logf.astype(MXU_DTYPE)
    r1 = logf - p1.astype(f32)
    p2 = r1.astype(MXU_DTYPE)
    p3 = (r1 - p2.astype(f32)).astype(MXU_DTYPE)
    seg = seg_ref[...]
    sums = (jnp.dot(seg, p1, preferred_element_type=f32) + jnp.dot(seg, p2, preferred_element_type=f32)
            + jnp.dot(seg, p3, preferred_element_type=f32))

    lane_head = lax.broadcasted_iota(jnp.int32, (c, HG_KDIM), 1) // HGRN_DK
    tok = lax.broadcasted_iota(jnp.int32, (c, HG_KDIM), 0)
    row_t = lax.broadcasted_iota(jnp.int32, (HGRN_HEADS * c, c), 0) % c
    col_s = lax.broadcasted_iota(jnp.int32, (HGRN_HEADS * c, c), 1)

    def per_head_rows(x):
        return jnp.concatenate([jnp.where(lane_head == h, x, 0.0) for h in range(HGRN_HEADS)],
                               axis=0).astype(MXU_DTYPE)

    def scores(ql, kl):
        return lax.dot_general(per_head_rows(ql), kl.astype(MXU_DTYPE), (((1,), (1,)), ((), ())),
                               preferred_element_type=f32)

    att = jnp.where(row_t == col_s, scores(q, kin), 0.0)
    for lvl in range(1, HG_LEVELS + 1):
        decay = jnp.exp(sums[(lvl - 1) * c:lvl * c])
        right = ((tok >> (lvl - 1)) & 1) == 1
        a = scores(jnp.where(right, q * decay, 0.0), jnp.where(right, 0.0, kin * decay))
        att = att + jnp.where((row_t >> lvl) == (col_s >> lvl), a, 0.0)

    b = sums[HG_LEVELS * c:(HG_LEVELS + 1) * c]
    tail = sums[(HG_LEVELS + 1) * c:(HG_LEVELS + 2) * c]
    v_m = v.astype(MXU_DTYPE)
    st = st_ref[...]
    o = lax.dot_general((q * jnp.exp(b)).astype(MXU_DTYPE), st.astype(MXU_DTYPE),
                        (((1,), (1,)), ((), ())), preferred_element_type=f32)
    for h in range(HGRN_HEADS):
        o_h = jnp.dot(att[h * c:(h + 1) * c].astype(MXU_DTYPE), v_m, preferred_element_type=f32)
        o = o + jnp.where(lane_head == h, o_h, 0.0)

    kv = jnp.dot(v.T.astype(MXU_DTYPE), (kin * jnp.exp(tail)).astype(MXU_DTYPE), preferred_element_type=f32)
    sr = lax.broadcasted_iota(jnp.int32, st.shape, 0) // HGRN_DV
    sc = lax.broadcasted_iota(jnp.int32, st.shape, 1) // HGRN_DK
    st_ref[...] = st * jnp.exp(b[c - 1:c, :]) + jnp.where(sr == sc, kv, 0.0)

    o2 = o * o
    o2_hi = o2.astype(MXU_DTYPE)
    o2_lo = (o2 - o2_hi.astype(f32)).astype(MXU_DTYPE)
    ms = (jnp.dot(o2_hi, hmean_ref[...], preferred_element_type=f32)
          + jnp.dot(o2_lo, hmean_ref[...], preferred_element_type=f32))
    g = g_ref[0]
    out = o * lax.rsqrt(ms + RMS_EPS) * ng_ref[...] * (g * jax.nn.sigmoid(g))
    o_ref[0] = jnp.dot(out.astype(MXU_DTYPE), wout_ref[...], preferred_element_type=f32)


def hgrn2_branch(hg_in, lower_bound, norm_g, w_out):
    bsz, seq, _ = hg_in.shape
    assert seq % HG_CHUNK == 0 and HGRN_DK == HGRN_DV and HG_IN_WIDTH == 4 * HG_KDIM
    seg = jnp.asarray(_hgrn_segment_sums(), MXU_DTYPE)
    head_mean = jnp.asarray(np.kron(np.eye(HGRN_HEADS), np.full((HGRN_DV, HGRN_DV), 1.0 / HGRN_DV)), MXU_DTYPE)
    tok = lambda b, i: (b, i, 0)
    full = lambda b, i: (0, 0)
    part = lambda k: pl.BlockSpec((1, HG_CHUNK, HG_KDIM), lambda b, i: (b, i, k))
    return pl.pallas_call(
        _hgrn_body,
        grid=(bsz, seq // HG_CHUNK),
        in_specs=[part(0), part(1), part(2), part(3),
                  pl.BlockSpec(seg.shape, full), pl.BlockSpec((1, HG_KDIM), full),
                  pl.BlockSpec((1, HGRN_WIDTH), full), pl.BlockSpec(head_mean.shape, full),
                  pl.BlockSpec((HGRN_WIDTH, D_MODEL), full)],
        out_specs=pl.BlockSpec((1, HG_CHUNK, D_MODEL), tok),
        out_shape=jax.ShapeDtypeStruct((bsz, seq, D_MODEL), jnp.float32),
        scratch_shapes=[pltpu.VMEM((HGRN_WIDTH, HG_KDIM), jnp.float32)],
        compiler_params=pltpu.CompilerParams(
            dimension_semantics=("parallel", "arbitrary"), vmem_limit_bytes=VMEM_LIMIT_BYTES),
        name="hgrn2",
    )(hg_in, hg_in, hg_in, hg_in, seg, lower_bound.reshape(1, HG_KDIM).astype(jnp.float32),
      jnp.tile(norm_g.astype(jnp.float32), HGRN_HEADS).reshape(1, HGRN_WIDTH), head_mean,
      w_out.astype(MXU_DTYPE))


ATT_QK_DIM = LANE
DSA_TILE = 512


def _rms_rows(x, g):
    return x * lax.rsqrt(jnp.mean(x * x, axis=-1, keepdims=True) + RMS_EPS) * g


def _dsa_project_body(x_ref, gq_ref, gkv_ref, wq_ref, wqi_ref, wkt_ref, wv_ref,
                      q_ref, qi_ref, kt_ref, v_ref, kidt_ref):
    f32 = jnp.float32
    tm = x_ref.shape[1]
    x = x_ref[0]
    cq = _rms_rows(x[:, :ATT_Q_RANK], gq_ref[...]).astype(MXU_DTYPE)
    ckv = _rms_rows(x[:, ATT_Q_RANK:ATT_Q_RANK + ATT_KV_RANK], gkv_ref[...])
    q_all = jnp.dot(cq, wq_ref[...], preferred_element_type=f32)
    qi_all = jnp.dot(cq, wqi_ref[...], preferred_element_type=f32)
    v_all = jnp.dot(ckv.astype(MXU_DTYPE), wv_ref[...], preferred_element_type=f32)
    kt_all = jnp.dot(wkt_ref[...], ckv.T.astype(MXU_DTYPE), preferred_element_type=f32)

    lane = lax.broadcasted_iota(jnp.int32, (tm, LANE), 1)
    t_loc = lax.broadcasted_iota(jnp.int32, (tm, LANE), 0)
    t_even = (t_loc // 2 * 2).astype(f32)
    t_odd = (t_loc % 2).astype(f32)
    sub = lax.broadcasted_iota(jnp.int32, (LANE, tm), 0)
    s_loc = lax.broadcasted_iota(jnp.int32, (LANE, tm), 1)
    k_rows = jnp.where(sub < ATT_HEAD_DIM + 2, 1.0,
                       jnp.where(sub == ATT_HEAD_DIM + 2, (s_loc // 2 * 2).astype(f32), (s_loc % 2).astype(f32)))
    for h in range(ATT_HEADS):
        slope = 2.0 ** (-8.0 * (h + 1) / ATT_HEADS)
        q_cols = jnp.where(lane == ATT_HEAD_DIM, -slope * t_even,
                           jnp.where(lane == ATT_HEAD_DIM + 1, -slope * t_odd, slope))
        q_h = q_all[:, h * LANE:(h + 1) * LANE]
        q_ref[0, h] = jnp.where(lane < ATT_HEAD_DIM, q_h,
                                jnp.where(lane < ATT_HEAD_DIM + 4, q_cols, 0.0)).astype(q_ref.dtype)
        k_h = kt_all[h * LANE:(h + 1) * LANE, :]
        kt_ref[0, h] = jnp.where(sub < ATT_HEAD_DIM, k_h,
                                 jnp.where(sub < ATT_HEAD_DIM + 4, k_rows, 0.0)).astype(kt_ref.dtype)
        v_h = v_all[:, h * LANE:(h + 1) * LANE]
        v_ref[0, h] = jnp.where(lane == ATT_HEAD_DIM, 1.0, v_h).astype(v_ref.dtype)
    for h in range(IDX_HEADS):
        qi_ref[0, h] = qi_all[:, h * LANE:(h + 1) * LANE].astype(qi_ref.dtype)
    kidt_ref[0] = x[:, ATT_IDX_BLOCK * LANE:(ATT_IDX_BLOCK + 1) * LANE].T.astype(kidt_ref.dtype)


def _head_padded(w, n_heads, dim):
    k = w.shape[0]
    return jnp.pad(w.reshape(k, n_heads, dim), ((0, 0), (0, 0), (0, LANE - dim))).reshape(k, n_heads * LANE)


def dsa_project(att_in, q_norm_g, kv_norm_g, w_uq, w_qidx, w_ukv):
    bsz, seq, _ = att_in.shape
    tm = DSA_TILE
    assert seq % tm == 0 and 8 % ATT_HEADS == 0 and tm <= 512
    w_q = _head_padded(w_uq * ATT_HEAD_DIM ** -0.5, ATT_HEADS, ATT_HEAD_DIM).astype(MXU_DTYPE)
    w_qi = _head_padded(w_qidx * IDX_DIM ** -0.5, IDX_HEADS, IDX_DIM).astype(MXU_DTYPE)
    w_kt = _head_padded(w_ukv[:, :ATT_WIDTH], ATT_HEADS, ATT_HEAD_DIM).T.astype(MXU_DTYPE)
    w_v = _head_padded(w_ukv[:, ATT_WIDTH:], ATT_HEADS, ATT_HEAD_DIM).astype(MXU_DTYPE)
    tok = lambda b, i: (b, 0, i, 0)
    full = lambda b, i: (0, 0)
    dt = MXU_DTYPE
    return pl.pallas_call(
        _dsa_project_body,
        grid=(bsz, seq // tm),
        in_specs=[pl.BlockSpec((1, tm, ATT_IN_WIDTH), lambda b, i: (b, i, 0)),
                  pl.BlockSpec((1, ATT_Q_RANK), full), pl.BlockSpec((1, ATT_KV_RANK), full),
                  pl.BlockSpec(w_q.shape, full), pl.BlockSpec(w_qi.shape, full),
                  pl.BlockSpec(w_kt.shape, full), pl.BlockSpec(w_v.shape, full)],
        out_specs=[pl.BlockSpec((1, ATT_HEADS, tm, LANE), tok),
                   pl.BlockSpec((1, IDX_HEADS, tm, LANE), tok),
                   pl.BlockSpec((1, ATT_HEADS, LANE, tm), lambda b, i: (b, 0, 0, i)),
                   pl.BlockSpec((1, ATT_HEADS, tm, LANE), tok),
                   pl.BlockSpec((1, LANE, tm), lambda b, i: (b, 0, i))],
        out_shape=[jax.ShapeDtypeStruct((bsz, ATT_HEADS, seq, LANE), dt),
                   jax.ShapeDtypeStruct((bsz, IDX_HEADS, seq, LANE), dt),
                   jax.ShapeDtypeStruct((bsz, ATT_HEADS, LANE, seq), dt),
                   jax.ShapeDtypeStruct((bsz, ATT_HEADS, seq, LANE), dt),
                   jax.ShapeDtypeStruct((bsz, LANE, seq), dt)],
        compiler_params=pltpu.CompilerParams(
            dimension_semantics=("parallel", "parallel"), vmem_limit_bytes=VMEM_LIMIT_BYTES),
        name="dsa_project",
    )(att_in, q_norm_g.reshape(1, ATT_Q_RANK), kv_norm_g.reshape(1, ATT_KV_RANK), w_q, w_qi, w_kt, w_v)


_INT_MIN = -2 ** 31
_MASK_KEY = int(np.float32(MASK_VALUE).view(np.int32)) ^ 0x7FFFFFFF
_SEARCH_ROWS = 128
_ATT_ROWS = 64
_SEARCH_LANES = 512


def _dsa_body(qi_ref, kj_ref, qidx_ref, w_ref, kidt_ref, q_ref, kt_ref, v_ref, tri_ref, wout_ref,
              o_ref, keys_ref, thr_ref, need_ref, carry_ref, m_ref, acc_ref, bias_ref, s_ref, p_ref,
              alpha_ref, *, tq, tk, seq, n_sel):
    f32 = jnp.float32
    p_id = pl.program_id(1)
    i = qi_ref[p_id]
    j = kj_ref[p_id]
    row_local = lax.broadcasted_iota(jnp.int32, (tq, tk), 0)
    col_local = lax.broadcasted_iota(jnp.int32, (tq, tk), 1)

    @pl.when(j == 0)
    def _select():
        def score_block(jj, carry):
            off = pl.multiple_of(jj * tk, tk)
            kb = kidt_ref[0, :, pl.ds(off, tk)]
            sc = jnp.zeros((tq, tk), f32)
            for h in range(IDX_HEADS):
                s = jnp.dot(qidx_ref[0, h], kb, preferred_element_type=f32)
                w_h = w_ref[0, :, IDX_DIM + h:IDX_DIM + h + 1] * IDX_HEADS ** -0.5
                sc = sc + jnp.maximum(s, 0.0) * w_h
            causal = (col_local + jj * tk) <= (row_local + i * tq)
            sc = jnp.where(causal, sc, MASK_VALUE)
            sc = jnp.where(sc == 0.0, 0.0, sc)
            bits = lax.bitcast_convert_type(sc, jnp.int32)
            keys_ref[:, pl.ds(off, tk)] = jnp.where(bits < 0, bits ^ 0x7FFFFFFF, bits)
            return carry

        lax.fori_loop(0, i + 1, score_block, 0)

        n_proc = (i + 1) * tk
        n_iter = n_proc // _SEARCH_LANES
        n_masked_tail = (seq - n_proc).astype(f32)

        def row_group(r, carry):
            r0 = pl.multiple_of(r * _SEARCH_ROWS, _SEARCH_ROWS)

            def count_ge(cand):
                cand_b = jnp.broadcast_to(cand, (_SEARCH_ROWS, LANE))

                def chunk(c, acc):
                    base = pl.multiple_of(c * _SEARCH_LANES, _SEARCH_LANES)
                    for u in range(_SEARCH_LANES // LANE):
                        kk = keys_ref[pl.ds(r0, _SEARCH_ROWS), pl.ds(base + u * LANE, LANE)]
                        acc = acc + jnp.where(kk >= cand_b, 1.0, 0.0)
                    return acc

                acc = lax.fori_loop(0, n_iter, chunk, jnp.zeros((_SEARCH_ROWS, LANE), f32))
                cnt = jnp.sum(acc, axis=1, keepdims=True)
                return cnt + jnp.where(cand <= _MASK_KEY, n_masked_tail, 0.0)

            zero = jnp.zeros((_SEARCH_ROWS, 1), jnp.int32)
            v0 = jnp.where(count_ge(zero) >= n_sel, zero, zero + _INT_MIN)

            def bit_step(b, v):
                cand = v | jnp.left_shift(jnp.int32(1), 30 - b)
                return jnp.where(count_ge(cand) >= n_sel, cand, v)

            v = lax.fori_loop(0, 31, bit_step, v0)
            thr_ref[pl.ds(r0, _SEARCH_ROWS), :] = v
            need_ref[pl.ds(r0, _SEARCH_ROWS), :] = n_sel - count_ge(v + 1)
            return carry

        lax.fori_loop(0, tq // _SEARCH_ROWS, row_group, 0)
        carry_ref[...] = jnp.zeros_like(carry_ref)
        m_ref[...] = jnp.full_like(m_ref, MASK_VALUE)
        acc_ref[...] = jnp.zeros_like(acc_ref)

    keys_blk = keys_ref[:, pl.ds(pl.multiple_of(j * tk, tk), tk)]
    thr = thr_ref[...]
    eq = keys_blk == thr
    tie_rank = carry_ref[...] + jnp.dot(jnp.where(eq, 1.0, 0.0).astype(MXU_DTYPE), tri_ref[...],
                                         preferred_element_type=f32)
    carry_ref[...] = tie_rank[:, tk - 1:tk]
    sel = (keys_blk > thr) | (eq & (tie_rank <= need_ref[...]))
    causal = (row_local + i * tq) >= (col_local + j * tk)
    bias_ref[...] = jnp.where(sel & causal, 0.0, MASK_VALUE)

    def logits(h, slot):
        s_ref[slot] = jnp.dot(q_ref[0, h], kt_ref[0, h], preferred_element_type=f32)

    def softmax(h, slot):
        slope_tk = jnp.right_shift(tk, h + 1).astype(f32)
        for r in range(0, tq, _ATT_ROWS):
            rows = pl.ds(r, _ATT_ROWS)
            s = s_ref[slot, rows, :] + bias_ref[rows, :]
            m_old = m_ref[h, rows, :] - slope_tk
            m_new = jnp.maximum(m_old, jnp.max(s, axis=1, keepdims=True))
            p_ref[slot, rows, :] = jnp.exp(s - m_new).astype(MXU_DTYPE)
            alpha_ref[slot, rows, :] = jnp.exp(m_old - m_new)
            m_ref[h, rows, :] = m_new

    def weighted_values(h, slot):
        acc_ref[h] = alpha_ref[slot] * acc_ref[h] + jnp.dot(p_ref[slot], v_ref[0, h],
                                                             preferred_element_type=f32)

    for t in range(ATT_HEADS + 2):
        if t < ATT_HEADS:
            logits(t, t % 2)
        if 1 <= t <= ATT_HEADS:
            softmax(t - 1, (t - 1) % 2)
        if t >= 2:
            weighted_values(t - 2, t % 2)

    @pl.when(j == i)
    def _finish():
        heads = []
        for h in range(ATT_HEADS):
            a = acc_ref[h]
            heads.append((a[:, :ATT_HEAD_DIM] / a[:, ATT_HEAD_DIM:ATT_HEAD_DIM + 1]).astype(MXU_DTYPE))
        o_ref[0] = jnp.dot(jnp.concatenate(heads, axis=1), wout_ref[...], preferred_element_type=f32)


def dsa_attention(att_in, q_idx, kid_t, q, k_t, v_aug, w_out):
    bsz, _, seq, _ = q.shape
    tq = tk = DSA_TILE
    n_sel = min(TOPK_MAX, seq // 4)
    nq = seq // tq
    assert seq % tq == 0 and tk % _SEARCH_LANES == 0 and tq % _SEARCH_ROWS == 0
    pairs = [(a, b) for a in range(nq) for b in range(a + 1)]
    qi = jnp.asarray([a for a, _ in pairs], jnp.int32)
    kj = jnp.asarray([b for _, b in pairs], jnp.int32)
    tri = jnp.triu(jnp.ones((tk, tk), MXU_DTYPE))
    grid_spec = pltpu.PrefetchScalarGridSpec(
        num_scalar_prefetch=2,
        grid=(bsz, len(pairs)),
        in_specs=[
            pl.BlockSpec((1, IDX_HEADS, tq, LANE), lambda b, p, qi, kj: (b, 0, qi[p], 0)),
            pl.BlockSpec((1, tq, LANE), lambda b, p, qi, kj: (b, qi[p], ATT_IDX_BLOCK)),
            pl.BlockSpec((1, LANE, seq), lambda b, p, qi, kj: (b, 0, 0)),
            pl.BlockSpec((1, ATT_HEADS, tq, ATT_QK_DIM), lambda b, p, qi, kj: (b, 0, qi[p], 0)),
            pl.BlockSpec((1, ATT_HEADS, ATT_QK_DIM, tk), lambda b, p, qi, kj: (b, 0, 0, kj[p])),
            pl.BlockSpec((1, ATT_HEADS, tk, LANE), lambda b, p, qi, kj: (b, 0, kj[p], 0)),
            pl.BlockSpec((tk, tk), lambda b, p, qi, kj: (0, 0)),
            pl.BlockSpec((ATT_WIDTH, D_MODEL), lambda b, p, qi, kj: (0, 0)),
        ],
        out_specs=pl.BlockSpec((1, tq, D_MODEL), lambda b, p, qi, kj: (b, qi[p], 0)),
        scratch_shapes=[
            pltpu.VMEM((tq, seq), jnp.int32),
            pltpu.VMEM((tq, 1), jnp.int32),
            pltpu.VMEM((tq, 1), jnp.float32),
            pltpu.VMEM((tq, 1), jnp.float32),
            pltpu.VMEM((ATT_HEADS, tq, 1), jnp.float32),
            pltpu.VMEM((ATT_HEADS, tq, LANE), jnp.float32),
            pltpu.VMEM((tq, tk), jnp.float32),
            pltpu.VMEM((2, tq, tk), jnp.float32),
            pltpu.VMEM((2, tq, tk), MXU_DTYPE),
            pltpu.VMEM((2, tq, 1), jnp.float32),
        ])
    return pl.pallas_call(
        functools.partial(_dsa_body, tq=tq, tk=tk, seq=seq, n_sel=n_sel),
        grid_spec=grid_spec,
        out_shape=jax.ShapeDtypeStruct((bsz, seq, D_MODEL), jnp.float32),
        compiler_params=pltpu.CompilerParams(
            dimension_semantics=("parallel", "arbitrary"), vmem_limit_bytes=VMEM_LIMIT_BYTES),
        name="dsa_attention",
    )(qi, kj, q_idx, att_in, kid_t, q, k_t, v_aug, tri, w_out)


def dsa_branch(att_in, q_norm_g, kv_norm_g, w_uq, w_qidx, w_ukv, w_out):
    q, q_idx, k_t, v_aug, kid_t = dsa_project(att_in, q_norm_g, kv_norm_g, w_uq, w_qidx, w_ukv)
    return dsa_attention(att_in, q_idx, kid_t, q, k_t, v_aug, w_out.astype(MXU_DTYPE))


MOE_TOKENS = 1024
MOE_ROW_CLASSES = (256, 320, 512, 1024)
MOE_VMEM_LIMIT_BYTES = 56 * 1024 * 1024


def _moe_body(cnt_ref, h_ref, gate_ref, gatet_ref, tri_ref, wg_ref, wu_ref, wd_ref, g_ref, b_ref, o_ref,
              xb_ref, gather_ref, scatter_ref, xc_ref, yc_ref, acc_ref):
    f32 = jnp.float32
    tm = MOE_TOKENS
    i = pl.program_id(0)
    e = pl.program_id(1)
    f = pl.program_id(2)
    last_f = pl.num_programs(2) - 1

    @pl.when((e == 0) & (f == 0))
    def _():
        xb_ref[...] = h_ref[...].astype(MXU_DTYPE)
        acc_ref[...] = jnp.zeros_like(acc_ref)

    lane = lax.broadcasted_iota(jnp.int32, gate_ref.shape, 1)
    gate_col = jnp.sum(jnp.where(lane == e, gate_ref[...], 0.0), axis=-1, keepdims=True)

    def routed(size):
        def run():
            @pl.when(f == 0)
            def _():
                active_row = jnp.where(gatet_ref[pl.ds(e, 1), :] != 0.0, 1.0, 0.0)
                rank_row = jnp.dot(jnp.broadcast_to(active_row, (8, tm)).astype(MXU_DTYPE), tri_ref[...],
                                   preferred_element_type=f32)[0:1]
                slot = lax.broadcasted_iota(jnp.int32, (size, tm), 0).astype(f32)
                gather = jnp.where((slot == rank_row) & (active_row != 0.0), 1.0, 0.0)
                gather_ref[0:size, :] = gather.astype(MXU_DTYPE)
                rank_col = jnp.broadcast_to(rank_row, (LANE, tm)).T[:, 0:1]
                slot_l = lax.broadcasted_iota(jnp.int32, (tm, size), 1).astype(f32)
                scatter = jnp.where((slot_l == rank_col) & (gate_col != 0.0), 1.0, 0.0)
                scatter_ref[:, 0:size] = scatter.astype(MXU_DTYPE)
                xc_ref[0:size, :] = jnp.dot(gather.astype(MXU_DTYPE), xb_ref[...],
                                            preferred_element_type=f32).astype(MXU_DTYPE)
                yc_ref[0:size, :] = jnp.zeros((size, D_MODEL), f32)

            xc = xc_ref[0:size, :]
            a = jnp.dot(xc, wg_ref[0], preferred_element_type=f32)
            u = jnp.dot(xc, wu_ref[0], preferred_element_type=f32)
            act = a * jax.nn.sigmoid(a) * u
            yc_ref[0:size, :] += jnp.dot(act.astype(MXU_DTYPE), wd_ref[0], preferred_element_type=f32)

            @pl.when(f == last_f)
            def _():
                back = jnp.dot(scatter_ref[:, 0:size], yc_ref[0:size, :].astype(MXU_DTYPE),
                               preferred_element_type=f32)
                acc_ref[...] += gate_col * back
        return run

    count = cnt_ref[i * N_EXPERTS + e]
    size_class = sum((count > s).astype(jnp.int32) for s in MOE_ROW_CLASSES[:-1])
    for k, size in enumerate(MOE_ROW_CLASSES):
        pl.when((size_class == k) & (count > 0))(routed(size))

    @pl.when((e == pl.num_programs(1) - 1) & (f == last_f))
    def _():
        o_ref[...] = _layer_norm_rows(DEEPNORM_ALPHA * h_ref[...] + acc_ref[...], g_ref[...], b_ref[...])


def moe_experts_norm(h, gates, wg, wu, wd, ln_g, ln_b, *, tf=512):
    m, d = h.shape
    n_experts, _, f_dim = wg.shape
    tm = MOE_TOKENS
    assert m % tm == 0 and f_dim % tf == 0 and MOE_ROW_CLASSES[-1] == tm
    n_tiles = m // tm
    routed = gates[:, :n_experts] != 0.0
    counts = jnp.sum(routed.reshape(n_tiles, tm, n_experts), axis=1, dtype=jnp.int32).reshape(-1)
    gates_t = gates[:, :n_experts].T
    tri = jnp.triu(jnp.ones((tm, tm), MXU_DTYPE), k=1)
    row = lambda i, e, f, c: (i, 0)
    full = lambda i, e, f, c: (0, 0)
    grid_spec = pltpu.PrefetchScalarGridSpec(
        num_scalar_prefetch=1,
        grid=(n_tiles, n_experts, f_dim // tf),
        in_specs=[pl.BlockSpec((tm, d), row), pl.BlockSpec((tm, LANE), row),
                  pl.BlockSpec((n_experts, tm), lambda i, e, f, c: (0, i)),
                  pl.BlockSpec((tm, tm), full),
                  pl.BlockSpec((1, d, tf), lambda i, e, f, c: (e, 0, f)),
                  pl.BlockSpec((1, d, tf), lambda i, e, f, c: (e, 0, f)),
                  pl.BlockSpec((1, tf, d), lambda i, e, f, c: (e, f, 0)),
                  pl.BlockSpec((1, d), full), pl.BlockSpec((1, d), full)],
        out_specs=pl.BlockSpec((tm, d), row),
        scratch_shapes=[pltpu.VMEM((tm, d), MXU_DTYPE),
                        pltpu.VMEM((tm, tm), MXU_DTYPE),
                        pltpu.VMEM((tm, tm), MXU_DTYPE),
                        pltpu.VMEM((tm, d), MXU_DTYPE),
                        pltpu.VMEM((tm, d), jnp.float32),
                        pltpu.VMEM((tm, d), jnp.float32)])
    return pl.pallas_call(
        _moe_body,
        grid_spec=grid_spec,
        out_shape=jax.ShapeDtypeStruct((m, d), jnp.float32),
        compiler_params=pltpu.CompilerParams(
            dimension_semantics=("parallel", "arbitrary", "arbitrary"),
            vmem_limit_bytes=MOE_VMEM_LIMIT_BYTES),
        name="moe_experts_norm",
    )(counts, h, gates, gates_t, tri, wg, wu, wd, ln_g.reshape(1, d), ln_b.reshape(1, d))


def router_gates(h, router):
    logits = (h @ router).astype(jnp.float32)
    top_val, top_idx = lax.top_k(logits, TOP_K)
    top_w = jax.nn.softmax(top_val, axis=-1)
    onehot = (top_idx[..., None] == jnp.arange(N_EXPERTS)[None, None, :])
    gates = jnp.sum(jnp.where(onehot, top_w[..., None], 0.0), axis=1)
    return jnp.pad(gates, ((0, 0), (0, LANE - N_EXPERTS)))


def kernel(x, w_in, ssm_log_dt, ssm_lambda_re, ssm_lambda_im, ssm_b_re, ssm_b_im, ssm_c_re, ssm_c_im,
           ssm_d, ssm_w_glu, ssm_b_glu, ssm_w_out, hgrn_lb_logits, hgrn_norm_g, hgrn_w_out,
           attn_q_norm_g, attn_kv_norm_g, attn_w_uq, attn_w_qidx, attn_w_ukv, attn_w_out, w_o,
           ln_g, ln_b, ffn_w_gate, ffn_w_up, ffn_w_down, moe_router, moe_w_gate, moe_w_up, moe_w_down):
    bsz, seq, d = x.shape
    m = bsz * seq
    bf16 = MXU_DTYPE
    assert MIX_IN_USED + N_BRANCHES * D_MODEL == N_IN
    lb_soft = jax.nn.softmax(hgrn_lb_logits.astype(jnp.float32), axis=0)
    lower_bounds = jnp.concatenate([jnp.zeros_like(lb_soft[:1]), jnp.cumsum(lb_soft[1:], axis=0)], axis=0)
    h = x.reshape(m, d)
    for l in range(DEPTH):
        w_mix = jnp.pad(w_in[l][:, :MIX_IN_USED], ((0, 0), (0, MIX_IN_WIDTH - MIX_IN_USED))).astype(bf16)
        u, hg_in, att_in = in_proj(h, w_mix)
        y_ssm = s5_branch(u.reshape(bsz, seq, SSM_WIDTH), ssm_log_dt[l], ssm_lambda_re[l], ssm_lambda_im[l],
                          ssm_b_re[l], ssm_b_im[l], ssm_c_re[l], ssm_c_im[l], ssm_d[l], ssm_w_glu[l],
                          ssm_b_glu[l], ssm_w_out[l])
        y_hg = hgrn2_branch(hg_in.reshape(bsz, seq, HG_IN_WIDTH), lower_bounds[l], hgrn_norm_g[l],
                            hgrn_w_out[l])
        y_att = dsa_branch(att_in.reshape(bsz, seq, ATT_IN_WIDTH), attn_q_norm_g[l], attn_kv_norm_g[l],
                           attn_w_uq[l], attn_w_qidx[l], attn_w_ukv[l], attn_w_out[l])
        h = merge_project_norm(h, y_ssm.reshape(m, d), y_hg.reshape(m, d), y_att.reshape(m, d),
                               w_in[l][:, MIX_IN_USED:].astype(bf16), w_o[l].astype(bf16),
                               ln_g[l, 0], ln_b[l, 0])
        if l % 2 == 0:
            ones = jnp.ones((m, LANE), jnp.float32)
            h = swiglu_experts_norm(h, ones, ffn_w_gate[l // 2][None].astype(bf16),
                                    ffn_w_up[l // 2][None].astype(bf16),
                                    ffn_w_down[l // 2][None].astype(bf16),
                                    ln_g[l, 1], ln_b[l, 1], tf=256)
        else:
            gate_w = router_gates(h, moe_router[l // 2])
            h = moe_experts_norm(h, gate_w, moe_w_gate[l // 2].astype(bf16),
                                 moe_w_up[l // 2].astype(bf16), moe_w_down[l // 2].astype(bf16),
                                 ln_g[l, 1], ln_b[l, 1])
    return h.reshape(bsz, seq, d)
```

```python
import functools
import math

import jax
import jax.numpy as jnp
import numpy as np
from jax import lax
from jax.experimental import pallas as pl
from jax.experimental.pallas import tpu as pltpu

D_MODEL = 1024
DEPTH = 2
SSM_WIDTH = 256
SSM_GROUP = 16
SSM_GROUPS = SSM_WIDTH // SSM_GROUP
SSM_STATE = 64
HGRN_HEADS = 4
HGRN_DK = 64
HGRN_DV = 64
HGRN_WIDTH = HGRN_HEADS * HGRN_DV
HGRN_CHUNK = 64
ATT_HEADS = 8
ATT_HEAD_DIM = 64
ATT_WIDTH = ATT_HEADS * ATT_HEAD_DIM
ATT_Q_RANK = 256
ATT_KV_RANK = 128
IDX_HEADS = 4
IDX_DIM = 64
TOPK_MAX = 256
Q_BLOCK = 128
MASK_VALUE = -1e30
N_BRANCHES = 3
N_EXPERTS = 8
TOP_K = 2
DEEPNORM_ALPHA = (2 * DEPTH) ** 0.25
LN_EPS = 1e-5
RMS_EPS = 1e-6
F_MIN = 1e-12

IN_SPLITS = (SSM_WIDTH, HGRN_HEADS * HGRN_DK, HGRN_HEADS * HGRN_DK, HGRN_WIDTH, HGRN_WIDTH,
             ATT_Q_RANK, ATT_KV_RANK, IDX_DIM, IDX_HEADS, N_BRANCHES * D_MODEL)
N_IN = sum(IN_SPLITS)

VMEM_LIMIT_BYTES = 48 * 1024 * 1024
LANE = 128
MXU_DTYPE = jnp.bfloat16


def _round_up(n, m):
    return (n + m - 1) // m * m


HG_IN_WIDTH = 2 * HGRN_HEADS * HGRN_DK + 2 * HGRN_WIDTH
ATT_IN_USED = ATT_Q_RANK + ATT_KV_RANK + IDX_DIM + IDX_HEADS
ATT_IN_WIDTH = _round_up(ATT_IN_USED, LANE)
ATT_IDX_BLOCK = (ATT_Q_RANK + ATT_KV_RANK) // LANE
MIX_IN_USED = SSM_WIDTH + HG_IN_WIDTH + ATT_IN_USED
MIX_IN_WIDTH = SSM_WIDTH + HG_IN_WIDTH + ATT_IN_WIDTH


def _in_proj_body(h_ref, w_ref, u_ref, hg_ref, att_ref):
    r = jnp.dot(h_ref[...].astype(MXU_DTYPE), w_ref[...], preferred_element_type=jnp.float32)
    u_ref[...] = r[:, :SSM_WIDTH]
    hg_ref[...] = r[:, SSM_WIDTH:SSM_WIDTH + HG_IN_WIDTH]
    att_ref[...] = r[:, SSM_WIDTH + HG_IN_WIDTH:]


def in_proj(h, w_mix, *, tm=512):
    m, d = h.shape
    row = lambda i: (i, 0)
    widths = (SSM_WIDTH, HG_IN_WIDTH, ATT_IN_WIDTH)
    return pl.pallas_call(
        _in_proj_body,
        grid=(m // tm,),
        in_specs=[pl.BlockSpec((tm, d), row), pl.BlockSpec((d, MIX_IN_WIDTH), lambda i: (0, 0))],
        out_specs=[pl.BlockSpec((tm, w), row) for w in widths],
        out_shape=[jax.ShapeDtypeStruct((m, w), jnp.float32) for w in widths],
        compiler_params=pltpu.CompilerParams(
            dimension_semantics=("parallel",), vmem_limit_bytes=VMEM_LIMIT_BYTES),
        name="in_proj",
    )(h, w_mix)


def _layer_norm_rows(y, g, b):
    mu = jnp.mean(y, axis=-1, keepdims=True)
    yc = y - mu
    var = jnp.mean(yc * yc, axis=-1, keepdims=True)
    return yc * lax.rsqrt(var + LN_EPS) * g + b


def _merge_body(h_ref, ys_ref, yh_ref, ya_ref, wgate_ref, wo_ref, g_ref, b_ref, o_ref):
    d = D_MODEL
    f32 = jnp.float32
    h = h_ref[...]
    gates = jax.nn.sigmoid(jnp.dot(h.astype(MXU_DTYPE), wgate_ref[...], preferred_element_type=f32))
    mixed = (gates[:, 0:d] * ys_ref[...] + gates[:, d:2 * d] * yh_ref[...]
             + gates[:, 2 * d:3 * d] * ya_ref[...])
    mix_out = jnp.dot(mixed.astype(MXU_DTYPE), wo_ref[...], preferred_element_type=f32)
    o_ref[...] = _layer_norm_rows(DEEPNORM_ALPHA * h + mix_out, g_ref[...], b_ref[...])


def merge_project_norm(h, y_ssm, y_hg, y_att, w_gates, w_o, ln_g, ln_b, *, tm=512):
    m, d = h.shape
    row = lambda i: (i, 0)
    full = lambda i: (0, 0)
    return pl.pallas_call(
        _merge_body,
        grid=(m // tm,),
        in_specs=[pl.BlockSpec((tm, d), row), pl.BlockSpec((tm, d), row), pl.BlockSpec((tm, d), row),
                  pl.BlockSpec((tm, d), row), pl.BlockSpec((d, N_BRANCHES * d), full),
                  pl.BlockSpec((d, d), full), pl.BlockSpec((1, d), full), pl.BlockSpec((1, d), full)],
        out_specs=pl.BlockSpec((tm, d), row),
        out_shape=jax.ShapeDtypeStruct((m, d), jnp.float32),
        compiler_params=pltpu.CompilerParams(
            dimension_semantics=("parallel",), vmem_limit_bytes=VMEM_LIMIT_BYTES),
        name="merge_project_norm",
    )(h, y_ssm, y_hg, y_att, w_gates, w_o, ln_g.reshape(1, d), ln_b.reshape(1, d))


def _ffn_body(h_ref, gate_ref, wg_ref, wu_ref, wd_ref, g_ref, b_ref, o_ref, acc_ref, *, n_experts):
    e = pl.program_id(1)
    f = pl.program_id(2)

    @pl.when((e == 0) & (f == 0))
    def _():
        acc_ref[...] = jnp.zeros_like(acc_ref)

    x = h_ref[...].astype(MXU_DTYPE)
    a = jnp.dot(x, wg_ref[0], preferred_element_type=jnp.float32)
    u = jnp.dot(x, wu_ref[0], preferred_element_type=jnp.float32)
    act = a * jax.nn.sigmoid(a) * u
    part = jnp.dot(act.astype(MXU_DTYPE), wd_ref[0], preferred_element_type=jnp.float32)
    if n_experts > 1:
        lane = lax.broadcasted_iota(jnp.int32, gate_ref.shape, 1)
        gate_e = jnp.sum(jnp.where(lane == e, gate_ref[...], 0.0), axis=-1, keepdims=True)
        part = gate_e * part
    acc_ref[...] += part

    @pl.when((e == pl.num_programs(1) - 1) & (f == pl.num_programs(2) - 1))
    def _():
        o_ref[...] = _layer_norm_rows(DEEPNORM_ALPHA * h_ref[...] + acc_ref[...], g_ref[...], b_ref[...])


def swiglu_experts_norm(h, gates, wg, wu, wd, ln_g, ln_b, *, tm=1024, tf=512):
    m, d = h.shape
    n_experts, _, f_dim = wg.shape
    assert f_dim % tf == 0 and m % tm == 0
    row = lambda i, e, f: (i, 0)
    full = lambda i, e, f: (0, 0)
    return pl.pallas_call(
        functools.partial(_ffn_body, n_experts=n_experts),
        grid=(m // tm, n_experts, f_dim // tf),
        in_specs=[pl.BlockSpec((tm, d), row), pl.BlockSpec((tm, LANE), row),
                  pl.BlockSpec((1, d, tf), lambda i, e, f: (e, 0, f)),
                  pl.BlockSpec((1, d, tf), lambda i, e, f: (e, 0, f)),
                  pl.BlockSpec((1, tf, d), lambda i, e, f: (e, f, 0)),
                  pl.BlockSpec((1, d), full), pl.BlockSpec((1, d), full)],
        out_specs=pl.BlockSpec((tm, d), row),
        out_shape=jax.ShapeDtypeStruct((m, d), jnp.float32),
        scratch_shapes=[pltpu.VMEM((tm, d), jnp.float32)],
        compiler_params=pltpu.CompilerParams(
            dimension_semantics=("parallel", "arbitrary", "arbitrary"),
            vmem_limit_bytes=VMEM_LIMIT_BYTES),
        name="swiglu_experts_norm",
    )(h, gates, wg, wu, wd, ln_g.reshape(1, d), ln_b.reshape(1, d))


def rms_norm(x, g):
    xf = x.astype(jnp.float32)
    return (xf * lax.rsqrt(jnp.mean(xf * xf, axis=-1, keepdims=True) + RMS_EPS) * g).astype(x.dtype)


def _ssm_combine(e1, e2):
    a1r, a1i, b1r, b1i = e1
    a2r, a2i, b2r, b2i = e2
    return (a2r * a1r - a2i * a1i, a2r * a1i + a2i * a1r,
            a2r * b1r - a2i * b1i + b2r, a2r * b1i + a2i * b1r + b2i)


S5_CHUNK = 8
S5_CHUNK_WIDTH = S5_CHUNK * SSM_WIDTH
S5_STATE_WIDTH = SSM_GROUPS * SSM_STATE


def s5_operators(log_dt, lam_re, lam_im, b_re, b_im, c_re, c_im):
    f32 = jnp.float32
    n = S5_CHUNK
    lre, lim = lam_re.astype(f32), lam_im.astype(f32)
    dt = jnp.exp(log_dt.astype(f32))[:, None]
    mag = jnp.exp(lre * dt)
    ang = lim * dt
    a_re, a_im = mag * jnp.cos(ang), mag * jnp.sin(ang)
    den = lre * lre + lim * lim
    coef_re = ((a_re - 1.0) * lre + a_im * lim) / den
    coef_im = (a_im * lre - (a_re - 1.0) * lim) / den
    br, bi = b_re.astype(f32), b_im.astype(f32)
    bb_re = coef_re[..., None] * br - coef_im[..., None] * bi
    bb_im = coef_re[..., None] * bi + coef_im[..., None] * br
    j = jnp.arange(n + 1, dtype=f32)[:, None, None]
    pw_re = jnp.exp(j * lre * dt) * jnp.cos(j * ang)
    pw_im = jnp.exp(j * lre * dt) * jnp.sin(j * ang)
    eye = jnp.eye(SSM_GROUPS, dtype=f32)
    cr, ci = c_re.astype(f32), c_im.astype(f32)
    ab_re = pw_re[..., None] * bb_re - pw_im[..., None] * bb_im
    ab_im = pw_re[..., None] * bb_im + pw_im[..., None] * bb_re
    kern = (jnp.einsum('gcp,jgpd->jgcd', cr, ab_re[:n]) - jnp.einsum('gcp,jgpd->jgcd', ci, ab_im[:n]))
    lag = jnp.arange(n)[None, :] - jnp.arange(n)[:, None]
    toep = jnp.where((lag >= 0)[:, :, None, None, None], kern[jnp.maximum(lag, 0)], 0.0)
    t_mat = jnp.einsum('abgcd,gh->agdbhc', toep, eye).reshape(S5_CHUNK_WIDTH, S5_CHUNK_WIDTH)
    v = jnp.stack([ab_re[:n][::-1], ab_im[:n][::-1]])
    w_in = jnp.einsum('rlgpc,gh->lhcrgp', v, eye).reshape(S5_CHUNK_WIDTH, 2 * S5_STATE_WIDTH)
    ar, ai = pw_re[1:], pw_im[1:]
    wo_re = jnp.einsum('gcp,lgp->gplc', cr, ar) - jnp.einsum('gcp,lgp->gplc', ci, ai)
    wo_im = -jnp.einsum('gcp,lgp->gplc', cr, ai) - jnp.einsum('gcp,lgp->gplc', ci, ar)
    w_out = jnp.einsum('rgplc,gh->rhplgc', jnp.stack([wo_re, wo_im]), eye)
    w_out = w_out.reshape(2 * S5_STATE_WIDTH, S5_CHUNK_WIDTH)
    decay = jnp.stack([pw_re[n].reshape(1, S5_STATE_WIDTH), pw_im[n].reshape(1, S5_STATE_WIDTH)])
    return (jnp.concatenate([t_mat, w_in], axis=1).astype(MXU_DTYPE), w_out.astype(MXU_DTYPE), decay)


def _s5_scan_body(u_ref, tw_ref, wout_ref, decay_ref, y_ref, h_ref, s_ref, hs_ref, *, tm):
    f32 = jnp.float32
    sw = S5_STATE_WIDTH

    @pl.when(pl.program_id(1) == 0)
    def _():
        h_ref[...] = jnp.zeros_like(h_ref)

    r = jnp.dot(u_ref[0].astype(MXU_DTYPE), tw_ref[...], preferred_element_type=f32)
    s_ref[...] = r[:, S5_CHUNK_WIDTH:]
    d_re = decay_ref[0]
    d_im = decay_ref[1]

    def eight_chunks(k, carry):
        h_re, h_im = carry
        r0 = pl.multiple_of(k * 8, 8)
        inc = s_ref[pl.ds(r0, 8), :]
        rows_re, rows_im = [], []
        for t in range(8):
            rows_re.append(h_re)
            rows_im.append(h_im)
            h_re, h_im = (d_re * h_re - d_im * h_im + inc[t:t + 1, :sw],
                          d_re * h_im + d_im * h_re + inc[t:t + 1, sw:])
        hs_ref[pl.ds(r0, 8), :] = jnp.concatenate(
            [jnp.concatenate(rows_re, axis=0), jnp.concatenate(rows_im, axis=0)], axis=1)
        return h_re, h_im

    h_re, h_im = lax.fori_loop(0, tm // 8, eight_chunks, (h_ref[0:1, :], h_ref[1:2, :]))
    h_ref[0:1, :] = h_re
    h_ref[1:2, :] = h_im
    y_ref[0] = r[:, :S5_CHUNK_WIDTH] + jnp.dot(hs_ref[...].astype(MXU_DTYPE), wout_ref[...],
                                                preferred_element_type=f32)


def s5_scan(u, tw, w_out, decay, *, tm=256):
    bsz, seq, _ = u.shape
    n_chunks = seq // S5_CHUNK
    tm = min(tm, n_chunks)
    assert seq % S5_CHUNK == 0 and n_chunks % tm == 0 and tm % 8 == 0
    once = pl.Buffered(1)
    y = pl.pallas_call(
        functools.partial(_s5_scan_body, tm=tm),
        grid=(bsz, n_chunks // tm),
        in_specs=[pl.BlockSpec((1, tm, S5_CHUNK_WIDTH), lambda b, i: (b, i, 0)),
                  pl.BlockSpec(tw.shape, lambda b, i: (0, 0), pipeline_mode=once),
                  pl.BlockSpec(w_out.shape, lambda b, i: (0, 0), pipeline_mode=once),
                  pl.BlockSpec(decay.shape, lambda b, i: (0, 0, 0), pipeline_mode=once)],
        out_specs=pl.BlockSpec((1, tm, S5_CHUNK_WIDTH), lambda b, i: (b, i, 0)),
        out_shape=jax.ShapeDtypeStruct((bsz, n_chunks, S5_CHUNK_WIDTH), jnp.float32),
        scratch_shapes=[pltpu.VMEM((2, S5_STATE_WIDTH), jnp.float32),
                        pltpu.VMEM((tm, 2 * S5_STATE_WIDTH), jnp.float32),
                        pltpu.VMEM((tm, 2 * S5_STATE_WIDTH), jnp.float32)],
        compiler_params=pltpu.CompilerParams(
            dimension_semantics=("parallel", "arbitrary"), vmem_limit_bytes=VMEM_LIMIT_BYTES),
        name="s5_scan",
    )(u.reshape(bsz, n_chunks, S5_CHUNK_WIDTH), tw, w_out, decay)
    return y.reshape(bsz, seq, SSM_WIDTH)


def _s5_out_body(y_ref, u_ref, d_ref, wglu_ref, bglu_ref, wout_ref, o_ref):
    f32 = jnp.float32
    y = jax.nn.gelu(y_ref[...] + d_ref[...] * u_ref[...])
    gate = jnp.dot(y.astype(MXU_DTYPE), wglu_ref[...], preferred_element_type=f32) + bglu_ref[...]
    y = y * jax.nn.sigmoid(gate)
    o_ref[...] = jnp.dot(y.astype(MXU_DTYPE), wout_ref[...], preferred_element_type=f32)


def s5_output(y, u, d_skip, w_glu, b_glu, w_out, *, tm=1024):
    m, c = y.shape
    tm = min(tm, m)
    row = lambda i: (i, 0)
    full = lambda i: (0, 0)
    return pl.pallas_call(
        _s5_out_body,
        grid=(m // tm,),
        in_specs=[pl.BlockSpec((tm, c), row), pl.BlockSpec((tm, c), row), pl.BlockSpec((1, c), full),
                  pl.BlockSpec((c, c), full), pl.BlockSpec((1, c), full), pl.BlockSpec((c, D_MODEL), full)],
        out_specs=pl.BlockSpec((tm, D_MODEL), row),
        out_shape=jax.ShapeDtypeStruct((m, D_MODEL), jnp.float32),
        compiler_params=pltpu.CompilerParams(
            dimension_semantics=("parallel",), vmem_limit_bytes=VMEM_LIMIT_BYTES),
        name="s5_output",
    )(y, u, d_skip.reshape(1, c), w_glu.astype(MXU_DTYPE), b_glu.reshape(1, c), w_out.astype(MXU_DTYPE))


def s5_branch(u, log_dt, lam_re, lam_im, b_re, b_im, c_re, c_im, d_skip, w_glu, b_glu, w_out):
    bsz, seq, _ = u.shape
    tw, w_state_out, decay = s5_operators(log_dt, lam_re, lam_im, b_re, b_im, c_re, c_im)
    y = s5_scan(u, tw, w_state_out, decay)
    out = s5_output(y.reshape(bsz * seq, SSM_WIDTH), u.reshape(bsz * seq, SSM_WIDTH),
                    d_skip, w_glu, b_glu, w_out)
    return out.reshape(bsz, seq, D_MODEL)


HG_CHUNK = 128
HG_LEVELS = 7
HG_KDIM = HGRN_HEADS * HGRN_DK


def _hgrn_segment_sums():
    c = HG_CHUNK
    t = np.arange(c)[:, None]
    u = np.arange(c)[None, :]
    blocks = []
    for lvl in range(1, HG_LEVELS + 1):
        m = (t >> lvl << lvl) + (1 << (lvl - 1)) - 1
        right = ((t >> (lvl - 1)) & 1) == 1
        blocks.append(np.where(right, (u > m) & (u <= t), (u > t) & (u <= m)))
    blocks.append(u <= t)
    blocks.append(u > t)
    return np.concatenate(blocks, axis=0).astype(np.float32)


def _hgrn_body(q_ref, z_ref, v_ref, g_ref, seg_ref, lb_ref, ng_ref, hmean_ref, wout_ref, o_ref, st_ref):
    f32 = jnp.float32
    c = HG_CHUNK

    @pl.when(pl.program_id(1) == 0)
    def _():
        st_ref[...] = jnp.zeros_like(st_ref)

    q = q_ref[0]
    z = z_ref[0]
    v = v_ref[0]
    lb = lb_ref[...]
    f = lb + (1.0 - lb) * jax.nn.sigmoid(z)
    logf = jnp.log(jnp.maximum(f, F_MIN))
    kin = (1.0 - lb) * jax.nn.sigmoid(-z)

    p1 = logf.astype(MXU_DTYPE)
    r1 = logf - p1.astype(f32)
    p2 = r1.astype(MXU_DTYPE)
    p3 = (r1 - p2.astype(f32)).astype(MXU_DTYPE)
    seg = seg_ref[...]
    sums = (jnp.dot(seg, p1, preferred_element_type=f32) + jnp.dot(seg, p2, preferred_element_type=f32)
            + jnp.dot(seg, p3, preferred_element_type=f32))

    lane_head = lax.broadcasted_iota(jnp.int32, (c, HG_KDIM), 1) // HGRN_DK
    tok = lax.broadcasted_iota(jnp.int32, (c, HG_KDIM), 0)
    row_t = lax.broadcasted_iota(jnp.int32, (HGRN_HEADS * c, c), 0) % c
    col_s = lax.broadcasted_iota(jnp.int32, (HGRN_HEADS * c, c), 1)

    def per_head_rows(x):
        return jnp.concatenate([jnp.where(lane_head == h, x, 0.0) for h in range(HGRN_HEADS)],
                               axis=0).astype(MXU_DTYPE)

    def scores(ql, kl):
        return lax.dot_general(per_head_rows(ql), kl.astype(MXU_DTYPE), (((1,), (1,)), ((), ())),
                               preferred_element_type=f32)

    att = jnp.where(row_t == col_s, scores(q, kin), 0.0)
    for lvl in range(1, HG_LEVELS + 1):
        decay = jnp.exp(sums[(lvl - 1) * c:lvl * c])
        right = ((tok >> (lvl - 1)) & 1) == 1
        a = scores(jnp.where(right, q * decay, 0.0), jnp.where(right, 0.0, kin * decay))
        att = att + jnp.where((row_t >> lvl) == (col_s >> lvl), a, 0.0)

    b = sums[HG_LEVELS * c:(HG_LEVELS + 1) * c]
    tail = sums[(HG_LEVELS + 1) * c:(HG_LEVELS + 2) * c]
    v_m = v.astype(MXU_DTYPE)
    st = st_ref[...]
    o = lax.dot_general((q * jnp.exp(b)).astype(MXU_DTYPE), st.astype(MXU_DTYPE),
                        (((1,), (1,)), ((), ())), preferred_element_type=f32)
    for h in range(HGRN_HEADS):
        o_h = jnp.dot(att[h * c:(h + 1) * c].astype(MXU_DTYPE), v_m, preferred_element_type=f32)
        o = o + jnp.where(lane_head == h, o_h, 0.0)

    kv = jnp.dot(v.T.astype(MXU_DTYPE), (kin * jnp.exp(tail)).astype(MXU_DTYPE), preferred_element_type=f32)
    sr = lax.broadcasted_iota(jnp.int32, st.shape, 0) // HGRN_DV
    sc = lax.broadcasted_iota(jnp.int32, st.shape, 1) // HGRN_DK
    st_ref[...] = st * jnp.exp(b[c - 1:c, :]) + jnp.where(sr == sc, kv, 0.0)

    o2 = o * o
    o2_hi = o2.astype(MXU_DTYPE)
    o2_lo = (o2 - o2_hi.astype(f32)).astype(MXU_DTYPE)
    ms = (jnp.dot(o2_hi, hmean_ref[...], preferred_element_type=f32)
          + jnp.dot(o2_lo, hmean_ref[...], preferred_element_type=f32))
    g = g_ref[0]
    out = o * lax.rsqrt(ms + RMS_EPS) * ng_ref[...] * (g * jax.nn.sigmoid(g))
    o_ref[0] = jnp.dot(out.astype(MXU_DTYPE), wout_ref[...], preferred_element_type=f32)


def hgrn2_branch(hg_in, lower_bound, norm_g, w_out):
    bsz, seq, _ = hg_in.shape
    assert seq % HG_CHUNK == 0 and HGRN_DK == HGRN_DV and HG_IN_WIDTH == 4 * HG_KDIM
    seg = jnp.asarray(_hgrn_segment_sums(), MXU_DTYPE)
    head_mean = jnp.asarray(np.kron(np.eye(HGRN_HEADS), np.full((HGRN_DV, HGRN_DV), 1.0 / HGRN_DV)), MXU_DTYPE)
    tok = lambda b, i: (b, i, 0)
    full = lambda b, i: (0, 0)
    part = lambda k: pl.BlockSpec((1, HG_CHUNK, HG_KDIM), lambda b, i: (b, i, k))
    return pl.pallas_call(
        _hgrn_body,
        grid=(bsz, seq // HG_CHUNK),
        in_specs=[part(0), part(1), part(2), part(3),
                  pl.BlockSpec(seg.shape, full), pl.BlockSpec((1, HG_KDIM), full),
                  pl.BlockSpec((1, HGRN_WIDTH), full), pl.BlockSpec(head_mean.shape, full),
                  pl.BlockSpec((HGRN_WIDTH, D_MODEL), full)],
        out_specs=pl.BlockSpec((1, HG_CHUNK, D_MODEL), tok),
        out_shape=jax.ShapeDtypeStruct((bsz, seq, D_MODEL), jnp.float32),
        scratch_shapes=[pltpu.VMEM((HGRN_WIDTH, HG_KDIM), jnp.float32)],
        compiler_params=pltpu.CompilerParams(
            dimension_semantics=("parallel", "arbitrary"), vmem_limit_bytes=VMEM_LIMIT_BYTES),
        name="hgrn2",
    )(hg_in, hg_in, hg_in, hg_in, seg, lower_bound.reshape(1, HG_KDIM).astype(jnp.float32),
      jnp.tile(norm_g.astype(jnp.float32), HGRN_HEADS).reshape(1, HGRN_WIDTH), head_mean,
      w_out.astype(MXU_DTYPE))


ATT_QK_DIM = LANE
DSA_TILE = 512


def _rms_rows(x, g):
    return x * lax.rsqrt(jnp.mean(x * x, axis=-1, keepdims=True) + RMS_EPS) * g


def _dsa_project_body(x_ref, gq_ref, gkv_ref, wq_ref, wqi_ref, wkt_ref, wv_ref,
                      q_ref, qi_ref, kt_ref, v_ref, kidt_ref):
    f32 = jnp.float32
    tm = x_ref.shape[1]
    x = x_ref[0]
    cq = _rms_rows(x[:, :ATT_Q_RANK], gq_ref[...]).astype(MXU_DTYPE)
    ckv = _rms_rows(x[:, ATT_Q_RANK:ATT_Q_RANK + ATT_KV_RANK], gkv_ref[...])
    q_all = jnp.dot(cq, wq_ref[...], preferred_element_type=f32)
    qi_all = jnp.dot(cq, wqi_ref[...], preferred_element_type=f32)
    v_all = jnp.dot(ckv.astype(MXU_DTYPE), wv_ref[...], preferred_element_type=f32)
    kt_all = jnp.dot(wkt_ref[...], ckv.T.astype(MXU_DTYPE), preferred_element_type=f32)

    lane = lax.broadcasted_iota(jnp.int32, (tm, LANE), 1)
    t_loc = lax.broadcasted_iota(jnp.int32, (tm, LANE), 0)
    t_even = (t_loc // 2 * 2).astype(f32)
    t_odd = (t_loc % 2).astype(f32)
    sub = lax.broadcasted_iota(jnp.int32, (LANE, tm), 0)
    s_loc = lax.broadcasted_iota(jnp.int32, (LANE, tm), 1)
    k_rows = jnp.where(sub < ATT_HEAD_DIM + 2, 1.0,
                       jnp.where(sub == ATT_HEAD_DIM + 2, (s_loc // 2 * 2).astype(f32), (s_loc % 2).astype(f32)))
    for h in range(ATT_HEADS):
        slope = 2.0 ** (-8.0 * (h + 1) / ATT_HEADS)
        q_cols = jnp.where(lane == ATT_HEAD_DIM, -slope * t_even,
                           jnp.where(lane == ATT_HEAD_DIM + 1, -slope * t_odd, slope))
        q_h = q_all[:, h * LANE:(h + 1) * LANE]
        q_ref[0, h] = jnp.where(lane < ATT_HEAD_DIM, q_h,
                                jnp.where(lane < ATT_HEAD_DIM + 4, q_cols, 0.0)).astype(q_ref.dtype)
        k_h = kt_all[h * LANE:(h + 1) * LANE, :]
        kt_ref[0, h] = jnp.where(sub < ATT_HEAD_DIM, k_h,
                                 jnp.where(sub < ATT_HEAD_DIM + 4, k_rows, 0.0)).astype(kt_ref.dtype)
        v_h = v_all[:, h * LANE:(h + 1) * LANE]
        v_ref[0, h] = jnp.where(lane == ATT_HEAD_DIM, 1.0, v_h).astype(v_ref.dtype)
    for h in range(IDX_HEADS):
        qi_ref[0, h] = qi_all[:, h * LANE:(h + 1) * LANE].astype(qi_ref.dtype)
    kidt_ref[0] = x[:, ATT_IDX_BLOCK * LANE:(ATT_IDX_BLOCK + 1) * LANE].T.astype(kidt_ref.dtype)


def _head_padded(w, n_heads, dim):
    k = w.shape[0]
    return jnp.pad(w.reshape(k, n_heads, dim), ((0, 0), (0, 0), (0, LANE - dim))).reshape(k, n_heads * LANE)


def dsa_project(att_in, q_norm_g, kv_norm_g, w_uq, w_qidx, w_ukv):
    bsz, seq, _ = att_in.shape
    tm = DSA_TILE
    assert seq % tm == 0 and 8 % ATT_HEADS == 0 and tm <= 512
    w_q = _head_padded(w_uq * ATT_HEAD_DIM ** -0.5, ATT_HEADS, ATT_HEAD_DIM).astype(MXU_DTYPE)
    w_qi = _head_padded(w_qidx * IDX_DIM ** -0.5, IDX_HEADS, IDX_DIM).astype(MXU_DTYPE)
    w_kt = _head_padded(w_ukv[:, :ATT_WIDTH], ATT_HEADS, ATT_HEAD_DIM).T.astype(MXU_DTYPE)
    w_v = _head_padded(w_ukv[:, ATT_WIDTH:], ATT_HEADS, ATT_HEAD_DIM).astype(MXU_DTYPE)
    tok = lambda b, i: (b, 0, i, 0)
    full = lambda b, i: (0, 0)
    dt = MXU_DTYPE
    return pl.pallas_call(
        _dsa_project_body,
        grid=(bsz, seq // tm),
        in_specs=[pl.BlockSpec((1, tm, ATT_IN_WIDTH), lambda b, i: (b, i, 0)),
                  pl.BlockSpec((1, ATT_Q_RANK), full), pl.BlockSpec((1, ATT_KV_RANK), full),
                  pl.BlockSpec(w_q.shape, full), pl.BlockSpec(w_qi.shape, full),
                  pl.BlockSpec(w_kt.shape, full), pl.BlockSpec(w_v.shape, full)],
        out_specs=[pl.BlockSpec((1, ATT_HEADS, tm, LANE), tok),
                   pl.BlockSpec((1, IDX_HEADS, tm, LANE), tok),
                   pl.BlockSpec((1, ATT_HEADS, LANE, tm), lambda b, i: (b, 0, 0, i)),
                   pl.BlockSpec((1, ATT_HEADS, tm, LANE), tok),
                   pl.BlockSpec((1, LANE, tm), lambda b, i: (b, 0, i))],
        out_shape=[jax.ShapeDtypeStruct((bsz, ATT_HEADS, seq, LANE), dt),
                   jax.ShapeDtypeStruct((bsz, IDX_HEADS, seq, LANE), dt),
                   jax.ShapeDtypeStruct((bsz, ATT_HEADS, LANE, seq), dt),
                   jax.ShapeDtypeStruct((bsz, ATT_HEADS, seq, LANE), dt),
                   jax.ShapeDtypeStruct((bsz, LANE, seq), dt)],
        compiler_params=pltpu.CompilerParams(
            dimension_semantics=("parallel", "parallel"), vmem_limit_bytes=VMEM_LIMIT_BYTES),
        name="dsa_project",
    )(att_in, q_norm_g.reshape(1, ATT_Q_RANK), kv_norm_g.reshape(1, ATT_KV_RANK), w_q, w_qi, w_kt, w_v)


_INT_MIN = -2 ** 31
_MASK_KEY = int(np.float32(MASK_VALUE).view(np.int32)) ^ 0x7FFFFFFF
_SEARCH_ROWS = 128
_ATT_ROWS = 64
_SEARCH_LANES = 512
_LIST_DEPTH = 12
_LIST_ROWS = 16
_LIST_MIN_BLOCKS = 4


def _key_to_score(key):
    return lax.bitcast_convert_type(jnp.where(key < 0, key ^ 0x7FFFFFFF, key), jnp.float32)


def _dsa_body(qi_ref, kj_ref, qidx_ref, w_ref, kidt_ref, q_ref, kt_ref, v_ref, tri_ref, wout_ref,
              o_ref, keys_ref, thr_ref, need_ref, carry_ref, m_ref, acc_ref, bias_ref, s_ref, p_ref,
              alpha_ref, cand_ref, *, tq, tk, seq, n_sel):
    f32 = jnp.float32
    p_id = pl.program_id(1)
    i = qi_ref[p_id]
    j = kj_ref[p_id]
    row_local = lax.broadcasted_iota(jnp.int32, (tq, tk), 0)
    col_local = lax.broadcasted_iota(jnp.int32, (tq, tk), 1)

    @pl.when(j == 0)
    def _select():
        def score_block(jj, carry):
            off = pl.multiple_of(jj * tk, tk)
            kb = kidt_ref[0, :, pl.ds(off, tk)]
            sc = jnp.zeros((tq, tk), f32)
            for h in range(IDX_HEADS):
                s = jnp.dot(qidx_ref[0, h], kb, preferred_element_type=f32)
                w_h = w_ref[0, :, IDX_DIM + h:IDX_DIM + h + 1] * IDX_HEADS ** -0.5
                sc = sc + jnp.maximum(s, 0.0) * w_h
            causal = (col_local + jj * tk) <= (row_local + i * tq)
            sc = jnp.where(causal, sc, MASK_VALUE)
            keys_ref[:, pl.ds(off, tk)] = jnp.where(sc == 0.0, 0.0, sc)
            return carry

        lax.fori_loop(0, i + 1, score_block, 0)

        n_blocks = i + 1
        n_masked_tail = (seq - n_blocks * tk).astype(f32)

        def row_group(r, carry):
            r0 = pl.multiple_of(r * _SEARCH_ROWS, _SEARCH_ROWS)

            def count_ge(ref, n_iter, cand):
                cand_b = jnp.broadcast_to(_key_to_score(cand), (_SEARCH_ROWS, LANE))

                def chunk(c, acc):
                    base = pl.multiple_of(c * _SEARCH_LANES, _SEARCH_LANES)
                    for u in range(_SEARCH_LANES // LANE):
                        kk = ref[pl.ds(r0, _SEARCH_ROWS), pl.ds(base + u * LANE, LANE)]
                        acc = acc + jnp.where(kk >= cand_b, 1.0, 0.0)
                    return acc

                acc = lax.fori_loop(0, n_iter, chunk, jnp.zeros((_SEARCH_ROWS, LANE), f32))
                cnt = jnp.sum(acc, axis=1, keepdims=True)
                return cnt + jnp.where(cand <= _MASK_KEY, n_masked_tail, 0.0)

            def kth_largest_key(count):
                zero = jnp.zeros((_SEARCH_ROWS, 1), jnp.int32)
                v0 = jnp.where(count(zero) >= n_sel, zero, zero + _INT_MIN)

                def bit_step(b, v):
                    cand = v | jnp.left_shift(jnp.int32(1), 30 - b)
                    return jnp.where(count(cand) >= n_sel, cand, v)

                return lax.fori_loop(0, 31, bit_step, v0)

            count_all = functools.partial(count_ge, keys_ref, n_blocks * (tk // _SEARCH_LANES))

            def store(v, n_above):
                thr_ref[pl.ds(r0, _SEARCH_ROWS), :] = _key_to_score(v)
                need_ref[pl.ds(r0, _SEARCH_ROWS), :] = n_sel - n_above

            def search_all_keys():
                v = kth_largest_key(count_all)
                store(v, count_all(v + 1))

            @pl.when(n_blocks < _LIST_MIN_BLOCKS)
            def _():
                search_all_keys()

            @pl.when(n_blocks >= _LIST_MIN_BLOCKS)
            def _():
                def shortlist(sub, carry):
                    rr = pl.multiple_of(r0 + sub * _LIST_ROWS, _LIST_ROWS)

                    def insert_block(c, tops):
                        base = pl.multiple_of(c * tk, tk)
                        for u in range(tk // LANE):
                            x = keys_ref[pl.ds(rr, _LIST_ROWS), pl.ds(base + u * LANE, LANE)]
                            new = []
                            for t in range(_LIST_DEPTH):
                                new.append(jnp.maximum(tops[t], x))
                                x = jnp.minimum(tops[t], x)
                            tops = tuple(new)
                        return tops

                    tops = lax.fori_loop(0, n_blocks, insert_block,
                                         tuple(jnp.full((_LIST_ROWS, LANE), -jnp.inf, f32)
                                               for _ in range(_LIST_DEPTH)))
                    for t in range(_LIST_DEPTH):
                        cand_ref[pl.ds(rr, _LIST_ROWS), t * LANE:(t + 1) * LANE] = tops[t]
                    return carry

                lax.fori_loop(0, _SEARCH_ROWS // _LIST_ROWS, shortlist, 0)
                v = kth_largest_key(functools.partial(count_ge, cand_ref,
                                                      _LIST_DEPTH * LANE // _SEARCH_LANES))
                n_at_least = count_all(v)
                n_above = count_all(v + 1)
                store(v, n_above)
                exact = (n_above < n_sel) & (n_at_least >= n_sel)
                pl.when(jnp.min(jnp.where(exact, 1.0, 0.0)) < 0.5)(search_all_keys)

            return carry

        lax.fori_loop(0, tq // _SEARCH_ROWS, row_group, 0)
        carry_ref[...] = jnp.zeros_like(carry_ref)
        m_ref[...] = jnp.full_like(m_ref, MASK_VALUE)
        acc_ref[...] = jnp.zeros_like(acc_ref)

    keys_blk = keys_ref[:, pl.ds(pl.multiple_of(j * tk, tk), tk)]
    thr = thr_ref[...]
    eq = keys_blk == thr
    tie_rank = carry_ref[...] + jnp.dot(jnp.where(eq, 1.0, 0.0).astype(MXU_DTYPE), tri_ref[...],
                                         preferred_element_type=f32)
    carry_ref[...] = tie_rank[:, tk - 1:tk]
    sel = (keys_blk > thr) | (eq & (tie_rank <= need_ref[...]))
    causal = (row_local + i * tq) >= (col_local + j * tk)
    bias_ref[...] = jnp.where(sel & causal, 0.0, MASK_VALUE)

    def logits(h, slot):
        s_ref[slot] = jnp.dot(q_ref[0, h], kt_ref[0, h], preferred_element_type=f32)

    def softmax(h, slot):
        slope_tk = jnp.right_shift(tk, h + 1).astype(f32)
        for r in range(0, tq, _ATT_ROWS):
            rows = pl.ds(r, _ATT_ROWS)
            s = s_ref[slot, rows, :] + bias_ref[rows, :]
            m_old = m_ref[h, rows, :] - slope_tk
            m_new = jnp.maximum(m_old, jnp.max(s, axis=1, keepdims=True))
            p_ref[slot, rows, :] = jnp.exp(s - m_new).astype(MXU_DTYPE)
            alpha_ref[slot, rows, :] = jnp.exp(m_old - m_new)
            m_ref[h, rows, :] = m_new

    def weighted_values(h, slot):
        acc_ref[h] = alpha_ref[slot] * acc_ref[h] + jnp.dot(p_ref[slot], v_ref[0, h],
                                                             preferred_element_type=f32)

    for t in range(ATT_HEADS + 2):
        if t < ATT_HEADS:
            logits(t, t % 2)
        if 1 <= t <= ATT_HEADS:
            softmax(t - 1, (t - 1) % 2)
        if t >= 2:
            weighted_values(t - 2, t % 2)

    @pl.when(j == i)
    def _finish():
        heads = []
        for h in range(ATT_HEADS):
            a = acc_ref[h]
            heads.append((a[:, :ATT_HEAD_DIM] / a[:, ATT_HEAD_DIM:ATT_HEAD_DIM + 1]).astype(MXU_DTYPE))
        o_ref[0] = jnp.dot(jnp.concatenate(heads, axis=1), wout_ref[...], preferred_element_type=f32)


def dsa_attention(att_in, q_idx, kid_t, q, k_t, v_aug, w_out):
    bsz, _, seq, _ = q.shape
    tq = tk = DSA_TILE
    n_sel = min(TOPK_MAX, seq // 4)
    nq = seq // tq
    assert seq % tq == 0 and tk % _SEARCH_LANES == 0 and tq % _SEARCH_ROWS == 0
    pairs = [(a, b) for a in range(nq) for b in range(a + 1)]
    qi = jnp.asarray([a for a, _ in pairs], jnp.int32)
    kj = jnp.asarray([b for _, b in pairs], jnp.int32)
    tri = jnp.triu(jnp.ones((tk, tk), MXU_DTYPE))
    grid_spec = pltpu.PrefetchScalarGridSpec(
        num_scalar_prefetch=2,
        grid=(bsz, len(pairs)),
        in_specs=[
            pl.BlockSpec((1, IDX_HEADS, tq, LANE), lambda b, p, qi, kj: (b, 0, qi[p], 0)),
            pl.BlockSpec((1, tq, LANE), lambda b, p, qi, kj: (b, qi[p], ATT_IDX_BLOCK)),
            pl.BlockSpec((1, LANE, seq), lambda b, p, qi, kj: (b, 0, 0)),
            pl.BlockSpec((1, ATT_HEADS, tq, ATT_QK_DIM), lambda b, p, qi, kj: (b, 0, qi[p], 0)),
            pl.BlockSpec((1, ATT_HEADS, ATT_QK_DIM, tk), lambda b, p, qi, kj: (b, 0, 0, kj[p])),
            pl.BlockSpec((1, ATT_HEADS, tk, LANE), lambda b, p, qi, kj: (b, 0, kj[p], 0)),
            pl.BlockSpec((tk, tk), lambda b, p, qi, kj: (0, 0)),
            pl.BlockSpec((ATT_WIDTH, D_MODEL), lambda b, p, qi, kj: (0, 0)),
        ],
        out_specs=pl.BlockSpec((1, tq, D_MODEL), lambda b, p, qi, kj: (b, qi[p], 0)),
        scratch_shapes=[
            pltpu.VMEM((tq, seq), jnp.float32),
            pltpu.VMEM((tq, 1), jnp.float32),
            pltpu.VMEM((tq, 1), jnp.float32),
            pltpu.VMEM((tq, 1), jnp.float32),
            pltpu.VMEM((ATT_HEADS, tq, 1), jnp.float32),
            pltpu.VMEM((ATT_HEADS, tq, LANE), jnp.float32),
            pltpu.VMEM((tq, tk), jnp.float32),
            pltpu.VMEM((2, tq, tk), jnp.float32),
            pltpu.VMEM((2, tq, tk), MXU_DTYPE),
            pltpu.VMEM((2, tq, 1), jnp.float32),
            pltpu.VMEM((tq, _LIST_DEPTH * LANE), jnp.float32),
        ])
    return pl.pallas_call(
        functools.partial(_dsa_body, tq=tq, tk=tk, seq=seq, n_sel=n_sel),
        grid_spec=grid_spec,
        out_shape=jax.ShapeDtypeStruct((bsz, seq, D_MODEL), jnp.float32),
        compiler_params=pltpu.CompilerParams(
            dimension_semantics=("parallel", "arbitrary"), vmem_limit_bytes=VMEM_LIMIT_BYTES),
        name="dsa_attention",
    )(qi, kj, q_idx, att_in, kid_t, q, k_t, v_aug, tri, w_out)


def dsa_branch(att_in, q_norm_g, kv_norm_g, w_uq, w_qidx, w_ukv, w_out):
    q, q_idx, k_t, v_aug, kid_t = dsa_project(att_in, q_norm_g, kv_norm_g, w_uq, w_qidx, w_ukv)
    return dsa_attention(att_in, q_idx, kid_t, q, k_t, v_aug, w_out.astype(MXU_DTYPE))


MOE_TOKENS = 1024
MOE_ROW_CLASSES = (256, 320, 512, 1024)
MOE_VMEM_LIMIT_BYTES = 56 * 1024 * 1024


def _moe_body(cnt_ref, h_ref, gate_ref, gatet_ref, tri_ref, wg_ref, wu_ref, wd_ref, g_ref, b_ref, o_ref,
              xb_ref, gather_ref, scatter_ref, xc_ref, yc_ref, acc_ref):
    f32 = jnp.float32
    tm = MOE_TOKENS
    i = pl.program_id(0)
    e = pl.program_id(1)
    f = pl.program_id(2)
    last_f = pl.num_programs(2) - 1

    @pl.when((e == 0) & (f == 0))
    def _():
        xb_ref[...] = h_ref[...].astype(MXU_DTYPE)
        acc_ref[...] = jnp.zeros_like(acc_ref)

    lane = lax.broadcasted_iota(jnp.int32, gate_ref.shape, 1)
    gate_col = jnp.sum(jnp.where(lane == e, gate_ref[...], 0.0), axis=-1, keepdims=True)

    def routed(size):
        def run():
            @pl.when(f == 0)
            def _():
                active_row = jnp.where(gatet_ref[pl.ds(e, 1), :] != 0.0, 1.0, 0.0)
                rank_row = jnp.dot(jnp.broadcast_to(active_row, (8, tm)).astype(MXU_DTYPE), tri_ref[...],
                                   preferred_element_type=f32)[0:1]
                slot = lax.broadcasted_iota(jnp.int32, (size, tm), 0).astype(f32)
                gather = jnp.where((slot == rank_row) & (active_row != 0.0), 1.0, 0.0)
                gather_ref[0:size, :] = gather.astype(MXU_DTYPE)
                rank_col = jnp.broadcast_to(rank_row, (LANE, tm)).T[:, 0:1]
                slot_l = lax.broadcasted_iota(jnp.int32, (tm, size), 1).astype(f32)
                scatter = jnp.where((slot_l == rank_col) & (gate_col != 0.0), 1.0, 0.0)
                scatter_ref[:, 0:size] = scatter.astype(MXU_DTYPE)
                xc_ref[0:size, :] = jnp.dot(gather.astype(MXU_DTYPE), xb_ref[...],
                                            preferred_element_type=f32).astype(MXU_DTYPE)
                yc_ref[0:size, :] = jnp.zeros((size, D_MODEL), f32)

            xc = xc_ref[0:size, :]
            a = jnp.dot(xc, wg_ref[0], preferred_element_type=f32)
            u = jnp.dot(xc, wu_ref[0], preferred_element_type=f32)
            act = a * jax.nn.sigmoid(a) * u
            yc_ref[0:size, :] += jnp.dot(act.astype(MXU_DTYPE), wd_ref[0], preferred_element_type=f32)

            @pl.when(f == last_f)
            def _():
                back = jnp.dot(scatter_ref[:, 0:size], yc_ref[0:size, :].astype(MXU_DTYPE),
                               preferred_element_type=f32)
                acc_ref[...] += gate_col * back
        return run

    count = cnt_ref[i * N_EXPERTS + e]
    size_class = sum((count > s).astype(jnp.int32) for s in MOE_ROW_CLASSES[:-1])
    for k, size in enumerate(MOE_ROW_CLASSES):
        pl.when((size_class == k) & (count > 0))(routed(size))

    @pl.when((e == pl.num_programs(1) - 1) & (f == last_f))
    def _():
        o_ref[...] = _layer_norm_rows(DEEPNORM_ALPHA * h_ref[...] + acc_ref[...], g_ref[...], b_ref[...])


def moe_experts_norm(h, gates, wg, wu, wd, ln_g, ln_b, *, tf=512):
    m, d = h.shape
    n_experts, _, f_dim = wg.shape
    tm = MOE_TOKENS
    assert m % tm == 0 and f_dim % tf == 0 and MOE_ROW_CLASSES[-1] == tm
    n_tiles = m // tm
    routed = gates[:, :n_experts] != 0.0
    counts = jnp.sum(routed.reshape(n_tiles, tm, n_experts), axis=1, dtype=jnp.int32).reshape(-1)
    gates_t = gates[:, :n_experts].T
    tri = jnp.triu(jnp.ones((tm, tm), MXU_DTYPE), k=1)
    row = lambda i, e, f, c: (i, 0)
    full = lambda i, e, f, c: (0, 0)
    grid_spec = pltpu.PrefetchScalarGridSpec(
        num_scalar_prefetch=1,
        grid=(n_tiles, n_experts, f_dim // tf),
        in_specs=[pl.BlockSpec((tm, d), row), pl.BlockSpec((tm, LANE), row),
                  pl.BlockSpec((n_experts, tm), lambda i, e, f, c: (0, i)),
                  pl.BlockSpec((tm, tm), full),
                  pl.BlockSpec((1, d, tf), lambda i, e, f, c: (e, 0, f)),
                  pl.BlockSpec((1, d, tf), lambda i, e, f, c: (e, 0, f)),
                  pl.BlockSpec((1, tf, d), lambda i, e, f, c: (e, f, 0)),
                  pl.BlockSpec((1, d), full), pl.BlockSpec((1, d), full)],
        out_specs=pl.BlockSpec((tm, d), row),
        scratch_shapes=[pltpu.VMEM((tm, d), MXU_DTYPE),
                        pltpu.VMEM((tm, tm), MXU_DTYPE),
                        pltpu.VMEM((tm, tm), MXU_DTYPE),
                        pltpu.VMEM((tm, d), MXU_DTYPE),
                        pltpu.VMEM((tm, d), jnp.float32),
                        pltpu.VMEM((tm, d), jnp.float32)])
    return pl.pallas_call(
        _moe_body,
        grid_spec=grid_spec,
        out_shape=jax.ShapeDtypeStruct((m, d), jnp.float32),
        compiler_params=pltpu.CompilerParams(
            dimension_semantics=("parallel", "arbitrary", "arbitrary"),
            vmem_limit_bytes=MOE_VMEM_LIMIT_BYTES),
        name="moe_experts_norm",
    )(counts, h, gates, gates_t, tri, wg, wu, wd, ln_g.reshape(1, d), ln_b.reshape(1, d))


def router_gates(h, router):
    logits = (h @ router).astype(jnp.float32)
    top_val, top_idx = lax.top_k(logits, TOP_K)
    top_w = jax.nn.softmax(top_val, axis=-1)
    onehot = (top_idx[..., None] == jnp.arange(N_EXPERTS)[None, None, :])
    gates = jnp.sum(jnp.where(onehot, top_w[..., None], 0.0), axis=1)
    return jnp.pad(gates, ((0, 0), (0, LANE - N_EXPERTS)))


def kernel(x, w_in, ssm_log_dt, ssm_lambda_re, ssm_lambda_im, ssm_b_re, ssm_b_im, ssm_c_re, ssm_c_im,
           ssm_d, ssm_w_glu, ssm_b_glu, ssm_w_out, hgrn_lb_logits, hgrn_norm_g, hgrn_w_out,
           attn_q_norm_g, attn_kv_norm_g, attn_w_uq, attn_w_qidx, attn_w_ukv, attn_w_out, w_o,
           ln_g, ln_b, ffn_w_gate, ffn_w_up, ffn_w_down, moe_router, moe_w_gate, moe_w_up, moe_w_down):
    bsz, seq, d = x.shape
    m = bsz * seq
    bf16 = MXU_DTYPE
    assert MIX_IN_USED + N_BRANCHES * D_MODEL == N_IN
    lb_soft = jax.nn.softmax(hgrn_lb_logits.astype(jnp.float32), axis=0)
    lower_bounds = jnp.concatenate([jnp.zeros_like(lb_soft[:1]), jnp.cumsum(lb_soft[1:], axis=0)], axis=0)
    h = x.reshape(m, d)
    for l in range(DEPTH):
        w_mix = jnp.pad(w_in[l][:, :MIX_IN_USED], ((0, 0), (0, MIX_IN_WIDTH - MIX_IN_USED))).astype(bf16)
        u, hg_in, att_in = in_proj(h, w_mix)
        y_ssm = s5_branch(u.reshape(bsz, seq, SSM_WIDTH), ssm_log_dt[l], ssm_lambda_re[l], ssm_lambda_im[l],
                          ssm_b_re[l], ssm_b_im[l], ssm_c_re[l], ssm_c_im[l], ssm_d[l], ssm_w_glu[l],
                          ssm_b_glu[l], ssm_w_out[l])
        y_hg = hgrn2_branch(hg_in.reshape(bsz, seq, HG_IN_WIDTH), lower_bounds[l], hgrn_norm_g[l],
                            hgrn_w_out[l])
        y_att = dsa_branch(att_in.reshape(bsz, seq, ATT_IN_WIDTH), attn_q_norm_g[l], attn_kv_norm_g[l],
                           attn_w_uq[l], attn_w_qidx[l], attn_w_ukv[l], attn_w_out[l])
        h = merge_project_norm(h, y_ssm.reshape(m, d), y_hg.reshape(m, d), y_att.reshape(m, d),
                               w_in[l][:, MIX_IN_USED:].astype(bf16), w_o[l].astype(bf16),
                               ln_g[l, 0], ln_b[l, 0])
        if l % 2 == 0:
            ones = jnp.ones((m, LANE), jnp.float32)
            h = swiglu_experts_norm(h, ones, ffn_w_gate[l // 2][None].astype(bf16),
                                    ffn_w_up[l // 2][None].astype(bf16),
                                    ffn_w_down[l // 2][None].astype(bf16),
                                    ln_g[l, 1], ln_b[l, 1], tf=256)
        else:
            gate_w = router_gates(h, moe_router[l // 2])
            h = moe_experts_norm(h, gate_w, moe_w_gate[l // 2].astype(bf16),
                                 moe_w_up[l // 2].astype(bf16), moe_w_down[l // 2].astype(bf16),
                                 ln_g[l, 1], ln_b[l, 1])
    return h.reshape(bsz, seq, d)
```

```python
import functools
import math

import jax
import jax.numpy as jnp
import numpy as np
from jax import lax
from jax.experimental import pallas as pl
from jax.experimental.pallas import tpu as pltpu

D_MODEL = 1024
DEPTH = 2
SSM_WIDTH = 256
SSM_GROUP = 16
SSM_GROUPS = SSM_WIDTH // SSM_GROUP
SSM_STATE = 64
HGRN_HEADS = 4
HGRN_DK = 64
HGRN_DV = 64
HGRN_WIDTH = HGRN_HEADS * HGRN_DV
HGRN_CHUNK = 64
ATT_HEADS = 8
ATT_HEAD_DIM = 64
ATT_WIDTH = ATT_HEADS * ATT_HEAD_DIM
ATT_Q_RANK = 256
ATT_KV_RANK = 128
IDX_HEADS = 4
IDX_DIM = 64
TOPK_MAX = 256
Q_BLOCK = 128
MASK_VALUE = -1e30
N_BRANCHES = 3
N_EXPERTS = 8
TOP_K = 2
DEEPNORM_ALPHA = (2 * DEPTH) ** 0.25
LN_EPS = 1e-5
RMS_EPS = 1e-6
F_MIN = 1e-12

IN_SPLITS = (SSM_WIDTH, HGRN_HEADS * HGRN_DK, HGRN_HEADS * HGRN_DK, HGRN_WIDTH, HGRN_WIDTH,
             ATT_Q_RANK, ATT_KV_RANK, IDX_DIM, IDX_HEADS, N_BRANCHES * D_MODEL)
N_IN = sum(IN_SPLITS)

VMEM_LIMIT_BYTES = 48 * 1024 * 1024
LANE = 128
MXU_DTYPE = jnp.bfloat16


def _round_up(n, m):
    return (n + m - 1) // m * m


HG_IN_WIDTH = 2 * HGRN_HEADS * HGRN_DK + 2 * HGRN_WIDTH
ATT_IN_USED = ATT_Q_RANK + ATT_KV_RANK + IDX_DIM + IDX_HEADS
ATT_IN_WIDTH = _round_up(ATT_IN_USED, LANE)
ATT_IDX_BLOCK = (ATT_Q_RANK + ATT_KV_RANK) // LANE
MIX_IN_USED = SSM_WIDTH + HG_IN_WIDTH + ATT_IN_USED
MIX_IN_WIDTH = SSM_WIDTH + HG_IN_WIDTH + ATT_IN_WIDTH


def _in_proj_body(h_ref, w_ref, u_ref, hg_ref, att_ref):
    r = jnp.dot(h_ref[...].astype(MXU_DTYPE), w_ref[...], preferred_element_type=jnp.float32)
    u_ref[...] = r[:, :SSM_WIDTH]
    hg_ref[...] = r[:, SSM_WIDTH:SSM_WIDTH + HG_IN_WIDTH]
    att_ref[...] = r[:, SSM_WIDTH + HG_IN_WIDTH:]


def in_proj(h, w_mix, *, tm=512):
    m, d = h.shape
    row = lambda i: (i, 0)
    widths = (SSM_WIDTH, HG_IN_WIDTH, ATT_IN_WIDTH)
    return pl.pallas_call(
        _in_proj_body,
        grid=(m // tm,),
        in_specs=[pl.BlockSpec((tm, d), row), pl.BlockSpec((d, MIX_IN_WIDTH), lambda i: (0, 0))],
        out_specs=[pl.BlockSpec((tm, w), row) for w in widths],
        out_shape=[jax.ShapeDtypeStruct((m, w), jnp.float32) for w in widths],
        compiler_params=pltpu.CompilerParams(
            dimension_semantics=("parallel",), vmem_limit_bytes=VMEM_LIMIT_BYTES),
        name="in_proj",
    )(h, w_mix)


def _layer_norm_rows(y, g, b):
    mu = jnp.mean(y, axis=-1, keepdims=True)
    yc = y - mu
    var = jnp.mean(yc * yc, axis=-1, keepdims=True)
    return yc * lax.rsqrt(var + LN_EPS) * g + b


def _merge_body(h_ref, ys_ref, yh_ref, ya_ref, wgate_ref, wo_ref, g_ref, b_ref, o_ref):
    d = D_MODEL
    f32 = jnp.float32
    h = h_ref[...]
    gates = jax.nn.sigmoid(jnp.dot(h.astype(MXU_DTYPE), wgate_ref[...], preferred_element_type=f32))
    mixed = (gates[:, 0:d] * ys_ref[...] + gates[:, d:2 * d] * yh_ref[...]
             + gates[:, 2 * d:3 * d] * ya_ref[...])
    mix_out = jnp.dot(mixed.astype(MXU_DTYPE), wo_ref[...], preferred_element_type=f32)
    o_ref[...] = _layer_norm_rows(DEEPNORM_ALPHA * h + mix_out, g_ref[...], b_ref[...])


def merge_project_norm(h, y_ssm, y_hg, y_att, w_gates, w_o, ln_g, ln_b, *, tm=512):
    m, d = h.shape
    row = lambda i: (i, 0)
    full = lambda i: (0, 0)
    return pl.pallas_call(
        _merge_body,
        grid=(m // tm,),
        in_specs=[pl.BlockSpec((tm, d), row), pl.BlockSpec((tm, d), row), pl.BlockSpec((tm, d), row),
                  pl.BlockSpec((tm, d), row), pl.BlockSpec((d, N_BRANCHES * d), full),
                  pl.BlockSpec((d, d), full), pl.BlockSpec((1, d), full), pl.BlockSpec((1, d), full)],
        out_specs=pl.BlockSpec((tm, d), row),
        out_shape=jax.ShapeDtypeStruct((m, d), jnp.float32),
        compiler_params=pltpu.CompilerParams(
            dimension_semantics=("parallel",), vmem_limit_bytes=VMEM_LIMIT_BYTES),
        name="merge_project_norm",
    )(h, y_ssm, y_hg, y_att, w_gates, w_o, ln_g.reshape(1, d), ln_b.reshape(1, d))


def _ffn_body(h_ref, gate_ref, wg_ref, wu_ref, wd_ref, g_ref, b_ref, o_ref, acc_ref, *, n_experts):
    e = pl.program_id(1)
    f = pl.program_id(2)

    @pl.when((e == 0) & (f == 0))
    def _():
        acc_ref[...] = jnp.zeros_like(acc_ref)

    x = h_ref[...].astype(MXU_DTYPE)
    a = jnp.dot(x, wg_ref[0], preferred_element_type=jnp.float32)
    u = jnp.dot(x, wu_ref[0], preferred_element_type=jnp.float32)
    act = a * jax.nn.sigmoid(a) * u
    part = jnp.dot(act.astype(MXU_DTYPE), wd_ref[0], preferred_element_type=jnp.float32)
    if n_experts > 1:
        lane = lax.broadcasted_iota(jnp.int32, gate_ref.shape, 1)
        gate_e = jnp.sum(jnp.where(lane == e, gate_ref[...], 0.0), axis=-1, keepdims=True)
        part = gate_e * part
    acc_ref[...] += part

    @pl.when((e == pl.num_programs(1) - 1) & (f == pl.num_programs(2) - 1))
    def _():
        o_ref[...] = _layer_norm_rows(DEEPNORM_ALPHA * h_ref[...] + acc_ref[...], g_ref[...], b_ref[...])


def swiglu_experts_norm(h, gates, wg, wu, wd, ln_g, ln_b, *, tm=1024, tf=512):
    m, d = h.shape
    n_experts, _, f_dim = wg.shape
    assert f_dim % tf == 0 and m % tm == 0
    row = lambda i, e, f: (i, 0)
    full = lambda i, e, f: (0, 0)
    return pl.pallas_call(
        functools.partial(_ffn_body, n_experts=n_experts),
        grid=(m // tm, n_experts, f_dim // tf),
        in_specs=[pl.BlockSpec((tm, d), row), pl.BlockSpec((tm, LANE), row),
                  pl.BlockSpec((1, d, tf), lambda i, e, f: (e, 0, f)),
                  pl.BlockSpec((1, d, tf), lambda i, e, f: (e, 0, f)),
                  pl.BlockSpec((1, tf, d), lambda i, e, f: (e, f, 0)),
                  pl.BlockSpec((1, d), full), pl.BlockSpec((1, d), full)],
        out_specs=pl.BlockSpec((tm, d), row),
        out_shape=jax.ShapeDtypeStruct((m, d), jnp.float32),
        scratch_shapes=[pltpu.VMEM((tm, d), jnp.float32)],
        compiler_params=pltpu.CompilerParams(
            dimension_semantics=("parallel", "arbitrary", "arbitrary"),
            vmem_limit_bytes=VMEM_LIMIT_BYTES),
        name="swiglu_experts_norm",
    )(h, gates, wg, wu, wd, ln_g.reshape(1, d), ln_b.reshape(1, d))


def rms_norm(x, g):
    xf = x.astype(jnp.float32)
    return (xf * lax.rsqrt(jnp.mean(xf * xf, axis=-1, keepdims=True) + RMS_EPS) * g).astype(x.dtype)


def _ssm_combine(e1, e2):
    a1r, a1i, b1r, b1i = e1
    a2r, a2i, b2r, b2i = e2
    return (a2r * a1r - a2i * a1i, a2r * a1i + a2i * a1r,
            a2r * b1r - a2i * b1i + b2r, a2r * b1i + a2i * b1r + b2i)


S5_CHUNK = 8
S5_CHUNK_WIDTH = S5_CHUNK * SSM_WIDTH
S5_STATE_WIDTH = SSM_GROUPS * SSM_STATE


def s5_operators(log_dt, lam_re, lam_im, b_re, b_im, c_re, c_im):
    f32 = jnp.float32
    n = S5_CHUNK
    lre, lim = lam_re.astype(f32), lam_im.astype(f32)
    dt = jnp.exp(log_dt.astype(f32))[:, None]
    mag = jnp.exp(lre * dt)
    ang = lim * dt
    a_re, a_im = mag * jnp.cos(ang), mag * jnp.sin(ang)
    den = lre * lre + lim * lim
    coef_re = ((a_re - 1.0) * lre + a_im * lim) / den
    coef_im = (a_im * lre - (a_re - 1.0) * lim) / den
    br, bi = b_re.astype(f32), b_im.astype(f32)
    bb_re = coef_re[..., None] * br - coef_im[..., None] * bi
    bb_im = coef_re[..., None] * bi + coef_im[..., None] * br
    j = jnp.arange(n + 1, dtype=f32)[:, None, None]
    pw_re = jnp.exp(j * lre * dt) * jnp.cos(j * ang)
    pw_im = jnp.exp(j * lre * dt) * jnp.sin(j * ang)
    eye = jnp.eye(SSM_GROUPS, dtype=f32)
    cr, ci = c_re.astype(f32), c_im.astype(f32)
    ab_re = pw_re[..., None] * bb_re - pw_im[..., None] * bb_im
    ab_im = pw_re[..., None] * bb_im + pw_im[..., None] * bb_re
    kern = (jnp.einsum('gcp,jgpd->jgcd', cr, ab_re[:n]) - jnp.einsum('gcp,jgpd->jgcd', ci, ab_im[:n]))
    lag = jnp.arange(n)[None, :] - jnp.arange(n)[:, None]
    toep = jnp.where((lag >= 0)[:, :, None, None, None], kern[jnp.maximum(lag, 0)], 0.0)
    t_mat = jnp.einsum('abgcd,gh->agdbhc', toep, eye).reshape(S5_CHUNK_WIDTH, S5_CHUNK_WIDTH)
    v = jnp.stack([ab_re[:n][::-1], ab_im[:n][::-1]])
    w_in = jnp.einsum('rlgpc,gh->lhcrgp', v, eye).reshape(S5_CHUNK_WIDTH, 2 * S5_STATE_WIDTH)
    ar, ai = pw_re[1:], pw_im[1:]
    wo_re = jnp.einsum('gcp,lgp->gplc', cr, ar) - jnp.einsum('gcp,lgp->gplc', ci, ai)
    wo_im = -jnp.einsum('gcp,lgp->gplc', cr, ai) - jnp.einsum('gcp,lgp->gplc', ci, ar)
    w_out = jnp.einsum('rgplc,gh->rhplgc', jnp.stack([wo_re, wo_im]), eye)
    w_out = w_out.reshape(2 * S5_STATE_WIDTH, S5_CHUNK_WIDTH)
    decay = jnp.stack([pw_re[n].reshape(1, S5_STATE_WIDTH), pw_im[n].reshape(1, S5_STATE_WIDTH)])
    return (jnp.concatenate([t_mat, w_in], axis=1).astype(MXU_DTYPE), w_out.astype(MXU_DTYPE), decay)


def _s5_scan_body(u_ref, tw_ref, wout_ref, decay_ref, y_ref, h_ref, s_ref, hs_ref, *, tm):
    f32 = jnp.float32
    sw = S5_STATE_WIDTH

    @pl.when(pl.program_id(1) == 0)
    def _():
        h_ref[...] = jnp.zeros_like(h_ref)

    r = jnp.dot(u_ref[0].astype(MXU_DTYPE), tw_ref[...], preferred_element_type=f32)
    s_ref[...] = r[:, S5_CHUNK_WIDTH:]
    d_re = decay_ref[0]
    d_im = decay_ref[1]

    def eight_chunks(k, carry):
        h_re, h_im = carry
        r0 = pl.multiple_of(k * 8, 8)
        inc = s_ref[pl.ds(r0, 8), :]
        rows_re, rows_im = [], []
        for t in range(8):
            rows_re.append(h_re)
            rows_im.append(h_im)
            h_re, h_im = (d_re * h_re - d_im * h_im + inc[t:t + 1, :sw],
                          d_re * h_im + d_im * h_re + inc[t:t + 1, sw:])
        hs_ref[pl.ds(r0, 8), :] = jnp.concatenate(
            [jnp.concatenate(rows_re, axis=0), jnp.concatenate(rows_im, axis=0)], axis=1)
        return h_re, h_im

    h_re, h_im = lax.fori_loop(0, tm // 8, eight_chunks, (h_ref[0:1, :], h_ref[1:2, :]))
    h_ref[0:1, :] = h_re
    h_ref[1:2, :] = h_im
    y_ref[0] = r[:, :S5_CHUNK_WIDTH] + jnp.dot(hs_ref[...].astype(MXU_DTYPE), wout_ref[...],
                                                preferred_element_type=f32)


def s5_scan(u, tw, w_out, decay, *, tm=256):
    bsz, seq, _ = u.shape
    n_chunks = seq // S5_CHUNK
    tm = min(tm, n_chunks)
    assert seq % S5_CHUNK == 0 and n_chunks % tm == 0 and tm % 8 == 0
    once = pl.Buffered(1)
    y = pl.pallas_call(
        functools.partial(_s5_scan_body, tm=tm),
        grid=(bsz, n_chunks // tm),
        in_specs=[pl.BlockSpec((1, tm, S5_CHUNK_WIDTH), lambda b, i: (b, i, 0)),
                  pl.BlockSpec(tw.shape, lambda b, i: (0, 0), pipeline_mode=once),
                  pl.BlockSpec(w_out.shape, lambda b, i: (0, 0), pipeline_mode=once),
                  pl.BlockSpec(decay.shape, lambda b, i: (0, 0, 0), pipeline_mode=once)],
        out_specs=pl.BlockSpec((1, tm, S5_CHUNK_WIDTH), lambda b, i: (b, i, 0)),
        out_shape=jax.ShapeDtypeStruct((bsz, n_chunks, S5_CHUNK_WIDTH), jnp.float32),
        scratch_shapes=[pltpu.VMEM((2, S5_STATE_WIDTH), jnp.float32),
                        pltpu.VMEM((tm, 2 * S5_STATE_WIDTH), jnp.float32),
                        pltpu.VMEM((tm, 2 * S5_STATE_WIDTH), jnp.float32)],
        compiler_params=pltpu.CompilerParams(
            dimension_semantics=("parallel", "arbitrary"), vmem_limit_bytes=VMEM_LIMIT_BYTES),
        name="s5_scan",
    )(u.reshape(bsz, n_chunks, S5_CHUNK_WIDTH), tw, w_out, decay)
    return y.reshape(bsz, seq, SSM_WIDTH)


def _s5_out_body(y_ref, u_ref, d_ref, wglu_ref, bglu_ref, wout_ref, o_ref):
    f32 = jnp.float32
    y = jax.nn.gelu(y_ref[...] + d_ref[...] * u_ref[...])
    gate = jnp.dot(y.astype(MXU_DTYPE), wglu_ref[...], preferred_element_type=f32) + bglu_ref[...]
    y = y * jax.nn.sigmoid(gate)
    o_ref[...] = jnp.dot(y.astype(MXU_DTYPE), wout_ref[...], preferred_element_type=f32)


def s5_output(y, u, d_skip, w_glu, b_glu, w_out, *, tm=1024):
    m, c = y.shape
    tm = min(tm, m)
    row = lambda i: (i, 0)
    full = lambda i: (0, 0)
    return pl.pallas_call(
        _s5_out_body,
        grid=(m // tm,),
        in_specs=[pl.BlockSpec((tm, c), row), pl.BlockSpec((tm, c), row), pl.BlockSpec((1, c), full),
                  pl.BlockSpec((c, c), full), pl.BlockSpec((1, c), full), pl.BlockSpec((c, D_MODEL), full)],
        out_specs=pl.BlockSpec((tm, D_MODEL), row),
        out_shape=jax.ShapeDtypeStruct((m, D_MODEL), jnp.float32),
        compiler_params=pltpu.CompilerParams(
            dimension_semantics=("parallel",), vmem_limit_bytes=VMEM_LIMIT_BYTES),
        name="s5_output",
    )(y, u, d_skip.reshape(1, c), w_glu.astype(MXU_DTYPE), b_glu.reshape(1, c), w_out.astype(MXU_DTYPE))


def s5_branch(u, log_dt, lam_re, lam_im, b_re, b_im, c_re, c_im, d_skip, w_glu, b_glu, w_out):
    bsz, seq, _ = u.shape
    tw, w_state_out, decay = s5_operators(log_dt, lam_re, lam_im, b_re, b_im, c_re, c_im)
    y = s5_scan(u, tw, w_state_out, decay)
    out = s5_output(y.reshape(bsz * seq, SSM_WIDTH), u.reshape(bsz * seq, SSM_WIDTH),
                    d_skip, w_glu, b_glu, w_out)
    return out.reshape(bsz, seq, D_MODEL)


HG_CHUNK = 128
HG_LEVELS = 7
HG_KDIM = HGRN_HEADS * HGRN_DK


def _hgrn_segment_sums():
    c = HG_CHUNK
    t = np.arange(c)[:, None]
    u = np.arange(c)[None, :]
    blocks = []
    for lvl in range(1, HG_LEVELS + 1):
        m = (t >> lvl << lvl) + (1 << (lvl - 1)) - 1
        right = ((t >> (lvl - 1)) & 1) == 1
        blocks.append(np.where(right, (u > m) & (u <= t), (u > t) & (u <= m)))
    blocks.append(u <= t)
    blocks.append(u > t)
    return np.concatenate(blocks, axis=0).astype(np.float32)


def _hgrn_body(q_ref, z_ref, v_ref, g_ref, seg_ref, lb_ref, ng_ref, hmean_ref, wout_ref, o_ref, st_ref):
    f32 = jnp.float32
    c = HG_CHUNK

    @pl.when(pl.program_id(1) == 0)
    def _():
        st_ref[...] = jnp.zeros_like(st_ref)

    q = q_ref[0]
    z = z_ref[0]
    v = v_ref[0]
    lb = lb_ref[...]
    f = lb + (1.0 - lb) * jax.nn.sigmoid(z)
    logf = jnp.log(jnp.maximum(f, F_MIN))
    kin = (1.0 - lb) * jax.nn.sigmoid(-z)

    p1 = logf.astype(MXU_DTYPE)
    r1 = logf - p1.astype(f32)
    p2 = r1.astype(MXU_DTYPE)
    p3 = (r1 - p2.astype(f32)).astype(MXU_DTYPE)
    seg = seg_ref[...]
    sums = (jnp.dot(seg, p1, preferred_element_type=f32) + jnp.dot(seg, p2, preferred_element_type=f32)
            + jnp.dot(seg, p3, preferred_element_type=f32))

    lane_head = lax.broadcasted_iota(jnp.int32, (c, HG_KDIM), 1) // HGRN_DK
    tok = lax.broadcasted_iota(jnp.int32, (c, HG_KDIM), 0)
    row_t = lax.broadcasted_iota(jnp.int32, (HGRN_HEADS * c, c), 0) % c
    col_s = lax.broadcasted_iota(jnp.int32, (HGRN_HEADS * c, c), 1)

    def per_head_rows(x):
        return jnp.concatenate([jnp.where(lane_head == h, x, 0.0) for h in range(HGRN_HEADS)],
                               axis=0).astype(MXU_DTYPE)

    def scores(ql, kl):
        return lax.dot_general(per_head_rows(ql), kl.astype(MXU_DTYPE), (((1,), (1,)), ((), ())),
                               preferred_element_type=f32)

    att = jnp.where(row_t == col_s, scores(q, kin), 0.0)
    for lvl in range(1, HG_LEVELS + 1):
        decay = jnp.exp(sums[(lvl - 1) * c:lvl * c])
        right = ((tok >> (lvl - 1)) & 1) == 1
        a = scores(jnp.where(right, q * decay, 0.0), jnp.where(right, 0.0, kin * decay))
        att = att + jnp.where((row_t >> lvl) == (col_s >> lvl), a, 0.0)

    b = sums[HG_LEVELS * c:(HG_LEVELS + 1) * c]
    tail = sums[(HG_LEVELS + 1) * c:(HG_LEVELS + 2) * c]
    v_m = v.astype(MXU_DTYPE)
    st = st_ref[...]
    o = lax.dot_general((q * jnp.exp(b)).astype(MXU_DTYPE), st.astype(MXU_DTYPE),
                        (((1,), (1,)), ((), ())), preferred_element_type=f32)
    for h in range(HGRN_HEADS):
        o_h = jnp.dot(att[h * c:(h + 1) * c].astype(MXU_DTYPE), v_m, preferred_element_type=f32)
        o = o + jnp.where(lane_head == h, o_h, 0.0)

    kv = jnp.dot(v.T.astype(MXU_DTYPE), (kin * jnp.exp(tail)).astype(MXU_DTYPE), preferred_element_type=f32)
    sr = lax.broadcasted_iota(jnp.int32, st.shape, 0) // HGRN_DV
    sc = lax.broadcasted_iota(jnp.int32, st.shape, 1) // HGRN_DK
    st_ref[...] = st * jnp.exp(b[c - 1:c, :]) + jnp.where(sr == sc, kv, 0.0)

    o2 = o * o
    o2_hi = o2.astype(MXU_DTYPE)
    o2_lo = (o2 - o2_hi.astype(f32)).astype(MXU_DTYPE)
    ms = (jnp.dot(o2_hi, hmean_ref[...], preferred_element_type=f32)
          + jnp.dot(o2_lo, hmean_ref[...], preferred_element_type=f32))
    g = g_ref[0]
    out = o * lax.rsqrt(ms + RMS_EPS) * ng_ref[...] * (g * jax.nn.sigmoid(g))
    o_ref[0] = jnp.dot(out.astype(MXU_DTYPE), wout_ref[...], preferred_element_type=f32)


def hgrn2_branch(hg_in, lower_bound, norm_g, w_out):
    bsz, seq, _ = hg_in.shape
    assert seq % HG_CHUNK == 0 and HGRN_DK == HGRN_DV and HG_IN_WIDTH == 4 * HG_KDIM
    seg = jnp.asarray(_hgrn_segment_sums(), MXU_DTYPE)
    head_mean = jnp.asarray(np.kron(np.eye(HGRN_HEADS), np.full((HGRN_DV, HGRN_DV), 1.0 / HGRN_DV)), MXU_DTYPE)
    tok = lambda b, i: (b, i, 0)
    full = lambda b, i: (0, 0)
    part = lambda k: pl.BlockSpec((1, HG_CHUNK, HG_KDIM), lambda b, i: (b, i, k))
    return pl.pallas_call(
        _hgrn_body,
        grid=(bsz, seq // HG_CHUNK),
        in_specs=[part(0), part(1), part(2), part(3),
                  pl.BlockSpec(seg.shape, full), pl.BlockSpec((1, HG_KDIM), full),
                  pl.BlockSpec((1, HGRN_WIDTH), full), pl.BlockSpec(head_mean.shape, full),
                  pl.BlockSpec((HGRN_WIDTH, D_MODEL), full)],
        out_specs=pl.BlockSpec((1, HG_CHUNK, D_MODEL), tok),
        out_shape=jax.ShapeDtypeStruct((bsz, seq, D_MODEL), jnp.float32),
        scratch_shapes=[pltpu.VMEM((HGRN_WIDTH, HG_KDIM), jnp.float32)],
        compiler_params=pltpu.CompilerParams(
            dimension_semantics=("parallel", "arbitrary"), vmem_limit_bytes=VMEM_LIMIT_BYTES),
        name="hgrn2",
    )(hg_in, hg_in, hg_in, hg_in, seg, lower_bound.reshape(1, HG_KDIM).astype(jnp.float32),
      jnp.tile(norm_g.astype(jnp.float32), HGRN_HEADS).reshape(1, HGRN_WIDTH), head_mean,
      w_out.astype(MXU_DTYPE))


ATT_QK_DIM = LANE
DSA_TILE = 512


def _rms_rows(x, g):
    return x * lax.rsqrt(jnp.mean(x * x, axis=-1, keepdims=True) + RMS_EPS) * g


def _dsa_project_body(x_ref, gq_ref, gkv_ref, wq_ref, wqi_ref, wkt_ref, wv_ref,
                      q_ref, qi_ref, kt_ref, v_ref, kidt_ref):
    f32 = jnp.float32
    tm = x_ref.shape[1]
    x = x_ref[0]
    cq = _rms_rows(x[:, :ATT_Q_RANK], gq_ref[...]).astype(MXU_DTYPE)
    ckv = _rms_rows(x[:, ATT_Q_RANK:ATT_Q_RANK + ATT_KV_RANK], gkv_ref[...])
    q_all = jnp.dot(cq, wq_ref[...], preferred_element_type=f32)
    qi_all = jnp.dot(cq, wqi_ref[...], preferred_element_type=f32)
    v_all = jnp.dot(ckv.astype(MXU_DTYPE), wv_ref[...], preferred_element_type=f32)
    kt_all = jnp.dot(wkt_ref[...], ckv.T.astype(MXU_DTYPE), preferred_element_type=f32)

    lane = lax.broadcasted_iota(jnp.int32, (tm, LANE), 1)
    t_loc = lax.broadcasted_iota(jnp.int32, (tm, LANE), 0)
    t_even = (t_loc // 2 * 2).astype(f32)
    t_odd = (t_loc % 2).astype(f32)
    sub = lax.broadcasted_iota(jnp.int32, (LANE, tm), 0)
    s_loc = lax.broadcasted_iota(jnp.int32, (LANE, tm), 1)
    k_rows = jnp.where(sub < ATT_HEAD_DIM + 2, 1.0,
                       jnp.where(sub == ATT_HEAD_DIM + 2, (s_loc // 2 * 2).astype(f32), (s_loc % 2).astype(f32)))
    for h in range(ATT_HEADS):
        slope = 2.0 ** (-8.0 * (h + 1) / ATT_HEADS)
        q_cols = jnp.where(lane == ATT_HEAD_DIM, -slope * t_even,
                           jnp.where(lane == ATT_HEAD_DIM + 1, -slope * t_odd, slope))
        q_h = q_all[:, h * LANE:(h + 1) * LANE]
        q_ref[0, h] = jnp.where(lane < ATT_HEAD_DIM, q_h,
                                jnp.where(lane < ATT_HEAD_DIM + 4, q_cols, 0.0)).astype(q_ref.dtype)
        k_h = kt_all[h * LANE:(h + 1) * LANE, :]
        kt_ref[0, h] = jnp.where(sub < ATT_HEAD_DIM, k_h,
                                 jnp.where(sub < ATT_HEAD_DIM + 4, k_rows, 0.0)).astype(kt_ref.dtype)
        v_h = v_all[:, h * LANE:(h + 1) * LANE]
        v_ref[0, h] = jnp.where(lane == ATT_HEAD_DIM, 1.0, v_h).astype(v_ref.dtype)
    for h in range(IDX_HEADS):
        qi_ref[0, h] = qi_all[:, h * LANE:(h + 1) * LANE].astype(qi_ref.dtype)
    kidt_ref[0] = x[:, ATT_IDX_BLOCK * LANE:(ATT_IDX_BLOCK + 1) * LANE].T.astype(kidt_ref.dtype)


def _head_padded(w, n_heads, dim):
    k = w.shape[0]
    return jnp.pad(w.reshape(k, n_heads, dim), ((0, 0), (0, 0), (0, LANE - dim))).reshape(k, n_heads * LANE)


def dsa_project(att_in, q_norm_g, kv_norm_g, w_uq, w_qidx, w_ukv):
    bsz, seq, _ = att_in.shape
    tm = DSA_TILE
    assert seq % tm == 0 and 8 % ATT_HEADS == 0 and tm <= 512
    w_q = _head_padded(w_uq * ATT_HEAD_DIM ** -0.5, ATT_HEADS, ATT_HEAD_DIM).astype(MXU_DTYPE)
    w_qi = _head_padded(w_qidx * IDX_DIM ** -0.5, IDX_HEADS, IDX_DIM).astype(MXU_DTYPE)
    w_kt = _head_padded(w_ukv[:, :ATT_WIDTH], ATT_HEADS, ATT_HEAD_DIM).T.astype(MXU_DTYPE)
    w_v = _head_padded(w_ukv[:, ATT_WIDTH:], ATT_HEADS, ATT_HEAD_DIM).astype(MXU_DTYPE)
    tok = lambda b, i: (b, 0, i, 0)
    full = lambda b, i: (0, 0)
    dt = MXU_DTYPE
    return pl.pallas_call(
        _dsa_project_body,
        grid=(bsz, seq // tm),
        in_specs=[pl.BlockSpec((1, tm, ATT_IN_WIDTH), lambda b, i: (b, i, 0)),
                  pl.BlockSpec((1, ATT_Q_RANK), full), pl.BlockSpec((1, ATT_KV_RANK), full),
                  pl.BlockSpec(w_q.shape, full), pl.BlockSpec(w_qi.shape, full),
                  pl.BlockSpec(w_kt.shape, full), pl.BlockSpec(w_v.shape, full)],
        out_specs=[pl.BlockSpec((1, ATT_HEADS, tm, LANE), tok),
                   pl.BlockSpec((1, IDX_HEADS, tm, LANE), tok),
                   pl.BlockSpec((1, ATT_HEADS, LANE, tm), lambda b, i: (b, 0, 0, i)),
                   pl.BlockSpec((1, ATT_HEADS, tm, LANE), tok),
                   pl.BlockSpec((1, LANE, tm), lambda b, i: (b, 0, i))],
        out_shape=[jax.ShapeDtypeStruct((bsz, ATT_HEADS, seq, LANE), dt),
                   jax.ShapeDtypeStruct((bsz, IDX_HEADS, seq, LANE), dt),
                   jax.ShapeDtypeStruct((bsz, ATT_HEADS, LANE, seq), dt),
                   jax.ShapeDtypeStruct((bsz, ATT_HEADS, seq, LANE), dt),
                   jax.ShapeDtypeStruct((bsz, LANE, seq), dt)],
        compiler_params=pltpu.CompilerParams(
            dimension_semantics=("parallel", "parallel"), vmem_limit_bytes=VMEM_LIMIT_BYTES),
        name="dsa_project",
    )(att_in, q_norm_g.reshape(1, ATT_Q_RANK), kv_norm_g.reshape(1, ATT_KV_RANK), w_q, w_qi, w_kt, w_v)


_INT_MIN = -2 ** 31
_MASK_KEY = int(np.float32(MASK_VALUE).view(np.int32)) ^ 0x7FFFFFFF
_SEARCH_ROWS = 128
_ATT_ROWS = 32
_SEARCH_LANES = 512
_LIST_DEPTH = 12
_LIST_ROWS = 16
_LIST_MIN_BLOCKS = 4


def _key_to_score(key):
    return lax.bitcast_convert_type(jnp.where(key < 0, key ^ 0x7FFFFFFF, key), jnp.float32)


def _dsa_body(qi_ref, kj_ref, qidx_ref, w_ref, kidt_ref, q_ref, kt_ref, v_ref, tri_ref, wout_ref,
              o_ref, keys_ref, thr_ref, need_ref, carry_ref, m_ref, acc_ref, bias_ref, s_ref, p_ref,
              alpha_ref, cand_ref, *, tq, tk, seq, n_sel):
    f32 = jnp.float32
    p_id = pl.program_id(1)
    i = qi_ref[p_id]
    j = kj_ref[p_id]
    row_local = lax.broadcasted_iota(jnp.int32, (tq, tk), 0)
    col_local = lax.broadcasted_iota(jnp.int32, (tq, tk), 1)

    @pl.when(j == 0)
    def _select():
        def score_block(jj, carry):
            off = pl.multiple_of(jj * tk, tk)
            kb = kidt_ref[0, :, pl.ds(off, tk)]
            sc = jnp.zeros((tq, tk), f32)
            for h in range(IDX_HEADS):
                s = jnp.dot(qidx_ref[0, h], kb, preferred_element_type=f32)
                w_h = w_ref[0, :, IDX_DIM + h:IDX_DIM + h + 1] * IDX_HEADS ** -0.5
                sc = sc + jnp.maximum(s, 0.0) * w_h
            causal = (col_local + jj * tk) <= (row_local + i * tq)
            sc = jnp.where(causal, sc, MASK_VALUE)
            keys_ref[:, pl.ds(off, tk)] = jnp.where(sc == 0.0, 0.0, sc)
            return carry

        lax.fori_loop(0, i + 1, score_block, 0)

        n_blocks = i + 1
        n_masked_tail = (seq - n_blocks * tk).astype(f32)

        def row_group(r, carry):
            r0 = pl.multiple_of(r * _SEARCH_ROWS, _SEARCH_ROWS)

            def count_ge(ref, n_iter, cand):
                cand_b = jnp.broadcast_to(_key_to_score(cand), (_SEARCH_ROWS, LANE))

                def chunk(c, acc):
                    base = pl.multiple_of(c * _SEARCH_LANES, _SEARCH_LANES)
                    for u in range(_SEARCH_LANES // LANE):
                        kk = ref[pl.ds(r0, _SEARCH_ROWS), pl.ds(base + u * LANE, LANE)]
                        acc = acc + jnp.where(kk >= cand_b, 1.0, 0.0)
                    return acc

                acc = lax.fori_loop(0, n_iter, chunk, jnp.zeros((_SEARCH_ROWS, LANE), f32))
                cnt = jnp.sum(acc, axis=1, keepdims=True)
                return cnt + jnp.where(cand <= _MASK_KEY, n_masked_tail, 0.0)

            def kth_largest_key(count):
                zero = jnp.zeros((_SEARCH_ROWS, 1), jnp.int32)
                v0 = jnp.where(count(zero) >= n_sel, zero, zero + _INT_MIN)

                def bit_step(b, v):
                    cand = v | jnp.left_shift(jnp.int32(1), 30 - b)
                    return jnp.where(count(cand) >= n_sel, cand, v)

                return lax.fori_loop(0, 31, bit_step, v0)

            count_all = functools.partial(count_ge, keys_ref, n_blocks * (tk // _SEARCH_LANES))

            def store(v, n_above):
                thr_ref[pl.ds(r0, _SEARCH_ROWS), :] = _key_to_score(v)
                need_ref[pl.ds(r0, _SEARCH_ROWS), :] = n_sel - n_above

            def search_all_keys():
                v = kth_largest_key(count_all)
                store(v, count_all(v + 1))

            @pl.when(n_blocks < _LIST_MIN_BLOCKS)
            def _():
                search_all_keys()

            @pl.when(n_blocks >= _LIST_MIN_BLOCKS)
            def _():
                def shortlist(sub, carry):
                    rr = pl.multiple_of(r0 + sub * _LIST_ROWS, _LIST_ROWS)

                    def insert_block(c, tops):
                        base = pl.multiple_of(c * tk, tk)
                        for u in range(tk // LANE):
                            x = keys_ref[pl.ds(rr, _LIST_ROWS), pl.ds(base + u * LANE, LANE)]
                            new = []
                            for t in range(_LIST_DEPTH):
                                new.append(jnp.maximum(tops[t], x))
                                x = jnp.minimum(tops[t], x)
                            tops = tuple(new)
                        return tops

                    tops = lax.fori_loop(0, n_blocks, insert_block,
                                         tuple(jnp.full((_LIST_ROWS, LANE), -jnp.inf, f32)
                                               for _ in range(_LIST_DEPTH)))
                    for t in range(_LIST_DEPTH):
                        cand_ref[pl.ds(rr, _LIST_ROWS), t * LANE:(t + 1) * LANE] = tops[t]
                    return carry

                lax.fori_loop(0, _SEARCH_ROWS // _LIST_ROWS, shortlist, 0)
                v = kth_largest_key(functools.partial(count_ge, cand_ref,
                                                      _LIST_DEPTH * LANE // _SEARCH_LANES))
                n_at_least = count_all(v)
                n_above = count_all(v + 1)
                store(v, n_above)
                exact = (n_above < n_sel) & (n_at_least >= n_sel)
                pl.when(jnp.min(jnp.where(exact, 1.0, 0.0)) < 0.5)(search_all_keys)

            return carry

        lax.fori_loop(0, tq // _SEARCH_ROWS, row_group, 0)
        carry_ref[...] = jnp.zeros_like(carry_ref)
        m_ref[...] = jnp.full_like(m_ref, MASK_VALUE)
        acc_ref[...] = jnp.zeros_like(acc_ref)

    keys_blk = keys_ref[:, pl.ds(pl.multiple_of(j * tk, tk), tk)]
    thr = thr_ref[...]
    eq = keys_blk == thr
    tie_rank = carry_ref[...] + jnp.dot(jnp.where(eq, 1.0, 0.0).astype(MXU_DTYPE), tri_ref[...],
                                         preferred_element_type=f32)
    carry_ref[...] = tie_rank[:, tk - 1:tk]
    sel = (keys_blk > thr) | (eq & (tie_rank <= need_ref[...]))
    causal = (row_local + i * tq) >= (col_local + j * tk)
    bias_ref[...] = jnp.where(sel & causal, 0.0, MASK_VALUE)

    def logits(h, slot):
        s_ref[slot] = jnp.dot(q_ref[0, h], kt_ref[0, h], preferred_element_type=f32)

    def softmax(h, slot):
        slope_tk = jnp.right_shift(tk, h + 1).astype(f32)
        for r in range(0, tq, _ATT_ROWS):
            rows = pl.ds(r, _ATT_ROWS)
            s = s_ref[slot, rows, :] + bias_ref[rows, :]
            m_old = m_ref[h, rows, :] - slope_tk
            m_new = jnp.maximum(m_old, jnp.max(s, axis=1, keepdims=True))
            p_ref[slot, rows, :] = jnp.exp(s - m_new).astype(MXU_DTYPE)
            alpha_ref[slot, rows, :] = jnp.exp(m_old - m_new)
            m_ref[h, rows, :] = m_new

    def weighted_values(h, slot):
        acc_ref[h] = alpha_ref[slot] * acc_ref[h] + jnp.dot(p_ref[slot], v_ref[0, h],
                                                             preferred_element_type=f32)

    for t in range(ATT_HEADS + 2):
        if t < ATT_HEADS:
            logits(t, t % 2)
        if 1 <= t <= ATT_HEADS:
            softmax(t - 1, (t - 1) % 2)
        if t >= 2:
            weighted_values(t - 2, t % 2)

    @pl.when(j == i)
    def _finish():
        heads = []
        for h in range(ATT_HEADS):
            a = acc_ref[h]
            heads.append((a[:, :ATT_HEAD_DIM] / a[:, ATT_HEAD_DIM:ATT_HEAD_DIM + 1]).astype(MXU_DTYPE))
        o_ref[0] = jnp.dot(jnp.concatenate(heads, axis=1), wout_ref[...], preferred_element_type=f32)


def dsa_attention(att_in, q_idx, kid_t, q, k_t, v_aug, w_out):
    bsz, _, seq, _ = q.shape
    tq = tk = DSA_TILE
    n_sel = min(TOPK_MAX, seq // 4)
    nq = seq // tq
    assert seq % tq == 0 and tk % _SEARCH_LANES == 0 and tq % _SEARCH_ROWS == 0
    pairs = [(a, b) for a in range(nq) for b in range(a + 1)]
    qi = jnp.asarray([a for a, _ in pairs], jnp.int32)
    kj = jnp.asarray([b for _, b in pairs], jnp.int32)
    tri = jnp.triu(jnp.ones((tk, tk), MXU_DTYPE))
    grid_spec = pltpu.PrefetchScalarGridSpec(
        num_scalar_prefetch=2,
        grid=(bsz, len(pairs)),
        in_specs=[
            pl.BlockSpec((1, IDX_HEADS, tq, LANE), lambda b, p, qi, kj: (b, 0, qi[p], 0)),
            pl.BlockSpec((1, tq, LANE), lambda b, p, qi, kj: (b, qi[p], ATT_IDX_BLOCK)),
            pl.BlockSpec((1, LANE, seq), lambda b, p, qi, kj: (b, 0, 0)),
            pl.BlockSpec((1, ATT_HEADS, tq, ATT_QK_DIM), lambda b, p, qi, kj: (b, 0, qi[p], 0)),
            pl.BlockSpec((1, ATT_HEADS, ATT_QK_DIM, tk), lambda b, p, qi, kj: (b, 0, 0, kj[p])),
            pl.BlockSpec((1, ATT_HEADS, tk, LANE), lambda b, p, qi, kj: (b, 0, kj[p], 0)),
            pl.BlockSpec((tk, tk), lambda b, p, qi, kj: (0, 0)),
            pl.BlockSpec((ATT_WIDTH, D_MODEL), lambda b, p, qi, kj: (0, 0)),
        ],
        out_specs=pl.BlockSpec((1, tq, D_MODEL), lambda b, p, qi, kj: (b, qi[p], 0)),
        scratch_shapes=[
            pltpu.VMEM((tq, seq), jnp.float32),
            pltpu.VMEM((tq, 1), jnp.float32),
            pltpu.VMEM((tq, 1), jnp.float32),
            pltpu.VMEM((tq, 1), jnp.float32),
            pltpu.VMEM((ATT_HEADS, tq, 1), jnp.float32),
            pltpu.VMEM((ATT_HEADS, tq, LANE), jnp.float32),
            pltpu.VMEM((tq, tk), jnp.float32),
            pltpu.VMEM((2, tq, tk), jnp.float32),
            pltpu.VMEM((2, tq, tk), MXU_DTYPE),
            pltpu.VMEM((2, tq, 1), jnp.float32),
            pltpu.VMEM((tq, _LIST_DEPTH * LANE), jnp.float32),
        ])
    return pl.pallas_call(
        functools.partial(_dsa_body, tq=tq, tk=tk, seq=seq, n_sel=n_sel),
        grid_spec=grid_spec,
        out_shape=jax.ShapeDtypeStruct((bsz, seq, D_MODEL), jnp.float32),
        compiler_params=pltpu.CompilerParams(
            dimension_semantics=("parallel", "arbitrary"), vmem_limit_bytes=VMEM_LIMIT_BYTES),
        name="dsa_attention",
    )(qi, kj, q_idx, att_in, kid_t, q, k_t, v_aug, tri, w_out)


def dsa_branch(att_in, q_norm_g, kv_norm_g, w_uq, w_qidx, w_ukv, w_out):
    q, q_idx, k_t, v_aug, kid_t = dsa_project(att_in, q_norm_g, kv_norm_g, w_uq, w_qidx, w_ukv)
    return dsa_attention(att_in, q_idx, kid_t, q, k_t, v_aug, w_out.astype(MXU_DTYPE))


MOE_TOKENS = 1024
MOE_ROW_CLASSES = (256, 320, 512, 1024)
MOE_FFN_TILES = (896, 512, 256)
MOE_VMEM_LIMIT_BYTES = 56 * 1024 * 1024


def _moe_body(cnt_ref, h_ref, gate_ref, gatet_ref, tri_ref, wg_ref, wu_ref, wd_ref, g_ref, b_ref, o_ref,
              xb_ref, gather_ref, scatter_ref, xc_ref, yc_ref, acc_ref):
    f32 = jnp.float32
    tm = MOE_TOKENS
    i = pl.program_id(0)
    e = pl.program_id(1)
    f = pl.program_id(2)
    last_f = pl.num_programs(2) - 1

    @pl.when((e == 0) & (f == 0))
    def _():
        xb_ref[...] = h_ref[...].astype(MXU_DTYPE)
        acc_ref[...] = jnp.zeros_like(acc_ref)

    lane = lax.broadcasted_iota(jnp.int32, gate_ref.shape, 1)
    gate_col = jnp.sum(jnp.where(lane == e, gate_ref[...], 0.0), axis=-1, keepdims=True)

    def routed(size):
        def run():
            @pl.when(f == 0)
            def _():
                active_row = jnp.where(gatet_ref[pl.ds(e, 1), :] != 0.0, 1.0, 0.0)
                rank_row = jnp.dot(jnp.broadcast_to(active_row, (8, tm)).astype(MXU_DTYPE), tri_ref[...],
                                   preferred_element_type=f32)[0:1]
                slot = lax.broadcasted_iota(jnp.int32, (size, tm), 0).astype(f32)
                gather = jnp.where((slot == rank_row) & (active_row != 0.0), 1.0, 0.0)
                gather_ref[0:size, :] = gather.astype(MXU_DTYPE)
                rank_col = jnp.broadcast_to(rank_row, (LANE, tm)).T[:, 0:1]
                slot_l = lax.broadcasted_iota(jnp.int32, (tm, size), 1).astype(f32)
                scatter = jnp.where((slot_l == rank_col) & (gate_col != 0.0), 1.0, 0.0)
                scatter_ref[:, 0:size] = scatter.astype(MXU_DTYPE)
                xc_ref[0:size, :] = jnp.dot(gather.astype(MXU_DTYPE), xb_ref[...],
                                            preferred_element_type=f32).astype(MXU_DTYPE)
                yc_ref[0:size, :] = jnp.zeros((size, D_MODEL), f32)

            xc = xc_ref[0:size, :]
            a = jnp.dot(xc, wg_ref[0], preferred_element_type=f32)
            u = jnp.dot(xc, wu_ref[0], preferred_element_type=f32)
            act = a * jax.nn.sigmoid(a) * u
            yc_ref[0:size, :] += jnp.dot(act.astype(MXU_DTYPE), wd_ref[0], preferred_element_type=f32)

            @pl.when(f == last_f)
            def _():
                back = jnp.dot(scatter_ref[:, 0:size], yc_ref[0:size, :].astype(MXU_DTYPE),
                               preferred_element_type=f32)
                acc_ref[...] += gate_col * back
        return run

    count = cnt_ref[i * N_EXPERTS + e]
    size_class = sum((count > s).astype(jnp.int32) for s in MOE_ROW_CLASSES[:-1])
    for k, size in enumerate(MOE_ROW_CLASSES):
        pl.when((size_class == k) & (count > 0))(routed(size))

    @pl.when((e == pl.num_programs(1) - 1) & (f == last_f))
    def _():
        o_ref[...] = _layer_norm_rows(DEEPNORM_ALPHA * h_ref[...] + acc_ref[...], g_ref[...], b_ref[...])


def moe_experts_norm(h, routing, wg, wu, wd, ln_g, ln_b):
    gates, gates_t, counts = routing
    m, d = h.shape
    n_experts, _, f_dim = wg.shape
    tm = MOE_TOKENS
    tf = next(t for t in MOE_FFN_TILES if f_dim % t == 0)
    assert m % tm == 0 and MOE_ROW_CLASSES[-1] == tm
    n_tiles = m // tm
    tri = jnp.triu(jnp.ones((tm, tm), MXU_DTYPE), k=1)
    row = lambda i, e, f, c: (i, 0)
    full = lambda i, e, f, c: (0, 0)
    grid_spec = pltpu.PrefetchScalarGridSpec(
        num_scalar_prefetch=1,
        grid=(n_tiles, n_experts, f_dim // tf),
        in_specs=[pl.BlockSpec((tm, d), row), pl.BlockSpec((tm, LANE), row),
                  pl.BlockSpec((n_experts, tm), lambda i, e, f, c: (0, i)),
                  pl.BlockSpec((tm, tm), full),
                  pl.BlockSpec((1, d, tf), lambda i, e, f, c: (e, 0, f)),
                  pl.BlockSpec((1, d, tf), lambda i, e, f, c: (e, 0, f)),
                  pl.BlockSpec((1, tf, d), lambda i, e, f, c: (e, f, 0)),
                  pl.BlockSpec((1, d), full), pl.BlockSpec((1, d), full)],
        out_specs=pl.BlockSpec((tm, d), row),
        scratch_shapes=[pltpu.VMEM((tm, d), MXU_DTYPE),
                        pltpu.VMEM((tm, tm), MXU_DTYPE),
                        pltpu.VMEM((tm, tm), MXU_DTYPE),
                        pltpu.VMEM((tm, d), MXU_DTYPE),
                        pltpu.VMEM((tm, d), jnp.float32),
                        pltpu.VMEM((tm, d), jnp.float32)])
    return pl.pallas_call(
        _moe_body,
        grid_spec=grid_spec,
        out_shape=jax.ShapeDtypeStruct((m, d), jnp.float32),
        compiler_params=pltpu.CompilerParams(
            dimension_semantics=("parallel", "arbitrary", "arbitrary"),
            vmem_limit_bytes=MOE_VMEM_LIMIT_BYTES),
        name="moe_experts_norm",
    )(counts, h, gates, gates_t, tri, wg, wu, wd, ln_g.reshape(1, d), ln_b.reshape(1, d))


def _router_body(h_ref, r_ref, gate_ref, gatet_ref, cnt_ref):
    f32 = jnp.float32
    logits = jnp.dot(h_ref[...], r_ref[...], preferred_element_type=f32)
    lane = lax.broadcasted_iota(jnp.int32, logits.shape, 1)
    logits = jnp.where(lane < N_EXPERTS, logits, -jnp.inf)
    top1 = jnp.max(logits, axis=1, keepdims=True)
    idx1 = jnp.min(jnp.where(logits == top1, lane, LANE), axis=1, keepdims=True)
    rest = jnp.where(lane == idx1, -jnp.inf, logits)
    top2 = jnp.max(rest, axis=1, keepdims=True)
    idx2 = jnp.min(jnp.where(rest == top2, lane, LANE), axis=1, keepdims=True)
    e2 = jnp.exp(top2 - top1)
    gates = jnp.where(lane == idx1, 1.0 / (1.0 + e2), jnp.where(lane == idx2, e2 / (1.0 + e2), 0.0))
    gate_ref[...] = gates
    gatet_ref[...] = gates.T[0:N_EXPERTS, :]
    cnt_ref[0] = jnp.sum(jnp.where(gates != 0.0, 1.0, 0.0), axis=0, keepdims=True)


def router_gates(h, router):
    m, d = h.shape
    tm = MOE_TOKENS
    assert TOP_K == 2 and m % tm == 0
    r_pad = jnp.pad(router.astype(jnp.float32), ((0, 0), (0, LANE - N_EXPERTS)))
    gates, gates_t, counts = pl.pallas_call(
        _router_body,
        grid=(m // tm,),
        in_specs=[pl.BlockSpec((tm, d), lambda i: (i, 0)), pl.BlockSpec((d, LANE), lambda i: (0, 0))],
        out_specs=[pl.BlockSpec((tm, LANE), lambda i: (i, 0)), pl.BlockSpec((N_EXPERTS, tm), lambda i: (0, i)),
                   pl.BlockSpec((1, 1, LANE), lambda i: (i, 0, 0))],
        out_shape=[jax.ShapeDtypeStruct((m, LANE), jnp.float32),
                   jax.ShapeDtypeStruct((N_EXPERTS, m), jnp.float32),
                   jax.ShapeDtypeStruct((m // tm, 1, LANE), jnp.float32)],
        compiler_params=pltpu.CompilerParams(
            dimension_semantics=("parallel",), vmem_limit_bytes=VMEM_LIMIT_BYTES),
        name="router_gates",
    )(h, r_pad)
    return gates, gates_t, counts[:, 0, :N_EXPERTS].astype(jnp.int32).reshape(-1)


def kernel(x, w_in, ssm_log_dt, ssm_lambda_re, ssm_lambda_im, ssm_b_re, ssm_b_im, ssm_c_re, ssm_c_im,
           ssm_d, ssm_w_glu, ssm_b_glu, ssm_w_out, hgrn_lb_logits, hgrn_norm_g, hgrn_w_out,
           attn_q_norm_g, attn_kv_norm_g, attn_w_uq, attn_w_qidx, attn_w_ukv, attn_w_out, w_o,
           ln_g, ln_b, ffn_w_gate, ffn_w_up, ffn_w_down, moe_router, moe_w_gate, moe_w_up, moe_w_down):
    bsz, seq, d = x.shape
    m = bsz * seq
    bf16 = MXU_DTYPE
    assert MIX_IN_USED + N_BRANCHES * D_MODEL == N_IN
    lb_soft = jax.nn.softmax(hgrn_lb_logits.astype(jnp.float32), axis=0)
    lower_bounds = jnp.concatenate([jnp.zeros_like(lb_soft[:1]), jnp.cumsum(lb_soft[1:], axis=0)], axis=0)
    h = x.reshape(m, d)
    for l in range(DEPTH):
        w_mix = jnp.pad(w_in[l][:, :MIX_IN_USED], ((0, 0), (0, MIX_IN_WIDTH - MIX_IN_USED))).astype(bf16)
        u, hg_in, att_in = in_proj(h, w_mix)
        y_ssm = s5_branch(u.reshape(bsz, seq, SSM_WIDTH), ssm_log_dt[l], ssm_lambda_re[l], ssm_lambda_im[l],
                          ssm_b_re[l], ssm_b_im[l], ssm_c_re[l], ssm_c_im[l], ssm_d[l], ssm_w_glu[l],
                          ssm_b_glu[l], ssm_w_out[l])
        y_hg = hgrn2_branch(hg_in.reshape(bsz, seq, HG_IN_WIDTH), lower_bounds[l], hgrn_norm_g[l],
                            hgrn_w_out[l])
        y_att = dsa_branch(att_in.reshape(bsz, seq, ATT_IN_WIDTH), attn_q_norm_g[l], attn_kv_norm_g[l],
                           attn_w_uq[l], attn_w_qidx[l], attn_w_ukv[l], attn_w_out[l])
        h = merge_project_norm(h, y_ssm.reshape(m, d), y_hg.reshape(m, d), y_att.reshape(m, d),
                               w_in[l][:, MIX_IN_USED:].astype(bf16), w_o[l].astype(bf16),
                               ln_g[l, 0], ln_b[l, 0])
        if l % 2 == 0:
            ones = jnp.ones((m, LANE), jnp.float32)
            h = swiglu_experts_norm(h, ones, ffn_w_gate[l // 2][None].astype(bf16),
                                    ffn_w_up[l // 2][None].astype(bf16),
                                    ffn_w_down[l // 2][None].astype(bf16),
                                    ln_g[l, 1], ln_b[l, 1], tf=256)
        else:
            gate_w = router_gates(h, moe_router[l // 2])
            h = moe_experts_norm(h, gate_w, moe_w_gate[l // 2].astype(bf16),
                                 moe_w_up[l // 2].astype(bf16), moe_w_down[l // 2].astype(bf16),
                                 ln_g[l, 1], ln_b[l, 1])
    return h.reshape(bsz, seq, d)
```

```python
import functools

import jax
import jax.numpy as jnp
import numpy as np
from jax import lax
from jax.experimental import pallas as pl
from jax.experimental.pallas import tpu as pltpu

D_MODEL = 1024
DEPTH = 2
SSM_WIDTH = 256
SSM_GROUP = 16
SSM_GROUPS = SSM_WIDTH // SSM_GROUP
SSM_STATE = 64
HGRN_HEADS = 4
HGRN_DK = 64
HGRN_DV = 64
HGRN_WIDTH = HGRN_HEADS * HGRN_DV
ATT_HEADS = 8
ATT_HEAD_DIM = 64
ATT_WIDTH = ATT_HEADS * ATT_HEAD_DIM
ATT_Q_RANK = 256
ATT_KV_RANK = 128
IDX_HEADS = 4
IDX_DIM = 64
TOPK_MAX = 256
MASK_VALUE = -1e30
N_BRANCHES = 3
N_EXPERTS = 8
TOP_K = 2
DEEPNORM_ALPHA = (2 * DEPTH) ** 0.25
LN_EPS = 1e-5
RMS_EPS = 1e-6
F_MIN = 1e-12

IN_SPLITS = (SSM_WIDTH, HGRN_HEADS * HGRN_DK, HGRN_HEADS * HGRN_DK, HGRN_WIDTH, HGRN_WIDTH,
             ATT_Q_RANK, ATT_KV_RANK, IDX_DIM, IDX_HEADS, N_BRANCHES * D_MODEL)
N_IN = sum(IN_SPLITS)

VMEM_LIMIT_BYTES = 48 * 1024 * 1024
LANE = 128
MXU_DTYPE = jnp.bfloat16


def _round_up(n, m):
    return (n + m - 1) // m * m


HG_IN_WIDTH = 2 * HGRN_HEADS * HGRN_DK + 2 * HGRN_WIDTH
ATT_IN_USED = ATT_Q_RANK + ATT_KV_RANK + IDX_DIM + IDX_HEADS
ATT_IN_WIDTH = _round_up(ATT_IN_USED, LANE)
ATT_IDX_BLOCK = (ATT_Q_RANK + ATT_KV_RANK) // LANE
MIX_IN_USED = SSM_WIDTH + HG_IN_WIDTH + ATT_IN_USED
MIX_IN_WIDTH = SSM_WIDTH + HG_IN_WIDTH + ATT_IN_WIDTH


def _in_proj_body(h_ref, w_ref, u_ref, hg_ref, att_ref):
    r = jnp.dot(h_ref[...].astype(MXU_DTYPE), w_ref[...], preferred_element_type=jnp.float32)
    u_ref[...] = r[:, :SSM_WIDTH]
    hg_ref[...] = r[:, SSM_WIDTH:SSM_WIDTH + HG_IN_WIDTH]
    att_ref[...] = r[:, SSM_WIDTH + HG_IN_WIDTH:]


def in_proj(h, w_mix, *, tm=512):
    m, d = h.shape
    row = lambda i: (i, 0)
    widths = (SSM_WIDTH, HG_IN_WIDTH, ATT_IN_WIDTH)
    return pl.pallas_call(
        _in_proj_body,
        grid=(m // tm,),
        in_specs=[pl.BlockSpec((tm, d), row), pl.BlockSpec((d, MIX_IN_WIDTH), lambda i: (0, 0))],
        out_specs=[pl.BlockSpec((tm, w), row) for w in widths],
        out_shape=[jax.ShapeDtypeStruct((m, w), jnp.float32) for w in widths],
        compiler_params=pltpu.CompilerParams(
            dimension_semantics=("parallel",), vmem_limit_bytes=VMEM_LIMIT_BYTES),
        name="in_proj",
    )(h, w_mix)


def _layer_norm_rows(y, g, b):
    mu = jnp.mean(y, axis=-1, keepdims=True)
    yc = y - mu
    var = jnp.mean(yc * yc, axis=-1, keepdims=True)
    return yc * lax.rsqrt(var + LN_EPS) * g + b


def _merge_body(h_ref, ys_ref, yh_ref, ya_ref, wgate_ref, wo_ref, g_ref, b_ref, o_ref):
    d = D_MODEL
    f32 = jnp.float32
    h = h_ref[...]
    gates = jax.nn.sigmoid(jnp.dot(h.astype(MXU_DTYPE), wgate_ref[...], preferred_element_type=f32))
    mixed = (gates[:, 0:d] * ys_ref[...] + gates[:, d:2 * d] * yh_ref[...]
             + gates[:, 2 * d:3 * d] * ya_ref[...])
    mix_out = jnp.dot(mixed.astype(MXU_DTYPE), wo_ref[...], preferred_element_type=f32)
    o_ref[...] = _layer_norm_rows(DEEPNORM_ALPHA * h + mix_out, g_ref[...], b_ref[...])


def merge_project_norm(h, y_ssm, y_hg, y_att, w_gates, w_o, ln_g, ln_b, *, tm=512):
    m, d = h.shape
    row = lambda i: (i, 0)
    full = lambda i: (0, 0)
    return pl.pallas_call(
        _merge_body,
        grid=(m // tm,),
        in_specs=[pl.BlockSpec((tm, d), row), pl.BlockSpec((tm, d), row), pl.BlockSpec((tm, d), row),
                  pl.BlockSpec((tm, d), row), pl.BlockSpec((d, N_BRANCHES * d), full),
                  pl.BlockSpec((d, d), full), pl.BlockSpec((1, d), full), pl.BlockSpec((1, d), full)],
        out_specs=pl.BlockSpec((tm, d), row),
        out_shape=jax.ShapeDtypeStruct((m, d), jnp.float32),
        compiler_params=pltpu.CompilerParams(
            dimension_semantics=("parallel",), vmem_limit_bytes=VMEM_LIMIT_BYTES),
        name="merge_project_norm",
    )(h, y_ssm, y_hg, y_att, w_gates, w_o, ln_g.reshape(1, d), ln_b.reshape(1, d))


DENSE_FFN_TOKENS = 512
DENSE_FFN_TILES = (1408, 512, 256)


def _ffn_body(h_ref, wg_ref, wu_ref, wd_ref, g_ref, b_ref, o_ref, acc_ref):
    f = pl.program_id(1)

    @pl.when(f == 0)
    def _():
        acc_ref[...] = jnp.zeros_like(acc_ref)

    x = h_ref[...].astype(MXU_DTYPE)
    a = jnp.dot(x, wg_ref[...], preferred_element_type=jnp.float32)
    u = jnp.dot(x, wu_ref[...], preferred_element_type=jnp.float32)
    act = a * jax.nn.sigmoid(a) * u
    acc_ref[...] += jnp.dot(act.astype(MXU_DTYPE), wd_ref[...], preferred_element_type=jnp.float32)

    @pl.when(f == pl.num_programs(1) - 1)
    def _():
        o_ref[...] = _layer_norm_rows(DEEPNORM_ALPHA * h_ref[...] + acc_ref[...], g_ref[...], b_ref[...])


def swiglu_norm(h, wg, wu, wd, ln_g, ln_b):
    m, d = h.shape
    f_dim = wg.shape[1]
    tm = DENSE_FFN_TOKENS
    tf = next(t for t in DENSE_FFN_TILES if f_dim % t == 0)
    assert m % tm == 0
    row = lambda i, f: (i, 0)
    full = lambda i, f: (0, 0)
    return pl.pallas_call(
        _ffn_body,
        grid=(m // tm, f_dim // tf),
        in_specs=[pl.BlockSpec((tm, d), row),
                  pl.BlockSpec((d, tf), lambda i, f: (0, f)), pl.BlockSpec((d, tf), lambda i, f: (0, f)),
                  pl.BlockSpec((tf, d), lambda i, f: (f, 0)),
                  pl.BlockSpec((1, d), full), pl.BlockSpec((1, d), full)],
        out_specs=pl.BlockSpec((tm, d), row),
        out_shape=jax.ShapeDtypeStruct((m, d), jnp.float32),
        scratch_shapes=[pltpu.VMEM((tm, d), jnp.float32)],
        compiler_params=pltpu.CompilerParams(
            dimension_semantics=("parallel", "arbitrary"), vmem_limit_bytes=VMEM_LIMIT_BYTES),
        name="swiglu_norm",
    )(h, wg, wu, wd, ln_g.reshape(1, d), ln_b.reshape(1, d))


S5_CHUNK = 8
S5_CHUNK_WIDTH = S5_CHUNK * SSM_WIDTH
S5_STATE_WIDTH = SSM_GROUPS * SSM_STATE


def s5_operators(log_dt, lam_re, lam_im, b_re, b_im, c_re, c_im):
    f32 = jnp.float32
    n = S5_CHUNK
    lre, lim = lam_re.astype(f32), lam_im.astype(f32)
    dt = jnp.exp(log_dt.astype(f32))[:, None]
    mag = jnp.exp(lre * dt)
    ang = lim * dt
    a_re, a_im = mag * jnp.cos(ang), mag * jnp.sin(ang)
    den = lre * lre + lim * lim
    coef_re = ((a_re - 1.0) * lre + a_im * lim) / den
    coef_im = (a_im * lre - (a_re - 1.0) * lim) / den
    br, bi = b_re.astype(f32), b_im.astype(f32)
    bb_re = coef_re[..., None] * br - coef_im[..., None] * bi
    bb_im = coef_re[..., None] * bi + coef_im[..., None] * br
    j = jnp.arange(n + 1, dtype=f32)[:, None, None]
    pw_re = jnp.exp(j * lre * dt) * jnp.cos(j * ang)
    pw_im = jnp.exp(j * lre * dt) * jnp.sin(j * ang)
    eye = jnp.eye(SSM_GROUPS, dtype=f32)
    cr, ci = c_re.astype(f32), c_im.astype(f32)
    ab_re = pw_re[..., None] * bb_re - pw_im[..., None] * bb_im
    ab_im = pw_re[..., None] * bb_im + pw_im[..., None] * bb_re
    kern = (jnp.einsum('gcp,jgpd->jgcd', cr, ab_re[:n]) - jnp.einsum('gcp,jgpd->jgcd', ci, ab_im[:n]))
    lag = jnp.arange(n)[None, :] - jnp.arange(n)[:, None]
    toep = jnp.where((lag >= 0)[:, :, None, None, None], kern[jnp.maximum(lag, 0)], 0.0)
    t_mat = jnp.einsum('abgcd,gh->agdbhc', toep, eye).reshape(S5_CHUNK_WIDTH, S5_CHUNK_WIDTH)
    v = jnp.stack([ab_re[:n][::-1], ab_im[:n][::-1]])
    w_in = jnp.einsum('rlgpc,gh->lhcrgp', v, eye).reshape(S5_CHUNK_WIDTH, 2 * S5_STATE_WIDTH)
    ar, ai = pw_re[1:], pw_im[1:]
    wo_re = jnp.einsum('gcp,lgp->gplc', cr, ar) - jnp.einsum('gcp,lgp->gplc', ci, ai)
    wo_im = -jnp.einsum('gcp,lgp->gplc', cr, ai) - jnp.einsum('gcp,lgp->gplc', ci, ar)
    w_out = jnp.einsum('rgplc,gh->rhplgc', jnp.stack([wo_re, wo_im]), eye)
    w_out = w_out.reshape(2 * S5_STATE_WIDTH, S5_CHUNK_WIDTH)
    decay = jnp.stack([pw_re[n].reshape(1, S5_STATE_WIDTH), pw_im[n].reshape(1, S5_STATE_WIDTH)])
    return (jnp.concatenate([t_mat, w_in], axis=1).astype(MXU_DTYPE), w_out.astype(MXU_DTYPE), decay)


def _s5_scan_body(u_ref, tw_ref, wout_ref, decay_ref, y_ref, h_ref, s_ref, hs_ref, *, tm):
    f32 = jnp.float32
    sw = S5_STATE_WIDTH

    @pl.when(pl.program_id(1) == 0)
    def _():
        h_ref[...] = jnp.zeros_like(h_ref)

    r = jnp.dot(u_ref[0].astype(MXU_DTYPE), tw_ref[...], preferred_element_type=f32)
    s_ref[...] = r[:, S5_CHUNK_WIDTH:]
    d_re = decay_ref[0]
    d_im = decay_ref[1]

    def eight_chunks(k, carry):
        h_re, h_im = carry
        r0 = pl.multiple_of(k * 8, 8)
        inc = s_ref[pl.ds(r0, 8), :]
        rows_re, rows_im = [], []
        for t in range(8):
            rows_re.append(h_re)
            rows_im.append(h_im)
            h_re, h_im = (d_re * h_re - d_im * h_im + inc[t:t + 1, :sw],
                          d_re * h_im + d_im * h_re + inc[t:t + 1, sw:])
        hs_ref[pl.ds(r0, 8), :] = jnp.concatenate(
            [jnp.concatenate(rows_re, axis=0), jnp.concatenate(rows_im, axis=0)], axis=1)
        return h_re, h_im

    h_re, h_im = lax.fori_loop(0, tm // 8, eight_chunks, (h_ref[0:1, :], h_ref[1:2, :]))
    h_ref[0:1, :] = h_re
    h_ref[1:2, :] = h_im
    y_ref[0] = r[:, :S5_CHUNK_WIDTH] + jnp.dot(hs_ref[...].astype(MXU_DTYPE), wout_ref[...],
                                                preferred_element_type=f32)


def s5_scan(u, tw, w_out, decay, *, tm=256):
    bsz, seq, _ = u.shape
    n_chunks = seq // S5_CHUNK
    tm = min(tm, n_chunks)
    assert seq % S5_CHUNK == 0 and n_chunks % tm == 0 and tm % 8 == 0
    once = pl.Buffered(1)
    y = pl.pallas_call(
        functools.partial(_s5_scan_body, tm=tm),
        grid=(bsz, n_chunks // tm),
        in_specs=[pl.BlockSpec((1, tm, S5_CHUNK_WIDTH), lambda b, i: (b, i, 0)),
                  pl.BlockSpec(tw.shape, lambda b, i: (0, 0), pipeline_mode=once),
                  pl.BlockSpec(w_out.shape, lambda b, i: (0, 0), pipeline_mode=once),
                  pl.BlockSpec(decay.shape, lambda b, i: (0, 0, 0), pipeline_mode=once)],
        out_specs=pl.BlockSpec((1, tm, S5_CHUNK_WIDTH), lambda b, i: (b, i, 0)),
        out_shape=jax.ShapeDtypeStruct((bsz, n_chunks, S5_CHUNK_WIDTH), jnp.float32),
        scratch_shapes=[pltpu.VMEM((2, S5_STATE_WIDTH), jnp.float32),
                        pltpu.VMEM((tm, 2 * S5_STATE_WIDTH), jnp.float32),
                        pltpu.VMEM((tm, 2 * S5_STATE_WIDTH), jnp.float32)],
        compiler_params=pltpu.CompilerParams(
            dimension_semantics=("parallel", "arbitrary"), vmem_limit_bytes=VMEM_LIMIT_BYTES),
        name="s5_scan",
    )(u.reshape(bsz, n_chunks, S5_CHUNK_WIDTH), tw, w_out, decay)
    return y.reshape(bsz, seq, SSM_WIDTH)


def _s5_out_body(y_ref, u_ref, d_ref, wglu_ref, bglu_ref, wout_ref, o_ref):
    f32 = jnp.float32
    y = jax.nn.gelu(y_ref[...] + d_ref[...] * u_ref[...])
    gate = jnp.dot(y.astype(MXU_DTYPE), wglu_ref[...], preferred_element_type=f32) + bglu_ref[...]
    y = y * jax.nn.sigmoid(gate)
    o_ref[...] = jnp.dot(y.astype(MXU_DTYPE), wout_ref[...], preferred_element_type=f32)


def s5_output(y, u, d_skip, w_glu, b_glu, w_out, *, tm=1024):
    m, c = y.shape
    tm = min(tm, m)
    row = lambda i: (i, 0)
    full = lambda i: (0, 0)
    return pl.pallas_call(
        _s5_out_body,
        grid=(m // tm,),
        in_specs=[pl.BlockSpec((tm, c), row), pl.BlockSpec((tm, c), row), pl.BlockSpec((1, c), full),
                  pl.BlockSpec((c, c), full), pl.BlockSpec((1, c), full), pl.BlockSpec((c, D_MODEL), full)],
        out_specs=pl.BlockSpec((tm, D_MODEL), row),
        out_shape=jax.ShapeDtypeStruct((m, D_MODEL), jnp.float32),
        compiler_params=pltpu.CompilerParams(
            dimension_semantics=("parallel",), vmem_limit_bytes=VMEM_LIMIT_BYTES),
        name="s5_output",
    )(y, u, d_skip.reshape(1, c), w_glu.astype(MXU_DTYPE), b_glu.reshape(1, c), w_out.astype(MXU_DTYPE))


def s5_branch(u, log_dt, lam_re, lam_im, b_re, b_im, c_re, c_im, d_skip, w_glu, b_glu, w_out):
    bsz, seq, _ = u.shape
    tw, w_state_out, decay = s5_operators(log_dt, lam_re, lam_im, b_re, b_im, c_re, c_im)
    y = s5_scan(u, tw, w_state_out, decay)
    out = s5_output(y.reshape(bsz * seq, SSM_WIDTH), u.reshape(bsz * seq, SSM_WIDTH),
                    d_skip, w_glu, b_glu, w_out)
    return out.reshape(bsz, seq, D_MODEL)


HG_CHUNK = 128
HG_LEVELS = 7
HG_KDIM = HGRN_HEADS * HGRN_DK


def _hgrn_segment_sums():
    c = HG_CHUNK
    t = np.arange(c)[:, None]
    u = np.arange(c)[None, :]
    blocks = []
    for lvl in range(1, HG_LEVELS + 1):
        m = (t >> lvl << lvl) + (1 << (lvl - 1)) - 1
        right = ((t >> (lvl - 1)) & 1) == 1
        blocks.append(np.where(right, (u > m) & (u <= t), (u > t) & (u <= m)))
    blocks.append(u <= t)
    blocks.append(u > t)
    return np.concatenate(blocks, axis=0).astype(np.float32)


def _hgrn_body(q_ref, z_ref, v_ref, g_ref, seg_ref, lb_ref, ng_ref, hmean_ref, wout_ref, o_ref, st_ref):
    f32 = jnp.float32
    c = HG_CHUNK

    @pl.when(pl.program_id(1) == 0)
    def _():
        st_ref[...] = jnp.zeros_like(st_ref)

    q = q_ref[0]
    z = z_ref[0]
    v = v_ref[0]
    lb = lb_ref[...]
    f = lb + (1.0 - lb) * jax.nn.sigmoid(z)
    logf = jnp.log(jnp.maximum(f, F_MIN))
    kin = (1.0 - lb) * jax.nn.sigmoid(-z)

    p1 = logf.astype(MXU_DTYPE)
    r1 = logf - p1.astype(f32)
    p2 = r1.astype(MXU_DTYPE)
    p3 = (r1 - p2.astype(f32)).astype(MXU_DTYPE)
    seg = seg_ref[...]
    sums = (jnp.dot(seg, p1, preferred_element_type=f32) + jnp.dot(seg, p2, preferred_element_type=f32)
            + jnp.dot(seg, p3, preferred_element_type=f32))

    lane_head = lax.broadcasted_iota(jnp.int32, (c, HG_KDIM), 1) // HGRN_DK
    tok = lax.broadcasted_iota(jnp.int32, (c, HG_KDIM), 0)
    row_t = lax.broadcasted_iota(jnp.int32, (HGRN_HEADS * c, c), 0) % c
    col_s = lax.broadcasted_iota(jnp.int32, (HGRN_HEADS * c, c), 1)

    def per_head_rows(x):
        return jnp.concatenate([jnp.where(lane_head == h, x, 0.0) for h in range(HGRN_HEADS)],
                               axis=0).astype(MXU_DTYPE)

    def scores(ql, kl):
        return lax.dot_general(per_head_rows(ql), kl.astype(MXU_DTYPE), (((1,), (1,)), ((), ())),
                               preferred_element_type=f32)

    att = jnp.where(row_t == col_s, scores(q, kin), 0.0)
    for lvl in range(1, HG_LEVELS + 1):
        decay = jnp.exp(sums[(lvl - 1) * c:lvl * c])
        right = ((tok >> (lvl - 1)) & 1) == 1
        a = scores(jnp.where(right, q * decay, 0.0), jnp.where(right, 0.0, kin * decay))
        att = att + jnp.where((row_t >> lvl) == (col_s >> lvl), a, 0.0)

    b = sums[HG_LEVELS * c:(HG_LEVELS + 1) * c]
    tail = sums[(HG_LEVELS + 1) * c:(HG_LEVELS + 2) * c]
    v_m = v.astype(MXU_DTYPE)
    st = st_ref[...]
    o = lax.dot_general((q * jnp.exp(b)).astype(MXU_DTYPE), st.astype(MXU_DTYPE),
                        (((1,), (1,)), ((), ())), preferred_element_type=f32)
    for h in range(HGRN_HEADS):
        o_h = jnp.dot(att[h * c:(h + 1) * c].astype(MXU_DTYPE), v_m, preferred_element_type=f32)
        o = o + jnp.where(lane_head == h, o_h, 0.0)

    kv = jnp.dot(v.T.astype(MXU_DTYPE), (kin * jnp.exp(tail)).astype(MXU_DTYPE), preferred_element_type=f32)
    sr = lax.broadcasted_iota(jnp.int32, st.shape, 0) // HGRN_DV
    sc = lax.broadcasted_iota(jnp.int32, st.shape, 1) // HGRN_DK
    st_ref[...] = st * jnp.exp(b[c - 1:c, :]) + jnp.where(sr == sc, kv, 0.0)

    o2 = o * o
    o2_hi = o2.astype(MXU_DTYPE)
    o2_lo = (o2 - o2_hi.astype(f32)).astype(MXU_DTYPE)
    ms = (jnp.dot(o2_hi, hmean_ref[...], preferred_element_type=f32)
          + jnp.dot(o2_lo, hmean_ref[...], preferred_element_type=f32))
    g = g_ref[0]
    out = o * lax.rsqrt(ms + RMS_EPS) * ng_ref[...] * (g * jax.nn.sigmoid(g))
    o_ref[0] = jnp.dot(out.astype(MXU_DTYPE), wout_ref[...], preferred_element_type=f32)


def hgrn2_branch(hg_in, lower_bound, norm_g, w_out):
    bsz, seq, _ = hg_in.shape
    assert seq % HG_CHUNK == 0 and HGRN_DK == HGRN_DV and HG_IN_WIDTH == 4 * HG_KDIM
    seg = jnp.asarray(_hgrn_segment_sums(), MXU_DTYPE)
    head_mean = jnp.asarray(np.kron(np.eye(HGRN_HEADS), np.full((HGRN_DV, HGRN_DV), 1.0 / HGRN_DV)), MXU_DTYPE)
    tok = lambda b, i: (b, i, 0)
    full = lambda b, i: (0, 0)
    part = lambda k: pl.BlockSpec((1, HG_CHUNK, HG_KDIM), lambda b, i: (b, i, k))
    return pl.pallas_call(
        _hgrn_body,
        grid=(bsz, seq // HG_CHUNK),
        in_specs=[part(0), part(1), part(2), part(3),
                  pl.BlockSpec(seg.shape, full), pl.BlockSpec((1, HG_KDIM), full),
                  pl.BlockSpec((1, HGRN_WIDTH), full), pl.BlockSpec(head_mean.shape, full),
                  pl.BlockSpec((HGRN_WIDTH, D_MODEL), full)],
        out_specs=pl.BlockSpec((1, HG_CHUNK, D_MODEL), tok),
        out_shape=jax.ShapeDtypeStruct((bsz, seq, D_MODEL), jnp.float32),
        scratch_shapes=[pltpu.VMEM((HGRN_WIDTH, HG_KDIM), jnp.float32)],
        compiler_params=pltpu.CompilerParams(
            dimension_semantics=("parallel", "arbitrary"), vmem_limit_bytes=VMEM_LIMIT_BYTES),
        name="hgrn2",
    )(hg_in, hg_in, hg_in, hg_in, seg, lower_bound.reshape(1, HG_KDIM).astype(jnp.float32),
      jnp.tile(norm_g.astype(jnp.float32), HGRN_HEADS).reshape(1, HGRN_WIDTH), head_mean,
      w_out.astype(MXU_DTYPE))


ATT_QK_DIM = LANE
DSA_TILE = 512


def _rms_rows(x, g):
    return x * lax.rsqrt(jnp.mean(x * x, axis=-1, keepdims=True) + RMS_EPS) * g


def _dsa_project_body(x_ref, gq_ref, gkv_ref, wq_ref, wqi_ref, wkt_ref, wv_ref,
                      q_ref, qi_ref, kt_ref, v_ref, kidt_ref):
    f32 = jnp.float32
    tm = x_ref.shape[1]
    x = x_ref[0]
    cq = _rms_rows(x[:, :ATT_Q_RANK], gq_ref[...]).astype(MXU_DTYPE)
    ckv = _rms_rows(x[:, ATT_Q_RANK:ATT_Q_RANK + ATT_KV_RANK], gkv_ref[...])
    q_all = jnp.dot(cq, wq_ref[...], preferred_element_type=f32)
    qi_all = jnp.dot(cq, wqi_ref[...], preferred_element_type=f32)
    v_all = jnp.dot(ckv.astype(MXU_DTYPE), wv_ref[...], preferred_element_type=f32)
    kt_all = jnp.dot(wkt_ref[...], ckv.T.astype(MXU_DTYPE), preferred_element_type=f32)

    lane = lax.broadcasted_iota(jnp.int32, (tm, LANE), 1)
    t_loc = lax.broadcasted_iota(jnp.int32, (tm, LANE), 0)
    t_even = (t_loc // 2 * 2).astype(f32)
    t_odd = (t_loc % 2).astype(f32)
    sub = lax.broadcasted_iota(jnp.int32, (LANE, tm), 0)
    s_loc = lax.broadcasted_iota(jnp.int32, (LANE, tm), 1)
    k_rows = jnp.where(sub < ATT_HEAD_DIM + 2, 1.0,
                       jnp.where(sub == ATT_HEAD_DIM + 2, (s_loc // 2 * 2).astype(f32), (s_loc % 2).astype(f32)))
    for h in range(ATT_HEADS):
        slope = 2.0 ** (-8.0 * (h + 1) / ATT_HEADS)
        q_cols = jnp.where(lane == ATT_HEAD_DIM, -slope * t_even,
                           jnp.where(lane == ATT_HEAD_DIM + 1, -slope * t_odd, slope))
        q_h = q_all[:, h * LANE:(h + 1) * LANE]
        q_ref[0, h] = jnp.where(lane < ATT_HEAD_DIM, q_h,
                                jnp.where(lane < ATT_HEAD_DIM + 4, q_cols, 0.0)).astype(q_ref.dtype)
        k_h = kt_all[h * LANE:(h + 1) * LANE, :]
        kt_ref[0, h] = jnp.where(sub < ATT_HEAD_DIM, k_h,
                                 jnp.where(sub < ATT_HEAD_DIM + 4, k_rows, 0.0)).astype(kt_ref.dtype)
        v_h = v_all[:, h * LANE:(h + 1) * LANE]
        v_ref[0, h] = jnp.where(lane == ATT_HEAD_DIM, 1.0, v_h).astype(v_ref.dtype)
    for h in range(IDX_HEADS):
        qi_ref[0, h] = qi_all[:, h * LANE:(h + 1) * LANE].astype(qi_ref.dtype)
    kidt_ref[0] = x[:, ATT_IDX_BLOCK * LANE:(ATT_IDX_BLOCK + 1) * LANE].T.astype(kidt_ref.dtype)


def _head_padded(w, n_heads, dim):
    k = w.shape[0]
    return jnp.pad(w.reshape(k, n_heads, dim), ((0, 0), (0, 0), (0, LANE - dim))).reshape(k, n_heads * LANE)


def dsa_project(att_in, q_norm_g, kv_norm_g, w_uq, w_qidx, w_ukv):
    bsz, seq, _ = att_in.shape
    tm = DSA_TILE
    assert seq % tm == 0 and 8 % ATT_HEADS == 0 and tm <= 512
    w_q = _head_padded(w_uq * ATT_HEAD_DIM ** -0.5, ATT_HEADS, ATT_HEAD_DIM).astype(MXU_DTYPE)
    w_qi = _head_padded(w_qidx * IDX_DIM ** -0.5, IDX_HEADS, IDX_DIM).astype(MXU_DTYPE)
    w_kt = _head_padded(w_ukv[:, :ATT_WIDTH], ATT_HEADS, ATT_HEAD_DIM).T.astype(MXU_DTYPE)
    w_v = _head_padded(w_ukv[:, ATT_WIDTH:], ATT_HEADS, ATT_HEAD_DIM).astype(MXU_DTYPE)
    tok = lambda b, i: (b, 0, i, 0)
    full = lambda b, i: (0, 0)
    dt = MXU_DTYPE
    return pl.pallas_call(
        _dsa_project_body,
        grid=(bsz, seq // tm),
        in_specs=[pl.BlockSpec((1, tm, ATT_IN_WIDTH), lambda b, i: (b, i, 0)),
                  pl.BlockSpec((1, ATT_Q_RANK), full), pl.BlockSpec((1, ATT_KV_RANK), full),
                  pl.BlockSpec(w_q.shape, full), pl.BlockSpec(w_qi.shape, full),
                  pl.BlockSpec(w_kt.shape, full), pl.BlockSpec(w_v.shape, full)],
        out_specs=[pl.BlockSpec((1, ATT_HEADS, tm, LANE), tok),
                   pl.BlockSpec((1, IDX_HEADS, tm, LANE), tok),
                   pl.BlockSpec((1, ATT_HEADS, LANE, tm), lambda b, i: (b, 0, 0, i)),
                   pl.BlockSpec((1, ATT_HEADS, tm, LANE), tok),
                   pl.BlockSpec((1, LANE, tm), lambda b, i: (b, 0, i))],
        out_shape=[jax.ShapeDtypeStruct((bsz, ATT_HEADS, seq, LANE), dt),
                   jax.ShapeDtypeStruct((bsz, IDX_HEADS, seq, LANE), dt),
                   jax.ShapeDtypeStruct((bsz, ATT_HEADS, LANE, seq), dt),
                   jax.ShapeDtypeStruct((bsz, ATT_HEADS, seq, LANE), dt),
                   jax.ShapeDtypeStruct((bsz, LANE, seq), dt)],
        compiler_params=pltpu.CompilerParams(
            dimension_semantics=("parallel", "parallel"), vmem_limit_bytes=VMEM_LIMIT_BYTES),
        name="dsa_project",
    )(att_in, q_norm_g.reshape(1, ATT_Q_RANK), kv_norm_g.reshape(1, ATT_KV_RANK), w_q, w_qi, w_kt, w_v)


_INT_MIN = -2 ** 31
_MASK_KEY = int(np.float32(MASK_VALUE).view(np.int32)) ^ 0x7FFFFFFF
_SEARCH_ROWS = 128
_ATT_ROWS = 32
_SEARCH_LANES = 512
_LIST_DEPTH = 12
_LIST_ROWS = 16
_LIST_MIN_BLOCKS = 4


def _key_to_score(key):
    return lax.bitcast_convert_type(jnp.where(key < 0, key ^ 0x7FFFFFFF, key), jnp.float32)


def _dsa_body(qi_ref, kj_ref, qidx_ref, w_ref, kidt_ref, q_ref, kt_ref, v_ref, tri_ref, wout_ref,
              o_ref, keys_ref, thr_ref, need_ref, carry_ref, m_ref, acc_ref, bias_ref, s_ref, p_ref,
              alpha_ref, cand_ref, *, tq, tk, seq, n_sel):
    f32 = jnp.float32
    p_id = pl.program_id(1)
    i = qi_ref[p_id]
    j = kj_ref[p_id]
    row_local = lax.broadcasted_iota(jnp.int32, (tq, tk), 0)
    col_local = lax.broadcasted_iota(jnp.int32, (tq, tk), 1)

    @pl.when(j == 0)
    def _select():
        def score_block(jj, carry):
            off = pl.multiple_of(jj * tk, tk)
            kb = kidt_ref[0, :, pl.ds(off, tk)]
            sc = jnp.zeros((tq, tk), f32)
            for h in range(IDX_HEADS):
                s = jnp.dot(qidx_ref[0, h], kb, preferred_element_type=f32)
                w_h = w_ref[0, :, IDX_DIM + h:IDX_DIM + h + 1] * IDX_HEADS ** -0.5
                sc = sc + jnp.maximum(s, 0.0) * w_h
            causal = (col_local + jj * tk) <= (row_local + i * tq)
            sc = jnp.where(causal, sc, MASK_VALUE)
            keys_ref[:, pl.ds(off, tk)] = jnp.where(sc == 0.0, 0.0, sc)
            return carry

        lax.fori_loop(0, i + 1, score_block, 0)

        n_blocks = i + 1
        n_masked_tail = (seq - n_blocks * tk).astype(f32)

        def row_group(r, carry):
            r0 = pl.multiple_of(r * _SEARCH_ROWS, _SEARCH_ROWS)

            def count_ge(ref, n_iter, cand):
                cand_b = jnp.broadcast_to(_key_to_score(cand), (_SEARCH_ROWS, LANE))

                def chunk(c, acc):
                    base = pl.multiple_of(c * _SEARCH_LANES, _SEARCH_LANES)
                    for u in range(_SEARCH_LANES // LANE):
                        kk = ref[pl.ds(r0, _SEARCH_ROWS), pl.ds(base + u * LANE, LANE)]
                        acc = acc + jnp.where(kk >= cand_b, 1.0, 0.0)
                    return acc

                acc = lax.fori_loop(0, n_iter, chunk, jnp.zeros((_SEARCH_ROWS, LANE), f32))
                cnt = jnp.sum(acc, axis=1, keepdims=True)
                return cnt + jnp.where(cand <= _MASK_KEY, n_masked_tail, 0.0)

            def kth_largest_key(count):
                zero = jnp.zeros((_SEARCH_ROWS, 1), jnp.int32)
                v0 = jnp.where(count(zero) >= n_sel, zero, zero + _INT_MIN)

                def bit_step(b, v):
                    cand = v | jnp.left_shift(jnp.int32(1), 30 - b)
                    return jnp.where(count(cand) >= n_sel, cand, v)

                return lax.fori_loop(0, 31, bit_step, v0)

            count_all = functools.partial(count_ge, keys_ref, n_blocks * (tk // _SEARCH_LANES))

            def store(v, n_above):
                thr_ref[pl.ds(r0, _SEARCH_ROWS), :] = _key_to_score(v)
                need_ref[pl.ds(r0, _SEARCH_ROWS), :] = n_sel - n_above

            def search_all_keys():
                v = kth_largest_key(count_all)
                store(v, count_all(v + 1))

            @pl.when(n_blocks < _LIST_MIN_BLOCKS)
            def _():
                search_all_keys()

            @pl.when(n_blocks >= _LIST_MIN_BLOCKS)
            def _():
                def shortlist(sub, carry):
                    rr = pl.multiple_of(r0 + sub * _LIST_ROWS, _LIST_ROWS)

                    def insert_block(c, tops):
                        base = pl.multiple_of(c * tk, tk)
                        for u in range(tk // LANE):
                            x = keys_ref[pl.ds(rr, _LIST_ROWS), pl.ds(base + u * LANE, LANE)]
                            new = []
                            for t in range(_LIST_DEPTH):
                                new.append(jnp.maximum(tops[t], x))
                                x = jnp.minimum(tops[t], x)
                            tops = tuple(new)
                        return tops

                    tops = lax.fori_loop(0, n_blocks, insert_block,
                                         tuple(jnp.full((_LIST_ROWS, LANE), -jnp.inf, f32)
                                               for _ in range(_LIST_DEPTH)))
                    for t in range(_LIST_DEPTH):
                        cand_ref[pl.ds(rr, _LIST_ROWS), t * LANE:(t + 1) * LANE] = tops[t]
                    return carry

                lax.fori_loop(0, _SEARCH_ROWS // _LIST_ROWS, shortlist, 0)
                v = kth_largest_key(functools.partial(count_ge, cand_ref,
                                                      _LIST_DEPTH * LANE // _SEARCH_LANES))
                n_at_least = count_all(v)
                n_above = count_all(v + 1)
                store(v, n_above)
                exact = (n_above < n_sel) & (n_at_least >= n_sel)
                pl.when(jnp.min(jnp.where(exact, 1.0, 0.0)) < 0.5)(search_all_keys)

            return carry

        lax.fori_loop(0, tq // _SEARCH_ROWS, row_group, 0)
        carry_ref[...] = jnp.zeros_like(carry_ref)
        m_ref[...] = jnp.full_like(m_ref, MASK_VALUE)
        acc_ref[...] = jnp.zeros_like(acc_ref)

    keys_blk = keys_ref[:, pl.ds(pl.multiple_of(j * tk, tk), tk)]
    thr = thr_ref[...]
    eq = keys_blk == thr
    tie_rank = carry_ref[...] + jnp.dot(jnp.where(eq, 1.0, 0.0).astype(MXU_DTYPE), tri_ref[...],
                                         preferred_element_type=f32)
    carry_ref[...] = tie_rank[:, tk - 1:tk]
    sel = (keys_blk > thr) | (eq & (tie_rank <= need_ref[...]))
    causal = (row_local + i * tq) >= (col_local + j * tk)
    bias_ref[...] = jnp.where(sel & causal, 0.0, MASK_VALUE)

    def logits(h, slot):
        s_ref[slot] = jnp.dot(q_ref[0, h], kt_ref[0, h], preferred_element_type=f32)

    def softmax(h, slot):
        slope_tk = jnp.right_shift(tk, h + 1).astype(f32)
        for r in range(0, tq, _ATT_ROWS):
            rows = pl.ds(r, _ATT_ROWS)
            s = s_ref[slot, rows, :] + bias_ref[rows, :]
            m_old = m_ref[h, rows, :] - slope_tk
            m_new = jnp.maximum(m_old, jnp.max(s, axis=1, keepdims=True))
            p_ref[slot, rows, :] = jnp.exp(s - m_new).astype(MXU_DTYPE)
            alpha_ref[slot, rows, :] = jnp.exp(m_old - m_new)
            m_ref[h, rows, :] = m_new

    def weighted_values(h, slot):
        acc_ref[h] = alpha_ref[slot] * acc_ref[h] + jnp.dot(p_ref[slot], v_ref[0, h],
                                                             preferred_element_type=f32)

    for t in range(ATT_HEADS + 2):
        if t < ATT_HEADS:
            logits(t, t % 2)
        if 1 <= t <= ATT_HEADS:
            softmax(t - 1, (t - 1) % 2)
        if t >= 2:
            weighted_values(t - 2, t % 2)

    @pl.when(j == i)
    def _finish():
        heads = []
        for h in range(ATT_HEADS):
            a = acc_ref[h]
            heads.append((a[:, :ATT_HEAD_DIM] / a[:, ATT_HEAD_DIM:ATT_HEAD_DIM + 1]).astype(MXU_DTYPE))
        o_ref[0] = jnp.dot(jnp.concatenate(heads, axis=1), wout_ref[...], preferred_element_type=f32)


def dsa_attention(att_in, q_idx, kid_t, q, k_t, v_aug, w_out):
    bsz, _, seq, _ = q.shape
    tq = tk = DSA_TILE
    n_sel = min(TOPK_MAX, seq // 4)
    nq = seq // tq
    assert seq % tq == 0 and tk % _SEARCH_LANES == 0 and tq % _SEARCH_ROWS == 0
    pairs = [(a, b) for a in range(nq) for b in range(a + 1)]
    qi = jnp.asarray([a for a, _ in pairs], jnp.int32)
    kj = jnp.asarray([b for _, b in pairs], jnp.int32)
    tri = jnp.triu(jnp.ones((tk, tk), MXU_DTYPE))
    grid_spec = pltpu.PrefetchScalarGridSpec(
        num_scalar_prefetch=2,
        grid=(bsz, len(pairs)),
        in_specs=[
            pl.BlockSpec((1, IDX_HEADS, tq, LANE), lambda b, p, qi, kj: (b, 0, qi[p], 0)),
            pl.BlockSpec((1, tq, LANE), lambda b, p, qi, kj: (b, qi[p], ATT_IDX_BLOCK)),
            pl.BlockSpec((1, LANE, seq), lambda b, p, qi, kj: (b, 0, 0)),
            pl.BlockSpec((1, ATT_HEADS, tq, ATT_QK_DIM), lambda b, p, qi, kj: (b, 0, qi[p], 0)),
            pl.BlockSpec((1, ATT_HEADS, ATT_QK_DIM, tk), lambda b, p, qi, kj: (b, 0, 0, kj[p])),
            pl.BlockSpec((1, ATT_HEADS, tk, LANE), lambda b, p, qi, kj: (b, 0, kj[p], 0)),
            pl.BlockSpec((tk, tk), lambda b, p, qi, kj: (0, 0)),
            pl.BlockSpec((ATT_WIDTH, D_MODEL), lambda b, p, qi, kj: (0, 0)),
        ],
        out_specs=pl.BlockSpec((1, tq, D_MODEL), lambda b, p, qi, kj: (b, qi[p], 0)),
        scratch_shapes=[
            pltpu.VMEM((tq, seq), jnp.float32),
            pltpu.VMEM((tq, 1), jnp.float32),
            pltpu.VMEM((tq, 1), jnp.float32),
            pltpu.VMEM((tq, 1), jnp.float32),
            pltpu.VMEM((ATT_HEADS, tq, 1), jnp.float32),
            pltpu.VMEM((ATT_HEADS, tq, LANE), jnp.float32),
            pltpu.VMEM((tq, tk), jnp.float32),
            pltpu.VMEM((2, tq, tk), jnp.float32),
            pltpu.VMEM((2, tq, tk), MXU_DTYPE),
            pltpu.VMEM((2, tq, 1), jnp.float32),
            pltpu.VMEM((tq, _LIST_DEPTH * LANE), jnp.float32),
        ])
    return pl.pallas_call(
        functools.partial(_dsa_body, tq=tq, tk=tk, seq=seq, n_sel=n_sel),
        grid_spec=grid_spec,
        out_shape=jax.ShapeDtypeStruct((bsz, seq, D_MODEL), jnp.float32),
        compiler_params=pltpu.CompilerParams(
            dimension_semantics=("parallel", "arbitrary"), vmem_limit_bytes=VMEM_LIMIT_BYTES),
        name="dsa_attention",
    )(qi, kj, q_idx, att_in, kid_t, q, k_t, v_aug, tri, w_out)


def dsa_branch(att_in, q_norm_g, kv_norm_g, w_uq, w_qidx, w_ukv, w_out):
    q, q_idx, k_t, v_aug, kid_t = dsa_project(att_in, q_norm_g, kv_norm_g, w_uq, w_qidx, w_ukv)
    return dsa_attention(att_in, q_idx, kid_t, q, k_t, v_aug, w_out.astype(MXU_DTYPE))


MOE_TOKENS = 1024
MOE_ROW_CLASSES = (256, 320, 512, 1024)
MOE_FFN_TILES = (896, 512, 256)
MOE_VMEM_LIMIT_BYTES = 56 * 1024 * 1024


def _moe_body(cnt_ref, h_ref, gate_ref, gatet_ref, tri_ref, wg_ref, wu_ref, wd_ref, g_ref, b_ref, o_ref,
              xb_ref, gather_ref, scatter_ref, xc_ref, yc_ref, acc_ref):
    f32 = jnp.float32
    tm = MOE_TOKENS
    i = pl.program_id(0)
    e = pl.program_id(1)
    f = pl.program_id(2)
    last_f = pl.num_programs(2) - 1

    @pl.when((e == 0) & (f == 0))
    def _():
        xb_ref[...] = h_ref[...].astype(MXU_DTYPE)
        acc_ref[...] = jnp.zeros_like(acc_ref)

    lane = lax.broadcasted_iota(jnp.int32, gate_ref.shape, 1)
    gate_col = jnp.sum(jnp.where(lane == e, gate_ref[...], 0.0), axis=-1, keepdims=True)

    def routed(size):
        def run():
            @pl.when(f == 0)
            def _():
                active_row = jnp.where(gatet_ref[pl.ds(e, 1), :] != 0.0, 1.0, 0.0)
                rank_row = jnp.dot(jnp.broadcast_to(active_row, (8, tm)).astype(MXU_DTYPE), tri_ref[...],
                                   preferred_element_type=f32)[0:1]
                slot = lax.broadcasted_iota(jnp.int32, (size, tm), 0).astype(f32)
                gather = jnp.where((slot == rank_row) & (active_row != 0.0), 1.0, 0.0)
                gather_ref[0:size, :] = gather.astype(MXU_DTYPE)
                rank_col = jnp.broadcast_to(rank_row, (LANE, tm)).T[:, 0:1]
                slot_l = lax.broadcasted_iota(jnp.int32, (tm, size), 1).astype(f32)
                scatter = jnp.where((slot_l == rank_col) & (gate_col != 0.0), 1.0, 0.0)
                scatter_ref[:, 0:size] = scatter.astype(MXU_DTYPE)
                xc_ref[0:size, :] = jnp.dot(gather.astype(MXU_DTYPE), xb_ref[...],
                                            preferred_element_type=f32).astype(MXU_DTYPE)
                yc_ref[0:size, :] = jnp.zeros((size, D_MODEL), f32)

            xc = xc_ref[0:size, :]
            a = jnp.dot(xc, wg_ref[0], preferred_element_type=f32)
            u = jnp.dot(xc, wu_ref[0], preferred_element_type=f32)
            act = a * jax.nn.sigmoid(a) * u
            yc_ref[0:size, :] += jnp.dot(act.astype(MXU_DTYPE), wd_ref[0], preferred_element_type=f32)

            @pl.when(f == last_f)
            def _():
                back = jnp.dot(scatter_ref[:, 0:size], yc_ref[0:size, :].astype(MXU_DTYPE),
                               preferred_element_type=f32)
                acc_ref[...] += gate_col * back
        return run

    count = cnt_ref[i * N_EXPERTS + e]
    size_class = sum((count > s).astype(jnp.int32) for s in MOE_ROW_CLASSES[:-1])
    for k, size in enumerate(MOE_ROW_CLASSES):
        pl.when((size_class == k) & (count > 0))(routed(size))

    @pl.when((e == pl.num_programs(1) - 1) & (f == last_f))
    def _():
        o_ref[...] = _layer_norm_rows(DEEPNORM_ALPHA * h_ref[...] + acc_ref[...], g_ref[...], b_ref[...])


def moe_experts_norm(h, routing, wg, wu, wd, ln_g, ln_b):
    gates, gates_t, counts = routing
    m, d = h.shape
    n_experts, _, f_dim = wg.shape
    tm = MOE_TOKENS
    tf = next(t for t in MOE_FFN_TILES if f_dim % t == 0)
    assert m % tm == 0 and MOE_ROW_CLASSES[-1] == tm
    n_tiles = m // tm
    tri = jnp.triu(jnp.ones((tm, tm), MXU_DTYPE), k=1)
    row = lambda i, e, f, c: (i, 0)
    full = lambda i, e, f, c: (0, 0)
    grid_spec = pltpu.PrefetchScalarGridSpec(
        num_scalar_prefetch=1,
        grid=(n_tiles, n_experts, f_dim // tf),
        in_specs=[pl.BlockSpec((tm, d), row), pl.BlockSpec((tm, LANE), row),
                  pl.BlockSpec((n_experts, tm), lambda i, e, f, c: (0, i)),
                  pl.BlockSpec((tm, tm), full),
                  pl.BlockSpec((1, d, tf), lambda i, e, f, c: (e, 0, f)),
                  pl.BlockSpec((1, d, tf), lambda i, e, f, c: (e, 0, f)),
                  pl.BlockSpec((1, tf, d), lambda i, e, f, c: (e, f, 0)),
                  pl.BlockSpec((1, d), full), pl.BlockSpec((1, d), full)],
        out_specs=pl.BlockSpec((tm, d), row),
        scratch_shapes=[pltpu.VMEM((tm, d), MXU_DTYPE),
                        pltpu.VMEM((tm, tm), MXU_DTYPE),
                        pltpu.VMEM((tm, tm), MXU_DTYPE),
                        pltpu.VMEM((tm, d), MXU_DTYPE),
                        pltpu.VMEM((tm, d), jnp.float32),
                        pltpu.VMEM((tm, d), jnp.float32)])
    return pl.pallas_call(
        _moe_body,
        grid_spec=grid_spec,
        out_shape=jax.ShapeDtypeStruct((m, d), jnp.float32),
        compiler_params=pltpu.CompilerParams(
            dimension_semantics=("parallel", "arbitrary", "arbitrary"),
            vmem_limit_bytes=MOE_VMEM_LIMIT_BYTES),
        name="moe_experts_norm",
    )(counts, h, gates, gates_t, tri, wg, wu, wd, ln_g.reshape(1, d), ln_b.reshape(1, d))


def _router_body(h_ref, r_ref, gate_ref, gatet_ref, cnt_ref):
    f32 = jnp.float32
    logits = jnp.dot(h_ref[...], r_ref[...], preferred_element_type=f32)
    lane = lax.broadcasted_iota(jnp.int32, logits.shape, 1)
    logits = jnp.where(lane < N_EXPERTS, logits, -jnp.inf)
    top1 = jnp.max(logits, axis=1, keepdims=True)
    idx1 = jnp.min(jnp.where(logits == top1, lane, LANE), axis=1, keepdims=True)
    rest = jnp.where(lane == idx1, -jnp.inf, logits)
    top2 = jnp.max(rest, axis=1, keepdims=True)
    idx2 = jnp.min(jnp.where(rest == top2, lane, LANE), axis=1, keepdims=True)
    e2 = jnp.exp(top2 - top1)
    gates = jnp.where(lane == idx1, 1.0 / (1.0 + e2), jnp.where(lane == idx2, e2 / (1.0 + e2), 0.0))
    gate_ref[...] = gates
    gatet_ref[...] = gates.T[0:N_EXPERTS, :]
    cnt_ref[0] = jnp.sum(jnp.where(gates != 0.0, 1.0, 0.0), axis=0, keepdims=True)


def router_gates(h, router):
    m, d = h.shape
    tm = MOE_TOKENS
    assert TOP_K == 2 and m % tm == 0
    r_pad = jnp.pad(router.astype(jnp.float32), ((0, 0), (0, LANE - N_EXPERTS)))
    gates, gates_t, counts = pl.pallas_call(
        _router_body,
        grid=(m // tm,),
        in_specs=[pl.BlockSpec((tm, d), lambda i: (i, 0)), pl.BlockSpec((d, LANE), lambda i: (0, 0))],
        out_specs=[pl.BlockSpec((tm, LANE), lambda i: (i, 0)), pl.BlockSpec((N_EXPERTS, tm), lambda i: (0, i)),
                   pl.BlockSpec((1, 1, LANE), lambda i: (i, 0, 0))],
        out_shape=[jax.ShapeDtypeStruct((m, LANE), jnp.float32),
                   jax.ShapeDtypeStruct((N_EXPERTS, m), jnp.float32),
                   jax.ShapeDtypeStruct((m // tm, 1, LANE), jnp.float32)],
        compiler_params=pltpu.CompilerParams(
            dimension_semantics=("parallel",), vmem_limit_bytes=VMEM_LIMIT_BYTES),
        name="router_gates",
    )(h, r_pad)
    return gates, gates_t, counts[:, 0, :N_EXPERTS].astype(jnp.int32).reshape(-1)


def kernel(x, w_in, ssm_log_dt, ssm_lambda_re, ssm_lambda_im, ssm_b_re, ssm_b_im, ssm_c_re, ssm_c_im,
           ssm_d, ssm_w_glu, ssm_b_glu, ssm_w_out, hgrn_lb_logits, hgrn_norm_g, hgrn_w_out,
           attn_q_norm_g, attn_kv_norm_g, attn_w_uq, attn_w_qidx, attn_w_ukv, attn_w_out, w_o,
           ln_g, ln_b, ffn_w_gate, ffn_w_up, ffn_w_down, moe_router, moe_w_gate, moe_w_up, moe_w_down):
    bsz, seq, d = x.shape
    m = bsz * seq
    bf16 = MXU_DTYPE
    assert MIX_IN_USED + N_BRANCHES * D_MODEL == N_IN
    lb_soft = jax.nn.softmax(hgrn_lb_logits.astype(jnp.float32), axis=0)
    lower_bounds = jnp.concatenate([jnp.zeros_like(lb_soft[:1]), jnp.cumsum(lb_soft[1:], axis=0)], axis=0)
    h = x.reshape(m, d)
    for l in range(DEPTH):
        w_mix = jnp.pad(w_in[l][:, :MIX_IN_USED], ((0, 0), (0, MIX_IN_WIDTH - MIX_IN_USED))).astype(bf16)
        u, hg_in, att_in = in_proj(h, w_mix)
        y_ssm = s5_branch(u.reshape(bsz, seq, SSM_WIDTH), ssm_log_dt[l], ssm_lambda_re[l], ssm_lambda_im[l],
                          ssm_b_re[l], ssm_b_im[l], ssm_c_re[l], ssm_c_im[l], ssm_d[l], ssm_w_glu[l],
                          ssm_b_glu[l], ssm_w_out[l])
        y_hg = hgrn2_branch(hg_in.reshape(bsz, seq, HG_IN_WIDTH), lower_bounds[l], hgrn_norm_g[l],
                            hgrn_w_out[l])
        y_att = dsa_branch(att_in.reshape(bsz, seq, ATT_IN_WIDTH), attn_q_norm_g[l], attn_kv_norm_g[l],
                           attn_w_uq[l], attn_w_qidx[l], attn_w_ukv[l], attn_w_out[l])
        h = merge_project_norm(h, y_ssm.reshape(m, d), y_hg.reshape(m, d), y_att.reshape(m, d),
                               w_in[l][:, MIX_IN_USED:].astype(bf16), w_o[l].astype(bf16),
                               ln_g[l, 0], ln_b[l, 0])
        if l % 2 == 0:
            h = swiglu_norm(h, ffn_w_gate[l // 2].astype(bf16), ffn_w_up[l // 2].astype(bf16),
                            ffn_w_down[l // 2].astype(bf16), ln_g[l, 1], ln_b[l, 1])
        else:
            gate_w = router_gates(h, moe_router[l // 2])
            h = moe_experts_norm(h, gate_w, moe_w_gate[l // 2].astype(bf16),
                                 moe_w_up[l // 2].astype(bf16), moe_w_down[l // 2].astype(bf16),
                                 ln_g[l, 1], ln_b[l, 1])
    return h.reshape(bsz, seq, d)
```

```python
import functools

import jax
import jax.numpy as jnp
import numpy as np
from jax import lax
from jax.experimental import pallas as pl
from jax.experimental.pallas import tpu as pltpu

D_MODEL = 1024
DEPTH = 2
SSM_WIDTH = 256
SSM_GROUP = 16
SSM_GROUPS = SSM_WIDTH // SSM_GROUP
SSM_STATE = 64
HGRN_HEADS = 4
HGRN_DK = 64
HGRN_DV = 64
HGRN_WIDTH = HGRN_HEADS * HGRN_DV
ATT_HEADS = 8
ATT_HEAD_DIM = 64
ATT_WIDTH = ATT_HEADS * ATT_HEAD_DIM
ATT_Q_RANK = 256
ATT_KV_RANK = 128
IDX_HEADS = 4
IDX_DIM = 64
TOPK_MAX = 256
MASK_VALUE = -1e30
N_BRANCHES = 3
N_EXPERTS = 8
TOP_K = 2
DEEPNORM_ALPHA = (2 * DEPTH) ** 0.25
LN_EPS = 1e-5
RMS_EPS = 1e-6
F_MIN = 1e-12

IN_SPLITS = (SSM_WIDTH, HGRN_HEADS * HGRN_DK, HGRN_HEADS * HGRN_DK, HGRN_WIDTH, HGRN_WIDTH,
             ATT_Q_RANK, ATT_KV_RANK, IDX_DIM, IDX_HEADS, N_BRANCHES * D_MODEL)
N_IN = sum(IN_SPLITS)

VMEM_LIMIT_BYTES = 48 * 1024 * 1024
VMEM_LIMIT_LARGE_BYTES = 56 * 1024 * 1024
LANE = 128
MXU_DTYPE = jnp.bfloat16


def _round_up(n, m):
    return (n + m - 1) // m * m


HG_IN_WIDTH = 2 * HGRN_HEADS * HGRN_DK + 2 * HGRN_WIDTH
ATT_IN_USED = ATT_Q_RANK + ATT_KV_RANK + IDX_DIM + IDX_HEADS
ATT_IN_WIDTH = _round_up(ATT_IN_USED, LANE)
ATT_IDX_BLOCK = (ATT_Q_RANK + ATT_KV_RANK) // LANE
MIX_IN_USED = SSM_WIDTH + HG_IN_WIDTH + ATT_IN_USED
MIX_IN_WIDTH = SSM_WIDTH + HG_IN_WIDTH + ATT_IN_WIDTH


def _in_proj_body(h_ref, w_ref, u_ref, hg_ref, att_ref):
    r = jnp.dot(h_ref[...].astype(MXU_DTYPE), w_ref[...], preferred_element_type=jnp.float32)
    u_ref[...] = r[:, :SSM_WIDTH]
    hg_ref[...] = r[:, SSM_WIDTH:SSM_WIDTH + HG_IN_WIDTH]
    att_ref[...] = r[:, SSM_WIDTH + HG_IN_WIDTH:]


def in_proj(h, w_mix, *, tm=512):
    m, d = h.shape
    row = lambda i: (i, 0)
    widths = (SSM_WIDTH, HG_IN_WIDTH, ATT_IN_WIDTH)
    return pl.pallas_call(
        _in_proj_body,
        grid=(m // tm,),
        in_specs=[pl.BlockSpec((tm, d), row), pl.BlockSpec((d, MIX_IN_WIDTH), lambda i: (0, 0))],
        out_specs=[pl.BlockSpec((tm, w), row) for w in widths],
        out_shape=[jax.ShapeDtypeStruct((m, w), jnp.float32) for w in widths],
        compiler_params=pltpu.CompilerParams(
            dimension_semantics=("parallel",), vmem_limit_bytes=VMEM_LIMIT_BYTES),
        name="in_proj",
    )(h, w_mix)


def _layer_norm_rows(y, g, b):
    mu = jnp.mean(y, axis=-1, keepdims=True)
    yc = y - mu
    var = jnp.mean(yc * yc, axis=-1, keepdims=True)
    return yc * lax.rsqrt(var + LN_EPS) * g + b


def _merge_body(h_ref, ys_ref, yh_ref, ya_ref, wgate_ref, wo_ref, g_ref, b_ref, o_ref):
    d = D_MODEL
    f32 = jnp.float32
    h = h_ref[...]
    gates = jax.nn.sigmoid(jnp.dot(h.astype(MXU_DTYPE), wgate_ref[...], preferred_element_type=f32))
    mixed = (gates[:, 0:d] * ys_ref[...] + gates[:, d:2 * d] * yh_ref[...]
             + gates[:, 2 * d:3 * d] * ya_ref[...])
    mix_out = jnp.dot(mixed.astype(MXU_DTYPE), wo_ref[...], preferred_element_type=f32)
    o_ref[...] = _layer_norm_rows(DEEPNORM_ALPHA * h + mix_out, g_ref[...], b_ref[...])


def merge_project_norm(h, y_ssm, y_hg, y_att, w_gates, w_o, ln_g, ln_b, *, tm=512):
    m, d = h.shape
    row = lambda i: (i, 0)
    full = lambda i: (0, 0)
    return pl.pallas_call(
        _merge_body,
        grid=(m // tm,),
        in_specs=[pl.BlockSpec((tm, d), row), pl.BlockSpec((tm, d), row), pl.BlockSpec((tm, d), row),
                  pl.BlockSpec((tm, d), row), pl.BlockSpec((d, N_BRANCHES * d), full),
                  pl.BlockSpec((d, d), full), pl.BlockSpec((1, d), full), pl.BlockSpec((1, d), full)],
        out_specs=pl.BlockSpec((tm, d), row),
        out_shape=jax.ShapeDtypeStruct((m, d), jnp.float32),
        compiler_params=pltpu.CompilerParams(
            dimension_semantics=("parallel",), vmem_limit_bytes=VMEM_LIMIT_BYTES),
        name="merge_project_norm",
    )(h, y_ssm, y_hg, y_att, w_gates, w_o, ln_g.reshape(1, d), ln_b.reshape(1, d))


DENSE_FFN_TOKENS = 512
DENSE_FFN_TILES = (1408, 512, 256)


def _ffn_body(h_ref, wg_ref, wu_ref, wd_ref, g_ref, b_ref, o_ref, acc_ref):
    f = pl.program_id(1)

    @pl.when(f == 0)
    def _():
        acc_ref[...] = jnp.zeros_like(acc_ref)

    x = h_ref[...].astype(MXU_DTYPE)
    a = jnp.dot(x, wg_ref[...], preferred_element_type=jnp.float32)
    u = jnp.dot(x, wu_ref[...], preferred_element_type=jnp.float32)
    act = a * jax.nn.sigmoid(a) * u
    acc_ref[...] += jnp.dot(act.astype(MXU_DTYPE), wd_ref[...], preferred_element_type=jnp.float32)

    @pl.when(f == pl.num_programs(1) - 1)
    def _():
        o_ref[...] = _layer_norm_rows(DEEPNORM_ALPHA * h_ref[...] + acc_ref[...], g_ref[...], b_ref[...])


def swiglu_norm(h, wg, wu, wd, ln_g, ln_b):
    m, d = h.shape
    f_dim = wg.shape[1]
    tm = DENSE_FFN_TOKENS
    tf = next(t for t in DENSE_FFN_TILES if f_dim % t == 0)
    assert m % tm == 0
    row = lambda i, f: (i, 0)
    full = lambda i, f: (0, 0)
    return pl.pallas_call(
        _ffn_body,
        grid=(m // tm, f_dim // tf),
        in_specs=[pl.BlockSpec((tm, d), row),
                  pl.BlockSpec((d, tf), lambda i, f: (0, f)), pl.BlockSpec((d, tf), lambda i, f: (0, f)),
                  pl.BlockSpec((tf, d), lambda i, f: (f, 0)),
                  pl.BlockSpec((1, d), full), pl.BlockSpec((1, d), full)],
        out_specs=pl.BlockSpec((tm, d), row),
        out_shape=jax.ShapeDtypeStruct((m, d), jnp.float32),
        scratch_shapes=[pltpu.VMEM((tm, d), jnp.float32)],
        compiler_params=pltpu.CompilerParams(
            dimension_semantics=("parallel", "arbitrary"), vmem_limit_bytes=VMEM_LIMIT_BYTES),
        name="swiglu_norm",
    )(h, wg, wu, wd, ln_g.reshape(1, d), ln_b.reshape(1, d))


S5_CHUNK = 8
S5_CHUNK_WIDTH = S5_CHUNK * SSM_WIDTH
S5_STATE_WIDTH = SSM_GROUPS * SSM_STATE


def s5_operators(log_dt, lam_re, lam_im, b_re, b_im, c_re, c_im):
    f32 = jnp.float32
    n = S5_CHUNK
    lre, lim = lam_re.astype(f32), lam_im.astype(f32)
    dt = jnp.exp(log_dt.astype(f32))[:, None]
    mag = jnp.exp(lre * dt)
    ang = lim * dt
    a_re, a_im = mag * jnp.cos(ang), mag * jnp.sin(ang)
    den = lre * lre + lim * lim
    coef_re = ((a_re - 1.0) * lre + a_im * lim) / den
    coef_im = (a_im * lre - (a_re - 1.0) * lim) / den
    br, bi = b_re.astype(f32), b_im.astype(f32)
    bb_re = coef_re[..., None] * br - coef_im[..., None] * bi
    bb_im = coef_re[..., None] * bi + coef_im[..., None] * br
    j = jnp.arange(n + 1, dtype=f32)[:, None, None]
    pw_re = jnp.exp(j * lre * dt) * jnp.cos(j * ang)
    pw_im = jnp.exp(j * lre * dt) * jnp.sin(j * ang)
    eye = jnp.eye(SSM_GROUPS, dtype=f32)
    cr, ci = c_re.astype(f32), c_im.astype(f32)
    ab_re = pw_re[..., None] * bb_re - pw_im[..., None] * bb_im
    ab_im = pw_re[..., None] * bb_im + pw_im[..., None] * bb_re
    kern = (jnp.einsum('gcp,jgpd->jgcd', cr, ab_re[:n]) - jnp.einsum('gcp,jgpd->jgcd', ci, ab_im[:n]))
    lag = jnp.arange(n)[None, :] - jnp.arange(n)[:, None]
    toep = jnp.where((lag >= 0)[:, :, None, None, None], kern[jnp.maximum(lag, 0)], 0.0)
    t_mat = jnp.einsum('abgcd,gh->agdbhc', toep, eye).reshape(S5_CHUNK_WIDTH, S5_CHUNK_WIDTH)
    v = jnp.stack([ab_re[:n][::-1], ab_im[:n][::-1]])
    w_in = jnp.einsum('rlgpc,gh->lhcrgp', v, eye).reshape(S5_CHUNK_WIDTH, 2 * S5_STATE_WIDTH)
    ar, ai = pw_re[1:], pw_im[1:]
    wo_re = jnp.einsum('gcp,lgp->gplc', cr, ar) - jnp.einsum('gcp,lgp->gplc', ci, ai)
    wo_im = -jnp.einsum('gcp,lgp->gplc', cr, ai) - jnp.einsum('gcp,lgp->gplc', ci, ar)
    w_out = jnp.einsum('rgplc,gh->rhplgc', jnp.stack([wo_re, wo_im]), eye)
    w_out = w_out.reshape(2 * S5_STATE_WIDTH, S5_CHUNK_WIDTH)
    decay = jnp.stack([pw_re[n].reshape(1, S5_STATE_WIDTH), pw_im[n].reshape(1, S5_STATE_WIDTH)])
    return (jnp.concatenate([t_mat, w_in], axis=1).astype(MXU_DTYPE), w_out.astype(MXU_DTYPE), decay)


def _s5_scan_body(u_ref, tw_ref, wout_ref, decay_ref, y_ref, h_ref, s_ref, hs_ref, *, tm):
    f32 = jnp.float32
    sw = S5_STATE_WIDTH

    @pl.when(pl.program_id(1) == 0)
    def _():
        h_ref[...] = jnp.zeros_like(h_ref)

    r = jnp.dot(u_ref[0].astype(MXU_DTYPE), tw_ref[...], preferred_element_type=f32)
    s_ref[...] = r[:, S5_CHUNK_WIDTH:]
    d_re = decay_ref[0]
    d_im = decay_ref[1]

    def eight_chunks(k, carry):
        h_re, h_im = carry
        r0 = pl.multiple_of(k * 8, 8)
        inc = s_ref[pl.ds(r0, 8), :]
        rows_re, rows_im = [], []
        for t in range(8):
            rows_re.append(h_re)
            rows_im.append(h_im)
            h_re, h_im = (d_re * h_re - d_im * h_im + inc[t:t + 1, :sw],
                          d_re * h_im + d_im * h_re + inc[t:t + 1, sw:])
        hs_ref[pl.ds(r0, 8), :] = jnp.concatenate(
            [jnp.concatenate(rows_re, axis=0), jnp.concatenate(rows_im, axis=0)], axis=1)
        return h_re, h_im

    h_re, h_im = lax.fori_loop(0, tm // 8, eight_chunks, (h_ref[0:1, :], h_ref[1:2, :]))
    h_ref[0:1, :] = h_re
    h_ref[1:2, :] = h_im
    y_ref[0] = r[:, :S5_CHUNK_WIDTH] + jnp.dot(hs_ref[...].astype(MXU_DTYPE), wout_ref[...],
                                                preferred_element_type=f32)


def s5_scan(u, tw, w_out, decay, *, tm=256):
    bsz, seq, _ = u.shape
    n_chunks = seq // S5_CHUNK
    tm = min(tm, n_chunks)
    assert seq % S5_CHUNK == 0 and n_chunks % tm == 0 and tm % 8 == 0
    once = pl.Buffered(1)
    y = pl.pallas_call(
        functools.partial(_s5_scan_body, tm=tm),
        grid=(bsz, n_chunks // tm),
        in_specs=[pl.BlockSpec((1, tm, S5_CHUNK_WIDTH), lambda b, i: (b, i, 0)),
                  pl.BlockSpec(tw.shape, lambda b, i: (0, 0), pipeline_mode=once),
                  pl.BlockSpec(w_out.shape, lambda b, i: (0, 0), pipeline_mode=once),
                  pl.BlockSpec(decay.shape, lambda b, i: (0, 0, 0), pipeline_mode=once)],
        out_specs=pl.BlockSpec((1, tm, S5_CHUNK_WIDTH), lambda b, i: (b, i, 0)),
        out_shape=jax.ShapeDtypeStruct((bsz, n_chunks, S5_CHUNK_WIDTH), jnp.float32),
        scratch_shapes=[pltpu.VMEM((2, S5_STATE_WIDTH), jnp.float32),
                        pltpu.VMEM((tm, 2 * S5_STATE_WIDTH), jnp.float32),
                        pltpu.VMEM((tm, 2 * S5_STATE_WIDTH), jnp.float32)],
        compiler_params=pltpu.CompilerParams(
            dimension_semantics=("parallel", "arbitrary"), vmem_limit_bytes=VMEM_LIMIT_BYTES),
        name="s5_scan",
    )(u.reshape(bsz, n_chunks, S5_CHUNK_WIDTH), tw, w_out, decay)
    return y.reshape(bsz, seq, SSM_WIDTH)


def _s5_out_body(y_ref, u_ref, d_ref, wglu_ref, bglu_ref, wout_ref, o_ref):
    f32 = jnp.float32
    y = jax.nn.gelu(y_ref[...] + d_ref[...] * u_ref[...])
    gate = jnp.dot(y.astype(MXU_DTYPE), wglu_ref[...], preferred_element_type=f32) + bglu_ref[...]
    y = y * jax.nn.sigmoid(gate)
    o_ref[...] = jnp.dot(y.astype(MXU_DTYPE), wout_ref[...], preferred_element_type=f32)


def s5_output(y, u, d_skip, w_glu, b_glu, w_out, *, tm=1024):
    m, c = y.shape
    tm = min(tm, m)
    row = lambda i: (i, 0)
    full = lambda i: (0, 0)
    return pl.pallas_call(
        _s5_out_body,
        grid=(m // tm,),
        in_specs=[pl.BlockSpec((tm, c), row), pl.BlockSpec((tm, c), row), pl.BlockSpec((1, c), full),
                  pl.BlockSpec((c, c), full), pl.BlockSpec((1, c), full), pl.BlockSpec((c, D_MODEL), full)],
        out_specs=pl.BlockSpec((tm, D_MODEL), row),
        out_shape=jax.ShapeDtypeStruct((m, D_MODEL), jnp.float32),
        compiler_params=pltpu.CompilerParams(
            dimension_semantics=("parallel",), vmem_limit_bytes=VMEM_LIMIT_BYTES),
        name="s5_output",
    )(y, u, d_skip.reshape(1, c), w_glu.astype(MXU_DTYPE), b_glu.reshape(1, c), w_out.astype(MXU_DTYPE))


def s5_branch(u, log_dt, lam_re, lam_im, b_re, b_im, c_re, c_im, d_skip, w_glu, b_glu, w_out):
    bsz, seq, _ = u.shape
    tw, w_state_out, decay = s5_operators(log_dt, lam_re, lam_im, b_re, b_im, c_re, c_im)
    y = s5_scan(u, tw, w_state_out, decay)
    out = s5_output(y.reshape(bsz * seq, SSM_WIDTH), u.reshape(bsz * seq, SSM_WIDTH),
                    d_skip, w_glu, b_glu, w_out)
    return out.reshape(bsz, seq, D_MODEL)


HG_CHUNK = 128
HG_LEVELS = 7
HG_KDIM = HGRN_HEADS * HGRN_DK


def _hgrn_segment_sums():
    c = HG_CHUNK
    t = np.arange(c)[:, None]
    u = np.arange(c)[None, :]
    blocks = []
    for lvl in range(1, HG_LEVELS + 1):
        m = (t >> lvl << lvl) + (1 << (lvl - 1)) - 1
        right = ((t >> (lvl - 1)) & 1) == 1
        blocks.append(np.where(right, (u > m) & (u <= t), (u > t) & (u <= m)))
    blocks.append(u <= t)
    blocks.append(u > t)
    return np.concatenate(blocks, axis=0).astype(np.float32)


def _hgrn_body(q_ref, z_ref, v_ref, g_ref, seg_ref, lb_ref, ng_ref, hmean_ref, wout_ref, o_ref, st_ref):
    f32 = jnp.float32
    c = HG_CHUNK

    @pl.when(pl.program_id(1) == 0)
    def _():
        st_ref[...] = jnp.zeros_like(st_ref)

    q = q_ref[0]
    z = z_ref[0]
    v = v_ref[0]
    lb = lb_ref[...]
    f = lb + (1.0 - lb) * jax.nn.sigmoid(z)
    logf = jnp.log(jnp.maximum(f, F_MIN))
    kin = (1.0 - lb) * jax.nn.sigmoid(-z)

    p1 = logf.astype(MXU_DTYPE)
    r1 = logf - p1.astype(f32)
    p2 = r1.astype(MXU_DTYPE)
    p3 = (r1 - p2.astype(f32)).astype(MXU_DTYPE)
    seg = seg_ref[...]
    sums = (jnp.dot(seg, p1, preferred_element_type=f32) + jnp.dot(seg, p2, preferred_element_type=f32)
            + jnp.dot(seg, p3, preferred_element_type=f32))

    lane_head = lax.broadcasted_iota(jnp.int32, (c, HG_KDIM), 1) // HGRN_DK
    tok = lax.broadcasted_iota(jnp.int32, (c, HG_KDIM), 0)
    row_t = lax.broadcasted_iota(jnp.int32, (HGRN_HEADS * c, c), 0) % c
    col_s = lax.broadcasted_iota(jnp.int32, (HGRN_HEADS * c, c), 1)

    def per_head_rows(x):
        return jnp.concatenate([jnp.where(lane_head == h, x, 0.0) for h in range(HGRN_HEADS)],
                               axis=0).astype(MXU_DTYPE)

    def scores(ql, kl):
        return lax.dot_general(per_head_rows(ql), kl.astype(MXU_DTYPE), (((1,), (1,)), ((), ())),
                               preferred_element_type=f32)

    att = jnp.where(row_t == col_s, scores(q, kin), 0.0)
    for lvl in range(1, HG_LEVELS + 1):
        decay = jnp.exp(sums[(lvl - 1) * c:lvl * c])
        right = ((tok >> (lvl - 1)) & 1) == 1
        a = scores(jnp.where(right, q * decay, 0.0), jnp.where(right, 0.0, kin * decay))
        att = att + jnp.where((row_t >> lvl) == (col_s >> lvl), a, 0.0)

    b = sums[HG_LEVELS * c:(HG_LEVELS + 1) * c]
    tail = sums[(HG_LEVELS + 1) * c:(HG_LEVELS + 2) * c]
    v_m = v.astype(MXU_DTYPE)
    st = st_ref[...]
    o = lax.dot_general((q * jnp.exp(b)).astype(MXU_DTYPE), st.astype(MXU_DTYPE),
                        (((1,), (1,)), ((), ())), preferred_element_type=f32)
    for h in range(HGRN_HEADS):
        o_h = jnp.dot(att[h * c:(h + 1) * c].astype(MXU_DTYPE), v_m, preferred_element_type=f32)
        o = o + jnp.where(lane_head == h, o_h, 0.0)

    kv = jnp.dot(v.T.astype(MXU_DTYPE), (kin * jnp.exp(tail)).astype(MXU_DTYPE), preferred_element_type=f32)
    sr = lax.broadcasted_iota(jnp.int32, st.shape, 0) // HGRN_DV
    sc = lax.broadcasted_iota(jnp.int32, st.shape, 1) // HGRN_DK
    st_ref[...] = st * jnp.exp(b[c - 1:c, :]) + jnp.where(sr == sc, kv, 0.0)

    o2 = o * o
    o2_hi = o2.astype(MXU_DTYPE)
    o2_lo = (o2 - o2_hi.astype(f32)).astype(MXU_DTYPE)
    ms = (jnp.dot(o2_hi, hmean_ref[...], preferred_element_type=f32)
          + jnp.dot(o2_lo, hmean_ref[...], preferred_element_type=f32))
    g = g_ref[0]
    out = o * lax.rsqrt(ms + RMS_EPS) * ng_ref[...] * (g * jax.nn.sigmoid(g))
    o_ref[0] = jnp.dot(out.astype(MXU_DTYPE), wout_ref[...], preferred_element_type=f32)


def hgrn2_branch(hg_in, lower_bound, norm_g, w_out):
    bsz, seq, _ = hg_in.shape
    assert seq % HG_CHUNK == 0 and HGRN_DK == HGRN_DV and HG_IN_WIDTH == 4 * HG_KDIM
    seg = jnp.asarray(_hgrn_segment_sums(), MXU_DTYPE)
    head_mean = jnp.asarray(np.kron(np.eye(HGRN_HEADS), np.full((HGRN_DV, HGRN_DV), 1.0 / HGRN_DV)), MXU_DTYPE)
    tok = lambda b, i: (b, i, 0)
    full = lambda b, i: (0, 0)
    part = lambda k: pl.BlockSpec((1, HG_CHUNK, HG_KDIM), lambda b, i: (b, i, k))
    return pl.pallas_call(
        _hgrn_body,
        grid=(bsz, seq // HG_CHUNK),
        in_specs=[part(0), part(1), part(2), part(3),
                  pl.BlockSpec(seg.shape, full), pl.BlockSpec((1, HG_KDIM), full),
                  pl.BlockSpec((1, HGRN_WIDTH), full), pl.BlockSpec(head_mean.shape, full),
                  pl.BlockSpec((HGRN_WIDTH, D_MODEL), full)],
        out_specs=pl.BlockSpec((1, HG_CHUNK, D_MODEL), tok),
        out_shape=jax.ShapeDtypeStruct((bsz, seq, D_MODEL), jnp.float32),
        scratch_shapes=[pltpu.VMEM((HGRN_WIDTH, HG_KDIM), jnp.float32)],
        compiler_params=pltpu.CompilerParams(
            dimension_semantics=("parallel", "arbitrary"), vmem_limit_bytes=VMEM_LIMIT_BYTES),
        name="hgrn2",
    )(hg_in, hg_in, hg_in, hg_in, seg, lower_bound.reshape(1, HG_KDIM).astype(jnp.float32),
      jnp.tile(norm_g.astype(jnp.float32), HGRN_HEADS).reshape(1, HGRN_WIDTH), head_mean,
      w_out.astype(MXU_DTYPE))


ATT_QK_DIM = LANE
DSA_TILE = 512


def _rms_rows(x, g):
    return x * lax.rsqrt(jnp.mean(x * x, axis=-1, keepdims=True) + RMS_EPS) * g


def _dsa_project_body(x_ref, gq_ref, gkv_ref, wq_ref, wqi_ref, wkt_ref, wv_ref,
                      q_ref, qi_ref, kt_ref, v_ref, kidt_ref, qn_ref, kn_ref):
    f32 = jnp.float32
    tm = x_ref.shape[1]
    x = x_ref[0]
    cq = _rms_rows(x[:, :ATT_Q_RANK], gq_ref[...]).astype(MXU_DTYPE)
    ckv = _rms_rows(x[:, ATT_Q_RANK:ATT_Q_RANK + ATT_KV_RANK], gkv_ref[...])
    q_all = jnp.dot(cq, wq_ref[...], preferred_element_type=f32)
    qi_all = jnp.dot(cq, wqi_ref[...], preferred_element_type=f32)
    v_all = jnp.dot(ckv.astype(MXU_DTYPE), wv_ref[...], preferred_element_type=f32)
    kt_all = jnp.dot(wkt_ref[...], ckv.T.astype(MXU_DTYPE), preferred_element_type=f32)

    lane = lax.broadcasted_iota(jnp.int32, (tm, LANE), 1)
    t_loc = lax.broadcasted_iota(jnp.int32, (tm, LANE), 0)
    t_even = (t_loc // 2 * 2).astype(f32)
    t_odd = (t_loc % 2).astype(f32)
    sub = lax.broadcasted_iota(jnp.int32, (LANE, tm), 0)
    s_loc = lax.broadcasted_iota(jnp.int32, (LANE, tm), 1)
    k_rows = jnp.where(sub < ATT_HEAD_DIM + 2, 1.0,
                       jnp.where(sub == ATT_HEAD_DIM + 2, (s_loc // 2 * 2).astype(f32), (s_loc % 2).astype(f32)))
    for h in range(ATT_HEADS):
        slope = 2.0 ** (-8.0 * (h + 1) / ATT_HEADS)
        q_cols = jnp.where(lane == ATT_HEAD_DIM, -slope * t_even,
                           jnp.where(lane == ATT_HEAD_DIM + 1, -slope * t_odd, slope))
        q_h = q_all[:, h * LANE:(h + 1) * LANE].astype(q_ref.dtype)
        q_ref[0, h] = jnp.where(lane < ATT_HEAD_DIM, q_h,
                                jnp.where(lane < ATT_HEAD_DIM + 4, q_cols, 0.0).astype(q_ref.dtype))
        k_h = kt_all[h * LANE:(h + 1) * LANE, :].astype(kt_ref.dtype)
        kt_ref[0, h] = jnp.where(sub < ATT_HEAD_DIM, k_h,
                                 jnp.where(sub < ATT_HEAD_DIM + 4, k_rows, 0.0).astype(kt_ref.dtype))
        q_sq = jnp.sum(jnp.square(q_h.astype(f32)), axis=1, keepdims=True)
        k_sq = jnp.sum(jnp.square(k_h.astype(f32)), axis=0, keepdims=True)
        qn_ref[0, 0, h:h + 1, :] = jnp.broadcast_to(jnp.sqrt(jnp.max(q_sq, axis=0, keepdims=True)), (1, LANE))
        kn_ref[0, 0, h:h + 1, :] = jnp.broadcast_to(jnp.sqrt(jnp.max(k_sq, axis=1, keepdims=True)), (1, LANE))
        v_h = v_all[:, h * LANE:(h + 1) * LANE]
        v_ref[0, h] = jnp.where(lane == ATT_HEAD_DIM, 1.0, v_h).astype(v_ref.dtype)
    for h in range(IDX_HEADS):
        qi_ref[0, h] = qi_all[:, h * LANE:(h + 1) * LANE].astype(qi_ref.dtype)
    kidt_ref[0] = x[:, ATT_IDX_BLOCK * LANE:(ATT_IDX_BLOCK + 1) * LANE].T.astype(kidt_ref.dtype)


def _head_padded(w, n_heads, dim):
    k = w.shape[0]
    return jnp.pad(w.reshape(k, n_heads, dim), ((0, 0), (0, 0), (0, LANE - dim))).reshape(k, n_heads * LANE)


def dsa_project(att_in, q_norm_g, kv_norm_g, w_uq, w_qidx, w_ukv):
    bsz, seq, _ = att_in.shape
    tm = DSA_TILE
    assert seq % tm == 0 and 8 % ATT_HEADS == 0 and tm <= 512
    w_q = _head_padded(w_uq * ATT_HEAD_DIM ** -0.5, ATT_HEADS, ATT_HEAD_DIM).astype(MXU_DTYPE)
    w_qi = _head_padded(w_qidx * IDX_DIM ** -0.5, IDX_HEADS, IDX_DIM).astype(MXU_DTYPE)
    w_kt = _head_padded(w_ukv[:, :ATT_WIDTH], ATT_HEADS, ATT_HEAD_DIM).T.astype(MXU_DTYPE)
    w_v = _head_padded(w_ukv[:, ATT_WIDTH:], ATT_HEADS, ATT_HEAD_DIM).astype(MXU_DTYPE)
    tok = lambda b, i: (b, 0, i, 0)
    full = lambda b, i: (0, 0)
    dt = MXU_DTYPE
    norms = pl.BlockSpec((1, 1, ATT_HEADS, LANE), lambda b, i: (b, i, 0, 0))
    norms_shape = jax.ShapeDtypeStruct((bsz, seq // tm, ATT_HEADS, LANE), jnp.float32)
    *operands, q_norm, k_norm = pl.pallas_call(
        _dsa_project_body,
        grid=(bsz, seq // tm),
        in_specs=[pl.BlockSpec((1, tm, ATT_IN_WIDTH), lambda b, i: (b, i, 0)),
                  pl.BlockSpec((1, ATT_Q_RANK), full), pl.BlockSpec((1, ATT_KV_RANK), full),
                  pl.BlockSpec(w_q.shape, full), pl.BlockSpec(w_qi.shape, full),
                  pl.BlockSpec(w_kt.shape, full), pl.BlockSpec(w_v.shape, full)],
        out_specs=[pl.BlockSpec((1, ATT_HEADS, tm, LANE), tok),
                   pl.BlockSpec((1, IDX_HEADS, tm, LANE), tok),
                   pl.BlockSpec((1, ATT_HEADS, LANE, tm), lambda b, i: (b, 0, 0, i)),
                   pl.BlockSpec((1, ATT_HEADS, tm, LANE), tok),
                   pl.BlockSpec((1, LANE, tm), lambda b, i: (b, 0, i)), norms, norms],
        out_shape=[jax.ShapeDtypeStruct((bsz, ATT_HEADS, seq, LANE), dt),
                   jax.ShapeDtypeStruct((bsz, IDX_HEADS, seq, LANE), dt),
                   jax.ShapeDtypeStruct((bsz, ATT_HEADS, LANE, seq), dt),
                   jax.ShapeDtypeStruct((bsz, ATT_HEADS, seq, LANE), dt),
                   jax.ShapeDtypeStruct((bsz, LANE, seq), dt), norms_shape, norms_shape],
        compiler_params=pltpu.CompilerParams(
            dimension_semantics=("parallel", "parallel"), vmem_limit_bytes=VMEM_LIMIT_BYTES),
        name="dsa_project",
    )(att_in, q_norm_g.reshape(1, ATT_Q_RANK), kv_norm_g.reshape(1, ATT_KV_RANK), w_q, w_qi, w_kt, w_v)
    return (*operands, q_norm[..., 0].reshape(-1), k_norm[..., 0].reshape(-1))


_INT_MIN = -2 ** 31
_MASK_KEY = int(np.float32(MASK_VALUE).view(np.int32)) ^ 0x7FFFFFFF
_SEARCH_ROWS = 128
_ATT_ROWS = 32
_SEARCH_LANES = 512
_LIST_DEPTH = 12
_LIST_ROWS = 16
_LIST_MIN_BLOCKS = 4
_EXP_ZERO_GAP = 105.0
_DARK_HEAD_GROUPS = (3, 5)


def _key_to_score(key):
    return lax.bitcast_convert_type(jnp.where(key < 0, key ^ 0x7FFFFFFF, key), jnp.float32)


def _dsa_body(qi_ref, kj_ref, qn_ref, kn_ref, qidx_ref, w_ref, kidt_ref, q_ref, kt_ref, v_ref, tri_ref,
              wout_ref, o_ref, keys_ref, thr_ref, excess_ref, carry_ref, m_ref, acc_ref, bias_ref, s_ref,
              p_ref, alpha_ref, cand_ref, mmin_ref, *, tq, tk, seq, n_sel):
    f32 = jnp.float32
    nq = seq // tq
    p_id = pl.program_id(1)
    i = qi_ref[p_id]
    j = kj_ref[p_id]
    row_local = lax.broadcasted_iota(jnp.int32, (tq, tk), 0)
    col_local = lax.broadcasted_iota(jnp.int32, (tq, tk), 1)

    @pl.when(j == i)
    def _select():
        def score_block(jj, carry):
            off = pl.multiple_of(jj * tk, tk)
            kb = kidt_ref[0, :, pl.ds(off, tk)]
            sc = jnp.zeros((tq, tk), f32)
            for h in range(IDX_HEADS):
                s = jnp.dot(qidx_ref[0, h], kb, preferred_element_type=f32)
                w_h = w_ref[0, :, IDX_DIM + h:IDX_DIM + h + 1] * IDX_HEADS ** -0.5
                sc = sc + jnp.maximum(s, 0.0) * w_h
            causal = (col_local + jj * tk) <= (row_local + i * tq)
            sc = jnp.where(causal, sc, MASK_VALUE)
            keys_ref[:, pl.ds(off, tk)] = jnp.where(sc == 0.0, 0.0, sc)
            return carry

        lax.fori_loop(0, i + 1, score_block, 0)

        n_blocks = i + 1
        n_masked_tail = (seq - n_blocks * tk).astype(f32)

        def row_group(r, carry):
            r0 = pl.multiple_of(r * _SEARCH_ROWS, _SEARCH_ROWS)

            def count_ge(ref, n_iter, cand):
                cand_b = jnp.broadcast_to(_key_to_score(cand), (_SEARCH_ROWS, LANE))

                def chunk(c, acc):
                    base = pl.multiple_of(c * _SEARCH_LANES, _SEARCH_LANES)
                    for u in range(_SEARCH_LANES // LANE):
                        kk = ref[pl.ds(r0, _SEARCH_ROWS), pl.ds(base + u * LANE, LANE)]
                        acc = acc + jnp.where(kk >= cand_b, 1.0, 0.0)
                    return acc

                acc = lax.fori_loop(0, n_iter, chunk, jnp.zeros((_SEARCH_ROWS, LANE), f32))
                cnt = jnp.sum(acc, axis=1, keepdims=True)
                return cnt + jnp.where(cand <= _MASK_KEY, n_masked_tail, 0.0)

            def kth_largest_key(count):
                zero = jnp.zeros((_SEARCH_ROWS, 1), jnp.int32)
                v0 = jnp.where(count(zero) >= n_sel, zero, zero + _INT_MIN)

                def bit_step(b, v):
                    cand = v | jnp.left_shift(jnp.int32(1), 30 - b)
                    return jnp.where(count(cand) >= n_sel, cand, v)

                return lax.fori_loop(0, 31, bit_step, v0)

            count_all = functools.partial(count_ge, keys_ref, n_blocks * (tk // _SEARCH_LANES))

            def store(v, n_at_least):
                thr_ref[pl.ds(r0, _SEARCH_ROWS), :] = _key_to_score(v)
                stored = n_at_least - jnp.where(v <= _MASK_KEY, n_masked_tail, 0.0)
                excess_ref[pl.ds(r0, _SEARCH_ROWS), :] = stored - n_sel

            def search_all_keys():
                v = kth_largest_key(count_all)
                store(v, count_all(v))

            @pl.when(n_blocks < _LIST_MIN_BLOCKS)
            def _():
                search_all_keys()

            @pl.when(n_blocks >= _LIST_MIN_BLOCKS)
            def _():
                def shortlist(sub, carry):
                    rr = pl.multiple_of(r0 + sub * _LIST_ROWS, _LIST_ROWS)

                    def insert_block(c, tops):
                        base = pl.multiple_of(c * tk, tk)
                        for u in range(tk // LANE):
                            x = keys_ref[pl.ds(rr, _LIST_ROWS), pl.ds(base + u * LANE, LANE)]
                            new = []
                            for t in range(_LIST_DEPTH):
                                new.append(jnp.maximum(tops[t], x))
                                x = jnp.minimum(tops[t], x)
                            tops = tuple(new)
                        return tops

                    tops = lax.fori_loop(0, n_blocks, insert_block,
                                         tuple(jnp.full((_LIST_ROWS, LANE), -jnp.inf, f32)
                                               for _ in range(_LIST_DEPTH)))
                    for t in range(_LIST_DEPTH):
                        cand_ref[pl.ds(rr, _LIST_ROWS), t * LANE:(t + 1) * LANE] = tops[t]
                    return carry

                lax.fori_loop(0, _SEARCH_ROWS // _LIST_ROWS, shortlist, 0)
                v = kth_largest_key(functools.partial(count_ge, cand_ref,
                                                      _LIST_DEPTH * LANE // _SEARCH_LANES))
                n_at_least = count_all(v)
                n_above = count_all(v + 1)
                store(v, n_at_least)
                exact = (n_above < n_sel) & (n_at_least >= n_sel)
                pl.when(jnp.min(jnp.where(exact, 1.0, 0.0)) < 0.5)(search_all_keys)

            return carry

        lax.fori_loop(0, tq // _SEARCH_ROWS, row_group, 0)
        carry_ref[...] = jnp.zeros_like(carry_ref)
        m_ref[...] = jnp.full_like(m_ref, MASK_VALUE)
        acc_ref[...] = jnp.zeros_like(acc_ref)
        for h in range(ATT_HEADS):
            mmin_ref[h] = jnp.float32(MASK_VALUE)

    keys_blk = keys_ref[:, pl.ds(pl.multiple_of(j * tk, tk), tk)]
    thr = thr_ref[...]
    eq = keys_blk == thr
    ties_after = carry_ref[...] + jnp.dot(jnp.where(eq, 1.0, 0.0).astype(MXU_DTYPE), tri_ref[...],
                                           preferred_element_type=f32)
    carry_ref[...] = ties_after[:, 0:1] + jnp.where(eq[:, 0:1], 1.0, 0.0)
    sel = (keys_blk > thr) | (eq & (ties_after >= excess_ref[...]))
    causal = (row_local + i * tq) >= (col_local + j * tk)
    bias_ref[...] = jnp.where(sel & causal, 0.0, MASK_VALUE)

    def slope_tk(h):
        return jnp.right_shift(tk, h + 1).astype(f32)

    def logits(h, slot):
        s_ref[slot] = jnp.dot(q_ref[0, h], kt_ref[0, h], preferred_element_type=f32)

    def softmax(h, slot):
        for r in range(0, tq, _ATT_ROWS):
            rows = pl.ds(r, _ATT_ROWS)
            s = s_ref[slot, rows, :] + bias_ref[rows, :]
            m_old = m_ref[h, rows, :] + slope_tk(h)
            m_new = jnp.maximum(m_old, jnp.max(s, axis=1, keepdims=True))
            p_ref[slot, rows, :] = jnp.exp(s - m_new).astype(MXU_DTYPE)
            alpha_ref[slot, rows, :] = jnp.exp(m_old - m_new)
            m_ref[h, rows, :] = m_new

    def weighted_values(h, slot):
        acc_ref[h] = alpha_ref[slot] * acc_ref[h] + jnp.dot(p_ref[slot], v_ref[0, h],
                                                             preferred_element_type=f32)

    def attend(first_head):
        def run():
            for h in range(first_head):
                m_ref[h] = m_ref[h] + slope_tk(h)
            heads = range(first_head, ATT_HEADS)
            for t in range(len(heads) + 2):
                if t < len(heads):
                    logits(heads[t], t % 2)
                if 1 <= t <= len(heads):
                    softmax(heads[t - 1], (t - 1) % 2)
                if t >= 2:
                    weighted_values(heads[t - 2], t % 2)
        return run

    gap = i - j
    b = pl.program_id(0)

    def head_is_dark(h):
        bound = qn_ref[(b * nq + i) * ATT_HEADS + h] * kn_ref[(b * nq + j) * ATT_HEADS + h] + 1.0
        bias = slope_tk(h) * (gap - 1).astype(f32) + 2.0 ** -(h + 1)
        return bound - bias - mmin_ref[h] <= -_EXP_ZERO_GAP

    dark = gap >= 2
    dark_upto = []
    for h in range(_DARK_HEAD_GROUPS[-1]):
        dark = dark & head_is_dark(h)
        dark_upto.append(dark)
    n_dark_groups = sum(dark_upto[g - 1].astype(jnp.int32) for g in _DARK_HEAD_GROUPS)
    for k, first_head in enumerate((0,) + _DARK_HEAD_GROUPS):
        pl.when(n_dark_groups == k)(attend(first_head))

    @pl.when(gap == 1)
    def _():
        for h in range(ATT_HEADS):
            mmin_ref[h] = jnp.min(m_ref[h]) - slope_tk(h)

    @pl.when(j == 0)
    def _finish():
        heads = []
        for h in range(ATT_HEADS):
            a = acc_ref[h]
            heads.append((a[:, :ATT_HEAD_DIM] / a[:, ATT_HEAD_DIM:ATT_HEAD_DIM + 1]).astype(MXU_DTYPE))
        o_ref[0] = jnp.dot(jnp.concatenate(heads, axis=1), wout_ref[...], preferred_element_type=f32)


def dsa_attention(att_in, q_idx, kid_t, q, k_t, v_aug, q_norm, k_norm, w_out):
    bsz, _, seq, _ = q.shape
    tq = tk = DSA_TILE
    n_sel = min(TOPK_MAX, seq // 4)
    nq = seq // tq
    assert seq % tq == 0 and tk % _SEARCH_LANES == 0 and tq % _SEARCH_ROWS == 0
    pairs = [(a, b) for a in range(nq) for b in range(a, -1, -1)]
    qi = jnp.asarray([a for a, _ in pairs], jnp.int32)
    kj = jnp.asarray([b for _, b in pairs], jnp.int32)
    later = jnp.tril(jnp.ones((tk, tk), MXU_DTYPE), k=-1)
    qmap = lambda b, p, qi, kj, qn, kn: (b, 0, qi[p], 0)
    kmap = lambda b, p, qi, kj, qn, kn: (b, 0, kj[p], 0)
    const = lambda b, p, qi, kj, qn, kn: (0, 0)
    grid_spec = pltpu.PrefetchScalarGridSpec(
        num_scalar_prefetch=4,
        grid=(bsz, len(pairs)),
        in_specs=[
            pl.BlockSpec((1, IDX_HEADS, tq, LANE), qmap),
            pl.BlockSpec((1, tq, LANE), lambda b, p, qi, kj, qn, kn: (b, qi[p], ATT_IDX_BLOCK)),
            pl.BlockSpec((1, LANE, seq), lambda b, p, qi, kj, qn, kn: (b, 0, 0)),
            pl.BlockSpec((1, ATT_HEADS, tq, ATT_QK_DIM), qmap),
            pl.BlockSpec((1, ATT_HEADS, ATT_QK_DIM, tk), lambda b, p, qi, kj, qn, kn: (b, 0, 0, kj[p])),
            pl.BlockSpec((1, ATT_HEADS, tk, LANE), kmap),
            pl.BlockSpec((tk, tk), const),
            pl.BlockSpec((ATT_WIDTH, D_MODEL), const),
        ],
        out_specs=pl.BlockSpec((1, tq, D_MODEL), lambda b, p, qi, kj, qn, kn: (b, qi[p], 0)),
        scratch_shapes=[
            pltpu.VMEM((tq, seq), jnp.float32),
            pltpu.VMEM((tq, 1), jnp.float32),
            pltpu.VMEM((tq, 1), jnp.float32),
            pltpu.VMEM((tq, 1), jnp.float32),
            pltpu.VMEM((ATT_HEADS, tq, 1), jnp.float32),
            pltpu.VMEM((ATT_HEADS, tq, LANE), jnp.float32),
            pltpu.VMEM((tq, tk), jnp.float32),
            pltpu.VMEM((2, tq, tk), jnp.float32),
            pltpu.VMEM((2, tq, tk), MXU_DTYPE),
            pltpu.VMEM((2, tq, 1), jnp.float32),
            pltpu.VMEM((tq, _LIST_DEPTH * LANE), jnp.float32),
            pltpu.SMEM((ATT_HEADS,), jnp.float32),
        ])
    return pl.pallas_call(
        functools.partial(_dsa_body, tq=tq, tk=tk, seq=seq, n_sel=n_sel),
        grid_spec=grid_spec,
        out_shape=jax.ShapeDtypeStruct((bsz, seq, D_MODEL), jnp.float32),
        compiler_params=pltpu.CompilerParams(
            dimension_semantics=("parallel", "arbitrary"), vmem_limit_bytes=VMEM_LIMIT_LARGE_BYTES),
        name="dsa_attention",
    )(qi, kj, q_norm, k_norm, q_idx, att_in, kid_t, q, k_t, v_aug, later, w_out)


def dsa_branch(att_in, q_norm_g, kv_norm_g, w_uq, w_qidx, w_ukv, w_out):
    q, q_idx, k_t, v_aug, kid_t, q_norm, k_norm = dsa_project(att_in, q_norm_g, kv_norm_g, w_uq, w_qidx, w_ukv)
    return dsa_attention(att_in, q_idx, kid_t, q, k_t, v_aug, q_norm, k_norm, w_out.astype(MXU_DTYPE))


MOE_TOKENS = 1024
MOE_ROW_CLASSES = (256, 320, 512, 1024)
MOE_FFN_TILES = (896, 512, 256)


def _moe_body(cnt_ref, h_ref, gate_ref, gatet_ref, tri_ref, wg_ref, wu_ref, wd_ref, g_ref, b_ref, o_ref,
              xb_ref, gather_ref, scatter_ref, xc_ref, yc_ref, acc_ref):
    f32 = jnp.float32
    tm = MOE_TOKENS
    i = pl.program_id(0)
    e = pl.program_id(1)
    f = pl.program_id(2)
    last_f = pl.num_programs(2) - 1

    @pl.when((e == 0) & (f == 0))
    def _():
        xb_ref[...] = h_ref[...].astype(MXU_DTYPE)
        acc_ref[...] = jnp.zeros_like(acc_ref)

    lane = lax.broadcasted_iota(jnp.int32, gate_ref.shape, 1)
    gate_col = jnp.sum(jnp.where(lane == e, gate_ref[...], 0.0), axis=-1, keepdims=True)

    def routed(size):
        def run():
            @pl.when(f == 0)
            def _():
                active_row = jnp.where(gatet_ref[pl.ds(e, 1), :] != 0.0, 1.0, 0.0)
                rank_row = jnp.dot(jnp.broadcast_to(active_row, (8, tm)).astype(MXU_DTYPE), tri_ref[...],
                                   preferred_element_type=f32)[0:1]
                slot = lax.broadcasted_iota(jnp.int32, (size, tm), 0).astype(f32)
                gather = jnp.where((slot == rank_row) & (active_row != 0.0), 1.0, 0.0)
                gather_ref[0:size, :] = gather.astype(MXU_DTYPE)
                rank_col = jnp.broadcast_to(rank_row, (LANE, tm)).T[:, 0:1]
                slot_l = lax.broadcasted_iota(jnp.int32, (tm, size), 1).astype(f32)
                scatter = jnp.where((slot_l == rank_col) & (gate_col != 0.0), 1.0, 0.0)
                scatter_ref[:, 0:size] = scatter.astype(MXU_DTYPE)
                xc_ref[0:size, :] = jnp.dot(gather.astype(MXU_DTYPE), xb_ref[...],
                                            preferred_element_type=f32).astype(MXU_DTYPE)
                yc_ref[0:size, :] = jnp.zeros((size, D_MODEL), f32)

            xc = xc_ref[0:size, :]
            a = jnp.dot(xc, wg_ref[0], preferred_element_type=f32)
            u = jnp.dot(xc, wu_ref[0], preferred_element_type=f32)
            act = a * jax.nn.sigmoid(a) * u
            yc_ref[0:size, :] += jnp.dot(act.astype(MXU_DTYPE), wd_ref[0], preferred_element_type=f32)

            @pl.when(f == last_f)
            def _():
                back = jnp.dot(scatter_ref[:, 0:size], yc_ref[0:size, :].astype(MXU_DTYPE),
                               preferred_element_type=f32)
                acc_ref[...] += gate_col * back
        return run

    count = cnt_ref[i * N_EXPERTS + e]
    size_class = sum((count > s).astype(jnp.int32) for s in MOE_ROW_CLASSES[:-1])
    for k, size in enumerate(MOE_ROW_CLASSES):
        pl.when((size_class == k) & (count > 0))(routed(size))

    @pl.when((e == pl.num_programs(1) - 1) & (f == last_f))
    def _():
        o_ref[...] = _layer_norm_rows(DEEPNORM_ALPHA * h_ref[...] + acc_ref[...], g_ref[...], b_ref[...])


def moe_experts_norm(h, routing, wg, wu, wd, ln_g, ln_b):
    gates, gates_t, counts = routing
    m, d = h.shape
    n_experts, _, f_dim = wg.shape
    tm = MOE_TOKENS
    tf = next(t for t in MOE_FFN_TILES if f_dim % t == 0)
    assert m % tm == 0 and MOE_ROW_CLASSES[-1] == tm
    n_tiles = m // tm
    tri = jnp.triu(jnp.ones((tm, tm), MXU_DTYPE), k=1)
    row = lambda i, e, f, c: (i, 0)
    full = lambda i, e, f, c: (0, 0)
    grid_spec = pltpu.PrefetchScalarGridSpec(
        num_scalar_prefetch=1,
        grid=(n_tiles, n_experts, f_dim // tf),
        in_specs=[pl.BlockSpec((tm, d), row), pl.BlockSpec((tm, LANE), row),
                  pl.BlockSpec((n_experts, tm), lambda i, e, f, c: (0, i)),
                  pl.BlockSpec((tm, tm), full),
                  pl.BlockSpec((1, d, tf), lambda i, e, f, c: (e, 0, f)),
                  pl.BlockSpec((1, d, tf), lambda i, e, f, c: (e, 0, f)),
                  pl.BlockSpec((1, tf, d), lambda i, e, f, c: (e, f, 0)),
                  pl.BlockSpec((1, d), full), pl.BlockSpec((1, d), full)],
        out_specs=pl.BlockSpec((tm, d), row),
        scratch_shapes=[pltpu.VMEM((tm, d), MXU_DTYPE),
                        pltpu.VMEM((tm, tm), MXU_DTYPE),
                        pltpu.VMEM((tm, tm), MXU_DTYPE),
                        pltpu.VMEM((tm, d), MXU_DTYPE),
                        pltpu.VMEM((tm, d), jnp.float32),
                        pltpu.VMEM((tm, d), jnp.float32)])
    return pl.pallas_call(
        _moe_body,
        grid_spec=grid_spec,
        out_shape=jax.ShapeDtypeStruct((m, d), jnp.float32),
        compiler_params=pltpu.CompilerParams(
            dimension_semantics=("parallel", "arbitrary", "arbitrary"),
            vmem_limit_bytes=VMEM_LIMIT_LARGE_BYTES),
        name="moe_experts_norm",
    )(counts, h, gates, gates_t, tri, wg, wu, wd, ln_g.reshape(1, d), ln_b.reshape(1, d))


def _router_body(h_ref, r_ref, gate_ref, gatet_ref, cnt_ref):
    f32 = jnp.float32
    logits = jnp.dot(h_ref[...], r_ref[...], preferred_element_type=f32)
    lane = lax.broadcasted_iota(jnp.int32, logits.shape, 1)
    logits = jnp.where(lane < N_EXPERTS, logits, -jnp.inf)
    top1 = jnp.max(logits, axis=1, keepdims=True)
    idx1 = jnp.min(jnp.where(logits == top1, lane, LANE), axis=1, keepdims=True)
    rest = jnp.where(lane == idx1, -jnp.inf, logits)
    top2 = jnp.max(rest, axis=1, keepdims=True)
    idx2 = jnp.min(jnp.where(rest == top2, lane, LANE), axis=1, keepdims=True)
    e2 = jnp.exp(top2 - top1)
    gates = jnp.where(lane == idx1, 1.0 / (1.0 + e2), jnp.where(lane == idx2, e2 / (1.0 + e2), 0.0))
    gate_ref[...] = gates
    gatet_ref[...] = gates.T[0:N_EXPERTS, :]
    cnt_ref[0] = jnp.sum(jnp.where(gates != 0.0, 1.0, 0.0), axis=0, keepdims=True)


def router_gates(h, router):
    m, d = h.shape
    tm = MOE_TOKENS
    assert TOP_K == 2 and m % tm == 0
    r_pad = jnp.pad(router.astype(jnp.float32), ((0, 0), (0, LANE - N_EXPERTS)))
    gates, gates_t, counts = pl.pallas_call(
        _router_body,
        grid=(m // tm,),
        in_specs=[pl.BlockSpec((tm, d), lambda i: (i, 0)), pl.BlockSpec((d, LANE), lambda i: (0, 0))],
        out_specs=[pl.BlockSpec((tm, LANE), lambda i: (i, 0)), pl.BlockSpec((N_EXPERTS, tm), lambda i: (0, i)),
                   pl.BlockSpec((1, 1, LANE), lambda i: (i, 0, 0))],
        out_shape=[jax.ShapeDtypeStruct((m, LANE), jnp.float32),
                   jax.ShapeDtypeStruct((N_EXPERTS, m), jnp.float32),
                   jax.ShapeDtypeStruct((m // tm, 1, LANE), jnp.float32)],
        compiler_params=pltpu.CompilerParams(
            dimension_semantics=("parallel",), vmem_limit_bytes=VMEM_LIMIT_BYTES),
        name="router_gates",
    )(h, r_pad)
    return gates, gates_t, counts[:, 0, :N_EXPERTS].astype(jnp.int32).reshape(-1)


def kernel(x, w_in, ssm_log_dt, ssm_lambda_re, ssm_lambda_im, ssm_b_re, ssm_b_im, ssm_c_re, ssm_c_im,
           ssm_d, ssm_w_glu, ssm_b_glu, ssm_w_out, hgrn_lb_logits, hgrn_norm_g, hgrn_w_out,
           attn_q_norm_g, attn_kv_norm_g, attn_w_uq, attn_w_qidx, attn_w_ukv, attn_w_out, w_o,
           ln_g, ln_b, ffn_w_gate, ffn_w_up, ffn_w_down, moe_router, moe_w_gate, moe_w_up, moe_w_down):
    bsz, seq, d = x.shape
    m = bsz * seq
    bf16 = MXU_DTYPE
    assert MIX_IN_USED + N_BRANCHES * D_MODEL == N_IN
    lb_soft = jax.nn.softmax(hgrn_lb_logits.astype(jnp.float32), axis=0)
    lower_bounds = jnp.concatenate([jnp.zeros_like(lb_soft[:1]), jnp.cumsum(lb_soft[1:], axis=0)], axis=0)
    h = x.reshape(m, d)
    for l in range(DEPTH):
        w_mix = jnp.pad(w_in[l][:, :MIX_IN_USED], ((0, 0), (0, MIX_IN_WIDTH - MIX_IN_USED))).astype(bf16)
        u, hg_in, att_in = in_proj(h, w_mix)
        y_ssm = s5_branch(u.reshape(bsz, seq, SSM_WIDTH), ssm_log_dt[l], ssm_lambda_re[l], ssm_lambda_im[l],
                          ssm_b_re[l], ssm_b_im[l], ssm_c_re[l], ssm_c_im[l], ssm_d[l], ssm_w_glu[l],
                          ssm_b_glu[l], ssm_w_out[l])
        y_hg = hgrn2_branch(hg_in.reshape(bsz, seq, HG_IN_WIDTH), lower_bounds[l], hgrn_norm_g[l],
                            hgrn_w_out[l])
        y_att = dsa_branch(att_in.reshape(bsz, seq, ATT_IN_WIDTH), attn_q_norm_g[l], attn_kv_norm_g[l],
                           attn_w_uq[l], attn_w_qidx[l], attn_w_ukv[l], attn_w_out[l])
        h = merge_project_norm(h, y_ssm.reshape(m, d), y_hg.reshape(m, d), y_att.reshape(m, d),
                               w_in[l][:, MIX_IN_USED:].astype(bf16), w_o[l].astype(bf16),
                               ln_g[l, 0], ln_b[l, 0])
        if l % 2 == 0:
            h = swiglu_norm(h, ffn_w_gate[l // 2].astype(bf16), ffn_w_up[l // 2].astype(bf16),
                            ffn_w_down[l // 2].astype(bf16), ln_g[l, 1], ln_b[l, 1])
        else:
            gate_w = router_gates(h, moe_router[l // 2])
            h = moe_experts_norm(h, gate_w, moe_w_gate[l // 2].astype(bf16),
                                 moe_w_up[l // 2].astype(bf16), moe_w_down[l // 2].astype(bf16),
                                 ln_g[l, 1], ln_b[l, 1])
    return h.reshape(bsz, seq, d)
```

```python
import functools

import jax
import jax.numpy as jnp
import numpy as np
from jax import lax
from jax.experimental import pallas as pl
from jax.experimental.pallas import tpu as pltpu

D_MODEL = 1024
DEPTH = 2
SSM_WIDTH = 256
SSM_GROUP = 16
SSM_GROUPS = SSM_WIDTH // SSM_GROUP
SSM_STATE = 64
HGRN_HEADS = 4
HGRN_DK = 64
HGRN_DV = 64
HGRN_WIDTH = HGRN_HEADS * HGRN_DV
ATT_HEADS = 8
ATT_HEAD_DIM = 64
ATT_WIDTH = ATT_HEADS * ATT_HEAD_DIM
ATT_Q_RANK = 256
ATT_KV_RANK = 128
IDX_HEADS = 4
IDX_DIM = 64
TOPK_MAX = 256
MASK_VALUE = -1e30
N_BRANCHES = 3
N_EXPERTS = 8
TOP_K = 2
DEEPNORM_ALPHA = (2 * DEPTH) ** 0.25
LN_EPS = 1e-5
RMS_EPS = 1e-6
F_MIN = 1e-12

IN_SPLITS = (SSM_WIDTH, HGRN_HEADS * HGRN_DK, HGRN_HEADS * HGRN_DK, HGRN_WIDTH, HGRN_WIDTH,
             ATT_Q_RANK, ATT_KV_RANK, IDX_DIM, IDX_HEADS, N_BRANCHES * D_MODEL)
N_IN = sum(IN_SPLITS)

VMEM_LIMIT_BYTES = 48 * 1024 * 1024
LANE = 128
MXU_DTYPE = jnp.bfloat16


def _round_up(n, m):
    return (n + m - 1) // m * m


HG_IN_WIDTH = 2 * HGRN_HEADS * HGRN_DK + 2 * HGRN_WIDTH
ATT_IN_USED = ATT_Q_RANK + ATT_KV_RANK + IDX_DIM + IDX_HEADS
ATT_IN_WIDTH = _round_up(ATT_IN_USED, LANE)
ATT_IDX_BLOCK = (ATT_Q_RANK + ATT_KV_RANK) // LANE
MIX_IN_USED = SSM_WIDTH + HG_IN_WIDTH + ATT_IN_USED
MIX_IN_WIDTH = SSM_WIDTH + HG_IN_WIDTH + ATT_IN_WIDTH


def _in_proj_body(h_ref, w_ref, u_ref, hg_ref, att_ref):
    r = jnp.dot(h_ref[...].astype(MXU_DTYPE), w_ref[...], preferred_element_type=jnp.float32)
    u_ref[...] = r[:, :SSM_WIDTH]
    hg_ref[...] = r[:, SSM_WIDTH:SSM_WIDTH + HG_IN_WIDTH]
    att_ref[...] = r[:, SSM_WIDTH + HG_IN_WIDTH:]


def in_proj(h, w_mix, *, tm=512):
    m, d = h.shape
    row = lambda i: (i, 0)
    widths = (SSM_WIDTH, HG_IN_WIDTH, ATT_IN_WIDTH)
    return pl.pallas_call(
        _in_proj_body,
        grid=(m // tm,),
        in_specs=[pl.BlockSpec((tm, d), row), pl.BlockSpec((d, MIX_IN_WIDTH), lambda i: (0, 0))],
        out_specs=[pl.BlockSpec((tm, w), row) for w in widths],
        out_shape=[jax.ShapeDtypeStruct((m, w), jnp.float32) for w in widths],
        compiler_params=pltpu.CompilerParams(
            dimension_semantics=("parallel",), vmem_limit_bytes=VMEM_LIMIT_BYTES),
        name="in_proj",
    )(h, w_mix)


def _layer_norm_rows(y, g, b):
    mu = jnp.mean(y, axis=-1, keepdims=True)
    yc = y - mu
    var = jnp.mean(yc * yc, axis=-1, keepdims=True)
    return yc * lax.rsqrt(var + LN_EPS) * g + b


def _merge_body(h_ref, ys_ref, yh_ref, ya_ref, wgate_ref, wo_ref, g_ref, b_ref, o_ref):
    d = D_MODEL
    f32 = jnp.float32
    h = h_ref[...]
    gates = jax.nn.sigmoid(jnp.dot(h.astype(MXU_DTYPE), wgate_ref[...], preferred_element_type=f32))
    mixed = (gates[:, 0:d] * ys_ref[...] + gates[:, d:2 * d] * yh_ref[...]
             + gates[:, 2 * d:3 * d] * ya_ref[...])
    mix_out = jnp.dot(mixed.astype(MXU_DTYPE), wo_ref[...], preferred_element_type=f32)
    o_ref[...] = _layer_norm_rows(DEEPNORM_ALPHA * h + mix_out, g_ref[...], b_ref[...])


def merge_project_norm(h, y_ssm, y_hg, y_att, w_gates, w_o, ln_g, ln_b, *, tm=512):
    m, d = h.shape
    row = lambda i: (i, 0)
    full = lambda i: (0, 0)
    return pl.pallas_call(
        _merge_body,
        grid=(m // tm,),
        in_specs=[pl.BlockSpec((tm, d), row), pl.BlockSpec((tm, d), row), pl.BlockSpec((tm, d), row),
                  pl.BlockSpec((tm, d), row), pl.BlockSpec((d, N_BRANCHES * d), full),
                  pl.BlockSpec((d, d), full), pl.BlockSpec((1, d), full), pl.BlockSpec((1, d), full)],
        out_specs=pl.BlockSpec((tm, d), row),
        out_shape=jax.ShapeDtypeStruct((m, d), jnp.float32),
        compiler_params=pltpu.CompilerParams(
            dimension_semantics=("parallel",), vmem_limit_bytes=VMEM_LIMIT_BYTES),
        name="merge_project_norm",
    )(h, y_ssm, y_hg, y_att, w_gates, w_o, ln_g.reshape(1, d), ln_b.reshape(1, d))


DENSE_FFN_TOKENS = 512
DENSE_FFN_TILES = (1408, 512, 256)


def _ffn_body(h_ref, wg_ref, wu_ref, wd_ref, g_ref, b_ref, o_ref, acc_ref):
    f = pl.program_id(1)

    @pl.when(f == 0)
    def _():
        acc_ref[...] = jnp.zeros_like(acc_ref)

    x = h_ref[...].astype(MXU_DTYPE)
    a = jnp.dot(x, wg_ref[...], preferred_element_type=jnp.float32)
    u = jnp.dot(x, wu_ref[...], preferred_element_type=jnp.float32)
    act = a * jax.nn.sigmoid(a) * u
    acc_ref[...] += jnp.dot(act.astype(MXU_DTYPE), wd_ref[...], preferred_element_type=jnp.float32)

    @pl.when(f == pl.num_programs(1) - 1)
    def _():
        o_ref[...] = _layer_norm_rows(DEEPNORM_ALPHA * h_ref[...] + acc_ref[...], g_ref[...], b_ref[...])


def swiglu_norm(h, wg, wu, wd, ln_g, ln_b):
    m, d = h.shape
    f_dim = wg.shape[1]
    tm = DENSE_FFN_TOKENS
    tf = next(t for t in DENSE_FFN_TILES if f_dim % t == 0)
    assert m % tm == 0
    row = lambda i, f: (i, 0)
    full = lambda i, f: (0, 0)
    return pl.pallas_call(
        _ffn_body,
        grid=(m // tm, f_dim // tf),
        in_specs=[pl.BlockSpec((tm, d), row),
                  pl.BlockSpec((d, tf), lambda i, f: (0, f)), pl.BlockSpec((d, tf), lambda i, f: (0, f)),
                  pl.BlockSpec((tf, d), lambda i, f: (f, 0)),
                  pl.BlockSpec((1, d), full), pl.BlockSpec((1, d), full)],
        out_specs=pl.BlockSpec((tm, d), row),
        out_shape=jax.ShapeDtypeStruct((m, d), jnp.float32),
        scratch_shapes=[pltpu.VMEM((tm, d), jnp.float32)],
        compiler_params=pltpu.CompilerParams(
            dimension_semantics=("parallel", "arbitrary"), vmem_limit_bytes=VMEM_LIMIT_BYTES),
        name="swiglu_norm",
    )(h, wg, wu, wd, ln_g.reshape(1, d), ln_b.reshape(1, d))


S5_CHUNK = 8
S5_CHUNK_WIDTH = S5_CHUNK * SSM_WIDTH
S5_STATE_WIDTH = SSM_GROUPS * SSM_STATE


def s5_operators(log_dt, lam_re, lam_im, b_re, b_im, c_re, c_im):
    f32 = jnp.float32
    n = S5_CHUNK
    lre, lim = lam_re.astype(f32), lam_im.astype(f32)
    dt = jnp.exp(log_dt.astype(f32))[:, None]
    mag = jnp.exp(lre * dt)
    ang = lim * dt
    a_re, a_im = mag * jnp.cos(ang), mag * jnp.sin(ang)
    den = lre * lre + lim * lim
    coef_re = ((a_re - 1.0) * lre + a_im * lim) / den
    coef_im = (a_im * lre - (a_re - 1.0) * lim) / den
    br, bi = b_re.astype(f32), b_im.astype(f32)
    bb_re = coef_re[..., None] * br - coef_im[..., None] * bi
    bb_im = coef_re[..., None] * bi + coef_im[..., None] * br
    j = jnp.arange(n + 1, dtype=f32)[:, None, None]
    pw_re = jnp.exp(j * lre * dt) * jnp.cos(j * ang)
    pw_im = jnp.exp(j * lre * dt) * jnp.sin(j * ang)
    eye = jnp.eye(SSM_GROUPS, dtype=f32)
    cr, ci = c_re.astype(f32), c_im.astype(f32)
    ab_re = pw_re[..., None] * bb_re - pw_im[..., None] * bb_im
    ab_im = pw_re[..., None] * bb_im + pw_im[..., None] * bb_re
    kern = (jnp.einsum('gcp,jgpd->jgcd', cr, ab_re[:n]) - jnp.einsum('gcp,jgpd->jgcd', ci, ab_im[:n]))
    lag = jnp.arange(n)[None, :] - jnp.arange(n)[:, None]
    toep = jnp.where((lag >= 0)[:, :, None, None, None], kern[jnp.maximum(lag, 0)], 0.0)
    t_mat = jnp.einsum('abgcd,gh->agdbhc', toep, eye).reshape(S5_CHUNK_WIDTH, S5_CHUNK_WIDTH)
    v = jnp.stack([ab_re[:n][::-1], ab_im[:n][::-1]])
    w_in = jnp.einsum('rlgpc,gh->lhcrgp', v, eye).reshape(S5_CHUNK_WIDTH, 2 * S5_STATE_WIDTH)
    ar, ai = pw_re[1:], pw_im[1:]
    wo_re = jnp.einsum('gcp,lgp->gplc', cr, ar) - jnp.einsum('gcp,lgp->gplc', ci, ai)
    wo_im = -jnp.einsum('gcp,lgp->gplc', cr, ai) - jnp.einsum('gcp,lgp->gplc', ci, ar)
    w_out = jnp.einsum('rgplc,gh->rhplgc', jnp.stack([wo_re, wo_im]), eye)
    w_out = w_out.reshape(2 * S5_STATE_WIDTH, S5_CHUNK_WIDTH)
    decay = jnp.stack([pw_re[n].reshape(1, S5_STATE_WIDTH), pw_im[n].reshape(1, S5_STATE_WIDTH)])
    return (jnp.concatenate([t_mat, w_in], axis=1).astype(MXU_DTYPE), w_out.astype(MXU_DTYPE), decay)


def _s5_scan_body(u_ref, tw_ref, wout_ref, decay_ref, y_ref, h_ref, s_ref, hs_ref, *, tm):
    f32 = jnp.float32
    sw = S5_STATE_WIDTH

    @pl.when(pl.program_id(1) == 0)
    def _():
        h_ref[...] = jnp.zeros_like(h_ref)

    r = jnp.dot(u_ref[0].astype(MXU_DTYPE), tw_ref[...], preferred_element_type=f32)
    s_ref[...] = r[:, S5_CHUNK_WIDTH:]
    d_re = decay_ref[0]
    d_im = decay_ref[1]

    def eight_chunks(k, carry):
        h_re, h_im = carry
        r0 = pl.multiple_of(k * 8, 8)
        inc = s_ref[pl.ds(r0, 8), :]
        rows_re, rows_im = [], []
        for t in range(8):
            rows_re.append(h_re)
            rows_im.append(h_im)
            h_re, h_im = (d_re * h_re - d_im * h_im + inc[t:t + 1, :sw],
                          d_re * h_im + d_im * h_re + inc[t:t + 1, sw:])
        hs_ref[pl.ds(r0, 8), :] = jnp.concatenate(
            [jnp.concatenate(rows_re, axis=0), jnp.concatenate(rows_im, axis=0)], axis=1)
        return h_re, h_im

    h_re, h_im = lax.fori_loop(0, tm // 8, eight_chunks, (h_ref[0:1, :], h_ref[1:2, :]))
    h_ref[0:1, :] = h_re
    h_ref[1:2, :] = h_im
    y_ref[0] = r[:, :S5_CHUNK_WIDTH] + jnp.dot(hs_ref[...].astype(MXU_DTYPE), wout_ref[...],
                                                preferred_element_type=f32)


def s5_scan(u, tw, w_out, decay, *, tm=256):
    bsz, seq, _ = u.shape
    n_chunks = seq // S5_CHUNK
    tm = min(tm, n_chunks)
    assert seq % S5_CHUNK == 0 and n_chunks % tm == 0 and tm % 8 == 0
    once = pl.Buffered(1)
    y = pl.pallas_call(
        functools.partial(_s5_scan_body, tm=tm),
        grid=(bsz, n_chunks // tm),
        in_specs=[pl.BlockSpec((1, tm, S5_CHUNK_WIDTH), lambda b, i: (b, i, 0)),
                  pl.BlockSpec(tw.shape, lambda b, i: (0, 0), pipeline_mode=once),
                  pl.BlockSpec(w_out.shape, lambda b, i: (0, 0), pipeline_mode=once),
                  pl.BlockSpec(decay.shape, lambda b, i: (0, 0, 0), pipeline_mode=once)],
        out_specs=pl.BlockSpec((1, tm, S5_CHUNK_WIDTH), lambda b, i: (b, i, 0)),
        out_shape=jax.ShapeDtypeStruct((bsz, n_chunks, S5_CHUNK_WIDTH), jnp.float32),
        scratch_shapes=[pltpu.VMEM((2, S5_STATE_WIDTH), jnp.float32),
                        pltpu.VMEM((tm, 2 * S5_STATE_WIDTH), jnp.float32),
                        pltpu.VMEM((tm, 2 * S5_STATE_WIDTH), jnp.float32)],
        compiler_params=pltpu.CompilerParams(
            dimension_semantics=("parallel", "arbitrary"), vmem_limit_bytes=VMEM_LIMIT_BYTES),
        name="s5_scan",
    )(u.reshape(bsz, n_chunks, S5_CHUNK_WIDTH), tw, w_out, decay)
    return y.reshape(bsz, seq, SSM_WIDTH)


def _s5_out_body(y_ref, u_ref, d_ref, wglu_ref, bglu_ref, wout_ref, o_ref):
    f32 = jnp.float32
    y = jax.nn.gelu(y_ref[...] + d_ref[...] * u_ref[...])
    gate = jnp.dot(y.astype(MXU_DTYPE), wglu_ref[...], preferred_element_type=f32) + bglu_ref[...]
    y = y * jax.nn.sigmoid(gate)
    o_ref[...] = jnp.dot(y.astype(MXU_DTYPE), wout_ref[...], preferred_element_type=f32)


def s5_output(y, u, d_skip, w_glu, b_glu, w_out, *, tm=1024):
    m, c = y.shape
    tm = min(tm, m)
    row = lambda i: (i, 0)
    full = lambda i: (0, 0)
    return pl.pallas_call(
        _s5_out_body,
        grid=(m // tm,),
        in_specs=[pl.BlockSpec((tm, c), row), pl.BlockSpec((tm, c), row), pl.BlockSpec((1, c), full),
                  pl.BlockSpec((c, c), full), pl.BlockSpec((1, c), full), pl.BlockSpec((c, D_MODEL), full)],
        out_specs=pl.BlockSpec((tm, D_MODEL), row),
        out_shape=jax.ShapeDtypeStruct((m, D_MODEL), jnp.float32),
        compiler_params=pltpu.CompilerParams(
            dimension_semantics=("parallel",), vmem_limit_bytes=VMEM_LIMIT_BYTES),
        name="s5_output",
    )(y, u, d_skip.reshape(1, c), w_glu.astype(MXU_DTYPE), b_glu.reshape(1, c), w_out.astype(MXU_DTYPE))


def s5_branch(u, log_dt, lam_re, lam_im, b_re, b_im, c_re, c_im, d_skip, w_glu, b_glu, w_out):
    bsz, seq, _ = u.shape
    tw, w_state_out, decay = s5_operators(log_dt, lam_re, lam_im, b_re, b_im, c_re, c_im)
    y = s5_scan(u, tw, w_state_out, decay)
    out = s5_output(y.reshape(bsz * seq, SSM_WIDTH), u.reshape(bsz * seq, SSM_WIDTH),
                    d_skip, w_glu, b_glu, w_out)
    return out.reshape(bsz, seq, D_MODEL)


HG_CHUNK = 128
HG_LEVELS = 7
HG_KDIM = HGRN_HEADS * HGRN_DK


def _hgrn_segment_sums():
    c = HG_CHUNK
    t = np.arange(c)[:, None]
    u = np.arange(c)[None, :]
    blocks = []
    for lvl in range(1, HG_LEVELS + 1):
        m = (t >> lvl << lvl) + (1 << (lvl - 1)) - 1
        right = ((t >> (lvl - 1)) & 1) == 1
        blocks.append(np.where(right, (u > m) & (u <= t), (u > t) & (u <= m)))
    blocks.append(u <= t)
    blocks.append(u > t)
    return np.concatenate(blocks, axis=0).astype(np.float32)


def _hgrn_body(q_ref, z_ref, v_ref, g_ref, seg_ref, lb_ref, ng_ref, hmean_ref, wout_ref, o_ref, st_ref):
    f32 = jnp.float32
    c = HG_CHUNK

    @pl.when(pl.program_id(1) == 0)
    def _():
        st_ref[...] = jnp.zeros_like(st_ref)

    q = q_ref[0]
    z = z_ref[0]
    v = v_ref[0]
    lb = lb_ref[...]
    f = lb + (1.0 - lb) * jax.nn.sigmoid(z)
    logf = jnp.log(jnp.maximum(f, F_MIN))
    kin = (1.0 - lb) * jax.nn.sigmoid(-z)

    p1 = logf.astype(MXU_DTYPE)
    r1 = logf - p1.astype(f32)
    p2 = r1.astype(MXU_DTYPE)
    p3 = (r1 - p2.astype(f32)).astype(MXU_DTYPE)
    seg = seg_ref[...]
    sums = (jnp.dot(seg, p1, preferred_element_type=f32) + jnp.dot(seg, p2, preferred_element_type=f32)
            + jnp.dot(seg, p3, preferred_element_type=f32))

    lane_head = lax.broadcasted_iota(jnp.int32, (c, HG_KDIM), 1) // HGRN_DK
    tok = lax.broadcasted_iota(jnp.int32, (c, HG_KDIM), 0)
    row_t = lax.broadcasted_iota(jnp.int32, (HGRN_HEADS * c, c), 0) % c
    col_s = lax.broadcasted_iota(jnp.int32, (HGRN_HEADS * c, c), 1)

    def per_head_rows(x):
        return jnp.concatenate([jnp.where(lane_head == h, x, 0.0) for h in range(HGRN_HEADS)],
                               axis=0).astype(MXU_DTYPE)

    def scores(ql, kl):
        return lax.dot_general(per_head_rows(ql), kl.astype(MXU_DTYPE), (((1,), (1,)), ((), ())),
                               preferred_element_type=f32)

    att = jnp.where(row_t == col_s, scores(q, kin), 0.0)
    for lvl in range(1, HG_LEVELS + 1):
        decay = jnp.exp(sums[(lvl - 1) * c:lvl * c])
        right = ((tok >> (lvl - 1)) & 1) == 1
        a = scores(jnp.where(right, q * decay, 0.0), jnp.where(right, 0.0, kin * decay))
        att = att + jnp.where((row_t >> lvl) == (col_s >> lvl), a, 0.0)

    b = sums[HG_LEVELS * c:(HG_LEVELS + 1) * c]
    tail = sums[(HG_LEVELS + 1) * c:(HG_LEVELS + 2) * c]
    v_m = v.astype(MXU_DTYPE)
    st = st_ref[...]
    o = lax.dot_general((q * jnp.exp(b)).astype(MXU_DTYPE), st.astype(MXU_DTYPE),
                        (((1,), (1,)), ((), ())), preferred_element_type=f32)
    for h in range(HGRN_HEADS):
        o_h = jnp.dot(att[h * c:(h + 1) * c].astype(MXU_DTYPE), v_m, preferred_element_type=f32)
        o = o + jnp.where(lane_head == h, o_h, 0.0)

    kv = jnp.dot(v.T.astype(MXU_DTYPE), (kin * jnp.exp(tail)).astype(MXU_DTYPE), preferred_element_type=f32)
    sr = lax.broadcasted_iota(jnp.int32, st.shape, 0) // HGRN_DV
    sc = lax.broadcasted_iota(jnp.int32, st.shape, 1) // HGRN_DK
    st_ref[...] = st * jnp.exp(b[c - 1:c, :]) + jnp.where(sr == sc, kv, 0.0)

    o2 = o * o
    o2_hi = o2.astype(MXU_DTYPE)
    o2_lo = (o2 - o2_hi.astype(f32)).astype(MXU_DTYPE)
    ms = (jnp.dot(o2_hi, hmean_ref[...], preferred_element_type=f32)
          + jnp.dot(o2_lo, hmean_ref[...], preferred_element_type=f32))
    g = g_ref[0]
    out = o * lax.rsqrt(ms + RMS_EPS) * ng_ref[...] * (g * jax.nn.sigmoid(g))
    o_ref[0] = jnp.dot(out.astype(MXU_DTYPE), wout_ref[...], preferred_element_type=f32)


def hgrn2_branch(hg_in, lower_bound, norm_g, w_out):
    bsz, seq, _ = hg_in.shape
    assert seq % HG_CHUNK == 0 and HGRN_DK == HGRN_DV and HG_IN_WIDTH == 4 * HG_KDIM
    seg = jnp.asarray(_hgrn_segment_sums(), MXU_DTYPE)
    head_mean = jnp.asarray(np.kron(np.eye(HGRN_HEADS), np.full((HGRN_DV, HGRN_DV), 1.0 / HGRN_DV)), MXU_DTYPE)
    tok = lambda b, i: (b, i, 0)
    full = lambda b, i: (0, 0)
    part = lambda k: pl.BlockSpec((1, HG_CHUNK, HG_KDIM), lambda b, i: (b, i, k))
    return pl.pallas_call(
        _hgrn_body,
        grid=(bsz, seq // HG_CHUNK),
        in_specs=[part(0), part(1), part(2), part(3),
                  pl.BlockSpec(seg.shape, full), pl.BlockSpec((1, HG_KDIM), full),
                  pl.BlockSpec((1, HGRN_WIDTH), full), pl.BlockSpec(head_mean.shape, full),
                  pl.BlockSpec((HGRN_WIDTH, D_MODEL), full)],
        out_specs=pl.BlockSpec((1, HG_CHUNK, D_MODEL), tok),
        out_shape=jax.ShapeDtypeStruct((bsz, seq, D_MODEL), jnp.float32),
        scratch_shapes=[pltpu.VMEM((HGRN_WIDTH, HG_KDIM), jnp.float32)],
        compiler_params=pltpu.CompilerParams(
            dimension_semantics=("parallel", "arbitrary"), vmem_limit_bytes=VMEM_LIMIT_BYTES),
        name="hgrn2",
    )(hg_in, hg_in, hg_in, hg_in, seg, lower_bound.reshape(1, HG_KDIM).astype(jnp.float32),
      jnp.tile(norm_g.astype(jnp.float32), HGRN_HEADS).reshape(1, HGRN_WIDTH), head_mean,
      w_out.astype(MXU_DTYPE))


ATT_QK_DIM = LANE
DSA_TILE = 512


def _rms_rows(x, g):
    return x * lax.rsqrt(jnp.mean(x * x, axis=-1, keepdims=True) + RMS_EPS) * g


def _dsa_project_body(x_ref, gq_ref, gkv_ref, wq_ref, wqi_ref, wkt_ref, wv_ref,
                      q_ref, qi_ref, kt_ref, v_ref, kidt_ref):
    f32 = jnp.float32
    tm = x_ref.shape[1]
    x = x_ref[0]
    cq = _rms_rows(x[:, :ATT_Q_RANK], gq_ref[...]).astype(MXU_DTYPE)
    ckv = _rms_rows(x[:, ATT_Q_RANK:ATT_Q_RANK + ATT_KV_RANK], gkv_ref[...])
    q_all = jnp.dot(cq, wq_ref[...], preferred_element_type=f32)
    qi_all = jnp.dot(cq, wqi_ref[...], preferred_element_type=f32)
    v_all = jnp.dot(ckv.astype(MXU_DTYPE), wv_ref[...], preferred_element_type=f32)
    kt_all = jnp.dot(wkt_ref[...], ckv.T.astype(MXU_DTYPE), preferred_element_type=f32)

    lane = lax.broadcasted_iota(jnp.int32, (tm, LANE), 1)
    t_loc = lax.broadcasted_iota(jnp.int32, (tm, LANE), 0)
    t_even = (t_loc // 2 * 2).astype(f32)
    t_odd = (t_loc % 2).astype(f32)
    sub = lax.broadcasted_iota(jnp.int32, (LANE, tm), 0)
    s_loc = lax.broadcasted_iota(jnp.int32, (LANE, tm), 1)
    k_rows = jnp.where(sub < ATT_HEAD_DIM + 2, 1.0,
                       jnp.where(sub == ATT_HEAD_DIM + 2, (s_loc // 2 * 2).astype(f32), (s_loc % 2).astype(f32)))
    for h in range(ATT_HEADS):
        slope = 2.0 ** (-8.0 * (h + 1) / ATT_HEADS)
        q_cols = jnp.where(lane == ATT_HEAD_DIM, -slope * t_even,
                           jnp.where(lane == ATT_HEAD_DIM + 1, -slope * t_odd, slope))
        q_h = q_all[:, h * LANE:(h + 1) * LANE]
        q_ref[0, h] = jnp.where(lane < ATT_HEAD_DIM, q_h,
                                jnp.where(lane < ATT_HEAD_DIM + 4, q_cols, 0.0)).astype(q_ref.dtype)
        k_h = kt_all[h * LANE:(h + 1) * LANE, :]
        kt_ref[0, h] = jnp.where(sub < ATT_HEAD_DIM, k_h,
                                 jnp.where(sub < ATT_HEAD_DIM + 4, k_rows, 0.0)).astype(kt_ref.dtype)
        v_h = v_all[:, h * LANE:(h + 1) * LANE]
        v_ref[0, h] = jnp.where(lane == ATT_HEAD_DIM, 1.0, v_h).astype(v_ref.dtype)
    for h in range(IDX_HEADS):
        qi_ref[0, h] = qi_all[:, h * LANE:(h + 1) * LANE].astype(qi_ref.dtype)
    kidt_ref[0] = x[:, ATT_IDX_BLOCK * LANE:(ATT_IDX_BLOCK + 1) * LANE].T.astype(kidt_ref.dtype)


def _head_padded(w, n_heads, dim):
    k = w.shape[0]
    return jnp.pad(w.reshape(k, n_heads, dim), ((0, 0), (0, 0), (0, LANE - dim))).reshape(k, n_heads * LANE)


def dsa_project(att_in, q_norm_g, kv_norm_g, w_uq, w_qidx, w_ukv):
    bsz, seq, _ = att_in.shape
    tm = DSA_TILE
    assert seq % tm == 0 and 8 % ATT_HEADS == 0 and tm <= 512
    w_q = _head_padded(w_uq * ATT_HEAD_DIM ** -0.5, ATT_HEADS, ATT_HEAD_DIM).astype(MXU_DTYPE)
    w_qi = _head_padded(w_qidx * IDX_DIM ** -0.5, IDX_HEADS, IDX_DIM).astype(MXU_DTYPE)
    w_kt = _head_padded(w_ukv[:, :ATT_WIDTH], ATT_HEADS, ATT_HEAD_DIM).T.astype(MXU_DTYPE)
    w_v = _head_padded(w_ukv[:, ATT_WIDTH:], ATT_HEADS, ATT_HEAD_DIM).astype(MXU_DTYPE)
    tok = lambda b, i: (b, 0, i, 0)
    full = lambda b, i: (0, 0)
    dt = MXU_DTYPE
    return pl.pallas_call(
        _dsa_project_body,
        grid=(bsz, seq // tm),
        in_specs=[pl.BlockSpec((1, tm, ATT_IN_WIDTH), lambda b, i: (b, i, 0)),
                  pl.BlockSpec((1, ATT_Q_RANK), full), pl.BlockSpec((1, ATT_KV_RANK), full),
                  pl.BlockSpec(w_q.shape, full), pl.BlockSpec(w_qi.shape, full),
                  pl.BlockSpec(w_kt.shape, full), pl.BlockSpec(w_v.shape, full)],
        out_specs=[pl.BlockSpec((1, ATT_HEADS, tm, LANE), tok),
                   pl.BlockSpec((1, IDX_HEADS, tm, LANE), tok),
                   pl.BlockSpec((1, ATT_HEADS, LANE, tm), lambda b, i: (b, 0, 0, i)),
                   pl.BlockSpec((1, ATT_HEADS, tm, LANE), tok),
                   pl.BlockSpec((1, LANE, tm), lambda b, i: (b, 0, i))],
        out_shape=[jax.ShapeDtypeStruct((bsz, ATT_HEADS, seq, LANE), dt),
                   jax.ShapeDtypeStruct((bsz, IDX_HEADS, seq, LANE), dt),
                   jax.ShapeDtypeStruct((bsz, ATT_HEADS, LANE, seq), dt),
                   jax.ShapeDtypeStruct((bsz, ATT_HEADS, seq, LANE), dt),
                   jax.ShapeDtypeStruct((bsz, LANE, seq), dt)],
        compiler_params=pltpu.CompilerParams(
            dimension_semantics=("parallel", "parallel"), vmem_limit_bytes=VMEM_LIMIT_BYTES),
        name="dsa_project",
    )(att_in, q_norm_g.reshape(1, ATT_Q_RANK), kv_norm_g.reshape(1, ATT_KV_RANK), w_q, w_qi, w_kt, w_v)


_INT_MIN = -2 ** 31
_MASK_KEY = int(np.float32(MASK_VALUE).view(np.int32)) ^ 0x7FFFFFFF
_SEARCH_ROWS = 128
_ATT_ROWS = 32
_SEARCH_LANES = 512
_LIST_DEPTH = 12
_LIST_ROWS = 16
_LIST_MIN_BLOCKS = 4


def _key_to_score(key):
    return lax.bitcast_convert_type(jnp.where(key < 0, key ^ 0x7FFFFFFF, key), jnp.float32)


def _dsa_body(qi_ref, kj_ref, qidx_ref, w_ref, kidt_ref, q_ref, kt_ref, v_ref, tri_ref, wout_ref,
              o_ref, keys_ref, thr_ref, need_ref, carry_ref, m_ref, acc_ref, bias_ref, s_ref, p_ref,
              alpha_ref, cand_ref, *, tq, tk, seq, n_sel):
    f32 = jnp.float32
    p_id = pl.program_id(1)
    i = qi_ref[p_id]
    j = kj_ref[p_id]
    row_local = lax.broadcasted_iota(jnp.int32, (tq, tk), 0)
    col_local = lax.broadcasted_iota(jnp.int32, (tq, tk), 1)

    @pl.when(j == 0)
    def _select():
        def score_block(jj, carry):
            off = pl.multiple_of(jj * tk, tk)
            kb = kidt_ref[0, :, pl.ds(off, tk)]
            sc = jnp.zeros((tq, tk), f32)
            for h in range(IDX_HEADS):
                s = jnp.dot(qidx_ref[0, h], kb, preferred_element_type=f32)
                w_h = w_ref[0, :, IDX_DIM + h:IDX_DIM + h + 1] * IDX_HEADS ** -0.5
                sc = sc + jnp.maximum(s, 0.0) * w_h
            causal = (col_local + jj * tk) <= (row_local + i * tq)
            sc = jnp.where(causal, sc, MASK_VALUE)
            keys_ref[:, pl.ds(off, tk)] = jnp.where(sc == 0.0, 0.0, sc)
            return carry

        lax.fori_loop(0, i + 1, score_block, 0)

        n_blocks = i + 1
        n_masked_tail = (seq - n_blocks * tk).astype(f32)

        def row_group(r, carry):
            r0 = pl.multiple_of(r * _SEARCH_ROWS, _SEARCH_ROWS)

            def count_ge(ref, n_iter, cand):
                cand_b = jnp.broadcast_to(_key_to_score(cand), (_SEARCH_ROWS, LANE))

                def chunk(c, acc):
                    base = pl.multiple_of(c * _SEARCH_LANES, _SEARCH_LANES)
                    for u in range(_SEARCH_LANES // LANE):
                        kk = ref[pl.ds(r0, _SEARCH_ROWS), pl.ds(base + u * LANE, LANE)]
                        acc = acc + jnp.where(kk >= cand_b, 1.0, 0.0)
                    return acc

                acc = lax.fori_loop(0, n_iter, chunk, jnp.zeros((_SEARCH_ROWS, LANE), f32))
                cnt = jnp.sum(acc, axis=1, keepdims=True)
                return cnt + jnp.where(cand <= _MASK_KEY, n_masked_tail, 0.0)

            def kth_largest_key(count):
                zero = jnp.zeros((_SEARCH_ROWS, 1), jnp.int32)
                v0 = jnp.where(count(zero) >= n_sel, zero, zero + _INT_MIN)

                def bit_step(b, v):
                    cand = v | jnp.left_shift(jnp.int32(1), 30 - b)
                    return jnp.where(count(cand) >= n_sel, cand, v)

                return lax.fori_loop(0, 31, bit_step, v0)

            count_all = functools.partial(count_ge, keys_ref, n_blocks * (tk // _SEARCH_LANES))

            def store(v, n_above):
                thr_ref[pl.ds(r0, _SEARCH_ROWS), :] = _key_to_score(v)
                need_ref[pl.ds(r0, _SEARCH_ROWS), :] = n_sel - n_above

            def search_all_keys():
                v = kth_largest_key(count_all)
                store(v, count_all(v + 1))

            @pl.when(n_blocks < _LIST_MIN_BLOCKS)
            def _():
                search_all_keys()

            @pl.when(n_blocks >= _LIST_MIN_BLOCKS)
            def _():
                def shortlist(sub, carry):
                    rr = pl.multiple_of(r0 + sub * _LIST_ROWS, _LIST_ROWS)

                    def insert_block(c, tops):
                        base = pl.multiple_of(c * tk, tk)
                        for u in range(tk // LANE):
                            x = keys_ref[pl.ds(rr, _LIST_ROWS), pl.ds(base + u * LANE, LANE)]
                            new = []
                            for t in range(_LIST_DEPTH):
                                new.append(jnp.maximum(tops[t], x))
                                x = jnp.minimum(tops[t], x)
                            tops = tuple(new)
                        return tops

                    tops = lax.fori_loop(0, n_blocks, insert_block,
                                         tuple(jnp.full((_LIST_ROWS, LANE), -jnp.inf, f32)
                                               for _ in range(_LIST_DEPTH)))
                    for t in range(_LIST_DEPTH):
                        cand_ref[pl.ds(rr, _LIST_ROWS), t * LANE:(t + 1) * LANE] = tops[t]
                    return carry

                lax.fori_loop(0, _SEARCH_ROWS // _LIST_ROWS, shortlist, 0)
                v = kth_largest_key(functools.partial(count_ge, cand_ref,
                                                      _LIST_DEPTH * LANE // _SEARCH_LANES))
                n_at_least = count_all(v)
                n_above = count_all(v + 1)
                store(v, n_above)
                exact = (n_above < n_sel) & (n_at_least >= n_sel)
                pl.when(jnp.min(jnp.where(exact, 1.0, 0.0)) < 0.5)(search_all_keys)

            return carry

        lax.fori_loop(0, tq // _SEARCH_ROWS, row_group, 0)
        carry_ref[...] = jnp.zeros_like(carry_ref)
        m_ref[...] = jnp.full_like(m_ref, MASK_VALUE)
        acc_ref[...] = jnp.zeros_like(acc_ref)

    keys_blk = keys_ref[:, pl.ds(pl.multiple_of(j * tk, tk), tk)]
    thr = thr_ref[...]
    eq = keys_blk == thr
    tie_rank = carry_ref[...] + jnp.dot(jnp.where(eq, 1.0, 0.0).astype(MXU_DTYPE), tri_ref[...],
                                         preferred_element_type=f32)
    carry_ref[...] = tie_rank[:, tk - 1:tk]
    tie_bias = jnp.where(eq, jnp.where(tie_rank <= need_ref[...], 0.0, MASK_VALUE), MASK_VALUE)
    bias = jnp.where(keys_blk > thr, 0.0, tie_bias)
    causal = (row_local + i * tq) >= (col_local + j * tk)
    bias_ref[...] = jnp.where(causal, bias, MASK_VALUE)

    def logits(h, slot):
        s_ref[slot] = jnp.dot(q_ref[0, h], kt_ref[0, h], preferred_element_type=f32)

    def softmax(h, slot):
        slope_tk = jnp.right_shift(tk, h + 1).astype(f32)
        for r in range(0, tq, _ATT_ROWS):
            rows = pl.ds(r, _ATT_ROWS)
            s = s_ref[slot, rows, :] + bias_ref[rows, :]
            m_old = m_ref[h, rows, :] - slope_tk
            m_new = jnp.maximum(m_old, jnp.max(s, axis=1, keepdims=True))
            p_ref[slot, rows, :] = jnp.exp(s - m_new).astype(MXU_DTYPE)
            alpha_ref[slot, rows, :] = jnp.exp(m_old - m_new)
            m_ref[h, rows, :] = m_new

    def weighted_values(h, slot):
        acc_ref[h] = alpha_ref[slot] * acc_ref[h] + jnp.dot(p_ref[slot], v_ref[0, h],
                                                             preferred_element_type=f32)

    for t in range(ATT_HEADS + 2):
        if t < ATT_HEADS:
            logits(t, t % 2)
        if 1 <= t <= ATT_HEADS:
            softmax(t - 1, (t - 1) % 2)
        if t >= 2:
            weighted_values(t - 2, t % 2)

    @pl.when(j == i)
    def _finish():
        heads = []
        for h in range(ATT_HEADS):
            a = acc_ref[h]
            heads.append((a[:, :ATT_HEAD_DIM] / a[:, ATT_HEAD_DIM:ATT_HEAD_DIM + 1]).astype(MXU_DTYPE))
        o_ref[0] = jnp.dot(jnp.concatenate(heads, axis=1), wout_ref[...], preferred_element_type=f32)


def dsa_attention(att_in, q_idx, kid_t, q, k_t, v_aug, w_out):
    bsz, _, seq, _ = q.shape
    tq = tk = DSA_TILE
    n_sel = min(TOPK_MAX, seq // 4)
    nq = seq // tq
    assert seq % tq == 0 and tk % _SEARCH_LANES == 0 and tq % _SEARCH_ROWS == 0
    pairs = [(a, b) for a in range(nq) for b in range(a + 1)]
    qi = jnp.asarray([a for a, _ in pairs], jnp.int32)
    kj = jnp.asarray([b for _, b in pairs], jnp.int32)
    tri = jnp.triu(jnp.ones((tk, tk), MXU_DTYPE))
    grid_spec = pltpu.PrefetchScalarGridSpec(
        num_scalar_prefetch=2,
        grid=(bsz, len(pairs)),
        in_specs=[
            pl.BlockSpec((1, IDX_HEADS, tq, LANE), lambda b, p, qi, kj: (b, 0, qi[p], 0)),
            pl.BlockSpec((1, tq, LANE), lambda b, p, qi, kj: (b, qi[p], ATT_IDX_BLOCK)),
            pl.BlockSpec((1, LANE, seq), lambda b, p, qi, kj: (b, 0, 0)),
            pl.BlockSpec((1, ATT_HEADS, tq, ATT_QK_DIM), lambda b, p, qi, kj: (b, 0, qi[p], 0)),
            pl.BlockSpec((1, ATT_HEADS, ATT_QK_DIM, tk), lambda b, p, qi, kj: (b, 0, 0, kj[p])),
            pl.BlockSpec((1, ATT_HEADS, tk, LANE), lambda b, p, qi, kj: (b, 0, kj[p], 0)),
            pl.BlockSpec((tk, tk), lambda b, p, qi, kj: (0, 0)),
            pl.BlockSpec((ATT_WIDTH, D_MODEL), lambda b, p, qi, kj: (0, 0)),
        ],
        out_specs=pl.BlockSpec((1, tq, D_MODEL), lambda b, p, qi, kj: (b, qi[p], 0)),
        scratch_shapes=[
            pltpu.VMEM((tq, seq), jnp.float32),
            pltpu.VMEM((tq, 1), jnp.float32),
            pltpu.VMEM((tq, 1), jnp.float32),
            pltpu.VMEM((tq, 1), jnp.float32),
            pltpu.VMEM((ATT_HEADS, tq, 1), jnp.float32),
            pltpu.VMEM((ATT_HEADS, tq, LANE), jnp.float32),
            pltpu.VMEM((tq, tk), jnp.float32),
            pltpu.VMEM((2, tq, tk), jnp.float32),
            pltpu.VMEM((2, tq, tk), MXU_DTYPE),
            pltpu.VMEM((2, tq, 1), jnp.float32),
            pltpu.VMEM((tq, _LIST_DEPTH * LANE), jnp.float32),
        ])
    return pl.pallas_call(
        functools.partial(_dsa_body, tq=tq, tk=tk, seq=seq, n_sel=n_sel),
        grid_spec=grid_spec,
        out_shape=jax.ShapeDtypeStruct((bsz, seq, D_MODEL), jnp.float32),
        compiler_params=pltpu.CompilerParams(
            dimension_semantics=("parallel", "arbitrary"), vmem_limit_bytes=VMEM_LIMIT_BYTES),
        name="dsa_attention",
    )(qi, kj, q_idx, att_in, kid_t, q, k_t, v_aug, tri, w_out)


def dsa_branch(att_in, q_norm_g, kv_norm_g, w_uq, w_qidx, w_ukv, w_out):
    q, q_idx, k_t, v_aug, kid_t = dsa_project(att_in, q_norm_g, kv_norm_g, w_uq, w_qidx, w_ukv)
    return dsa_attention(att_in, q_idx, kid_t, q, k_t, v_aug, w_out.astype(MXU_DTYPE))


MOE_TOKENS = 1024
MOE_ROW_CLASSES = (256, 320, 512, 1024)
MOE_FFN_TILES = (896, 512, 256)
MOE_VMEM_LIMIT_BYTES = 56 * 1024 * 1024


def _moe_body(cnt_ref, h_ref, gate_ref, gatet_ref, tri_ref, wg_ref, wu_ref, wd_ref, g_ref, b_ref, o_ref,
              xb_ref, gather_ref, scatter_ref, xc_ref, yc_ref, acc_ref):
    f32 = jnp.float32
    tm = MOE_TOKENS
    i = pl.program_id(0)
    e = pl.program_id(1)
    f = pl.program_id(2)
    last_f = pl.num_programs(2) - 1

    @pl.when((e == 0) & (f == 0))
    def _():
        xb_ref[...] = h_ref[...].astype(MXU_DTYPE)
        acc_ref[...] = jnp.zeros_like(acc_ref)

    lane = lax.broadcasted_iota(jnp.int32, gate_ref.shape, 1)
    gate_col = jnp.sum(jnp.where(lane == e, gate_ref[...], 0.0), axis=-1, keepdims=True)

    def routed(size):
        def run():
            @pl.when(f == 0)
            def _():
                active_row = jnp.where(gatet_ref[pl.ds(e, 1), :] != 0.0, 1.0, 0.0)
                rank_row = jnp.dot(jnp.broadcast_to(active_row, (8, tm)).astype(MXU_DTYPE), tri_ref[...],
                                   preferred_element_type=f32)[0:1]
                slot = lax.broadcasted_iota(jnp.int32, (size, tm), 0).astype(f32)
                gather = jnp.where((slot == rank_row) & (active_row != 0.0), 1.0, 0.0)
                gather_ref[0:size, :] = gather.astype(MXU_DTYPE)
                rank_col = jnp.broadcast_to(rank_row, (LANE, tm)).T[:, 0:1]
                slot_l = lax.broadcasted_iota(jnp.int32, (tm, size), 1).astype(f32)
                scatter = jnp.where((slot_l == rank_col) & (gate_col != 0.0), 1.0, 0.0)
                scatter_ref[:, 0:size] = scatter.astype(MXU_DTYPE)
                xc_ref[0:size, :] = jnp.dot(gather.astype(MXU_DTYPE), xb_ref[...],
                                            preferred_element_type=f32).astype(MXU_DTYPE)
                yc_ref[0:size, :] = jnp.zeros((size, D_MODEL), f32)

            xc = xc_ref[0:size, :]
            a = jnp.dot(xc, wg_ref[0], preferred_element_type=f32)
            u = jnp.dot(xc, wu_ref[0], preferred_element_type=f32)
            act = a * jax.nn.sigmoid(a) * u
            yc_ref[0:size, :] += jnp.dot(act.astype(MXU_DTYPE), wd_ref[0], preferred_element_type=f32)

            @pl.when(f == last_f)
            def _():
                back = jnp.dot(scatter_ref[:, 0:size], yc_ref[0:size, :].astype(MXU_DTYPE),
                               preferred_element_type=f32)
                acc_ref[...] += gate_col * back
        return run

    count = cnt_ref[i * N_EXPERTS + e]
    size_class = sum((count > s).astype(jnp.int32) for s in MOE_ROW_CLASSES[:-1])
    for k, size in enumerate(MOE_ROW_CLASSES):
        pl.when((size_class == k) & (count > 0))(routed(size))

    @pl.when((e == pl.num_programs(1) - 1) & (f == last_f))
    def _():
        o_ref[...] = _layer_norm_rows(DEEPNORM_ALPHA * h_ref[...] + acc_ref[...], g_ref[...], b_ref[...])


def moe_experts_norm(h, routing, wg, wu, wd, ln_g, ln_b):
    gates, gates_t, counts = routing
    m, d = h.shape
    n_experts, _, f_dim = wg.shape
    tm = MOE_TOKENS
    tf = next(t for t in MOE_FFN_TILES if f_dim % t == 0)
    assert m % tm == 0 and MOE_ROW_CLASSES[-1] == tm
    n_tiles = m // tm
    tri = jnp.triu(jnp.ones((tm, tm), MXU_DTYPE), k=1)
    row = lambda i, e, f, c: (i, 0)
    full = lambda i, e, f, c: (0, 0)
    grid_spec = pltpu.PrefetchScalarGridSpec(
        num_scalar_prefetch=1,
        grid=(n_tiles, n_experts, f_dim // tf),
        in_specs=[pl.BlockSpec((tm, d), row), pl.BlockSpec((tm, LANE), row),
                  pl.BlockSpec((n_experts, tm), lambda i, e, f, c: (0, i)),
                  pl.BlockSpec((tm, tm), full),
                  pl.BlockSpec((1, d, tf), lambda i, e, f, c: (e, 0, f)),
                  pl.BlockSpec((1, d, tf), lambda i, e, f, c: (e, 0, f)),
                  pl.BlockSpec((1, tf, d), lambda i, e, f, c: (e, f, 0)),
                  pl.BlockSpec((1, d), full), pl.BlockSpec((1, d), full)],
        out_specs=pl.BlockSpec((tm, d), row),
        scratch_shapes=[pltpu.VMEM((tm, d), MXU_DTYPE),
                        pltpu.VMEM((tm, tm), MXU_DTYPE),
                        pltpu.VMEM((tm, tm), MXU_DTYPE),
                        pltpu.VMEM((tm, d), MXU_DTYPE),
                        pltpu.VMEM((tm, d), jnp.float32),
                        pltpu.VMEM((tm, d), jnp.float32)])
    return pl.pallas_call(
        _moe_body,
        grid_spec=grid_spec,
        out_shape=jax.ShapeDtypeStruct((m, d), jnp.float32),
        compiler_params=pltpu.CompilerParams(
            dimension_semantics=("parallel", "arbitrary", "arbitrary"),
            vmem_limit_bytes=MOE_VMEM_LIMIT_BYTES),
        name="moe_experts_norm",
    )(counts, h, gates, gates_t, tri, wg, wu, wd, ln_g.reshape(1, d), ln_b.reshape(1, d))


def _router_body(h_ref, r_ref, gate_ref, gatet_ref, cnt_ref):
    f32 = jnp.float32
    logits = jnp.dot(h_ref[...], r_ref[...], preferred_element_type=f32)
    lane = lax.broadcasted_iota(jnp.int32, logits.shape, 1)
    logits = jnp.where(lane < N_EXPERTS, logits, -jnp.inf)
    top1 = jnp.max(logits, axis=1, keepdims=True)
    idx1 = jnp.min(jnp.where(logits == top1, lane, LANE), axis=1, keepdims=True)
    rest = jnp.where(lane == idx1, -jnp.inf, logits)
    top2 = jnp.max(rest, axis=1, keepdims=True)
    idx2 = jnp.min(jnp.where(rest == top2, lane, LANE), axis=1, keepdims=True)
    e2 = jnp.exp(top2 - top1)
    gates = jnp.where(lane == idx1, 1.0 / (1.0 + e2), jnp.where(lane == idx2, e2 / (1.0 + e2), 0.0))
    gate_ref[...] = gates
    gatet_ref[...] = gates.T[0:N_EXPERTS, :]
    cnt_ref[0] = jnp.sum(jnp.where(gates != 0.0, 1.0, 0.0), axis=0, keepdims=True)


def router_gates(h, router):
    m, d = h.shape
    tm = MOE_TOKENS
    assert TOP_K == 2 and m % tm == 0
    r_pad = jnp.pad(router.astype(jnp.float32), ((0, 0), (0, LANE - N_EXPERTS)))
    gates, gates_t, counts = pl.pallas_call(
        _router_body,
        grid=(m // tm,),
        in_specs=[pl.BlockSpec((tm, d), lambda i: (i, 0)), pl.BlockSpec((d, LANE), lambda i: (0, 0))],
        out_specs=[pl.BlockSpec((tm, LANE), lambda i: (i, 0)), pl.BlockSpec((N_EXPERTS, tm), lambda i: (0, i)),
                   pl.BlockSpec((1, 1, LANE), lambda i: (i, 0, 0))],
        out_shape=[jax.ShapeDtypeStruct((m, LANE), jnp.float32),
                   jax.ShapeDtypeStruct((N_EXPERTS, m), jnp.float32),
                   jax.ShapeDtypeStruct((m // tm, 1, LANE), jnp.float32)],
        compiler_params=pltpu.CompilerParams(
            dimension_semantics=("parallel",), vmem_limit_bytes=VMEM_LIMIT_BYTES),
        name="router_gates",
    )(h, r_pad)
    return gates, gates_t, counts[:, 0, :N_EXPERTS].astype(jnp.int32).reshape(-1)


def kernel(x, w_in, ssm_log_dt, ssm_lambda_re, ssm_lambda_im, ssm_b_re, ssm_b_im, ssm_c_re, ssm_c_im,
           ssm_d, ssm_w_glu, ssm_b_glu, ssm_w_out, hgrn_lb_logits, hgrn_norm_g, hgrn_w_out,
           attn_q_norm_g, attn_kv_norm_g, attn_w_uq, attn_w_qidx, attn_w_ukv, attn_w_out, w_o,
           ln_g, ln_b, ffn_w_gate, ffn_w_up, ffn_w_down, moe_router, moe_w_gate, moe_w_up, moe_w_down):
    bsz, seq, d = x.shape
    m = bsz * seq
    bf16 = MXU_DTYPE
    assert MIX_IN_USED + N_BRANCHES * D_MODEL == N_IN
    lb_soft = jax.nn.softmax(hgrn_lb_logits.astype(jnp.float32), axis=0)
    lower_bounds = jnp.concatenate([jnp.zeros_like(lb_soft[:1]), jnp.cumsum(lb_soft[1:], axis=0)], axis=0)
    h = x.reshape(m, d)
    for l in range(DEPTH):
        w_mix = jnp.pad(w_in[l][:, :MIX_IN_USED], ((0, 0), (0, MIX_IN_WIDTH - MIX_IN_USED))).astype(bf16)
        u, hg_in, att_in = in_proj(h, w_mix)
        y_ssm = s5_branch(u.reshape(bsz, seq, SSM_WIDTH), ssm_log_dt[l], ssm_lambda_re[l], ssm_lambda_im[l],
                          ssm_b_re[l], ssm_b_im[l], ssm_c_re[l], ssm_c_im[l], ssm_d[l], ssm_w_glu[l],
                          ssm_b_glu[l], ssm_w_out[l])
        y_hg = hgrn2_branch(hg_in.reshape(bsz, seq, HG_IN_WIDTH), lower_bounds[l], hgrn_norm_g[l],
                            hgrn_w_out[l])
        y_att = dsa_branch(att_in.reshape(bsz, seq, ATT_IN_WIDTH), attn_q_norm_g[l], attn_kv_norm_g[l],
                           attn_w_uq[l], attn_w_qidx[l], attn_w_ukv[l], attn_w_out[l])
        h = merge_project_norm(h, y_ssm.reshape(m, d), y_hg.reshape(m, d), y_att.reshape(m, d),
                               w_in[l][:, MIX_IN_USED:].astype(bf16), w_o[l].astype(bf16),
                               ln_g[l, 0], ln_b[l, 0])
        if l % 2 == 0:
            h = swiglu_norm(h, ffn_w_gate[l // 2].astype(bf16), ffn_w_up[l // 2].astype(bf16),
                            ffn_w_down[l // 2].astype(bf16), ln_g[l, 1], ln_b[l, 1])
        else:
            gate_w = router_gates(h, moe_router[l // 2])
            h = moe_experts_norm(h, gate_w, moe_w_gate[l // 2].astype(bf16),
                                 moe_w_up[l // 2].astype(bf16), moe_w_down[l // 2].astype(bf16),
                                 ln_g[l, 1], ln_b[l, 1])
    return h.reshape(bsz, seq, d)
```

```python
import functools

import jax
import jax.numpy as jnp
import numpy as np
from jax import lax
from jax.experimental import pallas as pl
from jax.experimental.pallas import tpu as pltpu

D_MODEL = 1024
DEPTH = 2
SSM_WIDTH = 256
SSM_GROUP = 16
SSM_GROUPS = SSM_WIDTH // SSM_GROUP
SSM_STATE = 64
HGRN_HEADS = 4
HGRN_DK = 64
HGRN_DV = 64
HGRN_WIDTH = HGRN_HEADS * HGRN_DV
ATT_HEADS = 8
ATT_HEAD_DIM = 64
ATT_WIDTH = ATT_HEADS * ATT_HEAD_DIM
ATT_Q_RANK = 256
ATT_KV_RANK = 128
IDX_HEADS = 4
IDX_DIM = 64
TOPK_MAX = 256
MASK_VALUE = -1e30
N_BRANCHES = 3
N_EXPERTS = 8
TOP_K = 2
DEEPNORM_ALPHA = (2 * DEPTH) ** 0.25
LN_EPS = 1e-5
RMS_EPS = 1e-6
F_MIN = 1e-12

IN_SPLITS = (SSM_WIDTH, HGRN_HEADS * HGRN_DK, HGRN_HEADS * HGRN_DK, HGRN_WIDTH, HGRN_WIDTH,
             ATT_Q_RANK, ATT_KV_RANK, IDX_DIM, IDX_HEADS, N_BRANCHES * D_MODEL)
N_IN = sum(IN_SPLITS)

VMEM_LIMIT_BYTES = 48 * 1024 * 1024
LANE = 128
MXU_DTYPE = jnp.bfloat16


def _round_up(n, m):
    return (n + m - 1) // m * m


HG_IN_WIDTH = 2 * HGRN_HEADS * HGRN_DK + 2 * HGRN_WIDTH
ATT_IN_USED = ATT_Q_RANK + ATT_KV_RANK + IDX_DIM + IDX_HEADS
ATT_IN_WIDTH = _round_up(ATT_IN_USED, LANE)
ATT_IDX_BLOCK = (ATT_Q_RANK + ATT_KV_RANK) // LANE
MIX_IN_USED = SSM_WIDTH + HG_IN_WIDTH + ATT_IN_USED
MIX_IN_WIDTH = SSM_WIDTH + HG_IN_WIDTH + ATT_IN_WIDTH


def _in_proj_body(h_ref, w_ref, u_ref, hg_ref, att_ref):
    r = jnp.dot(h_ref[...].astype(MXU_DTYPE), w_ref[...], preferred_element_type=jnp.float32)
    u_ref[...] = r[:, :SSM_WIDTH]
    hg_ref[...] = r[:, SSM_WIDTH:SSM_WIDTH + HG_IN_WIDTH]
    att_ref[...] = r[:, SSM_WIDTH + HG_IN_WIDTH:]


def in_proj(h, w_mix, *, tm=512):
    m, d = h.shape
    row = lambda i: (i, 0)
    widths = (SSM_WIDTH, HG_IN_WIDTH, ATT_IN_WIDTH)
    return pl.pallas_call(
        _in_proj_body,
        grid=(m // tm,),
        in_specs=[pl.BlockSpec((tm, d), row), pl.BlockSpec((d, MIX_IN_WIDTH), lambda i: (0, 0))],
        out_specs=[pl.BlockSpec((tm, w), row) for w in widths],
        out_shape=[jax.ShapeDtypeStruct((m, w), jnp.float32) for w in widths],
        compiler_params=pltpu.CompilerParams(
            dimension_semantics=("parallel",), vmem_limit_bytes=VMEM_LIMIT_BYTES),
        name="in_proj",
    )(h, w_mix)


def _layer_norm_rows(y, g, b):
    mu = jnp.mean(y, axis=-1, keepdims=True)
    yc = y - mu
    var = jnp.mean(yc * yc, axis=-1, keepdims=True)
    return yc * lax.rsqrt(var + LN_EPS) * g + b


def _merge_body(h_ref, ys_ref, yh_ref, ya_ref, wgate_ref, wo_ref, g_ref, b_ref, o_ref):
    d = D_MODEL
    f32 = jnp.float32
    h = h_ref[...]
    gates = jax.nn.sigmoid(jnp.dot(h.astype(MXU_DTYPE), wgate_ref[...], preferred_element_type=f32))
    mixed = (gates[:, 0:d] * ys_ref[...] + gates[:, d:2 * d] * yh_ref[...]
             + gates[:, 2 * d:3 * d] * ya_ref[...])
    mix_out = jnp.dot(mixed.astype(MXU_DTYPE), wo_ref[...], preferred_element_type=f32)
    o_ref[...] = _layer_norm_rows(DEEPNORM_ALPHA * h + mix_out, g_ref[...], b_ref[...])


def merge_project_norm(h, y_ssm, y_hg, y_att, w_gates, w_o, ln_g, ln_b, *, tm=512):
    m, d = h.shape
    row = lambda i: (i, 0)
    full = lambda i: (0, 0)
    return pl.pallas_call(
        _merge_body,
        grid=(m // tm,),
        in_specs=[pl.BlockSpec((tm, d), row), pl.BlockSpec((tm, d), row), pl.BlockSpec((tm, d), row),
                  pl.BlockSpec((tm, d), row), pl.BlockSpec((d, N_BRANCHES * d), full),
                  pl.BlockSpec((d, d), full), pl.BlockSpec((1, d), full), pl.BlockSpec((1, d), full)],
        out_specs=pl.BlockSpec((tm, d), row),
        out_shape=jax.ShapeDtypeStruct((m, d), jnp.float32),
        compiler_params=pltpu.CompilerParams(
            dimension_semantics=("parallel",), vmem_limit_bytes=VMEM_LIMIT_BYTES),
        name="merge_project_norm",
    )(h, y_ssm, y_hg, y_att, w_gates, w_o, ln_g.reshape(1, d), ln_b.reshape(1, d))


DENSE_FFN_TOKENS = 512
DENSE_FFN_TILES = (1408, 512, 256)


def _ffn_body(h_ref, wg_ref, wu_ref, wd_ref, g_ref, b_ref, o_ref, acc_ref):
    f = pl.program_id(1)

    @pl.when(f == 0)
    def _():
        acc_ref[...] = jnp.zeros_like(acc_ref)

    x = h_ref[...].astype(MXU_DTYPE)
    a = jnp.dot(x, wg_ref[...], preferred_element_type=jnp.float32)
    u = jnp.dot(x, wu_ref[...], preferred_element_type=jnp.float32)
    act = a * jax.nn.sigmoid(a) * u
    acc_ref[...] += jnp.dot(act.astype(MXU_DTYPE), wd_ref[...], preferred_element_type=jnp.float32)

    @pl.when(f == pl.num_programs(1) - 1)
    def _():
        o_ref[...] = _layer_norm_rows(DEEPNORM_ALPHA * h_ref[...] + acc_ref[...], g_ref[...], b_ref[...])


def swiglu_norm(h, wg, wu, wd, ln_g, ln_b):
    m, d = h.shape
    f_dim = wg.shape[1]
    tm = DENSE_FFN_TOKENS
    tf = next(t for t in DENSE_FFN_TILES if f_dim % t == 0)
    assert m % tm == 0
    row = lambda i, f: (i, 0)
    full = lambda i, f: (0, 0)
    return pl.pallas_call(
        _ffn_body,
        grid=(m // tm, f_dim // tf),
        in_specs=[pl.BlockSpec((tm, d), row),
                  pl.BlockSpec((d, tf), lambda i, f: (0, f)), pl.BlockSpec((d, tf), lambda i, f: (0, f)),
                  pl.BlockSpec((tf, d), lambda i, f: (f, 0)),
                  pl.BlockSpec((1, d), full), pl.BlockSpec((1, d), full)],
        out_specs=pl.BlockSpec((tm, d), row),
        out_shape=jax.ShapeDtypeStruct((m, d), jnp.float32),
        scratch_shapes=[pltpu.VMEM((tm, d), jnp.float32)],
        compiler_params=pltpu.CompilerParams(
            dimension_semantics=("parallel", "arbitrary"), vmem_limit_bytes=VMEM_LIMIT_BYTES),
        name="swiglu_norm",
    )(h, wg, wu, wd, ln_g.reshape(1, d), ln_b.reshape(1, d))


S5_CHUNK = 8
S5_CHUNK_WIDTH = S5_CHUNK * SSM_WIDTH
S5_STATE_WIDTH = SSM_GROUPS * SSM_STATE


def s5_operators(log_dt, lam_re, lam_im, b_re, b_im, c_re, c_im):
    f32 = jnp.float32
    n = S5_CHUNK
    lre, lim = lam_re.astype(f32), lam_im.astype(f32)
    dt = jnp.exp(log_dt.astype(f32))[:, None]
    mag = jnp.exp(lre * dt)
    ang = lim * dt
    a_re, a_im = mag * jnp.cos(ang), mag * jnp.sin(ang)
    den = lre * lre + lim * lim
    coef_re = ((a_re - 1.0) * lre + a_im * lim) / den
    coef_im = (a_im * lre - (a_re - 1.0) * lim) / den
    br, bi = b_re.astype(f32), b_im.astype(f32)
    bb_re = coef_re[..., None] * br - coef_im[..., None] * bi
    bb_im = coef_re[..., None] * bi + coef_im[..., None] * br
    j = jnp.arange(n + 1, dtype=f32)[:, None, None]
    pw_re = jnp.exp(j * lre * dt) * jnp.cos(j * ang)
    pw_im = jnp.exp(j * lre * dt) * jnp.sin(j * ang)
    eye = jnp.eye(SSM_GROUPS, dtype=f32)
    cr, ci = c_re.astype(f32), c_im.astype(f32)
    ab_re = pw_re[..., None] * bb_re - pw_im[..., None] * bb_im
    ab_im = pw_re[..., None] * bb_im + pw_im[..., None] * bb_re
    kern = (jnp.einsum('gcp,jgpd->jgcd', cr, ab_re[:n]) - jnp.einsum('gcp,jgpd->jgcd', ci, ab_im[:n]))
    lag = jnp.arange(n)[None, :] - jnp.arange(n)[:, None]
    toep = jnp.where((lag >= 0)[:, :, None, None, None], kern[jnp.maximum(lag, 0)], 0.0)
    t_mat = jnp.einsum('abgcd,gh->agdbhc', toep, eye).reshape(S5_CHUNK_WIDTH, S5_CHUNK_WIDTH)
    v = jnp.stack([ab_re[:n][::-1], ab_im[:n][::-1]])
    w_in = jnp.einsum('rlgpc,gh->lhcrgp', v, eye).reshape(S5_CHUNK_WIDTH, 2 * S5_STATE_WIDTH)
    ar, ai = pw_re[1:], pw_im[1:]
    wo_re = jnp.einsum('gcp,lgp->gplc', cr, ar) - jnp.einsum('gcp,lgp->gplc', ci, ai)
    wo_im = -jnp.einsum('gcp,lgp->gplc', cr, ai) - jnp.einsum('gcp,lgp->gplc', ci, ar)
    w_out = jnp.einsum('rgplc,gh->rhplgc', jnp.stack([wo_re, wo_im]), eye)
    w_out = w_out.reshape(2 * S5_STATE_WIDTH, S5_CHUNK_WIDTH)
    decay = jnp.stack([pw_re[n].reshape(1, S5_STATE_WIDTH), pw_im[n].reshape(1, S5_STATE_WIDTH)])
    return (jnp.concatenate([t_mat, w_in], axis=1).astype(MXU_DTYPE), w_out.astype(MXU_DTYPE), decay)


def _s5_scan_body(u_ref, tw_ref, wout_ref, decay_ref, y_ref, h_ref, s_ref, hs_ref, *, tm):
    f32 = jnp.float32
    sw = S5_STATE_WIDTH

    @pl.when(pl.program_id(1) == 0)
    def _():
        h_ref[...] = jnp.zeros_like(h_ref)

    r = jnp.dot(u_ref[0].astype(MXU_DTYPE), tw_ref[...], preferred_element_type=f32)
    s_ref[...] = r[:, S5_CHUNK_WIDTH:]
    d_re = decay_ref[0]
    d_im = decay_ref[1]

    def eight_chunks(k, carry):
        h_re, h_im = carry
        r0 = pl.multiple_of(k * 8, 8)
        inc = s_ref[pl.ds(r0, 8), :]
        rows_re, rows_im = [], []
        for t in range(8):
            rows_re.append(h_re)
            rows_im.append(h_im)
            h_re, h_im = (d_re * h_re - d_im * h_im + inc[t:t + 1, :sw],
                          d_re * h_im + d_im * h_re + inc[t:t + 1, sw:])
        hs_ref[pl.ds(r0, 8), :] = jnp.concatenate(
            [jnp.concatenate(rows_re, axis=0), jnp.concatenate(rows_im, axis=0)], axis=1)
        return h_re, h_im

    h_re, h_im = lax.fori_loop(0, tm // 8, eight_chunks, (h_ref[0:1, :], h_ref[1:2, :]))
    h_ref[0:1, :] = h_re
    h_ref[1:2, :] = h_im
    y_ref[0] = r[:, :S5_CHUNK_WIDTH] + jnp.dot(hs_ref[...].astype(MXU_DTYPE), wout_ref[...],
                                                preferred_element_type=f32)


def s5_scan(u, tw, w_out, decay, *, tm=256):
    bsz, seq, _ = u.shape
    n_chunks = seq // S5_CHUNK
    tm = min(tm, n_chunks)
    assert seq % S5_CHUNK == 0 and n_chunks % tm == 0 and tm % 8 == 0
    once = pl.Buffered(1)
    y = pl.pallas_call(
        functools.partial(_s5_scan_body, tm=tm),
        grid=(bsz, n_chunks // tm),
        in_specs=[pl.BlockSpec((1, tm, S5_CHUNK_WIDTH), lambda b, i: (b, i, 0)),
                  pl.BlockSpec(tw.shape, lambda b, i: (0, 0), pipeline_mode=once),
                  pl.BlockSpec(w_out.shape, lambda b, i: (0, 0), pipeline_mode=once),
                  pl.BlockSpec(decay.shape, lambda b, i: (0, 0, 0), pipeline_mode=once)],
        out_specs=pl.BlockSpec((1, tm, S5_CHUNK_WIDTH), lambda b, i: (b, i, 0)),
        out_shape=jax.ShapeDtypeStruct((bsz, n_chunks, S5_CHUNK_WIDTH), jnp.float32),
        scratch_shapes=[pltpu.VMEM((2, S5_STATE_WIDTH), jnp.float32),
                        pltpu.VMEM((tm, 2 * S5_STATE_WIDTH), jnp.float32),
                        pltpu.VMEM((tm, 2 * S5_STATE_WIDTH), jnp.float32)],
        compiler_params=pltpu.CompilerParams(
            dimension_semantics=("parallel", "arbitrary"), vmem_limit_bytes=VMEM_LIMIT_BYTES),
        name="s5_scan",
    )(u.reshape(bsz, n_chunks, S5_CHUNK_WIDTH), tw, w_out, decay)
    return y.reshape(bsz, seq, SSM_WIDTH)


def _s5_out_body(y_ref, u_ref, d_ref, wglu_ref, bglu_ref, wout_ref, o_ref):
    f32 = jnp.float32
    y = jax.nn.gelu(y_ref[...] + d_ref[...] * u_ref[...])
    gate = jnp.dot(y.astype(MXU_DTYPE), wglu_ref[...], preferred_element_type=f32) + bglu_ref[...]
    y = y * jax.nn.sigmoid(gate)
    o_ref[...] = jnp.dot(y.astype(MXU_DTYPE), wout_ref[...], preferred_element_type=f32)


def s5_output(y, u, d_skip, w_glu, b_glu, w_out, *, tm=1024):
    m, c = y.shape
    tm = min(tm, m)
    row = lambda i: (i, 0)
    full = lambda i: (0, 0)
    return pl.pallas_call(
        _s5_out_body,
        grid=(m // tm,),
        in_specs=[pl.BlockSpec((tm, c), row), pl.BlockSpec((tm, c), row), pl.BlockSpec((1, c), full),
                  pl.BlockSpec((c, c), full), pl.BlockSpec((1, c), full), pl.BlockSpec((c, D_MODEL), full)],
        out_specs=pl.BlockSpec((tm, D_MODEL), row),
        out_shape=jax.ShapeDtypeStruct((m, D_MODEL), jnp.float32),
        compiler_params=pltpu.CompilerParams(
            dimension_semantics=("parallel",), vmem_limit_bytes=VMEM_LIMIT_BYTES),
        name="s5_output",
    )(y, u, d_skip.reshape(1, c), w_glu.astype(MXU_DTYPE), b_glu.reshape(1, c), w_out.astype(MXU_DTYPE))


def s5_branch(u, log_dt, lam_re, lam_im, b_re, b_im, c_re, c_im, d_skip, w_glu, b_glu, w_out):
    bsz, seq, _ = u.shape
    tw, w_state_out, decay = s5_operators(log_dt, lam_re, lam_im, b_re, b_im, c_re, c_im)
    y = s5_scan(u, tw, w_state_out, decay)
    out = s5_output(y.reshape(bsz * seq, SSM_WIDTH), u.reshape(bsz * seq, SSM_WIDTH),
                    d_skip, w_glu, b_glu, w_out)
    return out.reshape(bsz, seq, D_MODEL)


HG_CHUNK = 128
HG_LEVELS = 7
HG_KDIM = HGRN_HEADS * HGRN_DK


def _hgrn_segment_sums():
    c = HG_CHUNK
    t = np.arange(c)[:, None]
    u = np.arange(c)[None, :]
    blocks = []
    for lvl in range(1, HG_LEVELS + 1):
        m = (t >> lvl << lvl) + (1 << (lvl - 1)) - 1
        right = ((t >> (lvl - 1)) & 1) == 1
        blocks.append(np.where(right, (u > m) & (u <= t), (u > t) & (u <= m)))
    blocks.append(u <= t)
    blocks.append(u > t)
    return np.concatenate(blocks, axis=0).astype(np.float32)


def _hgrn_body(q_ref, z_ref, v_ref, g_ref, seg_ref, lb_ref, ng_ref, hmean_ref, wout_ref, o_ref, st_ref):
    f32 = jnp.float32
    c = HG_CHUNK

    @pl.when(pl.program_id(1) == 0)
    def _():
        st_ref[...] = jnp.zeros_like(st_ref)

    q = q_ref[0]
    z = z_ref[0]
    v = v_ref[0]
    lb = lb_ref[...]
    f = lb + (1.0 - lb) * jax.nn.sigmoid(z)
    logf = jnp.log(jnp.maximum(f, F_MIN))
    kin = (1.0 - lb) * jax.nn.sigmoid(-z)

    p1 = logf.astype(MXU_DTYPE)
    r1 = logf - p1.astype(f32)
    p2 = r1.astype(MXU_DTYPE)
    p3 = (r1 - p2.astype(f32)).astype(MXU_DTYPE)
    seg = seg_ref[...]
    sums = (jnp.dot(seg, p1, preferred_element_type=f32) + jnp.dot(seg, p2, preferred_element_type=f32)
            + jnp.dot(seg, p3, preferred_element_type=f32))

    lane_head = lax.broadcasted_iota(jnp.int32, (c, HG_KDIM), 1) // HGRN_DK
    tok = lax.broadcasted_iota(jnp.int32, (c, HG_KDIM), 0)
    row_t = lax.broadcasted_iota(jnp.int32, (HGRN_HEADS * c, c), 0) % c
    col_s = lax.broadcasted_iota(jnp.int32, (HGRN_HEADS * c, c), 1)

    def per_head_rows(x):
        return jnp.concatenate([jnp.where(lane_head == h, x, 0.0) for h in range(HGRN_HEADS)],
                               axis=0).astype(MXU_DTYPE)

    def scores(ql, kl):
        return lax.dot_general(per_head_rows(ql), kl.astype(MXU_DTYPE), (((1,), (1,)), ((), ())),
                               preferred_element_type=f32)

    att = jnp.where(row_t == col_s, scores(q, kin), 0.0)
    for lvl in range(1, HG_LEVELS + 1):
        decay = jnp.exp(sums[(lvl - 1) * c:lvl * c])
        right = ((tok >> (lvl - 1)) & 1) == 1
        a = scores(jnp.where(right, q * decay, 0.0), jnp.where(right, 0.0, kin * decay))
        att = att + jnp.where((row_t >> lvl) == (col_s >> lvl), a, 0.0)

    b = sums[HG_LEVELS * c:(HG_LEVELS + 1) * c]
    tail = sums[(HG_LEVELS + 1) * c:(HG_LEVELS + 2) * c]
    v_m = v.astype(MXU_DTYPE)
    st = st_ref[...]
    o = lax.dot_general((q * jnp.exp(b)).astype(MXU_DTYPE), st.astype(MXU_DTYPE),
                        (((1,), (1,)), ((), ())), preferred_element_type=f32)
    for h in range(HGRN_HEADS):
        o_h = jnp.dot(att[h * c:(h + 1) * c].astype(MXU_DTYPE), v_m, preferred_element_type=f32)
        o = o + jnp.where(lane_head == h, o_h, 0.0)

    kv = jnp.dot(v.T.astype(MXU_DTYPE), (kin * jnp.exp(tail)).astype(MXU_DTYPE), preferred_element_type=f32)
    sr = lax.broadcasted_iota(jnp.int32, st.shape, 0) // HGRN_DV
    sc = lax.broadcasted_iota(jnp.int32, st.shape, 1) // HGRN_DK
    st_ref[...] = st * jnp.exp(b[c - 1:c, :]) + jnp.where(sr == sc, kv, 0.0)

    o2 = o * o
    o2_hi = o2.astype(MXU_DTYPE)
    o2_lo = (o2 - o2_hi.astype(f32)).astype(MXU_DTYPE)
    ms = (jnp.dot(o2_hi, hmean_ref[...], preferred_element_type=f32)
          + jnp.dot(o2_lo, hmean_ref[...], preferred_element_type=f32))
    g = g_ref[0]
    out = o * lax.rsqrt(ms + RMS_EPS) * ng_ref[...] * (g * jax.nn.sigmoid(g))
    o_ref[0] = jnp.dot(out.astype(MXU_DTYPE), wout_ref[...], preferred_element_type=f32)


def hgrn2_branch(hg_in, lower_bound, norm_g, w_out):
    bsz, seq, _ = hg_in.shape
    assert seq % HG_CHUNK == 0 and HGRN_DK == HGRN_DV and HG_IN_WIDTH == 4 * HG_KDIM
    seg = jnp.asarray(_hgrn_segment_sums(), MXU_DTYPE)
    head_mean = jnp.asarray(np.kron(np.eye(HGRN_HEADS), np.full((HGRN_DV, HGRN_DV), 1.0 / HGRN_DV)), MXU_DTYPE)
    tok = lambda b, i: (b, i, 0)
    full = lambda b, i: (0, 0)
    part = lambda k: pl.BlockSpec((1, HG_CHUNK, HG_KDIM), lambda b, i: (b, i, k))
    return pl.pallas_call(
        _hgrn_body,
        grid=(bsz, seq // HG_CHUNK),
        in_specs=[part(0), part(1), part(2), part(3),
                  pl.BlockSpec(seg.shape, full), pl.BlockSpec((1, HG_KDIM), full),
                  pl.BlockSpec((1, HGRN_WIDTH), full), pl.BlockSpec(head_mean.shape, full),
                  pl.BlockSpec((HGRN_WIDTH, D_MODEL), full)],
        out_specs=pl.BlockSpec((1, HG_CHUNK, D_MODEL), tok),
        out_shape=jax.ShapeDtypeStruct((bsz, seq, D_MODEL), jnp.float32),
        scratch_shapes=[pltpu.VMEM((HGRN_WIDTH, HG_KDIM), jnp.float32)],
        compiler_params=pltpu.CompilerParams(
            dimension_semantics=("parallel", "arbitrary"), vmem_limit_bytes=VMEM_LIMIT_BYTES),
        name="hgrn2",
    )(hg_in, hg_in, hg_in, hg_in, seg, lower_bound.reshape(1, HG_KDIM).astype(jnp.float32),
      jnp.tile(norm_g.astype(jnp.float32), HGRN_HEADS).reshape(1, HGRN_WIDTH), head_mean,
      w_out.astype(MXU_DTYPE))


ATT_QK_DIM = LANE
DSA_TILE = 512


def _rms_rows(x, g):
    return x * lax.rsqrt(jnp.mean(x * x, axis=-1, keepdims=True) + RMS_EPS) * g


def _dsa_project_body(x_ref, gq_ref, gkv_ref, wq_ref, wqi_ref, wkt_ref, wv_ref,
                      q_ref, qi_ref, kt_ref, v_ref, kidt_ref):
    f32 = jnp.float32
    tm = x_ref.shape[1]
    x = x_ref[0]
    cq = _rms_rows(x[:, :ATT_Q_RANK], gq_ref[...]).astype(MXU_DTYPE)
    ckv = _rms_rows(x[:, ATT_Q_RANK:ATT_Q_RANK + ATT_KV_RANK], gkv_ref[...])
    q_all = jnp.dot(cq, wq_ref[...], preferred_element_type=f32)
    qi_all = jnp.dot(cq, wqi_ref[...], preferred_element_type=f32)
    v_all = jnp.dot(ckv.astype(MXU_DTYPE), wv_ref[...], preferred_element_type=f32)
    kt_all = jnp.dot(wkt_ref[...], ckv.T.astype(MXU_DTYPE), preferred_element_type=f32)

    lane = lax.broadcasted_iota(jnp.int32, (tm, LANE), 1)
    t_loc = lax.broadcasted_iota(jnp.int32, (tm, LANE), 0)
    t_even = (t_loc // 2 * 2).astype(f32)
    t_odd = (t_loc % 2).astype(f32)
    sub = lax.broadcasted_iota(jnp.int32, (LANE, tm), 0)
    s_loc = lax.broadcasted_iota(jnp.int32, (LANE, tm), 1)
    k_rows = jnp.where(sub < ATT_HEAD_DIM + 2, 1.0,
                       jnp.where(sub == ATT_HEAD_DIM + 2, (s_loc // 2 * 2).astype(f32), (s_loc % 2).astype(f32)))
    for h in range(ATT_HEADS):
        slope = 2.0 ** (-8.0 * (h + 1) / ATT_HEADS)
        q_cols = jnp.where(lane == ATT_HEAD_DIM, -slope * t_even,
                           jnp.where(lane == ATT_HEAD_DIM + 1, -slope * t_odd, slope))
        q_h = q_all[:, h * LANE:(h + 1) * LANE]
        q_ref[0, h] = jnp.where(lane < ATT_HEAD_DIM, q_h,
                                jnp.where(lane < ATT_HEAD_DIM + 4, q_cols, 0.0)).astype(q_ref.dtype)
        k_h = kt_all[h * LANE:(h + 1) * LANE, :]
        kt_ref[0, h] = jnp.where(sub < ATT_HEAD_DIM, k_h,
                                 jnp.where(sub < ATT_HEAD_DIM + 4, k_rows, 0.0)).astype(kt_ref.dtype)
        v_h = v_all[:, h * LANE:(h + 1) * LANE]
        v_ref[0, h] = jnp.where(lane == ATT_HEAD_DIM, 1.0, v_h).astype(v_ref.dtype)
    for h in range(IDX_HEADS):
        qi_ref[0, h] = qi_all[:, h * LANE:(h + 1) * LANE].astype(qi_ref.dtype)
    kidt_ref[0] = x[:, ATT_IDX_BLOCK * LANE:(ATT_IDX_BLOCK + 1) * LANE].T.astype(kidt_ref.dtype)


def _head_padded(w, n_heads, dim):
    k = w.shape[0]
    return jnp.pad(w.reshape(k, n_heads, dim), ((0, 0), (0, 0), (0, LANE - dim))).reshape(k, n_heads * LANE)


def dsa_project(att_in, q_norm_g, kv_norm_g, w_uq, w_qidx, w_ukv):
    bsz, seq, _ = att_in.shape
    tm = DSA_TILE
    assert seq % tm == 0 and 8 % ATT_HEADS == 0 and tm <= 512
    w_q = _head_padded(w_uq * ATT_HEAD_DIM ** -0.5, ATT_HEADS, ATT_HEAD_DIM).astype(MXU_DTYPE)
    w_qi = _head_padded(w_qidx * IDX_DIM ** -0.5, IDX_HEADS, IDX_DIM).astype(MXU_DTYPE)
    w_kt = _head_padded(w_ukv[:, :ATT_WIDTH], ATT_HEADS, ATT_HEAD_DIM).T.astype(MXU_DTYPE)
    w_v = _head_padded(w_ukv[:, ATT_WIDTH:], ATT_HEADS, ATT_HEAD_DIM).astype(MXU_DTYPE)
    tok = lambda b, i: (b, 0, i, 0)
    full = lambda b, i: (0, 0)
    dt = MXU_DTYPE
    return pl.pallas_call(
        _dsa_project_body,
        grid=(bsz, seq // tm),
        in_specs=[pl.BlockSpec((1, tm, ATT_IN_WIDTH), lambda b, i: (b, i, 0)),
                  pl.BlockSpec((1, ATT_Q_RANK), full), pl.BlockSpec((1, ATT_KV_RANK), full),
                  pl.BlockSpec(w_q.shape, full), pl.BlockSpec(w_qi.shape, full),
                  pl.BlockSpec(w_kt.shape, full), pl.BlockSpec(w_v.shape, full)],
        out_specs=[pl.BlockSpec((1, ATT_HEADS, tm, LANE), tok),
                   pl.BlockSpec((1, IDX_HEADS, tm, LANE), tok),
                   pl.BlockSpec((1, ATT_HEADS, LANE, tm), lambda b, i: (b, 0, 0, i)),
                   pl.BlockSpec((1, ATT_HEADS, tm, LANE), tok),
                   pl.BlockSpec((1, LANE, tm), lambda b, i: (b, 0, i))],
        out_shape=[jax.ShapeDtypeStruct((bsz, ATT_HEADS, seq, LANE), dt),
                   jax.ShapeDtypeStruct((bsz, IDX_HEADS, seq, LANE), dt),
                   jax.ShapeDtypeStruct((bsz, ATT_HEADS, LANE, seq), dt),
                   jax.ShapeDtypeStruct((bsz, ATT_HEADS, seq, LANE), dt),
                   jax.ShapeDtypeStruct((bsz, LANE, seq), dt)],
        compiler_params=pltpu.CompilerParams(
            dimension_semantics=("parallel", "parallel"), vmem_limit_bytes=VMEM_LIMIT_BYTES),
        name="dsa_project",
    )(att_in, q_norm_g.reshape(1, ATT_Q_RANK), kv_norm_g.reshape(1, ATT_KV_RANK), w_q, w_qi, w_kt, w_v)


_INT_MIN = -2 ** 31
_MASK_KEY = int(np.float32(MASK_VALUE).view(np.int32)) ^ 0x7FFFFFFF
_SEARCH_ROWS = 128
_ATT_ROWS = 32
_SEARCH_LANES = 512
_LIST_DEPTH = 12
_LIST_ROWS = 16
_LIST_MIN_BLOCKS = 4


def _key_to_score(key):
    return lax.bitcast_convert_type(jnp.where(key < 0, key ^ 0x7FFFFFFF, key), jnp.float32)


def _dsa_body(qi_ref, kj_ref, qidx_ref, w_ref, kidt_ref, q_ref, kt_ref, v_ref, tri_ref, wout_ref,
              o_ref, keys_ref, thr_ref, need_ref, carry_ref, m_ref, acc_ref, bias_ref, s_ref, p_ref,
              alpha_ref, cand_ref, *, tq, tk, seq, n_sel):
    f32 = jnp.float32
    p_id = pl.program_id(1)
    i = qi_ref[p_id]
    j = kj_ref[p_id]
    row_local = lax.broadcasted_iota(jnp.int32, (tq, tk), 0)
    col_local = lax.broadcasted_iota(jnp.int32, (tq, tk), 1)

    @pl.when(j == 0)
    def _select():
        def score_block(jj, carry):
            off = pl.multiple_of(jj * tk, tk)
            kb = kidt_ref[0, :, pl.ds(off, tk)]
            sc = jnp.zeros((tq, tk), f32)
            for h in range(IDX_HEADS):
                s = jnp.dot(qidx_ref[0, h], kb, preferred_element_type=f32)
                w_h = w_ref[0, :, IDX_DIM + h:IDX_DIM + h + 1] * IDX_HEADS ** -0.5
                sc = sc + jnp.maximum(s, 0.0) * w_h
            causal = (col_local + jj * tk) <= (row_local + i * tq)
            sc = jnp.where(causal, sc, MASK_VALUE)
            keys_ref[:, pl.ds(off, tk)] = jnp.where(sc == 0.0, 0.0, sc)
            return carry

        lax.fori_loop(0, i + 1, score_block, 0)

        n_blocks = i + 1
        n_masked_tail = (seq - n_blocks * tk).astype(f32)

        def row_group(r, carry):
            r0 = pl.multiple_of(r * _SEARCH_ROWS, _SEARCH_ROWS)

            def count_ge(ref, n_iter, cand):
                cand_b = jnp.broadcast_to(_key_to_score(cand), (_SEARCH_ROWS, LANE))

                def chunk(c, acc):
                    base = pl.multiple_of(c * _SEARCH_LANES, _SEARCH_LANES)
                    for u in range(_SEARCH_LANES // LANE):
                        kk = ref[pl.ds(r0, _SEARCH_ROWS), pl.ds(base + u * LANE, LANE)]
                        acc = acc + jnp.where(kk >= cand_b, 1.0, 0.0)
                    return acc

                acc = lax.fori_loop(0, n_iter, chunk, jnp.zeros((_SEARCH_ROWS, LANE), f32))
                cnt = jnp.sum(acc, axis=1, keepdims=True)
                return cnt + jnp.where(cand <= _MASK_KEY, n_masked_tail, 0.0)

            def kth_largest_key(count):
                zero = jnp.zeros((_SEARCH_ROWS, 1), jnp.int32)
                v0 = jnp.where(count(zero) >= n_sel, zero, zero + _INT_MIN)

                def bit_step(b, v):
                    cand = v | jnp.left_shift(jnp.int32(1), 30 - b)
                    return jnp.where(count(cand) >= n_sel, cand, v)

                return lax.fori_loop(0, 31, bit_step, v0)

            count_all = functools.partial(count_ge, keys_ref, n_blocks * (tk // _SEARCH_LANES))

            def store(v, n_above):
                thr_ref[pl.ds(r0, _SEARCH_ROWS), :] = _key_to_score(v)
                need_ref[pl.ds(r0, _SEARCH_ROWS), :] = n_sel - n_above

            def search_all_keys():
                v = kth_largest_key(count_all)
                store(v, count_all(v + 1))

            @pl.when(n_blocks < _LIST_MIN_BLOCKS)
            def _():
                search_all_keys()

            @pl.when(n_blocks >= _LIST_MIN_BLOCKS)
            def _():
                def shortlist(sub, carry):
                    rr = pl.multiple_of(r0 + sub * _LIST_ROWS, _LIST_ROWS)

                    def insert_block(c, tops):
                        base = pl.multiple_of(c * tk, tk)
                        for u in range(tk // LANE):
                            x = keys_ref[pl.ds(rr, _LIST_ROWS), pl.ds(base + u * LANE, LANE)]
                            new = []
                            for t in range(_LIST_DEPTH):
                                new.append(jnp.maximum(tops[t], x))
                                x = jnp.minimum(tops[t], x)
                            tops = tuple(new)
                        return tops

                    tops = lax.fori_loop(0, n_blocks, insert_block,
                                         tuple(jnp.full((_LIST_ROWS, LANE), -jnp.inf, f32)
                                               for _ in range(_LIST_DEPTH)))
                    for t in range(_LIST_DEPTH):
                        cand_ref[pl.ds(rr, _LIST_ROWS), t * LANE:(t + 1) * LANE] = tops[t]
                    return carry

                lax.fori_loop(0, _SEARCH_ROWS // _LIST_ROWS, shortlist, 0)
                v = kth_largest_key(functools.partial(count_ge, cand_ref,
                                                      _LIST_DEPTH * LANE // _SEARCH_LANES))
                n_at_least = count_all(v)
                n_above = count_all(v + 1)
                store(v, n_above)
                exact = (n_above < n_sel) & (n_at_least >= n_sel)
                pl.when(jnp.min(jnp.where(exact, 1.0, 0.0)) < 0.5)(search_all_keys)

            return carry

        lax.fori_loop(0, tq // _SEARCH_ROWS, row_group, 0)
        carry_ref[...] = jnp.zeros_like(carry_ref)
        m_ref[...] = jnp.full_like(m_ref, MASK_VALUE)
        acc_ref[...] = jnp.zeros_like(acc_ref)

    keys_blk = keys_ref[:, pl.ds(pl.multiple_of(j * tk, tk), tk)]
    thr = thr_ref[...]
    eq = keys_blk == thr
    tie_rank = carry_ref[...] + jnp.dot(jnp.where(eq, 1.0, 0.0).astype(MXU_DTYPE), tri_ref[...],
                                         preferred_element_type=f32)
    carry_ref[...] = tie_rank[:, tk - 1:tk]
    tie_bias = jnp.where(eq, jnp.where(tie_rank <= need_ref[...], 0.0, MASK_VALUE), MASK_VALUE)
    bias = jnp.where(keys_blk > thr, 0.0, tie_bias)
    causal = (row_local + i * tq) >= (col_local + j * tk)
    bias_ref[...] = jnp.where(causal, bias, MASK_VALUE)

    def logits(h, slot):
        s_ref[slot] = jnp.dot(q_ref[0, h], kt_ref[0, h], preferred_element_type=f32)

    def softmax(h, slot):
        slope_tk = jnp.right_shift(tk, h + 1).astype(f32)
        for r in range(0, tq, _ATT_ROWS):
            rows = pl.ds(r, _ATT_ROWS)
            s = s_ref[slot, rows, :] + bias_ref[rows, :]
            m_old = m_ref[h, rows, :] - slope_tk
            m_new = jnp.maximum(m_old, jnp.max(s, axis=1, keepdims=True))
            p_ref[slot, rows, :] = jnp.exp(s - m_new).astype(MXU_DTYPE)
            alpha_ref[slot, rows, :] = jnp.exp(m_old - m_new)
            m_ref[h, rows, :] = m_new

    def weighted_values(h, slot):
        acc_ref[h] = alpha_ref[slot] * acc_ref[h] + jnp.dot(p_ref[slot], v_ref[0, h],
                                                             preferred_element_type=f32)

    for t in range(ATT_HEADS + 2):
        if t < ATT_HEADS:
            logits(t, t % 2)
        if 1 <= t <= ATT_HEADS:
            softmax(t - 1, (t - 1) % 2)
        if t >= 2:
            weighted_values(t - 2, t % 2)

    @pl.when(j == i)
    def _finish():
        heads = []
        for h in range(ATT_HEADS):
            a = acc_ref[h]
            heads.append((a[:, :ATT_HEAD_DIM] / a[:, ATT_HEAD_DIM:ATT_HEAD_DIM + 1]).astype(MXU_DTYPE))
        o_ref[0] = jnp.dot(jnp.concatenate(heads, axis=1), wout_ref[...], preferred_element_type=f32)


def dsa_attention(att_in, q_idx, kid_t, q, k_t, v_aug, w_out):
    bsz, _, seq, _ = q.shape
    tq = tk = DSA_TILE
    n_sel = min(TOPK_MAX, seq // 4)
    nq = seq // tq
    assert seq % tq == 0 and tk % _SEARCH_LANES == 0 and tq % _SEARCH_ROWS == 0
    pairs = [(a, b) for a in range(nq) for b in range(a + 1)]
    qi = jnp.asarray([a for a, _ in pairs], jnp.int32)
    kj = jnp.asarray([b for _, b in pairs], jnp.int32)
    tri = jnp.triu(jnp.ones((tk, tk), MXU_DTYPE))
    grid_spec = pltpu.PrefetchScalarGridSpec(
        num_scalar_prefetch=2,
        grid=(bsz, len(pairs)),
        in_specs=[
            pl.BlockSpec((1, IDX_HEADS, tq, LANE), lambda b, p, qi, kj: (b, 0, qi[p], 0)),
            pl.BlockSpec((1, tq, LANE), lambda b, p, qi, kj: (b, qi[p], ATT_IDX_BLOCK)),
            pl.BlockSpec((1, LANE, seq), lambda b, p, qi, kj: (b, 0, 0)),
            pl.BlockSpec((1, ATT_HEADS, tq, ATT_QK_DIM), lambda b, p, qi, kj: (b, 0, qi[p], 0)),
            pl.BlockSpec((1, ATT_HEADS, ATT_QK_DIM, tk), lambda b, p, qi, kj: (b, 0, 0, kj[p])),
            pl.BlockSpec((1, ATT_HEADS, tk, LANE), lambda b, p, qi, kj: (b, 0, kj[p], 0)),
            pl.BlockSpec((tk, tk), lambda b, p, qi, kj: (0, 0)),
            pl.BlockSpec((ATT_WIDTH, D_MODEL), lambda b, p, qi, kj: (0, 0)),
        ],
        out_specs=pl.BlockSpec((1, tq, D_MODEL), lambda b, p, qi, kj: (b, qi[p], 0)),
        scratch_shapes=[
            pltpu.VMEM((tq, seq), jnp.float32),
            pltpu.VMEM((tq, 1), jnp.float32),
            pltpu.VMEM((tq, 1), jnp.float32),
            pltpu.VMEM((tq, 1), jnp.float32),
            pltpu.VMEM((ATT_HEADS, tq, 1), jnp.float32),
            pltpu.VMEM((ATT_HEADS, tq, LANE), jnp.float32),
            pltpu.VMEM((tq, tk), jnp.float32),
            pltpu.VMEM((2, tq, tk), jnp.float32),
            pltpu.VMEM((2, tq, tk), MXU_DTYPE),
            pltpu.VMEM((2, tq, 1), jnp.float32),
            pltpu.VMEM((tq, _LIST_DEPTH * LANE), jnp.float32),
        ])
    return pl.pallas_call(
        functools.partial(_dsa_body, tq=tq, tk=tk, seq=seq, n_sel=n_sel),
        grid_spec=grid_spec,
        out_shape=jax.ShapeDtypeStruct((bsz, seq, D_MODEL), jnp.float32),
        compiler_params=pltpu.CompilerParams(
            dimension_semantics=("parallel", "arbitrary"), vmem_limit_bytes=VMEM_LIMIT_BYTES),
        name="dsa_attention",
    )(qi, kj, q_idx, att_in, kid_t, q, k_t, v_aug, tri, w_out)


def dsa_branch(att_in, q_norm_g, kv_norm_g, w_uq, w_qidx, w_ukv, w_out):
    q, q_idx, k_t, v_aug, kid_t = dsa_project(att_in, q_norm_g, kv_norm_g, w_uq, w_qidx, w_ukv)
    return dsa_attention(att_in, q_idx, kid_t, q, k_t, v_aug, w_out.astype(MXU_DTYPE))


MOE_TOKENS = 1024
MOE_ROW_CLASSES = (256, 288, 320, 512, 1024)
MOE_FFN_TILES = (896, 512, 256)
MOE_VMEM_LIMIT_BYTES = 56 * 1024 * 1024


def _moe_body(cnt_ref, h_ref, gate_ref, gatet_ref, tri_ref, wg_ref, wu_ref, wd_ref, g_ref, b_ref, o_ref,
              xb_ref, gather_ref, scatter_ref, xc_ref, yc_ref, acc_ref):
    f32 = jnp.float32
    tm = MOE_TOKENS
    i = pl.program_id(0)
    e = pl.program_id(1)
    f = pl.program_id(2)
    last_f = pl.num_programs(2) - 1

    @pl.when((e == 0) & (f == 0))
    def _():
        xb_ref[...] = h_ref[...].astype(MXU_DTYPE)
        acc_ref[...] = jnp.zeros_like(acc_ref)

    lane = lax.broadcasted_iota(jnp.int32, gate_ref.shape, 1)
    gate_col = jnp.sum(jnp.where(lane == e, gate_ref[...], 0.0), axis=-1, keepdims=True)

    def routed(size):
        def run():
            @pl.when(f == 0)
            def _():
                active_row = jnp.where(gatet_ref[pl.ds(e, 1), :] != 0.0, 1.0, 0.0)
                rank_row = jnp.dot(jnp.broadcast_to(active_row, (8, tm)).astype(MXU_DTYPE), tri_ref[...],
                                   preferred_element_type=f32)[0:1]
                slot = lax.broadcasted_iota(jnp.int32, (size, tm), 0).astype(f32)
                gather = jnp.where(slot == rank_row, active_row, 0.0)
                gather_ref[0:size, :] = gather.astype(MXU_DTYPE)
                rank_col = jnp.broadcast_to(rank_row, (LANE, tm)).T[:, 0:1]
                slot_l = lax.broadcasted_iota(jnp.int32, (tm, size), 1).astype(f32)
                scatter = jnp.where(slot_l == rank_col, jnp.where(gate_col != 0.0, 1.0, 0.0), 0.0)
                scatter_ref[:, 0:size] = scatter.astype(MXU_DTYPE)
                xc_ref[0:size, :] = jnp.dot(gather.astype(MXU_DTYPE), xb_ref[...],
                                            preferred_element_type=f32).astype(MXU_DTYPE)
                yc_ref[0:size, :] = jnp.zeros((size, D_MODEL), f32)

            xc = xc_ref[0:size, :]
            a = jnp.dot(xc, wg_ref[0], preferred_element_type=f32)
            u = jnp.dot(xc, wu_ref[0], preferred_element_type=f32)
            act = a * jax.nn.sigmoid(a) * u
            yc_ref[0:size, :] += jnp.dot(act.astype(MXU_DTYPE), wd_ref[0], preferred_element_type=f32)

            @pl.when(f == last_f)
            def _():
                back = jnp.dot(scatter_ref[:, 0:size], yc_ref[0:size, :].astype(MXU_DTYPE),
                               preferred_element_type=f32)
                acc_ref[...] += gate_col * back
        return run

    count = cnt_ref[i * N_EXPERTS + e]
    size_class = sum((count > s).astype(jnp.int32) for s in MOE_ROW_CLASSES[:-1])
    for k, size in enumerate(MOE_ROW_CLASSES):
        pl.when((size_class == k) & (count > 0))(routed(size))

    @pl.when((e == pl.num_programs(1) - 1) & (f == last_f))
    def _():
        o_ref[...] = _layer_norm_rows(DEEPNORM_ALPHA * h_ref[...] + acc_ref[...], g_ref[...], b_ref[...])


def moe_experts_norm(h, routing, wg, wu, wd, ln_g, ln_b):
    gates, gates_t, counts = routing
    m, d = h.shape
    n_experts, _, f_dim = wg.shape
    tm = MOE_TOKENS
    tf = next(t for t in MOE_FFN_TILES if f_dim % t == 0)
    assert m % tm == 0 and MOE_ROW_CLASSES[-1] == tm
    n_tiles = m // tm
    tri = jnp.triu(jnp.ones((tm, tm), MXU_DTYPE), k=1)
    row = lambda i, e, f, c: (i, 0)
    full = lambda i, e, f, c: (0, 0)
    grid_spec = pltpu.PrefetchScalarGridSpec(
        num_scalar_prefetch=1,
        grid=(n_tiles, n_experts, f_dim // tf),
        in_specs=[pl.BlockSpec((tm, d), row), pl.BlockSpec((tm, LANE), row),
                  pl.BlockSpec((n_experts, tm), lambda i, e, f, c: (0, i)),
                  pl.BlockSpec((tm, tm), full),
                  pl.BlockSpec((1, d, tf), lambda i, e, f, c: (e, 0, f)),
                  pl.BlockSpec((1, d, tf), lambda i, e, f, c: (e, 0, f)),
                  pl.BlockSpec((1, tf, d), lambda i, e, f, c: (e, f, 0)),
                  pl.BlockSpec((1, d), full), pl.BlockSpec((1, d), full)],
        out_specs=pl.BlockSpec((tm, d), row),
        scratch_shapes=[pltpu.VMEM((tm, d), MXU_DTYPE),
                        pltpu.VMEM((tm, tm), MXU_DTYPE),
                        pltpu.VMEM((tm, tm), MXU_DTYPE),
                        pltpu.VMEM((tm, d), MXU_DTYPE),
                        pltpu.VMEM((tm, d), jnp.float32),
                        pltpu.VMEM((tm, d), jnp.float32)])
    return pl.pallas_call(
        _moe_body,
        grid_spec=grid_spec,
        out_shape=jax.ShapeDtypeStruct((m, d), jnp.float32),
        compiler_params=pltpu.CompilerParams(
            dimension_semantics=("parallel", "arbitrary", "arbitrary"),
            vmem_limit_bytes=MOE_VMEM_LIMIT_BYTES),
        name="moe_experts_norm",
    )(counts, h, gates, gates_t, tri, wg, wu, wd, ln_g.reshape(1, d), ln_b.reshape(1, d))


def _router_body(h_ref, r_ref, gate_ref, gatet_ref, cnt_ref):
    f32 = jnp.float32
    logits = jnp.dot(h_ref[...], r_ref[...], preferred_element_type=f32)
    lane = lax.broadcasted_iota(jnp.int32, logits.shape, 1)
    logits = jnp.where(lane < N_EXPERTS, logits, -jnp.inf)
    top1 = jnp.max(logits, axis=1, keepdims=True)
    idx1 = jnp.min(jnp.where(logits == top1, lane, LANE), axis=1, keepdims=True)
    rest = jnp.where(lane == idx1, -jnp.inf, logits)
    top2 = jnp.max(rest, axis=1, keepdims=True)
    idx2 = jnp.min(jnp.where(rest == top2, lane, LANE), axis=1, keepdims=True)
    e2 = jnp.exp(top2 - top1)
    gates = jnp.where(lane == idx1, 1.0 / (1.0 + e2), jnp.where(lane == idx2, e2 / (1.0 + e2), 0.0))
    gate_ref[...] = gates
    gatet_ref[...] = gates.T[0:N_EXPERTS, :]
    cnt_ref[0] = jnp.sum(jnp.where(gates != 0.0, 1.0, 0.0), axis=0, keepdims=True)


def router_gates(h, router):
    m, d = h.shape
    tm = MOE_TOKENS
    assert TOP_K == 2 and m % tm == 0
    r_pad = jnp.pad(router.astype(jnp.float32), ((0, 0), (0, LANE - N_EXPERTS)))
    gates, gates_t, counts = pl.pallas_call(
        _router_body,
        grid=(m // tm,),
        in_specs=[pl.BlockSpec((tm, d), lambda i: (i, 0)), pl.BlockSpec((d, LANE), lambda i: (0, 0))],
        out_specs=[pl.BlockSpec((tm, LANE), lambda i: (i, 0)), pl.BlockSpec((N_EXPERTS, tm), lambda i: (0, i)),
                   pl.BlockSpec((1, 1, LANE), lambda i: (i, 0, 0))],
        out_shape=[jax.ShapeDtypeStruct((m, LANE), jnp.float32),
                   jax.ShapeDtypeStruct((N_EXPERTS, m), jnp.float32),
                   jax.ShapeDtypeStruct((m // tm, 1, LANE), jnp.float32)],
        compiler_params=pltpu.CompilerParams(
            dimension_semantics=("parallel",), vmem_limit_bytes=VMEM_LIMIT_BYTES),
        name="router_gates",
    )(h, r_pad)
    return gates, gates_t, counts[:, 0, :N_EXPERTS].astype(jnp.int32).reshape(-1)


def kernel(x, w_in, ssm_log_dt, ssm_lambda_re, ssm_lambda_im, ssm_b_re, ssm_b_im, ssm_c_re, ssm_c_im,
           ssm_d, ssm_w_glu, ssm_b_glu, ssm_w_out, hgrn_lb_logits, hgrn_norm_g, hgrn_w_out,
           attn_q_norm_g, attn_kv_norm_g, attn_w_uq, attn_w_qidx, attn_w_ukv, attn_w_out, w_o,
           ln_g, ln_b, ffn_w_gate, ffn_w_up, ffn_w_down, moe_router, moe_w_gate, moe_w_up, moe_w_down):
    bsz, seq, d = x.shape
    m = bsz * seq
    bf16 = MXU_DTYPE
    assert MIX_IN_USED + N_BRANCHES * D_MODEL == N_IN
    lb_soft = jax.nn.softmax(hgrn_lb_logits.astype(jnp.float32), axis=0)
    lower_bounds = jnp.concatenate([jnp.zeros_like(lb_soft[:1]), jnp.cumsum(lb_soft[1:], axis=0)], axis=0)
    h = x.reshape(m, d)
    for l in range(DEPTH):
        w_mix = jnp.pad(w_in[l][:, :MIX_IN_USED], ((0, 0), (0, MIX_IN_WIDTH - MIX_IN_USED))).astype(bf16)
        u, hg_in, att_in = in_proj(h, w_mix)
        y_ssm = s5_branch(u.reshape(bsz, seq, SSM_WIDTH), ssm_log_dt[l], ssm_lambda_re[l], ssm_lambda_im[l],
                          ssm_b_re[l], ssm_b_im[l], ssm_c_re[l], ssm_c_im[l], ssm_d[l], ssm_w_glu[l],
                          ssm_b_glu[l], ssm_w_out[l])
        y_hg = hgrn2_branch(hg_in.reshape(bsz, seq, HG_IN_WIDTH), lower_bounds[l], hgrn_norm_g[l],
                            hgrn_w_out[l])
        y_att = dsa_branch(att_in.reshape(bsz, seq, ATT_IN_WIDTH), attn_q_norm_g[l], attn_kv_norm_g[l],
                           attn_w_uq[l], attn_w_qidx[l], attn_w_ukv[l], attn_w_out[l])
        h = merge_project_norm(h, y_ssm.reshape(m, d), y_hg.reshape(m, d), y_att.reshape(m, d),
                               w_in[l][:, MIX_IN_USED:].astype(bf16), w_o[l].astype(bf16),
                               ln_g[l, 0], ln_b[l, 0])
        if l % 2 == 0:
            h = swiglu_norm(h, ffn_w_gate[l // 2].astype(bf16), ffn_w_up[l // 2].astype(bf16),
                            ffn_w_down[l // 2].astype(bf16), ln_g[l, 1], ln_b[l, 1])
        else:
            gate_w = router_gates(h, moe_router[l // 2])
            h = moe_experts_norm(h, gate_w, moe_w_gate[l // 2].astype(bf16),
                                 moe_w_up[l // 2].astype(bf16), moe_w_down[l // 2].astype(bf16),
                                 ln_g[l, 1], ln_b[l, 1])
    return h.reshape(bsz, seq, d)
```

```python
import functools

import jax
import jax.numpy as jnp
import numpy as np
from jax import lax
from jax.experimental import pallas as pl
from jax.experimental.pallas import tpu as pltpu

D_MODEL = 1024
DEPTH = 2
SSM_WIDTH = 256
SSM_GROUP = 16
SSM_GROUPS = SSM_WIDTH // SSM_GROUP
SSM_STATE = 64
HGRN_HEADS = 4
HGRN_DK = 64
HGRN_DV = 64
HGRN_WIDTH = HGRN_HEADS * HGRN_DV
ATT_HEADS = 8
ATT_HEAD_DIM = 64
ATT_WIDTH = ATT_HEADS * ATT_HEAD_DIM
ATT_Q_RANK = 256
ATT_KV_RANK = 128
IDX_HEADS = 4
IDX_DIM = 64
TOPK_MAX = 256
MASK_VALUE = -1e30
N_BRANCHES = 3
N_EXPERTS = 8
TOP_K = 2
DEEPNORM_ALPHA = (2 * DEPTH) ** 0.25
LN_EPS = 1e-5
RMS_EPS = 1e-6
F_MIN = 1e-12

IN_SPLITS = (SSM_WIDTH, HGRN_HEADS * HGRN_DK, HGRN_HEADS * HGRN_DK, HGRN_WIDTH, HGRN_WIDTH,
             ATT_Q_RANK, ATT_KV_RANK, IDX_DIM, IDX_HEADS, N_BRANCHES * D_MODEL)
N_IN = sum(IN_SPLITS)

VMEM_LIMIT_BYTES = 48 * 1024 * 1024
LANE = 128
MXU_DTYPE = jnp.bfloat16


def _round_up(n, m):
    return (n + m - 1) // m * m


HG_IN_WIDTH = 2 * HGRN_HEADS * HGRN_DK + 2 * HGRN_WIDTH
ATT_IN_USED = ATT_Q_RANK + ATT_KV_RANK + IDX_DIM + IDX_HEADS
ATT_IN_WIDTH = _round_up(ATT_IN_USED, LANE)
ATT_IDX_BLOCK = (ATT_Q_RANK + ATT_KV_RANK) // LANE
MIX_IN_USED = SSM_WIDTH + HG_IN_WIDTH + ATT_IN_USED
MIX_IN_WIDTH = SSM_WIDTH + HG_IN_WIDTH + ATT_IN_WIDTH


def _in_proj_body(h_ref, w_ref, u_ref, hg_ref, att_ref):
    r = jnp.dot(h_ref[...].astype(MXU_DTYPE), w_ref[...], preferred_element_type=jnp.float32)
    u_ref[...] = r[:, :SSM_WIDTH]
    hg_ref[...] = r[:, SSM_WIDTH:SSM_WIDTH + HG_IN_WIDTH]
    att_ref[...] = r[:, SSM_WIDTH + HG_IN_WIDTH:]


def in_proj(h, w_mix, *, tm=512):
    m, d = h.shape
    row = lambda i: (i, 0)
    widths = (SSM_WIDTH, HG_IN_WIDTH, ATT_IN_WIDTH)
    return pl.pallas_call(
        _in_proj_body,
        grid=(m // tm,),
        in_specs=[pl.BlockSpec((tm, d), row), pl.BlockSpec((d, MIX_IN_WIDTH), lambda i: (0, 0))],
        out_specs=[pl.BlockSpec((tm, w), row) for w in widths],
        out_shape=[jax.ShapeDtypeStruct((m, w), jnp.float32) for w in widths],
        compiler_params=pltpu.CompilerParams(
            dimension_semantics=("parallel",), vmem_limit_bytes=VMEM_LIMIT_BYTES),
        name="in_proj",
    )(h, w_mix)


def _layer_norm_rows(y, g, b):
    mu = jnp.mean(y, axis=-1, keepdims=True)
    yc = y - mu
    var = jnp.mean(yc * yc, axis=-1, keepdims=True)
    return yc * lax.rsqrt(var + LN_EPS) * g + b


def _merge_body(h_ref, ys_ref, yh_ref, ya_ref, wgate_ref, wo_ref, g_ref, b_ref, o_ref):
    d = D_MODEL
    f32 = jnp.float32
    h = h_ref[...]
    gates = jax.nn.sigmoid(jnp.dot(h.astype(MXU_DTYPE), wgate_ref[...], preferred_element_type=f32))
    mixed = (gates[:, 0:d] * ys_ref[...] + gates[:, d:2 * d] * yh_ref[...]
             + gates[:, 2 * d:3 * d] * ya_ref[...])
    mix_out = jnp.dot(mixed.astype(MXU_DTYPE), wo_ref[...], preferred_element_type=f32)
    o_ref[...] = _layer_norm_rows(DEEPNORM_ALPHA * h + mix_out, g_ref[...], b_ref[...])


def merge_project_norm(h, y_ssm, y_hg, y_att, w_gates, w_o, ln_g, ln_b, *, tm=512):
    m, d = h.shape
    row = lambda i: (i, 0)
    full = lambda i: (0, 0)
    return pl.pallas_call(
        _merge_body,
        grid=(m // tm,),
        in_specs=[pl.BlockSpec((tm, d), row), pl.BlockSpec((tm, d), row), pl.BlockSpec((tm, d), row),
                  pl.BlockSpec((tm, d), row), pl.BlockSpec((d, N_BRANCHES * d), full),
                  pl.BlockSpec((d, d), full), pl.BlockSpec((1, d), full), pl.BlockSpec((1, d), full)],
        out_specs=pl.BlockSpec((tm, d), row),
        out_shape=jax.ShapeDtypeStruct((m, d), jnp.float32),
        compiler_params=pltpu.CompilerParams(
            dimension_semantics=("parallel",), vmem_limit_bytes=VMEM_LIMIT_BYTES),
        name="merge_project_norm",
    )(h, y_ssm, y_hg, y_att, w_gates, w_o, ln_g.reshape(1, d), ln_b.reshape(1, d))


DENSE_FFN_TOKENS = 512
DENSE_FFN_TILES = (1408, 512, 256)


def _ffn_body(h_ref, wg_ref, wu_ref, wd_ref, g_ref, b_ref, o_ref, acc_ref):
    f = pl.program_id(1)

    @pl.when(f == 0)
    def _():
        acc_ref[...] = jnp.zeros_like(acc_ref)

    x = h_ref[...].astype(MXU_DTYPE)
    a = jnp.dot(x, wg_ref[...], preferred_element_type=jnp.float32)
    u = jnp.dot(x, wu_ref[...], preferred_element_type=jnp.float32)
    act = a * jax.nn.sigmoid(a) * u
    acc_ref[...] += jnp.dot(act.astype(MXU_DTYPE), wd_ref[...], preferred_element_type=jnp.float32)

    @pl.when(f == pl.num_programs(1) - 1)
    def _():
        o_ref[...] = _layer_norm_rows(DEEPNORM_ALPHA * h_ref[...] + acc_ref[...], g_ref[...], b_ref[...])


def swiglu_norm(h, wg, wu, wd, ln_g, ln_b):
    m, d = h.shape
    f_dim = wg.shape[1]
    tm = DENSE_FFN_TOKENS
    tf = next(t for t in DENSE_FFN_TILES if f_dim % t == 0)
    assert m % tm == 0
    row = lambda i, f: (i, 0)
    full = lambda i, f: (0, 0)
    return pl.pallas_call(
        _ffn_body,
        grid=(m // tm, f_dim // tf),
        in_specs=[pl.BlockSpec((tm, d), row),
                  pl.BlockSpec((d, tf), lambda i, f: (0, f)), pl.BlockSpec((d, tf), lambda i, f: (0, f)),
                  pl.BlockSpec((tf, d), lambda i, f: (f, 0)),
                  pl.BlockSpec((1, d), full), pl.BlockSpec((1, d), full)],
        out_specs=pl.BlockSpec((tm, d), row),
        out_shape=jax.ShapeDtypeStruct((m, d), jnp.float32),
        scratch_shapes=[pltpu.VMEM((tm, d), jnp.float32)],
        compiler_params=pltpu.CompilerParams(
            dimension_semantics=("parallel", "arbitrary"), vmem_limit_bytes=VMEM_LIMIT_BYTES),
        name="swiglu_norm",
    )(h, wg, wu, wd, ln_g.reshape(1, d), ln_b.reshape(1, d))


S5_CHUNK = 8
S5_CHUNK_WIDTH = S5_CHUNK * SSM_WIDTH
S5_STATE_WIDTH = SSM_GROUPS * SSM_STATE


def s5_operators(log_dt, lam_re, lam_im, b_re, b_im, c_re, c_im):
    f32 = jnp.float32
    n = S5_CHUNK
    lre, lim = lam_re.astype(f32), lam_im.astype(f32)
    dt = jnp.exp(log_dt.astype(f32))[:, None]
    mag = jnp.exp(lre * dt)
    ang = lim * dt
    a_re, a_im = mag * jnp.cos(ang), mag * jnp.sin(ang)
    den = lre * lre + lim * lim
    coef_re = ((a_re - 1.0) * lre + a_im * lim) / den
    coef_im = (a_im * lre - (a_re - 1.0) * lim) / den
    br, bi = b_re.astype(f32), b_im.astype(f32)
    bb_re = coef_re[..., None] * br - coef_im[..., None] * bi
    bb_im = coef_re[..., None] * bi + coef_im[..., None] * br
    j = jnp.arange(n + 1, dtype=f32)[:, None, None]
    pw_re = jnp.exp(j * lre * dt) * jnp.cos(j * ang)
    pw_im = jnp.exp(j * lre * dt) * jnp.sin(j * ang)
    eye = jnp.eye(SSM_GROUPS, dtype=f32)
    cr, ci = c_re.astype(f32), c_im.astype(f32)
    ab_re = pw_re[..., None] * bb_re - pw_im[..., None] * bb_im
    ab_im = pw_re[..., None] * bb_im + pw_im[..., None] * bb_re
    kern = (jnp.einsum('gcp,jgpd->jgcd', cr, ab_re[:n]) - jnp.einsum('gcp,jgpd->jgcd', ci, ab_im[:n]))
    lag = jnp.arange(n)[None, :] - jnp.arange(n)[:, None]
    toep = jnp.where((lag >= 0)[:, :, None, None, None], kern[jnp.maximum(lag, 0)], 0.0)
    t_mat = jnp.einsum('abgcd,gh->agdbhc', toep, eye).reshape(S5_CHUNK_WIDTH, S5_CHUNK_WIDTH)
    v = jnp.stack([ab_re[:n][::-1], ab_im[:n][::-1]])
    w_in = jnp.einsum('rlgpc,gh->lhcrgp', v, eye).reshape(S5_CHUNK_WIDTH, 2 * S5_STATE_WIDTH)
    ar, ai = pw_re[1:], pw_im[1:]
    wo_re = jnp.einsum('gcp,lgp->gplc', cr, ar) - jnp.einsum('gcp,lgp->gplc', ci, ai)
    wo_im = -jnp.einsum('gcp,lgp->gplc', cr, ai) - jnp.einsum('gcp,lgp->gplc', ci, ar)
    w_out = jnp.einsum('rgplc,gh->rhplgc', jnp.stack([wo_re, wo_im]), eye)
    w_out = w_out.reshape(2 * S5_STATE_WIDTH, S5_CHUNK_WIDTH)
    decay = jnp.stack([pw_re[n].reshape(1, S5_STATE_WIDTH), pw_im[n].reshape(1, S5_STATE_WIDTH)])
    return (jnp.concatenate([t_mat, w_in], axis=1).astype(MXU_DTYPE), w_out.astype(MXU_DTYPE), decay)


def _s5_scan_body(u_ref, tw_ref, wout_ref, decay_ref, y_ref, h_ref, s_ref, hs_ref, *, tm):
    f32 = jnp.float32
    sw = S5_STATE_WIDTH

    @pl.when(pl.program_id(1) == 0)
    def _():
        h_ref[...] = jnp.zeros_like(h_ref)

    r = jnp.dot(u_ref[0].astype(MXU_DTYPE), tw_ref[...], preferred_element_type=f32)
    s_ref[...] = r[:, S5_CHUNK_WIDTH:]
    d_re = decay_ref[0]
    d_im = decay_ref[1]

    def eight_chunks(k, carry):
        h_re, h_im = carry
        r0 = pl.multiple_of(k * 8, 8)
        inc = s_ref[pl.ds(r0, 8), :]
        rows_re, rows_im = [], []
        for t in range(8):
            rows_re.append(h_re)
            rows_im.append(h_im)
            h_re, h_im = (d_re * h_re - d_im * h_im + inc[t:t + 1, :sw],
                          d_re * h_im + d_im * h_re + inc[t:t + 1, sw:])
        hs_ref[pl.ds(r0, 8), :] = jnp.concatenate(
            [jnp.concatenate(rows_re, axis=0), jnp.concatenate(rows_im, axis=0)], axis=1)
        return h_re, h_im

    h_re, h_im = lax.fori_loop(0, tm // 8, eight_chunks, (h_ref[0:1, :], h_ref[1:2, :]))
    h_ref[0:1, :] = h_re
    h_ref[1:2, :] = h_im
    y_ref[0] = r[:, :S5_CHUNK_WIDTH] + jnp.dot(hs_ref[...].astype(MXU_DTYPE), wout_ref[...],
                                                preferred_element_type=f32)


def s5_scan(u, tw, w_out, decay, *, tm=256):
    bsz, seq, _ = u.shape
    n_chunks = seq // S5_CHUNK
    tm = min(tm, n_chunks)
    assert seq % S5_CHUNK == 0 and n_chunks % tm == 0 and tm % 8 == 0
    once = pl.Buffered(1)
    y = pl.pallas_call(
        functools.partial(_s5_scan_body, tm=tm),
        grid=(bsz, n_chunks // tm),
        in_specs=[pl.BlockSpec((1, tm, S5_CHUNK_WIDTH), lambda b, i: (b, i, 0)),
                  pl.BlockSpec(tw.shape, lambda b, i: (0, 0), pipeline_mode=once),
                  pl.BlockSpec(w_out.shape, lambda b, i: (0, 0), pipeline_mode=once),
                  pl.BlockSpec(decay.shape, lambda b, i: (0, 0, 0), pipeline_mode=once)],
        out_specs=pl.BlockSpec((1, tm, S5_CHUNK_WIDTH), lambda b, i: (b, i, 0)),
        out_shape=jax.ShapeDtypeStruct((bsz, n_chunks, S5_CHUNK_WIDTH), jnp.float32),
        scratch_shapes=[pltpu.VMEM((2, S5_STATE_WIDTH), jnp.float32),
                        pltpu.VMEM((tm, 2 * S5_STATE_WIDTH), jnp.float32),
                        pltpu.VMEM((tm, 2 * S5_STATE_WIDTH), jnp.float32)],
        compiler_params=pltpu.CompilerParams(
            dimension_semantics=("parallel", "arbitrary"), vmem_limit_bytes=VMEM_LIMIT_BYTES),
        name="s5_scan",
    )(u.reshape(bsz, n_chunks, S5_CHUNK_WIDTH), tw, w_out, decay)
    return y.reshape(bsz, seq, SSM_WIDTH)


def _s5_out_body(y_ref, u_ref, d_ref, wglu_ref, bglu_ref, wout_ref, o_ref):
    f32 = jnp.float32
    y = jax.nn.gelu(y_ref[...] + d_ref[...] * u_ref[...])
    gate = jnp.dot(y.astype(MXU_DTYPE), wglu_ref[...], preferred_element_type=f32) + bglu_ref[...]
    y = y * jax.nn.sigmoid(gate)
    o_ref[...] = jnp.dot(y.astype(MXU_DTYPE), wout_ref[...], preferred_element_type=f32)


def s5_output(y, u, d_skip, w_glu, b_glu, w_out, *, tm=1024):
    m, c = y.shape
    tm = min(tm, m)
    row = lambda i: (i, 0)
    full = lambda i: (0, 0)
    return pl.pallas_call(
        _s5_out_body,
        grid=(m // tm,),
        in_specs=[pl.BlockSpec((tm, c), row), pl.BlockSpec((tm, c), row), pl.BlockSpec((1, c), full),
                  pl.BlockSpec((c, c), full), pl.BlockSpec((1, c), full), pl.BlockSpec((c, D_MODEL), full)],
        out_specs=pl.BlockSpec((tm, D_MODEL), row),
        out_shape=jax.ShapeDtypeStruct((m, D_MODEL), jnp.float32),
        compiler_params=pltpu.CompilerParams(
            dimension_semantics=("parallel",), vmem_limit_bytes=VMEM_LIMIT_BYTES),
        name="s5_output",
    )(y, u, d_skip.reshape(1, c), w_glu.astype(MXU_DTYPE), b_glu.reshape(1, c), w_out.astype(MXU_DTYPE))


def s5_branch(u, log_dt, lam_re, lam_im, b_re, b_im, c_re, c_im, d_skip, w_glu, b_glu, w_out):
    bsz, seq, _ = u.shape
    tw, w_state_out, decay = s5_operators(log_dt, lam_re, lam_im, b_re, b_im, c_re, c_im)
    y = s5_scan(u, tw, w_state_out, decay)
    out = s5_output(y.reshape(bsz * seq, SSM_WIDTH), u.reshape(bsz * seq, SSM_WIDTH),
                    d_skip, w_glu, b_glu, w_out)
    return out.reshape(bsz, seq, D_MODEL)


HG_CHUNK = 128
HG_LEVELS = 7
HG_KDIM = HGRN_HEADS * HGRN_DK


def _hgrn_segment_sums():
    c = HG_CHUNK
    t = np.arange(c)[:, None]
    u = np.arange(c)[None, :]
    blocks = []
    for lvl in range(1, HG_LEVELS + 1):
        m = (t >> lvl << lvl) + (1 << (lvl - 1)) - 1
        right = ((t >> (lvl - 1)) & 1) == 1
        blocks.append(np.where(right, (u > m) & (u <= t), (u > t) & (u <= m)))
    blocks.append(u <= t)
    blocks.append(u > t)
    return np.concatenate(blocks, axis=0).astype(np.float32)


def _hgrn_body(q_ref, z_ref, v_ref, g_ref, seg_ref, lb_ref, ng_ref, hmean_ref, wout_ref, o_ref, st_ref):
    f32 = jnp.float32
    c = HG_CHUNK

    @pl.when(pl.program_id(1) == 0)
    def _():
        st_ref[...] = jnp.zeros_like(st_ref)

    q = q_ref[0]
    z = z_ref[0]
    v = v_ref[0]
    lb = lb_ref[...]
    f = lb + (1.0 - lb) * jax.nn.sigmoid(z)
    logf = jnp.log(jnp.maximum(f, F_MIN))
    kin = (1.0 - lb) * jax.nn.sigmoid(-z)

    p1 = logf.astype(MXU_DTYPE)
    r1 = logf - p1.astype(f32)
    p2 = r1.astype(MXU_DTYPE)
    p3 = (r1 - p2.astype(f32)).astype(MXU_DTYPE)
    seg = seg_ref[...]
    sums = (jnp.dot(seg, p1, preferred_element_type=f32) + jnp.dot(seg, p2, preferred_element_type=f32)
            + jnp.dot(seg, p3, preferred_element_type=f32))

    lane_head = lax.broadcasted_iota(jnp.int32, (c, HG_KDIM), 1) // HGRN_DK
    tok = lax.broadcasted_iota(jnp.int32, (c, HG_KDIM), 0)
    row_t = lax.broadcasted_iota(jnp.int32, (HGRN_HEADS * c, c), 0) % c
    col_s = lax.broadcasted_iota(jnp.int32, (HGRN_HEADS * c, c), 1)

    def per_head_rows(x):
        return jnp.concatenate([jnp.where(lane_head == h, x, 0.0) for h in range(HGRN_HEADS)],
                               axis=0).astype(MXU_DTYPE)

    def scores(ql, kl):
        return lax.dot_general(per_head_rows(ql), kl.astype(MXU_DTYPE), (((1,), (1,)), ((), ())),
                               preferred_element_type=f32)

    att = jnp.where(row_t == col_s, scores(q, kin), 0.0)
    for lvl in range(1, HG_LEVELS + 1):
        decay = jnp.exp(sums[(lvl - 1) * c:lvl * c])
        right = ((tok >> (lvl - 1)) & 1) == 1
        a = scores(jnp.where(right, q * decay, 0.0), jnp.where(right, 0.0, kin * decay))
        att = att + jnp.where((row_t >> lvl) == (col_s >> lvl), a, 0.0)

    b = sums[HG_LEVELS * c:(HG_LEVELS + 1) * c]
    tail = sums[(HG_LEVELS + 1) * c:(HG_LEVELS + 2) * c]
    v_m = v.astype(MXU_DTYPE)
    st = st_ref[...]
    o = lax.dot_general((q * jnp.exp(b)).astype(MXU_DTYPE), st.astype(MXU_DTYPE),
                        (((1,), (1,)), ((), ())), preferred_element_type=f32)
    for h in range(HGRN_HEADS):
        o_h = jnp.dot(att[h * c:(h + 1) * c].astype(MXU_DTYPE), v_m, preferred_element_type=f32)
        o = o + jnp.where(lane_head == h, o_h, 0.0)

    kv = jnp.dot(v.T.astype(MXU_DTYPE), (kin * jnp.exp(tail)).astype(MXU_DTYPE), preferred_element_type=f32)
    sr = lax.broadcasted_iota(jnp.int32, st.shape, 0) // HGRN_DV
    sc = lax.broadcasted_iota(jnp.int32, st.shape, 1) // HGRN_DK
    st_ref[...] = st * jnp.exp(b[c - 1:c, :]) + jnp.where(sr == sc, kv, 0.0)

    o2 = o * o
    o2_hi = o2.astype(MXU_DTYPE)
    o2_lo = (o2 - o2_hi.astype(f32)).astype(MXU_DTYPE)
    ms = (jnp.dot(o2_hi, hmean_ref[...], preferred_element_type=f32)
          + jnp.dot(o2_lo, hmean_ref[...], preferred_element_type=f32))
    g = g_ref[0]
    out = o * lax.rsqrt(ms + RMS_EPS) * ng_ref[...] * (g * jax.nn.sigmoid(g))
    o_ref[0] = jnp.dot(out.astype(MXU_DTYPE), wout_ref[...], preferred_element_type=f32)


def hgrn2_branch(hg_in, lower_bound, norm_g, w_out):
    bsz, seq, _ = hg_in.shape
    assert seq % HG_CHUNK == 0 and HGRN_DK == HGRN_DV and HG_IN_WIDTH == 4 * HG_KDIM
    seg = jnp.asarray(_hgrn_segment_sums(), MXU_DTYPE)
    head_mean = jnp.asarray(np.kron(np.eye(HGRN_HEADS), np.full((HGRN_DV, HGRN_DV), 1.0 / HGRN_DV)), MXU_DTYPE)
    tok = lambda b, i: (b, i, 0)
    full = lambda b, i: (0, 0)
    part = lambda k: pl.BlockSpec((1, HG_CHUNK, HG_KDIM), lambda b, i: (b, i, k))
    return pl.pallas_call(
        _hgrn_body,
        grid=(bsz, seq // HG_CHUNK),
        in_specs=[part(0), part(1), part(2), part(3),
                  pl.BlockSpec(seg.shape, full), pl.BlockSpec((1, HG_KDIM), full),
                  pl.BlockSpec((1, HGRN_WIDTH), full), pl.BlockSpec(head_mean.shape, full),
                  pl.BlockSpec((HGRN_WIDTH, D_MODEL), full)],
        out_specs=pl.BlockSpec((1, HG_CHUNK, D_MODEL), tok),
        out_shape=jax.ShapeDtypeStruct((bsz, seq, D_MODEL), jnp.float32),
        scratch_shapes=[pltpu.VMEM((HGRN_WIDTH, HG_KDIM), jnp.float32)],
        compiler_params=pltpu.CompilerParams(
            dimension_semantics=("parallel", "arbitrary"), vmem_limit_bytes=VMEM_LIMIT_BYTES),
        name="hgrn2",
    )(hg_in, hg_in, hg_in, hg_in, seg, lower_bound.reshape(1, HG_KDIM).astype(jnp.float32),
      jnp.tile(norm_g.astype(jnp.float32), HGRN_HEADS).reshape(1, HGRN_WIDTH), head_mean,
      w_out.astype(MXU_DTYPE))


ATT_QK_DIM = LANE
DSA_TILE = 512


def _rms_rows(x, g):
    return x * lax.rsqrt(jnp.mean(x * x, axis=-1, keepdims=True) + RMS_EPS) * g


def _dsa_project_body(x_ref, gq_ref, gkv_ref, wq_ref, wqi_ref, wkt_ref, wv_ref,
                      q_ref, qi_ref, kt_ref, v_ref, kidt_ref):
    f32 = jnp.float32
    tm = x_ref.shape[1]
    x = x_ref[0]
    cq = _rms_rows(x[:, :ATT_Q_RANK], gq_ref[...]).astype(MXU_DTYPE)
    ckv = _rms_rows(x[:, ATT_Q_RANK:ATT_Q_RANK + ATT_KV_RANK], gkv_ref[...])
    q_all = jnp.dot(cq, wq_ref[...], preferred_element_type=f32)
    qi_all = jnp.dot(cq, wqi_ref[...], preferred_element_type=f32)
    v_all = jnp.dot(ckv.astype(MXU_DTYPE), wv_ref[...], preferred_element_type=f32)
    kt_all = jnp.dot(wkt_ref[...], ckv.T.astype(MXU_DTYPE), preferred_element_type=f32)

    lane = lax.broadcasted_iota(jnp.int32, (tm, LANE), 1)
    t_loc = lax.broadcasted_iota(jnp.int32, (tm, LANE), 0)
    t_even = (t_loc // 2 * 2).astype(f32)
    t_odd = (t_loc % 2).astype(f32)
    sub = lax.broadcasted_iota(jnp.int32, (LANE, tm), 0)
    s_loc = lax.broadcasted_iota(jnp.int32, (LANE, tm), 1)
    k_rows = jnp.where(sub < ATT_HEAD_DIM + 2, 1.0,
                       jnp.where(sub == ATT_HEAD_DIM + 2, (s_loc // 2 * 2).astype(f32), (s_loc % 2).astype(f32)))
    for h in range(ATT_HEADS):
        slope = 2.0 ** (-8.0 * (h + 1) / ATT_HEADS)
        q_cols = jnp.where(lane == ATT_HEAD_DIM, -slope * t_even,
                           jnp.where(lane == ATT_HEAD_DIM + 1, -slope * t_odd, slope))
        q_h = q_all[:, h * LANE:(h + 1) * LANE]
        q_ref[0, h] = jnp.where(lane < ATT_HEAD_DIM, q_h,
                                jnp.where(lane < ATT_HEAD_DIM + 4, q_cols, 0.0)).astype(q_ref.dtype)
        k_h = kt_all[h * LANE:(h + 1) * LANE, :]
        kt_ref[0, h] = jnp.where(sub < ATT_HEAD_DIM, k_h,
                                 jnp.where(sub < ATT_HEAD_DIM + 4, k_rows, 0.0)).astype(kt_ref.dtype)
        v_h = v_all[:, h * LANE:(h + 1) * LANE]
        v_ref[0, h] = jnp.where(lane == ATT_HEAD_DIM, 1.0, v_h).astype(v_ref.dtype)
    for h in range(IDX_HEADS):
        qi_ref[0, h] = qi_all[:, h * LANE:(h + 1) * LANE].astype(qi_ref.dtype)
    kidt_ref[0] = x[:, ATT_IDX_BLOCK * LANE:(ATT_IDX_BLOCK + 1) * LANE].T.astype(kidt_ref.dtype)


def _head_padded(w, n_heads, dim):
    k = w.shape[0]
    return jnp.pad(w.reshape(k, n_heads, dim), ((0, 0), (0, 0), (0, LANE - dim))).reshape(k, n_heads * LANE)


def dsa_project(att_in, q_norm_g, kv_norm_g, w_uq, w_qidx, w_ukv):
    bsz, seq, _ = att_in.shape
    tm = DSA_TILE
    assert seq % tm == 0 and 8 % ATT_HEADS == 0 and tm <= 512
    w_q = _head_padded(w_uq * ATT_HEAD_DIM ** -0.5, ATT_HEADS, ATT_HEAD_DIM).astype(MXU_DTYPE)
    w_qi = _head_padded(w_qidx * IDX_DIM ** -0.5, IDX_HEADS, IDX_DIM).astype(MXU_DTYPE)
    w_kt = _head_padded(w_ukv[:, :ATT_WIDTH], ATT_HEADS, ATT_HEAD_DIM).T.astype(MXU_DTYPE)
    w_v = _head_padded(w_ukv[:, ATT_WIDTH:], ATT_HEADS, ATT_HEAD_DIM).astype(MXU_DTYPE)
    tok = lambda b, i: (b, 0, i, 0)
    full = lambda b, i: (0, 0)
    dt = MXU_DTYPE
    return pl.pallas_call(
        _dsa_project_body,
        grid=(bsz, seq // tm),
        in_specs=[pl.BlockSpec((1, tm, ATT_IN_WIDTH), lambda b, i: (b, i, 0)),
                  pl.BlockSpec((1, ATT_Q_RANK), full), pl.BlockSpec((1, ATT_KV_RANK), full),
                  pl.BlockSpec(w_q.shape, full), pl.BlockSpec(w_qi.shape, full),
                  pl.BlockSpec(w_kt.shape, full), pl.BlockSpec(w_v.shape, full)],
        out_specs=[pl.BlockSpec((1, ATT_HEADS, tm, LANE), tok),
                   pl.BlockSpec((1, IDX_HEADS, tm, LANE), tok),
                   pl.BlockSpec((1, ATT_HEADS, LANE, tm), lambda b, i: (b, 0, 0, i)),
                   pl.BlockSpec((1, ATT_HEADS, tm, LANE), tok),
                   pl.BlockSpec((1, LANE, tm), lambda b, i: (b, 0, i))],
        out_shape=[jax.ShapeDtypeStruct((bsz, ATT_HEADS, seq, LANE), dt),
                   jax.ShapeDtypeStruct((bsz, IDX_HEADS, seq, LANE), dt),
                   jax.ShapeDtypeStruct((bsz, ATT_HEADS, LANE, seq), dt),
                   jax.ShapeDtypeStruct((bsz, ATT_HEADS, seq, LANE), dt),
                   jax.ShapeDtypeStruct((bsz, LANE, seq), dt)],
        compiler_params=pltpu.CompilerParams(
            dimension_semantics=("parallel", "parallel"), vmem_limit_bytes=VMEM_LIMIT_BYTES),
        name="dsa_project",
    )(att_in, q_norm_g.reshape(1, ATT_Q_RANK), kv_norm_g.reshape(1, ATT_KV_RANK), w_q, w_qi, w_kt, w_v)


_INT_MIN = -2 ** 31
_MASK_KEY = int(np.float32(MASK_VALUE).view(np.int32)) ^ 0x7FFFFFFF
_SEARCH_ROWS = 128
_ATT_ROWS = 32
_SEARCH_LANES = 512
_LIST_DEPTH = 12
_LIST_ROWS = 16
_LIST_MIN_BLOCKS = 4


def _key_to_score(key):
    return lax.bitcast_convert_type(jnp.where(key < 0, key ^ 0x7FFFFFFF, key), jnp.float32)


def _dsa_body(qi_ref, kj_ref, qidx_ref, w_ref, kidt_ref, q_ref, kt_ref, v_ref, tri_ref, wout_ref,
              o_ref, keys_ref, thr_ref, need_ref, carry_ref, m_ref, acc_ref, bias_ref, s_ref, p_ref,
              alpha_ref, cand_ref, *, tq, tk, seq, n_sel):
    f32 = jnp.float32
    p_id = pl.program_id(1)
    i = qi_ref[p_id]
    j = kj_ref[p_id]
    row_local = lax.broadcasted_iota(jnp.int32, (tq, tk), 0)
    col_local = lax.broadcasted_iota(jnp.int32, (tq, tk), 1)

    @pl.when(j == 0)
    def _select():
        def score_block(jj, carry):
            off = pl.multiple_of(jj * tk, tk)
            kb = kidt_ref[0, :, pl.ds(off, tk)]
            sc = jnp.zeros((tq, tk), f32)
            for h in range(IDX_HEADS):
                s = jnp.dot(qidx_ref[0, h], kb, preferred_element_type=f32)
                w_h = w_ref[0, :, IDX_DIM + h:IDX_DIM + h + 1] * IDX_HEADS ** -0.5
                sc = sc + jnp.maximum(s, 0.0) * w_h
            causal = (col_local + jj * tk) <= (row_local + i * tq)
            sc = jnp.where(causal, sc, MASK_VALUE)
            keys_ref[:, pl.ds(off, tk)] = jnp.where(sc == 0.0, 0.0, sc)
            return carry

        lax.fori_loop(0, i + 1, score_block, 0)

        n_blocks = i + 1
        n_masked_tail = (seq - n_blocks * tk).astype(f32)

        def row_group(r, carry):
            r0 = pl.multiple_of(r * _SEARCH_ROWS, _SEARCH_ROWS)

            def count_ge(ref, n_iter, cand):
                cand_b = jnp.broadcast_to(_key_to_score(cand), (_SEARCH_ROWS, LANE))

                def chunk(c, acc):
                    base = pl.multiple_of(c * _SEARCH_LANES, _SEARCH_LANES)
                    for u in range(_SEARCH_LANES // LANE):
                        kk = ref[pl.ds(r0, _SEARCH_ROWS), pl.ds(base + u * LANE, LANE)]
                        acc = acc + jnp.where(kk >= cand_b, 1.0, 0.0)
                    return acc

                acc = lax.fori_loop(0, n_iter, chunk, jnp.zeros((_SEARCH_ROWS, LANE), f32))
                cnt = jnp.sum(acc, axis=1, keepdims=True)
                return cnt + jnp.where(cand <= _MASK_KEY, n_masked_tail, 0.0)

            def kth_largest_key(count):
                zero = jnp.zeros((_SEARCH_ROWS, 1), jnp.int32)
                v0 = jnp.where(count(zero) >= n_sel, zero, zero + _INT_MIN)

                def bit_step(b, v):
                    cand = v | jnp.left_shift(jnp.int32(1), 30 - b)
                    return jnp.where(count(cand) >= n_sel, cand, v)

                return lax.fori_loop(0, 31, bit_step, v0)

            count_all = functools.partial(count_ge, keys_ref, n_blocks * (tk // _SEARCH_LANES))

            def store(v, n_above):
                thr_ref[pl.ds(r0, _SEARCH_ROWS), :] = _key_to_score(v)
                need_ref[pl.ds(r0, _SEARCH_ROWS), :] = n_sel - n_above

            def search_all_keys():
                v = kth_largest_key(count_all)
                store(v, count_all(v + 1))

            @pl.when(n_blocks < _LIST_MIN_BLOCKS)
            def _():
                search_all_keys()

            @pl.when(n_blocks >= _LIST_MIN_BLOCKS)
            def _():
                def shortlist(sub, carry):
                    rr = pl.multiple_of(r0 + sub * _LIST_ROWS, _LIST_ROWS)

                    def insert_block(c, tops):
                        base = pl.multiple_of(c * tk, tk)
                        for u in range(tk // LANE):
                            x = keys_ref[pl.ds(rr, _LIST_ROWS), pl.ds(base + u * LANE, LANE)]
                            new = []
                            for t in range(_LIST_DEPTH):
                                new.append(jnp.maximum(tops[t], x))
                                x = jnp.minimum(tops[t], x)
                            tops = tuple(new)
                        return tops

                    tops = lax.fori_loop(0, n_blocks, insert_block,
                                         tuple(jnp.full((_LIST_ROWS, LANE), -jnp.inf, f32)
                                               for _ in range(_LIST_DEPTH)))
                    for t in range(_LIST_DEPTH):
                        cand_ref[pl.ds(rr, _LIST_ROWS), t * LANE:(t + 1) * LANE] = tops[t]
                    return carry

                lax.fori_loop(0, _SEARCH_ROWS // _LIST_ROWS, shortlist, 0)
                v = kth_largest_key(functools.partial(count_ge, cand_ref,
                                                      _LIST_DEPTH * LANE // _SEARCH_LANES))
                n_at_least = count_all(v)
                n_above = count_all(v + 1)
                store(v, n_above)
                exact = (n_above < n_sel) & (n_at_least >= n_sel)
                pl.when(jnp.min(jnp.where(exact, 1.0, 0.0)) < 0.5)(search_all_keys)

            return carry

        lax.fori_loop(0, tq // _SEARCH_ROWS, row_group, 0)
        carry_ref[...] = jnp.zeros_like(carry_ref)
        m_ref[...] = jnp.full_like(m_ref, MASK_VALUE)
        acc_ref[...] = jnp.zeros_like(acc_ref)

    keys_blk = keys_ref[:, pl.ds(pl.multiple_of(j * tk, tk), tk)]
    thr = thr_ref[...]
    eq = keys_blk == thr
    tie_rank = carry_ref[...] + jnp.dot(jnp.where(eq, 1.0, 0.0).astype(MXU_DTYPE), tri_ref[...],
                                         preferred_element_type=f32)
    carry_ref[...] = tie_rank[:, tk - 1:tk]
    tie_bias = jnp.where(eq, jnp.where(tie_rank <= need_ref[...], 0.0, MASK_VALUE), MASK_VALUE)
    bias = jnp.where(keys_blk > thr, 0.0, tie_bias)
    causal = (row_local + i * tq) >= (col_local + j * tk)
    bias_ref[...] = jnp.where(causal, bias, MASK_VALUE)

    def logits(h, slot):
        s_ref[slot] = jnp.dot(q_ref[0, h], kt_ref[0, h], preferred_element_type=f32)

    def softmax(h, slot):
        slope_tk = jnp.right_shift(tk, h + 1).astype(f32)
        for r in range(0, tq, _ATT_ROWS):
            rows = pl.ds(r, _ATT_ROWS)
            s = s_ref[slot, rows, :] + bias_ref[rows, :]
            m_old = m_ref[h, rows, :] - slope_tk
            m_new = jnp.maximum(m_old, jnp.max(s, axis=1, keepdims=True))
            p_ref[slot, rows, :] = jnp.exp(s - m_new).astype(MXU_DTYPE)
            alpha_ref[slot, rows, :] = jnp.exp(m_old - m_new)
            m_ref[h, rows, :] = m_new

    def weighted_values(h, slot):
        acc_ref[h] = alpha_ref[slot] * acc_ref[h] + jnp.dot(p_ref[slot], v_ref[0, h],
                                                             preferred_element_type=f32)

    for t in range(ATT_HEADS + 2):
        if t < ATT_HEADS:
            logits(t, t % 2)
        if 1 <= t <= ATT_HEADS:
            softmax(t - 1, (t - 1) % 2)
        if t >= 2:
            weighted_values(t - 2, t % 2)

    @pl.when(j == i)
    def _finish():
        heads = []
        for h in range(ATT_HEADS):
            a = acc_ref[h]
            heads.append((a[:, :ATT_HEAD_DIM] / a[:, ATT_HEAD_DIM:ATT_HEAD_DIM + 1]).astype(MXU_DTYPE))
        o_ref[0] = jnp.dot(jnp.concatenate(heads, axis=1), wout_ref[...], preferred_element_type=f32)


def dsa_attention(att_in, q_idx, kid_t, q, k_t, v_aug, w_out):
    bsz, _, seq, _ = q.shape
    tq = tk = DSA_TILE
    n_sel = min(TOPK_MAX, seq // 4)
    nq = seq // tq
    assert seq % tq == 0 and tk % _SEARCH_LANES == 0 and tq % _SEARCH_ROWS == 0
    pairs = [(a, b) for a in range(nq) for b in range(a + 1)]
    qi = jnp.asarray([a for a, _ in pairs], jnp.int32)
    kj = jnp.asarray([b for _, b in pairs], jnp.int32)
    tri = jnp.triu(jnp.ones((tk, tk), MXU_DTYPE))
    grid_spec = pltpu.PrefetchScalarGridSpec(
        num_scalar_prefetch=2,
        grid=(bsz, len(pairs)),
        in_specs=[
            pl.BlockSpec((1, IDX_HEADS, tq, LANE), lambda b, p, qi, kj: (b, 0, qi[p], 0)),
            pl.BlockSpec((1, tq, LANE), lambda b, p, qi, kj: (b, qi[p], ATT_IDX_BLOCK)),
            pl.BlockSpec((1, LANE, seq), lambda b, p, qi, kj: (b, 0, 0)),
            pl.BlockSpec((1, ATT_HEADS, tq, ATT_QK_DIM), lambda b, p, qi, kj: (b, 0, qi[p], 0)),
            pl.BlockSpec((1, ATT_HEADS, ATT_QK_DIM, tk), lambda b, p, qi, kj: (b, 0, 0, kj[p])),
            pl.BlockSpec((1, ATT_HEADS, tk, LANE), lambda b, p, qi, kj: (b, 0, kj[p], 0)),
            pl.BlockSpec((tk, tk), lambda b, p, qi, kj: (0, 0)),
            pl.BlockSpec((ATT_WIDTH, D_MODEL), lambda b, p, qi, kj: (0, 0)),
        ],
        out_specs=pl.BlockSpec((1, tq, D_MODEL), lambda b, p, qi, kj: (b, qi[p], 0)),
        scratch_shapes=[
            pltpu.VMEM((tq, seq), jnp.float32),
            pltpu.VMEM((tq, 1), jnp.float32),
            pltpu.VMEM((tq, 1), jnp.float32),
            pltpu.VMEM((tq, 1), jnp.float32),
            pltpu.VMEM((ATT_HEADS, tq, 1), jnp.float32),
            pltpu.VMEM((ATT_HEADS, tq, LANE), jnp.float32),
            pltpu.VMEM((tq, tk), jnp.float32),
            pltpu.VMEM((2, tq, tk), jnp.float32),
            pltpu.VMEM((2, tq, tk), MXU_DTYPE),
            pltpu.VMEM((2, tq, 1), jnp.float32),
            pltpu.VMEM((tq, _LIST_DEPTH * LANE), jnp.float32),
        ])
    return pl.pallas_call(
        functools.partial(_dsa_body, tq=tq, tk=tk, seq=seq, n_sel=n_sel),
        grid_spec=grid_spec,
        out_shape=jax.ShapeDtypeStruct((bsz, seq, D_MODEL), jnp.float32),
        compiler_params=pltpu.CompilerParams(
            dimension_semantics=("parallel", "arbitrary"), vmem_limit_bytes=VMEM_LIMIT_BYTES),
        name="dsa_attention",
    )(qi, kj, q_idx, att_in, kid_t, q, k_t, v_aug, tri, w_out)


def dsa_branch(att_in, q_norm_g, kv_norm_g, w_uq, w_qidx, w_ukv, w_out):
    q, q_idx, k_t, v_aug, kid_t = dsa_project(att_in, q_norm_g, kv_norm_g, w_uq, w_qidx, w_ukv)
    return dsa_attention(att_in, q_idx, kid_t, q, k_t, v_aug, w_out.astype(MXU_DTYPE))


MOE_TOKENS = 1024
MOE_ROW_CLASSES = (256, 288, 320, 512, 1024)
MOE_FFN_TILES = (896, 512, 256)
MOE_VMEM_LIMIT_BYTES = 56 * 1024 * 1024


def _moe_body(cnt_ref, h_ref, gate_ref, gatet_ref, tri_ref, wg_ref, wu_ref, wd_ref, g_ref, b_ref, o_ref,
              xb_ref, gather_ref, scatter_ref, xc_ref, yc_ref, acc_ref, rank_ref):
    f32 = jnp.float32
    tm = MOE_TOKENS
    i = pl.program_id(0)
    e = pl.program_id(1)
    f = pl.program_id(2)
    last_f = pl.num_programs(2) - 1

    @pl.when((e == 0) & (f == 0))
    def _():
        xb_ref[...] = h_ref[...].astype(MXU_DTYPE)
        acc_ref[...] = jnp.zeros_like(acc_ref)
        routed_rows = jnp.where(gatet_ref[...] != 0.0, 1.0, 0.0).astype(MXU_DTYPE)
        rank_ref[...] = jnp.dot(routed_rows, tri_ref[...], preferred_element_type=f32)

    def gate_column():
        lane = lax.broadcasted_iota(jnp.int32, gate_ref.shape, 1)
        return jnp.sum(jnp.where(lane == e, gate_ref[...], 0.0), axis=-1, keepdims=True)

    def routed(size):
        def run():
            @pl.when(f == 0)
            def _():
                active_row = jnp.where(gatet_ref[pl.ds(e, 1), :] != 0.0, 1.0, 0.0)
                rank_row = rank_ref[pl.ds(e, 1), :]
                slot = lax.broadcasted_iota(jnp.int32, (size, tm), 0).astype(f32)
                gather = jnp.where(slot == rank_row, active_row, 0.0)
                gather_ref[0:size, :] = gather.astype(MXU_DTYPE)
                rank_col = jnp.broadcast_to(rank_row, (LANE, tm)).T[:, 0:1]
                slot_l = lax.broadcasted_iota(jnp.int32, (tm, size), 1).astype(f32)
                scatter = jnp.where(slot_l == rank_col, jnp.where(gate_column() != 0.0, 1.0, 0.0), 0.0)
                scatter_ref[:, 0:size] = scatter.astype(MXU_DTYPE)
                xc_ref[0:size, :] = jnp.dot(gather.astype(MXU_DTYPE), xb_ref[...],
                                            preferred_element_type=f32).astype(MXU_DTYPE)
                yc_ref[0:size, :] = jnp.zeros((size, D_MODEL), f32)

            xc = xc_ref[0:size, :]
            a = jnp.dot(xc, wg_ref[0], preferred_element_type=f32)
            u = jnp.dot(xc, wu_ref[0], preferred_element_type=f32)
            act = a * jax.nn.sigmoid(a) * u
            yc_ref[0:size, :] += jnp.dot(act.astype(MXU_DTYPE), wd_ref[0], preferred_element_type=f32)

            @pl.when(f == last_f)
            def _():
                back = jnp.dot(scatter_ref[:, 0:size], yc_ref[0:size, :].astype(MXU_DTYPE),
                               preferred_element_type=f32)
                acc_ref[...] += gate_column() * back
        return run

    count = cnt_ref[i * N_EXPERTS + e]
    size_class = sum((count > s).astype(jnp.int32) for s in MOE_ROW_CLASSES[:-1])
    for k, size in enumerate(MOE_ROW_CLASSES):
        pl.when((size_class == k) & (count > 0))(routed(size))

    @pl.when((e == pl.num_programs(1) - 1) & (f == last_f))
    def _():
        o_ref[...] = _layer_norm_rows(DEEPNORM_ALPHA * h_ref[...] + acc_ref[...], g_ref[...], b_ref[...])


def moe_experts_norm(h, routing, wg, wu, wd, ln_g, ln_b):
    gates, gates_t, counts = routing
    m, d = h.shape
    n_experts, _, f_dim = wg.shape
    tm = MOE_TOKENS
    tf = next(t for t in MOE_FFN_TILES if f_dim % t == 0)
    assert m % tm == 0 and MOE_ROW_CLASSES[-1] == tm
    n_tiles = m // tm
    tri = jnp.triu(jnp.ones((tm, tm), MXU_DTYPE), k=1)
    row = lambda i, e, f, c: (i, 0)
    full = lambda i, e, f, c: (0, 0)
    grid_spec = pltpu.PrefetchScalarGridSpec(
        num_scalar_prefetch=1,
        grid=(n_tiles, n_experts, f_dim // tf),
        in_specs=[pl.BlockSpec((tm, d), row), pl.BlockSpec((tm, LANE), row),
                  pl.BlockSpec((n_experts, tm), lambda i, e, f, c: (0, i)),
                  pl.BlockSpec((tm, tm), full),
                  pl.BlockSpec((1, d, tf), lambda i, e, f, c: (e, 0, f)),
                  pl.BlockSpec((1, d, tf), lambda i, e, f, c: (e, 0, f)),
                  pl.BlockSpec((1, tf, d), lambda i, e, f, c: (e, f, 0)),
                  pl.BlockSpec((1, d), full), pl.BlockSpec((1, d), full)],
        out_specs=pl.BlockSpec((tm, d), row),
        scratch_shapes=[pltpu.VMEM((tm, d), MXU_DTYPE),
                        pltpu.VMEM((tm, tm), MXU_DTYPE),
                        pltpu.VMEM((tm, tm), MXU_DTYPE),
                        pltpu.VMEM((tm, d), MXU_DTYPE),
                        pltpu.VMEM((tm, d), jnp.float32),
                        pltpu.VMEM((tm, d), jnp.float32),
                        pltpu.VMEM((n_experts, tm), jnp.float32)])
    return pl.pallas_call(
        _moe_body,
        grid_spec=grid_spec,
        out_shape=jax.ShapeDtypeStruct((m, d), jnp.float32),
        compiler_params=pltpu.CompilerParams(
            dimension_semantics=("parallel", "arbitrary", "arbitrary"),
            vmem_limit_bytes=MOE_VMEM_LIMIT_BYTES),
        name="moe_experts_norm",
    )(counts, h, gates, gates_t, tri, wg, wu, wd, ln_g.reshape(1, d), ln_b.reshape(1, d))


def _router_body(h_ref, r_ref, gate_ref, gatet_ref, cnt_ref):
    f32 = jnp.float32
    logits = jnp.dot(h_ref[...], r_ref[...], preferred_element_type=f32)
    lane = lax.broadcasted_iota(jnp.int32, logits.shape, 1)
    logits = jnp.where(lane < N_EXPERTS, logits, -jnp.inf)
    top1 = jnp.max(logits, axis=1, keepdims=True)
    idx1 = jnp.min(jnp.where(logits == top1, lane, LANE), axis=1, keepdims=True)
    rest = jnp.where(lane == idx1, -jnp.inf, logits)
    top2 = jnp.max(rest, axis=1, keepdims=True)
    idx2 = jnp.min(jnp.where(rest == top2, lane, LANE), axis=1, keepdims=True)
    e2 = jnp.exp(top2 - top1)
    gates = jnp.where(lane == idx1, 1.0 / (1.0 + e2), jnp.where(lane == idx2, e2 / (1.0 + e2), 0.0))
    gate_ref[...] = gates
    gatet_ref[...] = gates.T[0:N_EXPERTS, :]
    cnt_ref[0] = jnp.sum(jnp.where(gates != 0.0, 1.0, 0.0), axis=0, keepdims=True)


def router_gates(h, router):
    m, d = h.shape
    tm = MOE_TOKENS
    assert TOP_K == 2 and m % tm == 0
    r_pad = jnp.pad(router.astype(jnp.float32), ((0, 0), (0, LANE - N_EXPERTS)))
    gates, gates_t, counts = pl.pallas_call(
        _router_body,
        grid=(m // tm,),
        in_specs=[pl.BlockSpec((tm, d), lambda i: (i, 0)), pl.BlockSpec((d, LANE), lambda i: (0, 0))],
        out_specs=[pl.BlockSpec((tm, LANE), lambda i: (i, 0)), pl.BlockSpec((N_EXPERTS, tm), lambda i: (0, i)),
                   pl.BlockSpec((1, 1, LANE), lambda i: (i, 0, 0))],
        out_shape=[jax.ShapeDtypeStruct((m, LANE), jnp.float32),
                   jax.ShapeDtypeStruct((N_EXPERTS, m), jnp.float32),
                   jax.ShapeDtypeStruct((m // tm, 1, LANE), jnp.float32)],
        compiler_params=pltpu.CompilerParams(
            dimension_semantics=("parallel",), vmem_limit_bytes=VMEM_LIMIT_BYTES),
        name="router_gates",
    )(h, r_pad)
    return gates, gates_t, counts[:, 0, :N_EXPERTS].astype(jnp.int32).reshape(-1)


def kernel(x, w_in, ssm_log_dt, ssm_lambda_re, ssm_lambda_im, ssm_b_re, ssm_b_im, ssm_c_re, ssm_c_im,
           ssm_d, ssm_w_glu, ssm_b_glu, ssm_w_out, hgrn_lb_logits, hgrn_norm_g, hgrn_w_out,
           attn_q_norm_g, attn_kv_norm_g, attn_w_uq, attn_w_qidx, attn_w_ukv, attn_w_out, w_o,
           ln_g, ln_b, ffn_w_gate, ffn_w_up, ffn_w_down, moe_router, moe_w_gate, moe_w_up, moe_w_down):
    bsz, seq, d = x.shape
    m = bsz * seq
    bf16 = MXU_DTYPE
    assert MIX_IN_USED + N_BRANCHES * D_MODEL == N_IN
    lb_soft = jax.nn.softmax(hgrn_lb_logits.astype(jnp.float32), axis=0)
    lower_bounds = jnp.concatenate([jnp.zeros_like(lb_soft[:1]), jnp.cumsum(lb_soft[1:], axis=0)], axis=0)
    h = x.reshape(m, d)
    for l in range(DEPTH):
        w_mix = jnp.pad(w_in[l][:, :MIX_IN_USED], ((0, 0), (0, MIX_IN_WIDTH - MIX_IN_USED))).astype(bf16)
        u, hg_in, att_in = in_proj(h, w_mix)
        y_ssm = s5_branch(u.reshape(bsz, seq, SSM_WIDTH), ssm_log_dt[l], ssm_lambda_re[l], ssm_lambda_im[l],
                          ssm_b_re[l], ssm_b_im[l], ssm_c_re[l], ssm_c_im[l], ssm_d[l], ssm_w_glu[l],
                          ssm_b_glu[l], ssm_w_out[l])
        y_hg = hgrn2_branch(hg_in.reshape(bsz, seq, HG_IN_WIDTH), lower_bounds[l], hgrn_norm_g[l],
                            hgrn_w_out[l])
        y_att = dsa_branch(att_in.reshape(bsz, seq, ATT_IN_WIDTH), attn_q_norm_g[l], attn_kv_norm_g[l],
                           attn_w_uq[l], attn_w_qidx[l], attn_w_ukv[l], attn_w_out[l])
        h = merge_project_norm(h, y_ssm.reshape(m, d), y_hg.reshape(m, d), y_att.reshape(m, d),
                               w_in[l][:, MIX_IN_USED:].astype(bf16), w_o[l].astype(bf16),
                               ln_g[l, 0], ln_b[l, 0])
        if l % 2 == 0:
            h = swiglu_norm(h, ffn_w_gate[l // 2].astype(bf16), ffn_w_up[l // 2].astype(bf16),
                            ffn_w_down[l // 2].astype(bf16), ln_g[l, 1], ln_b[l, 1])
        else:
            gate_w = router_gates(h, moe_router[l // 2])
            h = moe_experts_norm(h, gate_w, moe_w_gate[l // 2].astype(bf16),
                                 moe_w_up[l // 2].astype(bf16), moe_w_down[l // 2].astype(bf16),
                                 ln_g[l, 1], ln_b[l, 1])
    return h.reshape(bsz, seq, d)
```

```python
import functools

import jax
import jax.numpy as jnp
import numpy as np
from jax import lax
from jax.experimental import pallas as pl
from jax.experimental.pallas import tpu as pltpu

D_MODEL = 1024
DEPTH = 2
SSM_WIDTH = 256
SSM_GROUP = 16
SSM_GROUPS = SSM_WIDTH // SSM_GROUP
SSM_STATE = 64
HGRN_HEADS = 4
HGRN_DK = 64
HGRN_DV = 64
HGRN_WIDTH = HGRN_HEADS * HGRN_DV
ATT_HEADS = 8
ATT_HEAD_DIM = 64
ATT_WIDTH = ATT_HEADS * ATT_HEAD_DIM
ATT_Q_RANK = 256
ATT_KV_RANK = 128
IDX_HEADS = 4
IDX_DIM = 64
TOPK_MAX = 256
MASK_VALUE = -1e30
N_BRANCHES = 3
N_EXPERTS = 8
TOP_K = 2
DEEPNORM_ALPHA = (2 * DEPTH) ** 0.25
LN_EPS = 1e-5
RMS_EPS = 1e-6
F_MIN = 1e-12

IN_SPLITS = (SSM_WIDTH, HGRN_HEADS * HGRN_DK, HGRN_HEADS * HGRN_DK, HGRN_WIDTH, HGRN_WIDTH,
             ATT_Q_RANK, ATT_KV_RANK, IDX_DIM, IDX_HEADS, N_BRANCHES * D_MODEL)
N_IN = sum(IN_SPLITS)

VMEM_LIMIT_BYTES = 48 * 1024 * 1024
LANE = 128
MXU_DTYPE = jnp.bfloat16


def _round_up(n, m):
    return (n + m - 1) // m * m


CAST_ROWS = 512


def _cast_body(x_ref, o_ref):
    o_ref[...] = x_ref[...].astype(o_ref.dtype)


def as_mxu_operand(w):
    n, rows, cols = w.shape
    assert rows % CAST_ROWS == 0
    blk = pl.BlockSpec((1, CAST_ROWS, cols), lambda e, r: (e, r, 0))
    return pl.pallas_call(
        _cast_body,
        grid=(n, rows // CAST_ROWS),
        in_specs=[blk],
        out_specs=blk,
        out_shape=jax.ShapeDtypeStruct(w.shape, MXU_DTYPE),
        compiler_params=pltpu.CompilerParams(
            dimension_semantics=("parallel", "parallel"), vmem_limit_bytes=VMEM_LIMIT_BYTES),
        name="as_mxu_operand",
    )(w)


HG_IN_WIDTH = 2 * HGRN_HEADS * HGRN_DK + 2 * HGRN_WIDTH
ATT_IN_USED = ATT_Q_RANK + ATT_KV_RANK + IDX_DIM + IDX_HEADS
ATT_IN_WIDTH = _round_up(ATT_IN_USED, LANE)
ATT_IDX_BLOCK = (ATT_Q_RANK + ATT_KV_RANK) // LANE
MIX_IN_USED = SSM_WIDTH + HG_IN_WIDTH + ATT_IN_USED
MIX_IN_WIDTH = SSM_WIDTH + HG_IN_WIDTH + ATT_IN_WIDTH


def _in_proj_body(h_ref, w_ref, u_ref, hg_ref, att_ref):
    r = jnp.dot(h_ref[...].astype(MXU_DTYPE), w_ref[...], preferred_element_type=jnp.float32)
    u_ref[...] = r[:, :SSM_WIDTH]
    hg_ref[...] = r[:, SSM_WIDTH:SSM_WIDTH + HG_IN_WIDTH]
    att_ref[...] = r[:, SSM_WIDTH + HG_IN_WIDTH:]


def in_proj(h, w_mix, *, tm=512):
    m, d = h.shape
    row = lambda i: (i, 0)
    widths = (SSM_WIDTH, HG_IN_WIDTH, ATT_IN_WIDTH)
    return pl.pallas_call(
        _in_proj_body,
        grid=(m // tm,),
        in_specs=[pl.BlockSpec((tm, d), row), pl.BlockSpec((d, MIX_IN_WIDTH), lambda i: (0, 0))],
        out_specs=[pl.BlockSpec((tm, w), row) for w in widths],
        out_shape=[jax.ShapeDtypeStruct((m, w), jnp.float32) for w in widths],
        compiler_params=pltpu.CompilerParams(
            dimension_semantics=("parallel",), vmem_limit_bytes=VMEM_LIMIT_BYTES),
        name="in_proj",
    )(h, w_mix)


def _layer_norm_rows(y, g, b):
    mu = jnp.mean(y, axis=-1, keepdims=True)
    yc = y - mu
    var = jnp.mean(yc * yc, axis=-1, keepdims=True)
    return yc * lax.rsqrt(var + LN_EPS) * g + b


def _merge_body(h_ref, ys_ref, yh_ref, ya_ref, wgate_ref, wo_ref, g_ref, b_ref, o_ref):
    d = D_MODEL
    f32 = jnp.float32
    h = h_ref[...]
    gates = jax.nn.sigmoid(jnp.dot(h.astype(MXU_DTYPE), wgate_ref[...], preferred_element_type=f32))
    mixed = (gates[:, 0:d] * ys_ref[...] + gates[:, d:2 * d] * yh_ref[...]
             + gates[:, 2 * d:3 * d] * ya_ref[...])
    mix_out = jnp.dot(mixed.astype(MXU_DTYPE), wo_ref[...], preferred_element_type=f32)
    o_ref[...] = _layer_norm_rows(DEEPNORM_ALPHA * h + mix_out, g_ref[...], b_ref[...])


def merge_project_norm(h, y_ssm, y_hg, y_att, w_gates, w_o, ln_g, ln_b, *, tm=512):
    m, d = h.shape
    row = lambda i: (i, 0)
    full = lambda i: (0, 0)
    return pl.pallas_call(
        _merge_body,
        grid=(m // tm,),
        in_specs=[pl.BlockSpec((tm, d), row), pl.BlockSpec((tm, d), row), pl.BlockSpec((tm, d), row),
                  pl.BlockSpec((tm, d), row), pl.BlockSpec((d, N_BRANCHES * d), full),
                  pl.BlockSpec((d, d), full), pl.BlockSpec((1, d), full), pl.BlockSpec((1, d), full)],
        out_specs=pl.BlockSpec((tm, d), row),
        out_shape=jax.ShapeDtypeStruct((m, d), jnp.float32),
        compiler_params=pltpu.CompilerParams(
            dimension_semantics=("parallel",), vmem_limit_bytes=VMEM_LIMIT_BYTES),
        name="merge_project_norm",
    )(h, y_ssm, y_hg, y_att, w_gates, w_o, ln_g.reshape(1, d), ln_b.reshape(1, d))


DENSE_FFN_TOKENS = 512
DENSE_FFN_TILES = (1408, 512, 256)


def _ffn_body(h_ref, wg_ref, wu_ref, wd_ref, g_ref, b_ref, o_ref, acc_ref):
    f = pl.program_id(1)

    @pl.when(f == 0)
    def _():
        acc_ref[...] = jnp.zeros_like(acc_ref)

    x = h_ref[...].astype(MXU_DTYPE)
    a = jnp.dot(x, wg_ref[...], preferred_element_type=jnp.float32)
    u = jnp.dot(x, wu_ref[...], preferred_element_type=jnp.float32)
    act = a * jax.nn.sigmoid(a) * u
    acc_ref[...] += jnp.dot(act.astype(MXU_DTYPE), wd_ref[...], preferred_element_type=jnp.float32)

    @pl.when(f == pl.num_programs(1) - 1)
    def _():
        o_ref[...] = _layer_norm_rows(DEEPNORM_ALPHA * h_ref[...] + acc_ref[...], g_ref[...], b_ref[...])


def swiglu_norm(h, wg, wu, wd, ln_g, ln_b):
    m, d = h.shape
    f_dim = wg.shape[1]
    tm = DENSE_FFN_TOKENS
    tf = next(t for t in DENSE_FFN_TILES if f_dim % t == 0)
    assert m % tm == 0
    row = lambda i, f: (i, 0)
    full = lambda i, f: (0, 0)
    return pl.pallas_call(
        _ffn_body,
        grid=(m // tm, f_dim // tf),
        in_specs=[pl.BlockSpec((tm, d), row),
                  pl.BlockSpec((d, tf), lambda i, f: (0, f)), pl.BlockSpec((d, tf), lambda i, f: (0, f)),
                  pl.BlockSpec((tf, d), lambda i, f: (f, 0)),
                  pl.BlockSpec((1, d), full), pl.BlockSpec((1, d), full)],
        out_specs=pl.BlockSpec((tm, d), row),
        out_shape=jax.ShapeDtypeStruct((m, d), jnp.float32),
        scratch_shapes=[pltpu.VMEM((tm, d), jnp.float32)],
        compiler_params=pltpu.CompilerParams(
            dimension_semantics=("parallel", "arbitrary"), vmem_limit_bytes=VMEM_LIMIT_BYTES),
        name="swiglu_norm",
    )(h, wg, wu, wd, ln_g.reshape(1, d), ln_b.reshape(1, d))


S5_CHUNK = 8
S5_CHUNK_WIDTH = S5_CHUNK * SSM_WIDTH
S5_STATE_WIDTH = SSM_GROUPS * SSM_STATE


def s5_operators(log_dt, lam_re, lam_im, b_re, b_im, c_re, c_im):
    f32 = jnp.float32
    n = S5_CHUNK
    lre, lim = lam_re.astype(f32), lam_im.astype(f32)
    dt = jnp.exp(log_dt.astype(f32))[:, None]
    mag = jnp.exp(lre * dt)
    ang = lim * dt
    a_re, a_im = mag * jnp.cos(ang), mag * jnp.sin(ang)
    den = lre * lre + lim * lim
    coef_re = ((a_re - 1.0) * lre + a_im * lim) / den
    coef_im = (a_im * lre - (a_re - 1.0) * lim) / den
    br, bi = b_re.astype(f32), b_im.astype(f32)
    bb_re = coef_re[..., None] * br - coef_im[..., None] * bi
    bb_im = coef_re[..., None] * bi + coef_im[..., None] * br
    j = jnp.arange(n + 1, dtype=f32)[:, None, None]
    pw_re = jnp.exp(j * lre * dt) * jnp.cos(j * ang)
    pw_im = jnp.exp(j * lre * dt) * jnp.sin(j * ang)
    eye = jnp.eye(SSM_GROUPS, dtype=f32)
    cr, ci = c_re.astype(f32), c_im.astype(f32)
    ab_re = pw_re[..., None] * bb_re - pw_im[..., None] * bb_im
    ab_im = pw_re[..., None] * bb_im + pw_im[..., None] * bb_re
    kern = (jnp.einsum('gcp,jgpd->jgcd', cr, ab_re[:n]) - jnp.einsum('gcp,jgpd->jgcd', ci, ab_im[:n]))
    lag = jnp.arange(n)[None, :] - jnp.arange(n)[:, None]
    toep = jnp.where((lag >= 0)[:, :, None, None, None], kern[jnp.maximum(lag, 0)], 0.0)
    t_mat = jnp.einsum('abgcd,gh->agdbhc', toep, eye).reshape(S5_CHUNK_WIDTH, S5_CHUNK_WIDTH)
    v = jnp.stack([ab_re[:n][::-1], ab_im[:n][::-1]])
    w_in = jnp.einsum('rlgpc,gh->lhcrgp', v, eye).reshape(S5_CHUNK_WIDTH, 2 * S5_STATE_WIDTH)
    ar, ai = pw_re[1:], pw_im[1:]
    wo_re = jnp.einsum('gcp,lgp->gplc', cr, ar) - jnp.einsum('gcp,lgp->gplc', ci, ai)
    wo_im = -jnp.einsum('gcp,lgp->gplc', cr, ai) - jnp.einsum('gcp,lgp->gplc', ci, ar)
    w_out = jnp.einsum('rgplc,gh->rhplgc', jnp.stack([wo_re, wo_im]), eye)
    w_out = w_out.reshape(2 * S5_STATE_WIDTH, S5_CHUNK_WIDTH)
    decay = jnp.stack([pw_re[n].reshape(1, S5_STATE_WIDTH), pw_im[n].reshape(1, S5_STATE_WIDTH)])
    return (jnp.concatenate([t_mat, w_in], axis=1).astype(MXU_DTYPE), w_out.astype(MXU_DTYPE), decay)


def _s5_scan_body(u_ref, tw_ref, wout_ref, decay_ref, y_ref, h_ref, s_ref, hs_ref, *, tm):
    f32 = jnp.float32
    sw = S5_STATE_WIDTH

    @pl.when(pl.program_id(1) == 0)
    def _():
        h_ref[...] = jnp.zeros_like(h_ref)

    r = jnp.dot(u_ref[0].astype(MXU_DTYPE), tw_ref[...], preferred_element_type=f32)
    s_ref[...] = r[:, S5_CHUNK_WIDTH:]
    d_re = decay_ref[0]
    d_im = decay_ref[1]

    def eight_chunks(k, carry):
        h_re, h_im = carry
        r0 = pl.multiple_of(k * 8, 8)
        inc = s_ref[pl.ds(r0, 8), :]
        rows_re, rows_im = [], []
        for t in range(8):
            rows_re.append(h_re)
            rows_im.append(h_im)
            h_re, h_im = (d_re * h_re - d_im * h_im + inc[t:t + 1, :sw],
                          d_re * h_im + d_im * h_re + inc[t:t + 1, sw:])
        hs_ref[pl.ds(r0, 8), :] = jnp.concatenate(
            [jnp.concatenate(rows_re, axis=0), jnp.concatenate(rows_im, axis=0)], axis=1)
        return h_re, h_im

    h_re, h_im = lax.fori_loop(0, tm // 8, eight_chunks, (h_ref[0:1, :], h_ref[1:2, :]))
    h_ref[0:1, :] = h_re
    h_ref[1:2, :] = h_im
    y_ref[0] = r[:, :S5_CHUNK_WIDTH] + jnp.dot(hs_ref[...].astype(MXU_DTYPE), wout_ref[...],
                                                preferred_element_type=f32)


def s5_scan(u, tw, w_out, decay, *, tm=256):
    bsz, seq, _ = u.shape
    n_chunks = seq // S5_CHUNK
    tm = min(tm, n_chunks)
    assert seq % S5_CHUNK == 0 and n_chunks % tm == 0 and tm % 8 == 0
    once = pl.Buffered(1)
    y = pl.pallas_call(
        functools.partial(_s5_scan_body, tm=tm),
        grid=(bsz, n_chunks // tm),
        in_specs=[pl.BlockSpec((1, tm, S5_CHUNK_WIDTH), lambda b, i: (b, i, 0)),
                  pl.BlockSpec(tw.shape, lambda b, i: (0, 0), pipeline_mode=once),
                  pl.BlockSpec(w_out.shape, lambda b, i: (0, 0), pipeline_mode=once),
                  pl.BlockSpec(decay.shape, lambda b, i: (0, 0, 0), pipeline_mode=once)],
        out_specs=pl.BlockSpec((1, tm, S5_CHUNK_WIDTH), lambda b, i: (b, i, 0)),
        out_shape=jax.ShapeDtypeStruct((bsz, n_chunks, S5_CHUNK_WIDTH), jnp.float32),
        scratch_shapes=[pltpu.VMEM((2, S5_STATE_WIDTH), jnp.float32),
                        pltpu.VMEM((tm, 2 * S5_STATE_WIDTH), jnp.float32),
                        pltpu.VMEM((tm, 2 * S5_STATE_WIDTH), jnp.float32)],
        compiler_params=pltpu.CompilerParams(
            dimension_semantics=("parallel", "arbitrary"), vmem_limit_bytes=VMEM_LIMIT_BYTES),
        name="s5_scan",
    )(u.reshape(bsz, n_chunks, S5_CHUNK_WIDTH), tw, w_out, decay)
    return y.reshape(bsz, seq, SSM_WIDTH)


def _s5_out_body(y_ref, u_ref, d_ref, wglu_ref, bglu_ref, wout_ref, o_ref):
    f32 = jnp.float32
    y = jax.nn.gelu(y_ref[...] + d_ref[...] * u_ref[...])
    gate = jnp.dot(y.astype(MXU_DTYPE), wglu_ref[...], preferred_element_type=f32) + bglu_ref[...]
    y = y * jax.nn.sigmoid(gate)
    o_ref[...] = jnp.dot(y.astype(MXU_DTYPE), wout_ref[...], preferred_element_type=f32)


def s5_output(y, u, d_skip, w_glu, b_glu, w_out, *, tm=1024):
    m, c = y.shape
    tm = min(tm, m)
    row = lambda i: (i, 0)
    full = lambda i: (0, 0)
    return pl.pallas_call(
        _s5_out_body,
        grid=(m // tm,),
        in_specs=[pl.BlockSpec((tm, c), row), pl.BlockSpec((tm, c), row), pl.BlockSpec((1, c), full),
                  pl.BlockSpec((c, c), full), pl.BlockSpec((1, c), full), pl.BlockSpec((c, D_MODEL), full)],
        out_specs=pl.BlockSpec((tm, D_MODEL), row),
        out_shape=jax.ShapeDtypeStruct((m, D_MODEL), jnp.float32),
        compiler_params=pltpu.CompilerParams(
            dimension_semantics=("parallel",), vmem_limit_bytes=VMEM_LIMIT_BYTES),
        name="s5_output",
    )(y, u, d_skip.reshape(1, c), w_glu.astype(MXU_DTYPE), b_glu.reshape(1, c), w_out.astype(MXU_DTYPE))


def s5_branch(u, log_dt, lam_re, lam_im, b_re, b_im, c_re, c_im, d_skip, w_glu, b_glu, w_out):
    bsz, seq, _ = u.shape
    tw, w_state_out, decay = s5_operators(log_dt, lam_re, lam_im, b_re, b_im, c_re, c_im)
    y = s5_scan(u, tw, w_state_out, decay)
    out = s5_output(y.reshape(bsz * seq, SSM_WIDTH), u.reshape(bsz * seq, SSM_WIDTH),
                    d_skip, w_glu, b_glu, w_out)
    return out.reshape(bsz, seq, D_MODEL)


HG_CHUNK = 128
HG_LEVELS = 7
HG_KDIM = HGRN_HEADS * HGRN_DK


def _hgrn_segment_sums():
    c = HG_CHUNK
    t = np.arange(c)[:, None]
    u = np.arange(c)[None, :]
    blocks = []
    for lvl in range(1, HG_LEVELS + 1):
        m = (t >> lvl << lvl) + (1 << (lvl - 1)) - 1
        right = ((t >> (lvl - 1)) & 1) == 1
        blocks.append(np.where(right, (u > m) & (u <= t), (u > t) & (u <= m)))
    blocks.append(u <= t)
    blocks.append(u > t)
    return np.concatenate(blocks, axis=0).astype(np.float32)


def _hgrn_body(q_ref, z_ref, v_ref, g_ref, seg_ref, lb_ref, ng_ref, hmean_ref, wout_ref, o_ref, st_ref):
    f32 = jnp.float32
    c = HG_CHUNK

    @pl.when(pl.program_id(1) == 0)
    def _():
        st_ref[...] = jnp.zeros_like(st_ref)

    q = q_ref[0]
    z = z_ref[0]
    v = v_ref[0]
    lb = lb_ref[...]
    f = lb + (1.0 - lb) * jax.nn.sigmoid(z)
    logf = jnp.log(jnp.maximum(f, F_MIN))
    kin = (1.0 - lb) * jax.nn.sigmoid(-z)

    p1 = logf.astype(MXU_DTYPE)
    r1 = logf - p1.astype(f32)
    p2 = r1.astype(MXU_DTYPE)
    p3 = (r1 - p2.astype(f32)).astype(MXU_DTYPE)
    seg = seg_ref[...]
    sums = (jnp.dot(seg, p1, preferred_element_type=f32) + jnp.dot(seg, p2, preferred_element_type=f32)
            + jnp.dot(seg, p3, preferred_element_type=f32))

    lane_head = lax.broadcasted_iota(jnp.int32, (c, HG_KDIM), 1) // HGRN_DK
    tok = lax.broadcasted_iota(jnp.int32, (c, HG_KDIM), 0)
    row_t = lax.broadcasted_iota(jnp.int32, (HGRN_HEADS * c, c), 0) % c
    col_s = lax.broadcasted_iota(jnp.int32, (HGRN_HEADS * c, c), 1)

    def per_head_rows(x):
        return jnp.concatenate([jnp.where(lane_head == h, x, 0.0) for h in range(HGRN_HEADS)],
                               axis=0).astype(MXU_DTYPE)

    def scores(ql, kl):
        return lax.dot_general(per_head_rows(ql), kl.astype(MXU_DTYPE), (((1,), (1,)), ((), ())),
                               preferred_element_type=f32)

    att = jnp.where(row_t == col_s, scores(q, kin), 0.0)
    for lvl in range(1, HG_LEVELS + 1):
        decay = jnp.exp(sums[(lvl - 1) * c:lvl * c])
        right = ((tok >> (lvl - 1)) & 1) == 1
        a = scores(jnp.where(right, q * decay, 0.0), jnp.where(right, 0.0, kin * decay))
        att = att + jnp.where((row_t >> lvl) == (col_s >> lvl), a, 0.0)

    b = sums[HG_LEVELS * c:(HG_LEVELS + 1) * c]
    tail = sums[(HG_LEVELS + 1) * c:(HG_LEVELS + 2) * c]
    v_m = v.astype(MXU_DTYPE)
    st = st_ref[...]
    o = lax.dot_general((q * jnp.exp(b)).astype(MXU_DTYPE), st.astype(MXU_DTYPE),
                        (((1,), (1,)), ((), ())), preferred_element_type=f32)
    for h in range(HGRN_HEADS):
        o_h = jnp.dot(att[h * c:(h + 1) * c].astype(MXU_DTYPE), v_m, preferred_element_type=f32)
        o = o + jnp.where(lane_head == h, o_h, 0.0)

    kv = jnp.dot(v.T.astype(MXU_DTYPE), (kin * jnp.exp(tail)).astype(MXU_DTYPE), preferred_element_type=f32)
    sr = lax.broadcasted_iota(jnp.int32, st.shape, 0) // HGRN_DV
    sc = lax.broadcasted_iota(jnp.int32, st.shape, 1) // HGRN_DK
    st_ref[...] = st * jnp.exp(b[c - 1:c, :]) + jnp.where(sr == sc, kv, 0.0)

    o2 = o * o
    o2_hi = o2.astype(MXU_DTYPE)
    o2_lo = (o2 - o2_hi.astype(f32)).astype(MXU_DTYPE)
    ms = (jnp.dot(o2_hi, hmean_ref[...], preferred_element_type=f32)
          + jnp.dot(o2_lo, hmean_ref[...], preferred_element_type=f32))
    g = g_ref[0]
    out = o * lax.rsqrt(ms + RMS_EPS) * ng_ref[...] * (g * jax.nn.sigmoid(g))
    o_ref[0] = jnp.dot(out.astype(MXU_DTYPE), wout_ref[...], preferred_element_type=f32)


def hgrn2_branch(hg_in, lower_bound, norm_g, w_out):
    bsz, seq, _ = hg_in.shape
    assert seq % HG_CHUNK == 0 and HGRN_DK == HGRN_DV and HG_IN_WIDTH == 4 * HG_KDIM
    seg = jnp.asarray(_hgrn_segment_sums(), MXU_DTYPE)
    head_mean = jnp.asarray(np.kron(np.eye(HGRN_HEADS), np.full((HGRN_DV, HGRN_DV), 1.0 / HGRN_DV)), MXU_DTYPE)
    tok = lambda b, i: (b, i, 0)
    full = lambda b, i: (0, 0)
    part = lambda k: pl.BlockSpec((1, HG_CHUNK, HG_KDIM), lambda b, i: (b, i, k))
    return pl.pallas_call(
        _hgrn_body,
        grid=(bsz, seq // HG_CHUNK),
        in_specs=[part(0), part(1), part(2), part(3),
                  pl.BlockSpec(seg.shape, full), pl.BlockSpec((1, HG_KDIM), full),
                  pl.BlockSpec((1, HGRN_WIDTH), full), pl.BlockSpec(head_mean.shape, full),
                  pl.BlockSpec((HGRN_WIDTH, D_MODEL), full)],
        out_specs=pl.BlockSpec((1, HG_CHUNK, D_MODEL), tok),
        out_shape=jax.ShapeDtypeStruct((bsz, seq, D_MODEL), jnp.float32),
        scratch_shapes=[pltpu.VMEM((HGRN_WIDTH, HG_KDIM), jnp.float32)],
        compiler_params=pltpu.CompilerParams(
            dimension_semantics=("parallel", "arbitrary"), vmem_limit_bytes=VMEM_LIMIT_BYTES),
        name="hgrn2",
    )(hg_in, hg_in, hg_in, hg_in, seg, lower_bound.reshape(1, HG_KDIM).astype(jnp.float32),
      jnp.tile(norm_g.astype(jnp.float32), HGRN_HEADS).reshape(1, HGRN_WIDTH), head_mean,
      w_out.astype(MXU_DTYPE))


ATT_QK_DIM = LANE
DSA_TILE = 512


def _rms_rows(x, g):
    return x * lax.rsqrt(jnp.mean(x * x, axis=-1, keepdims=True) + RMS_EPS) * g


def _dsa_project_body(x_ref, gq_ref, gkv_ref, wq_ref, wqi_ref, wkt_ref, wv_ref,
                      q_ref, qi_ref, kt_ref, v_ref, kidt_ref):
    f32 = jnp.float32
    tm = x_ref.shape[1]
    x = x_ref[0]
    cq = _rms_rows(x[:, :ATT_Q_RANK], gq_ref[...]).astype(MXU_DTYPE)
    ckv = _rms_rows(x[:, ATT_Q_RANK:ATT_Q_RANK + ATT_KV_RANK], gkv_ref[...])
    q_all = jnp.dot(cq, wq_ref[...], preferred_element_type=f32)
    qi_all = jnp.dot(cq, wqi_ref[...], preferred_element_type=f32)
    v_all = jnp.dot(ckv.astype(MXU_DTYPE), wv_ref[...], preferred_element_type=f32)
    kt_all = jnp.dot(wkt_ref[...], ckv.T.astype(MXU_DTYPE), preferred_element_type=f32)

    lane = lax.broadcasted_iota(jnp.int32, (tm, LANE), 1)
    t_loc = lax.broadcasted_iota(jnp.int32, (tm, LANE), 0)
    t_even = (t_loc // 2 * 2).astype(f32)
    t_odd = (t_loc % 2).astype(f32)
    sub = lax.broadcasted_iota(jnp.int32, (LANE, tm), 0)
    s_loc = lax.broadcasted_iota(jnp.int32, (LANE, tm), 1)
    k_rows = jnp.where(sub < ATT_HEAD_DIM + 2, 1.0,
                       jnp.where(sub == ATT_HEAD_DIM + 2, (s_loc // 2 * 2).astype(f32), (s_loc % 2).astype(f32)))
    for h in range(ATT_HEADS):
        slope = 2.0 ** (-8.0 * (h + 1) / ATT_HEADS)
        q_cols = jnp.where(lane == ATT_HEAD_DIM, -slope * t_even,
                           jnp.where(lane == ATT_HEAD_DIM + 1, -slope * t_odd, slope))
        q_h = q_all[:, h * LANE:(h + 1) * LANE]
        q_ref[0, h] = jnp.where(lane < ATT_HEAD_DIM, q_h,
                                jnp.where(lane < ATT_HEAD_DIM + 4, q_cols, 0.0)).astype(q_ref.dtype)
        k_h = kt_all[h * LANE:(h + 1) * LANE, :]
        kt_ref[0, h] = jnp.where(sub < ATT_HEAD_DIM, k_h,
                                 jnp.where(sub < ATT_HEAD_DIM + 4, k_rows, 0.0)).astype(kt_ref.dtype)
        v_h = v_all[:, h * LANE:(h + 1) * LANE]
        v_ref[0, h] = jnp.where(lane == ATT_HEAD_DIM, 1.0, v_h).astype(v_ref.dtype)
    for h in range(IDX_HEADS):
        qi_ref[0, h] = qi_all[:, h * LANE:(h + 1) * LANE].astype(qi_ref.dtype)
    kidt_ref[0] = x[:, ATT_IDX_BLOCK * LANE:(ATT_IDX_BLOCK + 1) * LANE].T.astype(kidt_ref.dtype)


def _head_padded(w, n_heads, dim):
    k = w.shape[0]
    return jnp.pad(w.reshape(k, n_heads, dim), ((0, 0), (0, 0), (0, LANE - dim))).reshape(k, n_heads * LANE)


def dsa_project(att_in, q_norm_g, kv_norm_g, w_uq, w_qidx, w_ukv):
    bsz, seq, _ = att_in.shape
    tm = DSA_TILE
    assert seq % tm == 0 and 8 % ATT_HEADS == 0 and tm <= 512
    w_q = _head_padded(w_uq * ATT_HEAD_DIM ** -0.5, ATT_HEADS, ATT_HEAD_DIM).astype(MXU_DTYPE)
    w_qi = _head_padded(w_qidx * IDX_DIM ** -0.5, IDX_HEADS, IDX_DIM).astype(MXU_DTYPE)
    w_kt = _head_padded(w_ukv[:, :ATT_WIDTH], ATT_HEADS, ATT_HEAD_DIM).T.astype(MXU_DTYPE)
    w_v = _head_padded(w_ukv[:, ATT_WIDTH:], ATT_HEADS, ATT_HEAD_DIM).astype(MXU_DTYPE)
    tok = lambda b, i: (b, 0, i, 0)
    full = lambda b, i: (0, 0)
    dt = MXU_DTYPE
    return pl.pallas_call(
        _dsa_project_body,
        grid=(bsz, seq // tm),
        in_specs=[pl.BlockSpec((1, tm, ATT_IN_WIDTH), lambda b, i: (b, i, 0)),
                  pl.BlockSpec((1, ATT_Q_RANK), full), pl.BlockSpec((1, ATT_KV_RANK), full),
                  pl.BlockSpec(w_q.shape, full), pl.BlockSpec(w_qi.shape, full),
                  pl.BlockSpec(w_kt.shape, full), pl.BlockSpec(w_v.shape, full)],
        out_specs=[pl.BlockSpec((1, ATT_HEADS, tm, LANE), tok),
                   pl.BlockSpec((1, IDX_HEADS, tm, LANE), tok),
                   pl.BlockSpec((1, ATT_HEADS, LANE, tm), lambda b, i: (b, 0, 0, i)),
                   pl.BlockSpec((1, ATT_HEADS, tm, LANE), tok),
                   pl.BlockSpec((1, LANE, tm), lambda b, i: (b, 0, i))],
        out_shape=[jax.ShapeDtypeStruct((bsz, ATT_HEADS, seq, LANE), dt),
                   jax.ShapeDtypeStruct((bsz, IDX_HEADS, seq, LANE), dt),
                   jax.ShapeDtypeStruct((bsz, ATT_HEADS, LANE, seq), dt),
                   jax.ShapeDtypeStruct((bsz, ATT_HEADS, seq, LANE), dt),
                   jax.ShapeDtypeStruct((bsz, LANE, seq), dt)],
        compiler_params=pltpu.CompilerParams(
            dimension_semantics=("parallel", "parallel"), vmem_limit_bytes=VMEM_LIMIT_BYTES),
        name="dsa_project",
    )(att_in, q_norm_g.reshape(1, ATT_Q_RANK), kv_norm_g.reshape(1, ATT_KV_RANK), w_q, w_qi, w_kt, w_v)


_INT_MIN = -2 ** 31
_MASK_KEY = int(np.float32(MASK_VALUE).view(np.int32)) ^ 0x7FFFFFFF
_SEARCH_ROWS = 128
_ATT_ROWS = 32
_SEARCH_LANES = 512
_LIST_DEPTH = 12
_LIST_ROWS = 16
_LIST_MIN_BLOCKS = 4


def _key_to_score(key):
    return lax.bitcast_convert_type(jnp.where(key < 0, key ^ 0x7FFFFFFF, key), jnp.float32)


def _dsa_body(qi_ref, kj_ref, qidx_ref, w_ref, kidt_ref, q_ref, kt_ref, v_ref, tri_ref, wout_ref,
              o_ref, keys_ref, thr_ref, need_ref, carry_ref, m_ref, acc_ref, bias_ref, s_ref, p_ref,
              alpha_ref, cand_ref, *, tq, tk, seq, n_sel):
    f32 = jnp.float32
    p_id = pl.program_id(1)
    i = qi_ref[p_id]
    j = kj_ref[p_id]
    row_local = lax.broadcasted_iota(jnp.int32, (tq, tk), 0)
    col_local = lax.broadcasted_iota(jnp.int32, (tq, tk), 1)

    @pl.when(j == 0)
    def _select():
        def score_block(jj, carry):
            off = pl.multiple_of(jj * tk, tk)
            kb = kidt_ref[0, :, pl.ds(off, tk)]
            sc = jnp.zeros((tq, tk), f32)
            for h in range(IDX_HEADS):
                s = jnp.dot(qidx_ref[0, h], kb, preferred_element_type=f32)
                w_h = w_ref[0, :, IDX_DIM + h:IDX_DIM + h + 1] * IDX_HEADS ** -0.5
                sc = sc + jnp.maximum(s, 0.0) * w_h
            causal = (col_local + jj * tk) <= (row_local + i * tq)
            sc = jnp.where(causal, sc, MASK_VALUE)
            keys_ref[:, pl.ds(off, tk)] = jnp.where(sc == 0.0, 0.0, sc)
            return carry

        lax.fori_loop(0, i + 1, score_block, 0)

        n_blocks = i + 1
        n_masked_tail = (seq - n_blocks * tk).astype(f32)

        def row_group(r, carry):
            r0 = pl.multiple_of(r * _SEARCH_ROWS, _SEARCH_ROWS)

            def count_ge(ref, n_iter, cand):
                cand_b = jnp.broadcast_to(_key_to_score(cand), (_SEARCH_ROWS, LANE))

                def chunk(c, acc):
                    base = pl.multiple_of(c * _SEARCH_LANES, _SEARCH_LANES)
                    for u in range(_SEARCH_LANES // LANE):
                        kk = ref[pl.ds(r0, _SEARCH_ROWS), pl.ds(base + u * LANE, LANE)]
                        acc = acc + jnp.where(kk >= cand_b, 1.0, 0.0)
                    return acc

                acc = lax.fori_loop(0, n_iter, chunk, jnp.zeros((_SEARCH_ROWS, LANE), f32))
                cnt = jnp.sum(acc, axis=1, keepdims=True)
                return cnt + jnp.where(cand <= _MASK_KEY, n_masked_tail, 0.0)

            def kth_largest_key(count):
                zero = jnp.zeros((_SEARCH_ROWS, 1), jnp.int32)
                v0 = jnp.where(count(zero) >= n_sel, zero, zero + _INT_MIN)

                def bit_step(b, v):
                    cand = v | jnp.left_shift(jnp.int32(1), 30 - b)
                    return jnp.where(count(cand) >= n_sel, cand, v)

                return lax.fori_loop(0, 31, bit_step, v0)

            count_all = functools.partial(count_ge, keys_ref, n_blocks * (tk // _SEARCH_LANES))

            def store(v, n_above):
                thr_ref[pl.ds(r0, _SEARCH_ROWS), :] = _key_to_score(v)
                need_ref[pl.ds(r0, _SEARCH_ROWS), :] = n_sel - n_above

            def search_all_keys():
                v = kth_largest_key(count_all)
                store(v, count_all(v + 1))

            @pl.when(n_blocks < _LIST_MIN_BLOCKS)
            def _():
                search_all_keys()

            @pl.when(n_blocks >= _LIST_MIN_BLOCKS)
            def _():
                def shortlist(sub, carry):
                    rr = pl.multiple_of(r0 + sub * _LIST_ROWS, _LIST_ROWS)

                    def insert_block(c, tops):
                        base = pl.multiple_of(c * tk, tk)
                        for u in range(tk // LANE):
                            x = keys_ref[pl.ds(rr, _LIST_ROWS), pl.ds(base + u * LANE, LANE)]
                            new = []
                            for t in range(_LIST_DEPTH):
                                new.append(jnp.maximum(tops[t], x))
                                x = jnp.minimum(tops[t], x)
                            tops = tuple(new)
                        return tops

                    tops = lax.fori_loop(0, n_blocks, insert_block,
                                         tuple(jnp.full((_LIST_ROWS, LANE), -jnp.inf, f32)
                                               for _ in range(_LIST_DEPTH)))
                    for t in range(_LIST_DEPTH):
                        cand_ref[pl.ds(rr, _LIST_ROWS), t * LANE:(t + 1) * LANE] = tops[t]
                    return carry

                lax.fori_loop(0, _SEARCH_ROWS // _LIST_ROWS, shortlist, 0)
                v = kth_largest_key(functools.partial(count_ge, cand_ref,
                                                      _LIST_DEPTH * LANE // _SEARCH_LANES))
                n_at_least = count_all(v)
                n_above = count_all(v + 1)
                store(v, n_above)
                exact = (n_above < n_sel) & (n_at_least >= n_sel)
                pl.when(jnp.min(jnp.where(exact, 1.0, 0.0)) < 0.5)(search_all_keys)

            return carry

        lax.fori_loop(0, tq // _SEARCH_ROWS, row_group, 0)
        carry_ref[...] = jnp.zeros_like(carry_ref)
        m_ref[...] = jnp.full_like(m_ref, MASK_VALUE)
        acc_ref[...] = jnp.zeros_like(acc_ref)

    keys_blk = keys_ref[:, pl.ds(pl.multiple_of(j * tk, tk), tk)]
    thr = thr_ref[...]
    eq = keys_blk == thr
    tie_rank = carry_ref[...] + jnp.dot(jnp.where(eq, 1.0, 0.0).astype(MXU_DTYPE), tri_ref[...],
                                         preferred_element_type=f32)
    carry_ref[...] = tie_rank[:, tk - 1:tk]
    tie_bias = jnp.where(eq, jnp.where(tie_rank <= need_ref[...], 0.0, MASK_VALUE), MASK_VALUE)
    bias = jnp.where(keys_blk > thr, 0.0, tie_bias)
    causal = (row_local + i * tq) >= (col_local + j * tk)
    bias_ref[...] = jnp.where(causal, bias, MASK_VALUE)

    def logits(h, slot):
        s_ref[slot] = jnp.dot(q_ref[0, h], kt_ref[0, h], preferred_element_type=f32)

    def softmax(h, slot):
        slope_tk = jnp.right_shift(tk, h + 1).astype(f32)
        for r in range(0, tq, _ATT_ROWS):
            rows = pl.ds(r, _ATT_ROWS)
            s = s_ref[slot, rows, :] + bias_ref[rows, :]
            m_old = m_ref[h, rows, :] - slope_tk
            m_new = jnp.maximum(m_old, jnp.max(s, axis=1, keepdims=True))
            p_ref[slot, rows, :] = jnp.exp(s - m_new).astype(MXU_DTYPE)
            alpha_ref[slot, rows, :] = jnp.exp(m_old - m_new)
            m_ref[h, rows, :] = m_new

    def weighted_values(h, slot):
        acc_ref[h] = alpha_ref[slot] * acc_ref[h] + jnp.dot(p_ref[slot], v_ref[0, h],
                                                             preferred_element_type=f32)

    for t in range(ATT_HEADS + 2):
        if t < ATT_HEADS:
            logits(t, t % 2)
        if 1 <= t <= ATT_HEADS:
            softmax(t - 1, (t - 1) % 2)
        if t >= 2:
            weighted_values(t - 2, t % 2)

    @pl.when(j == i)
    def _finish():
        heads = []
        for h in range(ATT_HEADS):
            a = acc_ref[h]
            heads.append((a[:, :ATT_HEAD_DIM] / a[:, ATT_HEAD_DIM:ATT_HEAD_DIM + 1]).astype(MXU_DTYPE))
        o_ref[0] = jnp.dot(jnp.concatenate(heads, axis=1), wout_ref[...], preferred_element_type=f32)


def dsa_attention(att_in, q_idx, kid_t, q, k_t, v_aug, w_out):
    bsz, _, seq, _ = q.shape
    tq = tk = DSA_TILE
    n_sel = min(TOPK_MAX, seq // 4)
    nq = seq // tq
    assert seq % tq == 0 and tk % _SEARCH_LANES == 0 and tq % _SEARCH_ROWS == 0
    pairs = [(a, b) for a in range(nq) for b in range(a + 1)]
    qi = jnp.asarray([a for a, _ in pairs], jnp.int32)
    kj = jnp.asarray([b for _, b in pairs], jnp.int32)
    tri = jnp.triu(jnp.ones((tk, tk), MXU_DTYPE))
    grid_spec = pltpu.PrefetchScalarGridSpec(
        num_scalar_prefetch=2,
        grid=(bsz, len(pairs)),
        in_specs=[
            pl.BlockSpec((1, IDX_HEADS, tq, LANE), lambda b, p, qi, kj: (b, 0, qi[p], 0)),
            pl.BlockSpec((1, tq, LANE), lambda b, p, qi, kj: (b, qi[p], ATT_IDX_BLOCK)),
            pl.BlockSpec((1, LANE, seq), lambda b, p, qi, kj: (b, 0, 0)),
            pl.BlockSpec((1, ATT_HEADS, tq, ATT_QK_DIM), lambda b, p, qi, kj: (b, 0, qi[p], 0)),
            pl.BlockSpec((1, ATT_HEADS, ATT_QK_DIM, tk), lambda b, p, qi, kj: (b, 0, 0, kj[p])),
            pl.BlockSpec((1, ATT_HEADS, tk, LANE), lambda b, p, qi, kj: (b, 0, kj[p], 0)),
            pl.BlockSpec((tk, tk), lambda b, p, qi, kj: (0, 0)),
            pl.BlockSpec((ATT_WIDTH, D_MODEL), lambda b, p, qi, kj: (0, 0)),
        ],
        out_specs=pl.BlockSpec((1, tq, D_MODEL), lambda b, p, qi, kj: (b, qi[p], 0)),
        scratch_shapes=[
            pltpu.VMEM((tq, seq), jnp.float32),
            pltpu.VMEM((tq, 1), jnp.float32),
            pltpu.VMEM((tq, 1), jnp.float32),
            pltpu.VMEM((tq, 1), jnp.float32),
            pltpu.VMEM((ATT_HEADS, tq, 1), jnp.float32),
            pltpu.VMEM((ATT_HEADS, tq, LANE), jnp.float32),
            pltpu.VMEM((tq, tk), jnp.float32),
            pltpu.VMEM((2, tq, tk), jnp.float32),
            pltpu.VMEM((2, tq, tk), MXU_DTYPE),
            pltpu.VMEM((2, tq, 1), jnp.float32),
            pltpu.VMEM((tq, _LIST_DEPTH * LANE), jnp.float32),
        ])
    return pl.pallas_call(
        functools.partial(_dsa_body, tq=tq, tk=tk, seq=seq, n_sel=n_sel),
        grid_spec=grid_spec,
        out_shape=jax.ShapeDtypeStruct((bsz, seq, D_MODEL), jnp.float32),
        compiler_params=pltpu.CompilerParams(
            dimension_semantics=("parallel", "arbitrary"), vmem_limit_bytes=VMEM_LIMIT_BYTES),
        name="dsa_attention",
    )(qi, kj, q_idx, att_in, kid_t, q, k_t, v_aug, tri, w_out)


def dsa_branch(att_in, q_norm_g, kv_norm_g, w_uq, w_qidx, w_ukv, w_out):
    q, q_idx, k_t, v_aug, kid_t = dsa_project(att_in, q_norm_g, kv_norm_g, w_uq, w_qidx, w_ukv)
    return dsa_attention(att_in, q_idx, kid_t, q, k_t, v_aug, w_out.astype(MXU_DTYPE))


MOE_TOKENS = 1024
MOE_ROW_CLASSES = (256, 288, 320, 512, 1024)
MOE_FFN_TILES = (896, 512, 256)
MOE_VMEM_LIMIT_BYTES = 56 * 1024 * 1024


def _moe_body(cnt_ref, h_ref, gate_ref, gatet_ref, tri_ref, wg_ref, wu_ref, wd_ref, g_ref, b_ref, o_ref,
              xb_ref, gather_ref, scatter_ref, xc_ref, yc_ref, acc_ref, rank_ref):
    f32 = jnp.float32
    tm = MOE_TOKENS
    i = pl.program_id(0)
    e = pl.program_id(1)
    f = pl.program_id(2)
    last_f = pl.num_programs(2) - 1

    @pl.when((e == 0) & (f == 0))
    def _():
        xb_ref[...] = h_ref[...].astype(MXU_DTYPE)
        acc_ref[...] = jnp.zeros_like(acc_ref)
        routed_rows = jnp.where(gatet_ref[...] != 0.0, 1.0, 0.0).astype(MXU_DTYPE)
        rank_ref[...] = jnp.dot(routed_rows, tri_ref[...], preferred_element_type=f32)

    def gate_column():
        lane = lax.broadcasted_iota(jnp.int32, gate_ref.shape, 1)
        return jnp.sum(jnp.where(lane == e, gate_ref[...], 0.0), axis=-1, keepdims=True)

    def routed(size):
        def run():
            @pl.when(f == 0)
            def _():
                active_row = jnp.where(gatet_ref[pl.ds(e, 1), :] != 0.0, 1.0, 0.0)
                rank_row = rank_ref[pl.ds(e, 1), :]
                slot = lax.broadcasted_iota(jnp.int32, (size, tm), 0).astype(f32)
                gather = jnp.where(slot == rank_row, active_row, 0.0)
                gather_ref[0:size, :] = gather.astype(MXU_DTYPE)
                rank_col = jnp.broadcast_to(rank_row, (LANE, tm)).T[:, 0:1]
                slot_l = lax.broadcasted_iota(jnp.int32, (tm, size), 1).astype(f32)
                scatter = jnp.where(slot_l == rank_col, jnp.where(gate_column() != 0.0, 1.0, 0.0), 0.0)
                scatter_ref[:, 0:size] = scatter.astype(MXU_DTYPE)
                xc_ref[0:size, :] = jnp.dot(gather.astype(MXU_DTYPE), xb_ref[...],
                                            preferred_element_type=f32).astype(MXU_DTYPE)
                yc_ref[0:size, :] = jnp.zeros((size, D_MODEL), f32)

            xc = xc_ref[0:size, :]
            a = jnp.dot(xc, wg_ref[0], preferred_element_type=f32)
            u = jnp.dot(xc, wu_ref[0], preferred_element_type=f32)
            act = a * jax.nn.sigmoid(a) * u
            yc_ref[0:size, :] += jnp.dot(act.astype(MXU_DTYPE), wd_ref[0], preferred_element_type=f32)

            @pl.when(f == last_f)
            def _():
                back = jnp.dot(scatter_ref[:, 0:size], yc_ref[0:size, :].astype(MXU_DTYPE),
                               preferred_element_type=f32)
                acc_ref[...] += gate_column() * back
        return run

    count = cnt_ref[i * N_EXPERTS + e]
    size_class = sum((count > s).astype(jnp.int32) for s in MOE_ROW_CLASSES[:-1])
    for k, size in enumerate(MOE_ROW_CLASSES):
        pl.when((size_class == k) & (count > 0))(routed(size))

    @pl.when((e == pl.num_programs(1) - 1) & (f == last_f))
    def _():
        o_ref[...] = _layer_norm_rows(DEEPNORM_ALPHA * h_ref[...] + acc_ref[...], g_ref[...], b_ref[...])


def moe_experts_norm(h, routing, wg, wu, wd, ln_g, ln_b):
    gates, gates_t, counts = routing
    m, d = h.shape
    n_experts, _, f_dim = wg.shape
    tm = MOE_TOKENS
    tf = next(t for t in MOE_FFN_TILES if f_dim % t == 0)
    assert m % tm == 0 and MOE_ROW_CLASSES[-1] == tm
    n_tiles = m // tm
    tri = jnp.triu(jnp.ones((tm, tm), MXU_DTYPE), k=1)
    row = lambda i, e, f, c: (i, 0)
    full = lambda i, e, f, c: (0, 0)
    grid_spec = pltpu.PrefetchScalarGridSpec(
        num_scalar_prefetch=1,
        grid=(n_tiles, n_experts, f_dim // tf),
        in_specs=[pl.BlockSpec((tm, d), row), pl.BlockSpec((tm, LANE), row),
                  pl.BlockSpec((n_experts, tm), lambda i, e, f, c: (0, i)),
                  pl.BlockSpec((tm, tm), full),
                  pl.BlockSpec((1, d, tf), lambda i, e, f, c: (e, 0, f)),
                  pl.BlockSpec((1, d, tf), lambda i, e, f, c: (e, 0, f)),
                  pl.BlockSpec((1, tf, d), lambda i, e, f, c: (e, f, 0)),
                  pl.BlockSpec((1, d), full), pl.BlockSpec((1, d), full)],
        out_specs=pl.BlockSpec((tm, d), row),
        scratch_shapes=[pltpu.VMEM((tm, d), MXU_DTYPE),
                        pltpu.VMEM((tm, tm), MXU_DTYPE),
                        pltpu.VMEM((tm, tm), MXU_DTYPE),
                        pltpu.VMEM((tm, d), MXU_DTYPE),
                        pltpu.VMEM((tm, d), jnp.float32),
                        pltpu.VMEM((tm, d), jnp.float32),
                        pltpu.VMEM((n_experts, tm), jnp.float32)])
    return pl.pallas_call(
        _moe_body,
        grid_spec=grid_spec,
        out_shape=jax.ShapeDtypeStruct((m, d), jnp.float32),
        compiler_params=pltpu.CompilerParams(
            dimension_semantics=("parallel", "arbitrary", "arbitrary"),
            vmem_limit_bytes=MOE_VMEM_LIMIT_BYTES),
        name="moe_experts_norm",
    )(counts, h, gates, gates_t, tri, wg, wu, wd, ln_g.reshape(1, d), ln_b.reshape(1, d))


def _router_body(h_ref, r_ref, gate_ref, gatet_ref, cnt_ref):
    f32 = jnp.float32
    logits = jnp.dot(h_ref[...], r_ref[...], preferred_element_type=f32)
    lane = lax.broadcasted_iota(jnp.int32, logits.shape, 1)
    logits = jnp.where(lane < N_EXPERTS, logits, -jnp.inf)
    top1 = jnp.max(logits, axis=1, keepdims=True)
    idx1 = jnp.min(jnp.where(logits == top1, lane, LANE), axis=1, keepdims=True)
    rest = jnp.where(lane == idx1, -jnp.inf, logits)
    top2 = jnp.max(rest, axis=1, keepdims=True)
    idx2 = jnp.min(jnp.where(rest == top2, lane, LANE), axis=1, keepdims=True)
    e2 = jnp.exp(top2 - top1)
    gates = jnp.where(lane == idx1, 1.0 / (1.0 + e2), jnp.where(lane == idx2, e2 / (1.0 + e2), 0.0))
    gate_ref[...] = gates
    gatet_ref[...] = gates.T[0:N_EXPERTS, :]
    cnt_ref[0] = jnp.sum(jnp.where(gates != 0.0, 1.0, 0.0), axis=0, keepdims=True)


def router_gates(h, router):
    m, d = h.shape
    tm = MOE_TOKENS
    assert TOP_K == 2 and m % tm == 0
    r_pad = jnp.pad(router.astype(jnp.float32), ((0, 0), (0, LANE - N_EXPERTS)))
    gates, gates_t, counts = pl.pallas_call(
        _router_body,
        grid=(m // tm,),
        in_specs=[pl.BlockSpec((tm, d), lambda i: (i, 0)), pl.BlockSpec((d, LANE), lambda i: (0, 0))],
        out_specs=[pl.BlockSpec((tm, LANE), lambda i: (i, 0)), pl.BlockSpec((N_EXPERTS, tm), lambda i: (0, i)),
                   pl.BlockSpec((1, 1, LANE), lambda i: (i, 0, 0))],
        out_shape=[jax.ShapeDtypeStruct((m, LANE), jnp.float32),
                   jax.ShapeDtypeStruct((N_EXPERTS, m), jnp.float32),
                   jax.ShapeDtypeStruct((m // tm, 1, LANE), jnp.float32)],
        compiler_params=pltpu.CompilerParams(
            dimension_semantics=("parallel",), vmem_limit_bytes=VMEM_LIMIT_BYTES),
        name="router_gates",
    )(h, r_pad)
    return gates, gates_t, counts[:, 0, :N_EXPERTS].astype(jnp.int32).reshape(-1)


def kernel(x, w_in, ssm_log_dt, ssm_lambda_re, ssm_lambda_im, ssm_b_re, ssm_b_im, ssm_c_re, ssm_c_im,
           ssm_d, ssm_w_glu, ssm_b_glu, ssm_w_out, hgrn_lb_logits, hgrn_norm_g, hgrn_w_out,
           attn_q_norm_g, attn_kv_norm_g, attn_w_uq, attn_w_qidx, attn_w_ukv, attn_w_out, w_o,
           ln_g, ln_b, ffn_w_gate, ffn_w_up, ffn_w_down, moe_router, moe_w_gate, moe_w_up, moe_w_down):
    bsz, seq, d = x.shape
    m = bsz * seq
    bf16 = MXU_DTYPE
    assert MIX_IN_USED + N_BRANCHES * D_MODEL == N_IN
    lb_soft = jax.nn.softmax(hgrn_lb_logits.astype(jnp.float32), axis=0)
    lower_bounds = jnp.concatenate([jnp.zeros_like(lb_soft[:1]), jnp.cumsum(lb_soft[1:], axis=0)], axis=0)
    h = x.reshape(m, d)
    for l in range(DEPTH):
        w_mix = jnp.pad(w_in[l][:, :MIX_IN_USED], ((0, 0), (0, MIX_IN_WIDTH - MIX_IN_USED))).astype(bf16)
        u, hg_in, att_in = in_proj(h, w_mix)
        y_ssm = s5_branch(u.reshape(bsz, seq, SSM_WIDTH), ssm_log_dt[l], ssm_lambda_re[l], ssm_lambda_im[l],
                          ssm_b_re[l], ssm_b_im[l], ssm_c_re[l], ssm_c_im[l], ssm_d[l], ssm_w_glu[l],
                          ssm_b_glu[l], ssm_w_out[l])
        y_hg = hgrn2_branch(hg_in.reshape(bsz, seq, HG_IN_WIDTH), lower_bounds[l], hgrn_norm_g[l],
                            hgrn_w_out[l])
        y_att = dsa_branch(att_in.reshape(bsz, seq, ATT_IN_WIDTH), attn_q_norm_g[l], attn_kv_norm_g[l],
                           attn_w_uq[l], attn_w_qidx[l], attn_w_ukv[l], attn_w_out[l])
        h = merge_project_norm(h, y_ssm.reshape(m, d), y_hg.reshape(m, d), y_att.reshape(m, d),
                               w_in[l][:, MIX_IN_USED:].astype(bf16), w_o[l].astype(bf16),
                               ln_g[l, 0], ln_b[l, 0])
        if l % 2 == 0:
            h = swiglu_norm(h, ffn_w_gate[l // 2].astype(bf16), ffn_w_up[l // 2].astype(bf16),
                            ffn_w_down[l // 2].astype(bf16), ln_g[l, 1], ln_b[l, 1])
        else:
            gate_w = router_gates(h, moe_router[l // 2])
            h = moe_experts_norm(h, gate_w, as_mxu_operand(moe_w_gate[l // 2]),
                                 as_mxu_operand(moe_w_up[l // 2]), as_mxu_operand(moe_w_down[l // 2]),
                                 ln_g[l, 1], ln_b[l, 1])
    return h.reshape(bsz, seq, d)
```

```python
import functools

import jax
import jax.numpy as jnp
import numpy as np
from jax import lax
from jax.experimental import pallas as pl
from jax.experimental.pallas import tpu as pltpu

D_MODEL = 1024
DEPTH = 2
SSM_WIDTH = 256
SSM_GROUP = 16
SSM_GROUPS = SSM_WIDTH // SSM_GROUP
SSM_STATE = 64
HGRN_HEADS = 4
HGRN_DK = 64
HGRN_DV = 64
HGRN_WIDTH = HGRN_HEADS * HGRN_DV
ATT_HEADS = 8
ATT_HEAD_DIM = 64
ATT_WIDTH = ATT_HEADS * ATT_HEAD_DIM
ATT_Q_RANK = 256
ATT_KV_RANK = 128
IDX_HEADS = 4
IDX_DIM = 64
TOPK_MAX = 256
MASK_VALUE = -1e30
N_BRANCHES = 3
N_EXPERTS = 8
TOP_K = 2
DEEPNORM_ALPHA = (2 * DEPTH) ** 0.25
LN_EPS = 1e-5
RMS_EPS = 1e-6
F_MIN = 1e-12

IN_SPLITS = (SSM_WIDTH, HGRN_HEADS * HGRN_DK, HGRN_HEADS * HGRN_DK, HGRN_WIDTH, HGRN_WIDTH,
             ATT_Q_RANK, ATT_KV_RANK, IDX_DIM, IDX_HEADS, N_BRANCHES * D_MODEL)
N_IN = sum(IN_SPLITS)

VMEM_LIMIT_BYTES = 48 * 1024 * 1024
LANE = 128
MXU_DTYPE = jnp.bfloat16


def _round_up(n, m):
    return (n + m - 1) // m * m


CAST_ROWS = 512


def _cast_body(x_ref, o_ref):
    o_ref[...] = x_ref[0].astype(o_ref.dtype)


def as_mxu_operand(w, layer):
    _, n, rows, cols = w.shape
    assert rows % CAST_ROWS == 0
    return pl.pallas_call(
        _cast_body,
        grid=(n, rows // CAST_ROWS),
        in_specs=[pl.BlockSpec((1, 1, CAST_ROWS, cols), lambda e, r: (layer, e, r, 0))],
        out_specs=pl.BlockSpec((1, CAST_ROWS, cols), lambda e, r: (e, r, 0)),
        out_shape=jax.ShapeDtypeStruct((n, rows, cols), MXU_DTYPE),
        compiler_params=pltpu.CompilerParams(
            dimension_semantics=("parallel", "parallel"), vmem_limit_bytes=VMEM_LIMIT_BYTES),
        name="as_mxu_operand",
    )(w)


HG_IN_WIDTH = 2 * HGRN_HEADS * HGRN_DK + 2 * HGRN_WIDTH
ATT_IN_USED = ATT_Q_RANK + ATT_KV_RANK + IDX_DIM + IDX_HEADS
ATT_IN_WIDTH = _round_up(ATT_IN_USED, LANE)
ATT_IDX_BLOCK = (ATT_Q_RANK + ATT_KV_RANK) // LANE
MIX_IN_USED = SSM_WIDTH + HG_IN_WIDTH + ATT_IN_USED
MIX_IN_WIDTH = SSM_WIDTH + HG_IN_WIDTH + ATT_IN_WIDTH


def _in_proj_body(h_ref, w_ref, u_ref, hg_ref, att_ref):
    r = jnp.dot(h_ref[...].astype(MXU_DTYPE), w_ref[...], preferred_element_type=jnp.float32)
    u_ref[...] = r[:, :SSM_WIDTH]
    hg_ref[...] = r[:, SSM_WIDTH:SSM_WIDTH + HG_IN_WIDTH]
    att_ref[...] = r[:, SSM_WIDTH + HG_IN_WIDTH:]


def in_proj(h, w_mix, *, tm=512):
    m, d = h.shape
    row = lambda i: (i, 0)
    widths = (SSM_WIDTH, HG_IN_WIDTH, ATT_IN_WIDTH)
    return pl.pallas_call(
        _in_proj_body,
        grid=(m // tm,),
        in_specs=[pl.BlockSpec((tm, d), row), pl.BlockSpec((d, MIX_IN_WIDTH), lambda i: (0, 0))],
        out_specs=[pl.BlockSpec((tm, w), row) for w in widths],
        out_shape=[jax.ShapeDtypeStruct((m, w), jnp.float32) for w in widths],
        compiler_params=pltpu.CompilerParams(
            dimension_semantics=("parallel",), vmem_limit_bytes=VMEM_LIMIT_BYTES),
        name="in_proj",
    )(h, w_mix)


def _layer_norm_rows(y, g, b):
    mu = jnp.mean(y, axis=-1, keepdims=True)
    yc = y - mu
    var = jnp.mean(yc * yc, axis=-1, keepdims=True)
    return yc * lax.rsqrt(var + LN_EPS) * g + b


def _merge_body(h_ref, ys_ref, yh_ref, ya_ref, wgate_ref, wo_ref, g_ref, b_ref, o_ref):
    d = D_MODEL
    f32 = jnp.float32
    h = h_ref[...]
    gates = jax.nn.sigmoid(jnp.dot(h.astype(MXU_DTYPE), wgate_ref[...], preferred_element_type=f32))
    mixed = (gates[:, 0:d] * ys_ref[...] + gates[:, d:2 * d] * yh_ref[...]
             + gates[:, 2 * d:3 * d] * ya_ref[...])
    mix_out = jnp.dot(mixed.astype(MXU_DTYPE), wo_ref[...], preferred_element_type=f32)
    o_ref[...] = _layer_norm_rows(DEEPNORM_ALPHA * h + mix_out, g_ref[...], b_ref[...])


def merge_project_norm(h, y_ssm, y_hg, y_att, w_gates, w_o, ln_g, ln_b, *, tm=512):
    m, d = h.shape
    row = lambda i: (i, 0)
    full = lambda i: (0, 0)
    return pl.pallas_call(
        _merge_body,
        grid=(m // tm,),
        in_specs=[pl.BlockSpec((tm, d), row), pl.BlockSpec((tm, d), row), pl.BlockSpec((tm, d), row),
                  pl.BlockSpec((tm, d), row), pl.BlockSpec((d, N_BRANCHES * d), full),
                  pl.BlockSpec((d, d), full), pl.BlockSpec((1, d), full), pl.BlockSpec((1, d), full)],
        out_specs=pl.BlockSpec((tm, d), row),
        out_shape=jax.ShapeDtypeStruct((m, d), jnp.float32),
        compiler_params=pltpu.CompilerParams(
            dimension_semantics=("parallel",), vmem_limit_bytes=VMEM_LIMIT_BYTES),
        name="merge_project_norm",
    )(h, y_ssm, y_hg, y_att, w_gates, w_o, ln_g.reshape(1, d), ln_b.reshape(1, d))


DENSE_FFN_TOKENS = 512
DENSE_FFN_TILES = (1408, 512, 256)


def _ffn_body(h_ref, wg_ref, wu_ref, wd_ref, g_ref, b_ref, o_ref, acc_ref):
    f = pl.program_id(1)

    @pl.when(f == 0)
    def _():
        acc_ref[...] = jnp.zeros_like(acc_ref)

    x = h_ref[...].astype(MXU_DTYPE)
    a = jnp.dot(x, wg_ref[...], preferred_element_type=jnp.float32)
    u = jnp.dot(x, wu_ref[...], preferred_element_type=jnp.float32)
    act = a * jax.nn.sigmoid(a) * u
    acc_ref[...] += jnp.dot(act.astype(MXU_DTYPE), wd_ref[...], preferred_element_type=jnp.float32)

    @pl.when(f == pl.num_programs(1) - 1)
    def _():
        o_ref[...] = _layer_norm_rows(DEEPNORM_ALPHA * h_ref[...] + acc_ref[...], g_ref[...], b_ref[...])


def swiglu_norm(h, wg, wu, wd, ln_g, ln_b):
    m, d = h.shape
    f_dim = wg.shape[1]
    tm = DENSE_FFN_TOKENS
    tf = next(t for t in DENSE_FFN_TILES if f_dim % t == 0)
    assert m % tm == 0
    row = lambda i, f: (i, 0)
    full = lambda i, f: (0, 0)
    return pl.pallas_call(
        _ffn_body,
        grid=(m // tm, f_dim // tf),
        in_specs=[pl.BlockSpec((tm, d), row),
                  pl.BlockSpec((d, tf), lambda i, f: (0, f)), pl.BlockSpec((d, tf), lambda i, f: (0, f)),
                  pl.BlockSpec((tf, d), lambda i, f: (f, 0)),
                  pl.BlockSpec((1, d), full), pl.BlockSpec((1, d), full)],
        out_specs=pl.BlockSpec((tm, d), row),
        out_shape=jax.ShapeDtypeStruct((m, d), jnp.float32),
        scratch_shapes=[pltpu.VMEM((tm, d), jnp.float32)],
        compiler_params=pltpu.CompilerParams(
            dimension_semantics=("parallel", "arbitrary"), vmem_limit_bytes=VMEM_LIMIT_BYTES),
        name="swiglu_norm",
    )(h, wg, wu, wd, ln_g.reshape(1, d), ln_b.reshape(1, d))


S5_CHUNK = 8
S5_CHUNK_WIDTH = S5_CHUNK * SSM_WIDTH
S5_STATE_WIDTH = SSM_GROUPS * SSM_STATE


def s5_operators(log_dt, lam_re, lam_im, b_re, b_im, c_re, c_im):
    f32 = jnp.float32
    n = S5_CHUNK
    lre, lim = lam_re.astype(f32), lam_im.astype(f32)
    dt = jnp.exp(log_dt.astype(f32))[:, None]
    mag = jnp.exp(lre * dt)
    ang = lim * dt
    a_re, a_im = mag * jnp.cos(ang), mag * jnp.sin(ang)
    den = lre * lre + lim * lim
    coef_re = ((a_re - 1.0) * lre + a_im * lim) / den
    coef_im = (a_im * lre - (a_re - 1.0) * lim) / den
    br, bi = b_re.astype(f32), b_im.astype(f32)
    bb_re = coef_re[..., None] * br - coef_im[..., None] * bi
    bb_im = coef_re[..., None] * bi + coef_im[..., None] * br
    j = jnp.arange(n + 1, dtype=f32)[:, None, None]
    pw_re = jnp.exp(j * lre * dt) * jnp.cos(j * ang)
    pw_im = jnp.exp(j * lre * dt) * jnp.sin(j * ang)
    eye = jnp.eye(SSM_GROUPS, dtype=f32)
    cr, ci = c_re.astype(f32), c_im.astype(f32)
    ab_re = pw_re[..., None] * bb_re - pw_im[..., None] * bb_im
    ab_im = pw_re[..., None] * bb_im + pw_im[..., None] * bb_re
    kern = (jnp.einsum('gcp,jgpd->jgcd', cr, ab_re[:n]) - jnp.einsum('gcp,jgpd->jgcd', ci, ab_im[:n]))
    lag = jnp.arange(n)[None, :] - jnp.arange(n)[:, None]
    toep = jnp.where((lag >= 0)[:, :, None, None, None], kern[jnp.maximum(lag, 0)], 0.0)
    t_mat = jnp.einsum('abgcd,gh->agdbhc', toep, eye).reshape(S5_CHUNK_WIDTH, S5_CHUNK_WIDTH)
    v = jnp.stack([ab_re[:n][::-1], ab_im[:n][::-1]])
    w_in = jnp.einsum('rlgpc,gh->lhcrgp', v, eye).reshape(S5_CHUNK_WIDTH, 2 * S5_STATE_WIDTH)
    ar, ai = pw_re[1:], pw_im[1:]
    wo_re = jnp.einsum('gcp,lgp->gplc', cr, ar) - jnp.einsum('gcp,lgp->gplc', ci, ai)
    wo_im = -jnp.einsum('gcp,lgp->gplc', cr, ai) - jnp.einsum('gcp,lgp->gplc', ci, ar)
    w_out = jnp.einsum('rgplc,gh->rhplgc', jnp.stack([wo_re, wo_im]), eye)
    w_out = w_out.reshape(2 * S5_STATE_WIDTH, S5_CHUNK_WIDTH)
    decay = jnp.stack([pw_re[n].reshape(1, S5_STATE_WIDTH), pw_im[n].reshape(1, S5_STATE_WIDTH)])
    return (jnp.concatenate([t_mat, w_in], axis=1).astype(MXU_DTYPE), w_out.astype(MXU_DTYPE), decay)


def _s5_scan_body(u_ref, tw_ref, wout_ref, decay_ref, y_ref, h_ref, s_ref, hs_ref, *, tm):
    f32 = jnp.float32
    sw = S5_STATE_WIDTH

    @pl.when(pl.program_id(1) == 0)
    def _():
        h_ref[...] = jnp.zeros_like(h_ref)

    r = jnp.dot(u_ref[0].astype(MXU_DTYPE), tw_ref[...], preferred_element_type=f32)
    s_ref[...] = r[:, S5_CHUNK_WIDTH:]
    d_re = decay_ref[0]
    d_im = decay_ref[1]

    def eight_chunks(k, carry):
        h_re, h_im = carry
        r0 = pl.multiple_of(k * 8, 8)
        inc = s_ref[pl.ds(r0, 8), :]
        rows_re, rows_im = [], []
        for t in range(8):
            rows_re.append(h_re)
            rows_im.append(h_im)
            h_re, h_im = (d_re * h_re - d_im * h_im + inc[t:t + 1, :sw],
                          d_re * h_im + d_im * h_re + inc[t:t + 1, sw:])
        hs_ref[pl.ds(r0, 8), :] = jnp.concatenate(
            [jnp.concatenate(rows_re, axis=0), jnp.concatenate(rows_im, axis=0)], axis=1)
        return h_re, h_im

    h_re, h_im = lax.fori_loop(0, tm // 8, eight_chunks, (h_ref[0:1, :], h_ref[1:2, :]))
    h_ref[0:1, :] = h_re
    h_ref[1:2, :] = h_im
    y_ref[0] = r[:, :S5_CHUNK_WIDTH] + jnp.dot(hs_ref[...].astype(MXU_DTYPE), wout_ref[...],
                                                preferred_element_type=f32)


def s5_scan(u, tw, w_out, decay, *, tm=256):
    bsz, seq, _ = u.shape
    n_chunks = seq // S5_CHUNK
    tm = min(tm, n_chunks)
    assert seq % S5_CHUNK == 0 and n_chunks % tm == 0 and tm % 8 == 0
    once = pl.Buffered(1)
    y = pl.pallas_call(
        functools.partial(_s5_scan_body, tm=tm),
        grid=(bsz, n_chunks // tm),
        in_specs=[pl.BlockSpec((1, tm, S5_CHUNK_WIDTH), lambda b, i: (b, i, 0)),
                  pl.BlockSpec(tw.shape, lambda b, i: (0, 0), pipeline_mode=once),
                  pl.BlockSpec(w_out.shape, lambda b, i: (0, 0), pipeline_mode=once),
                  pl.BlockSpec(decay.shape, lambda b, i: (0, 0, 0), pipeline_mode=once)],
        out_specs=pl.BlockSpec((1, tm, S5_CHUNK_WIDTH), lambda b, i: (b, i, 0)),
        out_shape=jax.ShapeDtypeStruct((bsz, n_chunks, S5_CHUNK_WIDTH), jnp.float32),
        scratch_shapes=[pltpu.VMEM((2, S5_STATE_WIDTH), jnp.float32),
                        pltpu.VMEM((tm, 2 * S5_STATE_WIDTH), jnp.float32),
                        pltpu.VMEM((tm, 2 * S5_STATE_WIDTH), jnp.float32)],
        compiler_params=pltpu.CompilerParams(
            dimension_semantics=("parallel", "arbitrary"), vmem_limit_bytes=VMEM_LIMIT_BYTES),
        name="s5_scan",
    )(u.reshape(bsz, n_chunks, S5_CHUNK_WIDTH), tw, w_out, decay)
    return y.reshape(bsz, seq, SSM_WIDTH)


def _s5_out_body(y_ref, u_ref, d_ref, wglu_ref, bglu_ref, wout_ref, o_ref):
    f32 = jnp.float32
    y = jax.nn.gelu(y_ref[...] + d_ref[...] * u_ref[...])
    gate = jnp.dot(y.astype(MXU_DTYPE), wglu_ref[...], preferred_element_type=f32) + bglu_ref[...]
    y = y * jax.nn.sigmoid(gate)
    o_ref[...] = jnp.dot(y.astype(MXU_DTYPE), wout_ref[...], preferred_element_type=f32)


def s5_output(y, u, d_skip, w_glu, b_glu, w_out, *, tm=1024):
    m, c = y.shape
    tm = min(tm, m)
    row = lambda i: (i, 0)
    full = lambda i: (0, 0)
    return pl.pallas_call(
        _s5_out_body,
        grid=(m // tm,),
        in_specs=[pl.BlockSpec((tm, c), row), pl.BlockSpec((tm, c), row), pl.BlockSpec((1, c), full),
                  pl.BlockSpec((c, c), full), pl.BlockSpec((1, c), full), pl.BlockSpec((c, D_MODEL), full)],
        out_specs=pl.BlockSpec((tm, D_MODEL), row),
        out_shape=jax.ShapeDtypeStruct((m, D_MODEL), jnp.float32),
        compiler_params=pltpu.CompilerParams(
            dimension_semantics=("parallel",), vmem_limit_bytes=VMEM_LIMIT_BYTES),
        name="s5_output",
    )(y, u, d_skip.reshape(1, c), w_glu.astype(MXU_DTYPE), b_glu.reshape(1, c), w_out.astype(MXU_DTYPE))


def s5_branch(u, log_dt, lam_re, lam_im, b_re, b_im, c_re, c_im, d_skip, w_glu, b_glu, w_out):
    bsz, seq, _ = u.shape
    tw, w_state_out, decay = s5_operators(log_dt, lam_re, lam_im, b_re, b_im, c_re, c_im)
    y = s5_scan(u, tw, w_state_out, decay)
    out = s5_output(y.reshape(bsz * seq, SSM_WIDTH), u.reshape(bsz * seq, SSM_WIDTH),
                    d_skip, w_glu, b_glu, w_out)
    return out.reshape(bsz, seq, D_MODEL)


HG_CHUNK = 128
HG_LEVELS = 7
HG_KDIM = HGRN_HEADS * HGRN_DK


def _hgrn_segment_sums():
    c = HG_CHUNK
    t = np.arange(c)[:, None]
    u = np.arange(c)[None, :]
    blocks = []
    for lvl in range(1, HG_LEVELS + 1):
        m = (t >> lvl << lvl) + (1 << (lvl - 1)) - 1
        right = ((t >> (lvl - 1)) & 1) == 1
        blocks.append(np.where(right, (u > m) & (u <= t), (u > t) & (u <= m)))
    blocks.append(u <= t)
    blocks.append(u > t)
    return np.concatenate(blocks, axis=0).astype(np.float32)


def _hgrn_body(q_ref, z_ref, v_ref, g_ref, seg_ref, lb_ref, ng_ref, hmean_ref, wout_ref, o_ref, st_ref):
    f32 = jnp.float32
    c = HG_CHUNK

    @pl.when(pl.program_id(1) == 0)
    def _():
        st_ref[...] = jnp.zeros_like(st_ref)

    q = q_ref[0]
    z = z_ref[0]
    v = v_ref[0]
    lb = lb_ref[...]
    f = lb + (1.0 - lb) * jax.nn.sigmoid(z)
    logf = jnp.log(jnp.maximum(f, F_MIN))
    kin = (1.0 - lb) * jax.nn.sigmoid(-z)

    p1 = logf.astype(MXU_DTYPE)
    r1 = logf - p1.astype(f32)
    p2 = r1.astype(MXU_DTYPE)
    p3 = (r1 - p2.astype(f32)).astype(MXU_DTYPE)
    seg = seg_ref[...]
    sums = (jnp.dot(seg, p1, preferred_element_type=f32) + jnp.dot(seg, p2, preferred_element_type=f32)
            + jnp.dot(seg, p3, preferred_element_type=f32))

    lane_head = lax.broadcasted_iota(jnp.int32, (c, HG_KDIM), 1) // HGRN_DK
    tok = lax.broadcasted_iota(jnp.int32, (c, HG_KDIM), 0)
    row_t = lax.broadcasted_iota(jnp.int32, (HGRN_HEADS * c, c), 0) % c
    col_s = lax.broadcasted_iota(jnp.int32, (HGRN_HEADS * c, c), 1)

    def per_head_rows(x):
        return jnp.concatenate([jnp.where(lane_head == h, x, 0.0) for h in range(HGRN_HEADS)],
                               axis=0).astype(MXU_DTYPE)

    def scores(ql, kl):
        return lax.dot_general(per_head_rows(ql), kl.astype(MXU_DTYPE), (((1,), (1,)), ((), ())),
                               preferred_element_type=f32)

    att = jnp.where(row_t == col_s, scores(q, kin), 0.0)
    for lvl in range(1, HG_LEVELS + 1):
        decay = jnp.exp(sums[(lvl - 1) * c:lvl * c])
        right = ((tok >> (lvl - 1)) & 1) == 1
        a = scores(jnp.where(right, q * decay, 0.0), jnp.where(right, 0.0, kin * decay))
        att = att + jnp.where((row_t >> lvl) == (col_s >> lvl), a, 0.0)

    b = sums[HG_LEVELS * c:(HG_LEVELS + 1) * c]
    tail = sums[(HG_LEVELS + 1) * c:(HG_LEVELS + 2) * c]
    v_m = v.astype(MXU_DTYPE)
    st = st_ref[...]
    o = lax.dot_general((q * jnp.exp(b)).astype(MXU_DTYPE), st.astype(MXU_DTYPE),
                        (((1,), (1,)), ((), ())), preferred_element_type=f32)
    for h in range(HGRN_HEADS):
        o_h = jnp.dot(att[h * c:(h + 1) * c].astype(MXU_DTYPE), v_m, preferred_element_type=f32)
        o = o + jnp.where(lane_head == h, o_h, 0.0)

    kv = jnp.dot(v.T.astype(MXU_DTYPE), (kin * jnp.exp(tail)).astype(MXU_DTYPE), preferred_element_type=f32)
    sr = lax.broadcasted_iota(jnp.int32, st.shape, 0) // HGRN_DV
    sc = lax.broadcasted_iota(jnp.int32, st.shape, 1) // HGRN_DK
    st_ref[...] = st * jnp.exp(b[c - 1:c, :]) + jnp.where(sr == sc, kv, 0.0)

    o2 = o * o
    o2_hi = o2.astype(MXU_DTYPE)
    o2_lo = (o2 - o2_hi.astype(f32)).astype(MXU_DTYPE)
    ms = (jnp.dot(o2_hi, hmean_ref[...], preferred_element_type=f32)
          + jnp.dot(o2_lo, hmean_ref[...], preferred_element_type=f32))
    g = g_ref[0]
    out = o * lax.rsqrt(ms + RMS_EPS) * ng_ref[...] * (g * jax.nn.sigmoid(g))
    o_ref[0] = jnp.dot(out.astype(MXU_DTYPE), wout_ref[...], preferred_element_type=f32)


def hgrn2_branch(hg_in, lower_bound, norm_g, w_out):
    bsz, seq, _ = hg_in.shape
    assert seq % HG_CHUNK == 0 and HGRN_DK == HGRN_DV and HG_IN_WIDTH == 4 * HG_KDIM
    seg = jnp.asarray(_hgrn_segment_sums(), MXU_DTYPE)
    head_mean = jnp.asarray(np.kron(np.eye(HGRN_HEADS), np.full((HGRN_DV, HGRN_DV), 1.0 / HGRN_DV)), MXU_DTYPE)
    tok = lambda b, i: (b, i, 0)
    full = lambda b, i: (0, 0)
    part = lambda k: pl.BlockSpec((1, HG_CHUNK, HG_KDIM), lambda b, i: (b, i, k))
    return pl.pallas_call(
        _hgrn_body,
        grid=(bsz, seq // HG_CHUNK),
        in_specs=[part(0), part(1), part(2), part(3),
                  pl.BlockSpec(seg.shape, full), pl.BlockSpec((1, HG_KDIM), full),
                  pl.BlockSpec((1, HGRN_WIDTH), full), pl.BlockSpec(head_mean.shape, full),
                  pl.BlockSpec((HGRN_WIDTH, D_MODEL), full)],
        out_specs=pl.BlockSpec((1, HG_CHUNK, D_MODEL), tok),
        out_shape=jax.ShapeDtypeStruct((bsz, seq, D_MODEL), jnp.float32),
        scratch_shapes=[pltpu.VMEM((HGRN_WIDTH, HG_KDIM), jnp.float32)],
        compiler_params=pltpu.CompilerParams(
            dimension_semantics=("parallel", "arbitrary"), vmem_limit_bytes=VMEM_LIMIT_BYTES),
        name="hgrn2",
    )(hg_in, hg_in, hg_in, hg_in, seg, lower_bound.reshape(1, HG_KDIM).astype(jnp.float32),
      jnp.tile(norm_g.astype(jnp.float32), HGRN_HEADS).reshape(1, HGRN_WIDTH), head_mean,
      w_out.astype(MXU_DTYPE))


ATT_QK_DIM = LANE
DSA_TILE = 512


def _rms_rows(x, g):
    return x * lax.rsqrt(jnp.mean(x * x, axis=-1, keepdims=True) + RMS_EPS) * g


def _dsa_project_body(x_ref, gq_ref, gkv_ref, wq_ref, wqi_ref, wkt_ref, wv_ref,
                      q_ref, qi_ref, kt_ref, v_ref, kidt_ref):
    f32 = jnp.float32
    tm = x_ref.shape[1]
    x = x_ref[0]
    cq = _rms_rows(x[:, :ATT_Q_RANK], gq_ref[...]).astype(MXU_DTYPE)
    ckv = _rms_rows(x[:, ATT_Q_RANK:ATT_Q_RANK + ATT_KV_RANK], gkv_ref[...])
    q_all = jnp.dot(cq, wq_ref[...], preferred_element_type=f32)
    qi_all = jnp.dot(cq, wqi_ref[...], preferred_element_type=f32)
    v_all = jnp.dot(ckv.astype(MXU_DTYPE), wv_ref[...], preferred_element_type=f32)
    kt_all = jnp.dot(wkt_ref[...], ckv.T.astype(MXU_DTYPE), preferred_element_type=f32)

    lane = lax.broadcasted_iota(jnp.int32, (tm, LANE), 1)
    t_loc = lax.broadcasted_iota(jnp.int32, (tm, LANE), 0)
    t_even = (t_loc // 2 * 2).astype(f32)
    t_odd = (t_loc % 2).astype(f32)
    sub = lax.broadcasted_iota(jnp.int32, (LANE, tm), 0)
    s_loc = lax.broadcasted_iota(jnp.int32, (LANE, tm), 1)
    k_rows = jnp.where(sub < ATT_HEAD_DIM + 2, 1.0,
                       jnp.where(sub == ATT_HEAD_DIM + 2, (s_loc // 2 * 2).astype(f32), (s_loc % 2).astype(f32)))
    for h in range(ATT_HEADS):
        slope = 2.0 ** (-8.0 * (h + 1) / ATT_HEADS)
        q_cols = jnp.where(lane == ATT_HEAD_DIM, -slope * t_even,
                           jnp.where(lane == ATT_HEAD_DIM + 1, -slope * t_odd, slope))
        q_h = q_all[:, h * LANE:(h + 1) * LANE]
        q_ref[0, h] = jnp.where(lane < ATT_HEAD_DIM, q_h,
                                jnp.where(lane < ATT_HEAD_DIM + 4, q_cols, 0.0)).astype(q_ref.dtype)
        k_h = kt_all[h * LANE:(h + 1) * LANE, :]
        kt_ref[0, h] = jnp.where(sub < ATT_HEAD_DIM, k_h,
                                 jnp.where(sub < ATT_HEAD_DIM + 4, k_rows, 0.0)).astype(kt_ref.dtype)
        v_h = v_all[:, h * LANE:(h + 1) * LANE]
        v_ref[0, h] = jnp.where(lane == ATT_HEAD_DIM, 1.0, v_h).astype(v_ref.dtype)
    for h in range(IDX_HEADS):
        qi_ref[0, h] = qi_all[:, h * LANE:(h + 1) * LANE].astype(qi_ref.dtype)
    kidt_ref[0] = x[:, ATT_IDX_BLOCK * LANE:(ATT_IDX_BLOCK + 1) * LANE].T.astype(kidt_ref.dtype)


def _head_padded(w, n_heads, dim):
    k = w.shape[0]
    return jnp.pad(w.reshape(k, n_heads, dim), ((0, 0), (0, 0), (0, LANE - dim))).reshape(k, n_heads * LANE)


def dsa_project(att_in, q_norm_g, kv_norm_g, w_uq, w_qidx, w_ukv):
    bsz, seq, _ = att_in.shape
    tm = DSA_TILE
    assert seq % tm == 0 and 8 % ATT_HEADS == 0 and tm <= 512
    w_q = _head_padded(w_uq * ATT_HEAD_DIM ** -0.5, ATT_HEADS, ATT_HEAD_DIM).astype(MXU_DTYPE)
    w_qi = _head_padded(w_qidx * IDX_DIM ** -0.5, IDX_HEADS, IDX_DIM).astype(MXU_DTYPE)
    w_kt = _head_padded(w_ukv[:, :ATT_WIDTH], ATT_HEADS, ATT_HEAD_DIM).T.astype(MXU_DTYPE)
    w_v = _head_padded(w_ukv[:, ATT_WIDTH:], ATT_HEADS, ATT_HEAD_DIM).astype(MXU_DTYPE)
    tok = lambda b, i: (b, 0, i, 0)
    full = lambda b, i: (0, 0)
    dt = MXU_DTYPE
    return pl.pallas_call(
        _dsa_project_body,
        grid=(bsz, seq // tm),
        in_specs=[pl.BlockSpec((1, tm, ATT_IN_WIDTH), lambda b, i: (b, i, 0)),
                  pl.BlockSpec((1, ATT_Q_RANK), full), pl.BlockSpec((1, ATT_KV_RANK), full),
                  pl.BlockSpec(w_q.shape, full), pl.BlockSpec(w_qi.shape, full),
                  pl.BlockSpec(w_kt.shape, full), pl.BlockSpec(w_v.shape, full)],
        out_specs=[pl.BlockSpec((1, ATT_HEADS, tm, LANE), tok),
                   pl.BlockSpec((1, IDX_HEADS, tm, LANE), tok),
                   pl.BlockSpec((1, ATT_HEADS, LANE, tm), lambda b, i: (b, 0, 0, i)),
                   pl.BlockSpec((1, ATT_HEADS, tm, LANE), tok),
                   pl.BlockSpec((1, LANE, tm), lambda b, i: (b, 0, i))],
        out_shape=[jax.ShapeDtypeStruct((bsz, ATT_HEADS, seq, LANE), dt),
                   jax.ShapeDtypeStruct((bsz, IDX_HEADS, seq, LANE), dt),
                   jax.ShapeDtypeStruct((bsz, ATT_HEADS, LANE, seq), dt),
                   jax.ShapeDtypeStruct((bsz, ATT_HEADS, seq, LANE), dt),
                   jax.ShapeDtypeStruct((bsz, LANE, seq), dt)],
        compiler_params=pltpu.CompilerParams(
            dimension_semantics=("parallel", "parallel"), vmem_limit_bytes=VMEM_LIMIT_BYTES),
        name="dsa_project",
    )(att_in, q_norm_g.reshape(1, ATT_Q_RANK), kv_norm_g.reshape(1, ATT_KV_RANK), w_q, w_qi, w_kt, w_v)


_INT_MIN = -2 ** 31
_MASK_KEY = int(np.float32(MASK_VALUE).view(np.int32)) ^ 0x7FFFFFFF
_SEARCH_ROWS = 128
_ATT_ROWS = 32
_SEARCH_LANES = 512
_LIST_DEPTH = 12
_LIST_ROWS = 16
_LIST_MIN_BLOCKS = 4


def _key_to_score(key):
    return lax.bitcast_convert_type(jnp.where(key < 0, key ^ 0x7FFFFFFF, key), jnp.float32)


def _dsa_body(qi_ref, kj_ref, qidx_ref, w_ref, kidt_ref, q_ref, kt_ref, v_ref, tri_ref, wout_ref,
              o_ref, keys_ref, thr_ref, need_ref, carry_ref, m_ref, acc_ref, bias_ref, s_ref, p_ref,
              alpha_ref, cand_ref, *, tq, tk, seq, n_sel):
    f32 = jnp.float32
    p_id = pl.program_id(1)
    i = qi_ref[p_id]
    j = kj_ref[p_id]
    row_local = lax.broadcasted_iota(jnp.int32, (tq, tk), 0)
    col_local = lax.broadcasted_iota(jnp.int32, (tq, tk), 1)

    @pl.when(j == 0)
    def _select():
        def score_block(jj, carry):
            off = pl.multiple_of(jj * tk, tk)
            kb = kidt_ref[0, :, pl.ds(off, tk)]
            sc = jnp.zeros((tq, tk), f32)
            for h in range(IDX_HEADS):
                s = jnp.dot(qidx_ref[0, h], kb, preferred_element_type=f32)
                w_h = w_ref[0, :, IDX_DIM + h:IDX_DIM + h + 1] * IDX_HEADS ** -0.5
                sc = sc + jnp.maximum(s, 0.0) * w_h
            causal = (col_local + jj * tk) <= (row_local + i * tq)
            sc = jnp.where(causal, sc, MASK_VALUE)
            keys_ref[:, pl.ds(off, tk)] = jnp.where(sc == 0.0, 0.0, sc)
            return carry

        lax.fori_loop(0, i + 1, score_block, 0)

        n_blocks = i + 1
        n_masked_tail = (seq - n_blocks * tk).astype(f32)

        def row_group(r, carry):
            r0 = pl.multiple_of(r * _SEARCH_ROWS, _SEARCH_ROWS)

            def count_ge(ref, n_iter, cand):
                cand_b = jnp.broadcast_to(_key_to_score(cand), (_SEARCH_ROWS, LANE))

                def chunk(c, acc):
                    base = pl.multiple_of(c * _SEARCH_LANES, _SEARCH_LANES)
                    for u in range(_SEARCH_LANES // LANE):
                        kk = ref[pl.ds(r0, _SEARCH_ROWS), pl.ds(base + u * LANE, LANE)]
                        acc = acc + jnp.where(kk >= cand_b, 1.0, 0.0)
                    return acc

                acc = lax.fori_loop(0, n_iter, chunk, jnp.zeros((_SEARCH_ROWS, LANE), f32))
                cnt = jnp.sum(acc, axis=1, keepdims=True)
                return cnt + jnp.where(cand <= _MASK_KEY, n_masked_tail, 0.0)

            def kth_largest_key(count):
                zero = jnp.zeros((_SEARCH_ROWS, 1), jnp.int32)
                v0 = jnp.where(count(zero) >= n_sel, zero, zero + _INT_MIN)

                def bit_step(b, v):
                    cand = v | jnp.left_shift(jnp.int32(1), 30 - b)
                    return jnp.where(count(cand) >= n_sel, cand, v)

                return lax.fori_loop(0, 31, bit_step, v0)

            count_all = functools.partial(count_ge, keys_ref, n_blocks * (tk // _SEARCH_LANES))

            def store(v, n_above):
                thr_ref[pl.ds(r0, _SEARCH_ROWS), :] = _key_to_score(v)
                need_ref[pl.ds(r0, _SEARCH_ROWS), :] = n_sel - n_above

            def search_all_keys():
                v = kth_largest_key(count_all)
                store(v, count_all(v + 1))

            @pl.when(n_blocks < _LIST_MIN_BLOCKS)
            def _():
                search_all_keys()

            @pl.when(n_blocks >= _LIST_MIN_BLOCKS)
            def _():
                def shortlist(sub, carry):
                    rr = pl.multiple_of(r0 + sub * _LIST_ROWS, _LIST_ROWS)

                    def insert_block(c, tops):
                        base = pl.multiple_of(c * tk, tk)
                        for u in range(tk // LANE):
                            x = keys_ref[pl.ds(rr, _LIST_ROWS), pl.ds(base + u * LANE, LANE)]
                            new = []
                            for t in range(_LIST_DEPTH):
                                new.append(jnp.maximum(tops[t], x))
                                x = jnp.minimum(tops[t], x)
                            tops = tuple(new)
                        return tops

                    tops = lax.fori_loop(0, n_blocks, insert_block,
                                         tuple(jnp.full((_LIST_ROWS, LANE), -jnp.inf, f32)
                                               for _ in range(_LIST_DEPTH)))
                    for t in range(_LIST_DEPTH):
                        cand_ref[pl.ds(rr, _LIST_ROWS), t * LANE:(t + 1) * LANE] = tops[t]
                    return carry

                lax.fori_loop(0, _SEARCH_ROWS // _LIST_ROWS, shortlist, 0)
                v = kth_largest_key(functools.partial(count_ge, cand_ref,
                                                      _LIST_DEPTH * LANE // _SEARCH_LANES))
                n_at_least = count_all(v)
                n_above = count_all(v + 1)
                store(v, n_above)
                exact = (n_above < n_sel) & (n_at_least >= n_sel)
                pl.when(jnp.min(jnp.where(exact, 1.0, 0.0)) < 0.5)(search_all_keys)

            return carry

        lax.fori_loop(0, tq // _SEARCH_ROWS, row_group, 0)
        carry_ref[...] = jnp.zeros_like(carry_ref)
        m_ref[...] = jnp.full_like(m_ref, MASK_VALUE)
        acc_ref[...] = jnp.zeros_like(acc_ref)

    keys_blk = keys_ref[:, pl.ds(pl.multiple_of(j * tk, tk), tk)]
    thr = thr_ref[...]
    eq = keys_blk == thr
    tie_rank = carry_ref[...] + jnp.dot(jnp.where(eq, 1.0, 0.0).astype(MXU_DTYPE), tri_ref[...],
                                         preferred_element_type=f32)
    carry_ref[...] = tie_rank[:, tk - 1:tk]
    tie_bias = jnp.where(eq, jnp.where(tie_rank <= need_ref[...], 0.0, MASK_VALUE), MASK_VALUE)
    bias = jnp.where(keys_blk > thr, 0.0, tie_bias)
    causal = (row_local + i * tq) >= (col_local + j * tk)
    bias_ref[...] = jnp.where(causal, bias, MASK_VALUE)

    def logits(h, slot):
        s_ref[slot] = jnp.dot(q_ref[0, h], kt_ref[0, h], preferred_element_type=f32)

    def softmax(h, slot):
        slope_tk = jnp.right_shift(tk, h + 1).astype(f32)
        for r in range(0, tq, _ATT_ROWS):
            rows = pl.ds(r, _ATT_ROWS)
            s = s_ref[slot, rows, :] + bias_ref[rows, :]
            m_old = m_ref[h, rows, :] - slope_tk
            m_new = jnp.maximum(m_old, jnp.max(s, axis=1, keepdims=True))
            p_ref[slot, rows, :] = jnp.exp(s - m_new).astype(MXU_DTYPE)
            alpha_ref[slot, rows, :] = jnp.exp(m_old - m_new)
            m_ref[h, rows, :] = m_new

    def weighted_values(h, slot):
        acc_ref[h] = alpha_ref[slot] * acc_ref[h] + jnp.dot(p_ref[slot], v_ref[0, h],
                                                             preferred_element_type=f32)

    for t in range(ATT_HEADS + 2):
        if t < ATT_HEADS:
            logits(t, t % 2)
        if 1 <= t <= ATT_HEADS:
            softmax(t - 1, (t - 1) % 2)
        if t >= 2:
            weighted_values(t - 2, t % 2)

    @pl.when(j == i)
    def _finish():
        heads = []
        for h in range(ATT_HEADS):
            a = acc_ref[h]
            heads.append((a[:, :ATT_HEAD_DIM] / a[:, ATT_HEAD_DIM:ATT_HEAD_DIM + 1]).astype(MXU_DTYPE))
        o_ref[0] = jnp.dot(jnp.concatenate(heads, axis=1), wout_ref[...], preferred_element_type=f32)


def dsa_attention(att_in, q_idx, kid_t, q, k_t, v_aug, w_out):
    bsz, _, seq, _ = q.shape
    tq = tk = DSA_TILE
    n_sel = min(TOPK_MAX, seq // 4)
    nq = seq // tq
    assert seq % tq == 0 and tk % _SEARCH_LANES == 0 and tq % _SEARCH_ROWS == 0
    pairs = [(a, b) for a in range(nq) for b in range(a + 1)]
    qi = jnp.asarray([a for a, _ in pairs], jnp.int32)
    kj = jnp.asarray([b for _, b in pairs], jnp.int32)
    tri = jnp.triu(jnp.ones((tk, tk), MXU_DTYPE))
    grid_spec = pltpu.PrefetchScalarGridSpec(
        num_scalar_prefetch=2,
        grid=(bsz, len(pairs)),
        in_specs=[
            pl.BlockSpec((1, IDX_HEADS, tq, LANE), lambda b, p, qi, kj: (b, 0, qi[p], 0)),
            pl.BlockSpec((1, tq, LANE), lambda b, p, qi, kj: (b, qi[p], ATT_IDX_BLOCK)),
            pl.BlockSpec((1, LANE, seq), lambda b, p, qi, kj: (b, 0, 0)),
            pl.BlockSpec((1, ATT_HEADS, tq, ATT_QK_DIM), lambda b, p, qi, kj: (b, 0, qi[p], 0)),
            pl.BlockSpec((1, ATT_HEADS, ATT_QK_DIM, tk), lambda b, p, qi, kj: (b, 0, 0, kj[p])),
            pl.BlockSpec((1, ATT_HEADS, tk, LANE), lambda b, p, qi, kj: (b, 0, kj[p], 0)),
            pl.BlockSpec((tk, tk), lambda b, p, qi, kj: (0, 0)),
            pl.BlockSpec((ATT_WIDTH, D_MODEL), lambda b, p, qi, kj: (0, 0)),
        ],
        out_specs=pl.BlockSpec((1, tq, D_MODEL), lambda b, p, qi, kj: (b, qi[p], 0)),
        scratch_shapes=[
            pltpu.VMEM((tq, seq), jnp.float32),
            pltpu.VMEM((tq, 1), jnp.float32),
            pltpu.VMEM((tq, 1), jnp.float32),
            pltpu.VMEM((tq, 1), jnp.float32),
            pltpu.VMEM((ATT_HEADS, tq, 1), jnp.float32),
            pltpu.VMEM((ATT_HEADS, tq, LANE), jnp.float32),
            pltpu.VMEM((tq, tk), jnp.float32),
            pltpu.VMEM((2, tq, tk), jnp.float32),
            pltpu.VMEM((2, tq, tk), MXU_DTYPE),
            pltpu.VMEM((2, tq, 1), jnp.float32),
            pltpu.VMEM((tq, _LIST_DEPTH * LANE), jnp.float32),
        ])
    return pl.pallas_call(
        functools.partial(_dsa_body, tq=tq, tk=tk, seq=seq, n_sel=n_sel),
        grid_spec=grid_spec,
        out_shape=jax.ShapeDtypeStruct((bsz, seq, D_MODEL), jnp.float32),
        compiler_params=pltpu.CompilerParams(
            dimension_semantics=("parallel", "arbitrary"), vmem_limit_bytes=VMEM_LIMIT_BYTES),
        name="dsa_attention",
    )(qi, kj, q_idx, att_in, kid_t, q, k_t, v_aug, tri, w_out)


def dsa_branch(att_in, q_norm_g, kv_norm_g, w_uq, w_qidx, w_ukv, w_out):
    q, q_idx, k_t, v_aug, kid_t = dsa_project(att_in, q_norm_g, kv_norm_g, w_uq, w_qidx, w_ukv)
    return dsa_attention(att_in, q_idx, kid_t, q, k_t, v_aug, w_out.astype(MXU_DTYPE))


MOE_TOKENS = 1024
MOE_ROW_CLASSES = (256, 288, 320, 512, 1024)
MOE_FFN_TILES = (896, 512, 256)
MOE_VMEM_LIMIT_BYTES = 56 * 1024 * 1024


def _moe_body(cnt_ref, h_ref, gate_ref, gatet_ref, tri_ref, wg_ref, wu_ref, wd_ref, g_ref, b_ref, o_ref,
              xb_ref, gather_ref, scatter_ref, xc_ref, yc_ref, acc_ref, rank_ref):
    f32 = jnp.float32
    tm = MOE_TOKENS
    i = pl.program_id(0)
    e = pl.program_id(1)
    f = pl.program_id(2)
    last_f = pl.num_programs(2) - 1

    @pl.when((e == 0) & (f == 0))
    def _():
        xb_ref[...] = h_ref[...].astype(MXU_DTYPE)
        acc_ref[...] = jnp.zeros_like(acc_ref)
        routed_rows = jnp.where(gatet_ref[...] != 0.0, 1.0, 0.0).astype(MXU_DTYPE)
        rank_ref[...] = jnp.dot(routed_rows, tri_ref[...], preferred_element_type=f32)

    def gate_column():
        lane = lax.broadcasted_iota(jnp.int32, gate_ref.shape, 1)
        return jnp.sum(jnp.where(lane == e, gate_ref[...], 0.0), axis=-1, keepdims=True)

    def routed(size):
        def run():
            @pl.when(f == 0)
            def _():
                active_row = jnp.where(gatet_ref[pl.ds(e, 1), :] != 0.0, 1.0, 0.0)
                rank_row = rank_ref[pl.ds(e, 1), :]
                slot = lax.broadcasted_iota(jnp.int32, (size, tm), 0).astype(f32)
                gather = jnp.where(slot == rank_row, active_row, 0.0)
                gather_ref[0:size, :] = gather.astype(MXU_DTYPE)
                rank_col = jnp.broadcast_to(rank_row, (LANE, tm)).T[:, 0:1]
                slot_l = lax.broadcasted_iota(jnp.int32, (tm, size), 1).astype(f32)
                scatter = jnp.where(slot_l == rank_col, jnp.where(gate_column() != 0.0, 1.0, 0.0), 0.0)
                scatter_ref[:, 0:size] = scatter.astype(MXU_DTYPE)
                xc_ref[0:size, :] = jnp.dot(gather.astype(MXU_DTYPE), xb_ref[...],
                                            preferred_element_type=f32).astype(MXU_DTYPE)
                yc_ref[0:size, :] = jnp.zeros((size, D_MODEL), f32)

            xc = xc_ref[0:size, :]
            a = jnp.dot(xc, wg_ref[0], preferred_element_type=f32)
            u = jnp.dot(xc, wu_ref[0], preferred_element_type=f32)
            act = a * jax.nn.sigmoid(a) * u
            yc_ref[0:size, :] += jnp.dot(act.astype(MXU_DTYPE), wd_ref[0], preferred_element_type=f32)

            @pl.when(f == last_f)
            def _():
                back = jnp.dot(scatter_ref[:, 0:size], yc_ref[0:size, :].astype(MXU_DTYPE),
                               preferred_element_type=f32)
                acc_ref[...] += gate_column() * back
        return run

    count = cnt_ref[i * N_EXPERTS + e]
    size_class = sum((count > s).astype(jnp.int32) for s in MOE_ROW_CLASSES[:-1])
    for k, size in enumerate(MOE_ROW_CLASSES):
        pl.when((size_class == k) & (count > 0))(routed(size))

    @pl.when((e == pl.num_programs(1) - 1) & (f == last_f))
    def _():
        o_ref[...] = _layer_norm_rows(DEEPNORM_ALPHA * h_ref[...] + acc_ref[...], g_ref[...], b_ref[...])


def moe_experts_norm(h, routing, wg, wu, wd, ln_g, ln_b):
    gates, gates_t, counts = routing
    m, d = h.shape
    n_experts, _, f_dim = wg.shape
    tm = MOE_TOKENS
    tf = next(t for t in MOE_FFN_TILES if f_dim % t == 0)
    assert m % tm == 0 and MOE_ROW_CLASSES[-1] == tm
    n_tiles = m // tm
    tri = jnp.triu(jnp.ones((tm, tm), MXU_DTYPE), k=1)
    row = lambda i, e, f, c: (i, 0)
    full = lambda i, e, f, c: (0, 0)
    grid_spec = pltpu.PrefetchScalarGridSpec(
        num_scalar_prefetch=1,
        grid=(n_tiles, n_experts, f_dim // tf),
        in_specs=[pl.BlockSpec((tm, d), row), pl.BlockSpec((tm, LANE), row),
                  pl.BlockSpec((n_experts, tm), lambda i, e, f, c: (0, i)),
                  pl.BlockSpec((tm, tm), full),
                  pl.BlockSpec((1, d, tf), lambda i, e, f, c: (e, 0, f)),
                  pl.BlockSpec((1, d, tf), lambda i, e, f, c: (e, 0, f)),
                  pl.BlockSpec((1, tf, d), lambda i, e, f, c: (e, f, 0)),
                  pl.BlockSpec((1, d), full), pl.BlockSpec((1, d), full)],
        out_specs=pl.BlockSpec((tm, d), row),
        scratch_shapes=[pltpu.VMEM((tm, d), MXU_DTYPE),
                        pltpu.VMEM((tm, tm), MXU_DTYPE),
                        pltpu.VMEM((tm, tm), MXU_DTYPE),
                        pltpu.VMEM((tm, d), MXU_DTYPE),
                        pltpu.VMEM((tm, d), jnp.float32),
                        pltpu.VMEM((tm, d), jnp.float32),
                        pltpu.VMEM((n_experts, tm), jnp.float32)])
    return pl.pallas_call(
        _moe_body,
        grid_spec=grid_spec,
        out_shape=jax.ShapeDtypeStruct((m, d), jnp.float32),
        compiler_params=pltpu.CompilerParams(
            dimension_semantics=("parallel", "arbitrary", "arbitrary"),
            vmem_limit_bytes=MOE_VMEM_LIMIT_BYTES),
        name="moe_experts_norm",
    )(counts, h, gates, gates_t, tri, wg, wu, wd, ln_g.reshape(1, d), ln_b.reshape(1, d))


def _router_body(h_ref, r_ref, gate_ref, gatet_ref, cnt_ref):
    f32 = jnp.float32
    logits = jnp.dot(h_ref[...], r_ref[...], preferred_element_type=f32)
    lane = lax.broadcasted_iota(jnp.int32, logits.shape, 1)
    logits = jnp.where(lane < N_EXPERTS, logits, -jnp.inf)
    top1 = jnp.max(logits, axis=1, keepdims=True)
    idx1 = jnp.min(jnp.where(logits == top1, lane, LANE), axis=1, keepdims=True)
    rest = jnp.where(lane == idx1, -jnp.inf, logits)
    top2 = jnp.max(rest, axis=1, keepdims=True)
    idx2 = jnp.min(jnp.where(rest == top2, lane, LANE), axis=1, keepdims=True)
    e2 = jnp.exp(top2 - top1)
    gates = jnp.where(lane == idx1, 1.0 / (1.0 + e2), jnp.where(lane == idx2, e2 / (1.0 + e2), 0.0))
    gate_ref[...] = gates
    gatet_ref[...] = gates.T[0:N_EXPERTS, :]
    cnt_ref[0] = jnp.sum(jnp.where(gates != 0.0, 1.0, 0.0), axis=0, keepdims=True)


def router_gates(h, router):
    m, d = h.shape
    tm = MOE_TOKENS
    assert TOP_K == 2 and m % tm == 0
    r_pad = jnp.pad(router.astype(jnp.float32), ((0, 0), (0, LANE - N_EXPERTS)))
    gates, gates_t, counts = pl.pallas_call(
        _router_body,
        grid=(m // tm,),
        in_specs=[pl.BlockSpec((tm, d), lambda i: (i, 0)), pl.BlockSpec((d, LANE), lambda i: (0, 0))],
        out_specs=[pl.BlockSpec((tm, LANE), lambda i: (i, 0)), pl.BlockSpec((N_EXPERTS, tm), lambda i: (0, i)),
                   pl.BlockSpec((1, 1, LANE), lambda i: (i, 0, 0))],
        out_shape=[jax.ShapeDtypeStruct((m, LANE), jnp.float32),
                   jax.ShapeDtypeStruct((N_EXPERTS, m), jnp.float32),
                   jax.ShapeDtypeStruct((m // tm, 1, LANE), jnp.float32)],
        compiler_params=pltpu.CompilerParams(
            dimension_semantics=("parallel",), vmem_limit_bytes=VMEM_LIMIT_BYTES),
        name="router_gates",
    )(h, r_pad)
    return gates, gates_t, counts[:, 0, :N_EXPERTS].astype(jnp.int32).reshape(-1)


def kernel(x, w_in, ssm_log_dt, ssm_lambda_re, ssm_lambda_im, ssm_b_re, ssm_b_im, ssm_c_re, ssm_c_im,
           ssm_d, ssm_w_glu, ssm_b_glu, ssm_w_out, hgrn_lb_logits, hgrn_norm_g, hgrn_w_out,
           attn_q_norm_g, attn_kv_norm_g, attn_w_uq, attn_w_qidx, attn_w_ukv, attn_w_out, w_o,
           ln_g, ln_b, ffn_w_gate, ffn_w_up, ffn_w_down, moe_router, moe_w_gate, moe_w_up, moe_w_down):
    bsz, seq, d = x.shape
    m = bsz * seq
    bf16 = MXU_DTYPE
    assert MIX_IN_USED + N_BRANCHES * D_MODEL == N_IN
    lb_soft = jax.nn.softmax(hgrn_lb_logits.astype(jnp.float32), axis=0)
    lower_bounds = jnp.concatenate([jnp.zeros_like(lb_soft[:1]), jnp.cumsum(lb_soft[1:], axis=0)], axis=0)
    h = x.reshape(m, d)
    for l in range(DEPTH):
        w_mix = jnp.pad(w_in[l][:, :MIX_IN_USED], ((0, 0), (0, MIX_IN_WIDTH - MIX_IN_USED))).astype(bf16)
        u, hg_in, att_in = in_proj(h, w_mix)
        y_ssm = s5_branch(u.reshape(bsz, seq, SSM_WIDTH), ssm_log_dt[l], ssm_lambda_re[l], ssm_lambda_im[l],
                          ssm_b_re[l], ssm_b_im[l], ssm_c_re[l], ssm_c_im[l], ssm_d[l], ssm_w_glu[l],
                          ssm_b_glu[l], ssm_w_out[l])
        y_hg = hgrn2_branch(hg_in.reshape(bsz, seq, HG_IN_WIDTH), lower_bounds[l], hgrn_norm_g[l],
                            hgrn_w_out[l])
        y_att = dsa_branch(att_in.reshape(bsz, seq, ATT_IN_WIDTH), attn_q_norm_g[l], attn_kv_norm_g[l],
                           attn_w_uq[l], attn_w_qidx[l], attn_w_ukv[l], attn_w_out[l])
        h = merge_project_norm(h, y_ssm.reshape(m, d), y_hg.reshape(m, d), y_att.reshape(m, d),
                               w_in[l][:, MIX_IN_USED:].astype(bf16), w_o[l].astype(bf16),
                               ln_g[l, 0], ln_b[l, 0])
        if l % 2 == 0:
            h = swiglu_norm(h, ffn_w_gate[l // 2].astype(bf16), ffn_w_up[l // 2].astype(bf16),
                            ffn_w_down[l // 2].astype(bf16), ln_g[l, 1], ln_b[l, 1])
        else:
            gate_w = router_gates(h, moe_router[l // 2])
            h = moe_experts_norm(h, gate_w, as_mxu_operand(moe_w_gate, l // 2),
                                 as_mxu_operand(moe_w_up, l // 2), as_mxu_operand(moe_w_down, l // 2),
                                 ln_g[l, 1], ln_b[l, 1])
    return h.reshape(bsz, seq, d)
```

```python
import functools

import jax
import jax.numpy as jnp
import numpy as np
from jax import lax
from jax.experimental import pallas as pl
from jax.experimental.pallas import tpu as pltpu

D_MODEL = 1024
DEPTH = 2
SSM_WIDTH = 256
SSM_GROUP = 16
SSM_GROUPS = SSM_WIDTH // SSM_GROUP
SSM_STATE = 64
HGRN_HEADS = 4
HGRN_DK = 64
HGRN_DV = 64
HGRN_WIDTH = HGRN_HEADS * HGRN_DV
ATT_HEADS = 8
ATT_HEAD_DIM = 64
ATT_WIDTH = ATT_HEADS * ATT_HEAD_DIM
ATT_Q_RANK = 256
ATT_KV_RANK = 128
IDX_HEADS = 4
IDX_DIM = 64
TOPK_MAX = 256
MASK_VALUE = -1e30
N_BRANCHES = 3
N_EXPERTS = 8
TOP_K = 2
DEEPNORM_ALPHA = (2 * DEPTH) ** 0.25
LN_EPS = 1e-5
RMS_EPS = 1e-6
F_MIN = 1e-12

IN_SPLITS = (SSM_WIDTH, HGRN_HEADS * HGRN_DK, HGRN_HEADS * HGRN_DK, HGRN_WIDTH, HGRN_WIDTH,
             ATT_Q_RANK, ATT_KV_RANK, IDX_DIM, IDX_HEADS, N_BRANCHES * D_MODEL)
N_IN = sum(IN_SPLITS)

VMEM_LIMIT_BYTES = 48 * 1024 * 1024
LANE = 128
MXU_DTYPE = jnp.bfloat16


def _round_up(n, m):
    return (n + m - 1) // m * m


HG_IN_WIDTH = 2 * HGRN_HEADS * HGRN_DK + 2 * HGRN_WIDTH
ATT_IN_USED = ATT_Q_RANK + ATT_KV_RANK + IDX_DIM + IDX_HEADS
ATT_IN_WIDTH = _round_up(ATT_IN_USED, LANE)
ATT_IDX_BLOCK = (ATT_Q_RANK + ATT_KV_RANK) // LANE
MIX_IN_USED = SSM_WIDTH + HG_IN_WIDTH + ATT_IN_USED
MIX_IN_WIDTH = SSM_WIDTH + HG_IN_WIDTH + ATT_IN_WIDTH


def _in_proj_body(h_ref, w_ref, u_ref, hg_ref, att_ref):
    r = jnp.dot(h_ref[...].astype(MXU_DTYPE), w_ref[...], preferred_element_type=jnp.float32)
    u_ref[...] = r[:, :SSM_WIDTH]
    hg_ref[...] = r[:, SSM_WIDTH:SSM_WIDTH + HG_IN_WIDTH]
    att_ref[...] = r[:, SSM_WIDTH + HG_IN_WIDTH:]


def in_proj(h, w_mix, *, tm=512):
    m, d = h.shape
    row = lambda i: (i, 0)
    widths = (SSM_WIDTH, HG_IN_WIDTH, ATT_IN_WIDTH)
    return pl.pallas_call(
        _in_proj_body,
        grid=(m // tm,),
        in_specs=[pl.BlockSpec((tm, d), row), pl.BlockSpec((d, MIX_IN_WIDTH), lambda i: (0, 0))],
        out_specs=[pl.BlockSpec((tm, w), row) for w in widths],
        out_shape=[jax.ShapeDtypeStruct((m, w), jnp.float32) for w in widths],
        compiler_params=pltpu.CompilerParams(
            dimension_semantics=("parallel",), vmem_limit_bytes=VMEM_LIMIT_BYTES),
        name="in_proj",
    )(h, w_mix)


def _layer_norm_rows(y, g, b):
    mu = jnp.mean(y, axis=-1, keepdims=True)
    yc = y - mu
    var = jnp.mean(yc * yc, axis=-1, keepdims=True)
    return yc * lax.rsqrt(var + LN_EPS) * g + b


def _merge_body(h_ref, ys_ref, yh_ref, ya_ref, wgate_ref, wo_ref, g_ref, b_ref, o_ref):
    d = D_MODEL
    f32 = jnp.float32
    h = h_ref[...]
    gates = jax.nn.sigmoid(jnp.dot(h.astype(MXU_DTYPE), wgate_ref[...], preferred_element_type=f32))
    mixed = (gates[:, 0:d] * ys_ref[...] + gates[:, d:2 * d] * yh_ref[...]
             + gates[:, 2 * d:3 * d] * ya_ref[...])
    mix_out = jnp.dot(mixed.astype(MXU_DTYPE), wo_ref[...], preferred_element_type=f32)
    o_ref[...] = _layer_norm_rows(DEEPNORM_ALPHA * h + mix_out, g_ref[...], b_ref[...])


def merge_project_norm(h, y_ssm, y_hg, y_att, w_gates, w_o, ln_g, ln_b, *, tm=512):
    m, d = h.shape
    row = lambda i: (i, 0)
    full = lambda i: (0, 0)
    return pl.pallas_call(
        _merge_body,
        grid=(m // tm,),
        in_specs=[pl.BlockSpec((tm, d), row), pl.BlockSpec((tm, d), row), pl.BlockSpec((tm, d), row),
                  pl.BlockSpec((tm, d), row), pl.BlockSpec((d, N_BRANCHES * d), full),
                  pl.BlockSpec((d, d), full), pl.BlockSpec((1, d), full), pl.BlockSpec((1, d), full)],
        out_specs=pl.BlockSpec((tm, d), row),
        out_shape=jax.ShapeDtypeStruct((m, d), jnp.float32),
        compiler_params=pltpu.CompilerParams(
            dimension_semantics=("parallel",), vmem_limit_bytes=VMEM_LIMIT_BYTES),
        name="merge_project_norm",
    )(h, y_ssm, y_hg, y_att, w_gates, w_o, ln_g.reshape(1, d), ln_b.reshape(1, d))


DENSE_FFN_TOKENS = 512
DENSE_FFN_TILES = (1408, 512, 256)


def _ffn_body(h_ref, wg_ref, wu_ref, wd_ref, g_ref, b_ref, o_ref, acc_ref):
    f = pl.program_id(1)

    @pl.when(f == 0)
    def _():
        acc_ref[...] = jnp.zeros_like(acc_ref)

    x = h_ref[...].astype(MXU_DTYPE)
    a = jnp.dot(x, wg_ref[...], preferred_element_type=jnp.float32)
    u = jnp.dot(x, wu_ref[...], preferred_element_type=jnp.float32)
    act = a * jax.nn.sigmoid(a) * u
    acc_ref[...] += jnp.dot(act.astype(MXU_DTYPE), wd_ref[...], preferred_element_type=jnp.float32)

    @pl.when(f == pl.num_programs(1) - 1)
    def _():
        o_ref[...] = _layer_norm_rows(DEEPNORM_ALPHA * h_ref[...] + acc_ref[...], g_ref[...], b_ref[...])


def swiglu_norm(h, wg, wu, wd, ln_g, ln_b):
    m, d = h.shape
    f_dim = wg.shape[1]
    tm = DENSE_FFN_TOKENS
    tf = next(t for t in DENSE_FFN_TILES if f_dim % t == 0)
    assert m % tm == 0
    row = lambda i, f: (i, 0)
    full = lambda i, f: (0, 0)
    return pl.pallas_call(
        _ffn_body,
        grid=(m // tm, f_dim // tf),
        in_specs=[pl.BlockSpec((tm, d), row),
                  pl.BlockSpec((d, tf), lambda i, f: (0, f)), pl.BlockSpec((d, tf), lambda i, f: (0, f)),
                  pl.BlockSpec((tf, d), lambda i, f: (f, 0)),
                  pl.BlockSpec((1, d), full), pl.BlockSpec((1, d), full)],
        out_specs=pl.BlockSpec((tm, d), row),
        out_shape=jax.ShapeDtypeStruct((m, d), jnp.float32),
        scratch_shapes=[pltpu.VMEM((tm, d), jnp.float32)],
        compiler_params=pltpu.CompilerParams(
            dimension_semantics=("parallel", "arbitrary"), vmem_limit_bytes=VMEM_LIMIT_BYTES),
        name="swiglu_norm",
    )(h, wg, wu, wd, ln_g.reshape(1, d), ln_b.reshape(1, d))


S5_CHUNK = 8
S5_CHUNK_WIDTH = S5_CHUNK * SSM_WIDTH
S5_STATE_WIDTH = SSM_GROUPS * SSM_STATE


def s5_operators(log_dt, lam_re, lam_im, b_re, b_im, c_re, c_im):
    f32 = jnp.float32
    n = S5_CHUNK
    lre, lim = lam_re.astype(f32), lam_im.astype(f32)
    dt = jnp.exp(log_dt.astype(f32))[:, None]
    mag = jnp.exp(lre * dt)
    ang = lim * dt
    a_re, a_im = mag * jnp.cos(ang), mag * jnp.sin(ang)
    den = lre * lre + lim * lim
    coef_re = ((a_re - 1.0) * lre + a_im * lim) / den
    coef_im = (a_im * lre - (a_re - 1.0) * lim) / den
    br, bi = b_re.astype(f32), b_im.astype(f32)
    bb_re = coef_re[..., None] * br - coef_im[..., None] * bi
    bb_im = coef_re[..., None] * bi + coef_im[..., None] * br
    j = jnp.arange(n + 1, dtype=f32)[:, None, None]
    pw_re = jnp.exp(j * lre * dt) * jnp.cos(j * ang)
    pw_im = jnp.exp(j * lre * dt) * jnp.sin(j * ang)
    eye = jnp.eye(SSM_GROUPS, dtype=f32)
    cr, ci = c_re.astype(f32), c_im.astype(f32)
    ab_re = pw_re[..., None] * bb_re - pw_im[..., None] * bb_im
    ab_im = pw_re[..., None] * bb_im + pw_im[..., None] * bb_re
    kern = (jnp.einsum('gcp,jgpd->jgcd', cr, ab_re[:n]) - jnp.einsum('gcp,jgpd->jgcd', ci, ab_im[:n]))
    lag = jnp.arange(n)[None, :] - jnp.arange(n)[:, None]
    toep = jnp.where((lag >= 0)[:, :, None, None, None], kern[jnp.maximum(lag, 0)], 0.0)
    t_mat = jnp.einsum('abgcd,gh->agdbhc', toep, eye).reshape(S5_CHUNK_WIDTH, S5_CHUNK_WIDTH)
    v = jnp.stack([ab_re[:n][::-1], ab_im[:n][::-1]])
    w_in = jnp.einsum('rlgpc,gh->lhcrgp', v, eye).reshape(S5_CHUNK_WIDTH, 2 * S5_STATE_WIDTH)
    ar, ai = pw_re[1:], pw_im[1:]
    wo_re = jnp.einsum('gcp,lgp->gplc', cr, ar) - jnp.einsum('gcp,lgp->gplc', ci, ai)
    wo_im = -jnp.einsum('gcp,lgp->gplc', cr, ai) - jnp.einsum('gcp,lgp->gplc', ci, ar)
    w_out = jnp.einsum('rgplc,gh->rhplgc', jnp.stack([wo_re, wo_im]), eye)
    w_out = w_out.reshape(2 * S5_STATE_WIDTH, S5_CHUNK_WIDTH)
    decay = jnp.stack([pw_re[n].reshape(1, S5_STATE_WIDTH), pw_im[n].reshape(1, S5_STATE_WIDTH)])
    return (jnp.concatenate([t_mat, w_in], axis=1).astype(MXU_DTYPE), w_out.astype(MXU_DTYPE), decay)


def _s5_scan_body(u_ref, tw_ref, wout_ref, decay_ref, y_ref, h_ref, s_ref, hs_ref, *, tm):
    f32 = jnp.float32
    sw = S5_STATE_WIDTH

    @pl.when(pl.program_id(1) == 0)
    def _():
        h_ref[...] = jnp.zeros_like(h_ref)

    r = jnp.dot(u_ref[0].astype(MXU_DTYPE), tw_ref[...], preferred_element_type=f32)
    s_ref[...] = r[:, S5_CHUNK_WIDTH:]
    d_re = decay_ref[0]
    d_im = decay_ref[1]

    def eight_chunks(k, carry):
        h_re, h_im = carry
        r0 = pl.multiple_of(k * 8, 8)
        inc = s_ref[pl.ds(r0, 8), :]
        rows_re, rows_im = [], []
        for t in range(8):
            rows_re.append(h_re)
            rows_im.append(h_im)
            h_re, h_im = (d_re * h_re - d_im * h_im + inc[t:t + 1, :sw],
                          d_re * h_im + d_im * h_re + inc[t:t + 1, sw:])
        hs_ref[pl.ds(r0, 8), :] = jnp.concatenate(
            [jnp.concatenate(rows_re, axis=0), jnp.concatenate(rows_im, axis=0)], axis=1)
        return h_re, h_im

    h_re, h_im = lax.fori_loop(0, tm // 8, eight_chunks, (h_ref[0:1, :], h_ref[1:2, :]))
    h_ref[0:1, :] = h_re
    h_ref[1:2, :] = h_im
    y_ref[0] = r[:, :S5_CHUNK_WIDTH] + jnp.dot(hs_ref[...].astype(MXU_DTYPE), wout_ref[...],
                                                preferred_element_type=f32)


def s5_scan(u, tw, w_out, decay, *, tm=256):
    bsz, seq, _ = u.shape
    n_chunks = seq // S5_CHUNK
    tm = min(tm, n_chunks)
    assert seq % S5_CHUNK == 0 and n_chunks % tm == 0 and tm % 8 == 0
    once = pl.Buffered(1)
    y = pl.pallas_call(
        functools.partial(_s5_scan_body, tm=tm),
        grid=(bsz, n_chunks // tm),
        in_specs=[pl.BlockSpec((1, tm, S5_CHUNK_WIDTH), lambda b, i: (b, i, 0)),
                  pl.BlockSpec(tw.shape, lambda b, i: (0, 0), pipeline_mode=once),
                  pl.BlockSpec(w_out.shape, lambda b, i: (0, 0), pipeline_mode=once),
                  pl.BlockSpec(decay.shape, lambda b, i: (0, 0, 0), pipeline_mode=once)],
        out_specs=pl.BlockSpec((1, tm, S5_CHUNK_WIDTH), lambda b, i: (b, i, 0)),
        out_shape=jax.ShapeDtypeStruct((bsz, n_chunks, S5_CHUNK_WIDTH), jnp.float32),
        scratch_shapes=[pltpu.VMEM((2, S5_STATE_WIDTH), jnp.float32),
                        pltpu.VMEM((tm, 2 * S5_STATE_WIDTH), jnp.float32),
                        pltpu.VMEM((tm, 2 * S5_STATE_WIDTH), jnp.float32)],
        compiler_params=pltpu.CompilerParams(
            dimension_semantics=("parallel", "arbitrary"), vmem_limit_bytes=VMEM_LIMIT_BYTES),
        name="s5_scan",
    )(u.reshape(bsz, n_chunks, S5_CHUNK_WIDTH), tw, w_out, decay)
    return y.reshape(bsz, seq, SSM_WIDTH)


def _s5_out_body(y_ref, u_ref, d_ref, wglu_ref, bglu_ref, wout_ref, o_ref):
    f32 = jnp.float32
    y = jax.nn.gelu(y_ref[...] + d_ref[...] * u_ref[...])
    gate = jnp.dot(y.astype(MXU_DTYPE), wglu_ref[...], preferred_element_type=f32) + bglu_ref[...]
    y = y * jax.nn.sigmoid(gate)
    o_ref[...] = jnp.dot(y.astype(MXU_DTYPE), wout_ref[...], preferred_element_type=f32)


def s5_output(y, u, d_skip, w_glu, b_glu, w_out, *, tm=1024):
    m, c = y.shape
    tm = min(tm, m)
    row = lambda i: (i, 0)
    full = lambda i: (0, 0)
    return pl.pallas_call(
        _s5_out_body,
        grid=(m // tm,),
        in_specs=[pl.BlockSpec((tm, c), row), pl.BlockSpec((tm, c), row), pl.BlockSpec((1, c), full),
                  pl.BlockSpec((c, c), full), pl.BlockSpec((1, c), full), pl.BlockSpec((c, D_MODEL), full)],
        out_specs=pl.BlockSpec((tm, D_MODEL), row),
        out_shape=jax.ShapeDtypeStruct((m, D_MODEL), jnp.float32),
        compiler_params=pltpu.CompilerParams(
            dimension_semantics=("parallel",), vmem_limit_bytes=VMEM_LIMIT_BYTES),
        name="s5_output",
    )(y, u, d_skip.reshape(1, c), w_glu.astype(MXU_DTYPE), b_glu.reshape(1, c), w_out.astype(MXU_DTYPE))


def s5_branch(u, log_dt, lam_re, lam_im, b_re, b_im, c_re, c_im, d_skip, w_glu, b_glu, w_out):
    bsz, seq, _ = u.shape
    tw, w_state_out, decay = s5_operators(log_dt, lam_re, lam_im, b_re, b_im, c_re, c_im)
    y = s5_scan(u, tw, w_state_out, decay)
    out = s5_output(y.reshape(bsz * seq, SSM_WIDTH), u.reshape(bsz * seq, SSM_WIDTH),
                    d_skip, w_glu, b_glu, w_out)
    return out.reshape(bsz, seq, D_MODEL)


HG_CHUNK = 128
HG_LEVELS = 7
HG_KDIM = HGRN_HEADS * HGRN_DK


def _hgrn_segment_sums():
    c = HG_CHUNK
    t = np.arange(c)[:, None]
    u = np.arange(c)[None, :]
    blocks = []
    for lvl in range(1, HG_LEVELS + 1):
        m = (t >> lvl << lvl) + (1 << (lvl - 1)) - 1
        right = ((t >> (lvl - 1)) & 1) == 1
        blocks.append(np.where(right, (u > m) & (u <= t), (u > t) & (u <= m)))
    blocks.append(u <= t)
    blocks.append(u > t)
    return np.concatenate(blocks, axis=0).astype(np.float32)


def _hgrn_body(q_ref, z_ref, v_ref, g_ref, seg_ref, lb_ref, ng_ref, hmean_ref, wout_ref, o_ref, st_ref):
    f32 = jnp.float32
    c = HG_CHUNK

    @pl.when(pl.program_id(1) == 0)
    def _():
        st_ref[...] = jnp.zeros_like(st_ref)

    q = q_ref[0]
    z = z_ref[0]
    v = v_ref[0]
    lb = lb_ref[...]
    f = lb + (1.0 - lb) * jax.nn.sigmoid(z)
    logf = jnp.log(jnp.maximum(f, F_MIN))
    kin = (1.0 - lb) * jax.nn.sigmoid(-z)

    p1 = logf.astype(MXU_DTYPE)
    r1 = logf - p1.astype(f32)
    p2 = r1.astype(MXU_DTYPE)
    p3 = (r1 - p2.astype(f32)).astype(MXU_DTYPE)
    seg = seg_ref[...]
    sums = (jnp.dot(seg, p1, preferred_element_type=f32) + jnp.dot(seg, p2, preferred_element_type=f32)
            + jnp.dot(seg, p3, preferred_element_type=f32))

    lane_head = lax.broadcasted_iota(jnp.int32, (c, HG_KDIM), 1) // HGRN_DK
    tok = lax.broadcasted_iota(jnp.int32, (c, HG_KDIM), 0)
    row_t = lax.broadcasted_iota(jnp.int32, (HGRN_HEADS * c, c), 0) % c
    col_s = lax.broadcasted_iota(jnp.int32, (HGRN_HEADS * c, c), 1)

    def per_head_rows(x):
        return jnp.concatenate([jnp.where(lane_head == h, x, 0.0) for h in range(HGRN_HEADS)],
                               axis=0).astype(MXU_DTYPE)

    def scores(ql, kl):
        return lax.dot_general(per_head_rows(ql), kl.astype(MXU_DTYPE), (((1,), (1,)), ((), ())),
                               preferred_element_type=f32)

    att = jnp.where(row_t == col_s, scores(q, kin), 0.0)
    for lvl in range(1, HG_LEVELS + 1):
        decay = jnp.exp(sums[(lvl - 1) * c:lvl * c])
        right = ((tok >> (lvl - 1)) & 1) == 1
        a = scores(jnp.where(right, q * decay, 0.0), jnp.where(right, 0.0, kin * decay))
        att = att + jnp.where((row_t >> lvl) == (col_s >> lvl), a, 0.0)

    b = sums[HG_LEVELS * c:(HG_LEVELS + 1) * c]
    tail = sums[(HG_LEVELS + 1) * c:(HG_LEVELS + 2) * c]
    v_m = v.astype(MXU_DTYPE)
    st = st_ref[...]
    o = lax.dot_general((q * jnp.exp(b)).astype(MXU_DTYPE), st.astype(MXU_DTYPE),
                        (((1,), (1,)), ((), ())), preferred_element_type=f32)
    for h in range(HGRN_HEADS):
        o_h = jnp.dot(att[h * c:(h + 1) * c].astype(MXU_DTYPE), v_m, preferred_element_type=f32)
        o = o + jnp.where(lane_head == h, o_h, 0.0)

    kv = jnp.dot(v.T.astype(MXU_DTYPE), (kin * jnp.exp(tail)).astype(MXU_DTYPE), preferred_element_type=f32)
    sr = lax.broadcasted_iota(jnp.int32, st.shape, 0) // HGRN_DV
    sc = lax.broadcasted_iota(jnp.int32, st.shape, 1) // HGRN_DK
    st_ref[...] = st * jnp.exp(b[c - 1:c, :]) + jnp.where(sr == sc, kv, 0.0)

    o2 = o * o
    o2_hi = o2.astype(MXU_DTYPE)
    o2_lo = (o2 - o2_hi.astype(f32)).astype(MXU_DTYPE)
    ms = (jnp.dot(o2_hi, hmean_ref[...], preferred_element_type=f32)
          + jnp.dot(o2_lo, hmean_ref[...], preferred_element_type=f32))
    g = g_ref[0]
    out = o * lax.rsqrt(ms + RMS_EPS) * ng_ref[...] * (g * jax.nn.sigmoid(g))
    o_ref[0] = jnp.dot(out.astype(MXU_DTYPE), wout_ref[...], preferred_element_type=f32)


def hgrn2_branch(hg_in, lower_bound, norm_g, w_out):
    bsz, seq, _ = hg_in.shape
    assert seq % HG_CHUNK == 0 and HGRN_DK == HGRN_DV and HG_IN_WIDTH == 4 * HG_KDIM
    seg = jnp.asarray(_hgrn_segment_sums(), MXU_DTYPE)
    head_mean = jnp.asarray(np.kron(np.eye(HGRN_HEADS), np.full((HGRN_DV, HGRN_DV), 1.0 / HGRN_DV)), MXU_DTYPE)
    tok = lambda b, i: (b, i, 0)
    full = lambda b, i: (0, 0)
    part = lambda k: pl.BlockSpec((1, HG_CHUNK, HG_KDIM), lambda b, i: (b, i, k))
    return pl.pallas_call(
        _hgrn_body,
        grid=(bsz, seq // HG_CHUNK),
        in_specs=[part(0), part(1), part(2), part(3),
                  pl.BlockSpec(seg.shape, full), pl.BlockSpec((1, HG_KDIM), full),
                  pl.BlockSpec((1, HGRN_WIDTH), full), pl.BlockSpec(head_mean.shape, full),
                  pl.BlockSpec((HGRN_WIDTH, D_MODEL), full)],
        out_specs=pl.BlockSpec((1, HG_CHUNK, D_MODEL), tok),
        out_shape=jax.ShapeDtypeStruct((bsz, seq, D_MODEL), jnp.float32),
        scratch_shapes=[pltpu.VMEM((HGRN_WIDTH, HG_KDIM), jnp.float32)],
        compiler_params=pltpu.CompilerParams(
            dimension_semantics=("parallel", "arbitrary"), vmem_limit_bytes=VMEM_LIMIT_BYTES),
        name="hgrn2",
    )(hg_in, hg_in, hg_in, hg_in, seg, lower_bound.reshape(1, HG_KDIM).astype(jnp.float32),
      jnp.tile(norm_g.astype(jnp.float32), HGRN_HEADS).reshape(1, HGRN_WIDTH), head_mean,
      w_out.astype(MXU_DTYPE))


ATT_QK_DIM = LANE
DSA_TILE = 512


def _rms_rows(x, g):
    return x * lax.rsqrt(jnp.mean(x * x, axis=-1, keepdims=True) + RMS_EPS) * g


def _dsa_project_body(x_ref, gq_ref, gkv_ref, wq_ref, wqi_ref, wkt_ref, wv_ref,
                      q_ref, qi_ref, kt_ref, v_ref, kidt_ref):
    f32 = jnp.float32
    tm = x_ref.shape[1]
    x = x_ref[0]
    cq = _rms_rows(x[:, :ATT_Q_RANK], gq_ref[...]).astype(MXU_DTYPE)
    ckv = _rms_rows(x[:, ATT_Q_RANK:ATT_Q_RANK + ATT_KV_RANK], gkv_ref[...])
    q_all = jnp.dot(cq, wq_ref[...], preferred_element_type=f32)
    qi_all = jnp.dot(cq, wqi_ref[...], preferred_element_type=f32)
    v_all = jnp.dot(ckv.astype(MXU_DTYPE), wv_ref[...], preferred_element_type=f32)
    kt_all = jnp.dot(wkt_ref[...], ckv.T.astype(MXU_DTYPE), preferred_element_type=f32)

    lane = lax.broadcasted_iota(jnp.int32, (tm, LANE), 1)
    t_loc = lax.broadcasted_iota(jnp.int32, (tm, LANE), 0)
    t_even = (t_loc // 2 * 2).astype(f32)
    t_odd = (t_loc % 2).astype(f32)
    sub = lax.broadcasted_iota(jnp.int32, (LANE, tm), 0)
    s_loc = lax.broadcasted_iota(jnp.int32, (LANE, tm), 1)
    k_rows = jnp.where(sub < ATT_HEAD_DIM + 2, 1.0,
                       jnp.where(sub == ATT_HEAD_DIM + 2, (s_loc // 2 * 2).astype(f32), (s_loc % 2).astype(f32)))
    for h in range(ATT_HEADS):
        slope = 2.0 ** (-8.0 * (h + 1) / ATT_HEADS)
        q_cols = jnp.where(lane == ATT_HEAD_DIM, -slope * t_even,
                           jnp.where(lane == ATT_HEAD_DIM + 1, -slope * t_odd, slope))
        q_h = q_all[:, h * LANE:(h + 1) * LANE]
        q_ref[0, h] = jnp.where(lane < ATT_HEAD_DIM, q_h,
                                jnp.where(lane < ATT_HEAD_DIM + 4, q_cols, 0.0)).astype(q_ref.dtype)
        k_h = kt_all[h * LANE:(h + 1) * LANE, :]
        kt_ref[0, h] = jnp.where(sub < ATT_HEAD_DIM, k_h,
                                 jnp.where(sub < ATT_HEAD_DIM + 4, k_rows, 0.0)).astype(kt_ref.dtype)
        v_h = v_all[:, h * LANE:(h + 1) * LANE]
        v_ref[0, h] = jnp.where(lane == ATT_HEAD_DIM, 1.0, v_h).astype(v_ref.dtype)
    for h in range(IDX_HEADS):
        qi_ref[0, h] = qi_all[:, h * LANE:(h + 1) * LANE].astype(qi_ref.dtype)
    kidt_ref[0] = x[:, ATT_IDX_BLOCK * LANE:(ATT_IDX_BLOCK + 1) * LANE].T.astype(kidt_ref.dtype)


def _head_padded(w, n_heads, dim):
    k = w.shape[0]
    return jnp.pad(w.reshape(k, n_heads, dim), ((0, 0), (0, 0), (0, LANE - dim))).reshape(k, n_heads * LANE)


def dsa_project(att_in, q_norm_g, kv_norm_g, w_uq, w_qidx, w_ukv):
    bsz, seq, _ = att_in.shape
    tm = DSA_TILE
    assert seq % tm == 0 and 8 % ATT_HEADS == 0 and tm <= 512
    w_q = _head_padded(w_uq * ATT_HEAD_DIM ** -0.5, ATT_HEADS, ATT_HEAD_DIM).astype(MXU_DTYPE)
    w_qi = _head_padded(w_qidx * IDX_DIM ** -0.5, IDX_HEADS, IDX_DIM).astype(MXU_DTYPE)
    w_kt = _head_padded(w_ukv[:, :ATT_WIDTH], ATT_HEADS, ATT_HEAD_DIM).T.astype(MXU_DTYPE)
    w_v = _head_padded(w_ukv[:, ATT_WIDTH:], ATT_HEADS, ATT_HEAD_DIM).astype(MXU_DTYPE)
    tok = lambda b, i: (b, 0, i, 0)
    full = lambda b, i: (0, 0)
    dt = MXU_DTYPE
    return pl.pallas_call(
        _dsa_project_body,
        grid=(bsz, seq // tm),
        in_specs=[pl.BlockSpec((1, tm, ATT_IN_WIDTH), lambda b, i: (b, i, 0)),
                  pl.BlockSpec((1, ATT_Q_RANK), full), pl.BlockSpec((1, ATT_KV_RANK), full),
                  pl.BlockSpec(w_q.shape, full), pl.BlockSpec(w_qi.shape, full),
                  pl.BlockSpec(w_kt.shape, full), pl.BlockSpec(w_v.shape, full)],
        out_specs=[pl.BlockSpec((1, ATT_HEADS, tm, LANE), tok),
                   pl.BlockSpec((1, IDX_HEADS, tm, LANE), tok),
                   pl.BlockSpec((1, ATT_HEADS, LANE, tm), lambda b, i: (b, 0, 0, i)),
                   pl.BlockSpec((1, ATT_HEADS, tm, LANE), tok),
                   pl.BlockSpec((1, LANE, tm), lambda b, i: (b, 0, i))],
        out_shape=[jax.ShapeDtypeStruct((bsz, ATT_HEADS, seq, LANE), dt),
                   jax.ShapeDtypeStruct((bsz, IDX_HEADS, seq, LANE), dt),
                   jax.ShapeDtypeStruct((bsz, ATT_HEADS, LANE, seq), dt),
                   jax.ShapeDtypeStruct((bsz, ATT_HEADS, seq, LANE), dt),
                   jax.ShapeDtypeStruct((bsz, LANE, seq), dt)],
        compiler_params=pltpu.CompilerParams(
            dimension_semantics=("parallel", "parallel"), vmem_limit_bytes=VMEM_LIMIT_BYTES),
        name="dsa_project",
    )(att_in, q_norm_g.reshape(1, ATT_Q_RANK), kv_norm_g.reshape(1, ATT_KV_RANK), w_q, w_qi, w_kt, w_v)


_INT_MIN = -2 ** 31
_MASK_KEY = int(np.float32(MASK_VALUE).view(np.int32)) ^ 0x7FFFFFFF
_SEARCH_ROWS = 128
_ATT_ROWS = 32
_SEARCH_LANES = 512
_LIST_DEPTH = 12
_LIST_ROWS = 32
_LIST_MIN_BLOCKS = 4


def _key_to_score(key):
    return lax.bitcast_convert_type(jnp.where(key < 0, key ^ 0x7FFFFFFF, key), jnp.float32)


def _dsa_body(qi_ref, kj_ref, qidx_ref, w_ref, kidt_ref, q_ref, kt_ref, v_ref, tri_ref, wout_ref,
              o_ref, keys_ref, thr_ref, need_ref, carry_ref, m_ref, acc_ref, bias_ref, s_ref, p_ref,
              alpha_ref, cand_ref, *, tq, tk, seq, n_sel):
    f32 = jnp.float32
    p_id = pl.program_id(1)
    i = qi_ref[p_id]
    j = kj_ref[p_id]
    row_local = lax.broadcasted_iota(jnp.int32, (tq, tk), 0)
    col_local = lax.broadcasted_iota(jnp.int32, (tq, tk), 1)

    @pl.when(j == 0)
    def _select():
        def score_block(jj, carry):
            off = pl.multiple_of(jj * tk, tk)
            kb = kidt_ref[0, :, pl.ds(off, tk)]
            sc = jnp.zeros((tq, tk), f32)
            for h in range(IDX_HEADS):
                s = jnp.dot(qidx_ref[0, h], kb, preferred_element_type=f32)
                w_h = w_ref[0, :, IDX_DIM + h:IDX_DIM + h + 1] * IDX_HEADS ** -0.5
                sc = sc + jnp.maximum(s, 0.0) * w_h
            causal = (col_local + jj * tk) <= (row_local + i * tq)
            sc = jnp.where(causal, sc, MASK_VALUE)
            keys_ref[:, pl.ds(off, tk)] = jnp.where(sc == 0.0, 0.0, sc)
            return carry

        lax.fori_loop(0, i + 1, score_block, 0)

        n_blocks = i + 1
        n_masked_tail = (seq - n_blocks * tk).astype(f32)

        def row_group(r, carry):
            r0 = pl.multiple_of(r * _SEARCH_ROWS, _SEARCH_ROWS)

            def count_ge(ref, n_iter, cand):
                cand_b = jnp.broadcast_to(_key_to_score(cand), (_SEARCH_ROWS, LANE))

                def chunk(c, acc):
                    base = pl.multiple_of(c * _SEARCH_LANES, _SEARCH_LANES)
                    for u in range(_SEARCH_LANES // LANE):
                        kk = ref[pl.ds(r0, _SEARCH_ROWS), pl.ds(base + u * LANE, LANE)]
                        acc = acc + jnp.where(kk >= cand_b, 1.0, 0.0)
                    return acc

                acc = lax.fori_loop(0, n_iter, chunk, jnp.zeros((_SEARCH_ROWS, LANE), f32))
                cnt = jnp.sum(acc, axis=1, keepdims=True)
                return cnt + jnp.where(cand <= _MASK_KEY, n_masked_tail, 0.0)

            def kth_largest_key(count):
                zero = jnp.zeros((_SEARCH_ROWS, 1), jnp.int32)
                v0 = jnp.where(count(zero) >= n_sel, zero, zero + _INT_MIN)

                def bit_step(b, v):
                    cand = v | jnp.left_shift(jnp.int32(1), 30 - b)
                    return jnp.where(count(cand) >= n_sel, cand, v)

                return lax.fori_loop(0, 31, bit_step, v0)

            count_all = functools.partial(count_ge, keys_ref, n_blocks * (tk // _SEARCH_LANES))

            def store(v, n_above):
                thr_ref[pl.ds(r0, _SEARCH_ROWS), :] = _key_to_score(v)
                need_ref[pl.ds(r0, _SEARCH_ROWS), :] = n_sel - n_above

            def search_all_keys():
                v = kth_largest_key(count_all)
                store(v, count_all(v + 1))

            @pl.when(n_blocks < _LIST_MIN_BLOCKS)
            def _():
                search_all_keys()

            @pl.when(n_blocks >= _LIST_MIN_BLOCKS)
            def _():
                def shortlist(sub, carry):
                    rr = pl.multiple_of(r0 + sub * _LIST_ROWS, _LIST_ROWS)

                    def insert_block(c, tops):
                        base = pl.multiple_of(c * tk, tk)
                        for u in range(tk // LANE):
                            x = keys_ref[pl.ds(rr, _LIST_ROWS), pl.ds(base + u * LANE, LANE)]
                            new = []
                            for t in range(_LIST_DEPTH):
                                new.append(jnp.maximum(tops[t], x))
                                x = jnp.minimum(tops[t], x)
                            tops = tuple(new)
                        return tops

                    tops = lax.fori_loop(0, n_blocks, insert_block,
                                         tuple(jnp.full((_LIST_ROWS, LANE), -jnp.inf, f32)
                                               for _ in range(_LIST_DEPTH)))
                    for t in range(_LIST_DEPTH):
                        cand_ref[pl.ds(rr, _LIST_ROWS), t * LANE:(t + 1) * LANE] = tops[t]
                    return carry

                lax.fori_loop(0, _SEARCH_ROWS // _LIST_ROWS, shortlist, 0)
                v = kth_largest_key(functools.partial(count_ge, cand_ref,
                                                      _LIST_DEPTH * LANE // _SEARCH_LANES))
                n_at_least = count_all(v)
                n_above = count_all(v + 1)
                store(v, n_above)
                exact = (n_above < n_sel) & (n_at_least >= n_sel)
                pl.when(jnp.min(jnp.where(exact, 1.0, 0.0)) < 0.5)(search_all_keys)

            return carry

        lax.fori_loop(0, tq // _SEARCH_ROWS, row_group, 0)
        carry_ref[...] = jnp.zeros_like(carry_ref)
        m_ref[...] = jnp.full_like(m_ref, MASK_VALUE)
        acc_ref[...] = jnp.zeros_like(acc_ref)

    keys_blk = keys_ref[:, pl.ds(pl.multiple_of(j * tk, tk), tk)]
    thr = thr_ref[...]
    eq = keys_blk == thr
    tie_rank = carry_ref[...] + jnp.dot(jnp.where(eq, 1.0, 0.0).astype(MXU_DTYPE), tri_ref[...],
                                         preferred_element_type=f32)
    carry_ref[...] = tie_rank[:, tk - 1:tk]
    tie_bias = jnp.where(eq, jnp.where(tie_rank <= need_ref[...], 0.0, MASK_VALUE), MASK_VALUE)
    bias = jnp.where(keys_blk > thr, 0.0, tie_bias)
    causal = (row_local + i * tq) >= (col_local + j * tk)
    bias_ref[...] = jnp.where(causal, bias, MASK_VALUE)

    def logits(h, slot):
        s_ref[slot] = jnp.dot(q_ref[0, h], kt_ref[0, h], preferred_element_type=f32)

    def softmax(h, slot):
        slope_tk = jnp.right_shift(tk, h + 1).astype(f32)
        for r in range(0, tq, _ATT_ROWS):
            rows = pl.ds(r, _ATT_ROWS)
            s = s_ref[slot, rows, :] + bias_ref[rows, :]
            m_old = m_ref[h, rows, :] - slope_tk
            m_new = jnp.maximum(m_old, jnp.max(s, axis=1, keepdims=True))
            p_ref[slot, rows, :] = jnp.exp(s - m_new).astype(MXU_DTYPE)
            alpha_ref[slot, rows, :] = jnp.exp(m_old - m_new)
            m_ref[h, rows, :] = m_new

    def weighted_values(h, slot):
        acc_ref[h] = alpha_ref[slot] * acc_ref[h] + jnp.dot(p_ref[slot], v_ref[0, h],
                                                             preferred_element_type=f32)

    for t in range(ATT_HEADS + 2):
        if t < ATT_HEADS:
            logits(t, t % 2)
        if 1 <= t <= ATT_HEADS:
            softmax(t - 1, (t - 1) % 2)
        if t >= 2:
            weighted_values(t - 2, t % 2)

    @pl.when(j == i)
    def _finish():
        heads = []
        for h in range(ATT_HEADS):
            a = acc_ref[h]
            heads.append((a[:, :ATT_HEAD_DIM] / a[:, ATT_HEAD_DIM:ATT_HEAD_DIM + 1]).astype(MXU_DTYPE))
        o_ref[0] = jnp.dot(jnp.concatenate(heads, axis=1), wout_ref[...], preferred_element_type=f32)


def dsa_attention(att_in, q_idx, kid_t, q, k_t, v_aug, w_out):
    bsz, _, seq, _ = q.shape
    tq = tk = DSA_TILE
    n_sel = min(TOPK_MAX, seq // 4)
    nq = seq // tq
    assert seq % tq == 0 and tk % _SEARCH_LANES == 0 and tq % _SEARCH_ROWS == 0
    pairs = [(a, b) for a in range(nq) for b in range(a + 1)]
    qi = jnp.asarray([a for a, _ in pairs], jnp.int32)
    kj = jnp.asarray([b for _, b in pairs], jnp.int32)
    tri = jnp.triu(jnp.ones((tk, tk), MXU_DTYPE))
    grid_spec = pltpu.PrefetchScalarGridSpec(
        num_scalar_prefetch=2,
        grid=(bsz, len(pairs)),
        in_specs=[
            pl.BlockSpec((1, IDX_HEADS, tq, LANE), lambda b, p, qi, kj: (b, 0, qi[p], 0)),
            pl.BlockSpec((1, tq, LANE), lambda b, p, qi, kj: (b, qi[p], ATT_IDX_BLOCK)),
            pl.BlockSpec((1, LANE, seq), lambda b, p, qi, kj: (b, 0, 0)),
            pl.BlockSpec((1, ATT_HEADS, tq, ATT_QK_DIM), lambda b, p, qi, kj: (b, 0, qi[p], 0)),
            pl.BlockSpec((1, ATT_HEADS, ATT_QK_DIM, tk), lambda b, p, qi, kj: (b, 0, 0, kj[p])),
            pl.BlockSpec((1, ATT_HEADS, tk, LANE), lambda b, p, qi, kj: (b, 0, kj[p], 0)),
            pl.BlockSpec((tk, tk), lambda b, p, qi, kj: (0, 0)),
            pl.BlockSpec((ATT_WIDTH, D_MODEL), lambda b, p, qi, kj: (0, 0)),
        ],
        out_specs=pl.BlockSpec((1, tq, D_MODEL), lambda b, p, qi, kj: (b, qi[p], 0)),
        scratch_shapes=[
            pltpu.VMEM((tq, seq), jnp.float32),
            pltpu.VMEM((tq, 1), jnp.float32),
            pltpu.VMEM((tq, 1), jnp.float32),
            pltpu.VMEM((tq, 1), jnp.float32),
            pltpu.VMEM((ATT_HEADS, tq, 1), jnp.float32),
            pltpu.VMEM((ATT_HEADS, tq, LANE), jnp.float32),
            pltpu.VMEM((tq, tk), jnp.float32),
            pltpu.VMEM((2, tq, tk), jnp.float32),
            pltpu.VMEM((2, tq, tk), MXU_DTYPE),
            pltpu.VMEM((2, tq, 1), jnp.float32),
            pltpu.VMEM((tq, _LIST_DEPTH * LANE), jnp.float32),
        ])
    return pl.pallas_call(
        functools.partial(_dsa_body, tq=tq, tk=tk, seq=seq, n_sel=n_sel),
        grid_spec=grid_spec,
        out_shape=jax.ShapeDtypeStruct((bsz, seq, D_MODEL), jnp.float32),
        compiler_params=pltpu.CompilerParams(
            dimension_semantics=("parallel", "arbitrary"), vmem_limit_bytes=VMEM_LIMIT_BYTES),
        name="dsa_attention",
    )(qi, kj, q_idx, att_in, kid_t, q, k_t, v_aug, tri, w_out)


def dsa_branch(att_in, q_norm_g, kv_norm_g, w_uq, w_qidx, w_ukv, w_out):
    q, q_idx, k_t, v_aug, kid_t = dsa_project(att_in, q_norm_g, kv_norm_g, w_uq, w_qidx, w_ukv)
    return dsa_attention(att_in, q_idx, kid_t, q, k_t, v_aug, w_out.astype(MXU_DTYPE))


MOE_TOKENS = 1024
MOE_ROW_CLASSES = (256, 288, 320, 512, 1024)
MOE_FFN_TILES = (896, 512, 256)
MOE_VMEM_LIMIT_BYTES = 56 * 1024 * 1024


def _moe_body(cnt_ref, h_ref, gate_ref, gatet_ref, tri_ref, wg_ref, wu_ref, wd_ref, g_ref, b_ref, o_ref,
              xb_ref, gather_ref, scatter_ref, xc_ref, yc_ref, acc_ref, rank_ref):
    f32 = jnp.float32
    tm = MOE_TOKENS
    i = pl.program_id(0)
    e = pl.program_id(1)
    f = pl.program_id(2)
    last_f = pl.num_programs(2) - 1

    @pl.when((e == 0) & (f == 0))
    def _():
        xb_ref[...] = h_ref[...].astype(MXU_DTYPE)
        acc_ref[...] = jnp.zeros_like(acc_ref)
        routed_rows = jnp.where(gatet_ref[...] != 0.0, 1.0, 0.0).astype(MXU_DTYPE)
        rank_ref[...] = jnp.dot(routed_rows, tri_ref[...], preferred_element_type=f32)

    def gate_column():
        lane = lax.broadcasted_iota(jnp.int32, gate_ref.shape, 1)
        return jnp.sum(jnp.where(lane == e, gate_ref[...], 0.0), axis=-1, keepdims=True)

    def routed(size):
        def run():
            @pl.when(f == 0)
            def _():
                active_row = jnp.where(gatet_ref[pl.ds(e, 1), :] != 0.0, 1.0, 0.0)
                rank_row = rank_ref[pl.ds(e, 1), :]
                slot = lax.broadcasted_iota(jnp.int32, (size, tm), 0).astype(f32)
                gather = jnp.where(slot == rank_row, active_row, 0.0)
                gather_ref[0:size, :] = gather.astype(MXU_DTYPE)
                rank_col = jnp.broadcast_to(rank_row, (LANE, tm)).T[:, 0:1]
                slot_l = lax.broadcasted_iota(jnp.int32, (tm, size), 1).astype(f32)
                scatter = jnp.where(slot_l == rank_col, jnp.where(gate_column() != 0.0, 1.0, 0.0), 0.0)
                scatter_ref[:, 0:size] = scatter.astype(MXU_DTYPE)
                xc_ref[0:size, :] = jnp.dot(gather.astype(MXU_DTYPE), xb_ref[...],
                                            preferred_element_type=f32).astype(MXU_DTYPE)
                yc_ref[0:size, :] = jnp.zeros((size, D_MODEL), f32)

            xc = xc_ref[0:size, :]
            a = jnp.dot(xc, wg_ref[0], preferred_element_type=f32)
            u = jnp.dot(xc, wu_ref[0], preferred_element_type=f32)
            act = a * jax.nn.sigmoid(a) * u
            yc_ref[0:size, :] += jnp.dot(act.astype(MXU_DTYPE), wd_ref[0], preferred_element_type=f32)

            @pl.when(f == last_f)
            def _():
                back = jnp.dot(scatter_ref[:, 0:size], yc_ref[0:size, :].astype(MXU_DTYPE),
                               preferred_element_type=f32)
                acc_ref[...] += gate_column() * back
        return run

    count = cnt_ref[i * N_EXPERTS + e]
    size_class = sum((count > s).astype(jnp.int32) for s in MOE_ROW_CLASSES[:-1])
    for k, size in enumerate(MOE_ROW_CLASSES):
        pl.when((size_class == k) & (count > 0))(routed(size))

    @pl.when((e == pl.num_programs(1) - 1) & (f == last_f))
    def _():
        o_ref[...] = _layer_norm_rows(DEEPNORM_ALPHA * h_ref[...] + acc_ref[...], g_ref[...], b_ref[...])


def moe_experts_norm(h, routing, wg, wu, wd, ln_g, ln_b):
    gates, gates_t, counts = routing
    m, d = h.shape
    n_experts, _, f_dim = wg.shape
    tm = MOE_TOKENS
    tf = next(t for t in MOE_FFN_TILES if f_dim % t == 0)
    assert m % tm == 0 and MOE_ROW_CLASSES[-1] == tm
    n_tiles = m // tm
    tri = jnp.triu(jnp.ones((tm, tm), MXU_DTYPE), k=1)
    row = lambda i, e, f, c: (i, 0)
    full = lambda i, e, f, c: (0, 0)
    grid_spec = pltpu.PrefetchScalarGridSpec(
        num_scalar_prefetch=1,
        grid=(n_tiles, n_experts, f_dim // tf),
        in_specs=[pl.BlockSpec((tm, d), row), pl.BlockSpec((tm, LANE), row),
                  pl.BlockSpec((n_experts, tm), lambda i, e, f, c: (0, i)),
                  pl.BlockSpec((tm, tm), full),
                  pl.BlockSpec((1, d, tf), lambda i, e, f, c: (e, 0, f)),
                  pl.BlockSpec((1, d, tf), lambda i, e, f, c: (e, 0, f)),
                  pl.BlockSpec((1, tf, d), lambda i, e, f, c: (e, f, 0)),
                  pl.BlockSpec((1, d), full), pl.BlockSpec((1, d), full)],
        out_specs=pl.BlockSpec((tm, d), row),
        scratch_shapes=[pltpu.VMEM((tm, d), MXU_DTYPE),
                        pltpu.VMEM((tm, tm), MXU_DTYPE),
                        pltpu.VMEM((tm, tm), MXU_DTYPE),
                        pltpu.VMEM((tm, d), MXU_DTYPE),
                        pltpu.VMEM((tm, d), jnp.float32),
                        pltpu.VMEM((tm, d), jnp.float32),
                        pltpu.VMEM((n_experts, tm), jnp.float32)])
    return pl.pallas_call(
        _moe_body,
        grid_spec=grid_spec,
        out_shape=jax.ShapeDtypeStruct((m, d), jnp.float32),
        compiler_params=pltpu.CompilerParams(
            dimension_semantics=("parallel", "arbitrary", "arbitrary"),
            vmem_limit_bytes=MOE_VMEM_LIMIT_BYTES),
        name="moe_experts_norm",
    )(counts, h, gates, gates_t, tri, wg, wu, wd, ln_g.reshape(1, d), ln_b.reshape(1, d))


def _router_body(h_ref, r_ref, gate_ref, gatet_ref, cnt_ref):
    f32 = jnp.float32
    logits = jnp.dot(h_ref[...], r_ref[...], preferred_element_type=f32)
    lane = lax.broadcasted_iota(jnp.int32, logits.shape, 1)
    logits = jnp.where(lane < N_EXPERTS, logits, -jnp.inf)
    top1 = jnp.max(logits, axis=1, keepdims=True)
    idx1 = jnp.min(jnp.where(logits == top1, lane, LANE), axis=1, keepdims=True)
    rest = jnp.where(lane == idx1, -jnp.inf, logits)
    top2 = jnp.max(rest, axis=1, keepdims=True)
    idx2 = jnp.min(jnp.where(rest == top2, lane, LANE), axis=1, keepdims=True)
    e2 = jnp.exp(top2 - top1)
    gates = jnp.where(lane == idx1, 1.0 / (1.0 + e2), jnp.where(lane == idx2, e2 / (1.0 + e2), 0.0))
    gate_ref[...] = gates
    gatet_ref[...] = gates.T[0:N_EXPERTS, :]
    cnt_ref[0] = jnp.sum(jnp.where(gates != 0.0, 1.0, 0.0), axis=0, keepdims=True)


def router_gates(h, router):
    m, d = h.shape
    tm = MOE_TOKENS
    assert TOP_K == 2 and m % tm == 0
    r_pad = jnp.pad(router.astype(jnp.float32), ((0, 0), (0, LANE - N_EXPERTS)))
    gates, gates_t, counts = pl.pallas_call(
        _router_body,
        grid=(m // tm,),
        in_specs=[pl.BlockSpec((tm, d), lambda i: (i, 0)), pl.BlockSpec((d, LANE), lambda i: (0, 0))],
        out_specs=[pl.BlockSpec((tm, LANE), lambda i: (i, 0)), pl.BlockSpec((N_EXPERTS, tm), lambda i: (0, i)),
                   pl.BlockSpec((1, 1, LANE), lambda i: (i, 0, 0))],
        out_shape=[jax.ShapeDtypeStruct((m, LANE), jnp.float32),
                   jax.ShapeDtypeStruct((N_EXPERTS, m), jnp.float32),
                   jax.ShapeDtypeStruct((m // tm, 1, LANE), jnp.float32)],
        compiler_params=pltpu.CompilerParams(
            dimension_semantics=("parallel",), vmem_limit_bytes=VMEM_LIMIT_BYTES),
        name="router_gates",
    )(h, r_pad)
    return gates, gates_t, counts[:, 0, :N_EXPERTS].astype(jnp.int32).reshape(-1)


def kernel(x, w_in, ssm_log_dt, ssm_lambda_re, ssm_lambda_im, ssm_b_re, ssm_b_im, ssm_c_re, ssm_c_im,
           ssm_d, ssm_w_glu, ssm_b_glu, ssm_w_out, hgrn_lb_logits, hgrn_norm_g, hgrn_w_out,
           attn_q_norm_g, attn_kv_norm_g, attn_w_uq, attn_w_qidx, attn_w_ukv, attn_w_out, w_o,
           ln_g, ln_b, ffn_w_gate, ffn_w_up, ffn_w_down, moe_router, moe_w_gate, moe_w_up, moe_w_down):
    bsz, seq, d = x.shape
    m = bsz * seq
    bf16 = MXU_DTYPE
    assert MIX_IN_USED + N_BRANCHES * D_MODEL == N_IN
    lb_soft = jax.nn.softmax(hgrn_lb_logits.astype(jnp.float32), axis=0)
    lower_bounds = jnp.concatenate([jnp.zeros_like(lb_soft[:1]), jnp.cumsum(lb_soft[1:], axis=0)], axis=0)
    h = x.reshape(m, d)
    for l in range(DEPTH):
        w_mix = jnp.pad(w_in[l][:, :MIX_IN_USED], ((0, 0), (0, MIX_IN_WIDTH - MIX_IN_USED))).astype(bf16)
        u, hg_in, att_in = in_proj(h, w_mix)
        y_ssm = s5_branch(u.reshape(bsz, seq, SSM_WIDTH), ssm_log_dt[l], ssm_lambda_re[l], ssm_lambda_im[l],
                          ssm_b_re[l], ssm_b_im[l], ssm_c_re[l], ssm_c_im[l], ssm_d[l], ssm_w_glu[l],
                          ssm_b_glu[l], ssm_w_out[l])
        y_hg = hgrn2_branch(hg_in.reshape(bsz, seq, HG_IN_WIDTH), lower_bounds[l], hgrn_norm_g[l],
                            hgrn_w_out[l])
        y_att = dsa_branch(att_in.reshape(bsz, seq, ATT_IN_WIDTH), attn_q_norm_g[l], attn_kv_norm_g[l],
                           attn_w_uq[l], attn_w_qidx[l], attn_w_ukv[l], attn_w_out[l])
        h = merge_project_norm(h, y_ssm.reshape(m, d), y_hg.reshape(m, d), y_att.reshape(m, d),
                               w_in[l][:, MIX_IN_USED:].astype(bf16), w_o[l].astype(bf16),
                               ln_g[l, 0], ln_b[l, 0])
        if l % 2 == 0:
            h = swiglu_norm(h, ffn_w_gate[l // 2].astype(bf16), ffn_w_up[l // 2].astype(bf16),
                            ffn_w_down[l // 2].astype(bf16), ln_g[l, 1], ln_b[l, 1])
        else:
            gate_w = router_gates(h, moe_router[l // 2])
            h = moe_experts_norm(h, gate_w, moe_w_gate[l // 2].astype(bf16),
                                 moe_w_up[l // 2].astype(bf16), moe_w_down[l // 2].astype(bf16),
                                 ln_g[l, 1], ln_b[l, 1])
    return h.reshape(bsz, seq, d)
```

```python
import functools

import jax
import jax.numpy as jnp
import numpy as np
from jax import lax
from jax.experimental import pallas as pl
from jax.experimental.pallas import tpu as pltpu

D_MODEL = 1024
DEPTH = 2
SSM_WIDTH = 256
SSM_GROUP = 16
SSM_GROUPS = SSM_WIDTH // SSM_GROUP
SSM_STATE = 64
HGRN_HEADS = 4
HGRN_DK = 64
HGRN_DV = 64
HGRN_WIDTH = HGRN_HEADS * HGRN_DV
ATT_HEADS = 8
ATT_HEAD_DIM = 64
ATT_WIDTH = ATT_HEADS * ATT_HEAD_DIM
ATT_Q_RANK = 256
ATT_KV_RANK = 128
IDX_HEADS = 4
IDX_DIM = 64
TOPK_MAX = 256
MASK_VALUE = -1e30
N_BRANCHES = 3
N_EXPERTS = 8
TOP_K = 2
DEEPNORM_ALPHA = (2 * DEPTH) ** 0.25
LN_EPS = 1e-5
RMS_EPS = 1e-6
F_MIN = 1e-12

IN_SPLITS = (SSM_WIDTH, HGRN_HEADS * HGRN_DK, HGRN_HEADS * HGRN_DK, HGRN_WIDTH, HGRN_WIDTH,
             ATT_Q_RANK, ATT_KV_RANK, IDX_DIM, IDX_HEADS, N_BRANCHES * D_MODEL)
N_IN = sum(IN_SPLITS)

VMEM_LIMIT_BYTES = 48 * 1024 * 1024
LANE = 128
MXU_DTYPE = jnp.bfloat16


def _round_up(n, m):
    return (n + m - 1) // m * m


HG_IN_WIDTH = 2 * HGRN_HEADS * HGRN_DK + 2 * HGRN_WIDTH
ATT_IN_USED = ATT_Q_RANK + ATT_KV_RANK + IDX_DIM + IDX_HEADS
ATT_IN_WIDTH = _round_up(ATT_IN_USED, LANE)
ATT_IDX_BLOCK = (ATT_Q_RANK + ATT_KV_RANK) // LANE
MIX_IN_USED = SSM_WIDTH + HG_IN_WIDTH + ATT_IN_USED
MIX_IN_WIDTH = SSM_WIDTH + HG_IN_WIDTH + ATT_IN_WIDTH


def _in_proj_body(h_ref, w_ref, u_ref, hg_ref, att_ref):
    r = jnp.dot(h_ref[...].astype(MXU_DTYPE), w_ref[...], preferred_element_type=jnp.float32)
    u_ref[...] = r[:, :SSM_WIDTH]
    hg_ref[...] = r[:, SSM_WIDTH:SSM_WIDTH + HG_IN_WIDTH]
    att_ref[...] = r[:, SSM_WIDTH + HG_IN_WIDTH:]


def in_proj(h, w_mix, *, tm=512):
    m, d = h.shape
    row = lambda i: (i, 0)
    widths = (SSM_WIDTH, HG_IN_WIDTH, ATT_IN_WIDTH)
    return pl.pallas_call(
        _in_proj_body,
        grid=(m // tm,),
        in_specs=[pl.BlockSpec((tm, d), row), pl.BlockSpec((d, MIX_IN_WIDTH), lambda i: (0, 0))],
        out_specs=[pl.BlockSpec((tm, w), row) for w in widths],
        out_shape=[jax.ShapeDtypeStruct((m, w), jnp.float32) for w in widths],
        compiler_params=pltpu.CompilerParams(
            dimension_semantics=("parallel",), vmem_limit_bytes=VMEM_LIMIT_BYTES),
        name="in_proj",
    )(h, w_mix)


def _layer_norm_rows(y, g, b):
    mu = jnp.mean(y, axis=-1, keepdims=True)
    yc = y - mu
    var = jnp.mean(yc * yc, axis=-1, keepdims=True)
    return yc * lax.rsqrt(var + LN_EPS) * g + b


def _merge_body(h_ref, ys_ref, yh_ref, ya_ref, wgate_ref, wo_ref, g_ref, b_ref, o_ref):
    d = D_MODEL
    f32 = jnp.float32
    h = h_ref[...]
    gates = jax.nn.sigmoid(jnp.dot(h.astype(MXU_DTYPE), wgate_ref[...], preferred_element_type=f32))
    mixed = (gates[:, 0:d] * ys_ref[...] + gates[:, d:2 * d] * yh_ref[...]
             + gates[:, 2 * d:3 * d] * ya_ref[...])
    mix_out = jnp.dot(mixed.astype(MXU_DTYPE), wo_ref[...], preferred_element_type=f32)
    o_ref[...] = _layer_norm_rows(DEEPNORM_ALPHA * h + mix_out, g_ref[...], b_ref[...])


def merge_project_norm(h, y_ssm, y_hg, y_att, w_gates, w_o, ln_g, ln_b, *, tm=512):
    m, d = h.shape
    row = lambda i: (i, 0)
    full = lambda i: (0, 0)
    return pl.pallas_call(
        _merge_body,
        grid=(m // tm,),
        in_specs=[pl.BlockSpec((tm, d), row), pl.BlockSpec((tm, d), row), pl.BlockSpec((tm, d), row),
                  pl.BlockSpec((tm, d), row), pl.BlockSpec((d, N_BRANCHES * d), full),
                  pl.BlockSpec((d, d), full), pl.BlockSpec((1, d), full), pl.BlockSpec((1, d), full)],
        out_specs=pl.BlockSpec((tm, d), row),
        out_shape=jax.ShapeDtypeStruct((m, d), jnp.float32),
        compiler_params=pltpu.CompilerParams(
            dimension_semantics=("parallel",), vmem_limit_bytes=VMEM_LIMIT_BYTES),
        name="merge_project_norm",
    )(h, y_ssm, y_hg, y_att, w_gates, w_o, ln_g.reshape(1, d), ln_b.reshape(1, d))


DENSE_FFN_TOKENS = 512
DENSE_FFN_TILES = (1408, 512, 256)


def _ffn_body(h_ref, wg_ref, wu_ref, wd_ref, g_ref, b_ref, o_ref, acc_ref):
    f = pl.program_id(1)

    @pl.when(f == 0)
    def _():
        acc_ref[...] = jnp.zeros_like(acc_ref)

    x = h_ref[...].astype(MXU_DTYPE)
    a = jnp.dot(x, wg_ref[...], preferred_element_type=jnp.float32)
    u = jnp.dot(x, wu_ref[...], preferred_element_type=jnp.float32)
    act = a * jax.nn.sigmoid(a) * u
    acc_ref[...] += jnp.dot(act.astype(MXU_DTYPE), wd_ref[...], preferred_element_type=jnp.float32)

    @pl.when(f == pl.num_programs(1) - 1)
    def _():
        o_ref[...] = _layer_norm_rows(DEEPNORM_ALPHA * h_ref[...] + acc_ref[...], g_ref[...], b_ref[...])


def swiglu_norm(h, wg, wu, wd, ln_g, ln_b):
    m, d = h.shape
    f_dim = wg.shape[1]
    tm = DENSE_FFN_TOKENS
    tf = next(t for t in DENSE_FFN_TILES if f_dim % t == 0)
    assert m % tm == 0
    row = lambda i, f: (i, 0)
    full = lambda i, f: (0, 0)
    return pl.pallas_call(
        _ffn_body,
        grid=(m // tm, f_dim // tf),
        in_specs=[pl.BlockSpec((tm, d), row),
                  pl.BlockSpec((d, tf), lambda i, f: (0, f)), pl.BlockSpec((d, tf), lambda i, f: (0, f)),
                  pl.BlockSpec((tf, d), lambda i, f: (f, 0)),
                  pl.BlockSpec((1, d), full), pl.BlockSpec((1, d), full)],
        out_specs=pl.BlockSpec((tm, d), row),
        out_shape=jax.ShapeDtypeStruct((m, d), jnp.float32),
        scratch_shapes=[pltpu.VMEM((tm, d), jnp.float32)],
        compiler_params=pltpu.CompilerParams(
            dimension_semantics=("parallel", "arbitrary"), vmem_limit_bytes=VMEM_LIMIT_BYTES),
        name="swiglu_norm",
    )(h, wg, wu, wd, ln_g.reshape(1, d), ln_b.reshape(1, d))


S5_CHUNK = 8
S5_CHUNK_WIDTH = S5_CHUNK * SSM_WIDTH
S5_STATE_WIDTH = SSM_GROUPS * SSM_STATE


def s5_operators(log_dt, lam_re, lam_im, b_re, b_im, c_re, c_im):
    f32 = jnp.float32
    n = S5_CHUNK
    lre, lim = lam_re.astype(f32), lam_im.astype(f32)
    dt = jnp.exp(log_dt.astype(f32))[:, None]
    mag = jnp.exp(lre * dt)
    ang = lim * dt
    a_re, a_im = mag * jnp.cos(ang), mag * jnp.sin(ang)
    den = lre * lre + lim * lim
    coef_re = ((a_re - 1.0) * lre + a_im * lim) / den
    coef_im = (a_im * lre - (a_re - 1.0) * lim) / den
    br, bi = b_re.astype(f32), b_im.astype(f32)
    bb_re = coef_re[..., None] * br - coef_im[..., None] * bi
    bb_im = coef_re[..., None] * bi + coef_im[..., None] * br
    j = jnp.arange(n + 1, dtype=f32)[:, None, None]
    pw_re = jnp.exp(j * lre * dt) * jnp.cos(j * ang)
    pw_im = jnp.exp(j * lre * dt) * jnp.sin(j * ang)
    eye = jnp.eye(SSM_GROUPS, dtype=f32)
    cr, ci = c_re.astype(f32), c_im.astype(f32)
    ab_re = pw_re[..., None] * bb_re - pw_im[..., None] * bb_im
    ab_im = pw_re[..., None] * bb_im + pw_im[..., None] * bb_re
    kern = (jnp.einsum('gcp,jgpd->jgcd', cr, ab_re[:n]) - jnp.einsum('gcp,jgpd->jgcd', ci, ab_im[:n]))
    lag = jnp.arange(n)[None, :] - jnp.arange(n)[:, None]
    toep = jnp.where((lag >= 0)[:, :, None, None, None], kern[jnp.maximum(lag, 0)], 0.0)
    t_mat = jnp.einsum('abgcd,gh->agdbhc', toep, eye).reshape(S5_CHUNK_WIDTH, S5_CHUNK_WIDTH)
    v = jnp.stack([ab_re[:n][::-1], ab_im[:n][::-1]])
    w_in = jnp.einsum('rlgpc,gh->lhcrgp', v, eye).reshape(S5_CHUNK_WIDTH, 2 * S5_STATE_WIDTH)
    ar, ai = pw_re[1:], pw_im[1:]
    wo_re = jnp.einsum('gcp,lgp->gplc', cr, ar) - jnp.einsum('gcp,lgp->gplc', ci, ai)
    wo_im = -jnp.einsum('gcp,lgp->gplc', cr, ai) - jnp.einsum('gcp,lgp->gplc', ci, ar)
    w_out = jnp.einsum('rgplc,gh->rhplgc', jnp.stack([wo_re, wo_im]), eye)
    w_out = w_out.reshape(2 * S5_STATE_WIDTH, S5_CHUNK_WIDTH)
    decay = jnp.stack([pw_re[n].reshape(1, S5_STATE_WIDTH), pw_im[n].reshape(1, S5_STATE_WIDTH)])
    return (jnp.concatenate([t_mat, w_in], axis=1).astype(MXU_DTYPE), w_out.astype(MXU_DTYPE), decay)


def _s5_scan_body(u_ref, tw_ref, wout_ref, decay_ref, y_ref, h_ref, s_ref, hs_ref, *, tm):
    f32 = jnp.float32
    sw = S5_STATE_WIDTH

    @pl.when(pl.program_id(1) == 0)
    def _():
        h_ref[...] = jnp.zeros_like(h_ref)

    r = jnp.dot(u_ref[0].astype(MXU_DTYPE), tw_ref[...], preferred_element_type=f32)
    s_ref[...] = r[:, S5_CHUNK_WIDTH:]
    d_re = decay_ref[0]
    d_im = decay_ref[1]

    def eight_chunks(k, carry):
        h_re, h_im = carry
        r0 = pl.multiple_of(k * 8, 8)
        inc = s_ref[pl.ds(r0, 8), :]
        rows_re, rows_im = [], []
        for t in range(8):
            rows_re.append(h_re)
            rows_im.append(h_im)
            h_re, h_im = (d_re * h_re - d_im * h_im + inc[t:t + 1, :sw],
                          d_re * h_im + d_im * h_re + inc[t:t + 1, sw:])
        hs_ref[pl.ds(r0, 8), :] = jnp.concatenate(
            [jnp.concatenate(rows_re, axis=0), jnp.concatenate(rows_im, axis=0)], axis=1)
        return h_re, h_im

    h_re, h_im = lax.fori_loop(0, tm // 8, eight_chunks, (h_ref[0:1, :], h_ref[1:2, :]))
    h_ref[0:1, :] = h_re
    h_ref[1:2, :] = h_im
    y_ref[0] = r[:, :S5_CHUNK_WIDTH] + jnp.dot(hs_ref[...].astype(MXU_DTYPE), wout_ref[...],
                                                preferred_element_type=f32)


def s5_scan(u, tw, w_out, decay, *, tm=256):
    bsz, seq, _ = u.shape
    n_chunks = seq // S5_CHUNK
    tm = min(tm, n_chunks)
    assert seq % S5_CHUNK == 0 and n_chunks % tm == 0 and tm % 8 == 0
    once = pl.Buffered(1)
    y = pl.pallas_call(
        functools.partial(_s5_scan_body, tm=tm),
        grid=(bsz, n_chunks // tm),
        in_specs=[pl.BlockSpec((1, tm, S5_CHUNK_WIDTH), lambda b, i: (b, i, 0)),
                  pl.BlockSpec(tw.shape, lambda b, i: (0, 0), pipeline_mode=once),
                  pl.BlockSpec(w_out.shape, lambda b, i: (0, 0), pipeline_mode=once),
                  pl.BlockSpec(decay.shape, lambda b, i: (0, 0, 0), pipeline_mode=once)],
        out_specs=pl.BlockSpec((1, tm, S5_CHUNK_WIDTH), lambda b, i: (b, i, 0)),
        out_shape=jax.ShapeDtypeStruct((bsz, n_chunks, S5_CHUNK_WIDTH), jnp.float32),
        scratch_shapes=[pltpu.VMEM((2, S5_STATE_WIDTH), jnp.float32),
                        pltpu.VMEM((tm, 2 * S5_STATE_WIDTH), jnp.float32),
                        pltpu.VMEM((tm, 2 * S5_STATE_WIDTH), jnp.float32)],
        compiler_params=pltpu.CompilerParams(
            dimension_semantics=("parallel", "arbitrary"), vmem_limit_bytes=VMEM_LIMIT_BYTES),
        name="s5_scan",
    )(u.reshape(bsz, n_chunks, S5_CHUNK_WIDTH), tw, w_out, decay)
    return y.reshape(bsz, seq, SSM_WIDTH)


def _s5_out_body(y_ref, u_ref, d_ref, wglu_ref, bglu_ref, wout_ref, o_ref):
    f32 = jnp.float32
    y = jax.nn.gelu(y_ref[...] + d_ref[...] * u_ref[...])
    gate = jnp.dot(y.astype(MXU_DTYPE), wglu_ref[...], preferred_element_type=f32) + bglu_ref[...]
    y = y * jax.nn.sigmoid(gate)
    o_ref[...] = jnp.dot(y.astype(MXU_DTYPE), wout_ref[...], preferred_element_type=f32)


def s5_output(y, u, d_skip, w_glu, b_glu, w_out, *, tm=1024):
    m, c = y.shape
    tm = min(tm, m)
    row = lambda i: (i, 0)
    full = lambda i: (0, 0)
    return pl.pallas_call(
        _s5_out_body,
        grid=(m // tm,),
        in_specs=[pl.BlockSpec((tm, c), row), pl.BlockSpec((tm, c), row), pl.BlockSpec((1, c), full),
                  pl.BlockSpec((c, c), full), pl.BlockSpec((1, c), full), pl.BlockSpec((c, D_MODEL), full)],
        out_specs=pl.BlockSpec((tm, D_MODEL), row),
        out_shape=jax.ShapeDtypeStruct((m, D_MODEL), jnp.float32),
        compiler_params=pltpu.CompilerParams(
            dimension_semantics=("parallel",), vmem_limit_bytes=VMEM_LIMIT_BYTES),
        name="s5_output",
    )(y, u, d_skip.reshape(1, c), w_glu.astype(MXU_DTYPE), b_glu.reshape(1, c), w_out.astype(MXU_DTYPE))


def s5_branch(u, log_dt, lam_re, lam_im, b_re, b_im, c_re, c_im, d_skip, w_glu, b_glu, w_out):
    bsz, seq, _ = u.shape
    tw, w_state_out, decay = s5_operators(log_dt, lam_re, lam_im, b_re, b_im, c_re, c_im)
    y = s5_scan(u, tw, w_state_out, decay)
    out = s5_output(y.reshape(bsz * seq, SSM_WIDTH), u.reshape(bsz * seq, SSM_WIDTH),
                    d_skip, w_glu, b_glu, w_out)
    return out.reshape(bsz, seq, D_MODEL)


HG_CHUNK = 128
HG_LEVELS = 7
HG_KDIM = HGRN_HEADS * HGRN_DK


def _hgrn_segment_sums():
    c = HG_CHUNK
    t = np.arange(c)[:, None]
    u = np.arange(c)[None, :]
    blocks = []
    for lvl in range(1, HG_LEVELS + 1):
        m = (t >> lvl << lvl) + (1 << (lvl - 1)) - 1
        right = ((t >> (lvl - 1)) & 1) == 1
        blocks.append(np.where(right, (u > m) & (u <= t), (u > t) & (u <= m)))
    blocks.append(u <= t)
    blocks.append(u > t)
    return np.concatenate(blocks, axis=0).astype(np.float32)


def _hgrn_body(q_ref, z_ref, v_ref, g_ref, seg_ref, lb_ref, ng_ref, hmean_ref, wout_ref, o_ref, st_ref):
    f32 = jnp.float32
    c = HG_CHUNK

    @pl.when(pl.program_id(1) == 0)
    def _():
        st_ref[...] = jnp.zeros_like(st_ref)

    q = q_ref[0]
    z = z_ref[0]
    v = v_ref[0]
    lb = lb_ref[...]
    f = lb + (1.0 - lb) * jax.nn.sigmoid(z)
    logf = jnp.log(jnp.maximum(f, F_MIN))
    kin = (1.0 - lb) * jax.nn.sigmoid(-z)

    p1 = logf.astype(MXU_DTYPE)
    r1 = logf - p1.astype(f32)
    p2 = r1.astype(MXU_DTYPE)
    p3 = (r1 - p2.astype(f32)).astype(MXU_DTYPE)
    seg = seg_ref[...]
    sums = (jnp.dot(seg, p1, preferred_element_type=f32) + jnp.dot(seg, p2, preferred_element_type=f32)
            + jnp.dot(seg, p3, preferred_element_type=f32))

    lane_head = lax.broadcasted_iota(jnp.int32, (c, HG_KDIM), 1) // HGRN_DK
    tok = lax.broadcasted_iota(jnp.int32, (c, HG_KDIM), 0)
    row_t = lax.broadcasted_iota(jnp.int32, (HGRN_HEADS * c, c), 0) % c
    col_s = lax.broadcasted_iota(jnp.int32, (HGRN_HEADS * c, c), 1)

    def per_head_rows(x):
        return jnp.concatenate([jnp.where(lane_head == h, x, 0.0) for h in range(HGRN_HEADS)],
                               axis=0).astype(MXU_DTYPE)

    def scores(ql, kl):
        return lax.dot_general(per_head_rows(ql), kl.astype(MXU_DTYPE), (((1,), (1,)), ((), ())),
                               preferred_element_type=f32)

    att = jnp.where(row_t == col_s, scores(q, kin), 0.0)
    for lvl in range(1, HG_LEVELS + 1):
        decay = jnp.exp(sums[(lvl - 1) * c:lvl * c])
        right = ((tok >> (lvl - 1)) & 1) == 1
        a = scores(jnp.where(right, q * decay, 0.0), jnp.where(right, 0.0, kin * decay))
        att = att + jnp.where((row_t >> lvl) == (col_s >> lvl), a, 0.0)

    b = sums[HG_LEVELS * c:(HG_LEVELS + 1) * c]
    tail = sums[(HG_LEVELS + 1) * c:(HG_LEVELS + 2) * c]
    v_m = v.astype(MXU_DTYPE)
    st = st_ref[...]
    o = lax.dot_general((q * jnp.exp(b)).astype(MXU_DTYPE), st.astype(MXU_DTYPE),
                        (((1,), (1,)), ((), ())), preferred_element_type=f32)
    for h in range(HGRN_HEADS):
        o_h = jnp.dot(att[h * c:(h + 1) * c].astype(MXU_DTYPE), v_m, preferred_element_type=f32)
        o = o + jnp.where(lane_head == h, o_h, 0.0)

    kv = jnp.dot(v.T.astype(MXU_DTYPE), (kin * jnp.exp(tail)).astype(MXU_DTYPE), preferred_element_type=f32)
    sr = lax.broadcasted_iota(jnp.int32, st.shape, 0) // HGRN_DV
    sc = lax.broadcasted_iota(jnp.int32, st.shape, 1) // HGRN_DK
    st_ref[...] = st * jnp.exp(b[c - 1:c, :]) + jnp.where(sr == sc, kv, 0.0)

    o2 = o * o
    o2_hi = o2.astype(MXU_DTYPE)
    o2_lo = (o2 - o2_hi.astype(f32)).astype(MXU_DTYPE)
    ms = (jnp.dot(o2_hi, hmean_ref[...], preferred_element_type=f32)
          + jnp.dot(o2_lo, hmean_ref[...], preferred_element_type=f32))
    g = g_ref[0]
    out = o * lax.rsqrt(ms + RMS_EPS) * ng_ref[...] * (g * jax.nn.sigmoid(g))
    o_ref[0] = jnp.dot(out.astype(MXU_DTYPE), wout_ref[...], preferred_element_type=f32)


def hgrn2_branch(hg_in, lower_bound, norm_g, w_out):
    bsz, seq, _ = hg_in.shape
    assert seq % HG_CHUNK == 0 and HGRN_DK == HGRN_DV and HG_IN_WIDTH == 4 * HG_KDIM
    seg = jnp.asarray(_hgrn_segment_sums(), MXU_DTYPE)
    head_mean = jnp.asarray(np.kron(np.eye(HGRN_HEADS), np.full((HGRN_DV, HGRN_DV), 1.0 / HGRN_DV)), MXU_DTYPE)
    tok = lambda b, i: (b, i, 0)
    full = lambda b, i: (0, 0)
    part = lambda k: pl.BlockSpec((1, HG_CHUNK, HG_KDIM), lambda b, i: (b, i, k))
    return pl.pallas_call(
        _hgrn_body,
        grid=(bsz, seq // HG_CHUNK),
        in_specs=[part(0), part(1), part(2), part(3),
                  pl.BlockSpec(seg.shape, full), pl.BlockSpec((1, HG_KDIM), full),
                  pl.BlockSpec((1, HGRN_WIDTH), full), pl.BlockSpec(head_mean.shape, full),
                  pl.BlockSpec((HGRN_WIDTH, D_MODEL), full)],
        out_specs=pl.BlockSpec((1, HG_CHUNK, D_MODEL), tok),
        out_shape=jax.ShapeDtypeStruct((bsz, seq, D_MODEL), jnp.float32),
        scratch_shapes=[pltpu.VMEM((HGRN_WIDTH, HG_KDIM), jnp.float32)],
        compiler_params=pltpu.CompilerParams(
            dimension_semantics=("parallel", "arbitrary"), vmem_limit_bytes=VMEM_LIMIT_BYTES),
        name="hgrn2",
    )(hg_in, hg_in, hg_in, hg_in, seg, lower_bound.reshape(1, HG_KDIM).astype(jnp.float32),
      jnp.tile(norm_g.astype(jnp.float32), HGRN_HEADS).reshape(1, HGRN_WIDTH), head_mean,
      w_out.astype(MXU_DTYPE))


ATT_QK_DIM = LANE
DSA_TILE = 512


def _rms_rows(x, g):
    return x * lax.rsqrt(jnp.mean(x * x, axis=-1, keepdims=True) + RMS_EPS) * g


def _dsa_project_body(x_ref, gq_ref, gkv_ref, wq_ref, wqi_ref, wkt_ref, wv_ref,
                      q_ref, qi_ref, kt_ref, v_ref, kidt_ref):
    f32 = jnp.float32
    tm = x_ref.shape[1]
    x = x_ref[0]
    cq = _rms_rows(x[:, :ATT_Q_RANK], gq_ref[...]).astype(MXU_DTYPE)
    ckv = _rms_rows(x[:, ATT_Q_RANK:ATT_Q_RANK + ATT_KV_RANK], gkv_ref[...])
    q_all = jnp.dot(cq, wq_ref[...], preferred_element_type=f32)
    qi_all = jnp.dot(cq, wqi_ref[...], preferred_element_type=f32)
    v_all = jnp.dot(ckv.astype(MXU_DTYPE), wv_ref[...], preferred_element_type=f32)
    kt_all = jnp.dot(wkt_ref[...], ckv.T.astype(MXU_DTYPE), preferred_element_type=f32)

    lane = lax.broadcasted_iota(jnp.int32, (tm, LANE), 1)
    t_loc = lax.broadcasted_iota(jnp.int32, (tm, LANE), 0)
    t_even = (t_loc // 2 * 2).astype(f32)
    t_odd = (t_loc % 2).astype(f32)
    sub = lax.broadcasted_iota(jnp.int32, (LANE, tm), 0)
    s_loc = lax.broadcasted_iota(jnp.int32, (LANE, tm), 1)
    k_rows = jnp.where(sub < ATT_HEAD_DIM + 2, 1.0,
                       jnp.where(sub == ATT_HEAD_DIM + 2, (s_loc // 2 * 2).astype(f32), (s_loc % 2).astype(f32)))
    for h in range(ATT_HEADS):
        slope = 2.0 ** (-8.0 * (h + 1) / ATT_HEADS)
        q_cols = jnp.where(lane == ATT_HEAD_DIM, -slope * t_even,
                           jnp.where(lane == ATT_HEAD_DIM + 1, -slope * t_odd, slope))
        q_h = q_all[:, h * LANE:(h + 1) * LANE]
        q_ref[0, h] = jnp.where(lane < ATT_HEAD_DIM, q_h,
                                jnp.where(lane < ATT_HEAD_DIM + 4, q_cols, 0.0)).astype(q_ref.dtype)
        k_h = kt_all[h * LANE:(h + 1) * LANE, :]
        kt_ref[0, h] = jnp.where(sub < ATT_HEAD_DIM, k_h,
                                 jnp.where(sub < ATT_HEAD_DIM + 4, k_rows, 0.0)).astype(kt_ref.dtype)
        v_h = v_all[:, h * LANE:(h + 1) * LANE]
        v_ref[0, h] = jnp.where(lane == ATT_HEAD_DIM, 1.0, v_h).astype(v_ref.dtype)
    for h in range(IDX_HEADS):
        qi_ref[0, h] = qi_all[:, h * LANE:(h + 1) * LANE].astype(qi_ref.dtype)
    kidt_ref[0] = x[:, ATT_IDX_BLOCK * LANE:(ATT_IDX_BLOCK + 1) * LANE].T.astype(kidt_ref.dtype)


def _head_padded(w, n_heads, dim):
    k = w.shape[0]
    return jnp.pad(w.reshape(k, n_heads, dim), ((0, 0), (0, 0), (0, LANE - dim))).reshape(k, n_heads * LANE)


def dsa_project(att_in, q_norm_g, kv_norm_g, w_uq, w_qidx, w_ukv):
    bsz, seq, _ = att_in.shape
    tm = DSA_TILE
    assert seq % tm == 0 and 8 % ATT_HEADS == 0 and tm <= 512
    w_q = _head_padded(w_uq * ATT_HEAD_DIM ** -0.5, ATT_HEADS, ATT_HEAD_DIM).astype(MXU_DTYPE)
    w_qi = _head_padded(w_qidx * IDX_DIM ** -0.5, IDX_HEADS, IDX_DIM).astype(MXU_DTYPE)
    w_kt = _head_padded(w_ukv[:, :ATT_WIDTH], ATT_HEADS, ATT_HEAD_DIM).T.astype(MXU_DTYPE)
    w_v = _head_padded(w_ukv[:, ATT_WIDTH:], ATT_HEADS, ATT_HEAD_DIM).astype(MXU_DTYPE)
    tok = lambda b, i: (b, 0, i, 0)
    full = lambda b, i: (0, 0)
    dt = MXU_DTYPE
    return pl.pallas_call(
        _dsa_project_body,
        grid=(bsz, seq // tm),
        in_specs=[pl.BlockSpec((1, tm, ATT_IN_WIDTH), lambda b, i: (b, i, 0)),
                  pl.BlockSpec((1, ATT_Q_RANK), full), pl.BlockSpec((1, ATT_KV_RANK), full),
                  pl.BlockSpec(w_q.shape, full), pl.BlockSpec(w_qi.shape, full),
                  pl.BlockSpec(w_kt.shape, full), pl.BlockSpec(w_v.shape, full)],
        out_specs=[pl.BlockSpec((1, ATT_HEADS, tm, LANE), tok),
                   pl.BlockSpec((1, IDX_HEADS, tm, LANE), tok),
                   pl.BlockSpec((1, ATT_HEADS, LANE, tm), lambda b, i: (b, 0, 0, i)),
                   pl.BlockSpec((1, ATT_HEADS, tm, LANE), tok),
                   pl.BlockSpec((1, LANE, tm), lambda b, i: (b, 0, i))],
        out_shape=[jax.ShapeDtypeStruct((bsz, ATT_HEADS, seq, LANE), dt),
                   jax.ShapeDtypeStruct((bsz, IDX_HEADS, seq, LANE), dt),
                   jax.ShapeDtypeStruct((bsz, ATT_HEADS, LANE, seq), dt),
                   jax.ShapeDtypeStruct((bsz, ATT_HEADS, seq, LANE), dt),
                   jax.ShapeDtypeStruct((bsz, LANE, seq), dt)],
        compiler_params=pltpu.CompilerParams(
            dimension_semantics=("parallel", "parallel"), vmem_limit_bytes=VMEM_LIMIT_BYTES),
        name="dsa_project",
    )(att_in, q_norm_g.reshape(1, ATT_Q_RANK), kv_norm_g.reshape(1, ATT_KV_RANK), w_q, w_qi, w_kt, w_v)


_INT_MIN = -2 ** 31
_MASK_KEY = int(np.float32(MASK_VALUE).view(np.int32)) ^ 0x7FFFFFFF
_SEARCH_ROWS = 128
_ATT_ROWS = 32
_SEARCH_LANES = 512
_LIST_DEPTH = 12
_LIST_ROWS = 32
_LIST_MIN_BLOCKS = 4


def _key_to_score(key):
    return lax.bitcast_convert_type(jnp.where(key < 0, key ^ 0x7FFFFFFF, key), jnp.float32)


def _dsa_body(qi_ref, kj_ref, qidx_ref, w_ref, kidt_ref, q_ref, kt_ref, v_ref, tri_ref, wout_ref,
              o_ref, keys_ref, thr_ref, need_ref, carry_ref, m_ref, acc_ref, bias_ref, s_ref, p_ref,
              alpha_ref, cand_ref, *, tq, tk, seq, n_sel):
    f32 = jnp.float32
    p_id = pl.program_id(1)
    i = qi_ref[p_id]
    j = kj_ref[p_id]
    row_local = lax.broadcasted_iota(jnp.int32, (tq, tk), 0)
    col_local = lax.broadcasted_iota(jnp.int32, (tq, tk), 1)

    @pl.when(j == 0)
    def _select():
        def score_block(jj, carry):
            off = pl.multiple_of(jj * tk, tk)
            kb = kidt_ref[0, :, pl.ds(off, tk)]
            sc = jnp.zeros((tq, tk), f32)
            for h in range(IDX_HEADS):
                s = jnp.dot(qidx_ref[0, h], kb, preferred_element_type=f32)
                w_h = w_ref[0, :, IDX_DIM + h:IDX_DIM + h + 1] * IDX_HEADS ** -0.5
                sc = sc + jnp.maximum(s, 0.0) * w_h
            causal = (col_local + jj * tk) <= (row_local + i * tq)
            sc = jnp.where(causal, sc, MASK_VALUE)
            keys_ref[:, pl.ds(off, tk)] = jnp.where(sc == 0.0, 0.0, sc)
            return carry

        lax.fori_loop(0, i + 1, score_block, 0)

        n_blocks = i + 1
        n_masked_tail = (seq - n_blocks * tk).astype(f32)

        def row_group(r, carry):
            r0 = pl.multiple_of(r * _SEARCH_ROWS, _SEARCH_ROWS)

            def count_ge(ref, n_iter, cand):
                cand_b = jnp.broadcast_to(_key_to_score(cand), (_SEARCH_ROWS, LANE))

                def chunk(c, acc):
                    base = pl.multiple_of(c * _SEARCH_LANES, _SEARCH_LANES)
                    for u in range(_SEARCH_LANES // LANE):
                        kk = ref[pl.ds(r0, _SEARCH_ROWS), pl.ds(base + u * LANE, LANE)]
                        acc = acc + jnp.where(kk >= cand_b, 1.0, 0.0)
                    return acc

                acc = lax.fori_loop(0, n_iter, chunk, jnp.zeros((_SEARCH_ROWS, LANE), f32))
                cnt = jnp.sum(acc, axis=1, keepdims=True)
                return cnt + jnp.where(cand <= _MASK_KEY, n_masked_tail, 0.0)

            def kth_largest_key(count):
                zero = jnp.zeros((_SEARCH_ROWS, 1), jnp.int32)
                v0 = jnp.where(count(zero) >= n_sel, zero, zero + _INT_MIN)

                def bit_step(b, v):
                    cand = v | jnp.left_shift(jnp.int32(1), 30 - b)
                    return jnp.where(count(cand) >= n_sel, cand, v)

                return lax.fori_loop(0, 31, bit_step, v0)

            count_all = functools.partial(count_ge, keys_ref, n_blocks * (tk // _SEARCH_LANES))

            def store(v, n_above):
                thr_ref[pl.ds(r0, _SEARCH_ROWS), :] = _key_to_score(v)
                need_ref[pl.ds(r0, _SEARCH_ROWS), :] = n_sel - n_above

            def search_all_keys():
                v = kth_largest_key(count_all)
                store(v, count_all(v + 1))

            @pl.when(n_blocks < _LIST_MIN_BLOCKS)
            def _():
                search_all_keys()

            @pl.when(n_blocks >= _LIST_MIN_BLOCKS)
            def _():
                def shortlist(sub, carry):
                    rr = pl.multiple_of(r0 + sub * _LIST_ROWS, _LIST_ROWS)

                    def insert_block(c, tops):
                        base = pl.multiple_of(c * tk, tk)
                        for u in range(tk // LANE):
                            x = keys_ref[pl.ds(rr, _LIST_ROWS), pl.ds(base + u * LANE, LANE)]
                            new = []
                            for t in range(_LIST_DEPTH):
                                new.append(jnp.maximum(tops[t], x))
                                x = jnp.minimum(tops[t], x)
                            tops = tuple(new)
                        return tops

                    tops = lax.fori_loop(0, n_blocks, insert_block,
                                         tuple(jnp.full((_LIST_ROWS, LANE), -jnp.inf, f32)
                                               for _ in range(_LIST_DEPTH)))
                    for t in range(_LIST_DEPTH):
                        cand_ref[pl.ds(rr, _LIST_ROWS), t * LANE:(t + 1) * LANE] = tops[t]
                    return carry

                lax.fori_loop(0, _SEARCH_ROWS // _LIST_ROWS, shortlist, 0)
                v = kth_largest_key(functools.partial(count_ge, cand_ref,
                                                      _LIST_DEPTH * LANE // _SEARCH_LANES))
                n_above = count_all(v + 1)
                store(v, n_above)
                pl.when(jnp.max(n_above) >= n_sel)(search_all_keys)

            return carry

        lax.fori_loop(0, tq // _SEARCH_ROWS, row_group, 0)
        carry_ref[...] = jnp.zeros_like(carry_ref)
        m_ref[...] = jnp.full_like(m_ref, MASK_VALUE)
        acc_ref[...] = jnp.zeros_like(acc_ref)

    keys_blk = keys_ref[:, pl.ds(pl.multiple_of(j * tk, tk), tk)]
    thr = thr_ref[...]
    eq = keys_blk == thr
    tie_rank = carry_ref[...] + jnp.dot(jnp.where(eq, 1.0, 0.0).astype(MXU_DTYPE), tri_ref[...],
                                         preferred_element_type=f32)
    carry_ref[...] = tie_rank[:, tk - 1:tk]
    tie_bias = jnp.where(eq, jnp.where(tie_rank <= need_ref[...], 0.0, MASK_VALUE), MASK_VALUE)
    bias = jnp.where(keys_blk > thr, 0.0, tie_bias)
    causal = (row_local + i * tq) >= (col_local + j * tk)
    bias_ref[...] = jnp.where(causal, bias, MASK_VALUE)

    def logits(h, slot):
        s_ref[slot] = jnp.dot(q_ref[0, h], kt_ref[0, h], preferred_element_type=f32)

    def softmax(h, slot):
        slope_tk = jnp.right_shift(tk, h + 1).astype(f32)
        for r in range(0, tq, _ATT_ROWS):
            rows = pl.ds(r, _ATT_ROWS)
            s = s_ref[slot, rows, :] + bias_ref[rows, :]
            m_old = m_ref[h, rows, :] - slope_tk
            m_new = jnp.maximum(m_old, jnp.max(s, axis=1, keepdims=True))
            p_ref[slot, rows, :] = jnp.exp(s - m_new).astype(MXU_DTYPE)
            alpha_ref[slot, rows, :] = jnp.exp(m_old - m_new)
            m_ref[h, rows, :] = m_new

    def weighted_values(h, slot):
        acc_ref[h] = alpha_ref[slot] * acc_ref[h] + jnp.dot(p_ref[slot], v_ref[0, h],
                                                             preferred_element_type=f32)

    for t in range(ATT_HEADS + 2):
        if t < ATT_HEADS:
            logits(t, t % 2)
        if 1 <= t <= ATT_HEADS:
            softmax(t - 1, (t - 1) % 2)
        if t >= 2:
            weighted_values(t - 2, t % 2)

    @pl.when(j == i)
    def _finish():
        heads = []
        for h in range(ATT_HEADS):
            a = acc_ref[h]
            heads.append((a[:, :ATT_HEAD_DIM] / a[:, ATT_HEAD_DIM:ATT_HEAD_DIM + 1]).astype(MXU_DTYPE))
        o_ref[0] = jnp.dot(jnp.concatenate(heads, axis=1), wout_ref[...], preferred_element_type=f32)


def dsa_attention(att_in, q_idx, kid_t, q, k_t, v_aug, w_out):
    bsz, _, seq, _ = q.shape
    tq = tk = DSA_TILE
    n_sel = min(TOPK_MAX, seq // 4)
    nq = seq // tq
    assert seq % tq == 0 and tk % _SEARCH_LANES == 0 and tq % _SEARCH_ROWS == 0
    pairs = [(a, b) for a in range(nq) for b in range(a + 1)]
    qi = jnp.asarray([a for a, _ in pairs], jnp.int32)
    kj = jnp.asarray([b for _, b in pairs], jnp.int32)
    tri = jnp.triu(jnp.ones((tk, tk), MXU_DTYPE))
    grid_spec = pltpu.PrefetchScalarGridSpec(
        num_scalar_prefetch=2,
        grid=(bsz, len(pairs)),
        in_specs=[
            pl.BlockSpec((1, IDX_HEADS, tq, LANE), lambda b, p, qi, kj: (b, 0, qi[p], 0)),
            pl.BlockSpec((1, tq, LANE), lambda b, p, qi, kj: (b, qi[p], ATT_IDX_BLOCK)),
            pl.BlockSpec((1, LANE, seq), lambda b, p, qi, kj: (b, 0, 0)),
            pl.BlockSpec((1, ATT_HEADS, tq, ATT_QK_DIM), lambda b, p, qi, kj: (b, 0, qi[p], 0)),
            pl.BlockSpec((1, ATT_HEADS, ATT_QK_DIM, tk), lambda b, p, qi, kj: (b, 0, 0, kj[p])),
            pl.BlockSpec((1, ATT_HEADS, tk, LANE), lambda b, p, qi, kj: (b, 0, kj[p], 0)),
            pl.BlockSpec((tk, tk), lambda b, p, qi, kj: (0, 0)),
            pl.BlockSpec((ATT_WIDTH, D_MODEL), lambda b, p, qi, kj: (0, 0)),
        ],
        out_specs=pl.BlockSpec((1, tq, D_MODEL), lambda b, p, qi, kj: (b, qi[p], 0)),
        scratch_shapes=[
            pltpu.VMEM((tq, seq), jnp.float32),
            pltpu.VMEM((tq, 1), jnp.float32),
            pltpu.VMEM((tq, 1), jnp.float32),
            pltpu.VMEM((tq, 1), jnp.float32),
            pltpu.VMEM((ATT_HEADS, tq, 1), jnp.float32),
            pltpu.VMEM((ATT_HEADS, tq, LANE), jnp.float32),
            pltpu.VMEM((tq, tk), jnp.float32),
            pltpu.VMEM((2, tq, tk), jnp.float32),
            pltpu.VMEM((2, tq, tk), MXU_DTYPE),
            pltpu.VMEM((2, tq, 1), jnp.float32),
            pltpu.VMEM((tq, _LIST_DEPTH * LANE), jnp.float32),
        ])
    return pl.pallas_call(
        functools.partial(_dsa_body, tq=tq, tk=tk, seq=seq, n_sel=n_sel),
        grid_spec=grid_spec,
        out_shape=jax.ShapeDtypeStruct((bsz, seq, D_MODEL), jnp.float32),
        compiler_params=pltpu.CompilerParams(
            dimension_semantics=("parallel", "arbitrary"), vmem_limit_bytes=VMEM_LIMIT_BYTES),
        name="dsa_attention",
    )(qi, kj, q_idx, att_in, kid_t, q, k_t, v_aug, tri, w_out)


def dsa_branch(att_in, q_norm_g, kv_norm_g, w_uq, w_qidx, w_ukv, w_out):
    q, q_idx, k_t, v_aug, kid_t = dsa_project(att_in, q_norm_g, kv_norm_g, w_uq, w_qidx, w_ukv)
    return dsa_attention(att_in, q_idx, kid_t, q, k_t, v_aug, w_out.astype(MXU_DTYPE))


MOE_TOKENS = 1024
MOE_ROW_CLASSES = (256, 288, 320, 512, 1024)
MOE_FFN_TILES = (896, 512, 256)
MOE_VMEM_LIMIT_BYTES = 56 * 1024 * 1024


def _moe_body(cnt_ref, h_ref, gate_ref, gatet_ref, tri_ref, wg_ref, wu_ref, wd_ref, g_ref, b_ref, o_ref,
              xb_ref, gather_ref, scatter_ref, xc_ref, yc_ref, acc_ref, rank_ref):
    f32 = jnp.float32
    tm = MOE_TOKENS
    i = pl.program_id(0)
    e = pl.program_id(1)
    f = pl.program_id(2)
    last_f = pl.num_programs(2) - 1

    @pl.when((e == 0) & (f == 0))
    def _():
        xb_ref[...] = h_ref[...].astype(MXU_DTYPE)
        acc_ref[...] = jnp.zeros_like(acc_ref)
        routed_rows = jnp.where(gatet_ref[...] != 0.0, 1.0, 0.0).astype(MXU_DTYPE)
        rank_ref[...] = jnp.dot(routed_rows, tri_ref[...], preferred_element_type=f32)

    def gate_column():
        lane = lax.broadcasted_iota(jnp.int32, gate_ref.shape, 1)
        return jnp.sum(jnp.where(lane == e, gate_ref[...], 0.0), axis=-1, keepdims=True)

    def routed(size):
        def run():
            @pl.when(f == 0)
            def _():
                active_row = jnp.where(gatet_ref[pl.ds(e, 1), :] != 0.0, 1.0, 0.0)
                rank_row = rank_ref[pl.ds(e, 1), :]
                slot = lax.broadcasted_iota(jnp.int32, (size, tm), 0).astype(f32)
                gather = jnp.where(slot == rank_row, active_row, 0.0)
                gather_ref[0:size, :] = gather.astype(MXU_DTYPE)
                rank_col = jnp.broadcast_to(rank_row, (LANE, tm)).T[:, 0:1]
                slot_l = lax.broadcasted_iota(jnp.int32, (tm, size), 1).astype(f32)
                scatter = jnp.where(slot_l == rank_col, jnp.where(gate_column() != 0.0, 1.0, 0.0), 0.0)
                scatter_ref[:, 0:size] = scatter.astype(MXU_DTYPE)
                xc_ref[0:size, :] = jnp.dot(gather.astype(MXU_DTYPE), xb_ref[...],
                                            preferred_element_type=f32).astype(MXU_DTYPE)
                yc_ref[0:size, :] = jnp.zeros((size, D_MODEL), f32)

            xc = xc_ref[0:size, :]
            a = jnp.dot(xc, wg_ref[0], preferred_element_type=f32)
            u = jnp.dot(xc, wu_ref[0], preferred_element_type=f32)
            act = a * jax.nn.sigmoid(a) * u
            yc_ref[0:size, :] += jnp.dot(act.astype(MXU_DTYPE), wd_ref[0], preferred_element_type=f32)

            @pl.when(f == last_f)
            def _():
                back = jnp.dot(scatter_ref[:, 0:size], yc_ref[0:size, :].astype(MXU_DTYPE),
                               preferred_element_type=f32)
                acc_ref[...] += gate_column() * back
        return run

    count = cnt_ref[i * N_EXPERTS + e]
    size_class = sum((count > s).astype(jnp.int32) for s in MOE_ROW_CLASSES[:-1])
    for k, size in enumerate(MOE_ROW_CLASSES):
        pl.when((size_class == k) & (count > 0))(routed(size))

    @pl.when((e == pl.num_programs(1) - 1) & (f == last_f))
    def _():
        o_ref[...] = _layer_norm_rows(DEEPNORM_ALPHA * h_ref[...] + acc_ref[...], g_ref[...], b_ref[...])


def moe_experts_norm(h, routing, wg, wu, wd, ln_g, ln_b):
    gates, gates_t, counts = routing
    m, d = h.shape
    n_experts, _, f_dim = wg.shape
    tm = MOE_TOKENS
    tf = next(t for t in MOE_FFN_TILES if f_dim % t == 0)
    assert m % tm == 0 and MOE_ROW_CLASSES[-1] == tm
    n_tiles = m // tm
    tri = jnp.triu(jnp.ones((tm, tm), MXU_DTYPE), k=1)
    row = lambda i, e, f, c: (i, 0)
    full = lambda i, e, f, c: (0, 0)
    grid_spec = pltpu.PrefetchScalarGridSpec(
        num_scalar_prefetch=1,
        grid=(n_tiles, n_experts, f_dim // tf),
        in_specs=[pl.BlockSpec((tm, d), row), pl.BlockSpec((tm, LANE), row),
                  pl.BlockSpec((n_experts, tm), lambda i, e, f, c: (0, i)),
                  pl.BlockSpec((tm, tm), full),
                  pl.BlockSpec((1, d, tf), lambda i, e, f, c: (e, 0, f)),
                  pl.BlockSpec((1, d, tf), lambda i, e, f, c: (e, 0, f)),
                  pl.BlockSpec((1, tf, d), lambda i, e, f, c: (e, f, 0)),
                  pl.BlockSpec((1, d), full), pl.BlockSpec((1, d), full)],
        out_specs=pl.BlockSpec((tm, d), row),
        scratch_shapes=[pltpu.VMEM((tm, d), MXU_DTYPE),
                        pltpu.VMEM((tm, tm), MXU_DTYPE),
                        pltpu.VMEM((tm, tm), MXU_DTYPE),
                        pltpu.VMEM((tm, d), MXU_DTYPE),
                        pltpu.VMEM((tm, d), jnp.float32),
                        pltpu.VMEM((tm, d), jnp.float32),
                        pltpu.VMEM((n_experts, tm), jnp.float32)])
    return pl.pallas_call(
        _moe_body,
        grid_spec=grid_spec,
        out_shape=jax.ShapeDtypeStruct((m, d), jnp.float32),
        compiler_params=pltpu.CompilerParams(
            dimension_semantics=("parallel", "arbitrary", "arbitrary"),
            vmem_limit_bytes=MOE_VMEM_LIMIT_BYTES),
        name="moe_experts_norm",
    )(counts, h, gates, gates_t, tri, wg, wu, wd, ln_g.reshape(1, d), ln_b.reshape(1, d))


def _router_body(h_ref, r_ref, gate_ref, gatet_ref, cnt_ref):
    f32 = jnp.float32
    logits = jnp.dot(h_ref[...], r_ref[...], preferred_element_type=f32)
    lane = lax.broadcasted_iota(jnp.int32, logits.shape, 1)
    logits = jnp.where(lane < N_EXPERTS, logits, -jnp.inf)
    top1 = jnp.max(logits, axis=1, keepdims=True)
    idx1 = jnp.min(jnp.where(logits == top1, lane, LANE), axis=1, keepdims=True)
    rest = jnp.where(lane == idx1, -jnp.inf, logits)
    top2 = jnp.max(rest, axis=1, keepdims=True)
    idx2 = jnp.min(jnp.where(rest == top2, lane, LANE), axis=1, keepdims=True)
    e2 = jnp.exp(top2 - top1)
    gates = jnp.where(lane == idx1, 1.0 / (1.0 + e2), jnp.where(lane == idx2, e2 / (1.0 + e2), 0.0))
    gate_ref[...] = gates
    gatet_ref[...] = gates.T[0:N_EXPERTS, :]
    cnt_ref[0] = jnp.sum(jnp.where(gates != 0.0, 1.0, 0.0), axis=0, keepdims=True)


def router_gates(h, router):
    m, d = h.shape
    tm = MOE_TOKENS
    assert TOP_K == 2 and m % tm == 0
    r_pad = jnp.pad(router.astype(jnp.float32), ((0, 0), (0, LANE - N_EXPERTS)))
    gates, gates_t, counts = pl.pallas_call(
        _router_body,
        grid=(m // tm,),
        in_specs=[pl.BlockSpec((tm, d), lambda i: (i, 0)), pl.BlockSpec((d, LANE), lambda i: (0, 0))],
        out_specs=[pl.BlockSpec((tm, LANE), lambda i: (i, 0)), pl.BlockSpec((N_EXPERTS, tm), lambda i: (0, i)),
                   pl.BlockSpec((1, 1, LANE), lambda i: (i, 0, 0))],
        out_shape=[jax.ShapeDtypeStruct((m, LANE), jnp.float32),
                   jax.ShapeDtypeStruct((N_EXPERTS, m), jnp.float32),
                   jax.ShapeDtypeStruct((m // tm, 1, LANE), jnp.float32)],
        compiler_params=pltpu.CompilerParams(
            dimension_semantics=("parallel",), vmem_limit_bytes=VMEM_LIMIT_BYTES),
        name="router_gates",
    )(h, r_pad)
    return gates, gates_t, counts[:, 0, :N_EXPERTS].astype(jnp.int32).reshape(-1)


def kernel(x, w_in, ssm_log_dt, ssm_lambda_re, ssm_lambda_im, ssm_b_re, ssm_b_im, ssm_c_re, ssm_c_im,
           ssm_d, ssm_w_glu, ssm_b_glu, ssm_w_out, hgrn_lb_logits, hgrn_norm_g, hgrn_w_out,
           attn_q_norm_g, attn_kv_norm_g, attn_w_uq, attn_w_qidx, attn_w_ukv, attn_w_out, w_o,
           ln_g, ln_b, ffn_w_gate, ffn_w_up, ffn_w_down, moe_router, moe_w_gate, moe_w_up, moe_w_down):
    bsz, seq, d = x.shape
    m = bsz * seq
    bf16 = MXU_DTYPE
    assert MIX_IN_USED + N_BRANCHES * D_MODEL == N_IN
    lb_soft = jax.nn.softmax(hgrn_lb_logits.astype(jnp.float32), axis=0)
    lower_bounds = jnp.concatenate([jnp.zeros_like(lb_soft[:1]), jnp.cumsum(lb_soft[1:], axis=0)], axis=0)
    h = x.reshape(m, d)
    for l in range(DEPTH):
        w_mix = jnp.pad(w_in[l][:, :MIX_IN_USED], ((0, 0), (0, MIX_IN_WIDTH - MIX_IN_USED))).astype(bf16)
        u, hg_in, att_in = in_proj(h, w_mix)
        y_ssm = s5_branch(u.reshape(bsz, seq, SSM_WIDTH), ssm_log_dt[l], ssm_lambda_re[l], ssm_lambda_im[l],
                          ssm_b_re[l], ssm_b_im[l], ssm_c_re[l], ssm_c_im[l], ssm_d[l], ssm_w_glu[l],
                          ssm_b_glu[l], ssm_w_out[l])
        y_hg = hgrn2_branch(hg_in.reshape(bsz, seq, HG_IN_WIDTH), lower_bounds[l], hgrn_norm_g[l],
                            hgrn_w_out[l])
        y_att = dsa_branch(att_in.reshape(bsz, seq, ATT_IN_WIDTH), attn_q_norm_g[l], attn_kv_norm_g[l],
                           attn_w_uq[l], attn_w_qidx[l], attn_w_ukv[l], attn_w_out[l])
        h = merge_project_norm(h, y_ssm.reshape(m, d), y_hg.reshape(m, d), y_att.reshape(m, d),
                               w_in[l][:, MIX_IN_USED:].astype(bf16), w_o[l].astype(bf16),
                               ln_g[l, 0], ln_b[l, 0])
        if l % 2 == 0:
            h = swiglu_norm(h, ffn_w_gate[l // 2].astype(bf16), ffn_w_up[l // 2].astype(bf16),
                            ffn_w_down[l // 2].astype(bf16), ln_g[l, 1], ln_b[l, 1])
        else:
            gate_w = router_gates(h, moe_router[l // 2])
            h = moe_experts_norm(h, gate_w, moe_w_gate[l // 2].astype(bf16),
                                 moe_w_up[l // 2].astype(bf16), moe_w_down[l // 2].astype(bf16),
                                 ln_g[l, 1], ln_b[l, 1])
    return h.reshape(bsz, seq, d)
```

```python
import functools

import jax
import jax.numpy as jnp
import numpy as np
from jax import lax
from jax.experimental import pallas as pl
from jax.experimental.pallas import tpu as pltpu

D_MODEL = 1024
DEPTH = 2
SSM_WIDTH = 256
SSM_GROUP = 16
SSM_GROUPS = SSM_WIDTH // SSM_GROUP
SSM_STATE = 64
HGRN_HEADS = 4
HGRN_DK = 64
HGRN_DV = 64
HGRN_WIDTH = HGRN_HEADS * HGRN_DV
ATT_HEADS = 8
ATT_HEAD_DIM = 64
ATT_WIDTH = ATT_HEADS * ATT_HEAD_DIM
ATT_Q_RANK = 256
ATT_KV_RANK = 128
IDX_HEADS = 4
IDX_DIM = 64
TOPK_MAX = 256
MASK_VALUE = -1e30
N_BRANCHES = 3
N_EXPERTS = 8
TOP_K = 2
DEEPNORM_ALPHA = (2 * DEPTH) ** 0.25
LN_EPS = 1e-5
RMS_EPS = 1e-6
F_MIN = 1e-12

IN_SPLITS = (SSM_WIDTH, HGRN_HEADS * HGRN_DK, HGRN_HEADS * HGRN_DK, HGRN_WIDTH, HGRN_WIDTH,
             ATT_Q_RANK, ATT_KV_RANK, IDX_DIM, IDX_HEADS, N_BRANCHES * D_MODEL)
N_IN = sum(IN_SPLITS)

VMEM_LIMIT_BYTES = 48 * 1024 * 1024
LANE = 128
MXU_DTYPE = jnp.bfloat16


def _round_up(n, m):
    return (n + m - 1) // m * m


HG_IN_WIDTH = 2 * HGRN_HEADS * HGRN_DK + 2 * HGRN_WIDTH
ATT_IN_USED = ATT_Q_RANK + ATT_KV_RANK + IDX_DIM + IDX_HEADS
ATT_IN_WIDTH = _round_up(ATT_IN_USED, LANE)
ATT_IDX_BLOCK = (ATT_Q_RANK + ATT_KV_RANK) // LANE
MIX_IN_USED = SSM_WIDTH + HG_IN_WIDTH + ATT_IN_USED
MIX_IN_WIDTH = SSM_WIDTH + HG_IN_WIDTH + ATT_IN_WIDTH


def _in_proj_body(h_ref, w_ref, u_ref, hg_ref, att_ref):
    r = jnp.dot(h_ref[...].astype(MXU_DTYPE), w_ref[...], preferred_element_type=jnp.float32)
    u_ref[...] = r[:, :SSM_WIDTH]
    hg_ref[...] = r[:, SSM_WIDTH:SSM_WIDTH + HG_IN_WIDTH]
    att_ref[...] = r[:, SSM_WIDTH + HG_IN_WIDTH:]


def in_proj(h, w_mix, *, tm=512):
    m, d = h.shape
    row = lambda i: (i, 0)
    widths = (SSM_WIDTH, HG_IN_WIDTH, ATT_IN_WIDTH)
    return pl.pallas_call(
        _in_proj_body,
        grid=(m // tm,),
        in_specs=[pl.BlockSpec((tm, d), row), pl.BlockSpec((d, MIX_IN_WIDTH), lambda i: (0, 0))],
        out_specs=[pl.BlockSpec((tm, w), row) for w in widths],
        out_shape=[jax.ShapeDtypeStruct((m, w), jnp.float32) for w in widths],
        compiler_params=pltpu.CompilerParams(
            dimension_semantics=("parallel",), vmem_limit_bytes=VMEM_LIMIT_BYTES),
        name="in_proj",
    )(h, w_mix)


def _layer_norm_rows(y, g, b):
    mu = jnp.mean(y, axis=-1, keepdims=True)
    yc = y - mu
    var = jnp.mean(yc * yc, axis=-1, keepdims=True)
    return yc * lax.rsqrt(var + LN_EPS) * g + b


def _merge_body(h_ref, ys_ref, yh_ref, ya_ref, wgate_ref, wo_ref, g_ref, b_ref, o_ref):
    d = D_MODEL
    f32 = jnp.float32
    h = h_ref[...]
    gates = jax.nn.sigmoid(jnp.dot(h.astype(MXU_DTYPE), wgate_ref[...], preferred_element_type=f32))
    mixed = (gates[:, 0:d] * ys_ref[...] + gates[:, d:2 * d] * yh_ref[...]
             + gates[:, 2 * d:3 * d] * ya_ref[...])
    mix_out = jnp.dot(mixed.astype(MXU_DTYPE), wo_ref[...], preferred_element_type=f32)
    o_ref[...] = _layer_norm_rows(DEEPNORM_ALPHA * h + mix_out, g_ref[...], b_ref[...])


def merge_project_norm(h, y_ssm, y_hg, y_att, w_gates, w_o, ln_g, ln_b, *, tm=512):
    m, d = h.shape
    row = lambda i: (i, 0)
    full = lambda i: (0, 0)
    return pl.pallas_call(
        _merge_body,
        grid=(m // tm,),
        in_specs=[pl.BlockSpec((tm, d), row), pl.BlockSpec((tm, d), row), pl.BlockSpec((tm, d), row),
                  pl.BlockSpec((tm, d), row), pl.BlockSpec((d, N_BRANCHES * d), full),
                  pl.BlockSpec((d, d), full), pl.BlockSpec((1, d), full), pl.BlockSpec((1, d), full)],
        out_specs=pl.BlockSpec((tm, d), row),
        out_shape=jax.ShapeDtypeStruct((m, d), jnp.float32),
        compiler_params=pltpu.CompilerParams(
            dimension_semantics=("parallel",), vmem_limit_bytes=VMEM_LIMIT_BYTES),
        name="merge_project_norm",
    )(h, y_ssm, y_hg, y_att, w_gates, w_o, ln_g.reshape(1, d), ln_b.reshape(1, d))


DENSE_FFN_TOKENS = 512
DENSE_FFN_TILES = (1408, 512, 256)


def _ffn_body(h_ref, wg_ref, wu_ref, wd_ref, g_ref, b_ref, o_ref, acc_ref):
    f = pl.program_id(1)

    @pl.when(f == 0)
    def _():
        acc_ref[...] = jnp.zeros_like(acc_ref)

    x = h_ref[...].astype(MXU_DTYPE)
    a = jnp.dot(x, wg_ref[...], preferred_element_type=jnp.float32)
    u = jnp.dot(x, wu_ref[...], preferred_element_type=jnp.float32)
    act = a * jax.nn.sigmoid(a) * u
    acc_ref[...] += jnp.dot(act.astype(MXU_DTYPE), wd_ref[...], preferred_element_type=jnp.float32)

    @pl.when(f == pl.num_programs(1) - 1)
    def _():
        o_ref[...] = _layer_norm_rows(DEEPNORM_ALPHA * h_ref[...] + acc_ref[...], g_ref[...], b_ref[...])


def swiglu_norm(h, wg, wu, wd, ln_g, ln_b):
    m, d = h.shape
    f_dim = wg.shape[1]
    tm = DENSE_FFN_TOKENS
    tf = next(t for t in DENSE_FFN_TILES if f_dim % t == 0)
    assert m % tm == 0
    row = lambda i, f: (i, 0)
    full = lambda i, f: (0, 0)
    return pl.pallas_call(
        _ffn_body,
        grid=(m // tm, f_dim // tf),
        in_specs=[pl.BlockSpec((tm, d), row),
                  pl.BlockSpec((d, tf), lambda i, f: (0, f)), pl.BlockSpec((d, tf), lambda i, f: (0, f)),
                  pl.BlockSpec((tf, d), lambda i, f: (f, 0)),
                  pl.BlockSpec((1, d), full), pl.BlockSpec((1, d), full)],
        out_specs=pl.BlockSpec((tm, d), row),
        out_shape=jax.ShapeDtypeStruct((m, d), jnp.float32),
        scratch_shapes=[pltpu.VMEM((tm, d), jnp.float32)],
        compiler_params=pltpu.CompilerParams(
            dimension_semantics=("parallel", "arbitrary"), vmem_limit_bytes=VMEM_LIMIT_BYTES),
        name="swiglu_norm",
    )(h, wg, wu, wd, ln_g.reshape(1, d), ln_b.reshape(1, d))


S5_CHUNK = 8
S5_CHUNK_WIDTH = S5_CHUNK * SSM_WIDTH
S5_STATE_WIDTH = SSM_GROUPS * SSM_STATE


def s5_operators(log_dt, lam_re, lam_im, b_re, b_im, c_re, c_im):
    f32 = jnp.float32
    n = S5_CHUNK
    lre, lim = lam_re.astype(f32), lam_im.astype(f32)
    dt = jnp.exp(log_dt.astype(f32))[:, None]
    mag = jnp.exp(lre * dt)
    ang = lim * dt
    a_re, a_im = mag * jnp.cos(ang), mag * jnp.sin(ang)
    den = lre * lre + lim * lim
    coef_re = ((a_re - 1.0) * lre + a_im * lim) / den
    coef_im = (a_im * lre - (a_re - 1.0) * lim) / den
    br, bi = b_re.astype(f32), b_im.astype(f32)
    bb_re = coef_re[..., None] * br - coef_im[..., None] * bi
    bb_im = coef_re[..., None] * bi + coef_im[..., None] * br
    j = jnp.arange(n + 1, dtype=f32)[:, None, None]
    pw_re = jnp.exp(j * lre * dt) * jnp.cos(j * ang)
    pw_im = jnp.exp(j * lre * dt) * jnp.sin(j * ang)
    eye = jnp.eye(SSM_GROUPS, dtype=f32)
    cr, ci = c_re.astype(f32), c_im.astype(f32)
    ab_re = pw_re[..., None] * bb_re - pw_im[..., None] * bb_im
    ab_im = pw_re[..., None] * bb_im + pw_im[..., None] * bb_re
    kern = (jnp.einsum('gcp,jgpd->jgcd', cr, ab_re[:n]) - jnp.einsum('gcp,jgpd->jgcd', ci, ab_im[:n]))
    lag = jnp.arange(n)[None, :] - jnp.arange(n)[:, None]
    toep = jnp.where((lag >= 0)[:, :, None, None, None], kern[jnp.maximum(lag, 0)], 0.0)
    t_mat = jnp.einsum('abgcd,gh->agdbhc', toep, eye).reshape(S5_CHUNK_WIDTH, S5_CHUNK_WIDTH)
    v = jnp.stack([ab_re[:n][::-1], ab_im[:n][::-1]])
    w_in = jnp.einsum('rlgpc,gh->lhcrgp', v, eye).reshape(S5_CHUNK_WIDTH, 2 * S5_STATE_WIDTH)
    ar, ai = pw_re[1:], pw_im[1:]
    wo_re = jnp.einsum('gcp,lgp->gplc', cr, ar) - jnp.einsum('gcp,lgp->gplc', ci, ai)
    wo_im = -jnp.einsum('gcp,lgp->gplc', cr, ai) - jnp.einsum('gcp,lgp->gplc', ci, ar)
    w_out = jnp.einsum('rgplc,gh->rhplgc', jnp.stack([wo_re, wo_im]), eye)
    w_out = w_out.reshape(2 * S5_STATE_WIDTH, S5_CHUNK_WIDTH)
    decay = jnp.stack([pw_re[n].reshape(1, S5_STATE_WIDTH), pw_im[n].reshape(1, S5_STATE_WIDTH)])
    return (jnp.concatenate([t_mat, w_in], axis=1).astype(MXU_DTYPE), w_out.astype(MXU_DTYPE), decay)


def _s5_scan_body(u_ref, tw_ref, wout_ref, decay_ref, y_ref, h_ref, s_ref, hs_ref, *, tm):
    f32 = jnp.float32
    sw = S5_STATE_WIDTH

    @pl.when(pl.program_id(1) == 0)
    def _():
        h_ref[...] = jnp.zeros_like(h_ref)

    r = jnp.dot(u_ref[0].astype(MXU_DTYPE), tw_ref[...], preferred_element_type=f32)
    s_ref[...] = r[:, S5_CHUNK_WIDTH:]
    d_re = decay_ref[0]
    d_im = decay_ref[1]

    def eight_chunks(k, carry):
        h_re, h_im = carry
        r0 = pl.multiple_of(k * 8, 8)
        inc = s_ref[pl.ds(r0, 8), :]
        rows_re, rows_im = [], []
        for t in range(8):
            rows_re.append(h_re)
            rows_im.append(h_im)
            h_re, h_im = (d_re * h_re - d_im * h_im + inc[t:t + 1, :sw],
                          d_re * h_im + d_im * h_re + inc[t:t + 1, sw:])
        hs_ref[pl.ds(r0, 8), :] = jnp.concatenate(
            [jnp.concatenate(rows_re, axis=0), jnp.concatenate(rows_im, axis=0)], axis=1)
        return h_re, h_im

    h_re, h_im = lax.fori_loop(0, tm // 8, eight_chunks, (h_ref[0:1, :], h_ref[1:2, :]))
    h_ref[0:1, :] = h_re
    h_ref[1:2, :] = h_im
    y_ref[0] = r[:, :S5_CHUNK_WIDTH] + jnp.dot(hs_ref[...].astype(MXU_DTYPE), wout_ref[...],
                                                preferred_element_type=f32)


def s5_scan(u, tw, w_out, decay, *, tm=256):
    bsz, seq, _ = u.shape
    n_chunks = seq // S5_CHUNK
    tm = min(tm, n_chunks)
    assert seq % S5_CHUNK == 0 and n_chunks % tm == 0 and tm % 8 == 0
    once = pl.Buffered(1)
    y = pl.pallas_call(
        functools.partial(_s5_scan_body, tm=tm),
        grid=(bsz, n_chunks // tm),
        in_specs=[pl.BlockSpec((1, tm, S5_CHUNK_WIDTH), lambda b, i: (b, i, 0)),
                  pl.BlockSpec(tw.shape, lambda b, i: (0, 0), pipeline_mode=once),
                  pl.BlockSpec(w_out.shape, lambda b, i: (0, 0), pipeline_mode=once),
                  pl.BlockSpec(decay.shape, lambda b, i: (0, 0, 0), pipeline_mode=once)],
        out_specs=pl.BlockSpec((1, tm, S5_CHUNK_WIDTH), lambda b, i: (b, i, 0)),
        out_shape=jax.ShapeDtypeStruct((bsz, n_chunks, S5_CHUNK_WIDTH), jnp.float32),
        scratch_shapes=[pltpu.VMEM((2, S5_STATE_WIDTH), jnp.float32),
                        pltpu.VMEM((tm, 2 * S5_STATE_WIDTH), jnp.float32),
                        pltpu.VMEM((tm, 2 * S5_STATE_WIDTH), jnp.float32)],
        compiler_params=pltpu.CompilerParams(
            dimension_semantics=("parallel", "arbitrary"), vmem_limit_bytes=VMEM_LIMIT_BYTES),
        name="s5_scan",
    )(u.reshape(bsz, n_chunks, S5_CHUNK_WIDTH), tw, w_out, decay)
    return y.reshape(bsz, seq, SSM_WIDTH)


def _s5_out_body(y_ref, u_ref, d_ref, wglu_ref, bglu_ref, wout_ref, o_ref):
    f32 = jnp.float32
    y = jax.nn.gelu(y_ref[...] + d_ref[...] * u_ref[...])
    gate = jnp.dot(y.astype(MXU_DTYPE), wglu_ref[...], preferred_element_type=f32) + bglu_ref[...]
    y = y * jax.nn.sigmoid(gate)
    o_ref[...] = jnp.dot(y.astype(MXU_DTYPE), wout_ref[...], preferred_element_type=f32)


def s5_output(y, u, d_skip, w_glu, b_glu, w_out, *, tm=1024):
    m, c = y.shape
    tm = min(tm, m)
    row = lambda i: (i, 0)
    full = lambda i: (0, 0)
    return pl.pallas_call(
        _s5_out_body,
        grid=(m // tm,),
        in_specs=[pl.BlockSpec((tm, c), row), pl.BlockSpec((tm, c), row), pl.BlockSpec((1, c), full),
                  pl.BlockSpec((c, c), full), pl.BlockSpec((1, c), full), pl.BlockSpec((c, D_MODEL), full)],
        out_specs=pl.BlockSpec((tm, D_MODEL), row),
        out_shape=jax.ShapeDtypeStruct((m, D_MODEL), jnp.float32),
        compiler_params=pltpu.CompilerParams(
            dimension_semantics=("parallel",), vmem_limit_bytes=VMEM_LIMIT_BYTES),
        name="s5_output",
    )(y, u, d_skip.reshape(1, c), w_glu.astype(MXU_DTYPE), b_glu.reshape(1, c), w_out.astype(MXU_DTYPE))


def s5_branch(u, log_dt, lam_re, lam_im, b_re, b_im, c_re, c_im, d_skip, w_glu, b_glu, w_out):
    bsz, seq, _ = u.shape
    tw, w_state_out, decay = s5_operators(log_dt, lam_re, lam_im, b_re, b_im, c_re, c_im)
    y = s5_scan(u, tw, w_state_out, decay)
    out = s5_output(y.reshape(bsz * seq, SSM_WIDTH), u.reshape(bsz * seq, SSM_WIDTH),
                    d_skip, w_glu, b_glu, w_out)
    return out.reshape(bsz, seq, D_MODEL)


HG_CHUNK = 128
HG_LEVELS = 7
HG_KDIM = HGRN_HEADS * HGRN_DK
HG_STEP_CHUNKS = 2


def _hgrn_segment_sums():
    c = HG_CHUNK
    t = np.arange(c)[:, None]
    u = np.arange(c)[None, :]
    blocks = []
    for lvl in range(1, HG_LEVELS + 1):
        m = (t >> lvl << lvl) + (1 << (lvl - 1)) - 1
        right = ((t >> (lvl - 1)) & 1) == 1
        blocks.append(np.where(right, (u > m) & (u <= t), (u > t) & (u <= m)))
    blocks.append(u <= t)
    blocks.append(u > t)
    return np.concatenate(blocks, axis=0).astype(np.float32)


def _hgrn_body(q_ref, z_ref, v_ref, g_ref, seg_ref, lb_ref, ng_ref, hmean_ref, wout_ref, o_ref, st_ref):
    f32 = jnp.float32
    c = HG_CHUNK

    @pl.when(pl.program_id(1) == 0)
    def _():
        st_ref[...] = jnp.zeros_like(st_ref)

    lb = lb_ref[...]
    seg = seg_ref[...]
    lane_head = lax.broadcasted_iota(jnp.int32, (c, HG_KDIM), 1) // HGRN_DK
    tok = lax.broadcasted_iota(jnp.int32, (c, HG_KDIM), 0)
    row_t = lax.broadcasted_iota(jnp.int32, (HGRN_HEADS * c, c), 0) % c
    col_s = lax.broadcasted_iota(jnp.int32, (HGRN_HEADS * c, c), 1)
    sr = lax.broadcasted_iota(jnp.int32, st_ref.shape, 0) // HGRN_DV
    sc = lax.broadcasted_iota(jnp.int32, st_ref.shape, 1) // HGRN_DK

    def per_head_rows(x):
        return jnp.concatenate([jnp.where(lane_head == h, x, 0.0) for h in range(HGRN_HEADS)],
                               axis=0).astype(MXU_DTYPE)

    def scores(ql, kl):
        return lax.dot_general(per_head_rows(ql), kl.astype(MXU_DTYPE), (((1,), (1,)), ((), ())),
                               preferred_element_type=f32)

    def chunk_local(rows):
        q = q_ref[0, rows, :]
        z = z_ref[0, rows, :]
        v = v_ref[0, rows, :]
        f = lb + (1.0 - lb) * jax.nn.sigmoid(z)
        logf = jnp.log(jnp.maximum(f, F_MIN))
        kin = (1.0 - lb) * jax.nn.sigmoid(-z)

        p1 = logf.astype(MXU_DTYPE)
        r1 = logf - p1.astype(f32)
        p2 = r1.astype(MXU_DTYPE)
        p3 = (r1 - p2.astype(f32)).astype(MXU_DTYPE)
        sums = (jnp.dot(seg, p1, preferred_element_type=f32) + jnp.dot(seg, p2, preferred_element_type=f32)
                + jnp.dot(seg, p3, preferred_element_type=f32))

        att = jnp.where(row_t == col_s, scores(q, kin), 0.0)
        for lvl in range(1, HG_LEVELS + 1):
            decay = jnp.exp(sums[(lvl - 1) * c:lvl * c])
            right = ((tok >> (lvl - 1)) & 1) == 1
            a = scores(jnp.where(right, q * decay, 0.0), jnp.where(right, 0.0, kin * decay))
            att = att + jnp.where((row_t >> lvl) == (col_s >> lvl), a, 0.0)

        b = sums[HG_LEVELS * c:(HG_LEVELS + 1) * c]
        tail = sums[(HG_LEVELS + 1) * c:(HG_LEVELS + 2) * c]
        v_m = v.astype(MXU_DTYPE)
        o_intra = jnp.zeros((c, HGRN_WIDTH), f32)
        for h in range(HGRN_HEADS):
            o_h = jnp.dot(att[h * c:(h + 1) * c].astype(MXU_DTYPE), v_m, preferred_element_type=f32)
            o_intra = o_intra + jnp.where(lane_head == h, o_h, 0.0)
        kv = jnp.dot(v.T.astype(MXU_DTYPE), (kin * jnp.exp(tail)).astype(MXU_DTYPE),
                     preferred_element_type=f32)
        return ((q * jnp.exp(b)).astype(MXU_DTYPE), o_intra, jnp.where(sr == sc, kv, 0.0),
                jnp.exp(b[c - 1:c, :]))

    chunks = [pl.ds(r * c, c) for r in range(HG_STEP_CHUNKS)]
    local = [chunk_local(rows) for rows in chunks]
    st = st_ref[...]
    for rows, (q_dec, o_intra, kv, last_decay) in zip(chunks, local):
        o = o_intra + lax.dot_general(q_dec, st.astype(MXU_DTYPE), (((1,), (1,)), ((), ())),
                                      preferred_element_type=f32)
        st = st * last_decay + kv

        o2 = o * o
        o2_hi = o2.astype(MXU_DTYPE)
        o2_lo = (o2 - o2_hi.astype(f32)).astype(MXU_DTYPE)
        ms = (jnp.dot(o2_hi, hmean_ref[...], preferred_element_type=f32)
              + jnp.dot(o2_lo, hmean_ref[...], preferred_element_type=f32))
        g = g_ref[0, rows, :]
        out = o * lax.rsqrt(ms + RMS_EPS) * ng_ref[...] * (g * jax.nn.sigmoid(g))
        o_ref[0, rows, :] = jnp.dot(out.astype(MXU_DTYPE), wout_ref[...], preferred_element_type=f32)
    st_ref[...] = st


def hgrn2_branch(hg_in, lower_bound, norm_g, w_out):
    bsz, seq, _ = hg_in.shape
    step = HG_STEP_CHUNKS * HG_CHUNK
    assert seq % step == 0 and HGRN_DK == HGRN_DV and HG_IN_WIDTH == 4 * HG_KDIM
    seg = jnp.asarray(_hgrn_segment_sums(), MXU_DTYPE)
    head_mean = jnp.asarray(np.kron(np.eye(HGRN_HEADS), np.full((HGRN_DV, HGRN_DV), 1.0 / HGRN_DV)), MXU_DTYPE)
    tok = lambda b, i: (b, i, 0)
    full = lambda b, i: (0, 0)
    part = lambda k: pl.BlockSpec((1, step, HG_KDIM), lambda b, i: (b, i, k))
    return pl.pallas_call(
        _hgrn_body,
        grid=(bsz, seq // step),
        in_specs=[part(0), part(1), part(2), part(3),
                  pl.BlockSpec(seg.shape, full), pl.BlockSpec((1, HG_KDIM), full),
                  pl.BlockSpec((1, HGRN_WIDTH), full), pl.BlockSpec(head_mean.shape, full),
                  pl.BlockSpec((HGRN_WIDTH, D_MODEL), full)],
        out_specs=pl.BlockSpec((1, step, D_MODEL), tok),
        out_shape=jax.ShapeDtypeStruct((bsz, seq, D_MODEL), jnp.float32),
        scratch_shapes=[pltpu.VMEM((HGRN_WIDTH, HG_KDIM), jnp.float32)],
        compiler_params=pltpu.CompilerParams(
            dimension_semantics=("parallel", "arbitrary"), vmem_limit_bytes=VMEM_LIMIT_BYTES),
        name="hgrn2",
    )(hg_in, hg_in, hg_in, hg_in, seg, lower_bound.reshape(1, HG_KDIM).astype(jnp.float32),
      jnp.tile(norm_g.astype(jnp.float32), HGRN_HEADS).reshape(1, HGRN_WIDTH), head_mean,
      w_out.astype(MXU_DTYPE))


ATT_QK_DIM = LANE
DSA_TILE = 512


def _rms_rows(x, g):
    return x * lax.rsqrt(jnp.mean(x * x, axis=-1, keepdims=True) + RMS_EPS) * g


def _dsa_project_body(x_ref, gq_ref, gkv_ref, wq_ref, wqi_ref, wkt_ref, wv_ref,
                      q_ref, qi_ref, kt_ref, v_ref, kidt_ref):
    f32 = jnp.float32
    tm = x_ref.shape[1]
    x = x_ref[0]
    cq = _rms_rows(x[:, :ATT_Q_RANK], gq_ref[...]).astype(MXU_DTYPE)
    ckv = _rms_rows(x[:, ATT_Q_RANK:ATT_Q_RANK + ATT_KV_RANK], gkv_ref[...])
    q_all = jnp.dot(cq, wq_ref[...], preferred_element_type=f32)
    qi_all = jnp.dot(cq, wqi_ref[...], preferred_element_type=f32)
    v_all = jnp.dot(ckv.astype(MXU_DTYPE), wv_ref[...], preferred_element_type=f32)
    kt_all = jnp.dot(wkt_ref[...], ckv.T.astype(MXU_DTYPE), preferred_element_type=f32)

    lane = lax.broadcasted_iota(jnp.int32, (tm, LANE), 1)
    t_loc = lax.broadcasted_iota(jnp.int32, (tm, LANE), 0)
    t_even = (t_loc // 2 * 2).astype(f32)
    t_odd = (t_loc % 2).astype(f32)
    sub = lax.broadcasted_iota(jnp.int32, (LANE, tm), 0)
    s_loc = lax.broadcasted_iota(jnp.int32, (LANE, tm), 1)
    k_rows = jnp.where(sub < ATT_HEAD_DIM + 2, 1.0,
                       jnp.where(sub == ATT_HEAD_DIM + 2, (s_loc // 2 * 2).astype(f32), (s_loc % 2).astype(f32)))
    for h in range(ATT_HEADS):
        slope = 2.0 ** (-8.0 * (h + 1) / ATT_HEADS)
        q_cols = jnp.where(lane == ATT_HEAD_DIM, -slope * t_even,
                           jnp.where(lane == ATT_HEAD_DIM + 1, -slope * t_odd, slope))
        q_h = q_all[:, h * LANE:(h + 1) * LANE]
        q_ref[0, h] = jnp.where(lane < ATT_HEAD_DIM, q_h,
                                jnp.where(lane < ATT_HEAD_DIM + 4, q_cols, 0.0)).astype(q_ref.dtype)
        k_h = kt_all[h * LANE:(h + 1) * LANE, :]
        kt_ref[0, h] = jnp.where(sub < ATT_HEAD_DIM, k_h,
                                 jnp.where(sub < ATT_HEAD_DIM + 4, k_rows, 0.0)).astype(kt_ref.dtype)
        v_h = v_all[:, h * LANE:(h + 1) * LANE]
        v_ref[0, h] = jnp.where(lane == ATT_HEAD_DIM, 1.0, v_h).astype(v_ref.dtype)
    for h in range(IDX_HEADS):
        qi_ref[0, h] = qi_all[:, h * LANE:(h + 1) * LANE].astype(qi_ref.dtype)
    kidt_ref[0] = x[:, ATT_IDX_BLOCK * LANE:(ATT_IDX_BLOCK + 1) * LANE].T.astype(kidt_ref.dtype)


def _head_padded(w, n_heads, dim):
    k = w.shape[0]
    return jnp.pad(w.reshape(k, n_heads, dim), ((0, 0), (0, 0), (0, LANE - dim))).reshape(k, n_heads * LANE)


def dsa_project(att_in, q_norm_g, kv_norm_g, w_uq, w_qidx, w_ukv):
    bsz, seq, _ = att_in.shape
    tm = DSA_TILE
    assert seq % tm == 0 and 8 % ATT_HEADS == 0 and tm <= 512
    w_q = _head_padded(w_uq * ATT_HEAD_DIM ** -0.5, ATT_HEADS, ATT_HEAD_DIM).astype(MXU_DTYPE)
    w_qi = _head_padded(w_qidx * IDX_DIM ** -0.5, IDX_HEADS, IDX_DIM).astype(MXU_DTYPE)
    w_kt = _head_padded(w_ukv[:, :ATT_WIDTH], ATT_HEADS, ATT_HEAD_DIM).T.astype(MXU_DTYPE)
    w_v = _head_padded(w_ukv[:, ATT_WIDTH:], ATT_HEADS, ATT_HEAD_DIM).astype(MXU_DTYPE)
    tok = lambda b, i: (b, 0, i, 0)
    full = lambda b, i: (0, 0)
    dt = MXU_DTYPE
    return pl.pallas_call(
        _dsa_project_body,
        grid=(bsz, seq // tm),
        in_specs=[pl.BlockSpec((1, tm, ATT_IN_WIDTH), lambda b, i: (b, i, 0)),
                  pl.BlockSpec((1, ATT_Q_RANK), full), pl.BlockSpec((1, ATT_KV_RANK), full),
                  pl.BlockSpec(w_q.shape, full), pl.BlockSpec(w_qi.shape, full),
                  pl.BlockSpec(w_kt.shape, full), pl.BlockSpec(w_v.shape, full)],
        out_specs=[pl.BlockSpec((1, ATT_HEADS, tm, LANE), tok),
                   pl.BlockSpec((1, IDX_HEADS, tm, LANE), tok),
                   pl.BlockSpec((1, ATT_HEADS, LANE, tm), lambda b, i: (b, 0, 0, i)),
                   pl.BlockSpec((1, ATT_HEADS, tm, LANE), tok),
                   pl.BlockSpec((1, LANE, tm), lambda b, i: (b, 0, i))],
        out_shape=[jax.ShapeDtypeStruct((bsz, ATT_HEADS, seq, LANE), dt),
                   jax.ShapeDtypeStruct((bsz, IDX_HEADS, seq, LANE), dt),
                   jax.ShapeDtypeStruct((bsz, ATT_HEADS, LANE, seq), dt),
                   jax.ShapeDtypeStruct((bsz, ATT_HEADS, seq, LANE), dt),
                   jax.ShapeDtypeStruct((bsz, LANE, seq), dt)],
        compiler_params=pltpu.CompilerParams(
            dimension_semantics=("parallel", "parallel"), vmem_limit_bytes=VMEM_LIMIT_BYTES),
        name="dsa_project",
    )(att_in, q_norm_g.reshape(1, ATT_Q_RANK), kv_norm_g.reshape(1, ATT_KV_RANK), w_q, w_qi, w_kt, w_v)


_INT_MIN = -2 ** 31
_MASK_KEY = int(np.float32(MASK_VALUE).view(np.int32)) ^ 0x7FFFFFFF
_SEARCH_ROWS = 128
_ATT_ROWS = 32
_SEARCH_LANES = 512
_LIST_DEPTH = 12
_LIST_ROWS = 32
_LIST_MIN_BLOCKS = 4


def _key_to_score(key):
    return lax.bitcast_convert_type(jnp.where(key < 0, key ^ 0x7FFFFFFF, key), jnp.float32)


def _dsa_body(qi_ref, kj_ref, qidx_ref, w_ref, kidt_ref, q_ref, kt_ref, v_ref, tri_ref, wout_ref,
              o_ref, keys_ref, thr_ref, need_ref, carry_ref, m_ref, acc_ref, bias_ref, s_ref, p_ref,
              alpha_ref, cand_ref, *, tq, tk, seq, n_sel):
    f32 = jnp.float32
    p_id = pl.program_id(1)
    i = qi_ref[p_id]
    j = kj_ref[p_id]
    row_local = lax.broadcasted_iota(jnp.int32, (tq, tk), 0)
    col_local = lax.broadcasted_iota(jnp.int32, (tq, tk), 1)

    @pl.when(j == 0)
    def _select():
        def score_block(jj, carry):
            off = pl.multiple_of(jj * tk, tk)
            kb = kidt_ref[0, :, pl.ds(off, tk)]
            sc = jnp.zeros((tq, tk), f32)
            for h in range(IDX_HEADS):
                s = jnp.dot(qidx_ref[0, h], kb, preferred_element_type=f32)
                w_h = w_ref[0, :, IDX_DIM + h:IDX_DIM + h + 1] * IDX_HEADS ** -0.5
                sc = sc + jnp.maximum(s, 0.0) * w_h
            causal = (col_local + jj * tk) <= (row_local + i * tq)
            sc = jnp.where(causal, sc, MASK_VALUE)
            keys_ref[:, pl.ds(off, tk)] = jnp.where(sc == 0.0, 0.0, sc)
            return carry

        lax.fori_loop(0, i + 1, score_block, 0)

        n_blocks = i + 1
        n_masked_tail = (seq - n_blocks * tk).astype(f32)

        def row_group(r, carry):
            r0 = pl.multiple_of(r * _SEARCH_ROWS, _SEARCH_ROWS)

            def count_ge(ref, n_iter, cand):
                cand_b = jnp.broadcast_to(_key_to_score(cand), (_SEARCH_ROWS, LANE))

                def chunk(c, acc):
                    base = pl.multiple_of(c * _SEARCH_LANES, _SEARCH_LANES)
                    for u in range(_SEARCH_LANES // LANE):
                        kk = ref[pl.ds(r0, _SEARCH_ROWS), pl.ds(base + u * LANE, LANE)]
                        acc = acc + jnp.where(kk >= cand_b, 1.0, 0.0)
                    return acc

                acc = lax.fori_loop(0, n_iter, chunk, jnp.zeros((_SEARCH_ROWS, LANE), f32))
                cnt = jnp.sum(acc, axis=1, keepdims=True)
                return cnt + jnp.where(cand <= _MASK_KEY, n_masked_tail, 0.0)

            def kth_largest_key(count):
                zero = jnp.zeros((_SEARCH_ROWS, 1), jnp.int32)
                v0 = jnp.where(count(zero) >= n_sel, zero, zero + _INT_MIN)

                def bit_step(b, v):
                    cand = v | jnp.left_shift(jnp.int32(1), 30 - b)
                    return jnp.where(count(cand) >= n_sel, cand, v)

                return lax.fori_loop(0, 31, bit_step, v0)

            count_all = functools.partial(count_ge, keys_ref, n_blocks * (tk // _SEARCH_LANES))

            def store(v, n_above):
                thr_ref[pl.ds(r0, _SEARCH_ROWS), :] = _key_to_score(v)
                need_ref[pl.ds(r0, _SEARCH_ROWS), :] = n_sel - n_above

            def search_all_keys():
                v = kth_largest_key(count_all)
                store(v, count_all(v + 1))

            @pl.when(n_blocks < _LIST_MIN_BLOCKS)
            def _():
                search_all_keys()

            @pl.when(n_blocks >= _LIST_MIN_BLOCKS)
            def _():
                def shortlist(sub, carry):
                    rr = pl.multiple_of(r0 + sub * _LIST_ROWS, _LIST_ROWS)

                    def insert_block(c, tops):
                        base = pl.multiple_of(c * tk, tk)
                        for u in range(tk // LANE):
                            x = keys_ref[pl.ds(rr, _LIST_ROWS), pl.ds(base + u * LANE, LANE)]
                            new = []
                            for t in range(_LIST_DEPTH):
                                new.append(jnp.maximum(tops[t], x))
                                x = jnp.minimum(tops[t], x)
                            tops = tuple(new)
                        return tops

                    tops = lax.fori_loop(0, n_blocks, insert_block,
                                         tuple(jnp.full((_LIST_ROWS, LANE), -jnp.inf, f32)
                                               for _ in range(_LIST_DEPTH)))
                    for t in range(_LIST_DEPTH):
                        cand_ref[pl.ds(rr, _LIST_ROWS), t * LANE:(t + 1) * LANE] = tops[t]
                    return carry

                lax.fori_loop(0, _SEARCH_ROWS // _LIST_ROWS, shortlist, 0)
                v = kth_largest_key(functools.partial(count_ge, cand_ref,
                                                      _LIST_DEPTH * LANE // _SEARCH_LANES))
                n_above = count_all(v + 1)
                store(v, n_above)
                pl.when(jnp.max(n_above) >= n_sel)(search_all_keys)

            return carry

        lax.fori_loop(0, tq // _SEARCH_ROWS, row_group, 0)
        carry_ref[...] = jnp.zeros_like(carry_ref)
        m_ref[...] = jnp.full_like(m_ref, MASK_VALUE)
        acc_ref[...] = jnp.zeros_like(acc_ref)

    keys_blk = keys_ref[:, pl.ds(pl.multiple_of(j * tk, tk), tk)]
    thr = thr_ref[...]
    eq = keys_blk == thr
    tie_rank = carry_ref[...] + jnp.dot(jnp.where(eq, 1.0, 0.0).astype(MXU_DTYPE), tri_ref[...],
                                         preferred_element_type=f32)
    carry_ref[...] = tie_rank[:, tk - 1:tk]
    tie_bias = jnp.where(eq, jnp.where(tie_rank <= need_ref[...], 0.0, MASK_VALUE), MASK_VALUE)
    bias = jnp.where(keys_blk > thr, 0.0, tie_bias)
    causal = (row_local + i * tq) >= (col_local + j * tk)
    bias_ref[...] = jnp.where(causal, bias, MASK_VALUE)

    def logits(h, slot):
        s_ref[slot] = jnp.dot(q_ref[0, h], kt_ref[0, h], preferred_element_type=f32)

    def softmax(h, slot):
        slope_tk = jnp.right_shift(tk, h + 1).astype(f32)
        for r in range(0, tq, _ATT_ROWS):
            rows = pl.ds(r, _ATT_ROWS)
            s = s_ref[slot, rows, :] + bias_ref[rows, :]
            m_old = m_ref[h, rows, :] - slope_tk
            m_new = jnp.maximum(m_old, jnp.max(s, axis=1, keepdims=True))
            p_ref[slot, rows, :] = jnp.exp(s - m_new).astype(MXU_DTYPE)
            alpha_ref[slot, rows, :] = jnp.exp(m_old - m_new)
            m_ref[h, rows, :] = m_new

    def weighted_values(h, slot):
        acc_ref[h] = alpha_ref[slot] * acc_ref[h] + jnp.dot(p_ref[slot], v_ref[0, h],
                                                             preferred_element_type=f32)

    for t in range(ATT_HEADS + 2):
        if t < ATT_HEADS:
            logits(t, t % 2)
        if 1 <= t <= ATT_HEADS:
            softmax(t - 1, (t - 1) % 2)
        if t >= 2:
            weighted_values(t - 2, t % 2)

    @pl.when(j == i)
    def _finish():
        heads = []
        for h in range(ATT_HEADS):
            a = acc_ref[h]
            heads.append((a[:, :ATT_HEAD_DIM] / a[:, ATT_HEAD_DIM:ATT_HEAD_DIM + 1]).astype(MXU_DTYPE))
        o_ref[0] = jnp.dot(jnp.concatenate(heads, axis=1), wout_ref[...], preferred_element_type=f32)


def dsa_attention(att_in, q_idx, kid_t, q, k_t, v_aug, w_out):
    bsz, _, seq, _ = q.shape
    tq = tk = DSA_TILE
    n_sel = min(TOPK_MAX, seq // 4)
    nq = seq // tq
    assert seq % tq == 0 and tk % _SEARCH_LANES == 0 and tq % _SEARCH_ROWS == 0
    pairs = [(a, b) for a in range(nq) for b in range(a + 1)]
    qi = jnp.asarray([a for a, _ in pairs], jnp.int32)
    kj = jnp.asarray([b for _, b in pairs], jnp.int32)
    tri = jnp.triu(jnp.ones((tk, tk), MXU_DTYPE))
    grid_spec = pltpu.PrefetchScalarGridSpec(
        num_scalar_prefetch=2,
        grid=(bsz, len(pairs)),
        in_specs=[
            pl.BlockSpec((1, IDX_HEADS, tq, LANE), lambda b, p, qi, kj: (b, 0, qi[p], 0)),
            pl.BlockSpec((1, tq, LANE), lambda b, p, qi, kj: (b, qi[p], ATT_IDX_BLOCK)),
            pl.BlockSpec((1, LANE, seq), lambda b, p, qi, kj: (b, 0, 0)),
            pl.BlockSpec((1, ATT_HEADS, tq, ATT_QK_DIM), lambda b, p, qi, kj: (b, 0, qi[p], 0)),
            pl.BlockSpec((1, ATT_HEADS, ATT_QK_DIM, tk), lambda b, p, qi, kj: (b, 0, 0, kj[p])),
            pl.BlockSpec((1, ATT_HEADS, tk, LANE), lambda b, p, qi, kj: (b, 0, kj[p], 0)),
            pl.BlockSpec((tk, tk), lambda b, p, qi, kj: (0, 0)),
            pl.BlockSpec((ATT_WIDTH, D_MODEL), lambda b, p, qi, kj: (0, 0)),
        ],
        out_specs=pl.BlockSpec((1, tq, D_MODEL), lambda b, p, qi, kj: (b, qi[p], 0)),
        scratch_shapes=[
            pltpu.VMEM((tq, seq), jnp.float32),
            pltpu.VMEM((tq, 1), jnp.float32),
            pltpu.VMEM((tq, 1), jnp.float32),
            pltpu.VMEM((tq, 1), jnp.float32),
            pltpu.VMEM((ATT_HEADS, tq, 1), jnp.float32),
            pltpu.VMEM((ATT_HEADS, tq, LANE), jnp.float32),
            pltpu.VMEM((tq, tk), jnp.float32),
            pltpu.VMEM((2, tq, tk), jnp.float32),
            pltpu.VMEM((2, tq, tk), MXU_DTYPE),
            pltpu.VMEM((2, tq, 1), jnp.float32),
            pltpu.VMEM((tq, _LIST_DEPTH * LANE), jnp.float32),
        ])
    return pl.pallas_call(
        functools.partial(_dsa_body, tq=tq, tk=tk, seq=seq, n_sel=n_sel),
        grid_spec=grid_spec,
        out_shape=jax.ShapeDtypeStruct((bsz, seq, D_MODEL), jnp.float32),
        compiler_params=pltpu.CompilerParams(
            dimension_semantics=("parallel", "arbitrary"), vmem_limit_bytes=VMEM_LIMIT_BYTES),
        name="dsa_attention",
    )(qi, kj, q_idx, att_in, kid_t, q, k_t, v_aug, tri, w_out)


def dsa_branch(att_in, q_norm_g, kv_norm_g, w_uq, w_qidx, w_ukv, w_out):
    q, q_idx, k_t, v_aug, kid_t = dsa_project(att_in, q_norm_g, kv_norm_g, w_uq, w_qidx, w_ukv)
    return dsa_attention(att_in, q_idx, kid_t, q, k_t, v_aug, w_out.astype(MXU_DTYPE))


MOE_TOKENS = 1024
MOE_ROW_CLASSES = (256, 288, 320, 512, 1024)
MOE_FFN_TILES = (896, 512, 256)
MOE_VMEM_LIMIT_BYTES = 56 * 1024 * 1024


def _moe_body(cnt_ref, h_ref, gate_ref, gatet_ref, tri_ref, wg_ref, wu_ref, wd_ref, g_ref, b_ref, o_ref,
              xb_ref, gather_ref, scatter_ref, xc_ref, yc_ref, acc_ref, rank_ref):
    f32 = jnp.float32
    tm = MOE_TOKENS
    i = pl.program_id(0)
    e = pl.program_id(1)
    f = pl.program_id(2)
    last_f = pl.num_programs(2) - 1

    @pl.when((e == 0) & (f == 0))
    def _():
        xb_ref[...] = h_ref[...].astype(MXU_DTYPE)
        acc_ref[...] = jnp.zeros_like(acc_ref)
        routed_rows = jnp.where(gatet_ref[...] != 0.0, 1.0, 0.0).astype(MXU_DTYPE)
        rank_ref[...] = jnp.dot(routed_rows, tri_ref[...], preferred_element_type=f32)

    def gate_column():
        lane = lax.broadcasted_iota(jnp.int32, gate_ref.shape, 1)
        return jnp.sum(jnp.where(lane == e, gate_ref[...], 0.0), axis=-1, keepdims=True)

    def routed(size):
        def run():
            @pl.when(f == 0)
            def _():
                active_row = jnp.where(gatet_ref[pl.ds(e, 1), :] != 0.0, 1.0, 0.0)
                rank_row = rank_ref[pl.ds(e, 1), :]
                slot = lax.broadcasted_iota(jnp.int32, (size, tm), 0).astype(f32)
                gather = jnp.where(slot == rank_row, active_row, 0.0)
                gather_ref[0:size, :] = gather.astype(MXU_DTYPE)
                rank_col = jnp.broadcast_to(rank_row, (LANE, tm)).T[:, 0:1]
                slot_l = lax.broadcasted_iota(jnp.int32, (tm, size), 1).astype(f32)
                scatter = jnp.where(slot_l == rank_col, jnp.where(gate_column() != 0.0, 1.0, 0.0), 0.0)
                scatter_ref[:, 0:size] = scatter.astype(MXU_DTYPE)
                xc_ref[0:size, :] = jnp.dot(gather.astype(MXU_DTYPE), xb_ref[...],
                                            preferred_element_type=f32).astype(MXU_DTYPE)
                yc_ref[0:size, :] = jnp.zeros((size, D_MODEL), f32)

            xc = xc_ref[0:size, :]
            a = jnp.dot(xc, wg_ref[0], preferred_element_type=f32)
            u = jnp.dot(xc, wu_ref[0], preferred_element_type=f32)
            act = a * jax.nn.sigmoid(a) * u
            yc_ref[0:size, :] += jnp.dot(act.astype(MXU_DTYPE), wd_ref[0], preferred_element_type=f32)

            @pl.when(f == last_f)
            def _():
                back = jnp.dot(scatter_ref[:, 0:size], yc_ref[0:size, :].astype(MXU_DTYPE),
                               preferred_element_type=f32)
                acc_ref[...] += gate_column() * back
        return run

    count = cnt_ref[i * N_EXPERTS + e]
    size_class = sum((count > s).astype(jnp.int32) for s in MOE_ROW_CLASSES[:-1])
    for k, size in enumerate(MOE_ROW_CLASSES):
        pl.when((size_class == k) & (count > 0))(routed(size))

    @pl.when((e == pl.num_programs(1) - 1) & (f == last_f))
    def _():
        o_ref[...] = _layer_norm_rows(DEEPNORM_ALPHA * h_ref[...] + acc_ref[...], g_ref[...], b_ref[...])


def moe_experts_norm(h, routing, wg, wu, wd, ln_g, ln_b):
    gates, gates_t, counts = routing
    m, d = h.shape
    n_experts, _, f_dim = wg.shape
    tm = MOE_TOKENS
    tf = next(t for t in MOE_FFN_TILES if f_dim % t == 0)
    assert m % tm == 0 and MOE_ROW_CLASSES[-1] == tm
    n_tiles = m // tm
    tri = jnp.triu(jnp.ones((tm, tm), MXU_DTYPE), k=1)
    row = lambda i, e, f, c: (i, 0)
    full = lambda i, e, f, c: (0, 0)
    grid_spec = pltpu.PrefetchScalarGridSpec(
        num_scalar_prefetch=1,
        grid=(n_tiles, n_experts, f_dim // tf),
        in_specs=[pl.BlockSpec((tm, d), row), pl.BlockSpec((tm, LANE), row),
                  pl.BlockSpec((n_experts, tm), lambda i, e, f, c: (0, i)),
                  pl.BlockSpec((tm, tm), full),
                  pl.BlockSpec((1, d, tf), lambda i, e, f, c: (e, 0, f)),
                  pl.BlockSpec((1, d, tf), lambda i, e, f, c: (e, 0, f)),
                  pl.BlockSpec((1, tf, d), lambda i, e, f, c: (e, f, 0)),
                  pl.BlockSpec((1, d), full), pl.BlockSpec((1, d), full)],
        out_specs=pl.BlockSpec((tm, d), row),
        scratch_shapes=[pltpu.VMEM((tm, d), MXU_DTYPE),
                        pltpu.VMEM((tm, tm), MXU_DTYPE),
                        pltpu.VMEM((tm, tm), MXU_DTYPE),
                        pltpu.VMEM((tm, d), MXU_DTYPE),
                        pltpu.VMEM((tm, d), jnp.float32),
                        pltpu.VMEM((tm, d), jnp.float32),
                        pltpu.VMEM((n_experts, tm), jnp.float32)])
    return pl.pallas_call(
        _moe_body,
        grid_spec=grid_spec,
        out_shape=jax.ShapeDtypeStruct((m, d), jnp.float32),
        compiler_params=pltpu.CompilerParams(
            dimension_semantics=("parallel", "arbitrary", "arbitrary"),
            vmem_limit_bytes=MOE_VMEM_LIMIT_BYTES),
        name="moe_experts_norm",
    )(counts, h, gates, gates_t, tri, wg, wu, wd, ln_g.reshape(1, d), ln_b.reshape(1, d))


def _router_body(h_ref, r_ref, gate_ref, gatet_ref, cnt_ref):
    f32 = jnp.float32
    logits = jnp.dot(h_ref[...], r_ref[...], preferred_element_type=f32)
    lane = lax.broadcasted_iota(jnp.int32, logits.shape, 1)
    logits = jnp.where(lane < N_EXPERTS, logits, -jnp.inf)
    top1 = jnp.max(logits, axis=1, keepdims=True)
    idx1 = jnp.min(jnp.where(logits == top1, lane, LANE), axis=1, keepdims=True)
    rest = jnp.where(lane == idx1, -jnp.inf, logits)
    top2 = jnp.max(rest, axis=1, keepdims=True)
    idx2 = jnp.min(jnp.where(rest == top2, lane, LANE), axis=1, keepdims=True)
    e2 = jnp.exp(top2 - top1)
    gates = jnp.where(lane == idx1, 1.0 / (1.0 + e2), jnp.where(lane == idx2, e2 / (1.0 + e2), 0.0))
    gate_ref[...] = gates
    gatet_ref[...] = gates.T[0:N_EXPERTS, :]
    cnt_ref[0] = jnp.sum(jnp.where(gates != 0.0, 1.0, 0.0), axis=0, keepdims=True)


def router_gates(h, router):
    m, d = h.shape
    tm = MOE_TOKENS
    assert TOP_K == 2 and m % tm == 0
    r_pad = jnp.pad(router.astype(jnp.float32), ((0, 0), (0, LANE - N_EXPERTS)))
    gates, gates_t, counts = pl.pallas_call(
        _router_body,
        grid=(m // tm,),
        in_specs=[pl.BlockSpec((tm, d), lambda i: (i, 0)), pl.BlockSpec((d, LANE), lambda i: (0, 0))],
        out_specs=[pl.BlockSpec((tm, LANE), lambda i: (i, 0)), pl.BlockSpec((N_EXPERTS, tm), lambda i: (0, i)),
                   pl.BlockSpec((1, 1, LANE), lambda i: (i, 0, 0))],
        out_shape=[jax.ShapeDtypeStruct((m, LANE), jnp.float32),
                   jax.ShapeDtypeStruct((N_EXPERTS, m), jnp.float32),
                   jax.ShapeDtypeStruct((m // tm, 1, LANE), jnp.float32)],
        compiler_params=pltpu.CompilerParams(
            dimension_semantics=("parallel",), vmem_limit_bytes=VMEM_LIMIT_BYTES),
        name="router_gates",
    )(h, r_pad)
    return gates, gates_t, counts[:, 0, :N_EXPERTS].astype(jnp.int32).reshape(-1)


def kernel(x, w_in, ssm_log_dt, ssm_lambda_re, ssm_lambda_im, ssm_b_re, ssm_b_im, ssm_c_re, ssm_c_im,
           ssm_d, ssm_w_glu, ssm_b_glu, ssm_w_out, hgrn_lb_logits, hgrn_norm_g, hgrn_w_out,
           attn_q_norm_g, attn_kv_norm_g, attn_w_uq, attn_w_qidx, attn_w_ukv, attn_w_out, w_o,
           ln_g, ln_b, ffn_w_gate, ffn_w_up, ffn_w_down, moe_router, moe_w_gate, moe_w_up, moe_w_down):
    bsz, seq, d = x.shape
    m = bsz * seq
    bf16 = MXU_DTYPE
    assert MIX_IN_USED + N_BRANCHES * D_MODEL == N_IN
    lb_soft = jax.nn.softmax(hgrn_lb_logits.astype(jnp.float32), axis=0)
    lower_bounds = jnp.concatenate([jnp.zeros_like(lb_soft[:1]), jnp.cumsum(lb_soft[1:], axis=0)], axis=0)
    h = x.reshape(m, d)
    for l in range(DEPTH):
        w_mix = jnp.pad(w_in[l][:, :MIX_IN_USED], ((0, 0), (0, MIX_IN_WIDTH - MIX_IN_USED))).astype(bf16)
        u, hg_in, att_in = in_proj(h, w_mix)
        y_ssm = s5_branch(u.reshape(bsz, seq, SSM_WIDTH), ssm_log_dt[l], ssm_lambda_re[l], ssm_lambda_im[l],
                          ssm_b_re[l], ssm_b_im[l], ssm_c_re[l], ssm_c_im[l], ssm_d[l], ssm_w_glu[l],
                          ssm_b_glu[l], ssm_w_out[l])
        y_hg = hgrn2_branch(hg_in.reshape(bsz, seq, HG_IN_WIDTH), lower_bounds[l], hgrn_norm_g[l],
                            hgrn_w_out[l])
        y_att = dsa_branch(att_in.reshape(bsz, seq, ATT_IN_WIDTH), attn_q_norm_g[l], attn_kv_norm_g[l],
                           attn_w_uq[l], attn_w_qidx[l], attn_w_ukv[l], attn_w_out[l])
        h = merge_project_norm(h, y_ssm.reshape(m, d), y_hg.reshape(m, d), y_att.reshape(m, d),
                               w_in[l][:, MIX_IN_USED:].astype(bf16), w_o[l].astype(bf16),
                               ln_g[l, 0], ln_b[l, 0])
        if l % 2 == 0:
            h = swiglu_norm(h, ffn_w_gate[l // 2].astype(bf16), ffn_w_up[l // 2].astype(bf16),
                            ffn_w_down[l // 2].astype(bf16), ln_g[l, 1], ln_b[l, 1])
        else:
            gate_w = router_gates(h, moe_router[l // 2])
            h = moe_experts_norm(h, gate_w, moe_w_gate[l // 2].astype(bf16),
                                 moe_w_up[l // 2].astype(bf16), moe_w_down[l // 2].astype(bf16),
                                 ln_g[l, 1], ln_b[l, 1])
    return h.reshape(bsz, seq, d)
```
